```python
import jax, jax.numpy as jnp
from jax import lax
import numpy as np


D_MODEL = 1024
BATCH = 2
SEQ = 16384
DEPTH = 2

HEAD_DIM = 64
NSA_HEADS = D_MODEL // 128
NSA_KV_HEADS = 2
NSA_GROUP = NSA_HEADS // NSA_KV_HEADS
CMP_LEN = 32
CMP_STRIDE = 16
CMP_HIDDEN = 4 * HEAD_DIM
SLC_BLOCK = 64
SLC_TOPN = 16
WINDOW = 512
Q_BLOCK = 128
NSA_BRANCHES = 3
POOL_WINDOWS = (2, 4, 8, 16)
POOL_WIDTH = D_MODEL // 4
POOL_GROUP_DIM = POOL_WIDTH // 4
CONV_WIDTH = D_MODEL // 4
CONV_K = 3
N_BRANCHES = 3
NSA_Q_DIM = NSA_HEADS * HEAD_DIM
NSA_KV_DIM = 6 * NSA_KV_HEADS * HEAD_DIM
NSA_GATE_DIM = NSA_HEADS * NSA_BRANCHES
MERGE_GATE_DIM = N_BRANCHES * D_MODEL
IN_SIZES = (NSA_Q_DIM, NSA_KV_DIM, NSA_GATE_DIM, POOL_WIDTH, CONV_WIDTH, CONV_WIDTH, CONV_WIDTH, MERGE_GATE_DIM)
IN_DIM = NSA_Q_DIM + NSA_KV_DIM + NSA_GATE_DIM + POOL_WIDTH + 3 * CONV_WIDTH + MERGE_GATE_DIM
N_EXPERT_GROUPS = 4
EXPERTS_PER_GROUP = 8
N_EXPERTS = N_EXPERT_GROUPS * EXPERTS_PER_GROUP
EXPERT_TOP_K = 2
D_EXPERT = D_MODEL // 4

RMS_EPS = 1e-6
NEG_INF = -1e30
FORCE_SCORE = 1e30

kernel_name = 'nsa_pool_conv_hier_moe_hybrid'


def rms_norm(x, g):
    xf = x.astype(jnp.float32)
    y = xf * lax.rsqrt(jnp.mean(xf * xf, axis=-1, keepdims=True) + RMS_EPS)
    return (y * g.astype(jnp.float32)).astype(x.dtype)


def alibi_slopes(n_heads):
    return jnp.exp2(-8.0 * jnp.arange(1, n_heads + 1, dtype=jnp.float32) / n_heads)


def masked_softmax(s, mask):
    s = jnp.where(mask, s, NEG_INF)
    return jnp.where(mask, jax.nn.softmax(s, axis=-1), 0.0)


def compress_tokens(k, pe, w1, w2):
    B, T, G, Dh = k.shape
    n_sub = CMP_LEN // CMP_STRIDE
    n_cmp = T // CMP_STRIDE - n_sub + 1
    chunks = k.reshape(B, T // CMP_STRIDE, CMP_STRIDE, G, Dh)
    blocks = jnp.concatenate([chunks[:, i:i + n_cmp] for i in range(n_sub)], axis=2)
    blocks = blocks + pe[None, None, :, None, :]
    flat = blocks.transpose(0, 1, 3, 2, 4).reshape(B, n_cmp, G, CMP_LEN * Dh)
    return jax.nn.gelu(flat @ w1) @ w2


def nsa_mixer(q, kv, gate_logits, cmp_pe, cmp_w1, cmp_w2):
    B, T, _ = q.shape
    G, R, Dh = NSA_KV_HEADS, NSA_GROUP, HEAD_DIM
    q = q.reshape(B, T, G, R, Dh)
    kv = kv.reshape(B, T, 6, G, Dh)
    k_cmp = compress_tokens(kv[:, :, 0], cmp_pe[0], cmp_w1[0], cmp_w2[0])
    v_cmp = compress_tokens(kv[:, :, 1], cmp_pe[1], cmp_w1[1], cmp_w2[1])
    n_cmp = k_cmp.shape[1]
    cmp_end = (jnp.arange(n_cmp) * CMP_STRIDE + CMP_LEN - 1).astype(jnp.float32)
    n_slc = T // SLC_BLOCK
    n_sel = min(SLC_TOPN, n_slc)
    ratio = SLC_BLOCK // CMP_STRIDE
    lead = CMP_LEN // CMP_STRIDE - 1
    to_blocks = lambda a: a.reshape(B, n_slc, SLC_BLOCK, G, Dh).transpose(0, 3, 1, 2, 4)
    k_slc, v_slc = to_blocks(kv[:, :, 2]), to_blocks(kv[:, :, 3])
    pad = ((0, 0), (WINDOW, 0), (0, 0), (0, 0))
    k_win, v_win = jnp.pad(kv[:, :, 4], pad), jnp.pad(kv[:, :, 5], pad)
    gates = jax.nn.sigmoid(gate_logits).reshape(B, T, G, R, NSA_BRANCHES)
    slopes = alibi_slopes(NSA_HEADS).reshape(G, R)
    scale = HEAD_DIM ** -0.5
    b_ix = jnp.arange(B)[:, None, None, None]
    g_ix = jnp.arange(G)[None, :, None, None]
    blk_ids = jnp.arange(n_slc)
    n_qb = T // Q_BLOCK

    def block(args):
        qb, qq, gg = args
        t = qb * Q_BLOCK + jnp.arange(Q_BLOCK)
        tf = t.astype(jnp.float32)
        d_c = tf[:, None] - cmp_end[None, :]
        s = jnp.einsum('bqgrd,bcgd->bgrqc', qq, k_cmp).astype(jnp.float32) * scale - slopes[:, :, None, None] * d_c
        p_c = masked_softmax(s, d_c >= 0)
        o_c = jnp.einsum('bgrqc,bcgd->bqgrd', p_c.astype(v_cmp.dtype), v_cmp)
        imp = jnp.pad(p_c.sum(axis=2), ((0, 0), (0, 0), (0, 0), (lead, ratio * n_slc - n_cmp)))
        imp = sum(imp[..., o:o + ratio * n_slc:ratio] for o in range(ratio + lead))
        cur = t // SLC_BLOCK
        valid = blk_ids[None, :] <= cur[:, None]
        forced = (blk_ids[None, :] == 0) | (blk_ids[None, :] == cur[:, None]) | (blk_ids[None, :] == cur[:, None] - 1)
        score = jnp.where(forced, FORCE_SCORE, jnp.where(valid, imp, NEG_INF))
        _, idx = lax.top_k(score, n_sel)
        kb = k_slc[b_ix, g_ix, idx].reshape(B, G, Q_BLOCK, n_sel * SLC_BLOCK, Dh)
        vb = v_slc[b_ix, g_ix, idx].reshape(B, G, Q_BLOCK, n_sel * SLC_BLOCK, Dh)
        pos = (idx[..., None] * SLC_BLOCK + jnp.arange(SLC_BLOCK)).reshape(B, G, Q_BLOCK, n_sel * SLC_BLOCK)
        d_s = (t[:, None] - pos).astype(jnp.float32)
        s = jnp.einsum('bqgrd,bgqkd->bgrqk', qq, kb).astype(jnp.float32) * scale - slopes[None, :, :, None, None] * d_s[:, :, None]
        p_s = masked_softmax(s, (d_s >= 0)[:, :, None])
        o_s = jnp.einsum('bgrqk,bgqkd->bqgrd', p_s.astype(vb.dtype), vb)
        start = qb * Q_BLOCK
        kw = lax.dynamic_slice_in_dim(k_win, start, WINDOW + Q_BLOCK, axis=1)
        vw = lax.dynamic_slice_in_dim(v_win, start, WINDOW + Q_BLOCK, axis=1)
        pos_w = start - WINDOW + jnp.arange(WINDOW + Q_BLOCK)
        d_w = t[:, None] - pos_w[None, :]
        m_w = (pos_w[None, :] >= 0) & (d_w >= 0) & (d_w < WINDOW)
        s = jnp.einsum('bqgrd,bkgd->bgrqk', qq, kw).astype(jnp.float32) * scale - slopes[:, :, None, None] * d_w.astype(jnp.float32)
        p_w = masked_softmax(s, m_w)
        o_w = jnp.einsum('bgrqk,bkgd->bqgrd', p_w.astype(vw.dtype), vw)
        o = gg[..., 0:1] * o_c + gg[..., 1:2] * o_s + gg[..., 2:3] * o_w
        return o.reshape(B, Q_BLOCK, NSA_HEADS * Dh)

    qs = jnp.moveaxis(q.reshape(B, n_qb, Q_BLOCK, G, R, Dh), 1, 0)
    gs = jnp.moveaxis(gates.reshape(B, n_qb, Q_BLOCK, G, R, NSA_BRANCHES), 1, 0)
    out = lax.map(block, (jnp.arange(n_qb), qs, gs))
    return jnp.moveaxis(out, 0, 1).reshape(B, T, NSA_HEADS * Dh)


def pool_mixer(u, w_grp, scale):
    B, T, _ = u.shape
    ug = u.reshape(B, T, len(POOL_WINDOWS), POOL_GROUP_DIM).astype(jnp.float32)
    cs = jnp.pad(jnp.cumsum(ug, axis=1), ((0, 0), (1, 0), (0, 0), (0, 0)))
    t = jnp.arange(T)
    outs = []
    for g, w in enumerate(POOL_WINDOWS):
        lo = jnp.maximum(t + 1 - w, 0)
        cg = cs[:, :, g]
        cnt = (t + 1 - lo).astype(jnp.float32)
        outs.append((cg[:, 1:] - cg[:, lo]) / cnt[None, :, None] - ug[:, :, g])
    pooled = jnp.stack(outs, axis=2).astype(u.dtype)
    mixed = jnp.einsum('btgc,gcd->btgd', pooled, w_grp)
    return mixed.reshape(B, T, POOL_WIDTH) * scale


def short_conv_mixer(h, b_gate, c_gate, conv_w):
    z = c_gate * h
    y = lax.conv_general_dilated(z, conv_w[:, None, :], window_strides=(1,), padding=[(CONV_K - 1, 0)],
                                 dimension_numbers=('NWC', 'WIO', 'NWC'), feature_group_count=z.shape[-1])
    return b_gate * y


def hier_moe(x, wg, bg, we, be, w1, w3, w2):
    B, T, D = x.shape
    n_tok = B * T
    xt = x.reshape(n_tok, D)
    tok = jnp.arange(n_tok)
    g_logits = (xt @ wg + bg).astype(jnp.float32)
    g_prob = jax.nn.softmax(g_logits, axis=-1)
    g_sel = jnp.argmax(g_logits, axis=-1)
    e_logits = (xt @ we + be).astype(jnp.float32).reshape(n_tok, N_EXPERT_GROUPS, EXPERTS_PER_GROUP)
    e_in = e_logits[tok, g_sel]
    top_v, top_i = lax.top_k(e_in, EXPERT_TOP_K)
    top_w = jax.nn.softmax(top_v, axis=-1) * g_prob[tok, g_sel][:, None]
    expert_id = g_sel[:, None] * EXPERTS_PER_GROUP + top_i
    combine = jnp.einsum('nk,nke->ne', top_w, jax.nn.one_hot(expert_id, N_EXPERTS, dtype=jnp.float32)).astype(x.dtype)
    out = jnp.zeros_like(xt)
    for e in range(N_EXPERTS):
        h = jax.nn.silu(xt @ w1[e]) * (xt @ w3[e])
        out = out + combine[:, e:e + 1] * (h @ w2[e])
    return out.reshape(B, T, D)


def setup_inputs(seed: int = 0) -> dict:
    key = jax.random.key(seed)
    ks = jax.random.split(key, 24)
    nrm = lambda k, shape, s: jax.random.normal(k, shape, jnp.float32) * s
    L, D = DEPTH, D_MODEL
    return {
        'x': nrm(ks[0], (BATCH, SEQ, D), 1.0),
        'norm1_g': 1.0 + nrm(ks[1], (L, D), 0.02),
        'w_in': nrm(ks[2], (L, D, IN_DIM), D ** -0.5),
        'cmp_pe': nrm(ks[3], (L, 2, CMP_LEN, HEAD_DIM), 0.1),
        'cmp_w1': nrm(ks[4], (L, 2, CMP_LEN * HEAD_DIM, CMP_HIDDEN), (CMP_LEN * HEAD_DIM) ** -0.5),
        'cmp_w2': nrm(ks[5], (L, 2, CMP_HIDDEN, HEAD_DIM), CMP_HIDDEN ** -0.5),
        'w_nsa_proj': nrm(ks[6], (L, NSA_Q_DIM, D), NSA_Q_DIM ** -0.5),
        'pool_w': nrm(ks[7], (L, len(POOL_WINDOWS), POOL_GROUP_DIM, POOL_GROUP_DIM), POOL_GROUP_DIM ** -0.5),
        'pool_scale': 1.0 + nrm(ks[8], (L, POOL_WIDTH), 0.02),
        'w_pool_proj': nrm(ks[9], (L, POOL_WIDTH, D), POOL_WIDTH ** -0.5),
        'conv_w': nrm(ks[10], (L, CONV_K, CONV_WIDTH), CONV_K ** -0.5),
        'w_conv_proj': nrm(ks[11], (L, CONV_WIDTH, D), CONV_WIDTH ** -0.5),
        'w_o': nrm(ks[12], (L, D, D), D ** -0.5),
        'norm2_g': 1.0 + nrm(ks[13], (L, D), 0.02),
        'router_group_w': nrm(ks[14], (L, D, N_EXPERT_GROUPS), D ** -0.5),
        'router_group_b': nrm(ks[15], (L, N_EXPERT_GROUPS), 0.01),
        'router_expert_w': nrm(ks[16], (L, D, N_EXPERTS), D ** -0.5),
        'router_expert_b': nrm(ks[17], (L, N_EXPERTS), 0.01),
        'expert_w1': nrm(ks[18], (L, N_EXPERTS, D, D_EXPERT), D ** -0.5),
        'expert_w3': nrm(ks[19], (L, N_EXPERTS, D, D_EXPERT), D ** -0.5),
        'expert_w2': nrm(ks[20], (L, N_EXPERTS, D_EXPERT, D), D_EXPERT ** -0.5),
        'final_norm_g': 1.0 + nrm(ks[21], (D,), 0.02),
    }


def reference(x, norm1_g, w_in, cmp_pe, cmp_w1, cmp_w2, w_nsa_proj, pool_w, pool_scale, w_pool_proj,
              conv_w, w_conv_proj, w_o, norm2_g, router_group_w, router_group_b, router_expert_w,
              router_expert_b, expert_w1, expert_w3, expert_w2, final_norm_g):
    B, T, D = x.shape
    split_at = np.cumsum(IN_SIZES)[:-1].tolist()
    h = x
    for l in range(DEPTH):
        xn = rms_norm(h, norm1_g[l])
        proj = xn @ w_in[l]
        q, kv, nsa_g, pool_u, conv_h, conv_b, conv_c, merge_g = jnp.split(proj, split_at, axis=-1)
        y_nsa = nsa_mixer(q, kv, nsa_g, cmp_pe[l], cmp_w1[l], cmp_w2[l]) @ w_nsa_proj[l]
        y_pool = pool_mixer(pool_u, pool_w[l], pool_scale[l]) @ w_pool_proj[l]
        y_conv = short_conv_mixer(conv_h, conv_b, conv_c, conv_w[l]) @ w_conv_proj[l]
        mg = jax.nn.sigmoid(merge_g).reshape(B, T, N_BRANCHES, D)
        merged = mg[:, :, 0] * y_nsa + mg[:, :, 1] * y_pool + mg[:, :, 2] * y_conv
        h = h + merged @ w_o[l]
        h = h + hier_moe(rms_norm(h, norm2_g[l]), router_group_w[l], router_group_b[l],
                         router_expert_w[l], router_expert_b[l], expert_w1[l], expert_w3[l], expert_w2[l])
    return rms_norm(h, final_norm_g)
```

```python
import functools

import jax
import jax.numpy as jnp
import numpy as np
from jax import lax
from jax.experimental import pallas as pl
from jax.experimental.pallas import tpu as pltpu

F32 = jnp.float32
BF16 = jnp.bfloat16

HEAD_DIM = 64
NSA_HEADS = 8
NSA_KV_HEADS = 2
NSA_GROUP = NSA_HEADS // NSA_KV_HEADS
CMP_LEN = 32
CMP_STRIDE = 16
CMP_HIDDEN = 4 * HEAD_DIM
SLC_BLOCK = 64
SLC_TOPN = 16
WINDOW = 512
NSA_BRANCHES = 3
POOL_WINDOWS = (2, 4, 8, 16)
CONV_K = 3
N_BRANCHES = 3
N_EXPERT_GROUPS = 4
EXPERTS_PER_GROUP = 8
N_EXPERTS = N_EXPERT_GROUPS * EXPERTS_PER_GROUP
RMS_EPS = 1e-6
NEG_INF = -1e30
FORCE_SCORE = 1e30
ALIBI_SLOPES = tuple(float(2.0 ** (-8.0 * (h + 1) / NSA_HEADS)) for h in range(NSA_HEADS))

LANES = 128
VMEM_LIMIT = 56 * 1024 * 1024
TOKEN_TILE = 512
MOE_TILE = 1024
Q_TILE = 128
KEY_CHUNK = 128
GATE_PAD = LANES
ROUTER_PAD = LANES
WORD_BITS = 16


def _params(*semantics):
    return pltpu.CompilerParams(dimension_semantics=semantics, vmem_limit_bytes=VMEM_LIMIT)


def _dot(a, b):
    return jnp.dot(a, b, preferred_element_type=F32)


def _dot_nt(a, b):
    return lax.dot_general(a, b, (((1,), (1,)), ((), ())), preferred_element_type=F32)


def _rms_norm(x, g):
    y = x * lax.rsqrt(jnp.mean(x * x, axis=-1, keepdims=True) + RMS_EPS)
    return y * g


def _iota(shape, dim):
    return lax.broadcasted_iota(jnp.int32, shape, dim)


def _inproj_kernel(x_ref, g_ref, w_ref, q_ref, kv_ref, gate_ref, pool_ref, conv_ref):
    xn = _rms_norm(x_ref[...], g_ref[...]).astype(BF16)
    col = 0
    for ref in (q_ref, kv_ref, gate_ref, pool_ref, conv_ref):
        width = ref.shape[1]
        ref[...] = _dot(xn, w_ref[:, col:col + width]).astype(ref.dtype)
        col += width


def _inproj(h, g, w):
    n, d = h.shape
    dq = NSA_HEADS * HEAD_DIM
    dkv = 6 * NSA_KV_HEADS * HEAD_DIM
    dpool = d // 4
    dconv = 3 * (d // 4)
    widths = (dq, dkv, GATE_PAD, dpool, dconv)
    dtypes = (BF16, BF16, F32, F32, F32)
    tm = TOKEN_TILE
    return pl.pallas_call(
        _inproj_kernel,
        grid=(n // tm,),
        in_specs=[
            pl.BlockSpec((tm, d), lambda i: (i, 0)),
            pl.BlockSpec((1, d), lambda i: (0, 0)),
            pl.BlockSpec((d, sum(widths)), lambda i: (0, 0)),
        ],
        out_specs=[pl.BlockSpec((tm, wd), lambda i: (i, 0)) for wd in widths],
        out_shape=[jax.ShapeDtypeStruct((n, wd), dt) for wd, dt in zip(widths, dtypes)],
        compiler_params=_params("parallel"),
        name="inproj",
    )(h, g, w)


def _gelu_tanh(x):
    return 0.5 * x * (1.0 + jnp.tanh(0.7978845608028654 * (x + 0.044715 * x * x * x)))


def _compress_kernel(c_ref, pe_ref, w1_ref, w2_ref, out_ref):
    n_chunks = c_ref.shape[3]
    half = w1_ref.shape[1] // 2
    w_top = w1_ref[0, :half, :]
    w_bot = w1_ref[0, half:, :]
    bias = _dot(pe_ref[0], w1_ref[0])[0:1, :]
    row = _iota((n_chunks, 1), 0)
    outs = []
    for g in range(NSA_KV_HEADS):
        c = c_ref[0, 0, g]
        first = _dot(c, w_top)
        second = _dot(c, w_bot)
        hid = first + pltpu.roll(second, n_chunks - 1, 0) + bias
        y = _dot(_gelu_tanh(hid).astype(BF16), w2_ref[0])
        outs.append(jnp.where(row < n_chunks - 1, y, 0.0))
    out_ref[0, 0] = jnp.concatenate(outs, axis=1).astype(BF16)


def _compress(chunks, pe, w1, w2):
    _, b, g, nc, cw = chunks.shape
    return pl.pallas_call(
        _compress_kernel,
        grid=(2, b),
        in_specs=[
            pl.BlockSpec((1, 1, g, nc, cw), lambda k, bi: (k, bi, 0, 0, 0)),
            pl.BlockSpec((1,) + pe.shape[1:], lambda k, bi: (k, 0, 0)),
            pl.BlockSpec((1,) + w1.shape[1:], lambda k, bi: (k, 0, 0)),
            pl.BlockSpec((1,) + w2.shape[1:], lambda k, bi: (k, 0, 0)),
        ],
        out_specs=pl.BlockSpec((1, 1, nc, g * HEAD_DIM), lambda k, bi: (k, bi, 0, 0)),
        out_shape=jax.ShapeDtypeStruct((2, b, nc, g * HEAD_DIM), BF16),
        compiler_params=_params("parallel", "parallel"),
        name="compress",
    )(chunks, pe, w1, w2)


def _padded_queries(q, g):
    rows = jnp.concatenate([q[:, r * HEAD_DIM:(r + 1) * HEAD_DIM] for r in range(NSA_GROUP)], axis=0)
    both = jnp.concatenate([rows] * NSA_KV_HEADS, axis=1)
    lane_group = _iota(both.shape, 1) // HEAD_DIM
    return jnp.where(lane_group == g, both, jnp.zeros_like(both))


def _slope(g, r):
    s = jnp.float32(ALIBI_SLOPES[r])
    for gi in range(1, NSA_KV_HEADS):
        s = jnp.where(g == gi, jnp.float32(ALIBI_SLOPES[gi * NSA_GROUP + r]), s)
    return s


def _group_half(x, g):
    out = x[:, :HEAD_DIM]
    for gi in range(1, NSA_KV_HEADS):
        out = jnp.where(g == gi, x[:, gi * HEAD_DIM:(gi + 1) * HEAD_DIM], out)
    return out


def _nsa_cmp_kernel(q_ref, kc_ref, vc_ref, pool_ref, pair_ref, bits_ref, oc_ref, sel_ref, words_ref, *, n_sel):
    g = pl.program_id(1)
    i = pl.program_id(2)
    qt = q_ref.shape[1]
    ncp = kc_ref.shape[1]
    n_slc = pool_ref.shape[1]
    q_pad = _padded_queries(q_ref[0], g)
    s = _dot_nt(q_pad, kc_ref[0])
    t = i * qt + _iota((qt, 1), 0)
    cmp_end = _iota((1, ncp), 1) * CMP_STRIDE + (CMP_LEN - 1)
    d = (t - cmp_end).astype(F32)
    valid = d >= 0.0
    vc = vc_ref[0]
    psum = jnp.zeros((qt, ncp), F32)
    for r in range(NSA_GROUP):
        sr = jnp.where(valid, s[r * qt:(r + 1) * qt] - _slope(g, r) * d, NEG_INF)
        m = jnp.max(sr, axis=1, keepdims=True)
        e = jnp.where(valid, jnp.exp(sr - m), 0.0)
        l = jnp.sum(e, axis=1, keepdims=True)
        p = e * jnp.where(l > 0.0, 1.0 / l, 0.0)
        psum = psum + p
        oc_ref[0, 0, 0, r * qt:(r + 1) * qt, :] = _dot(p.astype(BF16), vc).astype(BF16)
    p_hi = psum.astype(BF16)
    p_lo = (psum - p_hi.astype(F32)).astype(BF16)
    imp = _dot(p_hi, pool_ref[...]) + _dot(p_lo, pool_ref[...])
    blk = _iota((1, n_slc), 1)
    cur = t // SLC_BLOCK
    forced = (blk == 0) | (blk == cur) | (blk == cur - 1)
    score = jnp.where(forced, FORCE_SCORE, jnp.where(blk <= cur, imp, NEG_INF))
    blk_f = blk.astype(F32)
    sel = jnp.zeros((qt, n_slc), F32)
    for _ in range(n_sel):
        m = jnp.max(score, axis=1, keepdims=True)
        first = jnp.min(jnp.where(score == m, blk_f, F32(1e9)), axis=1, keepdims=True)
        hit = blk_f == first
        sel = jnp.where(hit, 1.0, sel)
        score = jnp.where(hit, -3e38, score)
    sel_bf = sel.astype(BF16)
    sel_ref[0, 0] = sel_bf
    count = _dot(jnp.ones((8, qt), BF16), sel_bf)
    used = jnp.where(count > 0.0, 1.0, 0.0).astype(BF16)
    chunk_used = jnp.where(_dot(used, pair_ref[...]) > 0.0, 1.0, 0.0).astype(BF16)
    words_ref[0, 0, 0] = _dot(chunk_used, bits_ref[...]).astype(jnp.int32)


def _nsa_cmp(q, kc, vc, pool_m, pair_m, bits_m, n_sel):
    b, t, _ = q.shape
    ncp = kc.shape[1]
    n_slc = pool_m.shape[1]
    n_tiles = t // Q_TILE
    gw = NSA_GROUP * HEAD_DIM
    kvw = NSA_KV_HEADS * HEAD_DIM
    const = lambda shape: pl.BlockSpec(shape, lambda bi, g, i: (0,) * len(shape))
    return pl.pallas_call(
        functools.partial(_nsa_cmp_kernel, n_sel=n_sel),
        grid=(b, NSA_KV_HEADS, n_tiles),
        in_specs=[
            pl.BlockSpec((1, Q_TILE, gw), lambda bi, g, i: (bi, i, g)),
            pl.BlockSpec((1, ncp, kvw), lambda bi, g, i: (bi, 0, 0)),
            pl.BlockSpec((1, ncp, kvw), lambda bi, g, i: (bi, 0, 0)),
            const(pool_m.shape), const(pair_m.shape), const(bits_m.shape),
        ],
        out_specs=[
            pl.BlockSpec((1, 1, 1, NSA_GROUP * Q_TILE, kvw), lambda bi, g, i: (bi, g, i, 0, 0)),
            pl.BlockSpec((1, 1, Q_TILE, n_slc), lambda bi, g, i: (bi, g, i, 0)),
            pl.BlockSpec((1, 1, 1, 8, LANES), lambda bi, g, i: (bi, g, i, 0, 0)),
        ],
        out_shape=[
            jax.ShapeDtypeStruct((b, NSA_KV_HEADS, n_tiles, NSA_GROUP * Q_TILE, kvw), BF16),
            jax.ShapeDtypeStruct((b, NSA_KV_HEADS, t, n_slc), BF16),
            jax.ShapeDtypeStruct((b, NSA_KV_HEADS, n_tiles, 8, LANES), jnp.int32),
        ],
        compiler_params=_params("parallel", "parallel", "parallel"),
        name="nsa_compressed",
    )(q, kc, vc, pool_m, pair_m, bits_m)


def _nsa_slc_kernel(words_ref, q_ref, ks_ref, vs_ref, kw_ref, vw_ref, sel_ref, oc_ref, gate_ref,
                    out_ref, m_ref, l_ref, acc_ref, ow_ref, *, words_per_tile):
    bi = pl.program_id(0)
    g = pl.program_id(1)
    i = pl.program_id(2)
    n_tiles = pl.num_programs(2)
    qt = q_ref.shape[1]
    n_slc = sel_ref.shape[3]
    q_pad = _padded_queries(q_ref[0], g)
    start = i * qt
    t = start + _iota((qt, 1), 0)
    key_lane = _iota((1, KEY_CHUNK), 1)

    def reset():
        m_ref[...] = jnp.full(m_ref.shape, NEG_INF, F32)
        l_ref[...] = jnp.zeros(l_ref.shape, F32)
        acc_ref[...] = jnp.zeros(acc_ref.shape, F32)

    def attend(k, v, d, mask):
        s = _dot_nt(q_pad, k)
        for r in range(NSA_GROUP):
            rows = slice(r * qt, (r + 1) * qt)
            sr = jnp.where(mask, s[rows] - _slope(g, r) * d, NEG_INF)
            m_old = m_ref[rows, :]
            m_new = jnp.maximum(m_old, jnp.max(sr, axis=1, keepdims=True))
            alpha = jnp.exp(m_old - m_new)
            p = jnp.where(mask, jnp.exp(sr - m_new), 0.0)
            l_ref[rows, :] = alpha * l_ref[rows, :] + jnp.sum(p, axis=1, keepdims=True)
            acc_ref[rows, :] = alpha * acc_ref[rows, :] + _dot(p.astype(BF16), v)
            m_ref[rows, :] = m_new

    def finalize():
        l = l_ref[...]
        return acc_ref[...] * jnp.where(l > 0.0, 1.0 / l, 0.0)

    reset()
    for c in range(WINDOW // KEY_CHUNK + qt // KEY_CHUNK):
        cs = start - WINDOW + c * KEY_CHUNK

        @pl.when(cs >= 0)
        def _():
            at = pl.multiple_of(jnp.maximum(cs, 0), KEY_CHUNK)
            d = t - (cs + key_lane)
            mask = (d >= 0) & (d < WINDOW)
            attend(kw_ref[0, pl.ds(at, KEY_CHUNK), :], vw_ref[0, pl.ds(at, KEY_CHUNK), :], d.astype(F32), mask)

    ow_ref[...] = finalize()

    reset()
    sel = sel_ref[0, 0]
    word_base = ((bi * NSA_KV_HEADS + g) * n_tiles + i) * words_per_tile
    blocks_per_chunk = KEY_CHUNK // SLC_BLOCK

    def chunk_step(c, carry):
        word = words_ref[word_base + c // WORD_BITS]
        used = (word >> (c % WORD_BITS)) & 1

        @pl.when(used == 1)
        def _():
            at = pl.multiple_of(c * KEY_CHUNK, KEY_CHUNK)
            expand = (_iota((n_slc, KEY_CHUNK), 0)
                      == c * blocks_per_chunk + _iota((n_slc, KEY_CHUNK), 1) // SLC_BLOCK)
            chosen = _dot(sel, jnp.where(expand, 1.0, 0.0).astype(BF16))
            d = t - (c * KEY_CHUNK + key_lane)
            mask = (chosen > 0.5) & (d >= 0)
            attend(ks_ref[0, pl.ds(at, KEY_CHUNK), :], vs_ref[0, pl.ds(at, KEY_CHUNK), :], d.astype(F32), mask)

        return carry

    lax.fori_loop(0, (start + qt) // KEY_CHUNK, chunk_step, 0)
    o_s = finalize()
    o_w = ow_ref[...]

    gates = jax.nn.sigmoid(gate_ref[0])
    gate_lane = _iota(gates.shape, 1)
    parts = []
    for r in range(NSA_GROUP):
        rows = slice(r * qt, (r + 1) * qt)
        col = (g * NSA_GROUP + r) * NSA_BRANCHES
        pick = lambda br: jnp.sum(jnp.where(gate_lane == col + br, gates, 0.0), axis=1, keepdims=True)
        o = pick(0) * oc_ref[0, 0, 0, rows, :].astype(F32) + pick(1) * o_s[rows] + pick(2) * o_w[rows]
        parts.append(_group_half(o, g))
    out_ref[0] = jnp.concatenate(parts, axis=1).astype(out_ref.dtype)


def _nsa_slc(words, q, kv, sel, oc, gates, words_per_tile):
    b, t, _ = q.shape
    n_tiles = t // Q_TILE
    n_slc = sel.shape[3]
    gw = NSA_GROUP * HEAD_DIM
    kvw = NSA_KV_HEADS * HEAD_DIM
    rows = NSA_GROUP * Q_TILE
    kv_spec = lambda kind: pl.BlockSpec((1, t, kvw), lambda bi, g, i, w: (bi, 0, kind),
                                        pipeline_mode=pl.Buffered(1))
    grid_spec = pltpu.PrefetchScalarGridSpec(
        num_scalar_prefetch=1,
        grid=(b, NSA_KV_HEADS, n_tiles),
        in_specs=[
            pl.BlockSpec((1, Q_TILE, gw), lambda bi, g, i, w: (bi, i, g)),
            kv_spec(2), kv_spec(3), kv_spec(4), kv_spec(5),
            pl.BlockSpec((1, 1, Q_TILE, n_slc), lambda bi, g, i, w: (bi, g, i, 0)),
            pl.BlockSpec((1, 1, 1, rows, kvw), lambda bi, g, i, w: (bi, g, i, 0, 0)),
            pl.BlockSpec((1, Q_TILE, GATE_PAD), lambda bi, g, i, w: (bi, i, 0)),
        ],
        out_specs=pl.BlockSpec((1, Q_TILE, gw), lambda bi, g, i, w: (bi, i, g)),
        scratch_shapes=[pltpu.VMEM((rows, kvw), F32)] * 4,
    )
    return pl.pallas_call(
        functools.partial(_nsa_slc_kernel, words_per_tile=words_per_tile),
        grid_spec=grid_spec,
        out_shape=jax.ShapeDtypeStruct((b, t, NSA_HEADS * HEAD_DIM), BF16),
        compiler_params=_params("parallel", "parallel", "parallel"),
        name="nsa_selected_window",
    )(words, q, kv, kv, kv, kv, sel, oc, gates)


POOL_HALO = 16
CONV_HALO = 8


def _merge_kernel(h_ref, g_ref, nsa_ref, pool_ref, pool_halo_ref, conv_ref, conv_halo_ref,
                  wmg_ref, wnsa_ref, pool_bd_ref, pool_scale_ref, wpool_ref, convw_ref, wconv_ref, wo_ref,
                  out_ref, pool_ext, conv_ext, *, seq_len):
    i = pl.program_id(0)
    tm, d = h_ref.shape
    cw = pool_ref.shape[1]
    pos0 = (i * tm) % seq_len
    keep_halo = jnp.where(pos0 == 0, 0.0, 1.0)
    pos = pos0 + _iota((tm, 1), 0)

    u = pool_ref[...]
    pool_ext[0:POOL_HALO, :] = pool_halo_ref[...] * keep_halo
    pool_ext[POOL_HALO:, :] = u
    lane_group = _iota((1, cw), 1) // (cw // len(POOL_WINDOWS))
    total = u
    mean = jnp.zeros_like(u)
    done = 1
    for gi, win in enumerate(POOL_WINDOWS):
        for k in range(done, win):
            total = total + pool_ext[POOL_HALO - k:POOL_HALO - k + tm, :]
        done = win
        cnt = jnp.minimum(pos + 1, win).astype(F32)
        mean = jnp.where(lane_group == gi, total / cnt, mean)
    pooled = (mean - u).astype(BF16)
    mixed = _dot(pooled, pool_bd_ref[...]) * pool_scale_ref[...]
    y_pool = _dot(mixed.astype(BF16), wpool_ref[...])

    ch = conv_ref[:, 0:cw]
    cb = conv_ref[:, cw:2 * cw]
    cc = conv_ref[:, 2 * cw:3 * cw]
    conv_ext[0:CONV_HALO, :] = conv_halo_ref[:, 0:cw] * conv_halo_ref[:, 2 * cw:3 * cw] * keep_halo
    conv_ext[CONV_HALO:, :] = cc * ch
    y = jnp.zeros((tm, cw), F32)
    for k in range(CONV_K):
        off = CONV_HALO - (CONV_K - 1) + k
        y = y + convw_ref[k:k + 1, :] * conv_ext[off:off + tm, :]
    y_conv = _dot((cb * y).astype(BF16), wconv_ref[...])

    y_nsa = _dot(nsa_ref[...], wnsa_ref[...])

    h = h_ref[...]
    xn = _rms_norm(h, g_ref[...]).astype(BF16)
    merged = jnp.zeros((tm, d), F32)
    for br, y_br in enumerate((y_nsa, y_pool, y_conv)):
        mg = jax.nn.sigmoid(_dot(xn, wmg_ref[:, br * d:(br + 1) * d]))
        merged = merged + mg * y_br
    out_ref[...] = h + _dot(merged.astype(BF16), wo_ref[...])


def _merge(h, g, nsa, pool_u, conv, wmg, wnsa, pool_bd, pool_scale, wpool, convw, wconv, wo, seq_len):
    n, d = h.shape
    tm = TOKEN_TILE
    cw = pool_u.shape[1]
    row = lambda width: pl.BlockSpec((tm, width), lambda i: (i, 0))
    full = lambda a: pl.BlockSpec(a.shape, lambda i: (0,) * a.ndim)
    halo = lambda rows, width: pl.BlockSpec(
        (rows, width), lambda i: (jnp.maximum(i * (tm // rows) - 1, 0), 0))
    return pl.pallas_call(
        functools.partial(_merge_kernel, seq_len=seq_len),
        grid=(n // tm,),
        in_specs=[row(d), full(g), row(nsa.shape[1]), row(cw), halo(POOL_HALO, cw),
                  row(conv.shape[1]), halo(CONV_HALO, conv.shape[1]),
                  full(wmg), full(wnsa), full(pool_bd), full(pool_scale), full(wpool), full(convw),
                  full(wconv), full(wo)],
        out_specs=row(d),
        out_shape=jax.ShapeDtypeStruct((n, d), F32),
        scratch_shapes=[pltpu.VMEM((tm + POOL_HALO, cw), F32), pltpu.VMEM((tm + CONV_HALO, cw), F32)],
        compiler_params=_params("parallel"),
        name="merge",
    )(h, g, nsa, pool_u, pool_u, conv, conv, wmg, wnsa, pool_bd, pool_scale, wpool, convw, wconv, wo)


def _route(logits):
    lane = _iota(logits.shape, 1)
    lane_f = lane.astype(F32)
    big = F32(1e9)
    is_group = lane < N_EXPERT_GROUPS
    gl = jnp.where(is_group, logits, NEG_INF)
    g_max = jnp.max(gl, axis=1, keepdims=True)
    g_sel = jnp.min(jnp.where(gl == g_max, lane_f, big), axis=1, keepdims=True)
    g_prob = 1.0 / jnp.sum(jnp.where(is_group, jnp.exp(gl - g_max), 0.0), axis=1, keepdims=True)
    lo = N_EXPERT_GROUPS + EXPERTS_PER_GROUP * g_sel
    in_group = (lane_f >= lo) & (lane_f < lo + EXPERTS_PER_GROUP)
    el = jnp.where(in_group, logits, NEG_INF)
    v1 = jnp.max(el, axis=1, keepdims=True)
    i1 = jnp.min(jnp.where((el == v1) & in_group, lane_f, big), axis=1, keepdims=True)
    el2 = jnp.where(lane_f == i1, NEG_INF, el)
    rest = in_group & (lane_f != i1)
    v2 = jnp.max(el2, axis=1, keepdims=True)
    i2 = jnp.min(jnp.where((el2 == v2) & rest, lane_f, big), axis=1, keepdims=True)
    e2 = jnp.exp(v2 - v1)
    w1 = g_prob / (1.0 + e2)
    w2 = g_prob * e2 / (1.0 + e2)
    return jnp.where(lane_f == i1, w1, 0.0) + jnp.where(lane_f == i2, w2, 0.0)


def _moe_kernel(h_ref, g_ref, wr_ref, br_ref, w1_ref, w3_ref, w2_ref, gf_ref, out_ref, xn_ref, comb_ref,
                *, final_norm):
    e = pl.program_id(1)

    @pl.when(e == 0)
    def _():
        h = h_ref[...]
        xn = _rms_norm(h, g_ref[...])
        logits = jnp.dot(xn, wr_ref[...], preferred_element_type=F32,
                         precision=lax.Precision.HIGHEST) + br_ref[...]
        comb_ref[...] = _route(logits)
        xn_ref[...] = xn.astype(BF16)
        out_ref[...] = h

    xn = xn_ref[...]
    comb = comb_ref[...]
    c_e = jnp.sum(jnp.where(_iota(comb.shape, 1) == N_EXPERT_GROUPS + e, comb, 0.0), axis=1, keepdims=True)
    a = jax.nn.silu(_dot(xn, w1_ref[0])) * _dot(xn, w3_ref[0]) * c_e
    out_ref[...] += _dot(a.astype(BF16), w2_ref[0])

    if final_norm:
        @pl.when(e == pl.num_programs(1) - 1)
        def _():
            out_ref[...] = _rms_norm(out_ref[...], gf_ref[...])


def _moe(h, g, wr, br, w1, w3, w2, gf, final_norm):
    n, d = h.shape
    tm = MOE_TILE
    ne, _, de = w1.shape
    full = lambda a: pl.BlockSpec(a.shape, lambda i, e: (0,) * a.ndim)
    return pl.pallas_call(
        functools.partial(_moe_kernel, final_norm=final_norm),
        grid=(n // tm, ne),
        in_specs=[
            pl.BlockSpec((tm, d), lambda i, e: (i, 0)),
            full(g), full(wr), full(br),
            pl.BlockSpec((1, d, de), lambda i, e: (e, 0, 0)),
            pl.BlockSpec((1, d, de), lambda i, e: (e, 0, 0)),
            pl.BlockSpec((1, de, d), lambda i, e: (e, 0, 0)),
            full(gf),
        ],
        out_specs=pl.BlockSpec((tm, d), lambda i, e: (i, 0)),
        out_shape=jax.ShapeDtypeStruct((n, d), F32),
        scratch_shapes=[pltpu.VMEM((tm, d), BF16), pltpu.VMEM((tm, ROUTER_PAD), F32)],
        compiler_params=_params("parallel", "arbitrary"),
        name="moe",
    )(h, g, wr, br, w1, w3, w2, gf)


def _selection_constants(seq_len):
    ncp = seq_len // CMP_STRIDE
    n_slc = seq_len // SLC_BLOCK
    ratio = SLC_BLOCK // CMP_STRIDE
    lead = CMP_LEN // CMP_STRIDE - 1
    c = np.arange(ncp)[:, None]
    j = np.arange(n_slc)[None, :]
    pool_m = ((c >= ratio * j - lead) & (c < ratio * j + ratio)).astype(np.float32)
    blocks_per_chunk = KEY_CHUNK // SLC_BLOCK
    n_chunks = seq_len // KEY_CHUNK
    pair_m = np.zeros((n_slc, LANES * ((n_chunks + LANES - 1) // LANES)), np.float32)
    pair_m[np.arange(n_slc), np.arange(n_slc) // blocks_per_chunk] = 1.0
    n_words = (n_chunks + WORD_BITS - 1) // WORD_BITS
    bits_m = np.zeros((pair_m.shape[1], LANES), np.float32)
    ch = np.arange(n_chunks)
    bits_m[ch, ch // WORD_BITS] = 2.0 ** (ch % WORD_BITS)
    as_bf16 = lambda a: jnp.asarray(a, BF16)
    return as_bf16(pool_m), as_bf16(pair_m), as_bf16(bits_m), n_words


def kernel(x, norm1_g, w_in, cmp_pe, cmp_w1, cmp_w2, w_nsa_proj, pool_w, pool_scale, w_pool_proj, conv_w,
           w_conv_proj, w_o, norm2_g, router_group_w, router_group_b, router_expert_w, router_expert_b,
           expert_w1, expert_w3, expert_w2, final_norm_g):
    b, t, d = x.shape
    n = b * t
    depth = w_in.shape[0]
    dq = NSA_HEADS * HEAD_DIM
    dkv = 6 * NSA_KV_HEADS * HEAD_DIM
    dgate = NSA_HEADS * NSA_BRANCHES
    cw = d // 4
    assert t % TOKEN_TILE == 0 and n % MOE_TILE == 0 and t % (Q_TILE * WORD_BITS // WORD_BITS) == 0
    n_slc = t // SLC_BLOCK
    n_sel = min(SLC_TOPN, n_slc)
    n_chunks16 = t // CMP_STRIDE
    pool_m, pair_m, bits_m, n_words = _selection_constants(t)
    assert n_words <= LANES and pair_m.shape[1] == LANES

    h = x.reshape(n, d)
    for l in range(depth):
        wl = w_in[l]
        o_gate = dq + dkv
        o_pool = o_gate + dgate
        o_merge = o_pool + cw + 3 * cw
        w_a = jnp.concatenate([
            wl[:, :dq] * (HEAD_DIM ** -0.5),
            wl[:, dq:o_gate],
            jnp.pad(wl[:, o_gate:o_pool], ((0, 0), (0, GATE_PAD - dgate))),
            wl[:, o_pool:o_merge],
        ], axis=1).astype(BF16)
        wmg = wl[:, o_merge:].astype(BF16)
        pool_bd = jax.scipy.linalg.block_diag(*[pool_w[l, gi] for gi in range(pool_w.shape[1])]).astype(BF16)
        convw = jnp.pad(conv_w[l], ((0, 8 - CONV_K), (0, 0)))
        wr = jnp.pad(jnp.concatenate([router_group_w[l], router_expert_w[l]], axis=1),
                     ((0, 0), (0, ROUTER_PAD - N_EXPERT_GROUPS - N_EXPERTS)))
        br = jnp.pad(jnp.concatenate([router_group_b[l], router_expert_b[l]]),
                     (0, ROUTER_PAD - N_EXPERT_GROUPS - N_EXPERTS))[None, :]
        pe = jnp.broadcast_to(cmp_pe[l].reshape(2, 1, CMP_LEN * HEAD_DIM), (2, 8, CMP_LEN * HEAD_DIM)).astype(BF16)

        q, kv, gates, pool_u, conv = _inproj(h, norm1_g[l][None, :], w_a)
        chunks = kv[:, :2 * NSA_KV_HEADS * HEAD_DIM].reshape(b, n_chunks16, CMP_STRIDE, 2, NSA_KV_HEADS, HEAD_DIM)
        chunks = chunks.transpose(3, 0, 4, 1, 2, 5).reshape(2, b, NSA_KV_HEADS, n_chunks16, CMP_STRIDE * HEAD_DIM)
        kvc = _compress(chunks, pe, cmp_w1[l].astype(BF16), cmp_w2[l].astype(BF16))
        q3 = q.reshape(b, t, dq)
        oc, sel, words = _nsa_cmp(q3, kvc[0], kvc[1], pool_m, pair_m, bits_m, n_sel)
        words = words[:, :, :, 0, :n_words].reshape(-1)
        nsa = _nsa_slc(words, q3, kv.reshape(b, t, dkv), sel, oc, gates.reshape(b, t, GATE_PAD), n_words)
        h = _merge(h, norm1_g[l][None, :], nsa.reshape(n, dq), pool_u, conv, wmg,
                   w_nsa_proj[l].astype(BF16), pool_bd, pool_scale[l][None, :], w_pool_proj[l].astype(BF16),
                   convw, w_conv_proj[l].astype(BF16), w_o[l].astype(BF16), t)
        h = _moe(h, norm2_g[l][None, :], wr, br, expert_w1[l].astype(BF16), expert_w3[l].astype(BF16),
                 expert_w2[l].astype(BF16), final_norm_g[None, :], final_norm=(l == depth - 1))
    return h.reshape(b, t, d)
```

```python
import functools

import jax
import jax.numpy as jnp
import numpy as np
from jax import lax
from jax.experimental import pallas as pl
from jax.experimental.pallas import tpu as pltpu

F32 = jnp.float32
BF16 = jnp.bfloat16

HEAD_DIM = 64
NSA_HEADS = 8
NSA_KV_HEADS = 2
NSA_GROUP = NSA_HEADS // NSA_KV_HEADS
CMP_LEN = 32
CMP_STRIDE = 16
CMP_HIDDEN = 4 * HEAD_DIM
SLC_BLOCK = 64
SLC_TOPN = 16
WINDOW = 512
NSA_BRANCHES = 3
POOL_WINDOWS = (2, 4, 8, 16)
CONV_K = 3
N_BRANCHES = 3
N_EXPERT_GROUPS = 4
EXPERTS_PER_GROUP = 8
N_EXPERTS = N_EXPERT_GROUPS * EXPERTS_PER_GROUP
RMS_EPS = 1e-6
NEG_INF = -1e30
FORCE_SCORE = 1e30
ALIBI_SLOPES = tuple(float(2.0 ** (-8.0 * (h + 1) / NSA_HEADS)) for h in range(NSA_HEADS))

LANES = 128
VMEM_LIMIT = 56 * 1024 * 1024
TOKEN_TILE = 512
MOE_TILE = 1024
Q_TILE = 128
KEY_CHUNK = 128
GATE_PAD = LANES
ROUTER_PAD = LANES
WORD_BITS = 16


def _params(*semantics):
    return pltpu.CompilerParams(dimension_semantics=semantics, vmem_limit_bytes=VMEM_LIMIT)


def _dot(a, b):
    return jnp.dot(a, b, preferred_element_type=F32)


def _dot_nt(a, b):
    return lax.dot_general(a, b, (((1,), (1,)), ((), ())), preferred_element_type=F32)


def _rms_norm(x, g):
    y = x * lax.rsqrt(jnp.mean(x * x, axis=-1, keepdims=True) + RMS_EPS)
    return y * g


def _iota(shape, dim):
    return lax.broadcasted_iota(jnp.int32, shape, dim)


def _inproj_kernel(x_ref, g_ref, w_ref, q_ref, kv_ref, gate_ref, pool_ref, conv_ref):
    xn = _rms_norm(x_ref[...], g_ref[...]).astype(BF16)
    col = 0
    for ref in (q_ref, kv_ref, gate_ref, pool_ref, conv_ref):
        width = ref.shape[1]
        ref[...] = _dot(xn, w_ref[:, col:col + width]).astype(ref.dtype)
        col += width


def _inproj(h, g, w):
    n, d = h.shape
    dq = NSA_HEADS * HEAD_DIM
    dkv = 6 * NSA_KV_HEADS * HEAD_DIM
    dpool = d // 4
    dconv = 3 * (d // 4)
    widths = (dq, dkv, GATE_PAD, dpool, dconv)
    dtypes = (BF16, BF16, F32, F32, F32)
    tm = TOKEN_TILE
    return pl.pallas_call(
        _inproj_kernel,
        grid=(n // tm,),
        in_specs=[
            pl.BlockSpec((tm, d), lambda i: (i, 0)),
            pl.BlockSpec((1, d), lambda i: (0, 0)),
            pl.BlockSpec((d, sum(widths)), lambda i: (0, 0)),
        ],
        out_specs=[pl.BlockSpec((tm, wd), lambda i: (i, 0)) for wd in widths],
        out_shape=[jax.ShapeDtypeStruct((n, wd), dt) for wd, dt in zip(widths, dtypes)],
        compiler_params=_params("parallel"),
        name="inproj",
    )(h, g, w)


def _gelu_tanh(x):
    return 0.5 * x * (1.0 + jnp.tanh(0.7978845608028654 * (x + 0.044715 * x * x * x)))


def _compress_kernel(c_ref, pe_ref, w1_ref, w2_ref, out_ref):
    n_chunks = c_ref.shape[3]
    half = w1_ref.shape[1] // 2
    w_top = w1_ref[0, :half, :]
    w_bot = w1_ref[0, half:, :]
    bias = _dot(pe_ref[0], w1_ref[0])[0:1, :]
    row = _iota((n_chunks, 1), 0)
    outs = []
    for g in range(NSA_KV_HEADS):
        c = c_ref[0, 0, g]
        first = _dot(c, w_top)
        second = _dot(c, w_bot)
        hid = first + pltpu.roll(second, n_chunks - 1, 0) + bias
        y = _dot(_gelu_tanh(hid).astype(BF16), w2_ref[0])
        outs.append(jnp.where(row < n_chunks - 1, y, 0.0))
    out_ref[0, 0] = jnp.concatenate(outs, axis=1).astype(BF16)


def _compress(chunks, pe, w1, w2):
    _, b, g, nc, cw = chunks.shape
    return pl.pallas_call(
        _compress_kernel,
        grid=(2, b),
        in_specs=[
            pl.BlockSpec((1, 1, g, nc, cw), lambda k, bi: (k, bi, 0, 0, 0)),
            pl.BlockSpec((1,) + pe.shape[1:], lambda k, bi: (k, 0, 0)),
            pl.BlockSpec((1,) + w1.shape[1:], lambda k, bi: (k, 0, 0)),
            pl.BlockSpec((1,) + w2.shape[1:], lambda k, bi: (k, 0, 0)),
        ],
        out_specs=pl.BlockSpec((1, 1, nc, g * HEAD_DIM), lambda k, bi: (k, bi, 0, 0)),
        out_shape=jax.ShapeDtypeStruct((2, b, nc, g * HEAD_DIM), BF16),
        compiler_params=_params("parallel", "parallel"),
        name="compress",
    )(chunks, pe, w1, w2)


SOFTMAX_FLOOR = -1e29


def _slope(g, r):
    s = jnp.float32(ALIBI_SLOPES[r])
    for gi in range(1, NSA_KV_HEADS):
        s = jnp.where(g == gi, jnp.float32(ALIBI_SLOPES[gi * NSA_GROUP + r]), s)
    return s


def _nsa_cmp_kernel(qt_ref, kc_ref, vct_ref, poolt_ref, pair_ref, bits_ref, oct_ref, selt_ref, words_ref,
                    *, n_sel):
    g = pl.program_id(1)
    i = pl.program_id(2)
    qt = selt_ref.shape[4]
    ncp = kc_ref.shape[1]
    n_slc = poolt_ref.shape[0]
    s = _dot(kc_ref[0], qt_ref[0, 0, 0])
    t = i * qt + _iota((1, qt), 1)
    cmp_end = _iota((ncp, 1), 0) * CMP_STRIDE + (CMP_LEN - 1)
    d = (t - cmp_end).astype(F32)
    valid = d >= 0.0
    vct = vct_ref[0]
    psum = jnp.zeros((ncp, qt), F32)
    for r in range(NSA_GROUP):
        cols = slice(r * qt, (r + 1) * qt)
        sr = jnp.where(valid, s[:, cols] - _slope(g, r) * d, NEG_INF)
        m = jnp.max(sr, axis=0, keepdims=True)
        e = jnp.where(valid, jnp.exp(sr - m), 0.0)
        l = jnp.sum(e, axis=0, keepdims=True)
        p = e * jnp.where(l > 0.0, 1.0 / l, 0.0)
        psum = psum + p
        oct_ref[0, 0, 0, :, cols] = _dot(vct, p.astype(BF16)).astype(BF16)
    p_hi = psum.astype(BF16)
    p_lo = (psum - p_hi.astype(F32)).astype(BF16)
    imp = _dot(poolt_ref[...], p_hi) + _dot(poolt_ref[...], p_lo)
    blk = _iota((n_slc, 1), 0)
    cur = t // SLC_BLOCK
    forced = (blk == 0) | (blk == cur) | (blk == cur - 1)
    score = jnp.where(forced, FORCE_SCORE, jnp.where(blk <= cur, imp, NEG_INF))
    blk_f = blk.astype(F32)
    sel = jnp.zeros((n_slc, qt), F32)
    for _ in range(n_sel):
        m = jnp.max(score, axis=0, keepdims=True)
        first = jnp.min(jnp.where(score == m, blk_f, F32(1e9)), axis=0, keepdims=True)
        hit = blk_f == first
        sel = jnp.where(hit, 1.0, sel)
        score = jnp.where(hit, -3e38, score)
    sel_bf = sel.astype(BF16)
    selt_ref[0, 0, 0] = sel_bf
    count = _dot_nt(jnp.ones((8, qt), BF16), sel_bf)
    used = jnp.where(count > 0.0, 1.0, 0.0).astype(BF16)
    chunk_used = jnp.where(_dot(used, pair_ref[...]) > 0.0, 1.0, 0.0).astype(BF16)
    words_ref[0, 0, 0] = _dot(chunk_used, bits_ref[...]).astype(jnp.int32)


def _nsa_cmp(qt_pad, kc, vct, poolt_m, pair_m, bits_m, n_sel):
    b, _, n_tiles, qrows, qcols = qt_pad.shape
    ncp = kc.shape[1]
    n_slc = poolt_m.shape[0]
    kvw = NSA_KV_HEADS * HEAD_DIM
    const = lambda shape: pl.BlockSpec(shape, lambda bi, g, i: (0,) * len(shape))
    return pl.pallas_call(
        functools.partial(_nsa_cmp_kernel, n_sel=n_sel),
        grid=(b, NSA_KV_HEADS, n_tiles),
        in_specs=[
            pl.BlockSpec((1, 1, 1, qrows, qcols), lambda bi, g, i: (bi, g, i, 0, 0)),
            pl.BlockSpec((1, ncp, kvw), lambda bi, g, i: (bi, 0, 0)),
            pl.BlockSpec((1, HEAD_DIM, ncp), lambda bi, g, i: (bi, g, 0)),
            const(poolt_m.shape), const(pair_m.shape), const(bits_m.shape),
        ],
        out_specs=[
            pl.BlockSpec((1, 1, 1, HEAD_DIM, qcols), lambda bi, g, i: (bi, g, i, 0, 0)),
            pl.BlockSpec((1, 1, 1, n_slc, Q_TILE), lambda bi, g, i: (bi, g, i, 0, 0)),
            pl.BlockSpec((1, 1, 1, 8, LANES), lambda bi, g, i: (bi, g, i, 0, 0)),
        ],
        out_shape=[
            jax.ShapeDtypeStruct((b, NSA_KV_HEADS, n_tiles, HEAD_DIM, qcols), BF16),
            jax.ShapeDtypeStruct((b, NSA_KV_HEADS, n_tiles, n_slc, Q_TILE), BF16),
            jax.ShapeDtypeStruct((b, NSA_KV_HEADS, n_tiles, 8, LANES), jnp.int32),
        ],
        compiler_params=_params("parallel", "parallel", "parallel"),
        name="nsa_compressed",
    )(qt_pad, kc, vct, poolt_m, pair_m, bits_m)


def _nsa_slc_kernel(words_ref, qt_ref, ks_ref, vst_ref, kw_ref, vwt_ref, selt_ref, oct_ref, gt_ref,
                    out_ref, m_ref, l_ref, acc_ref, ow_ref, *, words_per_tile):
    bi = pl.program_id(0)
    g = pl.program_id(1)
    i = pl.program_id(2)
    n_tiles = pl.num_programs(2)
    qt = out_ref.shape[1]
    n_slc = selt_ref.shape[3]
    q_t = qt_ref[0, 0, 0]
    start = i * qt
    t = start + _iota((1, qt), 1)
    key = _iota((KEY_CHUNK, 1), 0)

    def reset():
        m_ref[...] = jnp.full(m_ref.shape, SOFTMAX_FLOOR, F32)
        l_ref[...] = jnp.zeros(l_ref.shape, F32)
        acc_ref[...] = jnp.zeros(acc_ref.shape, F32)

    def attend(k, v_t, d, mask):
        s = _dot(k, q_t)
        for r in range(NSA_GROUP):
            cols = slice(r * qt, (r + 1) * qt)
            sr = jnp.where(mask, s[:, cols] - _slope(g, r) * d, NEG_INF)
            m_old = m_ref[r:r + 1, :]
            m_new = jnp.maximum(m_old, jnp.max(sr, axis=0, keepdims=True))
            alpha = jnp.exp(m_old - m_new)
            p = jnp.exp(sr - m_new)
            l_ref[r:r + 1, :] = alpha * l_ref[r:r + 1, :] + jnp.sum(p, axis=0, keepdims=True)
            acc_ref[:, cols] = alpha * acc_ref[:, cols] + _dot(v_t, p.astype(BF16))
            m_ref[r:r + 1, :] = m_new

    def finalize(r):
        l = l_ref[r:r + 1, :]
        return acc_ref[:, r * qt:(r + 1) * qt] * jnp.where(l > 0.0, 1.0 / l, 0.0)

    reset()
    for c in range(WINDOW // KEY_CHUNK + qt // KEY_CHUNK):
        cs = start - WINDOW + c * KEY_CHUNK

        @pl.when(cs >= 0)
        def _():
            at = pl.multiple_of(jnp.maximum(cs, 0), KEY_CHUNK)
            d = t - (cs + key)
            mask = (d >= 0) & (d < WINDOW)
            attend(kw_ref[0, pl.ds(at, KEY_CHUNK), :], vwt_ref[0, 0, jnp.maximum(cs, 0) // KEY_CHUNK],
                   d.astype(F32), mask)

    for r in range(NSA_GROUP):
        ow_ref[:, r * qt:(r + 1) * qt] = finalize(r)

    reset()
    sel_t = selt_ref[0, 0, 0]
    word_base = ((bi * NSA_KV_HEADS + g) * n_tiles + i) * words_per_tile
    blocks_per_chunk = KEY_CHUNK // SLC_BLOCK

    def chunk_step(c, carry):
        word = words_ref[word_base + c // WORD_BITS]
        used = (word >> (c % WORD_BITS)) & 1

        @pl.when(used == 1)
        def _():
            at = pl.multiple_of(c * KEY_CHUNK, KEY_CHUNK)
            expand = (_iota((KEY_CHUNK, n_slc), 1)
                      == c * blocks_per_chunk + _iota((KEY_CHUNK, n_slc), 0) // SLC_BLOCK)
            chosen = _dot(jnp.where(expand, 1.0, 0.0).astype(BF16), sel_t)
            d = t - (c * KEY_CHUNK + key)
            mask = (chosen > 0.5) & (d >= 0)
            attend(ks_ref[0, pl.ds(at, KEY_CHUNK), :], vst_ref[0, 0, c], d.astype(F32), mask)

        return carry

    lax.fori_loop(0, (start + qt) // KEY_CHUNK, chunk_step, 0)

    outs = []
    for r in range(NSA_GROUP):
        cols = slice(r * qt, (r + 1) * qt)
        col = (g * NSA_GROUP + r) * NSA_BRANCHES
        gate = lambda br: jax.nn.sigmoid(gt_ref[0, pl.ds(col + br, 1), :])
        outs.append(gate(0) * oct_ref[0, 0, 0, :, cols].astype(F32) + gate(1) * finalize(r)
                    + gate(2) * ow_ref[:, cols])
    out_ref[0] = jnp.concatenate(outs, axis=0).T.astype(out_ref.dtype)


def _nsa_slc(words, qt_pad, kv, vst, vwt, selt, oct, gates_t, words_per_tile):
    b, t, _ = kv.shape
    _, _, n_tiles, qrows, qcols = qt_pad.shape
    n_slc = selt.shape[3]
    n_chunks = vst.shape[2]
    gw = NSA_GROUP * HEAD_DIM
    kvw = NSA_KV_HEADS * HEAD_DIM
    once = dict(pipeline_mode=pl.Buffered(1))
    k_spec = lambda kind: pl.BlockSpec((1, t, kvw), lambda bi, g, i, w: (bi, 0, kind), **once)
    vt_spec = pl.BlockSpec((1, 1, n_chunks, HEAD_DIM, KEY_CHUNK), lambda bi, g, i, w: (bi, g, 0, 0, 0), **once)
    tile5 = lambda rows, cols: pl.BlockSpec((1, 1, 1, rows, cols), lambda bi, g, i, w: (bi, g, i, 0, 0))
    grid_spec = pltpu.PrefetchScalarGridSpec(
        num_scalar_prefetch=1,
        grid=(b, NSA_KV_HEADS, n_tiles),
        in_specs=[
            tile5(qrows, qcols),
            k_spec(2), vt_spec, k_spec(4), vt_spec,
            tile5(n_slc, Q_TILE),
            tile5(HEAD_DIM, qcols),
            pl.BlockSpec((1, GATE_PAD, Q_TILE), lambda bi, g, i, w: (bi, 0, i)),
        ],
        out_specs=pl.BlockSpec((1, Q_TILE, gw), lambda bi, g, i, w: (bi, i, g)),
        scratch_shapes=[pltpu.VMEM((8, Q_TILE), F32), pltpu.VMEM((8, Q_TILE), F32),
                        pltpu.VMEM((HEAD_DIM, qcols), F32), pltpu.VMEM((HEAD_DIM, qcols), F32)],
    )
    return pl.pallas_call(
        functools.partial(_nsa_slc_kernel, words_per_tile=words_per_tile),
        grid_spec=grid_spec,
        out_shape=jax.ShapeDtypeStruct((b, t, NSA_HEADS * HEAD_DIM), BF16),
        compiler_params=_params("parallel", "parallel", "parallel"),
        name="nsa_selected_window",
    )(words, qt_pad, kv, vst, kv, vwt, selt, oct, gates_t)


POOL_HALO = 16
CONV_HALO = 8


def _merge_kernel(h_ref, g_ref, nsa_ref, pool_ref, pool_halo_ref, conv_ref, conv_halo_ref,
                  wmg_ref, wnsa_ref, pool_bd_ref, pool_scale_ref, wpool_ref, convw_ref, wconv_ref, wo_ref,
                  out_ref, pool_ext, conv_ext, *, seq_len):
    i = pl.program_id(0)
    tm, d = h_ref.shape
    cw = pool_ref.shape[1]
    pos0 = (i * tm) % seq_len
    keep_halo = jnp.where(pos0 == 0, 0.0, 1.0)
    pos = pos0 + _iota((tm, 1), 0)

    u = pool_ref[...]
    pool_ext[0:POOL_HALO, :] = pool_halo_ref[...] * keep_halo
    pool_ext[POOL_HALO:, :] = u
    lane_group = _iota((1, cw), 1) // (cw // len(POOL_WINDOWS))
    total = u
    mean = jnp.zeros_like(u)
    done = 1
    for gi, win in enumerate(POOL_WINDOWS):
        for k in range(done, win):
            total = total + pool_ext[POOL_HALO - k:POOL_HALO - k + tm, :]
        done = win
        cnt = jnp.minimum(pos + 1, win).astype(F32)
        mean = jnp.where(lane_group == gi, total / cnt, mean)
    pooled = (mean - u).astype(BF16)
    mixed = _dot(pooled, pool_bd_ref[...]) * pool_scale_ref[...]
    y_pool = _dot(mixed.astype(BF16), wpool_ref[...])

    ch = conv_ref[:, 0:cw]
    cb = conv_ref[:, cw:2 * cw]
    cc = conv_ref[:, 2 * cw:3 * cw]
    conv_ext[0:CONV_HALO, :] = conv_halo_ref[:, 0:cw] * conv_halo_ref[:, 2 * cw:3 * cw] * keep_halo
    conv_ext[CONV_HALO:, :] = cc * ch
    y = jnp.zeros((tm, cw), F32)
    for k in range(CONV_K):
        off = CONV_HALO - (CONV_K - 1) + k
        y = y + convw_ref[k:k + 1, :] * conv_ext[off:off + tm, :]
    y_conv = _dot((cb * y).astype(BF16), wconv_ref[...])

    y_nsa = _dot(nsa_ref[...], wnsa_ref[...])

    h = h_ref[...]
    xn = _rms_norm(h, g_ref[...]).astype(BF16)
    merged = jnp.zeros((tm, d), F32)
    for br, y_br in enumerate((y_nsa, y_pool, y_conv)):
        mg = jax.nn.sigmoid(_dot(xn, wmg_ref[:, br * d:(br + 1) * d]))
        merged = merged + mg * y_br
    out_ref[...] = h + _dot(merged.astype(BF16), wo_ref[...])


def _merge(h, g, nsa, pool_u, conv, wmg, wnsa, pool_bd, pool_scale, wpool, convw, wconv, wo, seq_len):
    n, d = h.shape
    tm = TOKEN_TILE
    cw = pool_u.shape[1]
    row = lambda width: pl.BlockSpec((tm, width), lambda i: (i, 0))
    full = lambda a: pl.BlockSpec(a.shape, lambda i: (0,) * a.ndim)
    halo = lambda rows, width: pl.BlockSpec(
        (rows, width), lambda i: (jnp.maximum(i * (tm // rows) - 1, 0), 0))
    return pl.pallas_call(
        functools.partial(_merge_kernel, seq_len=seq_len),
        grid=(n // tm,),
        in_specs=[row(d), full(g), row(nsa.shape[1]), row(cw), halo(POOL_HALO, cw),
                  row(conv.shape[1]), halo(CONV_HALO, conv.shape[1]),
                  full(wmg), full(wnsa), full(pool_bd), full(pool_scale), full(wpool), full(convw),
                  full(wconv), full(wo)],
        out_specs=row(d),
        out_shape=jax.ShapeDtypeStruct((n, d), F32),
        scratch_shapes=[pltpu.VMEM((tm + POOL_HALO, cw), F32), pltpu.VMEM((tm + CONV_HALO, cw), F32)],
        compiler_params=_params("parallel"),
        name="merge",
    )(h, g, nsa, pool_u, pool_u, conv, conv, wmg, wnsa, pool_bd, pool_scale, wpool, convw, wconv, wo)


def _route(logits):
    lane = _iota(logits.shape, 1)
    lane_f = lane.astype(F32)
    big = F32(1e9)
    is_group = lane < N_EXPERT_GROUPS
    gl = jnp.where(is_group, logits, NEG_INF)
    g_max = jnp.max(gl, axis=1, keepdims=True)
    g_sel = jnp.min(jnp.where(gl == g_max, lane_f, big), axis=1, keepdims=True)
    g_prob = 1.0 / jnp.sum(jnp.where(is_group, jnp.exp(gl - g_max), 0.0), axis=1, keepdims=True)
    lo = N_EXPERT_GROUPS + EXPERTS_PER_GROUP * g_sel
    in_group = (lane_f >= lo) & (lane_f < lo + EXPERTS_PER_GROUP)
    el = jnp.where(in_group, logits, NEG_INF)
    v1 = jnp.max(el, axis=1, keepdims=True)
    i1 = jnp.min(jnp.where((el == v1) & in_group, lane_f, big), axis=1, keepdims=True)
    el2 = jnp.where(lane_f == i1, NEG_INF, el)
    rest = in_group & (lane_f != i1)
    v2 = jnp.max(el2, axis=1, keepdims=True)
    i2 = jnp.min(jnp.where((el2 == v2) & rest, lane_f, big), axis=1, keepdims=True)
    e2 = jnp.exp(v2 - v1)
    w1 = g_prob / (1.0 + e2)
    w2 = g_prob * e2 / (1.0 + e2)
    return jnp.where(lane_f == i1, w1, 0.0) + jnp.where(lane_f == i2, w2, 0.0)


def _moe_kernel(h_ref, g_ref, wr_ref, br_ref, w1_ref, w3_ref, w2_ref, gf_ref, out_ref, xn_ref, comb_ref,
                *, final_norm):
    e = pl.program_id(1)

    @pl.when(e == 0)
    def _():
        h = h_ref[...]
        xn = _rms_norm(h, g_ref[...])
        logits = jnp.dot(xn, wr_ref[...], preferred_element_type=F32,
                         precision=lax.Precision.HIGHEST) + br_ref[...]
        comb_ref[...] = _route(logits)
        xn_ref[...] = xn.astype(BF16)
        out_ref[...] = h

    xn = xn_ref[...]
    comb = comb_ref[...]
    c_e = jnp.sum(jnp.where(_iota(comb.shape, 1) == N_EXPERT_GROUPS + e, comb, 0.0), axis=1, keepdims=True)
    a = jax.nn.silu(_dot(xn, w1_ref[0])) * _dot(xn, w3_ref[0]) * c_e
    out_ref[...] += _dot(a.astype(BF16), w2_ref[0])

    if final_norm:
        @pl.when(e == pl.num_programs(1) - 1)
        def _():
            out_ref[...] = _rms_norm(out_ref[...], gf_ref[...])


def _moe(h, g, wr, br, w1, w3, w2, gf, final_norm):
    n, d = h.shape
    tm = MOE_TILE
    ne, _, de = w1.shape
    full = lambda a: pl.BlockSpec(a.shape, lambda i, e: (0,) * a.ndim)
    return pl.pallas_call(
        functools.partial(_moe_kernel, final_norm=final_norm),
        grid=(n // tm, ne),
        in_specs=[
            pl.BlockSpec((tm, d), lambda i, e: (i, 0)),
            full(g), full(wr), full(br),
            pl.BlockSpec((1, d, de), lambda i, e: (e, 0, 0)),
            pl.BlockSpec((1, d, de), lambda i, e: (e, 0, 0)),
            pl.BlockSpec((1, de, d), lambda i, e: (e, 0, 0)),
            full(gf),
        ],
        out_specs=pl.BlockSpec((tm, d), lambda i, e: (i, 0)),
        out_shape=jax.ShapeDtypeStruct((n, d), F32),
        scratch_shapes=[pltpu.VMEM((tm, d), BF16), pltpu.VMEM((tm, ROUTER_PAD), F32)],
        compiler_params=_params("parallel", "arbitrary"),
        name="moe",
    )(h, g, wr, br, w1, w3, w2, gf)


def _selection_constants(seq_len):
    ncp = seq_len // CMP_STRIDE
    n_slc = seq_len // SLC_BLOCK
    ratio = SLC_BLOCK // CMP_STRIDE
    lead = CMP_LEN // CMP_STRIDE - 1
    c = np.arange(ncp)[:, None]
    j = np.arange(n_slc)[None, :]
    pool_m = ((c >= ratio * j - lead) & (c < ratio * j + ratio)).astype(np.float32)
    blocks_per_chunk = KEY_CHUNK // SLC_BLOCK
    n_chunks = seq_len // KEY_CHUNK
    pair_m = np.zeros((n_slc, LANES * ((n_chunks + LANES - 1) // LANES)), np.float32)
    pair_m[np.arange(n_slc), np.arange(n_slc) // blocks_per_chunk] = 1.0
    n_words = (n_chunks + WORD_BITS - 1) // WORD_BITS
    bits_m = np.zeros((pair_m.shape[1], LANES), np.float32)
    ch = np.arange(n_chunks)
    bits_m[ch, ch // WORD_BITS] = 2.0 ** (ch % WORD_BITS)
    as_bf16 = lambda a: jnp.asarray(a, BF16)
    return as_bf16(pool_m.T), as_bf16(pair_m), as_bf16(bits_m), n_words


def kernel(x, norm1_g, w_in, cmp_pe, cmp_w1, cmp_w2, w_nsa_proj, pool_w, pool_scale, w_pool_proj, conv_w,
           w_conv_proj, w_o, norm2_g, router_group_w, router_group_b, router_expert_w, router_expert_b,
           expert_w1, expert_w3, expert_w2, final_norm_g):
    b, t, d = x.shape
    n = b * t
    depth = w_in.shape[0]
    dq = NSA_HEADS * HEAD_DIM
    dkv = 6 * NSA_KV_HEADS * HEAD_DIM
    dgate = NSA_HEADS * NSA_BRANCHES
    cw = d // 4
    assert t % TOKEN_TILE == 0 and n % MOE_TILE == 0 and t % Q_TILE == 0
    n_slc = t // SLC_BLOCK
    n_sel = min(SLC_TOPN, n_slc)
    n_chunks16 = t // CMP_STRIDE
    n_tiles = t // Q_TILE
    poolt_m, pair_m, bits_m, n_words = _selection_constants(t)
    assert n_words <= LANES and pair_m.shape[1] == LANES

    h = x.reshape(n, d)
    for l in range(depth):
        wl = w_in[l]
        o_gate = dq + dkv
        o_pool = o_gate + dgate
        o_merge = o_pool + cw + 3 * cw
        w_a = jnp.concatenate([
            wl[:, :dq] * (HEAD_DIM ** -0.5),
            wl[:, dq:o_gate],
            jnp.pad(wl[:, o_gate:o_pool], ((0, 0), (0, GATE_PAD - dgate))),
            wl[:, o_pool:o_merge],
        ], axis=1).astype(BF16)
        wmg = wl[:, o_merge:].astype(BF16)
        pool_bd = jax.scipy.linalg.block_diag(*[pool_w[l, gi] for gi in range(pool_w.shape[1])]).astype(BF16)
        convw = jnp.pad(conv_w[l], ((0, 8 - CONV_K), (0, 0)))
        wr = jnp.pad(jnp.concatenate([router_group_w[l], router_expert_w[l]], axis=1),
                     ((0, 0), (0, ROUTER_PAD - N_EXPERT_GROUPS - N_EXPERTS)))
        br = jnp.pad(jnp.concatenate([router_group_b[l], router_expert_b[l]]),
                     (0, ROUTER_PAD - N_EXPERT_GROUPS - N_EXPERTS))[None, :]
        pe = jnp.broadcast_to(cmp_pe[l].reshape(2, 1, CMP_LEN * HEAD_DIM), (2, 8, CMP_LEN * HEAD_DIM)).astype(BF16)

        q, kv, gates, pool_u, conv = _inproj(h, norm1_g[l][None, :], w_a)
        chunks = kv[:, :2 * NSA_KV_HEADS * HEAD_DIM].reshape(b, n_chunks16, CMP_STRIDE, 2, NSA_KV_HEADS, HEAD_DIM)
        chunks = chunks.transpose(3, 0, 4, 1, 2, 5).reshape(2, b, NSA_KV_HEADS, n_chunks16, CMP_STRIDE * HEAD_DIM)
        kvc = _compress(chunks, pe, cmp_w1[l].astype(BF16), cmp_w2[l].astype(BF16))
        kv3 = kv.reshape(b, t, dkv)
        kvw = NSA_KV_HEADS * HEAD_DIM
        q_t = q.reshape(b, n_tiles, Q_TILE, NSA_KV_HEADS, NSA_GROUP, HEAD_DIM).transpose(0, 3, 1, 5, 4, 2)
        q_t = q_t.reshape(b, NSA_KV_HEADS, n_tiles, 1, HEAD_DIM, NSA_GROUP * Q_TILE)
        q_t = (q_t * jnp.eye(NSA_KV_HEADS, dtype=BF16)[None, :, None, :, None, None])
        q_t = q_t.reshape(b, NSA_KV_HEADS, n_tiles, kvw, NSA_GROUP * Q_TILE)
        chunked_t = lambda kind: kv3[:, :, kind * kvw:(kind + 1) * kvw].reshape(
            b, t // KEY_CHUNK, KEY_CHUNK, NSA_KV_HEADS, HEAD_DIM).transpose(0, 3, 1, 4, 2)
        oc_t, sel_t, words = _nsa_cmp(q_t, kvc[0], kvc[1].transpose(0, 2, 1), poolt_m, pair_m, bits_m, n_sel)
        words = words[:, :, :, 0, :n_words].reshape(-1)
        nsa = _nsa_slc(words, q_t, kv3, chunked_t(3), chunked_t(5), sel_t, oc_t,
                       gates.reshape(b, t, GATE_PAD).transpose(0, 2, 1), n_words)
        h = _merge(h, norm1_g[l][None, :], nsa.reshape(n, dq), pool_u, conv, wmg,
                   w_nsa_proj[l].astype(BF16), pool_bd, pool_scale[l][None, :], w_pool_proj[l].astype(BF16),
                   convw, w_conv_proj[l].astype(BF16), w_o[l].astype(BF16), t)
        h = _moe(h, norm2_g[l][None, :], wr, br, expert_w1[l].astype(BF16), expert_w3[l].astype(BF16),
                 expert_w2[l].astype(BF16), final_norm_g[None, :], final_norm=(l == depth - 1))
    return h.reshape(b, t, d)
```

```python
import functools

import jax
import jax.numpy as jnp
import numpy as np
from jax import lax
from jax.experimental import pallas as pl
from jax.experimental.pallas import tpu as pltpu

F32 = jnp.float32
BF16 = jnp.bfloat16

HEAD_DIM = 64
NSA_HEADS = 8
NSA_KV_HEADS = 2
NSA_GROUP = NSA_HEADS // NSA_KV_HEADS
CMP_LEN = 32
CMP_STRIDE = 16
CMP_HIDDEN = 4 * HEAD_DIM
SLC_BLOCK = 64
SLC_TOPN = 16
WINDOW = 512
NSA_BRANCHES = 3
POOL_WINDOWS = (2, 4, 8, 16)
CONV_K = 3
N_BRANCHES = 3
N_EXPERT_GROUPS = 4
EXPERTS_PER_GROUP = 8
N_EXPERTS = N_EXPERT_GROUPS * EXPERTS_PER_GROUP
RMS_EPS = 1e-6
NEG_INF = -1e30
FORCE_SCORE = 1e30
ALIBI_SLOPES = tuple(float(2.0 ** (-8.0 * (h + 1) / NSA_HEADS)) for h in range(NSA_HEADS))

LANES = 128
VMEM_LIMIT = 56 * 1024 * 1024
TOKEN_TILE = 512
MOE_TILE = 1024
Q_TILE = 128
KEY_CHUNK = 128
GATE_PAD = LANES
ROUTER_PAD = LANES
WORD_BITS = 16


def _params(*semantics):
    return pltpu.CompilerParams(dimension_semantics=semantics, vmem_limit_bytes=VMEM_LIMIT)


def _dot(a, b):
    return jnp.dot(a, b, preferred_element_type=F32)


def _dot_nt(a, b):
    return lax.dot_general(a, b, (((1,), (1,)), ((), ())), preferred_element_type=F32)


def _rms_norm(x, g):
    y = x * lax.rsqrt(jnp.mean(x * x, axis=-1, keepdims=True) + RMS_EPS)
    return y * g


def _iota(shape, dim):
    return lax.broadcasted_iota(jnp.int32, shape, dim)


def _inproj_kernel(x_ref, g_ref, w_ref, q_ref, kv_ref, gate_ref, pool_ref, conv_ref):
    xn = _rms_norm(x_ref[...], g_ref[...]).astype(BF16)
    col = 0
    for ref in (q_ref, kv_ref, gate_ref, pool_ref, conv_ref):
        width = ref.shape[1]
        ref[...] = _dot(xn, w_ref[:, col:col + width]).astype(ref.dtype)
        col += width


def _inproj(h, g, w):
    n, d = h.shape
    dq = NSA_HEADS * HEAD_DIM
    dkv = 6 * NSA_KV_HEADS * HEAD_DIM
    dpool = d // 4
    dconv = 3 * (d // 4)
    widths = (dq, dkv, GATE_PAD, dpool, dconv)
    dtypes = (BF16, BF16, F32, F32, F32)
    tm = TOKEN_TILE
    return pl.pallas_call(
        _inproj_kernel,
        grid=(n // tm,),
        in_specs=[
            pl.BlockSpec((tm, d), lambda i: (i, 0)),
            pl.BlockSpec((1, d), lambda i: (0, 0)),
            pl.BlockSpec((d, sum(widths)), lambda i: (0, 0)),
        ],
        out_specs=[pl.BlockSpec((tm, wd), lambda i: (i, 0)) for wd in widths],
        out_shape=[jax.ShapeDtypeStruct((n, wd), dt) for wd, dt in zip(widths, dtypes)],
        compiler_params=_params("parallel"),
        name="inproj",
    )(h, g, w)


def _gelu_tanh(x):
    return 0.5 * x * (1.0 + jnp.tanh(0.7978845608028654 * (x + 0.044715 * x * x * x)))


def _compress_kernel(c_ref, pe_ref, w1_ref, w2_ref, out_ref):
    n_chunks = c_ref.shape[3]
    half = w1_ref.shape[1] // 2
    w_top = w1_ref[0, :half, :]
    w_bot = w1_ref[0, half:, :]
    bias = _dot(pe_ref[0], w1_ref[0])[0:1, :]
    row = _iota((n_chunks, 1), 0)
    outs = []
    for g in range(NSA_KV_HEADS):
        c = c_ref[0, 0, g]
        first = _dot(c, w_top)
        second = _dot(c, w_bot)
        hid = first + pltpu.roll(second, n_chunks - 1, 0) + bias
        y = _dot(_gelu_tanh(hid).astype(BF16), w2_ref[0])
        outs.append(jnp.where(row < n_chunks - 1, y, 0.0))
    out_ref[0, 0] = jnp.concatenate(outs, axis=1).astype(BF16)


def _compress(chunks, pe, w1, w2):
    _, b, g, nc, cw = chunks.shape
    return pl.pallas_call(
        _compress_kernel,
        grid=(2, b),
        in_specs=[
            pl.BlockSpec((1, 1, g, nc, cw), lambda k, bi: (k, bi, 0, 0, 0)),
            pl.BlockSpec((1,) + pe.shape[1:], lambda k, bi: (k, 0, 0)),
            pl.BlockSpec((1,) + w1.shape[1:], lambda k, bi: (k, 0, 0)),
            pl.BlockSpec((1,) + w2.shape[1:], lambda k, bi: (k, 0, 0)),
        ],
        out_specs=pl.BlockSpec((1, 1, nc, g * HEAD_DIM), lambda k, bi: (k, bi, 0, 0)),
        out_shape=jax.ShapeDtypeStruct((2, b, nc, g * HEAD_DIM), BF16),
        compiler_params=_params("parallel", "parallel"),
        name="compress",
    )(chunks, pe, w1, w2)


SOFTMAX_FLOOR = -1e29
ATTN_BATCH = 4
BLOCKS_PER_CHUNK = KEY_CHUNK // SLC_BLOCK
AUX_COLS = 16
AUX_SLOPE = BLOCKS_PER_CHUNK


def _slope(g, r):
    s = jnp.float32(ALIBI_SLOPES[r])
    for gi in range(1, NSA_KV_HEADS):
        s = jnp.where(g == gi, jnp.float32(ALIBI_SLOPES[gi * NSA_GROUP + r]), s)
    return s


def _nsa_cmp_kernel(qt_ref, kc_ref, vct_ref, poolt_ref, pair_ref, bits_ref, oct_ref, selt_ref, words_ref,
                    *, n_sel):
    g = pl.program_id(1)
    i = pl.program_id(2)
    qt = selt_ref.shape[4]
    ncp = kc_ref.shape[1]
    n_slc = poolt_ref.shape[0]
    q_rows = qt_ref[0, 0, 0]
    q_t = jnp.concatenate([jnp.where(g == gi, q_rows, jnp.zeros_like(q_rows)) for gi in range(NSA_KV_HEADS)],
                          axis=0)
    s = _dot(kc_ref[0], q_t)
    t = i * qt + _iota((1, qt), 1)
    cmp_end = _iota((ncp, 1), 0) * CMP_STRIDE + (CMP_LEN - 1)
    d = (t - cmp_end).astype(F32)
    valid = d >= 0.0
    vct = vct_ref[0]
    psum = jnp.zeros((ncp, qt), F32)
    for r in range(NSA_GROUP):
        cols = slice(r * qt, (r + 1) * qt)
        sr = jnp.where(valid, s[:, cols] - _slope(g, r) * d, NEG_INF)
        m = jnp.max(sr, axis=0, keepdims=True)
        e = jnp.where(valid, jnp.exp(sr - m), 0.0)
        l = jnp.sum(e, axis=0, keepdims=True)
        p = e * jnp.where(l > 0.0, 1.0 / l, 0.0)
        psum = psum + p
        oct_ref[0, 0, 0, :, cols] = _dot(vct, p.astype(BF16)).astype(BF16)
    p_hi = psum.astype(BF16)
    p_lo = (psum - p_hi.astype(F32)).astype(BF16)
    imp = _dot(poolt_ref[...], p_hi) + _dot(poolt_ref[...], p_lo)
    blk = _iota((n_slc, 1), 0)
    cur = t // SLC_BLOCK
    forced = (blk == 0) | (blk == cur) | (blk == cur - 1)
    score = jnp.where(forced, FORCE_SCORE, jnp.where(blk <= cur, imp, NEG_INF))
    blk_f = blk.astype(F32)
    sel = jnp.zeros((n_slc, qt), F32)
    for _ in range(n_sel):
        m = jnp.max(score, axis=0, keepdims=True)
        first = jnp.min(jnp.where(score == m, blk_f, F32(1e9)), axis=0, keepdims=True)
        hit = blk_f == first
        sel = jnp.where(hit, 1.0, sel)
        score = jnp.where(hit, -3e38, score)
    sel_bf = sel.astype(BF16)
    selt_ref[0, 0, 0] = jnp.where(sel > 0.5, 0.0, NEG_INF)
    count = _dot_nt(jnp.ones((8, qt), BF16), sel_bf)
    used = jnp.where(count > 0.0, 1.0, 0.0).astype(BF16)
    chunk_used = jnp.where(_dot(used, pair_ref[...]) > 0.0, 1.0, 0.0).astype(BF16)
    words_ref[0, 0, 0] = _dot(chunk_used, bits_ref[...]).astype(jnp.int32)


def _nsa_cmp(qt_pad, kc, vct, poolt_m, pair_m, bits_m, n_sel):
    b, _, n_tiles, qrows, qcols = qt_pad.shape
    ncp = kc.shape[1]
    n_slc = poolt_m.shape[0]
    kvw = NSA_KV_HEADS * HEAD_DIM
    const = lambda shape: pl.BlockSpec(shape, lambda bi, g, i: (0,) * len(shape))
    return pl.pallas_call(
        functools.partial(_nsa_cmp_kernel, n_sel=n_sel),
        grid=(b, NSA_KV_HEADS, n_tiles),
        in_specs=[
            pl.BlockSpec((1, 1, 1, qrows, qcols), lambda bi, g, i: (bi, g, i, 0, 0)),
            pl.BlockSpec((1, ncp, kvw), lambda bi, g, i: (bi, 0, 0)),
            pl.BlockSpec((1, HEAD_DIM, ncp), lambda bi, g, i: (bi, g, 0)),
            const(poolt_m.shape), const(pair_m.shape), const(bits_m.shape),
        ],
        out_specs=[
            pl.BlockSpec((1, 1, 1, HEAD_DIM, qcols), lambda bi, g, i: (bi, g, i, 0, 0)),
            pl.BlockSpec((1, 1, 1, n_slc, Q_TILE), lambda bi, g, i: (bi, g, i, 0, 0)),
            pl.BlockSpec((1, 1, 1, 8, LANES), lambda bi, g, i: (bi, g, i, 0, 0)),
        ],
        out_shape=[
            jax.ShapeDtypeStruct((b, NSA_KV_HEADS, n_tiles, HEAD_DIM, qcols), BF16),
            jax.ShapeDtypeStruct((b, NSA_KV_HEADS, n_tiles, n_slc, Q_TILE), F32),
            jax.ShapeDtypeStruct((b, NSA_KV_HEADS, n_tiles, 8, LANES), jnp.int32),
        ],
        compiler_params=_params("parallel", "parallel", "parallel"),
        name="nsa_compressed",
    )(qt_pad, kc, vct, poolt_m, pair_m, bits_m)


def _nsa_slc_kernel(words_ref, qt_ref, ks_ref, vst_ref, kw_ref, vwt_ref, selt_ref, oct_ref, gt_ref,
                    out_ref, m_ref, l_ref, acc_ref, ow_ref, list_ref, *, words_per_tile):
    bi = pl.program_id(0)
    g = pl.program_id(1)
    i = pl.program_id(2)
    n_tiles = pl.num_programs(2)
    qt = out_ref.shape[1]
    q_rows = qt_ref[0, 0, 0]
    gq = q_rows.shape[1]
    start = i * qt
    lane_f = _iota((1, qt), 1).astype(F32)
    key_in_chunk = _iota((KEY_CHUNK, qt), 0)
    query_in_tile = _iota((KEY_CHUNK, qt), 1)

    aux_row = _iota((AUX_COLS, gq), 0)
    col_head = _iota((1, gq), 1) // qt
    slope_cols = jnp.zeros((1, gq), F32)
    for r in range(NSA_GROUP):
        slope_cols = jnp.where(col_head == r, _slope(g, r), slope_cols)
    aux_base = jnp.where(aux_row == AUX_SLOPE, slope_cols, 0.0)
    pad_rows = jnp.zeros((ks_ref.shape[3] - q_rows.shape[0] - AUX_COLS, gq), BF16)

    def reset():
        m_ref[...] = jnp.full(m_ref.shape, SOFTMAX_FLOOR, F32)
        l_ref[...] = jnp.zeros(l_ref.shape, F32)
        acc_ref[...] = jnp.zeros(acc_ref.shape, F32)

    def attend(slots):
        scores = [_dot(k, jnp.concatenate([q_rows, aux.astype(BF16), pad_rows], axis=0))
                  for k, aux, _, _, _ in slots]
        v_cat = jnp.concatenate([v for _, _, v, _, _ in slots], axis=1)
        for r in range(NSA_GROUP):
            cols = slice(r * qt, (r + 1) * qt)
            srs, tops = [], []
            for (_, _, _, shift, mask), s in zip(slots, scores):
                sr = s[:, cols] if mask is None else jnp.where(mask, s[:, cols], NEG_INF)
                srs.append(sr)
                tops.append(jnp.max(sr, axis=0, keepdims=True) + shift[r])
            m_old = m_ref[r]
            m_new = functools.reduce(jnp.maximum, tops, m_old)
            alpha = jnp.exp(m_old - m_new)
            ps = [jnp.exp(sr - (m_new - slot[3][r])) for slot, sr in zip(slots, srs)]
            l_ref[r] = alpha * l_ref[r] + functools.reduce(
                jnp.add, [jnp.sum(p, axis=0, keepdims=True) for p in ps])
            p_cat = jnp.concatenate([p.astype(BF16) for p in ps], axis=0)
            acc_ref[:, cols] = alpha * acc_ref[:, cols] + _dot(v_cat, p_cat)
            m_ref[r] = m_new

    def finalize(r):
        l = l_ref[r]
        return acc_ref[:, r * qt:(r + 1) * qt] * jnp.where(l > 0.0, 1.0 / l, 0.0)

    def shifts(dist0, ok):
        rows = [-_slope(g, r) * (dist0 + lane_f) for r in range(NSA_GROUP)]
        return rows if ok is None else [jnp.where(ok, row, NEG_INF) for row in rows]

    reset()
    n_back = WINDOW // KEY_CHUNK
    slots = []
    for j in range(n_back + 1):
        cs = start - WINDOW + j * KEY_CHUNK
        chunk = jnp.maximum(cs, 0) // KEY_CHUNK
        at = pl.multiple_of(chunk * KEY_CHUNK, KEY_CHUNK)
        mask = (query_in_tile < key_in_chunk) if j == 0 else (
            (key_in_chunk <= query_in_tile) if j == n_back else None)
        slots.append((kw_ref[0, 0, pl.ds(at, KEY_CHUNK), :], aux_base, vwt_ref[0, 0, chunk],
                      shifts(F32(WINDOW - j * KEY_CHUNK), cs >= 0), mask))
    attend(slots)
    for r in range(NSA_GROUP):
        ow_ref[:, r * qt:(r + 1) * qt] = finalize(r)

    reset()
    word_base = ((bi * NSA_KV_HEADS + g) * n_tiles + i) * words_per_tile
    list_ref[0] = 0

    def scan_word(w, n):
        word = words_ref[word_base + w]

        def scan_bits(n):
            for bit in range(WORD_BITS):
                c = w * WORD_BITS + bit
                list_ref[n] = c
                n = n + jnp.where(c < i, (word >> bit) & 1, 0)
            return n

        return lax.cond(word != 0, scan_bits, lambda n: n, n)

    n_listed = lax.fori_loop(0, (i + WORD_BITS - 1) // WORD_BITS, scan_word, 0)

    def selected_slot(c, ok, mask):
        at = pl.multiple_of(c * KEY_CHUNK, KEY_CHUNK)
        bias = selt_ref[0, 0, 0, pl.ds(c * BLOCKS_PER_CHUNK, BLOCKS_PER_CHUNK), :]
        aux = aux_base
        for blk in range(BLOCKS_PER_CHUNK):
            aux = jnp.where(aux_row == blk, jnp.concatenate([bias[blk:blk + 1]] * NSA_GROUP, axis=1), aux)
        return (ks_ref[0, 0, pl.ds(at, KEY_CHUNK), :], aux, vst_ref[0, 0, c],
                shifts((start - c * KEY_CHUNK).astype(F32), ok), mask)

    def listed_slot(idx):
        ok = idx < n_listed
        c = jnp.where(ok, list_ref[jnp.minimum(idx, jnp.maximum(n_listed - 1, 0))], 0)
        return selected_slot(c, ok, None)

    attend([selected_slot(i, None, key_in_chunk <= query_in_tile)]
           + [listed_slot(j) for j in range(ATTN_BATCH - 1)])

    def batch(it, carry):
        first = ATTN_BATCH - 1 + it * ATTN_BATCH
        attend([listed_slot(first + j) for j in range(ATTN_BATCH)])
        return carry

    n_rest = jnp.maximum(n_listed - (ATTN_BATCH - 1), 0)
    lax.fori_loop(0, (n_rest + ATTN_BATCH - 1) // ATTN_BATCH, batch, 0)

    outs = []
    for r in range(NSA_GROUP):
        cols = slice(r * qt, (r + 1) * qt)
        col = (g * NSA_GROUP + r) * NSA_BRANCHES
        gate = lambda br: jax.nn.sigmoid(gt_ref[0, pl.ds(col + br, 1), :])
        outs.append(gate(0) * oct_ref[0, 0, 0, :, cols].astype(F32) + gate(1) * finalize(r)
                    + gate(2) * ow_ref[:, cols])
    out_ref[0] = jnp.concatenate(outs, axis=0).T.astype(out_ref.dtype)


def _nsa_slc(words, q_t, ks, vst, kw, vwt, selt, oct, gates_t, words_per_tile):
    b, _, t, kw_cols = ks.shape
    _, _, n_tiles, qrows, qcols = q_t.shape
    n_slc = selt.shape[3]
    n_chunks = vst.shape[2]
    gw = NSA_GROUP * HEAD_DIM
    once = dict(pipeline_mode=pl.Buffered(1))
    k_spec = pl.BlockSpec((1, 1, t, kw_cols), lambda bi, g, i, w: (bi, g, 0, 0), **once)
    vt_spec = pl.BlockSpec((1, 1, n_chunks, HEAD_DIM, KEY_CHUNK), lambda bi, g, i, w: (bi, g, 0, 0, 0), **once)
    tile5 = lambda rows, cols: pl.BlockSpec((1, 1, 1, rows, cols), lambda bi, g, i, w: (bi, g, i, 0, 0))
    grid_spec = pltpu.PrefetchScalarGridSpec(
        num_scalar_prefetch=1,
        grid=(b, NSA_KV_HEADS, n_tiles),
        in_specs=[
            tile5(qrows, qcols),
            k_spec, vt_spec, k_spec, vt_spec,
            tile5(n_slc, Q_TILE),
            tile5(HEAD_DIM, qcols),
            pl.BlockSpec((1, GATE_PAD, Q_TILE), lambda bi, g, i, w: (bi, 0, i)),
        ],
        out_specs=pl.BlockSpec((1, Q_TILE, gw), lambda bi, g, i, w: (bi, i, g)),
        scratch_shapes=[pltpu.VMEM((NSA_GROUP, 1, Q_TILE), F32), pltpu.VMEM((NSA_GROUP, 1, Q_TILE), F32),
                        pltpu.VMEM((HEAD_DIM, qcols), F32), pltpu.VMEM((HEAD_DIM, qcols), F32),
                        pltpu.SMEM((n_chunks,), jnp.int32)],
    )
    return pl.pallas_call(
        functools.partial(_nsa_slc_kernel, words_per_tile=words_per_tile),
        grid_spec=grid_spec,
        out_shape=jax.ShapeDtypeStruct((b, t, NSA_HEADS * HEAD_DIM), BF16),
        compiler_params=_params("parallel", "parallel", "parallel"),
        name="nsa_selected_window",
    )(words, q_t, ks, vst, kw, vwt, selt, oct, gates_t)


POOL_HALO = 16
CONV_HALO = 8


def _merge_kernel(h_ref, g_ref, nsa_ref, pool_ref, pool_halo_ref, conv_ref, conv_halo_ref,
                  wmg_ref, wnsa_ref, pool_bd_ref, pool_scale_ref, wpool_ref, convw_ref, wconv_ref, wo_ref,
                  out_ref, pool_ext, conv_ext, *, seq_len):
    i = pl.program_id(0)
    tm, d = h_ref.shape
    cw = pool_ref.shape[1]
    pos0 = (i * tm) % seq_len
    keep_halo = jnp.where(pos0 == 0, 0.0, 1.0)
    pos = pos0 + _iota((tm, 1), 0)

    u = pool_ref[...]
    pool_ext[0:POOL_HALO, :] = pool_halo_ref[...] * keep_halo
    pool_ext[POOL_HALO:, :] = u
    lane_group = _iota((1, cw), 1) // (cw // len(POOL_WINDOWS))
    total = u
    mean = jnp.zeros_like(u)
    done = 1
    for gi, win in enumerate(POOL_WINDOWS):
        for k in range(done, win):
            total = total + pool_ext[POOL_HALO - k:POOL_HALO - k + tm, :]
        done = win
        cnt = jnp.minimum(pos + 1, win).astype(F32)
        mean = jnp.where(lane_group == gi, total / cnt, mean)
    pooled = (mean - u).astype(BF16)
    mixed = _dot(pooled, pool_bd_ref[...]) * pool_scale_ref[...]
    y_pool = _dot(mixed.astype(BF16), wpool_ref[...])

    ch = conv_ref[:, 0:cw]
    cb = conv_ref[:, cw:2 * cw]
    cc = conv_ref[:, 2 * cw:3 * cw]
    conv_ext[0:CONV_HALO, :] = conv_halo_ref[:, 0:cw] * conv_halo_ref[:, 2 * cw:3 * cw] * keep_halo
    conv_ext[CONV_HALO:, :] = cc * ch
    y = jnp.zeros((tm, cw), F32)
    for k in range(CONV_K):
        off = CONV_HALO - (CONV_K - 1) + k
        y = y + convw_ref[k:k + 1, :] * conv_ext[off:off + tm, :]
    y_conv = _dot((cb * y).astype(BF16), wconv_ref[...])

    y_nsa = _dot(nsa_ref[...], wnsa_ref[...])

    h = h_ref[...]
    xn = _rms_norm(h, g_ref[...]).astype(BF16)
    merged = jnp.zeros((tm, d), F32)
    for br, y_br in enumerate((y_nsa, y_pool, y_conv)):
        mg = jax.nn.sigmoid(_dot(xn, wmg_ref[:, br * d:(br + 1) * d]))
        merged = merged + mg * y_br
    out_ref[...] = h + _dot(merged.astype(BF16), wo_ref[...])


def _merge(h, g, nsa, pool_u, conv, wmg, wnsa, pool_bd, pool_scale, wpool, convw, wconv, wo, seq_len):
    n, d = h.shape
    tm = TOKEN_TILE
    cw = pool_u.shape[1]
    row = lambda width: pl.BlockSpec((tm, width), lambda i: (i, 0))
    full = lambda a: pl.BlockSpec(a.shape, lambda i: (0,) * a.ndim)
    halo = lambda rows, width: pl.BlockSpec(
        (rows, width), lambda i: (jnp.maximum(i * (tm // rows) - 1, 0), 0))
    return pl.pallas_call(
        functools.partial(_merge_kernel, seq_len=seq_len),
        grid=(n // tm,),
        in_specs=[row(d), full(g), row(nsa.shape[1]), row(cw), halo(POOL_HALO, cw),
                  row(conv.shape[1]), halo(CONV_HALO, conv.shape[1]),
                  full(wmg), full(wnsa), full(pool_bd), full(pool_scale), full(wpool), full(convw),
                  full(wconv), full(wo)],
        out_specs=row(d),
        out_shape=jax.ShapeDtypeStruct((n, d), F32),
        scratch_shapes=[pltpu.VMEM((tm + POOL_HALO, cw), F32), pltpu.VMEM((tm + CONV_HALO, cw), F32)],
        compiler_params=_params("parallel"),
        name="merge",
    )(h, g, nsa, pool_u, pool_u, conv, conv, wmg, wnsa, pool_bd, pool_scale, wpool, convw, wconv, wo)


def _route(logits):
    lane = _iota(logits.shape, 1)
    lane_f = lane.astype(F32)
    big = F32(1e9)
    is_group = lane < N_EXPERT_GROUPS
    gl = jnp.where(is_group, logits, NEG_INF)
    g_max = jnp.max(gl, axis=1, keepdims=True)
    g_sel = jnp.min(jnp.where(gl == g_max, lane_f, big), axis=1, keepdims=True)
    g_prob = 1.0 / jnp.sum(jnp.where(is_group, jnp.exp(gl - g_max), 0.0), axis=1, keepdims=True)
    lo = N_EXPERT_GROUPS + EXPERTS_PER_GROUP * g_sel
    in_group = (lane_f >= lo) & (lane_f < lo + EXPERTS_PER_GROUP)
    el = jnp.where(in_group, logits, NEG_INF)
    v1 = jnp.max(el, axis=1, keepdims=True)
    i1 = jnp.min(jnp.where((el == v1) & in_group, lane_f, big), axis=1, keepdims=True)
    el2 = jnp.where(lane_f == i1, NEG_INF, el)
    rest = in_group & (lane_f != i1)
    v2 = jnp.max(el2, axis=1, keepdims=True)
    i2 = jnp.min(jnp.where((el2 == v2) & rest, lane_f, big), axis=1, keepdims=True)
    e2 = jnp.exp(v2 - v1)
    w1 = g_prob / (1.0 + e2)
    w2 = g_prob * e2 / (1.0 + e2)
    return jnp.where(lane_f == i1, w1, 0.0) + jnp.where(lane_f == i2, w2, 0.0)


def _moe_kernel(h_ref, g_ref, wr_ref, br_ref, w1_ref, w3_ref, w2_ref, gf_ref, out_ref, xn_ref, comb_ref,
                *, final_norm):
    e = pl.program_id(1)

    @pl.when(e == 0)
    def _():
        h = h_ref[...]
        xn = _rms_norm(h, g_ref[...])
        logits = jnp.dot(xn, wr_ref[...], preferred_element_type=F32,
                         precision=lax.Precision.HIGHEST) + br_ref[...]
        comb_ref[...] = _route(logits)
        xn_ref[...] = xn.astype(BF16)
        out_ref[...] = h

    xn = xn_ref[...]
    comb = comb_ref[...]
    c_e = jnp.sum(jnp.where(_iota(comb.shape, 1) == N_EXPERT_GROUPS + e, comb, 0.0), axis=1, keepdims=True)
    a = jax.nn.silu(_dot(xn, w1_ref[0])) * _dot(xn, w3_ref[0]) * c_e
    out_ref[...] += _dot(a.astype(BF16), w2_ref[0])

    if final_norm:
        @pl.when(e == pl.num_programs(1) - 1)
        def _():
            out_ref[...] = _rms_norm(out_ref[...], gf_ref[...])


def _moe(h, g, wr, br, w1, w3, w2, gf, final_norm):
    n, d = h.shape
    tm = MOE_TILE
    ne, _, de = w1.shape
    full = lambda a: pl.BlockSpec(a.shape, lambda i, e: (0,) * a.ndim)
    return pl.pallas_call(
        functools.partial(_moe_kernel, final_norm=final_norm),
        grid=(n // tm, ne),
        in_specs=[
            pl.BlockSpec((tm, d), lambda i, e: (i, 0)),
            full(g), full(wr), full(br),
            pl.BlockSpec((1, d, de), lambda i, e: (e, 0, 0)),
            pl.BlockSpec((1, d, de), lambda i, e: (e, 0, 0)),
            pl.BlockSpec((1, de, d), lambda i, e: (e, 0, 0)),
            full(gf),
        ],
        out_specs=pl.BlockSpec((tm, d), lambda i, e: (i, 0)),
        out_shape=jax.ShapeDtypeStruct((n, d), F32),
        scratch_shapes=[pltpu.VMEM((tm, d), BF16), pltpu.VMEM((tm, ROUTER_PAD), F32)],
        compiler_params=_params("parallel", "arbitrary"),
        name="moe",
    )(h, g, wr, br, w1, w3, w2, gf)


def _selection_constants(seq_len):
    ncp = seq_len // CMP_STRIDE
    n_slc = seq_len // SLC_BLOCK
    ratio = SLC_BLOCK // CMP_STRIDE
    lead = CMP_LEN // CMP_STRIDE - 1
    c = np.arange(ncp)[:, None]
    j = np.arange(n_slc)[None, :]
    pool_m = ((c >= ratio * j - lead) & (c < ratio * j + ratio)).astype(np.float32)
    blocks_per_chunk = KEY_CHUNK // SLC_BLOCK
    n_chunks = seq_len // KEY_CHUNK
    pair_m = np.zeros((n_slc, LANES * ((n_chunks + LANES - 1) // LANES)), np.float32)
    pair_m[np.arange(n_slc), np.arange(n_slc) // blocks_per_chunk] = 1.0
    n_words = (n_chunks + WORD_BITS - 1) // WORD_BITS
    bits_m = np.zeros((pair_m.shape[1], LANES), np.float32)
    ch = np.arange(n_chunks)
    bits_m[ch, ch // WORD_BITS] = 2.0 ** (ch % WORD_BITS)
    key_aux = np.zeros((seq_len, LANES - HEAD_DIM), np.float32)
    in_chunk = np.arange(seq_len) % KEY_CHUNK
    key_aux[np.arange(seq_len), in_chunk // SLC_BLOCK] = 1.0
    key_aux[:, AUX_SLOPE] = in_chunk
    as_bf16 = lambda a: jnp.asarray(a, BF16)
    return as_bf16(pool_m.T), as_bf16(pair_m), as_bf16(bits_m), as_bf16(key_aux), n_words


def kernel(x, norm1_g, w_in, cmp_pe, cmp_w1, cmp_w2, w_nsa_proj, pool_w, pool_scale, w_pool_proj, conv_w,
           w_conv_proj, w_o, norm2_g, router_group_w, router_group_b, router_expert_w, router_expert_b,
           expert_w1, expert_w3, expert_w2, final_norm_g):
    b, t, d = x.shape
    n = b * t
    depth = w_in.shape[0]
    dq = NSA_HEADS * HEAD_DIM
    dkv = 6 * NSA_KV_HEADS * HEAD_DIM
    dgate = NSA_HEADS * NSA_BRANCHES
    cw = d // 4
    assert t % TOKEN_TILE == 0 and n % MOE_TILE == 0 and t % Q_TILE == 0
    n_slc = t // SLC_BLOCK
    n_sel = min(SLC_TOPN, n_slc)
    n_chunks16 = t // CMP_STRIDE
    n_tiles = t // Q_TILE
    poolt_m, pair_m, bits_m, key_aux, n_words = _selection_constants(t)
    assert Q_TILE == KEY_CHUNK and BLOCKS_PER_CHUNK < AUX_SLOPE + 1 <= AUX_COLS and KEY_CHUNK <= 256
    assert n_words <= LANES and pair_m.shape[1] == LANES

    h = x.reshape(n, d)
    for l in range(depth):
        wl = w_in[l]
        o_gate = dq + dkv
        o_pool = o_gate + dgate
        o_merge = o_pool + cw + 3 * cw
        w_a = jnp.concatenate([
            wl[:, :dq] * (HEAD_DIM ** -0.5),
            wl[:, dq:o_gate],
            jnp.pad(wl[:, o_gate:o_pool], ((0, 0), (0, GATE_PAD - dgate))),
            wl[:, o_pool:o_merge],
        ], axis=1).astype(BF16)
        wmg = wl[:, o_merge:].astype(BF16)
        pool_bd = jax.scipy.linalg.block_diag(*[pool_w[l, gi] for gi in range(pool_w.shape[1])]).astype(BF16)
        convw = jnp.pad(conv_w[l], ((0, 8 - CONV_K), (0, 0)))
        wr = jnp.pad(jnp.concatenate([router_group_w[l], router_expert_w[l]], axis=1),
                     ((0, 0), (0, ROUTER_PAD - N_EXPERT_GROUPS - N_EXPERTS)))
        br = jnp.pad(jnp.concatenate([router_group_b[l], router_expert_b[l]]),
                     (0, ROUTER_PAD - N_EXPERT_GROUPS - N_EXPERTS))[None, :]
        pe = jnp.broadcast_to(cmp_pe[l].reshape(2, 1, CMP_LEN * HEAD_DIM), (2, 8, CMP_LEN * HEAD_DIM)).astype(BF16)

        q, kv, gates, pool_u, conv = _inproj(h, norm1_g[l][None, :], w_a)
        chunks = kv[:, :2 * NSA_KV_HEADS * HEAD_DIM].reshape(b, n_chunks16, CMP_STRIDE, 2, NSA_KV_HEADS, HEAD_DIM)
        chunks = chunks.transpose(3, 0, 4, 1, 2, 5).reshape(2, b, NSA_KV_HEADS, n_chunks16, CMP_STRIDE * HEAD_DIM)
        kvc = _compress(chunks, pe, cmp_w1[l].astype(BF16), cmp_w2[l].astype(BF16))
        kv3 = kv.reshape(b, t, dkv)
        kvw = NSA_KV_HEADS * HEAD_DIM
        q_t = q.reshape(b, n_tiles, Q_TILE, NSA_KV_HEADS, NSA_GROUP, HEAD_DIM).transpose(0, 3, 1, 5, 4, 2)
        q_t = q_t.reshape(b, NSA_KV_HEADS, n_tiles, HEAD_DIM, NSA_GROUP * Q_TILE)
        per_group = lambda kind: kv3[:, :, kind * kvw:(kind + 1) * kvw].reshape(b, t, NSA_KV_HEADS, HEAD_DIM)
        with_aux = lambda kind: jnp.concatenate(
            [per_group(kind).transpose(0, 2, 1, 3),
             jnp.broadcast_to(key_aux, (b, NSA_KV_HEADS, t, LANES - HEAD_DIM))], axis=-1)
        chunked_t = lambda kind: per_group(kind).reshape(
            b, t // KEY_CHUNK, KEY_CHUNK, NSA_KV_HEADS, HEAD_DIM).transpose(0, 3, 1, 4, 2)
        oc_t, sel_t, words = _nsa_cmp(q_t, kvc[0], kvc[1].transpose(0, 2, 1), poolt_m, pair_m, bits_m, n_sel)
        words = words[:, :, :, 0, :n_words].reshape(-1)
        nsa = _nsa_slc(words, q_t, with_aux(2), chunked_t(3), with_aux(4), chunked_t(5), sel_t, oc_t,
                       gates.reshape(b, t, GATE_PAD).transpose(0, 2, 1), n_words)
        h = _merge(h, norm1_g[l][None, :], nsa.reshape(n, dq), pool_u, conv, wmg,
                   w_nsa_proj[l].astype(BF16), pool_bd, pool_scale[l][None, :], w_pool_proj[l].astype(BF16),
                   convw, w_conv_proj[l].astype(BF16), w_o[l].astype(BF16), t)
        h = _moe(h, norm2_g[l][None, :], wr, br, expert_w1[l].astype(BF16), expert_w3[l].astype(BF16),
                 expert_w2[l].astype(BF16), final_norm_g[None, :], final_norm=(l == depth - 1))
    return h.reshape(b, t, d)
```

```python
import functools

import jax
import jax.numpy as jnp
import numpy as np
from jax import lax
from jax.experimental import pallas as pl
from jax.experimental.pallas import tpu as pltpu

F32 = jnp.float32
BF16 = jnp.bfloat16

HEAD_DIM = 64
NSA_HEADS = 8
NSA_KV_HEADS = 2
NSA_GROUP = NSA_HEADS // NSA_KV_HEADS
CMP_LEN = 32
CMP_STRIDE = 16
CMP_HIDDEN = 4 * HEAD_DIM
SLC_BLOCK = 64
SLC_TOPN = 16
WINDOW = 512
NSA_BRANCHES = 3
POOL_WINDOWS = (2, 4, 8, 16)
CONV_K = 3
N_BRANCHES = 3
N_EXPERT_GROUPS = 4
EXPERTS_PER_GROUP = 8
N_EXPERTS = N_EXPERT_GROUPS * EXPERTS_PER_GROUP
RMS_EPS = 1e-6
NEG_INF = -1e30
FORCE_SCORE = 1e30
ALIBI_SLOPES = tuple(float(2.0 ** (-8.0 * (h + 1) / NSA_HEADS)) for h in range(NSA_HEADS))

LANES = 128
VMEM_LIMIT = 56 * 1024 * 1024
TOKEN_TILE = 512
MOE_TILE = 1024
Q_TILE = 128
KEY_CHUNK = 128
GATE_PAD = LANES
ROUTER_PAD = LANES
WORD_BITS = 16


def _params(*semantics):
    return pltpu.CompilerParams(dimension_semantics=semantics, vmem_limit_bytes=VMEM_LIMIT)


def _dot(a, b):
    return jnp.dot(a, b, preferred_element_type=F32)


def _dot_nt(a, b):
    return lax.dot_general(a, b, (((1,), (1,)), ((), ())), preferred_element_type=F32)


def _rms_norm(x, g):
    y = x * lax.rsqrt(jnp.mean(x * x, axis=-1, keepdims=True) + RMS_EPS)
    return y * g


def _iota(shape, dim):
    return lax.broadcasted_iota(jnp.int32, shape, dim)


def _inproj_kernel(x_ref, g_ref, w_ref, q_ref, kv_ref, gate_ref, pool_ref, conv_ref):
    xn = _rms_norm(x_ref[...], g_ref[...]).astype(BF16)
    col = 0
    for ref in (q_ref, kv_ref, gate_ref, pool_ref, conv_ref):
        width = ref.shape[1]
        ref[...] = _dot(xn, w_ref[:, col:col + width]).astype(ref.dtype)
        col += width


def _inproj(h, g, w):
    n, d = h.shape
    dq = NSA_HEADS * HEAD_DIM
    dkv = 6 * NSA_KV_HEADS * HEAD_DIM
    dpool = d // 4
    dconv = 3 * (d // 4)
    widths = (dq, dkv, GATE_PAD, dpool, dconv)
    dtypes = (BF16, BF16, F32, F32, F32)
    tm = TOKEN_TILE
    return pl.pallas_call(
        _inproj_kernel,
        grid=(n // tm,),
        in_specs=[
            pl.BlockSpec((tm, d), lambda i: (i, 0)),
            pl.BlockSpec((1, d), lambda i: (0, 0)),
            pl.BlockSpec((d, sum(widths)), lambda i: (0, 0)),
        ],
        out_specs=[pl.BlockSpec((tm, wd), lambda i: (i, 0)) for wd in widths],
        out_shape=[jax.ShapeDtypeStruct((n, wd), dt) for wd, dt in zip(widths, dtypes)],
        compiler_params=_params("parallel"),
        name="inproj",
    )(h, g, w)


def _gelu_tanh(x):
    return 0.5 * x * (1.0 + jnp.tanh(0.7978845608028654 * (x + 0.044715 * x * x * x)))


def _compress_kernel(c_ref, pe_ref, w1_ref, w2_ref, out_ref):
    n_chunks = c_ref.shape[3]
    half = w1_ref.shape[1] // 2
    w_top = w1_ref[0, :half, :]
    w_bot = w1_ref[0, half:, :]
    bias = _dot(pe_ref[0], w1_ref[0])[0:1, :]
    row = _iota((n_chunks, 1), 0)
    outs = []
    for g in range(NSA_KV_HEADS):
        c = c_ref[0, 0, g]
        first = _dot(c, w_top)
        second = _dot(c, w_bot)
        hid = first + pltpu.roll(second, n_chunks - 1, 0) + bias
        y = _dot(_gelu_tanh(hid).astype(BF16), w2_ref[0])
        outs.append(jnp.where(row < n_chunks - 1, y, 0.0))
    out_ref[0, 0] = jnp.concatenate(outs, axis=1).astype(BF16)


def _compress(chunks, pe, w1, w2):
    _, b, g, nc, cw = chunks.shape
    return pl.pallas_call(
        _compress_kernel,
        grid=(2, b),
        in_specs=[
            pl.BlockSpec((1, 1, g, nc, cw), lambda k, bi: (k, bi, 0, 0, 0)),
            pl.BlockSpec((1,) + pe.shape[1:], lambda k, bi: (k, 0, 0)),
            pl.BlockSpec((1,) + w1.shape[1:], lambda k, bi: (k, 0, 0)),
            pl.BlockSpec((1,) + w2.shape[1:], lambda k, bi: (k, 0, 0)),
        ],
        out_specs=pl.BlockSpec((1, 1, nc, g * HEAD_DIM), lambda k, bi: (k, bi, 0, 0)),
        out_shape=jax.ShapeDtypeStruct((2, b, nc, g * HEAD_DIM), BF16),
        compiler_params=_params("parallel", "parallel"),
        name="compress",
    )(chunks, pe, w1, w2)


SOFTMAX_FLOOR = -1e29
TAKEN = -3e38
ATTN_BATCH = 4
BLOCKS_PER_CHUNK = KEY_CHUNK // SLC_BLOCK
AUX_COLS = 16
AUX_SLOPE = BLOCKS_PER_CHUNK
CMP_ROWS_STEP = 256
CMP_AUX_SPLIT = 128


def _slope(g, r):
    if isinstance(g, int):
        return jnp.float32(ALIBI_SLOPES[g * NSA_GROUP + r])
    s = jnp.float32(ALIBI_SLOPES[r])
    for gi in range(1, NSA_KV_HEADS):
        s = jnp.where(g == gi, jnp.float32(ALIBI_SLOPES[gi * NSA_GROUP + r]), s)
    return s


def _nsa_cmp_kernel(qt_ref, kc_ref, vct_ref, poolt_ref, pair_ref, bits_ref, oct_ref, selt_ref, words_ref,
                    *, n_sel):
    i = pl.program_id(1)
    qt = selt_ref.shape[4]
    ncp = kc_ref.shape[2]
    n_slc = poolt_ref.shape[0]
    gq = qt_ref.shape[4]
    start = i * qt
    t = start + _iota((1, qt), 1)
    aux_row = _iota((AUX_COLS, gq), 0)
    col_head = _iota((1, gq), 1) // qt
    pad_rows = jnp.zeros((kc_ref.shape[3] - qt_ref.shape[3] - AUX_COLS, gq), BF16)

    def weights(g):
        slope_cols = jnp.zeros((1, gq), F32)
        for r in range(NSA_GROUP):
            slope_cols = jnp.where(col_head == r, _slope(g, r), slope_cols)
        aux = jnp.where(aux_row == 0, slope_cols * (CMP_STRIDE * CMP_AUX_SPLIT),
                        jnp.where(aux_row == 1, slope_cols * CMP_STRIDE, 0.0))
        return jnp.concatenate([qt_ref[0, g, 0], aux.astype(BF16), pad_rows], axis=0)

    def importance(g, nr, nb):
        s = _dot(kc_ref[0, g, :nr, :], weights(g))
        edge = min(nr, 2 * CMP_ROWS_STEP)
        cmp_end = ((nr - edge) + _iota((edge, 1), 0)) * CMP_STRIDE + (CMP_LEN - 1)
        visible = cmp_end <= t
        vct = vct_ref[0, g * HEAD_DIM:(g + 1) * HEAD_DIM, :nr]
        psum = jnp.zeros((nr, qt), F32)
        for r in range(NSA_GROUP):
            cols = slice(r * qt, (r + 1) * qt)
            sr = s[:, cols]
            tail = jnp.where(visible, sr[nr - edge:], NEG_INF)
            sr = tail if edge == nr else jnp.concatenate([sr[:nr - edge], tail], axis=0)
            m = jnp.maximum(jnp.max(sr, axis=0, keepdims=True), SOFTMAX_FLOOR)
            e = jnp.exp(sr - m)
            l = jnp.sum(e, axis=0, keepdims=True)
            inv = jnp.where(l > 0.0, 1.0 / l, 0.0)
            oct_ref[0, g, 0, :, cols] = (_dot(vct, e.astype(BF16)) * inv).astype(BF16)
            psum = psum + e * inv
        return _dot(poolt_ref[:nb, :nr], psum.astype(BF16))

    def visible_prefix(nr):
        nb = min(n_slc, nr * CMP_STRIDE // SLC_BLOCK)
        imp = jnp.concatenate([importance(g, nr, nb) for g in range(NSA_KV_HEADS)], axis=1)
        blk = _iota((nb, 1), 0)
        cur = jnp.concatenate([t // SLC_BLOCK] * NSA_KV_HEADS, axis=1)
        forced = (blk == 0) | (blk == cur) | (blk == cur - 1)
        score = jnp.where(forced, TAKEN, jnp.where(blk <= cur, imp, NEG_INF))
        n_forced = 1 + jnp.where(cur >= 1, 1, 0) + jnp.where(cur >= 2, 1, 0)
        blk_f = blk.astype(F32)

        def take_one(score, active):
            m = jnp.max(score, axis=0, keepdims=True)
            first = jnp.min(jnp.where(score == m, blk_f, F32(1e9)), axis=0, keepdims=True)
            hit = (blk_f == first) if active is None else ((blk_f == first) & active)
            return jnp.where(hit, TAKEN, score)

        common_rounds = max(n_sel - 3, 0)
        for _ in range(common_rounds):
            score = take_one(score, None)

        def early_rounds(score):
            for k in range(common_rounds, n_sel - 1):
                score = take_one(score, n_sel - n_forced > k)
            return score

        score = lax.cond(start < 2 * SLC_BLOCK, early_rounds, lambda sc: sc, score)
        for g in range(NSA_KV_HEADS):
            sel_g = score[:, g * qt:(g + 1) * qt] == TAKEN
            selt_ref[0, g, 0, :nb, :] = jnp.where(sel_g, 0.0, NEG_INF)
            if nb < n_slc:
                selt_ref[0, g, 0, nb:, :] = jnp.full((n_slc - nb, qt), NEG_INF, F32)
            count = _dot_nt(jnp.ones((8, qt), BF16), jnp.where(sel_g, 1.0, 0.0).astype(BF16))
            used = jnp.where(count > 0.0, 1.0, 0.0).astype(BF16)
            chunk_used = jnp.where(_dot(used, pair_ref[:nb, :]) > 0.0, 1.0, 0.0).astype(BF16)
            words_ref[0, g, 0] = _dot(chunk_used, bits_ref[...]).astype(jnp.int32)

    step = min(CMP_ROWS_STEP, ncp)
    rows_needed = jnp.minimum((start + qt - CMP_LEN) // CMP_STRIDE + 1, ncp)
    n_steps = (rows_needed + step - 1) // step
    for k in range(ncp // step):
        pl.when(n_steps == k + 1)(functools.partial(visible_prefix, (k + 1) * step))


def _nsa_cmp(q_t, kc, vct, poolt_m, pair_m, bits_m, n_sel):
    b, ng, n_tiles, qrows, qcols = q_t.shape
    ncp = kc.shape[2]
    n_slc = poolt_m.shape[0]
    const = lambda shape: pl.BlockSpec(shape, lambda bi, i: (0,) * len(shape))
    tile5 = lambda rows, cols: pl.BlockSpec((1, ng, 1, rows, cols), lambda bi, i: (bi, 0, i, 0, 0))
    return pl.pallas_call(
        functools.partial(_nsa_cmp_kernel, n_sel=n_sel),
        grid=(b, n_tiles),
        in_specs=[
            tile5(qrows, qcols),
            pl.BlockSpec((1, ng, ncp, kc.shape[3]), lambda bi, i: (bi, 0, 0, 0)),
            pl.BlockSpec((1, ng * HEAD_DIM, ncp), lambda bi, i: (bi, 0, 0)),
            const(poolt_m.shape), const(pair_m.shape), const(bits_m.shape),
        ],
        out_specs=[tile5(HEAD_DIM, qcols), tile5(n_slc, Q_TILE), tile5(8, LANES)],
        out_shape=[
            jax.ShapeDtypeStruct((b, ng, n_tiles, HEAD_DIM, qcols), BF16),
            jax.ShapeDtypeStruct((b, ng, n_tiles, n_slc, Q_TILE), F32),
            jax.ShapeDtypeStruct((b, ng, n_tiles, 8, LANES), jnp.int32),
        ],
        compiler_params=_params("parallel", "parallel"),
        name="nsa_compressed",
    )(q_t, kc, vct, poolt_m, pair_m, bits_m)


def _nsa_slc_kernel(words_ref, qt_ref, ks_ref, vst_ref, kw_ref, vwt_ref, selt_ref, oct_ref, gt_ref,
                    out_ref, m_ref, l_ref, acc_ref, ow_ref, list_ref, *, words_per_tile):
    bi = pl.program_id(0)
    g = pl.program_id(1)
    i = pl.program_id(2)
    n_tiles = pl.num_programs(2)
    qt = out_ref.shape[1]
    q_rows = qt_ref[0, 0, 0]
    gq = q_rows.shape[1]
    start = i * qt
    lane_f = _iota((1, qt), 1).astype(F32)
    key_in_chunk = _iota((KEY_CHUNK, qt), 0)
    query_in_tile = _iota((KEY_CHUNK, qt), 1)

    aux_row = _iota((AUX_COLS, gq), 0)
    col_head = _iota((1, gq), 1) // qt
    slope_cols = jnp.zeros((1, gq), F32)
    for r in range(NSA_GROUP):
        slope_cols = jnp.where(col_head == r, _slope(g, r), slope_cols)
    aux_base = jnp.where(aux_row == AUX_SLOPE, slope_cols, 0.0)
    pad_rows = jnp.zeros((ks_ref.shape[3] - q_rows.shape[0] - AUX_COLS, gq), BF16)

    def reset():
        m_ref[...] = jnp.full(m_ref.shape, SOFTMAX_FLOOR, F32)
        l_ref[...] = jnp.zeros(l_ref.shape, F32)
        acc_ref[...] = jnp.zeros(acc_ref.shape, F32)

    def attend(slots):
        scores = [_dot(k, jnp.concatenate([q_rows, aux.astype(BF16), pad_rows], axis=0))
                  for k, aux, _, _, _ in slots]
        v_cat = jnp.concatenate([v for _, _, v, _, _ in slots], axis=1)
        for r in range(NSA_GROUP):
            cols = slice(r * qt, (r + 1) * qt)
            srs, tops = [], []
            for (_, _, _, shift, mask), s in zip(slots, scores):
                sr = s[:, cols] if mask is None else jnp.where(mask, s[:, cols], NEG_INF)
                srs.append(sr)
                tops.append(jnp.max(sr, axis=0, keepdims=True) + shift[r])
            m_old = m_ref[r]
            m_new = functools.reduce(jnp.maximum, tops, m_old)
            alpha = jnp.exp(m_old - m_new)
            ps = [jnp.exp(sr - (m_new - slot[3][r])) for slot, sr in zip(slots, srs)]
            l_ref[r] = alpha * l_ref[r] + functools.reduce(
                jnp.add, [jnp.sum(p, axis=0, keepdims=True) for p in ps])
            p_cat = jnp.concatenate([p.astype(BF16) for p in ps], axis=0)
            acc_ref[:, cols] = alpha * acc_ref[:, cols] + _dot(v_cat, p_cat)
            m_ref[r] = m_new

    def finalize(r):
        l = l_ref[r]
        return acc_ref[:, r * qt:(r + 1) * qt] * jnp.where(l > 0.0, 1.0 / l, 0.0)

    def shifts(dist0, ok):
        rows = [-_slope(g, r) * (dist0 + lane_f) for r in range(NSA_GROUP)]
        return rows if ok is None else [jnp.where(ok, row, NEG_INF) for row in rows]

    reset()
    n_back = WINDOW // KEY_CHUNK
    slots = []
    for j in range(n_back + 1):
        cs = start - WINDOW + j * KEY_CHUNK
        chunk = jnp.maximum(cs, 0) // KEY_CHUNK
        at = pl.multiple_of(chunk * KEY_CHUNK, KEY_CHUNK)
        mask = (query_in_tile < key_in_chunk) if j == 0 else (
            (key_in_chunk <= query_in_tile) if j == n_back else None)
        slots.append((kw_ref[0, 0, pl.ds(at, KEY_CHUNK), :], aux_base, vwt_ref[0, 0, chunk],
                      shifts(F32(WINDOW - j * KEY_CHUNK), cs >= 0), mask))
    attend(slots)
    for r in range(NSA_GROUP):
        ow_ref[:, r * qt:(r + 1) * qt] = finalize(r)

    reset()
    word_base = ((bi * NSA_KV_HEADS + g) * n_tiles + i) * words_per_tile
    list_ref[0] = 0

    def scan_word(w, n):
        word = words_ref[word_base + w]

        def scan_bits(n):
            for bit in range(WORD_BITS):
                c = w * WORD_BITS + bit
                list_ref[n] = c
                n = n + jnp.where(c < i, (word >> bit) & 1, 0)
            return n

        return lax.cond(word != 0, scan_bits, lambda n: n, n)

    n_listed = lax.fori_loop(0, (i + WORD_BITS - 1) // WORD_BITS, scan_word, 0)

    def selected_slot(c, ok, mask):
        at = pl.multiple_of(c * KEY_CHUNK, KEY_CHUNK)
        bias = selt_ref[0, 0, 0, pl.ds(c * BLOCKS_PER_CHUNK, BLOCKS_PER_CHUNK), :]
        aux = aux_base
        for blk in range(BLOCKS_PER_CHUNK):
            aux = jnp.where(aux_row == blk, jnp.concatenate([bias[blk:blk + 1]] * NSA_GROUP, axis=1), aux)
        return (ks_ref[0, 0, pl.ds(at, KEY_CHUNK), :], aux, vst_ref[0, 0, c],
                shifts((start - c * KEY_CHUNK).astype(F32), ok), mask)

    def listed_slot(idx):
        ok = idx < n_listed
        c = jnp.where(ok, list_ref[jnp.minimum(idx, jnp.maximum(n_listed - 1, 0))], 0)
        return selected_slot(c, ok, None)

    attend([selected_slot(i, None, key_in_chunk <= query_in_tile)]
           + [listed_slot(j) for j in range(ATTN_BATCH - 1)])

    def batch(it, carry):
        first = ATTN_BATCH - 1 + it * ATTN_BATCH
        attend([listed_slot(first + j) for j in range(ATTN_BATCH)])
        return carry

    n_rest = jnp.maximum(n_listed - (ATTN_BATCH - 1), 0)
    lax.fori_loop(0, (n_rest + ATTN_BATCH - 1) // ATTN_BATCH, batch, 0)

    outs = []
    for r in range(NSA_GROUP):
        cols = slice(r * qt, (r + 1) * qt)
        col = (g * NSA_GROUP + r) * NSA_BRANCHES
        gate = lambda br: jax.nn.sigmoid(gt_ref[0, pl.ds(col + br, 1), :])
        outs.append(gate(0) * oct_ref[0, 0, 0, :, cols].astype(F32) + gate(1) * finalize(r)
                    + gate(2) * ow_ref[:, cols])
    out_ref[0] = jnp.concatenate(outs, axis=0).T.astype(out_ref.dtype)


def _nsa_slc(words, q_t, ks, vst, kw, vwt, selt, oct, gates_t, words_per_tile):
    b, _, t, kw_cols = ks.shape
    _, _, n_tiles, qrows, qcols = q_t.shape
    n_slc = selt.shape[3]
    n_chunks = vst.shape[2]
    gw = NSA_GROUP * HEAD_DIM
    once = dict(pipeline_mode=pl.Buffered(1))
    k_spec = pl.BlockSpec((1, 1, t, kw_cols), lambda bi, g, i, w: (bi, g, 0, 0), **once)
    vt_spec = pl.BlockSpec((1, 1, n_chunks, HEAD_DIM, KEY_CHUNK), lambda bi, g, i, w: (bi, g, 0, 0, 0), **once)
    tile5 = lambda rows, cols: pl.BlockSpec((1, 1, 1, rows, cols), lambda bi, g, i, w: (bi, g, i, 0, 0))
    grid_spec = pltpu.PrefetchScalarGridSpec(
        num_scalar_prefetch=1,
        grid=(b, NSA_KV_HEADS, n_tiles),
        in_specs=[
            tile5(qrows, qcols),
            k_spec, vt_spec, k_spec, vt_spec,
            tile5(n_slc, Q_TILE),
            tile5(HEAD_DIM, qcols),
            pl.BlockSpec((1, GATE_PAD, Q_TILE), lambda bi, g, i, w: (bi, 0, i)),
        ],
        out_specs=pl.BlockSpec((1, Q_TILE, gw), lambda bi, g, i, w: (bi, i, g)),
        scratch_shapes=[pltpu.VMEM((NSA_GROUP, 1, Q_TILE), F32), pltpu.VMEM((NSA_GROUP, 1, Q_TILE), F32),
                        pltpu.VMEM((HEAD_DIM, qcols), F32), pltpu.VMEM((HEAD_DIM, qcols), F32),
                        pltpu.SMEM((n_chunks,), jnp.int32)],
    )
    return pl.pallas_call(
        functools.partial(_nsa_slc_kernel, words_per_tile=words_per_tile),
        grid_spec=grid_spec,
        out_shape=jax.ShapeDtypeStruct((b, t, NSA_HEADS * HEAD_DIM), BF16),
        compiler_params=_params("parallel", "parallel", "parallel"),
        name="nsa_selected_window",
    )(words, q_t, ks, vst, kw, vwt, selt, oct, gates_t)


POOL_HALO = 16
CONV_HALO = 8


def _merge_kernel(h_ref, g_ref, nsa_ref, pool_ref, pool_halo_ref, conv_ref, conv_halo_ref,
                  wmg_ref, wnsa_ref, pool_bd_ref, pool_scale_ref, wpool_ref, convw_ref, wconv_ref, wo_ref,
                  out_ref, pool_ext, conv_ext, *, seq_len):
    i = pl.program_id(0)
    tm, d = h_ref.shape
    cw = pool_ref.shape[1]
    pos0 = (i * tm) % seq_len
    keep_halo = jnp.where(pos0 == 0, 0.0, 1.0)
    pos = pos0 + _iota((tm, 1), 0)

    u = pool_ref[...]
    pool_ext[0:POOL_HALO, :] = pool_halo_ref[...] * keep_halo
    pool_ext[POOL_HALO:, :] = u
    lane_group = _iota((1, cw), 1) // (cw // len(POOL_WINDOWS))
    total = u
    mean = jnp.zeros_like(u)
    done = 1
    for gi, win in enumerate(POOL_WINDOWS):
        for k in range(done, win):
            total = total + pool_ext[POOL_HALO - k:POOL_HALO - k + tm, :]
        done = win
        cnt = jnp.minimum(pos + 1, win).astype(F32)
        mean = jnp.where(lane_group == gi, total / cnt, mean)
    pooled = (mean - u).astype(BF16)
    mixed = _dot(pooled, pool_bd_ref[...]) * pool_scale_ref[...]
    y_pool = _dot(mixed.astype(BF16), wpool_ref[...])

    ch = conv_ref[:, 0:cw]
    cb = conv_ref[:, cw:2 * cw]
    cc = conv_ref[:, 2 * cw:3 * cw]
    conv_ext[0:CONV_HALO, :] = conv_halo_ref[:, 0:cw] * conv_halo_ref[:, 2 * cw:3 * cw] * keep_halo
    conv_ext[CONV_HALO:, :] = cc * ch
    y = jnp.zeros((tm, cw), F32)
    for k in range(CONV_K):
        off = CONV_HALO - (CONV_K - 1) + k
        y = y + convw_ref[k:k + 1, :] * conv_ext[off:off + tm, :]
    y_conv = _dot((cb * y).astype(BF16), wconv_ref[...])

    y_nsa = _dot(nsa_ref[...], wnsa_ref[...])

    h = h_ref[...]
    xn = _rms_norm(h, g_ref[...]).astype(BF16)
    merged = jnp.zeros((tm, d), F32)
    for br, y_br in enumerate((y_nsa, y_pool, y_conv)):
        mg = jax.nn.sigmoid(_dot(xn, wmg_ref[:, br * d:(br + 1) * d]))
        merged = merged + mg * y_br
    out_ref[...] = h + _dot(merged.astype(BF16), wo_ref[...])


def _merge(h, g, nsa, pool_u, conv, wmg, wnsa, pool_bd, pool_scale, wpool, convw, wconv, wo, seq_len):
    n, d = h.shape
    tm = TOKEN_TILE
    cw = pool_u.shape[1]
    row = lambda width: pl.BlockSpec((tm, width), lambda i: (i, 0))
    full = lambda a: pl.BlockSpec(a.shape, lambda i: (0,) * a.ndim)
    halo = lambda rows, width: pl.BlockSpec(
        (rows, width), lambda i: (jnp.maximum(i * (tm // rows) - 1, 0), 0))
    return pl.pallas_call(
        functools.partial(_merge_kernel, seq_len=seq_len),
        grid=(n // tm,),
        in_specs=[row(d), full(g), row(nsa.shape[1]), row(cw), halo(POOL_HALO, cw),
                  row(conv.shape[1]), halo(CONV_HALO, conv.shape[1]),
                  full(wmg), full(wnsa), full(pool_bd), full(pool_scale), full(wpool), full(convw),
                  full(wconv), full(wo)],
        out_specs=row(d),
        out_shape=jax.ShapeDtypeStruct((n, d), F32),
        scratch_shapes=[pltpu.VMEM((tm + POOL_HALO, cw), F32), pltpu.VMEM((tm + CONV_HALO, cw), F32)],
        compiler_params=_params("parallel"),
        name="merge",
    )(h, g, nsa, pool_u, pool_u, conv, conv, wmg, wnsa, pool_bd, pool_scale, wpool, convw, wconv, wo)


def _route(logits):
    lane = _iota(logits.shape, 1)
    lane_f = lane.astype(F32)
    big = F32(1e9)
    is_group = lane < N_EXPERT_GROUPS
    gl = jnp.where(is_group, logits, NEG_INF)
    g_max = jnp.max(gl, axis=1, keepdims=True)
    g_sel = jnp.min(jnp.where(gl == g_max, lane_f, big), axis=1, keepdims=True)
    g_prob = 1.0 / jnp.sum(jnp.where(is_group, jnp.exp(gl - g_max), 0.0), axis=1, keepdims=True)
    lo = N_EXPERT_GROUPS + EXPERTS_PER_GROUP * g_sel
    in_group = (lane_f >= lo) & (lane_f < lo + EXPERTS_PER_GROUP)
    el = jnp.where(in_group, logits, NEG_INF)
    v1 = jnp.max(el, axis=1, keepdims=True)
    i1 = jnp.min(jnp.where((el == v1) & in_group, lane_f, big), axis=1, keepdims=True)
    el2 = jnp.where(lane_f == i1, NEG_INF, el)
    rest = in_group & (lane_f != i1)
    v2 = jnp.max(el2, axis=1, keepdims=True)
    i2 = jnp.min(jnp.where((el2 == v2) & rest, lane_f, big), axis=1, keepdims=True)
    e2 = jnp.exp(v2 - v1)
    w1 = g_prob / (1.0 + e2)
    w2 = g_prob * e2 / (1.0 + e2)
    return jnp.where(lane_f == i1, w1, 0.0) + jnp.where(lane_f == i2, w2, 0.0)


def _moe_kernel(h_ref, g_ref, wr_ref, br_ref, w1_ref, w3_ref, w2_ref, gf_ref, out_ref, xn_ref, comb_ref,
                *, final_norm):
    e = pl.program_id(1)

    @pl.when(e == 0)
    def _():
        h = h_ref[...]
        xn = _rms_norm(h, g_ref[...])
        logits = jnp.dot(xn, wr_ref[...], preferred_element_type=F32,
                         precision=lax.Precision.HIGHEST) + br_ref[...]
        comb_ref[...] = _route(logits)
        xn_ref[...] = xn.astype(BF16)
        out_ref[...] = h

    xn = xn_ref[...]
    comb = comb_ref[...]
    c_e = jnp.sum(jnp.where(_iota(comb.shape, 1) == N_EXPERT_GROUPS + e, comb, 0.0), axis=1, keepdims=True)
    a = jax.nn.silu(_dot(xn, w1_ref[0])) * _dot(xn, w3_ref[0]) * c_e
    out_ref[...] += _dot(a.astype(BF16), w2_ref[0])

    if final_norm:
        @pl.when(e == pl.num_programs(1) - 1)
        def _():
            out_ref[...] = _rms_norm(out_ref[...], gf_ref[...])


def _moe(h, g, wr, br, w1, w3, w2, gf, final_norm):
    n, d = h.shape
    tm = MOE_TILE
    ne, _, de = w1.shape
    full = lambda a: pl.BlockSpec(a.shape, lambda i, e: (0,) * a.ndim)
    return pl.pallas_call(
        functools.partial(_moe_kernel, final_norm=final_norm),
        grid=(n // tm, ne),
        in_specs=[
            pl.BlockSpec((tm, d), lambda i, e: (i, 0)),
            full(g), full(wr), full(br),
            pl.BlockSpec((1, d, de), lambda i, e: (e, 0, 0)),
            pl.BlockSpec((1, d, de), lambda i, e: (e, 0, 0)),
            pl.BlockSpec((1, de, d), lambda i, e: (e, 0, 0)),
            full(gf),
        ],
        out_specs=pl.BlockSpec((tm, d), lambda i, e: (i, 0)),
        out_shape=jax.ShapeDtypeStruct((n, d), F32),
        scratch_shapes=[pltpu.VMEM((tm, d), BF16), pltpu.VMEM((tm, ROUTER_PAD), F32)],
        compiler_params=_params("parallel", "arbitrary"),
        name="moe",
    )(h, g, wr, br, w1, w3, w2, gf)


def _selection_constants(seq_len):
    ncp = seq_len // CMP_STRIDE
    n_slc = seq_len // SLC_BLOCK
    ratio = SLC_BLOCK // CMP_STRIDE
    lead = CMP_LEN // CMP_STRIDE - 1
    c = np.arange(ncp)[:, None]
    j = np.arange(n_slc)[None, :]
    pool_m = ((c >= ratio * j - lead) & (c < ratio * j + ratio)).astype(np.float32)
    blocks_per_chunk = KEY_CHUNK // SLC_BLOCK
    n_chunks = seq_len // KEY_CHUNK
    pair_m = np.zeros((n_slc, LANES * ((n_chunks + LANES - 1) // LANES)), np.float32)
    pair_m[np.arange(n_slc), np.arange(n_slc) // blocks_per_chunk] = 1.0
    n_words = (n_chunks + WORD_BITS - 1) // WORD_BITS
    bits_m = np.zeros((pair_m.shape[1], LANES), np.float32)
    ch = np.arange(n_chunks)
    bits_m[ch, ch // WORD_BITS] = 2.0 ** (ch % WORD_BITS)
    key_aux = np.zeros((seq_len, LANES - HEAD_DIM), np.float32)
    in_chunk = np.arange(seq_len) % KEY_CHUNK
    key_aux[np.arange(seq_len), in_chunk // SLC_BLOCK] = 1.0
    key_aux[:, AUX_SLOPE] = in_chunk
    cmp_aux = np.zeros((ncp, LANES - HEAD_DIM), np.float32)
    cmp_aux[:, 0] = np.arange(ncp) // CMP_AUX_SPLIT
    cmp_aux[:, 1] = np.arange(ncp) % CMP_AUX_SPLIT
    as_bf16 = lambda a: jnp.asarray(a, BF16)
    return as_bf16(pool_m.T), as_bf16(pair_m), as_bf16(bits_m), as_bf16(key_aux), as_bf16(cmp_aux), n_words


def kernel(x, norm1_g, w_in, cmp_pe, cmp_w1, cmp_w2, w_nsa_proj, pool_w, pool_scale, w_pool_proj, conv_w,
           w_conv_proj, w_o, norm2_g, router_group_w, router_group_b, router_expert_w, router_expert_b,
           expert_w1, expert_w3, expert_w2, final_norm_g):
    b, t, d = x.shape
    n = b * t
    depth = w_in.shape[0]
    dq = NSA_HEADS * HEAD_DIM
    dkv = 6 * NSA_KV_HEADS * HEAD_DIM
    dgate = NSA_HEADS * NSA_BRANCHES
    cw = d // 4
    assert t % TOKEN_TILE == 0 and n % MOE_TILE == 0 and t % Q_TILE == 0
    n_slc = t // SLC_BLOCK
    n_sel = min(SLC_TOPN, n_slc)
    n_chunks16 = t // CMP_STRIDE
    n_tiles = t // Q_TILE
    poolt_m, pair_m, bits_m, key_aux, cmp_aux, n_words = _selection_constants(t)
    assert Q_TILE == KEY_CHUNK and AUX_SLOPE < AUX_COLS and KEY_CHUNK <= 256
    assert n_chunks16 <= 256 * CMP_AUX_SPLIT and n_chunks16 % min(CMP_ROWS_STEP, n_chunks16) == 0
    assert n_words <= LANES and pair_m.shape[1] == LANES

    h = x.reshape(n, d)
    for l in range(depth):
        wl = w_in[l]
        o_gate = dq + dkv
        o_pool = o_gate + dgate
        o_merge = o_pool + cw + 3 * cw
        w_a = jnp.concatenate([
            wl[:, :dq] * (HEAD_DIM ** -0.5),
            wl[:, dq:o_gate],
            jnp.pad(wl[:, o_gate:o_pool], ((0, 0), (0, GATE_PAD - dgate))),
            wl[:, o_pool:o_merge],
        ], axis=1).astype(BF16)
        wmg = wl[:, o_merge:].astype(BF16)
        pool_bd = jax.scipy.linalg.block_diag(*[pool_w[l, gi] for gi in range(pool_w.shape[1])]).astype(BF16)
        convw = jnp.pad(conv_w[l], ((0, 8 - CONV_K), (0, 0)))
        wr = jnp.pad(jnp.concatenate([router_group_w[l], router_expert_w[l]], axis=1),
                     ((0, 0), (0, ROUTER_PAD - N_EXPERT_GROUPS - N_EXPERTS)))
        br = jnp.pad(jnp.concatenate([router_group_b[l], router_expert_b[l]]),
                     (0, ROUTER_PAD - N_EXPERT_GROUPS - N_EXPERTS))[None, :]
        pe = jnp.broadcast_to(cmp_pe[l].reshape(2, 1, CMP_LEN * HEAD_DIM), (2, 8, CMP_LEN * HEAD_DIM)).astype(BF16)

        q, kv, gates, pool_u, conv = _inproj(h, norm1_g[l][None, :], w_a)
        chunks = kv[:, :2 * NSA_KV_HEADS * HEAD_DIM].reshape(b, n_chunks16, CMP_STRIDE, 2, NSA_KV_HEADS, HEAD_DIM)
        chunks = chunks.transpose(3, 0, 4, 1, 2, 5).reshape(2, b, NSA_KV_HEADS, n_chunks16, CMP_STRIDE * HEAD_DIM)
        kvc = _compress(chunks, pe, cmp_w1[l].astype(BF16), cmp_w2[l].astype(BF16))
        kv3 = kv.reshape(b, t, dkv)
        kvw = NSA_KV_HEADS * HEAD_DIM
        q_t = q.reshape(b, n_tiles, Q_TILE, NSA_KV_HEADS, NSA_GROUP, HEAD_DIM).transpose(0, 3, 1, 5, 4, 2)
        q_t = q_t.reshape(b, NSA_KV_HEADS, n_tiles, HEAD_DIM, NSA_GROUP * Q_TILE)
        per_group = lambda kind: kv3[:, :, kind * kvw:(kind + 1) * kvw].reshape(b, t, NSA_KV_HEADS, HEAD_DIM)
        with_aux = lambda kind: jnp.concatenate(
            [per_group(kind).transpose(0, 2, 1, 3),
             jnp.broadcast_to(key_aux, (b, NSA_KV_HEADS, t, LANES - HEAD_DIM))], axis=-1)
        chunked_t = lambda kind: per_group(kind).reshape(
            b, t // KEY_CHUNK, KEY_CHUNK, NSA_KV_HEADS, HEAD_DIM).transpose(0, 3, 1, 4, 2)
        kc_aux = jnp.concatenate(
            [kvc[0].reshape(b, n_chunks16, NSA_KV_HEADS, HEAD_DIM).transpose(0, 2, 1, 3),
             jnp.broadcast_to(cmp_aux, (b, NSA_KV_HEADS, n_chunks16, LANES - HEAD_DIM))], axis=-1)
        oc_t, sel_t, words = _nsa_cmp(q_t, kc_aux, kvc[1].transpose(0, 2, 1), poolt_m, pair_m, bits_m, n_sel)
        words = words[:, :, :, 0, :n_words].reshape(-1)
        nsa = _nsa_slc(words, q_t, with_aux(2), chunked_t(3), with_aux(4), chunked_t(5), sel_t, oc_t,
                       gates.reshape(b, t, GATE_PAD).transpose(0, 2, 1), n_words)
        h = _merge(h, norm1_g[l][None, :], nsa.reshape(n, dq), pool_u, conv, wmg,
                   w_nsa_proj[l].astype(BF16), pool_bd, pool_scale[l][None, :], w_pool_proj[l].astype(BF16),
                   convw, w_conv_proj[l].astype(BF16), w_o[l].astype(BF16), t)
        h = _moe(h, norm2_g[l][None, :], wr, br, expert_w1[l].astype(BF16), expert_w3[l].astype(BF16),
                 expert_w2[l].astype(BF16), final_norm_g[None, :], final_norm=(l == depth - 1))
    return h.reshape(b, t, d)
```

```python
import functools

import jax
import jax.numpy as jnp
import numpy as np
from jax import lax
from jax.experimental import pallas as pl
from jax.experimental.pallas import tpu as pltpu

F32 = jnp.float32
BF16 = jnp.bfloat16

HEAD_DIM = 64
NSA_HEADS = 8
NSA_KV_HEADS = 2
NSA_GROUP = NSA_HEADS // NSA_KV_HEADS
CMP_LEN = 32
CMP_STRIDE = 16
CMP_HIDDEN = 4 * HEAD_DIM
SLC_BLOCK = 64
SLC_TOPN = 16
WINDOW = 512
NSA_BRANCHES = 3
POOL_WINDOWS = (2, 4, 8, 16)
CONV_K = 3
N_BRANCHES = 3
N_EXPERT_GROUPS = 4
EXPERTS_PER_GROUP = 8
N_EXPERTS = N_EXPERT_GROUPS * EXPERTS_PER_GROUP
RMS_EPS = 1e-6
NEG_INF = -1e30
FORCE_SCORE = 1e30
ALIBI_SLOPES = tuple(float(2.0 ** (-8.0 * (h + 1) / NSA_HEADS)) for h in range(NSA_HEADS))

LANES = 128
VMEM_LIMIT = 56 * 1024 * 1024
TOKEN_TILE = 512
MOE_TILE = 1024
MOE_SORT_TILE = 512
Q_TILE = 128
KEY_CHUNK = 128
GATE_PAD = LANES
ROUTER_PAD = LANES
WORD_BITS = 16


def _params(*semantics):
    return pltpu.CompilerParams(dimension_semantics=semantics, vmem_limit_bytes=VMEM_LIMIT)


def _dot(a, b):
    return jnp.dot(a, b, preferred_element_type=F32)


def _dot_nt(a, b):
    return lax.dot_general(a, b, (((1,), (1,)), ((), ())), preferred_element_type=F32)


def _rms_norm(x, g):
    y = x * lax.rsqrt(jnp.mean(x * x, axis=-1, keepdims=True) + RMS_EPS)
    return y * g


def _iota(shape, dim):
    return lax.broadcasted_iota(jnp.int32, shape, dim)


def _inproj_kernel(x_ref, g_ref, w_ref, q_ref, kv_ref, gate_ref, pool_ref, conv_ref):
    xn = _rms_norm(x_ref[...], g_ref[...]).astype(BF16)
    col = 0
    for ref in (q_ref, kv_ref, gate_ref, pool_ref, conv_ref):
        width = ref.shape[1]
        ref[...] = _dot(xn, w_ref[:, col:col + width]).astype(ref.dtype)
        col += width


def _inproj(h, g, w):
    n, d = h.shape
    dq = NSA_HEADS * HEAD_DIM
    dkv = 6 * NSA_KV_HEADS * HEAD_DIM
    dpool = d // 4
    dconv = 3 * (d // 4)
    widths = (dq, dkv, GATE_PAD, dpool, dconv)
    dtypes = (BF16, BF16, F32, F32, F32)
    tm = TOKEN_TILE
    return pl.pallas_call(
        _inproj_kernel,
        grid=(n // tm,),
        in_specs=[
            pl.BlockSpec((tm, d), lambda i: (i, 0)),
            pl.BlockSpec((1, d), lambda i: (0, 0)),
            pl.BlockSpec((d, sum(widths)), lambda i: (0, 0)),
        ],
        out_specs=[pl.BlockSpec((tm, wd), lambda i: (i, 0)) for wd in widths],
        out_shape=[jax.ShapeDtypeStruct((n, wd), dt) for wd, dt in zip(widths, dtypes)],
        compiler_params=_params("parallel"),
        name="inproj",
    )(h, g, w)


def _gelu_tanh(x):
    return 0.5 * x * (1.0 + jnp.tanh(0.7978845608028654 * (x + 0.044715 * x * x * x)))


def _compress_kernel(c_ref, pe_ref, w1_ref, w2_ref, out_ref):
    n_chunks = c_ref.shape[3]
    half = w1_ref.shape[1] // 2
    w_top = w1_ref[0, :half, :]
    w_bot = w1_ref[0, half:, :]
    bias = _dot(pe_ref[0], w1_ref[0])[0:1, :]
    row = _iota((n_chunks, 1), 0)
    outs = []
    for g in range(NSA_KV_HEADS):
        c = c_ref[0, 0, g]
        first = _dot(c, w_top)
        second = _dot(c, w_bot)
        hid = first + pltpu.roll(second, n_chunks - 1, 0) + bias
        y = _dot(_gelu_tanh(hid).astype(BF16), w2_ref[0])
        outs.append(jnp.where(row < n_chunks - 1, y, 0.0))
    out_ref[0, 0] = jnp.concatenate(outs, axis=1).astype(BF16)


def _compress(chunks, pe, w1, w2):
    _, b, g, nc, cw = chunks.shape
    return pl.pallas_call(
        _compress_kernel,
        grid=(2, b),
        in_specs=[
            pl.BlockSpec((1, 1, g, nc, cw), lambda k, bi: (k, bi, 0, 0, 0)),
            pl.BlockSpec((1,) + pe.shape[1:], lambda k, bi: (k, 0, 0)),
            pl.BlockSpec((1,) + w1.shape[1:], lambda k, bi: (k, 0, 0)),
            pl.BlockSpec((1,) + w2.shape[1:], lambda k, bi: (k, 0, 0)),
        ],
        out_specs=pl.BlockSpec((1, 1, nc, g * HEAD_DIM), lambda k, bi: (k, bi, 0, 0)),
        out_shape=jax.ShapeDtypeStruct((2, b, nc, g * HEAD_DIM), BF16),
        compiler_params=_params("parallel", "parallel"),
        name="compress",
    )(chunks, pe, w1, w2)


SOFTMAX_FLOOR = -1e29
TAKEN = -3e38
ATTN_BATCH = 4
BLOCKS_PER_CHUNK = KEY_CHUNK // SLC_BLOCK
AUX_COLS = 16
AUX_SLOPE = BLOCKS_PER_CHUNK
CMP_ROWS_STEP = 256
CMP_AUX_SPLIT = 128


def _slope(g, r):
    if isinstance(g, int):
        return jnp.float32(ALIBI_SLOPES[g * NSA_GROUP + r])
    s = jnp.float32(ALIBI_SLOPES[r])
    for gi in range(1, NSA_KV_HEADS):
        s = jnp.where(g == gi, jnp.float32(ALIBI_SLOPES[gi * NSA_GROUP + r]), s)
    return s


def _nsa_cmp_kernel(qt_ref, kc_ref, vct_ref, poolt_ref, pair_ref, bits_ref, oct_ref, selt_ref, words_ref,
                    *, n_sel):
    i = pl.program_id(1)
    qt = selt_ref.shape[4]
    ncp = kc_ref.shape[2]
    n_slc = poolt_ref.shape[0]
    gq = qt_ref.shape[4]
    start = i * qt
    t = start + _iota((1, qt), 1)
    aux_row = _iota((AUX_COLS, gq), 0)
    col_head = _iota((1, gq), 1) // qt
    pad_rows = jnp.zeros((kc_ref.shape[3] - qt_ref.shape[3] - AUX_COLS, gq), BF16)

    def weights(g):
        slope_cols = jnp.zeros((1, gq), F32)
        for r in range(NSA_GROUP):
            slope_cols = jnp.where(col_head == r, _slope(g, r), slope_cols)
        aux = jnp.where(aux_row == 0, slope_cols * (CMP_STRIDE * CMP_AUX_SPLIT),
                        jnp.where(aux_row == 1, slope_cols * CMP_STRIDE, 0.0))
        return jnp.concatenate([qt_ref[0, g, 0], aux.astype(BF16), pad_rows], axis=0)

    def importance(g, nr, nb):
        s = _dot(kc_ref[0, g, :nr, :], weights(g))
        edge = min(nr, 2 * CMP_ROWS_STEP)
        cmp_end = ((nr - edge) + _iota((edge, 1), 0)) * CMP_STRIDE + (CMP_LEN - 1)
        visible = cmp_end <= t
        vct = vct_ref[0, g * HEAD_DIM:(g + 1) * HEAD_DIM, :nr]
        psum = jnp.zeros((nr, qt), F32)
        for r in range(NSA_GROUP):
            cols = slice(r * qt, (r + 1) * qt)
            sr = s[:, cols]
            tail = jnp.where(visible, sr[nr - edge:], NEG_INF)
            sr = tail if edge == nr else jnp.concatenate([sr[:nr - edge], tail], axis=0)
            m = jnp.maximum(jnp.max(sr, axis=0, keepdims=True), SOFTMAX_FLOOR)
            e = jnp.exp(sr - m)
            l = jnp.sum(e, axis=0, keepdims=True)
            inv = jnp.where(l > 0.0, 1.0 / l, 0.0)
            oct_ref[0, g, 0, :, cols] = (_dot(vct, e.astype(BF16)) * inv).astype(BF16)
            psum = psum + e * inv
        return _dot(poolt_ref[:nb, :nr], psum.astype(BF16))

    def visible_prefix(nr):
        nb = min(n_slc, nr * CMP_STRIDE // SLC_BLOCK)
        imp = jnp.concatenate([importance(g, nr, nb) for g in range(NSA_KV_HEADS)], axis=1)
        blk = _iota((nb, 1), 0)
        cur = jnp.concatenate([t // SLC_BLOCK] * NSA_KV_HEADS, axis=1)
        forced = (blk == 0) | (blk == cur) | (blk == cur - 1)
        score = jnp.where(forced, TAKEN, jnp.where(blk <= cur, imp, NEG_INF))
        n_forced = 1 + jnp.where(cur >= 1, 1, 0) + jnp.where(cur >= 2, 1, 0)
        blk_f = blk.astype(F32)

        def take_one(score, active):
            m = jnp.max(score, axis=0, keepdims=True)
            first = jnp.min(jnp.where(score == m, blk_f, F32(1e9)), axis=0, keepdims=True)
            hit = (blk_f == first) if active is None else ((blk_f == first) & active)
            return jnp.where(hit, TAKEN, score)

        common_rounds = max(n_sel - 3, 0)
        for _ in range(common_rounds):
            score = take_one(score, None)

        def early_rounds(score):
            for k in range(common_rounds, n_sel - 1):
                score = take_one(score, n_sel - n_forced > k)
            return score

        score = lax.cond(start < 2 * SLC_BLOCK, early_rounds, lambda sc: sc, score)
        for g in range(NSA_KV_HEADS):
            sel_g = score[:, g * qt:(g + 1) * qt] == TAKEN
            selt_ref[0, g, 0, :nb, :] = jnp.where(sel_g, 0.0, NEG_INF)
            if nb < n_slc:
                selt_ref[0, g, 0, nb:, :] = jnp.full((n_slc - nb, qt), NEG_INF, F32)
            count = _dot_nt(jnp.ones((8, qt), BF16), jnp.where(sel_g, 1.0, 0.0).astype(BF16))
            used = jnp.where(count > 0.0, 1.0, 0.0).astype(BF16)
            chunk_used = jnp.where(_dot(used, pair_ref[:nb, :]) > 0.0, 1.0, 0.0).astype(BF16)
            words_ref[0, g, 0] = _dot(chunk_used, bits_ref[...]).astype(jnp.int32)

    step = min(CMP_ROWS_STEP, ncp)
    rows_needed = jnp.minimum((start + qt - CMP_LEN) // CMP_STRIDE + 1, ncp)
    n_steps = (rows_needed + step - 1) // step
    for k in range(ncp // step):
        pl.when(n_steps == k + 1)(functools.partial(visible_prefix, (k + 1) * step))


def _nsa_cmp(q_t, kc, vct, poolt_m, pair_m, bits_m, n_sel):
    b, ng, n_tiles, qrows, qcols = q_t.shape
    ncp = kc.shape[2]
    n_slc = poolt_m.shape[0]
    const = lambda shape: pl.BlockSpec(shape, lambda bi, i: (0,) * len(shape))
    tile5 = lambda rows, cols: pl.BlockSpec((1, ng, 1, rows, cols), lambda bi, i: (bi, 0, i, 0, 0))
    return pl.pallas_call(
        functools.partial(_nsa_cmp_kernel, n_sel=n_sel),
        grid=(b, n_tiles),
        in_specs=[
            tile5(qrows, qcols),
            pl.BlockSpec((1, ng, ncp, kc.shape[3]), lambda bi, i: (bi, 0, 0, 0)),
            pl.BlockSpec((1, ng * HEAD_DIM, ncp), lambda bi, i: (bi, 0, 0)),
            const(poolt_m.shape), const(pair_m.shape), const(bits_m.shape),
        ],
        out_specs=[tile5(HEAD_DIM, qcols), tile5(n_slc, Q_TILE), tile5(8, LANES)],
        out_shape=[
            jax.ShapeDtypeStruct((b, ng, n_tiles, HEAD_DIM, qcols), BF16),
            jax.ShapeDtypeStruct((b, ng, n_tiles, n_slc, Q_TILE), F32),
            jax.ShapeDtypeStruct((b, ng, n_tiles, 8, LANES), jnp.int32),
        ],
        compiler_params=_params("parallel", "parallel"),
        name="nsa_compressed",
    )(q_t, kc, vct, poolt_m, pair_m, bits_m)


def _nsa_slc_kernel(words_ref, qt_ref, ks_ref, vst_ref, kw_ref, vwt_ref, selt_ref, oct_ref, gt_ref,
                    out_ref, m_ref, l_ref, acc_ref, ow_ref, list_ref, *, words_per_tile):
    bi = pl.program_id(0)
    g = pl.program_id(1)
    i = pl.program_id(2)
    n_tiles = pl.num_programs(2)
    qt = out_ref.shape[1]
    q_rows = qt_ref[0, 0, 0]
    gq = q_rows.shape[1]
    start = i * qt
    lane_f = _iota((1, qt), 1).astype(F32)
    key_in_chunk = _iota((KEY_CHUNK, qt), 0)
    query_in_tile = _iota((KEY_CHUNK, qt), 1)

    aux_row = _iota((AUX_COLS, gq), 0)
    col_head = _iota((1, gq), 1) // qt
    slope_cols = jnp.zeros((1, gq), F32)
    for r in range(NSA_GROUP):
        slope_cols = jnp.where(col_head == r, _slope(g, r), slope_cols)
    aux_base = jnp.where(aux_row == AUX_SLOPE, slope_cols, 0.0)
    pad_rows = jnp.zeros((ks_ref.shape[3] - q_rows.shape[0] - AUX_COLS, gq), BF16)

    def reset():
        m_ref[...] = jnp.full(m_ref.shape, SOFTMAX_FLOOR, F32)
        l_ref[...] = jnp.zeros(l_ref.shape, F32)
        acc_ref[...] = jnp.zeros(acc_ref.shape, F32)

    def attend(slots):
        scores = [_dot(k, jnp.concatenate([q_rows, aux.astype(BF16), pad_rows], axis=0))
                  for k, aux, _, _, _ in slots]
        v_cat = jnp.concatenate([v for _, _, v, _, _ in slots], axis=1)
        for r in range(NSA_GROUP):
            cols = slice(r * qt, (r + 1) * qt)
            srs, tops = [], []
            for (_, _, _, shift, mask), s in zip(slots, scores):
                sr = s[:, cols] if mask is None else jnp.where(mask, s[:, cols], NEG_INF)
                srs.append(sr)
                tops.append(jnp.max(sr, axis=0, keepdims=True) + shift[r])
            m_old = m_ref[r]
            m_new = functools.reduce(jnp.maximum, tops, m_old)
            alpha = jnp.exp(m_old - m_new)
            ps = [jnp.exp(sr - (m_new - slot[3][r])) for slot, sr in zip(slots, srs)]
            l_ref[r] = alpha * l_ref[r] + functools.reduce(
                jnp.add, [jnp.sum(p, axis=0, keepdims=True) for p in ps])
            p_cat = jnp.concatenate([p.astype(BF16) for p in ps], axis=0)
            acc_ref[:, cols] = alpha * acc_ref[:, cols] + _dot(v_cat, p_cat)
            m_ref[r] = m_new

    def finalize(r):
        l = l_ref[r]
        return acc_ref[:, r * qt:(r + 1) * qt] * jnp.where(l > 0.0, 1.0 / l, 0.0)

    def shifts(dist0, ok):
        rows = [-_slope(g, r) * (dist0 + lane_f) for r in range(NSA_GROUP)]
        return rows if ok is None else [jnp.where(ok, row, NEG_INF) for row in rows]

    reset()
    n_back = WINDOW // KEY_CHUNK
    slots = []
    for j in range(n_back + 1):
        cs = start - WINDOW + j * KEY_CHUNK
        chunk = jnp.maximum(cs, 0) // KEY_CHUNK
        at = pl.multiple_of(chunk * KEY_CHUNK, KEY_CHUNK)
        mask = (query_in_tile < key_in_chunk) if j == 0 else (
            (key_in_chunk <= query_in_tile) if j == n_back else None)
        slots.append((kw_ref[0, 0, pl.ds(at, KEY_CHUNK), :], aux_base, vwt_ref[0, 0, chunk],
                      shifts(F32(WINDOW - j * KEY_CHUNK), cs >= 0), mask))
    attend(slots)
    for r in range(NSA_GROUP):
        ow_ref[:, r * qt:(r + 1) * qt] = finalize(r)

    reset()
    word_base = ((bi * NSA_KV_HEADS + g) * n_tiles + i) * words_per_tile
    list_ref[0] = 0

    def scan_word(w, n):
        word = words_ref[word_base + w]

        def scan_bits(n):
            for bit in range(WORD_BITS):
                c = w * WORD_BITS + bit
                list_ref[n] = c
                n = n + jnp.where(c < i, (word >> bit) & 1, 0)
            return n

        return lax.cond(word != 0, scan_bits, lambda n: n, n)

    n_listed = lax.fori_loop(0, (i + WORD_BITS - 1) // WORD_BITS, scan_word, 0)

    def selected_slot(c, ok, mask):
        at = pl.multiple_of(c * KEY_CHUNK, KEY_CHUNK)
        bias = selt_ref[0, 0, 0, pl.ds(c * BLOCKS_PER_CHUNK, BLOCKS_PER_CHUNK), :]
        aux = aux_base
        for blk in range(BLOCKS_PER_CHUNK):
            aux = jnp.where(aux_row == blk, jnp.concatenate([bias[blk:blk + 1]] * NSA_GROUP, axis=1), aux)
        return (ks_ref[0, 0, pl.ds(at, KEY_CHUNK), :], aux, vst_ref[0, 0, c],
                shifts((start - c * KEY_CHUNK).astype(F32), ok), mask)

    def listed_slot(idx):
        ok = idx < n_listed
        c = jnp.where(ok, list_ref[jnp.minimum(idx, jnp.maximum(n_listed - 1, 0))], 0)
        return selected_slot(c, ok, None)

    attend([selected_slot(i, None, key_in_chunk <= query_in_tile)]
           + [listed_slot(j) for j in range(ATTN_BATCH - 1)])

    def batch(it, carry):
        first = ATTN_BATCH - 1 + it * ATTN_BATCH
        attend([listed_slot(first + j) for j in range(ATTN_BATCH)])
        return carry

    n_rest = jnp.maximum(n_listed - (ATTN_BATCH - 1), 0)
    lax.fori_loop(0, (n_rest + ATTN_BATCH - 1) // ATTN_BATCH, batch, 0)

    outs = []
    for r in range(NSA_GROUP):
        cols = slice(r * qt, (r + 1) * qt)
        col = (g * NSA_GROUP + r) * NSA_BRANCHES
        gate = lambda br: jax.nn.sigmoid(gt_ref[0, pl.ds(col + br, 1), :])
        outs.append(gate(0) * oct_ref[0, 0, 0, :, cols].astype(F32) + gate(1) * finalize(r)
                    + gate(2) * ow_ref[:, cols])
    out_ref[0] = jnp.concatenate(outs, axis=0).T.astype(out_ref.dtype)


def _nsa_slc(words, q_t, ks, vst, kw, vwt, selt, oct, gates_t, words_per_tile):
    b, _, t, kw_cols = ks.shape
    _, _, n_tiles, qrows, qcols = q_t.shape
    n_slc = selt.shape[3]
    n_chunks = vst.shape[2]
    gw = NSA_GROUP * HEAD_DIM
    once = dict(pipeline_mode=pl.Buffered(1))
    k_spec = pl.BlockSpec((1, 1, t, kw_cols), lambda bi, g, i, w: (bi, g, 0, 0), **once)
    vt_spec = pl.BlockSpec((1, 1, n_chunks, HEAD_DIM, KEY_CHUNK), lambda bi, g, i, w: (bi, g, 0, 0, 0), **once)
    tile5 = lambda rows, cols: pl.BlockSpec((1, 1, 1, rows, cols), lambda bi, g, i, w: (bi, g, i, 0, 0))
    grid_spec = pltpu.PrefetchScalarGridSpec(
        num_scalar_prefetch=1,
        grid=(b, NSA_KV_HEADS, n_tiles),
        in_specs=[
            tile5(qrows, qcols),
            k_spec, vt_spec, k_spec, vt_spec,
            tile5(n_slc, Q_TILE),
            tile5(HEAD_DIM, qcols),
            pl.BlockSpec((1, GATE_PAD, Q_TILE), lambda bi, g, i, w: (bi, 0, i)),
        ],
        out_specs=pl.BlockSpec((1, Q_TILE, gw), lambda bi, g, i, w: (bi, i, g)),
        scratch_shapes=[pltpu.VMEM((NSA_GROUP, 1, Q_TILE), F32), pltpu.VMEM((NSA_GROUP, 1, Q_TILE), F32),
                        pltpu.VMEM((HEAD_DIM, qcols), F32), pltpu.VMEM((HEAD_DIM, qcols), F32),
                        pltpu.SMEM((n_chunks,), jnp.int32)],
    )
    return pl.pallas_call(
        functools.partial(_nsa_slc_kernel, words_per_tile=words_per_tile),
        grid_spec=grid_spec,
        out_shape=jax.ShapeDtypeStruct((b, t, NSA_HEADS * HEAD_DIM), BF16),
        compiler_params=_params("parallel", "parallel", "parallel"),
        name="nsa_selected_window",
    )(words, q_t, ks, vst, kw, vwt, selt, oct, gates_t)


POOL_HALO = 16
CONV_HALO = 8


def _merge_kernel(h_ref, g_ref, nsa_ref, pool_ref, pool_halo_ref, conv_ref, conv_halo_ref,
                  wmg_ref, wnsa_ref, pool_bd_ref, pool_scale_ref, wpool_ref, convw_ref, wconv_ref, wo_ref,
                  out_ref, pool_ext, conv_ext, *, seq_len):
    i = pl.program_id(0)
    tm, d = h_ref.shape
    cw = pool_ref.shape[1]
    pos0 = (i * tm) % seq_len
    keep_halo = jnp.where(pos0 == 0, 0.0, 1.0)
    pos = pos0 + _iota((tm, 1), 0)

    u = pool_ref[...]
    pool_ext[0:POOL_HALO, :] = pool_halo_ref[...] * keep_halo
    pool_ext[POOL_HALO:, :] = u
    lane_group = _iota((1, cw), 1) // (cw // len(POOL_WINDOWS))
    total = u
    mean = jnp.zeros_like(u)
    done = 1
    for gi, win in enumerate(POOL_WINDOWS):
        for k in range(done, win):
            total = total + pool_ext[POOL_HALO - k:POOL_HALO - k + tm, :]
        done = win
        cnt = jnp.minimum(pos + 1, win).astype(F32)
        mean = jnp.where(lane_group == gi, total / cnt, mean)
    pooled = (mean - u).astype(BF16)
    mixed = _dot(pooled, pool_bd_ref[...]) * pool_scale_ref[...]
    y_pool = _dot(mixed.astype(BF16), wpool_ref[...])

    ch = conv_ref[:, 0:cw]
    cb = conv_ref[:, cw:2 * cw]
    cc = conv_ref[:, 2 * cw:3 * cw]
    conv_ext[0:CONV_HALO, :] = conv_halo_ref[:, 0:cw] * conv_halo_ref[:, 2 * cw:3 * cw] * keep_halo
    conv_ext[CONV_HALO:, :] = cc * ch
    y = jnp.zeros((tm, cw), F32)
    for k in range(CONV_K):
        off = CONV_HALO - (CONV_K - 1) + k
        y = y + convw_ref[k:k + 1, :] * conv_ext[off:off + tm, :]
    y_conv = _dot((cb * y).astype(BF16), wconv_ref[...])

    y_nsa = _dot(nsa_ref[...], wnsa_ref[...])

    h = h_ref[...]
    xn = _rms_norm(h, g_ref[...]).astype(BF16)
    merged = jnp.zeros((tm, d), F32)
    for br, y_br in enumerate((y_nsa, y_pool, y_conv)):
        mg = jax.nn.sigmoid(_dot(xn, wmg_ref[:, br * d:(br + 1) * d]))
        merged = merged + mg * y_br
    out_ref[...] = h + _dot(merged.astype(BF16), wo_ref[...])


def _merge(h, g, nsa, pool_u, conv, wmg, wnsa, pool_bd, pool_scale, wpool, convw, wconv, wo, seq_len):
    n, d = h.shape
    tm = TOKEN_TILE
    cw = pool_u.shape[1]
    row = lambda width: pl.BlockSpec((tm, width), lambda i: (i, 0))
    full = lambda a: pl.BlockSpec(a.shape, lambda i: (0,) * a.ndim)
    halo = lambda rows, width: pl.BlockSpec(
        (rows, width), lambda i: (jnp.maximum(i * (tm // rows) - 1, 0), 0))
    return pl.pallas_call(
        functools.partial(_merge_kernel, seq_len=seq_len),
        grid=(n // tm,),
        in_specs=[row(d), full(g), row(nsa.shape[1]), row(cw), halo(POOL_HALO, cw),
                  row(conv.shape[1]), halo(CONV_HALO, conv.shape[1]),
                  full(wmg), full(wnsa), full(pool_bd), full(pool_scale), full(wpool), full(convw),
                  full(wconv), full(wo)],
        out_specs=row(d),
        out_shape=jax.ShapeDtypeStruct((n, d), F32),
        scratch_shapes=[pltpu.VMEM((tm + POOL_HALO, cw), F32), pltpu.VMEM((tm + CONV_HALO, cw), F32)],
        compiler_params=_params("parallel"),
        name="merge",
    )(h, g, nsa, pool_u, pool_u, conv, conv, wmg, wnsa, pool_bd, pool_scale, wpool, convw, wconv, wo)


def _route(logits):
    lane = _iota(logits.shape, 1)
    lane_f = lane.astype(F32)
    big = F32(1e9)
    is_group = lane < N_EXPERT_GROUPS
    gl = jnp.where(is_group, logits, NEG_INF)
    g_max = jnp.max(gl, axis=1, keepdims=True)
    g_sel = jnp.min(jnp.where(gl == g_max, lane_f, big), axis=1, keepdims=True)
    g_prob = 1.0 / jnp.sum(jnp.where(is_group, jnp.exp(gl - g_max), 0.0), axis=1, keepdims=True)
    lo = N_EXPERT_GROUPS + EXPERTS_PER_GROUP * g_sel
    in_group = (lane_f >= lo) & (lane_f < lo + EXPERTS_PER_GROUP)
    el = jnp.where(in_group, logits, NEG_INF)
    v1 = jnp.max(el, axis=1, keepdims=True)
    i1 = jnp.min(jnp.where((el == v1) & in_group, lane_f, big), axis=1, keepdims=True)
    el2 = jnp.where(lane_f == i1, NEG_INF, el)
    rest = in_group & (lane_f != i1)
    v2 = jnp.max(el2, axis=1, keepdims=True)
    i2 = jnp.min(jnp.where((el2 == v2) & rest, lane_f, big), axis=1, keepdims=True)
    e2 = jnp.exp(v2 - v1)
    w1 = g_prob / (1.0 + e2)
    w2 = g_prob * e2 / (1.0 + e2)
    return jnp.where(lane_f == i1, w1, 0.0) + jnp.where(lane_f == i2, w2, 0.0), g_sel


GROUP_LANE = N_EXPERT_GROUPS + N_EXPERTS


def _router_kernel(h_ref, g_ref, wr_ref, br_ref, xn_ref, comb_ref):
    xn = _rms_norm(h_ref[...], g_ref[...])
    logits = jnp.dot(xn, wr_ref[...], preferred_element_type=F32,
                     precision=lax.Precision.HIGHEST) + br_ref[...]
    comb, g_sel = _route(logits)
    comb_ref[...] = jnp.where(_iota(comb.shape, 1) == GROUP_LANE, g_sel, comb)
    xn_ref[...] = xn.astype(BF16)


def _router(h, g, wr, br):
    n, d = h.shape
    tm = MOE_TILE
    full = lambda a: pl.BlockSpec(a.shape, lambda i: (0,) * a.ndim)
    return pl.pallas_call(
        _router_kernel,
        grid=(n // tm,),
        in_specs=[pl.BlockSpec((tm, d), lambda i: (i, 0)), full(g), full(wr), full(br)],
        out_specs=[pl.BlockSpec((tm, d), lambda i: (i, 0)), pl.BlockSpec((tm, ROUTER_PAD), lambda i: (i, 0))],
        out_shape=[jax.ShapeDtypeStruct((n, d), BF16), jax.ShapeDtypeStruct((n, ROUTER_PAD), F32)],
        compiler_params=_params("parallel"),
        name="router",
    )(h, g, wr, br)


def _experts_kernel(tile_group_ref, n_active_ref, x_ref, comb_ref, w1_ref, w3_ref, w2_ref, out_ref):
    i = pl.program_id(0)

    @pl.when(i < n_active_ref[0])
    def _():
        x = x_ref[...]
        comb = comb_ref[...]
        lane = _iota(comb.shape, 1)
        first = N_EXPERT_GROUPS + tile_group_ref[i] * EXPERTS_PER_GROUP
        for e in range(EXPERTS_PER_GROUP):
            c_e = jnp.sum(jnp.where(lane == first + e, comb, 0.0), axis=1, keepdims=True)
            a = jax.nn.silu(_dot(x, w1_ref[0, e])) * _dot(x, w3_ref[0, e]) * c_e
            y = _dot(a.astype(BF16), w2_ref[0, e])
            if e == 0:
                out_ref[...] = y
            else:
                out_ref[...] += y


def _experts(tile_group, n_active, x_sorted, comb_sorted, w1, w3, w2):
    ns, d = x_sorted.shape
    tm = MOE_SORT_TILE
    group_w = lambda w: pl.BlockSpec((1,) + w.shape[1:], lambda i, tg, na: (tg[i], 0, 0, 0))
    grid_spec = pltpu.PrefetchScalarGridSpec(
        num_scalar_prefetch=2,
        grid=(ns // tm,),
        in_specs=[
            pl.BlockSpec((tm, d), lambda i, tg, na: (i, 0)),
            pl.BlockSpec((tm, ROUTER_PAD), lambda i, tg, na: (i, 0)),
            group_w(w1), group_w(w3), group_w(w2),
        ],
        out_specs=pl.BlockSpec((tm, d), lambda i, tg, na: (i, 0)),
    )
    return pl.pallas_call(
        _experts_kernel,
        grid_spec=grid_spec,
        out_shape=jax.ShapeDtypeStruct((ns, d), F32),
        compiler_params=_params("arbitrary"),
        name="experts",
    )(tile_group, n_active, x_sorted, comb_sorted, w1, w3, w2)


def _residual_kernel(h_ref, y_ref, gf_ref, out_ref, *, final_norm):
    out = h_ref[...] + y_ref[...]
    out_ref[...] = _rms_norm(out, gf_ref[...]) if final_norm else out


def _residual(h, y, gf, final_norm):
    n, d = h.shape
    tm = MOE_TILE
    row = pl.BlockSpec((tm, d), lambda i: (i, 0))
    return pl.pallas_call(
        functools.partial(_residual_kernel, final_norm=final_norm),
        grid=(n // tm,),
        in_specs=[row, row, pl.BlockSpec(gf.shape, lambda i: (0, 0))],
        out_specs=row,
        out_shape=jax.ShapeDtypeStruct((n, d), F32),
        compiler_params=_params("parallel"),
        name="residual",
    )(h, y, gf)


def _group_sort_plan(group_id, tile):
    n = group_id.shape[0]
    n_slots = n + N_EXPERT_GROUPS * tile
    one_hot = (group_id[:, None] == jnp.arange(N_EXPERT_GROUPS)[None, :]).astype(jnp.int32)
    running = jnp.cumsum(one_hot, axis=0)
    padded = (running[-1] + tile - 1) // tile * tile
    ends = jnp.cumsum(padded)
    rank = jnp.take_along_axis(running, group_id[:, None], axis=1)[:, 0] - 1
    slot = (ends - padded)[group_id] + rank
    source = jnp.zeros((n_slots,), jnp.int32).at[slot].set(jnp.arange(n, dtype=jnp.int32))
    tile_start = jnp.arange(n_slots // tile, dtype=jnp.int32) * tile
    tile_group = jnp.minimum(jnp.searchsorted(ends, tile_start, side="right"), N_EXPERT_GROUPS - 1)
    return slot, source, tile_group.astype(jnp.int32), (ends[-1:] // tile).astype(jnp.int32)


def _moe(h, g, wr, br, w1, w3, w2, gf, final_norm):
    xn, comb = _router(h, g, wr, br)
    group_id = comb[:, GROUP_LANE].astype(jnp.int32)
    slot, source, tile_group, n_active = _group_sort_plan(group_id, MOE_SORT_TILE)
    grouped = lambda w: w.reshape((N_EXPERT_GROUPS, EXPERTS_PER_GROUP) + w.shape[1:])
    y_sorted = _experts(tile_group, n_active, jnp.take(xn, source, axis=0), jnp.take(comb, source, axis=0),
                        grouped(w1), grouped(w3), grouped(w2))
    return _residual(h, jnp.take(y_sorted, slot, axis=0), gf, final_norm)


def _selection_constants(seq_len):
    ncp = seq_len // CMP_STRIDE
    n_slc = seq_len // SLC_BLOCK
    ratio = SLC_BLOCK // CMP_STRIDE
    lead = CMP_LEN // CMP_STRIDE - 1
    c = np.arange(ncp)[:, None]
    j = np.arange(n_slc)[None, :]
    pool_m = ((c >= ratio * j - lead) & (c < ratio * j + ratio)).astype(np.float32)
    blocks_per_chunk = KEY_CHUNK // SLC_BLOCK
    n_chunks = seq_len // KEY_CHUNK
    pair_m = np.zeros((n_slc, LANES * ((n_chunks + LANES - 1) // LANES)), np.float32)
    pair_m[np.arange(n_slc), np.arange(n_slc) // blocks_per_chunk] = 1.0
    n_words = (n_chunks + WORD_BITS - 1) // WORD_BITS
    bits_m = np.zeros((pair_m.shape[1], LANES), np.float32)
    ch = np.arange(n_chunks)
    bits_m[ch, ch // WORD_BITS] = 2.0 ** (ch % WORD_BITS)
    key_aux = np.zeros((seq_len, LANES - HEAD_DIM), np.float32)
    in_chunk = np.arange(seq_len) % KEY_CHUNK
    key_aux[np.arange(seq_len), in_chunk // SLC_BLOCK] = 1.0
    key_aux[:, AUX_SLOPE] = in_chunk
    cmp_aux = np.zeros((ncp, LANES - HEAD_DIM), np.float32)
    cmp_aux[:, 0] = np.arange(ncp) // CMP_AUX_SPLIT
    cmp_aux[:, 1] = np.arange(ncp) % CMP_AUX_SPLIT
    as_bf16 = lambda a: jnp.asarray(a, BF16)
    return as_bf16(pool_m.T), as_bf16(pair_m), as_bf16(bits_m), as_bf16(key_aux), as_bf16(cmp_aux), n_words


def kernel(x, norm1_g, w_in, cmp_pe, cmp_w1, cmp_w2, w_nsa_proj, pool_w, pool_scale, w_pool_proj, conv_w,
           w_conv_proj, w_o, norm2_g, router_group_w, router_group_b, router_expert_w, router_expert_b,
           expert_w1, expert_w3, expert_w2, final_norm_g):
    b, t, d = x.shape
    n = b * t
    depth = w_in.shape[0]
    dq = NSA_HEADS * HEAD_DIM
    dkv = 6 * NSA_KV_HEADS * HEAD_DIM
    dgate = NSA_HEADS * NSA_BRANCHES
    cw = d // 4
    assert t % TOKEN_TILE == 0 and n % MOE_TILE == 0 and t % Q_TILE == 0
    n_slc = t // SLC_BLOCK
    n_sel = min(SLC_TOPN, n_slc)
    n_chunks16 = t // CMP_STRIDE
    n_tiles = t // Q_TILE
    poolt_m, pair_m, bits_m, key_aux, cmp_aux, n_words = _selection_constants(t)
    assert Q_TILE == KEY_CHUNK and AUX_SLOPE < AUX_COLS and KEY_CHUNK <= 256
    assert n_chunks16 <= 256 * CMP_AUX_SPLIT and n_chunks16 % min(CMP_ROWS_STEP, n_chunks16) == 0
    assert n_words <= LANES and pair_m.shape[1] == LANES

    h = x.reshape(n, d)
    for l in range(depth):
        wl = w_in[l]
        o_gate = dq + dkv
        o_pool = o_gate + dgate
        o_merge = o_pool + cw + 3 * cw
        w_a = jnp.concatenate([
            wl[:, :dq] * (HEAD_DIM ** -0.5),
            wl[:, dq:o_gate],
            jnp.pad(wl[:, o_gate:o_pool], ((0, 0), (0, GATE_PAD - dgate))),
            wl[:, o_pool:o_merge],
        ], axis=1).astype(BF16)
        wmg = wl[:, o_merge:].astype(BF16)
        pool_bd = jax.scipy.linalg.block_diag(*[pool_w[l, gi] for gi in range(pool_w.shape[1])]).astype(BF16)
        convw = jnp.pad(conv_w[l], ((0, 8 - CONV_K), (0, 0)))
        wr = jnp.pad(jnp.concatenate([router_group_w[l], router_expert_w[l]], axis=1),
                     ((0, 0), (0, ROUTER_PAD - N_EXPERT_GROUPS - N_EXPERTS)))
        br = jnp.pad(jnp.concatenate([router_group_b[l], router_expert_b[l]]),
                     (0, ROUTER_PAD - N_EXPERT_GROUPS - N_EXPERTS))[None, :]
        pe = jnp.broadcast_to(cmp_pe[l].reshape(2, 1, CMP_LEN * HEAD_DIM), (2, 8, CMP_LEN * HEAD_DIM)).astype(BF16)

        q, kv, gates, pool_u, conv = _inproj(h, norm1_g[l][None, :], w_a)
        chunks = kv[:, :2 * NSA_KV_HEADS * HEAD_DIM].reshape(b, n_chunks16, CMP_STRIDE, 2, NSA_KV_HEADS, HEAD_DIM)
        chunks = chunks.transpose(3, 0, 4, 1, 2, 5).reshape(2, b, NSA_KV_HEADS, n_chunks16, CMP_STRIDE * HEAD_DIM)
        kvc = _compress(chunks, pe, cmp_w1[l].astype(BF16), cmp_w2[l].astype(BF16))
        kv3 = kv.reshape(b, t, dkv)
        kvw = NSA_KV_HEADS * HEAD_DIM
        q_t = q.reshape(b, n_tiles, Q_TILE, NSA_KV_HEADS, NSA_GROUP, HEAD_DIM).transpose(0, 3, 1, 5, 4, 2)
        q_t = q_t.reshape(b, NSA_KV_HEADS, n_tiles, HEAD_DIM, NSA_GROUP * Q_TILE)
        per_group = lambda kind: kv3[:, :, kind * kvw:(kind + 1) * kvw].reshape(b, t, NSA_KV_HEADS, HEAD_DIM)
        with_aux = lambda kind: jnp.concatenate(
            [per_group(kind).transpose(0, 2, 1, 3),
             jnp.broadcast_to(key_aux, (b, NSA_KV_HEADS, t, LANES - HEAD_DIM))], axis=-1)
        chunked_t = lambda kind: per_group(kind).reshape(
            b, t // KEY_CHUNK, KEY_CHUNK, NSA_KV_HEADS, HEAD_DIM).transpose(0, 3, 1, 4, 2)
        kc_aux = jnp.concatenate(
            [kvc[0].reshape(b, n_chunks16, NSA_KV_HEADS, HEAD_DIM).transpose(0, 2, 1, 3),
             jnp.broadcast_to(cmp_aux, (b, NSA_KV_HEADS, n_chunks16, LANES - HEAD_DIM))], axis=-1)
        oc_t, sel_t, words = _nsa_cmp(q_t, kc_aux, kvc[1].transpose(0, 2, 1), poolt_m, pair_m, bits_m, n_sel)
        words = words[:, :, :, 0, :n_words].reshape(-1)
        nsa = _nsa_slc(words, q_t, with_aux(2), chunked_t(3), with_aux(4), chunked_t(5), sel_t, oc_t,
                       gates.reshape(b, t, GATE_PAD).transpose(0, 2, 1), n_words)
        h = _merge(h, norm1_g[l][None, :], nsa.reshape(n, dq), pool_u, conv, wmg,
                   w_nsa_proj[l].astype(BF16), pool_bd, pool_scale[l][None, :], w_pool_proj[l].astype(BF16),
                   convw, w_conv_proj[l].astype(BF16), w_o[l].astype(BF16), t)
        h = _moe(h, norm2_g[l][None, :], wr, br, expert_w1[l].astype(BF16), expert_w3[l].astype(BF16),
                 expert_w2[l].astype(BF16), final_norm_g[None, :], final_norm=(l == depth - 1))
    return h.reshape(b, t, d)
```

```python
import functools

import jax
import jax.numpy as jnp
import numpy as np
from jax import lax
from jax.experimental import pallas as pl
from jax.experimental.pallas import tpu as pltpu

F32 = jnp.float32
BF16 = jnp.bfloat16

HEAD_DIM = 64
NSA_HEADS = 8
NSA_KV_HEADS = 2
NSA_GROUP = NSA_HEADS // NSA_KV_HEADS
CMP_LEN = 32
CMP_STRIDE = 16
CMP_HIDDEN = 4 * HEAD_DIM
SLC_BLOCK = 64
SLC_TOPN = 16
WINDOW = 512
NSA_BRANCHES = 3
POOL_WINDOWS = (2, 4, 8, 16)
CONV_K = 3
N_BRANCHES = 3
N_EXPERT_GROUPS = 4
EXPERTS_PER_GROUP = 8
N_EXPERTS = N_EXPERT_GROUPS * EXPERTS_PER_GROUP
RMS_EPS = 1e-6
NEG_INF = -1e30
FORCE_SCORE = 1e30
ALIBI_SLOPES = tuple(float(2.0 ** (-8.0 * (h + 1) / NSA_HEADS)) for h in range(NSA_HEADS))

LANES = 128
VMEM_LIMIT = 56 * 1024 * 1024
TOKEN_TILE = 512
MOE_TILE = 1024
MOE_SORT_TILE = 512
Q_TILE = 128
KEY_CHUNK = 128
GATE_PAD = LANES
ROUTER_PAD = LANES
WORD_BITS = 16


def _params(*semantics):
    return pltpu.CompilerParams(dimension_semantics=semantics, vmem_limit_bytes=VMEM_LIMIT)


def _dot(a, b):
    return jnp.dot(a, b, preferred_element_type=F32)


def _dot_nt(a, b):
    return lax.dot_general(a, b, (((1,), (1,)), ((), ())), preferred_element_type=F32)


def _rms_norm(x, g):
    y = x * lax.rsqrt(jnp.mean(x * x, axis=-1, keepdims=True) + RMS_EPS)
    return y * g


def _iota(shape, dim):
    return lax.broadcasted_iota(jnp.int32, shape, dim)


def _inproj_kernel(x_ref, g_ref, wq_ref, wv_ref, wg_ref, wk_ref, wn_ref, kaux_ref,
                   q_ref, vs_ref, vw_ref, gate_ref, ks_ref, kw_ref, cmp_ref, pool_ref, conv_ref):
    xn = _rms_norm(x_ref[...], g_ref[...]).astype(BF16)
    sub_tiles = x_ref.shape[0] // Q_TILE
    q_t = _dot_nt(wq_ref[...], xn)
    for g in range(NSA_KV_HEADS):
        for j in range(sub_tiles):
            for r in range(NSA_GROUP):
                head = g * NSA_GROUP + r
                q_ref[0, g, j, :, r * Q_TILE:(r + 1) * Q_TILE] = q_t[
                    head * HEAD_DIM:(head + 1) * HEAD_DIM, j * Q_TILE:(j + 1) * Q_TILE].astype(BF16)
    v_t = _dot_nt(wv_ref[...], xn)
    k = _dot(xn, wk_ref[...])
    for branch, (v_ref, k_ref) in enumerate(((vs_ref, ks_ref), (vw_ref, kw_ref))):
        for g in range(NSA_KV_HEADS):
            slab = branch * NSA_KV_HEADS + g
            for j in range(sub_tiles):
                v_ref[0, g, j] = v_t[slab * HEAD_DIM:(slab + 1) * HEAD_DIM,
                                     j * KEY_CHUNK:(j + 1) * KEY_CHUNK].astype(BF16)
            k_ref[0, g] = (k[:, slab * LANES:(slab + 1) * LANES] + kaux_ref[...]).astype(BF16)
    gate_ref[0] = _dot_nt(wg_ref[...], xn)
    col = 0
    for ref in (cmp_ref, pool_ref, conv_ref):
        width = ref.shape[-1]
        ref[...] = _dot(xn, wn_ref[:, col:col + width]).reshape(ref.shape)
        col += width


def _inproj(h, g, wq_t, wv_t, wg_t, wk, wn, kaux, batch):
    n, d = h.shape
    t = n // batch
    tm = TOKEN_TILE
    steps = t // tm
    sub = tm // Q_TILE
    cw = d // 4
    gq = NSA_GROUP * Q_TILE
    full = lambda a: pl.BlockSpec(a.shape, lambda i: (0,) * a.ndim)
    row = lambda width: pl.BlockSpec((tm, width), lambda i: (i, 0))
    tiles = lambda rows, cols: pl.BlockSpec((1, NSA_KV_HEADS, sub, rows, cols),
                                            lambda i: (i // steps, 0, i % steps, 0, 0))
    keys = pl.BlockSpec((1, NSA_KV_HEADS, tm, LANES), lambda i: (i // steps, 0, i % steps, 0))
    sds = jax.ShapeDtypeStruct
    v_shape = sds((batch, NSA_KV_HEADS, t // KEY_CHUNK, HEAD_DIM, KEY_CHUNK), BF16)
    k_shape = sds((batch, NSA_KV_HEADS, t, LANES), BF16)
    return pl.pallas_call(
        _inproj_kernel,
        grid=(n // tm,),
        in_specs=[row(d), full(g), full(wq_t), full(wv_t), full(wg_t), full(wk), full(wn), full(kaux)],
        out_specs=[tiles(HEAD_DIM, gq), tiles(HEAD_DIM, KEY_CHUNK), tiles(HEAD_DIM, KEY_CHUNK),
                   pl.BlockSpec((1, GATE_PAD, tm), lambda i: (i // steps, 0, i % steps)),
                   keys, keys,
                   pl.BlockSpec((1, tm, 2 * NSA_KV_HEADS * HEAD_DIM), lambda i: (i // steps, i % steps, 0)),
                   row(cw), row(3 * cw)],
        out_shape=[sds((batch, NSA_KV_HEADS, t // Q_TILE, HEAD_DIM, gq), BF16), v_shape, v_shape,
                   sds((batch, GATE_PAD, t), F32), k_shape, k_shape,
                   sds((batch, t, 2 * NSA_KV_HEADS * HEAD_DIM), F32), sds((n, cw), F32), sds((n, 3 * cw), F32)],
        compiler_params=_params("parallel"),
        name="inproj",
    )(h, g, wq_t, wv_t, wg_t, wk, wn, kaux)


def _gelu_tanh(x):
    return 0.5 * x * (1.0 + jnp.tanh(0.7978845608028654 * (x + 0.044715 * x * x * x)))


def _compress_kernel(src_ref, pe_ref, w1_ref, w1bd_ref, w2k_ref, w2vt_ref, caux_ref, kc_ref, vct_ref):
    kind = pl.program_id(1)
    ncp = kc_ref.shape[2]
    hidden = w1_ref.shape[2]
    pieces = [src_ref[0, pl.ds(l, ncp, stride=CMP_STRIDE), :].astype(BF16) for l in range(CMP_STRIDE)]
    chunk = jnp.concatenate(pieces, axis=1)
    first = _dot(chunk, w1bd_ref[0, 0])
    second = _dot(chunk, w1bd_ref[0, 1])
    bias = _dot(pe_ref[0], w1_ref[0])[0:1, :]
    hid = first + pltpu.roll(second, ncp - 1, 0) + jnp.concatenate([bias] * NSA_KV_HEADS, axis=1)
    row = _iota((ncp, 1), 0)
    act = jnp.where(row < ncp - 1, _gelu_tanh(hid), 0.0).astype(BF16)
    for g in range(NSA_KV_HEADS):
        act_g = act[:, g * hidden:(g + 1) * hidden]

        @pl.when(kind == 0)
        def _():
            kc_ref[0, g] = (_dot(act_g, w2k_ref[...]) + caux_ref[...]).astype(BF16)

        @pl.when(kind == 1)
        def _():
            vct_ref[0, g * HEAD_DIM:(g + 1) * HEAD_DIM, :] = _dot_nt(w2vt_ref[...], act_g).astype(BF16)


def _compress(src, pe, w1, w1bd, w2k, w2vt, caux):
    b, t, _ = src.shape
    ncp = t // CMP_STRIDE
    gd = NSA_KV_HEADS * HEAD_DIM
    full = lambda a: pl.BlockSpec(a.shape, lambda bi, k: (0,) * a.ndim)
    per_kind = lambda a: pl.BlockSpec((1,) + a.shape[1:], lambda bi, k: (k,) + (0,) * (a.ndim - 1))
    return pl.pallas_call(
        _compress_kernel,
        grid=(b, 2),
        in_specs=[pl.BlockSpec((1, t, gd), lambda bi, k: (bi, 0, k)),
                  per_kind(pe), per_kind(w1), per_kind(w1bd), full(w2k), full(w2vt), full(caux)],
        out_specs=[pl.BlockSpec((1, NSA_KV_HEADS, ncp, LANES), lambda bi, k: (bi, 0, 0, 0)),
                   pl.BlockSpec((1, gd, ncp), lambda bi, k: (bi, 0, 0))],
        out_shape=[jax.ShapeDtypeStruct((b, NSA_KV_HEADS, ncp, LANES), BF16),
                   jax.ShapeDtypeStruct((b, gd, ncp), BF16)],
        compiler_params=_params("parallel", "arbitrary"),
        name="compress",
    )(src, pe, w1, w1bd, w2k, w2vt, caux)


SOFTMAX_FLOOR = -1e29
TAKEN = -3e38
ATTN_BATCH = 4
BLOCKS_PER_CHUNK = KEY_CHUNK // SLC_BLOCK
AUX_COLS = 16
AUX_SLOPE = BLOCKS_PER_CHUNK
CMP_ROWS_STEP = 256
CMP_AUX_SPLIT = 128


def _slope(g, r):
    if isinstance(g, int):
        return jnp.float32(ALIBI_SLOPES[g * NSA_GROUP + r])
    s = jnp.float32(ALIBI_SLOPES[r])
    for gi in range(1, NSA_KV_HEADS):
        s = jnp.where(g == gi, jnp.float32(ALIBI_SLOPES[gi * NSA_GROUP + r]), s)
    return s


def _nsa_cmp_kernel(qt_ref, kc_ref, vct_ref, poolt_ref, pair_ref, bits_ref, oct_ref, selt_ref, words_ref,
                    *, n_sel):
    i = pl.program_id(1)
    qt = selt_ref.shape[4]
    ncp = kc_ref.shape[2]
    n_slc = poolt_ref.shape[0]
    gq = qt_ref.shape[4]
    start = i * qt
    t = start + _iota((1, qt), 1)
    aux_row = _iota((AUX_COLS, gq), 0)
    col_head = _iota((1, gq), 1) // qt
    pad_rows = jnp.zeros((kc_ref.shape[3] - qt_ref.shape[3] - AUX_COLS, gq), BF16)

    def weights(g):
        slope_cols = jnp.zeros((1, gq), F32)
        for r in range(NSA_GROUP):
            slope_cols = jnp.where(col_head == r, _slope(g, r), slope_cols)
        aux = jnp.where(aux_row == 0, slope_cols * (CMP_STRIDE * CMP_AUX_SPLIT),
                        jnp.where(aux_row == 1, slope_cols * CMP_STRIDE, 0.0))
        return jnp.concatenate([qt_ref[0, g, 0], aux.astype(BF16), pad_rows], axis=0)

    def importance(g, nr, nb):
        s = _dot(kc_ref[0, g, :nr, :], weights(g))
        edge = min(nr, 2 * CMP_ROWS_STEP)
        cmp_end = ((nr - edge) + _iota((edge, 1), 0)) * CMP_STRIDE + (CMP_LEN - 1)
        visible = cmp_end <= t
        vct = vct_ref[0, g * HEAD_DIM:(g + 1) * HEAD_DIM, :nr]
        psum = jnp.zeros((nr, qt), F32)
        for r in range(NSA_GROUP):
            cols = slice(r * qt, (r + 1) * qt)
            sr = s[:, cols]
            tail = jnp.where(visible, sr[nr - edge:], NEG_INF)
            sr = tail if edge == nr else jnp.concatenate([sr[:nr - edge], tail], axis=0)
            m = jnp.maximum(jnp.max(sr, axis=0, keepdims=True), SOFTMAX_FLOOR)
            e = jnp.exp(sr - m)
            l = jnp.sum(e, axis=0, keepdims=True)
            inv = jnp.where(l > 0.0, 1.0 / l, 0.0)
            oct_ref[0, g, 0, :, cols] = (_dot(vct, e.astype(BF16)) * inv).astype(BF16)
            psum = psum + e * inv
        return _dot(poolt_ref[:nb, :nr], psum.astype(BF16))

    def visible_prefix(nr):
        nb = min(n_slc, nr * CMP_STRIDE // SLC_BLOCK)
        imp = jnp.concatenate([importance(g, nr, nb) for g in range(NSA_KV_HEADS)], axis=1)
        blk = _iota((nb, 1), 0)
        cur = jnp.concatenate([t // SLC_BLOCK] * NSA_KV_HEADS, axis=1)
        forced = (blk == 0) | (blk == cur) | (blk == cur - 1)
        score = jnp.where(forced, TAKEN, jnp.where(blk <= cur, imp, NEG_INF))
        n_forced = 1 + jnp.where(cur >= 1, 1, 0) + jnp.where(cur >= 2, 1, 0)
        blk_f = blk.astype(F32)

        def take_one(score, active):
            m = jnp.max(score, axis=0, keepdims=True)
            first = jnp.min(jnp.where(score == m, blk_f, F32(1e9)), axis=0, keepdims=True)
            hit = (blk_f == first) if active is None else ((blk_f == first) & active)
            return jnp.where(hit, TAKEN, score)

        common_rounds = max(n_sel - 3, 0)
        for _ in range(common_rounds):
            score = take_one(score, None)

        def early_rounds(score):
            for k in range(common_rounds, n_sel - 1):
                score = take_one(score, n_sel - n_forced > k)
            return score

        score = lax.cond(start < 2 * SLC_BLOCK, early_rounds, lambda sc: sc, score)
        for g in range(NSA_KV_HEADS):
            sel_g = score[:, g * qt:(g + 1) * qt] == TAKEN
            selt_ref[0, g, 0, :nb, :] = jnp.where(sel_g, 0.0, NEG_INF)
            if nb < n_slc:
                selt_ref[0, g, 0, nb:, :] = jnp.full((n_slc - nb, qt), NEG_INF, F32)
            count = _dot_nt(jnp.ones((8, qt), BF16), jnp.where(sel_g, 1.0, 0.0).astype(BF16))
            used = jnp.where(count > 0.0, 1.0, 0.0).astype(BF16)
            chunk_used = jnp.where(_dot(used, pair_ref[:nb, :]) > 0.0, 1.0, 0.0).astype(BF16)
            words_ref[0, g, 0] = _dot(chunk_used, bits_ref[...]).astype(jnp.int32)

    step = min(CMP_ROWS_STEP, ncp)
    rows_needed = jnp.minimum((start + qt - CMP_LEN) // CMP_STRIDE + 1, ncp)
    n_steps = (rows_needed + step - 1) // step
    for k in range(ncp // step):
        pl.when(n_steps == k + 1)(functools.partial(visible_prefix, (k + 1) * step))


def _nsa_cmp(q_t, kc, vct, poolt_m, pair_m, bits_m, n_sel):
    b, ng, n_tiles, qrows, qcols = q_t.shape
    ncp = kc.shape[2]
    n_slc = poolt_m.shape[0]
    const = lambda shape: pl.BlockSpec(shape, lambda bi, i: (0,) * len(shape))
    tile5 = lambda rows, cols: pl.BlockSpec((1, ng, 1, rows, cols), lambda bi, i: (bi, 0, i, 0, 0))
    return pl.pallas_call(
        functools.partial(_nsa_cmp_kernel, n_sel=n_sel),
        grid=(b, n_tiles),
        in_specs=[
            tile5(qrows, qcols),
            pl.BlockSpec((1, ng, ncp, kc.shape[3]), lambda bi, i: (bi, 0, 0, 0)),
            pl.BlockSpec((1, ng * HEAD_DIM, ncp), lambda bi, i: (bi, 0, 0)),
            const(poolt_m.shape), const(pair_m.shape), const(bits_m.shape),
        ],
        out_specs=[tile5(HEAD_DIM, qcols), tile5(n_slc, Q_TILE), tile5(8, LANES)],
        out_shape=[
            jax.ShapeDtypeStruct((b, ng, n_tiles, HEAD_DIM, qcols), BF16),
            jax.ShapeDtypeStruct((b, ng, n_tiles, n_slc, Q_TILE), F32),
            jax.ShapeDtypeStruct((b, ng, n_tiles, 8, LANES), jnp.int32),
        ],
        compiler_params=_params("parallel", "parallel"),
        name="nsa_compressed",
    )(q_t, kc, vct, poolt_m, pair_m, bits_m)


def _nsa_slc_kernel(words_ref, qt_ref, ks_ref, vst_ref, kw_ref, vwt_ref, selt_ref, oct_ref, gt_ref,
                    out_ref, m_ref, l_ref, acc_ref, ow_ref, list_ref, *, words_per_tile):
    bi = pl.program_id(0)
    g = pl.program_id(1)
    i = pl.program_id(2)
    n_tiles = pl.num_programs(2)
    qt = out_ref.shape[1]
    q_rows = qt_ref[0, 0, 0]
    gq = q_rows.shape[1]
    start = i * qt
    lane_f = _iota((1, qt), 1).astype(F32)
    key_in_chunk = _iota((KEY_CHUNK, qt), 0)
    query_in_tile = _iota((KEY_CHUNK, qt), 1)

    aux_row = _iota((AUX_COLS, gq), 0)
    col_head = _iota((1, gq), 1) // qt
    slope_cols = jnp.zeros((1, gq), F32)
    for r in range(NSA_GROUP):
        slope_cols = jnp.where(col_head == r, _slope(g, r), slope_cols)
    aux_base = jnp.where(aux_row == AUX_SLOPE, slope_cols, 0.0)
    pad_rows = jnp.zeros((ks_ref.shape[3] - q_rows.shape[0] - AUX_COLS, gq), BF16)

    def reset():
        m_ref[...] = jnp.full(m_ref.shape, SOFTMAX_FLOOR, F32)
        l_ref[...] = jnp.zeros(l_ref.shape, F32)
        acc_ref[...] = jnp.zeros(acc_ref.shape, F32)

    def attend(slots):
        scores = [_dot(k, jnp.concatenate([q_rows, aux.astype(BF16), pad_rows], axis=0))
                  for k, aux, _, _, _ in slots]
        v_cat = jnp.concatenate([v for _, _, v, _, _ in slots], axis=1)
        for r in range(NSA_GROUP):
            cols = slice(r * qt, (r + 1) * qt)
            srs, tops = [], []
            for (_, _, _, shift, mask), s in zip(slots, scores):
                sr = s[:, cols] if mask is None else jnp.where(mask, s[:, cols], NEG_INF)
                srs.append(sr)
                tops.append(jnp.max(sr, axis=0, keepdims=True) + shift[r])
            m_old = m_ref[r]
            m_new = functools.reduce(jnp.maximum, tops, m_old)
            alpha = jnp.exp(m_old - m_new)
            ps = [jnp.exp(sr - (m_new - slot[3][r])) for slot, sr in zip(slots, srs)]
            l_ref[r] = alpha * l_ref[r] + functools.reduce(
                jnp.add, [jnp.sum(p, axis=0, keepdims=True) for p in ps])
            p_cat = jnp.concatenate([p.astype(BF16) for p in ps], axis=0)
            acc_ref[:, cols] = alpha * acc_ref[:, cols] + _dot(v_cat, p_cat)
            m_ref[r] = m_new

    def finalize(r):
        l = l_ref[r]
        return acc_ref[:, r * qt:(r + 1) * qt] * jnp.where(l > 0.0, 1.0 / l, 0.0)

    def shifts(dist0, ok):
        rows = [-_slope(g, r) * (dist0 + lane_f) for r in range(NSA_GROUP)]
        return rows if ok is None else [jnp.where(ok, row, NEG_INF) for row in rows]

    reset()
    n_back = WINDOW // KEY_CHUNK
    slots = []
    for j in range(n_back + 1):
        cs = start - WINDOW + j * KEY_CHUNK
        chunk = jnp.maximum(cs, 0) // KEY_CHUNK
        at = pl.multiple_of(chunk * KEY_CHUNK, KEY_CHUNK)
        mask = (query_in_tile < key_in_chunk) if j == 0 else (
            (key_in_chunk <= query_in_tile) if j == n_back else None)
        slots.append((kw_ref[0, 0, pl.ds(at, KEY_CHUNK), :], aux_base, vwt_ref[0, 0, chunk],
                      shifts(F32(WINDOW - j * KEY_CHUNK), cs >= 0), mask))
    attend(slots)
    for r in range(NSA_GROUP):
        ow_ref[:, r * qt:(r + 1) * qt] = finalize(r)

    reset()
    word_base = ((bi * NSA_KV_HEADS + g) * n_tiles + i) * words_per_tile
    list_ref[0] = 0

    def scan_word(w, n):
        word = words_ref[word_base + w]

        def scan_bits(n):
            for bit in range(WORD_BITS):
                c = w * WORD_BITS + bit
                list_ref[n] = c
                n = n + jnp.where(c < i, (word >> bit) & 1, 0)
            return n

        return lax.cond(word != 0, scan_bits, lambda n: n, n)

    n_listed = lax.fori_loop(0, (i + WORD_BITS - 1) // WORD_BITS, scan_word, 0)

    def selected_slot(c, ok, mask):
        at = pl.multiple_of(c * KEY_CHUNK, KEY_CHUNK)
        bias = selt_ref[0, 0, 0, pl.ds(c * BLOCKS_PER_CHUNK, BLOCKS_PER_CHUNK), :]
        aux = aux_base
        for blk in range(BLOCKS_PER_CHUNK):
            aux = jnp.where(aux_row == blk, jnp.concatenate([bias[blk:blk + 1]] * NSA_GROUP, axis=1), aux)
        return (ks_ref[0, 0, pl.ds(at, KEY_CHUNK), :], aux, vst_ref[0, 0, c],
                shifts((start - c * KEY_CHUNK).astype(F32), ok), mask)

    def listed_slot(idx):
        ok = idx < n_listed
        c = jnp.where(ok, list_ref[jnp.minimum(idx, jnp.maximum(n_listed - 1, 0))], 0)
        return selected_slot(c, ok, None)

    attend([selected_slot(i, None, key_in_chunk <= query_in_tile)]
           + [listed_slot(j) for j in range(ATTN_BATCH - 1)])

    def batch(it, carry):
        first = ATTN_BATCH - 1 + it * ATTN_BATCH
        attend([listed_slot(first + j) for j in range(ATTN_BATCH)])
        return carry

    n_rest = jnp.maximum(n_listed - (ATTN_BATCH - 1), 0)
    lax.fori_loop(0, (n_rest + ATTN_BATCH - 1) // ATTN_BATCH, batch, 0)

    outs = []
    for r in range(NSA_GROUP):
        cols = slice(r * qt, (r + 1) * qt)
        col = (g * NSA_GROUP + r) * NSA_BRANCHES
        gate = lambda br: jax.nn.sigmoid(gt_ref[0, pl.ds(col + br, 1), :])
        outs.append(gate(0) * oct_ref[0, 0, 0, :, cols].astype(F32) + gate(1) * finalize(r)
                    + gate(2) * ow_ref[:, cols])
    out_ref[0] = jnp.concatenate(outs, axis=0).T.astype(out_ref.dtype)


def _nsa_slc(words, q_t, ks, vst, kw, vwt, selt, oct, gates_t, words_per_tile):
    b, _, t, kw_cols = ks.shape
    _, _, n_tiles, qrows, qcols = q_t.shape
    n_slc = selt.shape[3]
    n_chunks = vst.shape[2]
    gw = NSA_GROUP * HEAD_DIM
    once = dict(pipeline_mode=pl.Buffered(1))
    k_spec = pl.BlockSpec((1, 1, t, kw_cols), lambda bi, g, i, w: (bi, g, 0, 0), **once)
    vt_spec = pl.BlockSpec((1, 1, n_chunks, HEAD_DIM, KEY_CHUNK), lambda bi, g, i, w: (bi, g, 0, 0, 0), **once)
    tile5 = lambda rows, cols: pl.BlockSpec((1, 1, 1, rows, cols), lambda bi, g, i, w: (bi, g, i, 0, 0))
    grid_spec = pltpu.PrefetchScalarGridSpec(
        num_scalar_prefetch=1,
        grid=(b, NSA_KV_HEADS, n_tiles),
        in_specs=[
            tile5(qrows, qcols),
            k_spec, vt_spec, k_spec, vt_spec,
            tile5(n_slc, Q_TILE),
            tile5(HEAD_DIM, qcols),
            pl.BlockSpec((1, GATE_PAD, Q_TILE), lambda bi, g, i, w: (bi, 0, i)),
        ],
        out_specs=pl.BlockSpec((1, Q_TILE, gw), lambda bi, g, i, w: (bi, i, g)),
        scratch_shapes=[pltpu.VMEM((NSA_GROUP, 1, Q_TILE), F32), pltpu.VMEM((NSA_GROUP, 1, Q_TILE), F32),
                        pltpu.VMEM((HEAD_DIM, qcols), F32), pltpu.VMEM((HEAD_DIM, qcols), F32),
                        pltpu.SMEM((n_chunks,), jnp.int32)],
    )
    return pl.pallas_call(
        functools.partial(_nsa_slc_kernel, words_per_tile=words_per_tile),
        grid_spec=grid_spec,
        out_shape=jax.ShapeDtypeStruct((b, t, NSA_HEADS * HEAD_DIM), BF16),
        compiler_params=_params("parallel", "parallel", "parallel"),
        name="nsa_selected_window",
    )(words, q_t, ks, vst, kw, vwt, selt, oct, gates_t)


POOL_HALO = 16
CONV_HALO = 8


def _merge_kernel(h_ref, g_ref, nsa_ref, pool_ref, pool_halo_ref, conv_ref, conv_halo_ref,
                  wmg_ref, wnsa_ref, pool_bd_ref, pool_scale_ref, wpool_ref, convw_ref, wconv_ref, wo_ref,
                  out_ref, pool_ext, conv_ext, *, seq_len):
    i = pl.program_id(0)
    tm, d = h_ref.shape
    cw = pool_ref.shape[1]
    pos0 = (i * tm) % seq_len
    keep_halo = jnp.where(pos0 == 0, 0.0, 1.0)
    pos = pos0 + _iota((tm, 1), 0)

    u = pool_ref[...]
    pool_ext[0:POOL_HALO, :] = pool_halo_ref[...] * keep_halo
    pool_ext[POOL_HALO:, :] = u
    lane_group = _iota((1, cw), 1) // (cw // len(POOL_WINDOWS))
    total = u
    mean = jnp.zeros_like(u)
    done = 1
    for gi, win in enumerate(POOL_WINDOWS):
        for k in range(done, win):
            total = total + pool_ext[POOL_HALO - k:POOL_HALO - k + tm, :]
        done = win
        cnt = jnp.minimum(pos + 1, win).astype(F32)
        mean = jnp.where(lane_group == gi, total / cnt, mean)
    pooled = (mean - u).astype(BF16)
    mixed = _dot(pooled, pool_bd_ref[...]) * pool_scale_ref[...]
    y_pool = _dot(mixed.astype(BF16), wpool_ref[...])

    ch = conv_ref[:, 0:cw]
    cb = conv_ref[:, cw:2 * cw]
    cc = conv_ref[:, 2 * cw:3 * cw]
    conv_ext[0:CONV_HALO, :] = conv_halo_ref[:, 0:cw] * conv_halo_ref[:, 2 * cw:3 * cw] * keep_halo
    conv_ext[CONV_HALO:, :] = cc * ch
    y = jnp.zeros((tm, cw), F32)
    for k in range(CONV_K):
        off = CONV_HALO - (CONV_K - 1) + k
        y = y + convw_ref[k:k + 1, :] * conv_ext[off:off + tm, :]
    y_conv = _dot((cb * y).astype(BF16), wconv_ref[...])

    y_nsa = _dot(nsa_ref[...], wnsa_ref[...])

    h = h_ref[...]
    xn = _rms_norm(h, g_ref[...]).astype(BF16)
    merged = jnp.zeros((tm, d), F32)
    for br, y_br in enumerate((y_nsa, y_pool, y_conv)):
        mg = jax.nn.sigmoid(_dot(xn, wmg_ref[:, br * d:(br + 1) * d]))
        merged = merged + mg * y_br
    out_ref[...] = h + _dot(merged.astype(BF16), wo_ref[...])


def _merge(h, g, nsa, pool_u, conv, wmg, wnsa, pool_bd, pool_scale, wpool, convw, wconv, wo, seq_len):
    n, d = h.shape
    tm = TOKEN_TILE
    cw = pool_u.shape[1]
    row = lambda width: pl.BlockSpec((tm, width), lambda i: (i, 0))
    full = lambda a: pl.BlockSpec(a.shape, lambda i: (0,) * a.ndim)
    halo = lambda rows, width: pl.BlockSpec(
        (rows, width), lambda i: (jnp.maximum(i * (tm // rows) - 1, 0), 0))
    return pl.pallas_call(
        functools.partial(_merge_kernel, seq_len=seq_len),
        grid=(n // tm,),
        in_specs=[row(d), full(g), row(nsa.shape[1]), row(cw), halo(POOL_HALO, cw),
                  row(conv.shape[1]), halo(CONV_HALO, conv.shape[1]),
                  full(wmg), full(wnsa), full(pool_bd), full(pool_scale), full(wpool), full(convw),
                  full(wconv), full(wo)],
        out_specs=row(d),
        out_shape=jax.ShapeDtypeStruct((n, d), F32),
        scratch_shapes=[pltpu.VMEM((tm + POOL_HALO, cw), F32), pltpu.VMEM((tm + CONV_HALO, cw), F32)],
        compiler_params=_params("parallel"),
        name="merge",
    )(h, g, nsa, pool_u, pool_u, conv, conv, wmg, wnsa, pool_bd, pool_scale, wpool, convw, wconv, wo)


def _route(logits):
    lane = _iota(logits.shape, 1)
    lane_f = lane.astype(F32)
    big = F32(1e9)
    is_group = lane < N_EXPERT_GROUPS
    gl = jnp.where(is_group, logits, NEG_INF)
    g_max = jnp.max(gl, axis=1, keepdims=True)
    g_sel = jnp.min(jnp.where(gl == g_max, lane_f, big), axis=1, keepdims=True)
    g_prob = 1.0 / jnp.sum(jnp.where(is_group, jnp.exp(gl - g_max), 0.0), axis=1, keepdims=True)
    lo = N_EXPERT_GROUPS + EXPERTS_PER_GROUP * g_sel
    in_group = (lane_f >= lo) & (lane_f < lo + EXPERTS_PER_GROUP)
    el = jnp.where(in_group, logits, NEG_INF)
    v1 = jnp.max(el, axis=1, keepdims=True)
    i1 = jnp.min(jnp.where((el == v1) & in_group, lane_f, big), axis=1, keepdims=True)
    el2 = jnp.where(lane_f == i1, NEG_INF, el)
    rest = in_group & (lane_f != i1)
    v2 = jnp.max(el2, axis=1, keepdims=True)
    i2 = jnp.min(jnp.where((el2 == v2) & rest, lane_f, big), axis=1, keepdims=True)
    e2 = jnp.exp(v2 - v1)
    w1 = g_prob / (1.0 + e2)
    w2 = g_prob * e2 / (1.0 + e2)
    return jnp.where(lane_f == i1, w1, 0.0) + jnp.where(lane_f == i2, w2, 0.0), g_sel


GROUP_LANE = N_EXPERT_GROUPS + N_EXPERTS


def _router_kernel(h_ref, g_ref, wr_ref, br_ref, xn_ref, comb_ref):
    xn = _rms_norm(h_ref[...], g_ref[...])
    logits = jnp.dot(xn, wr_ref[...], preferred_element_type=F32,
                     precision=lax.Precision.HIGHEST) + br_ref[...]
    comb, g_sel = _route(logits)
    comb_ref[...] = jnp.where(_iota(comb.shape, 1) == GROUP_LANE, g_sel, comb)
    xn_ref[...] = xn.astype(BF16)


def _router(h, g, wr, br):
    n, d = h.shape
    tm = MOE_TILE
    full = lambda a: pl.BlockSpec(a.shape, lambda i: (0,) * a.ndim)
    return pl.pallas_call(
        _router_kernel,
        grid=(n // tm,),
        in_specs=[pl.BlockSpec((tm, d), lambda i: (i, 0)), full(g), full(wr), full(br)],
        out_specs=[pl.BlockSpec((tm, d), lambda i: (i, 0)), pl.BlockSpec((tm, ROUTER_PAD), lambda i: (i, 0))],
        out_shape=[jax.ShapeDtypeStruct((n, d), BF16), jax.ShapeDtypeStruct((n, ROUTER_PAD), F32)],
        compiler_params=_params("parallel"),
        name="router",
    )(h, g, wr, br)


def _experts_kernel(tile_group_ref, n_active_ref, x_ref, comb_ref, w1_ref, w3_ref, w2_ref, out_ref):
    i = pl.program_id(0)

    @pl.when(i < n_active_ref[0])
    def _():
        x = x_ref[...]
        comb = comb_ref[...]
        lane = _iota(comb.shape, 1)
        first = N_EXPERT_GROUPS + tile_group_ref[i] * EXPERTS_PER_GROUP
        for e in range(EXPERTS_PER_GROUP):
            c_e = jnp.sum(jnp.where(lane == first + e, comb, 0.0), axis=1, keepdims=True)
            a = jax.nn.silu(_dot(x, w1_ref[0, e])) * _dot(x, w3_ref[0, e]) * c_e
            y = _dot(a.astype(BF16), w2_ref[0, e])
            if e == 0:
                out_ref[...] = y
            else:
                out_ref[...] += y


def _experts(tile_group, n_active, x_sorted, comb_sorted, w1, w3, w2):
    ns, d = x_sorted.shape
    tm = MOE_SORT_TILE
    group_w = lambda w: pl.BlockSpec((1,) + w.shape[1:], lambda i, tg, na: (tg[i], 0, 0, 0))
    grid_spec = pltpu.PrefetchScalarGridSpec(
        num_scalar_prefetch=2,
        grid=(ns // tm,),
        in_specs=[
            pl.BlockSpec((tm, d), lambda i, tg, na: (i, 0)),
            pl.BlockSpec((tm, ROUTER_PAD), lambda i, tg, na: (i, 0)),
            group_w(w1), group_w(w3), group_w(w2),
        ],
        out_specs=pl.BlockSpec((tm, d), lambda i, tg, na: (i, 0)),
    )
    return pl.pallas_call(
        _experts_kernel,
        grid_spec=grid_spec,
        out_shape=jax.ShapeDtypeStruct((ns, d), F32),
        compiler_params=_params("arbitrary"),
        name="experts",
    )(tile_group, n_active, x_sorted, comb_sorted, w1, w3, w2)


def _residual_kernel(h_ref, y_ref, gf_ref, out_ref, *, final_norm):
    out = h_ref[...] + y_ref[...]
    out_ref[...] = _rms_norm(out, gf_ref[...]) if final_norm else out


def _residual(h, y, gf, final_norm):
    n, d = h.shape
    tm = MOE_TILE
    row = pl.BlockSpec((tm, d), lambda i: (i, 0))
    return pl.pallas_call(
        functools.partial(_residual_kernel, final_norm=final_norm),
        grid=(n // tm,),
        in_specs=[row, row, pl.BlockSpec(gf.shape, lambda i: (0, 0))],
        out_specs=row,
        out_shape=jax.ShapeDtypeStruct((n, d), F32),
        compiler_params=_params("parallel"),
        name="residual",
    )(h, y, gf)


def _group_sort_plan(group_id, tile):
    n = group_id.shape[0]
    n_slots = n + N_EXPERT_GROUPS * tile
    one_hot = (group_id[:, None] == jnp.arange(N_EXPERT_GROUPS)[None, :]).astype(jnp.int32)
    running = jnp.cumsum(one_hot, axis=0)
    padded = (running[-1] + tile - 1) // tile * tile
    ends = jnp.cumsum(padded)
    rank = jnp.take_along_axis(running, group_id[:, None], axis=1)[:, 0] - 1
    slot = (ends - padded)[group_id] + rank
    source = jnp.zeros((n_slots,), jnp.int32).at[slot].set(jnp.arange(n, dtype=jnp.int32))
    tile_start = jnp.arange(n_slots // tile, dtype=jnp.int32) * tile
    tile_group = jnp.minimum(jnp.searchsorted(ends, tile_start, side="right"), N_EXPERT_GROUPS - 1)
    return slot, source, tile_group.astype(jnp.int32), (ends[-1:] // tile).astype(jnp.int32)


def _moe(h, g, wr, br, w1, w3, w2, gf, final_norm):
    xn, comb = _router(h, g, wr, br)
    group_id = comb[:, GROUP_LANE].astype(jnp.int32)
    slot, source, tile_group, n_active = _group_sort_plan(group_id, MOE_SORT_TILE)
    grouped = lambda w: w.reshape((N_EXPERT_GROUPS, EXPERTS_PER_GROUP) + w.shape[1:])
    y_sorted = _experts(tile_group, n_active, jnp.take(xn, source, axis=0), jnp.take(comb, source, axis=0),
                        grouped(w1), grouped(w3), grouped(w2))
    return _residual(h, jnp.take(y_sorted, slot, axis=0), gf, final_norm)


def _selection_constants(seq_len):
    ncp = seq_len // CMP_STRIDE
    n_slc = seq_len // SLC_BLOCK
    ratio = SLC_BLOCK // CMP_STRIDE
    lead = CMP_LEN // CMP_STRIDE - 1
    c = np.arange(ncp)[:, None]
    j = np.arange(n_slc)[None, :]
    pool_m = ((c >= ratio * j - lead) & (c < ratio * j + ratio)).astype(np.float32)
    blocks_per_chunk = KEY_CHUNK // SLC_BLOCK
    n_chunks = seq_len // KEY_CHUNK
    pair_m = np.zeros((n_slc, LANES * ((n_chunks + LANES - 1) // LANES)), np.float32)
    pair_m[np.arange(n_slc), np.arange(n_slc) // blocks_per_chunk] = 1.0
    n_words = (n_chunks + WORD_BITS - 1) // WORD_BITS
    bits_m = np.zeros((pair_m.shape[1], LANES), np.float32)
    ch = np.arange(n_chunks)
    bits_m[ch, ch // WORD_BITS] = 2.0 ** (ch % WORD_BITS)
    key_aux = np.zeros((TOKEN_TILE, LANES), np.float32)
    in_chunk = np.arange(TOKEN_TILE) % KEY_CHUNK
    key_aux[np.arange(TOKEN_TILE), HEAD_DIM + in_chunk // SLC_BLOCK] = 1.0
    key_aux[:, HEAD_DIM + AUX_SLOPE] = in_chunk
    cmp_aux = np.zeros((ncp, LANES), np.float32)
    cmp_aux[:, HEAD_DIM] = np.arange(ncp) // CMP_AUX_SPLIT
    cmp_aux[:, HEAD_DIM + 1] = np.arange(ncp) % CMP_AUX_SPLIT
    as_bf16 = lambda a: jnp.asarray(a, BF16)
    return as_bf16(pool_m.T), as_bf16(pair_m), as_bf16(bits_m), jnp.asarray(key_aux), jnp.asarray(cmp_aux), n_words


def kernel(x, norm1_g, w_in, cmp_pe, cmp_w1, cmp_w2, w_nsa_proj, pool_w, pool_scale, w_pool_proj, conv_w,
           w_conv_proj, w_o, norm2_g, router_group_w, router_group_b, router_expert_w, router_expert_b,
           expert_w1, expert_w3, expert_w2, final_norm_g):
    b, t, d = x.shape
    n = b * t
    depth = w_in.shape[0]
    dq = NSA_HEADS * HEAD_DIM
    dkv = 6 * NSA_KV_HEADS * HEAD_DIM
    dgate = NSA_HEADS * NSA_BRANCHES
    cw = d // 4
    assert t % TOKEN_TILE == 0 and n % MOE_TILE == 0 and t % Q_TILE == 0
    n_slc = t // SLC_BLOCK
    n_sel = min(SLC_TOPN, n_slc)
    n_chunks16 = t // CMP_STRIDE
    kvw = NSA_KV_HEADS * HEAD_DIM
    poolt_m, pair_m, bits_m, key_aux, cmp_aux, n_words = _selection_constants(t)
    assert Q_TILE == KEY_CHUNK and AUX_SLOPE < AUX_COLS and KEY_CHUNK <= 256
    assert n_chunks16 <= 256 * CMP_AUX_SPLIT and n_chunks16 % min(CMP_ROWS_STEP, n_chunks16) == 0
    assert n_words <= LANES and pair_m.shape[1] == LANES

    h = x.reshape(n, d)
    for l in range(depth):
        wl = w_in[l]
        o_gate = dq + dkv
        o_pool = o_gate + dgate
        o_merge = o_pool + cw + 3 * cw
        kv_cols = lambda kind: wl[:, dq + kind * kvw:dq + (kind + 1) * kvw]
        wq_t = (wl[:, :dq] * (HEAD_DIM ** -0.5)).T.astype(BF16)
        wv_t = jnp.concatenate([kv_cols(3), kv_cols(5)], axis=1).T.astype(BF16)
        wg_t = jnp.pad(wl[:, o_gate:o_pool], ((0, 0), (0, GATE_PAD - dgate))).T.astype(BF16)
        no_aux = jnp.zeros((d, LANES - HEAD_DIM), F32)
        wk = jnp.concatenate([piece for kind in (2, 4) for gi in range(NSA_KV_HEADS)
                              for piece in (kv_cols(kind)[:, gi * HEAD_DIM:(gi + 1) * HEAD_DIM], no_aux)],
                             axis=1).astype(BF16)
        wn = jnp.concatenate([kv_cols(0), kv_cols(1), wl[:, o_pool:o_merge]], axis=1).astype(BF16)
        wmg = wl[:, o_merge:].astype(BF16)
        pool_bd = jax.scipy.linalg.block_diag(*[pool_w[l, gi] for gi in range(pool_w.shape[1])]).astype(BF16)
        convw = jnp.pad(conv_w[l], ((0, 8 - CONV_K), (0, 0)))
        wr = jnp.pad(jnp.concatenate([router_group_w[l], router_expert_w[l]], axis=1),
                     ((0, 0), (0, ROUTER_PAD - N_EXPERT_GROUPS - N_EXPERTS)))
        br = jnp.pad(jnp.concatenate([router_group_b[l], router_expert_b[l]]),
                     (0, ROUTER_PAD - N_EXPERT_GROUPS - N_EXPERTS))[None, :]
        pe = jnp.broadcast_to(cmp_pe[l].reshape(2, 1, CMP_LEN * HEAD_DIM), (2, 8, CMP_LEN * HEAD_DIM)).astype(BF16)
        halves = CMP_LEN // CMP_STRIDE
        w1_bd = jnp.einsum("khldc,gq->khlgdqc",
                           cmp_w1[l].reshape(2, halves, CMP_STRIDE, HEAD_DIM, CMP_HIDDEN),
                           jnp.eye(NSA_KV_HEADS, dtype=F32))
        w1_bd = w1_bd.reshape(2, halves, CMP_STRIDE * kvw, NSA_KV_HEADS * CMP_HIDDEN).astype(BF16)
        w2_k = jnp.pad(cmp_w2[l, 0], ((0, 0), (0, LANES - HEAD_DIM))).astype(BF16)
        w2_vt = cmp_w2[l, 1].T.astype(BF16)

        q_t, vst, vwt, gates_t, ks, kw, cmp_src, pool_u, conv = _inproj(
            h, norm1_g[l][None, :], wq_t, wv_t, wg_t, wk, wn, key_aux, b)
        kc_aux, vc_t = _compress(cmp_src, pe, cmp_w1[l].astype(BF16), w1_bd, w2_k, w2_vt, cmp_aux)
        oc_t, sel_t, words = _nsa_cmp(q_t, kc_aux, vc_t, poolt_m, pair_m, bits_m, n_sel)
        words = words[:, :, :, 0, :n_words].reshape(-1)
        nsa = _nsa_slc(words, q_t, ks, vst, kw, vwt, sel_t, oc_t, gates_t, n_words)
        h = _merge(h, norm1_g[l][None, :], nsa.reshape(n, dq), pool_u, conv, wmg,
                   w_nsa_proj[l].astype(BF16), pool_bd, pool_scale[l][None, :], w_pool_proj[l].astype(BF16),
                   convw, w_conv_proj[l].astype(BF16), w_o[l].astype(BF16), t)
        h = _moe(h, norm2_g[l][None, :], wr, br, expert_w1[l].astype(BF16), expert_w3[l].astype(BF16),
                 expert_w2[l].astype(BF16), final_norm_g[None, :], final_norm=(l == depth - 1))
    return h.reshape(b, t, d)
```

```python
import functools

import jax
import jax.numpy as jnp
import numpy as np
from jax import lax
from jax.experimental import pallas as pl
from jax.experimental.pallas import tpu as pltpu

F32 = jnp.float32
BF16 = jnp.bfloat16

HEAD_DIM = 64
NSA_HEADS = 8
NSA_KV_HEADS = 2
NSA_GROUP = NSA_HEADS // NSA_KV_HEADS
CMP_LEN = 32
CMP_STRIDE = 16
CMP_HIDDEN = 4 * HEAD_DIM
SLC_BLOCK = 64
SLC_TOPN = 16
WINDOW = 512
NSA_BRANCHES = 3
POOL_WINDOWS = (2, 4, 8, 16)
CONV_K = 3
N_BRANCHES = 3
N_EXPERT_GROUPS = 4
EXPERTS_PER_GROUP = 8
N_EXPERTS = N_EXPERT_GROUPS * EXPERTS_PER_GROUP
RMS_EPS = 1e-6
NEG_INF = -1e30
FORCE_SCORE = 1e30
ALIBI_SLOPES = tuple(float(2.0 ** (-8.0 * (h + 1) / NSA_HEADS)) for h in range(NSA_HEADS))

LANES = 128
VMEM_LIMIT = 56 * 1024 * 1024
TOKEN_TILE = 512
MOE_TILE = 1024
MOE_SORT_TILE = 512
Q_TILE = 128
KEY_CHUNK = 128
GATE_PAD = LANES
ROUTER_PAD = LANES
WORD_BITS = 16


def _params(*semantics):
    return pltpu.CompilerParams(dimension_semantics=semantics, vmem_limit_bytes=VMEM_LIMIT)


def _dot(a, b):
    return jnp.dot(a, b, preferred_element_type=F32)


def _dot_nt(a, b):
    return lax.dot_general(a, b, (((1,), (1,)), ((), ())), preferred_element_type=F32)


def _rms_norm(x, g):
    y = x * lax.rsqrt(jnp.mean(x * x, axis=-1, keepdims=True) + RMS_EPS)
    return y * g


def _iota(shape, dim):
    return lax.broadcasted_iota(jnp.int32, shape, dim)


def _inproj_kernel(x_ref, g_ref, wq_ref, wv_ref, wg_ref, wk_ref, wn_ref, kaux_ref,
                   q_ref, vs_ref, vw_ref, gate_ref, ks_ref, kw_ref, cmp_ref, pool_ref, conv_ref):
    xn = _rms_norm(x_ref[...], g_ref[...]).astype(BF16)
    sub_tiles = x_ref.shape[0] // Q_TILE
    q_t = _dot_nt(wq_ref[...], xn)
    for g in range(NSA_KV_HEADS):
        for j in range(sub_tiles):
            for r in range(NSA_GROUP):
                head = g * NSA_GROUP + r
                q_ref[0, g, j, :, r * Q_TILE:(r + 1) * Q_TILE] = q_t[
                    head * HEAD_DIM:(head + 1) * HEAD_DIM, j * Q_TILE:(j + 1) * Q_TILE].astype(BF16)
    v_t = _dot_nt(wv_ref[...], xn)
    k = _dot(xn, wk_ref[...])
    for branch, (v_ref, k_ref) in enumerate(((vs_ref, ks_ref), (vw_ref, kw_ref))):
        for g in range(NSA_KV_HEADS):
            slab = branch * NSA_KV_HEADS + g
            for j in range(sub_tiles):
                v_ref[0, g, j] = v_t[slab * HEAD_DIM:(slab + 1) * HEAD_DIM,
                                     j * KEY_CHUNK:(j + 1) * KEY_CHUNK].astype(BF16)
            k_ref[0, g] = (k[:, slab * LANES:(slab + 1) * LANES] + kaux_ref[...]).astype(BF16)
    gate_ref[0] = _dot_nt(wg_ref[...], xn)
    col = 0
    for ref in (cmp_ref, pool_ref, conv_ref):
        width = ref.shape[-1]
        ref[...] = _dot(xn, wn_ref[:, col:col + width]).reshape(ref.shape)
        col += width


def _inproj(h, g, wq_t, wv_t, wg_t, wk, wn, kaux, batch):
    n, d = h.shape
    t = n // batch
    tm = TOKEN_TILE
    steps = t // tm
    sub = tm // Q_TILE
    cw = d // 4
    gq = NSA_GROUP * Q_TILE
    full = lambda a: pl.BlockSpec(a.shape, lambda i: (0,) * a.ndim)
    row = lambda width: pl.BlockSpec((tm, width), lambda i: (i, 0))
    tiles = lambda rows, cols: pl.BlockSpec((1, NSA_KV_HEADS, sub, rows, cols),
                                            lambda i: (i // steps, 0, i % steps, 0, 0))
    keys = pl.BlockSpec((1, NSA_KV_HEADS, tm, LANES), lambda i: (i // steps, 0, i % steps, 0))
    sds = jax.ShapeDtypeStruct
    v_shape = sds((batch, NSA_KV_HEADS, t // KEY_CHUNK, HEAD_DIM, KEY_CHUNK), BF16)
    k_shape = sds((batch, NSA_KV_HEADS, t, LANES), BF16)
    return pl.pallas_call(
        _inproj_kernel,
        grid=(n // tm,),
        in_specs=[row(d), full(g), full(wq_t), full(wv_t), full(wg_t), full(wk), full(wn), full(kaux)],
        out_specs=[tiles(HEAD_DIM, gq), tiles(HEAD_DIM, KEY_CHUNK), tiles(HEAD_DIM, KEY_CHUNK),
                   pl.BlockSpec((1, GATE_PAD, tm), lambda i: (i // steps, 0, i % steps)),
                   keys, keys,
                   pl.BlockSpec((1, tm, 2 * NSA_KV_HEADS * HEAD_DIM), lambda i: (i // steps, i % steps, 0)),
                   row(cw), row(3 * cw)],
        out_shape=[sds((batch, NSA_KV_HEADS, t // Q_TILE, HEAD_DIM, gq), BF16), v_shape, v_shape,
                   sds((batch, GATE_PAD, t), F32), k_shape, k_shape,
                   sds((batch, t, 2 * NSA_KV_HEADS * HEAD_DIM), F32), sds((n, cw), F32), sds((n, 3 * cw), F32)],
        compiler_params=_params("parallel"),
        name="inproj",
    )(h, g, wq_t, wv_t, wg_t, wk, wn, kaux)


def _gelu_tanh(x):
    return 0.5 * x * (1.0 + jnp.tanh(0.7978845608028654 * (x + 0.044715 * x * x * x)))


def _compress_kernel(src_ref, pe_ref, w1_ref, w1bd_ref, w2k_ref, w2vt_ref, caux_ref, kc_ref, vct_ref):
    kind = pl.program_id(1)
    ncp = kc_ref.shape[2]
    hidden = w1_ref.shape[2]
    pieces = [src_ref[0, pl.ds(l, ncp, stride=CMP_STRIDE), :].astype(BF16) for l in range(CMP_STRIDE)]
    chunk = jnp.concatenate(pieces, axis=1)
    first = _dot(chunk, w1bd_ref[0, 0])
    second = _dot(chunk, w1bd_ref[0, 1])
    bias = _dot(pe_ref[0], w1_ref[0])[0:1, :]
    hid = first + pltpu.roll(second, ncp - 1, 0) + jnp.concatenate([bias] * NSA_KV_HEADS, axis=1)
    row = _iota((ncp, 1), 0)
    act = jnp.where(row < ncp - 1, _gelu_tanh(hid), 0.0).astype(BF16)
    for g in range(NSA_KV_HEADS):
        act_g = act[:, g * hidden:(g + 1) * hidden]

        @pl.when(kind == 0)
        def _():
            kc_ref[0, g] = (_dot(act_g, w2k_ref[...]) + caux_ref[...]).astype(BF16)

        @pl.when(kind == 1)
        def _():
            vct_ref[0, g * HEAD_DIM:(g + 1) * HEAD_DIM, :] = _dot_nt(w2vt_ref[...], act_g).astype(BF16)


def _compress(src, pe, w1, w1bd, w2k, w2vt, caux):
    b, t, _ = src.shape
    ncp = t // CMP_STRIDE
    gd = NSA_KV_HEADS * HEAD_DIM
    full = lambda a: pl.BlockSpec(a.shape, lambda bi, k: (0,) * a.ndim)
    per_kind = lambda a: pl.BlockSpec((1,) + a.shape[1:], lambda bi, k: (k,) + (0,) * (a.ndim - 1))
    return pl.pallas_call(
        _compress_kernel,
        grid=(b, 2),
        in_specs=[pl.BlockSpec((1, t, gd), lambda bi, k: (bi, 0, k)),
                  per_kind(pe), per_kind(w1), per_kind(w1bd), full(w2k), full(w2vt), full(caux)],
        out_specs=[pl.BlockSpec((1, NSA_KV_HEADS, ncp, LANES), lambda bi, k: (bi, 0, 0, 0)),
                   pl.BlockSpec((1, gd, ncp), lambda bi, k: (bi, 0, 0))],
        out_shape=[jax.ShapeDtypeStruct((b, NSA_KV_HEADS, ncp, LANES), BF16),
                   jax.ShapeDtypeStruct((b, gd, ncp), BF16)],
        compiler_params=_params("parallel", "arbitrary"),
        name="compress",
    )(src, pe, w1, w1bd, w2k, w2vt, caux)


SOFTMAX_FLOOR = -1e29
TAKEN = -3e38
ATTN_BATCH = 8
BLOCKS_PER_CHUNK = KEY_CHUNK // SLC_BLOCK
AUX_COLS = 16
AUX_SLOPE = BLOCKS_PER_CHUNK
ONES_ROWS = 16
CMP_ROWS_STEP = 256
CMP_AUX_SPLIT = 128


def _slope(g, r):
    if isinstance(g, int):
        return jnp.float32(ALIBI_SLOPES[g * NSA_GROUP + r])
    s = jnp.float32(ALIBI_SLOPES[r])
    for gi in range(1, NSA_KV_HEADS):
        s = jnp.where(g == gi, jnp.float32(ALIBI_SLOPES[gi * NSA_GROUP + r]), s)
    return s


def _nsa_cmp_kernel(qt_ref, kc_ref, vct_ref, poolt_ref, pair_ref, bits_ref, oct_ref, selt_ref, words_ref,
                    *, n_sel):
    i = pl.program_id(1)
    qt = selt_ref.shape[4]
    ncp = kc_ref.shape[2]
    n_slc = poolt_ref.shape[0]
    gq = qt_ref.shape[4]
    start = i * qt
    t = start + _iota((1, qt), 1)
    aux_row = _iota((AUX_COLS, gq), 0)
    col_head = _iota((1, gq), 1) // qt
    pad_rows = jnp.zeros((kc_ref.shape[3] - qt_ref.shape[3] - AUX_COLS, gq), BF16)

    def weights(g):
        slope_cols = jnp.zeros((1, gq), F32)
        for r in range(NSA_GROUP):
            slope_cols = jnp.where(col_head == r, _slope(g, r), slope_cols)
        aux = jnp.where(aux_row == 0, slope_cols * (CMP_STRIDE * CMP_AUX_SPLIT),
                        jnp.where(aux_row == 1, slope_cols * CMP_STRIDE, 0.0))
        return jnp.concatenate([qt_ref[0, g, 0], aux.astype(BF16), pad_rows], axis=0)

    def importance(g, nr, nb):
        s = _dot(kc_ref[0, g, :nr, :], weights(g))
        edge = min(nr, 2 * CMP_ROWS_STEP)
        cmp_end = ((nr - edge) + _iota((edge, 1), 0)) * CMP_STRIDE + (CMP_LEN - 1)
        visible = cmp_end <= t
        vct = vct_ref[0, g * HEAD_DIM:(g + 1) * HEAD_DIM, :nr]
        psum = jnp.zeros((nr, qt), F32)
        for r in range(NSA_GROUP):
            cols = slice(r * qt, (r + 1) * qt)
            sr = s[:, cols]
            tail = jnp.where(visible, sr[nr - edge:], NEG_INF)
            sr = tail if edge == nr else jnp.concatenate([sr[:nr - edge], tail], axis=0)
            m = jnp.maximum(jnp.max(sr, axis=0, keepdims=True), SOFTMAX_FLOOR)
            e = jnp.exp(sr - m)
            l = jnp.sum(e, axis=0, keepdims=True)
            inv = jnp.where(l > 0.0, 1.0 / l, 0.0)
            oct_ref[0, g, 0, :, cols] = (_dot(vct, e.astype(BF16)) * inv).astype(BF16)
            psum = psum + e * inv
        return _dot(poolt_ref[:nb, :nr], psum.astype(BF16))

    def visible_prefix(nr):
        nb = min(n_slc, nr * CMP_STRIDE // SLC_BLOCK)
        imp = jnp.concatenate([importance(g, nr, nb) for g in range(NSA_KV_HEADS)], axis=1)
        blk = _iota((nb, 1), 0)
        cur = jnp.concatenate([t // SLC_BLOCK] * NSA_KV_HEADS, axis=1)
        forced = (blk == 0) | (blk == cur) | (blk == cur - 1)
        score = jnp.where(forced, TAKEN, jnp.where(blk <= cur, imp, NEG_INF))
        n_forced = 1 + jnp.where(cur >= 1, 1, 0) + jnp.where(cur >= 2, 1, 0)
        blk_f = blk.astype(F32)

        def take_one(score, active):
            m = jnp.max(score, axis=0, keepdims=True)
            first = jnp.min(jnp.where(score == m, blk_f, F32(1e9)), axis=0, keepdims=True)
            hit = (blk_f == first) if active is None else ((blk_f == first) & active)
            return jnp.where(hit, TAKEN, score)

        common_rounds = max(n_sel - 3, 0)
        for _ in range(common_rounds):
            score = take_one(score, None)

        def early_rounds(score):
            for k in range(common_rounds, n_sel - 1):
                score = take_one(score, n_sel - n_forced > k)
            return score

        score = lax.cond(start < 2 * SLC_BLOCK, early_rounds, lambda sc: sc, score)
        for g in range(NSA_KV_HEADS):
            sel_g = score[:, g * qt:(g + 1) * qt] == TAKEN
            selt_ref[0, g, 0, :nb, :] = jnp.where(sel_g, 0.0, NEG_INF)
            if nb < n_slc:
                selt_ref[0, g, 0, nb:, :] = jnp.full((n_slc - nb, qt), NEG_INF, F32)
            count = _dot_nt(jnp.ones((8, qt), BF16), jnp.where(sel_g, 1.0, 0.0).astype(BF16))
            used = jnp.where(count > 0.0, 1.0, 0.0).astype(BF16)
            chunk_used = jnp.where(_dot(used, pair_ref[:nb, :]) > 0.0, 1.0, 0.0).astype(BF16)
            words_ref[0, g, 0] = _dot(chunk_used, bits_ref[...]).astype(jnp.int32)

    step = min(CMP_ROWS_STEP, ncp)
    rows_needed = jnp.minimum((start + qt - CMP_LEN) // CMP_STRIDE + 1, ncp)
    n_steps = (rows_needed + step - 1) // step
    for k in range(ncp // step):
        pl.when(n_steps == k + 1)(functools.partial(visible_prefix, (k + 1) * step))


def _nsa_cmp(q_t, kc, vct, poolt_m, pair_m, bits_m, n_sel):
    b, ng, n_tiles, qrows, qcols = q_t.shape
    ncp = kc.shape[2]
    n_slc = poolt_m.shape[0]
    const = lambda shape: pl.BlockSpec(shape, lambda bi, i: (0,) * len(shape))
    tile5 = lambda rows, cols: pl.BlockSpec((1, ng, 1, rows, cols), lambda bi, i: (bi, 0, i, 0, 0))
    return pl.pallas_call(
        functools.partial(_nsa_cmp_kernel, n_sel=n_sel),
        grid=(b, n_tiles),
        in_specs=[
            tile5(qrows, qcols),
            pl.BlockSpec((1, ng, ncp, kc.shape[3]), lambda bi, i: (bi, 0, 0, 0)),
            pl.BlockSpec((1, ng * HEAD_DIM, ncp), lambda bi, i: (bi, 0, 0)),
            const(poolt_m.shape), const(pair_m.shape), const(bits_m.shape),
        ],
        out_specs=[tile5(HEAD_DIM, qcols), tile5(n_slc, Q_TILE), tile5(8, LANES)],
        out_shape=[
            jax.ShapeDtypeStruct((b, ng, n_tiles, HEAD_DIM, qcols), BF16),
            jax.ShapeDtypeStruct((b, ng, n_tiles, n_slc, Q_TILE), F32),
            jax.ShapeDtypeStruct((b, ng, n_tiles, 8, LANES), jnp.int32),
        ],
        compiler_params=_params("parallel", "parallel"),
        name="nsa_compressed",
    )(q_t, kc, vct, poolt_m, pair_m, bits_m)


def _nsa_slc_kernel(words_ref, qt_ref, ks_ref, vst_ref, kw_ref, vwt_ref, selt_ref, oct_ref, gt_ref,
                    out_ref, m_ref, l_ref, acc_ref, ow_ref, list_ref, *, words_per_tile):
    bi = pl.program_id(0)
    g = pl.program_id(1)
    i = pl.program_id(2)
    n_tiles = pl.num_programs(2)
    qt = out_ref.shape[1]
    q_rows = qt_ref[0, 0, 0]
    gq = q_rows.shape[1]
    start = i * qt
    lane_f = _iota((1, qt), 1).astype(F32)
    key_in_chunk = _iota((KEY_CHUNK, qt), 0)
    query_in_tile = _iota((KEY_CHUNK, qt), 1)

    aux_row = _iota((AUX_COLS, gq), 0)
    col_head = _iota((1, gq), 1) // qt
    slope_cols = jnp.zeros((1, gq), F32)
    for r in range(NSA_GROUP):
        slope_cols = jnp.where(col_head == r, _slope(g, r), slope_cols)
    aux_base = jnp.where(aux_row == AUX_SLOPE, slope_cols, 0.0)
    pad_rows = jnp.zeros((ks_ref.shape[3] - q_rows.shape[0] - AUX_COLS, gq), BF16)

    def scores_and_values(slots):
        scores = [_dot(k, jnp.concatenate([q_rows, aux.astype(BF16), pad_rows], axis=0))
                  for k, aux, _, _, _ in slots]
        values = jnp.concatenate([v for _, _, v, _, _ in slots], axis=1)
        return scores, jnp.concatenate([values, jnp.ones((ONES_ROWS, values.shape[1]), BF16)], axis=0)

    def softmax_step(slots, scores, v_cat, r, m_old):
        cols = slice(r * qt, (r + 1) * qt)
        srs, tops = [], []
        for (_, _, _, shift, mask), s in zip(slots, scores):
            sr = s[:, cols] if mask is None else jnp.where(mask, s[:, cols], NEG_INF)
            srs.append(sr)
            tops.append(jnp.max(sr, axis=0, keepdims=True) + shift[r])
        m_new = functools.reduce(jnp.maximum, tops, m_old)
        ps = [jnp.exp((sr - (m_new - slot[3][r])).astype(BF16)) for slot, sr in zip(slots, srs)]
        weighted = _dot(v_cat, jnp.concatenate(ps, axis=0))
        return m_new, weighted[HEAD_DIM:HEAD_DIM + 1], weighted[:HEAD_DIM]

    def shifts(dist0, ok):
        rows = [-_slope(g, r) * (dist0 + lane_f) for r in range(NSA_GROUP)]
        return rows if ok is None else [jnp.where(ok, row, NEG_INF) for row in rows]

    word_base = ((bi * NSA_KV_HEADS + g) * n_tiles + i) * words_per_tile
    list_ref[0] = 0

    def scan_word(w, n):
        word = words_ref[word_base + w]

        def scan_bits(n):
            for bit in range(WORD_BITS):
                c = w * WORD_BITS + bit
                list_ref[n] = c
                n = n + jnp.where(c < i, (word >> bit) & 1, 0)
            return n

        return lax.cond(word != 0, scan_bits, lambda n: n, n)

    n_listed = lax.fori_loop(0, (i + WORD_BITS - 1) // WORD_BITS, scan_word, 0)

    def selected_slot(c, ok, mask):
        at = pl.multiple_of(c * KEY_CHUNK, KEY_CHUNK)
        bias = selt_ref[0, 0, 0, pl.ds(c * BLOCKS_PER_CHUNK, BLOCKS_PER_CHUNK), :]
        aux = aux_base
        for blk in range(BLOCKS_PER_CHUNK):
            aux = jnp.where(aux_row == blk, jnp.concatenate([bias[blk:blk + 1]] * NSA_GROUP, axis=1), aux)
        return (ks_ref[0, 0, pl.ds(at, KEY_CHUNK), :], aux, vst_ref[0, 0, c],
                shifts((start - c * KEY_CHUNK).astype(F32), ok), mask)

    def listed_slot(idx):
        ok = idx < n_listed
        c = jnp.where(ok, list_ref[jnp.minimum(idx, jnp.maximum(n_listed - 1, 0))], 0)
        return selected_slot(c, ok, None)

    floor = jnp.full((1, qt), SOFTMAX_FLOOR, F32)

    n_back = WINDOW // KEY_CHUNK
    slots = []
    for j in range(n_back + 1):
        cs = start - WINDOW + j * KEY_CHUNK
        chunk = jnp.maximum(cs, 0) // KEY_CHUNK
        at = pl.multiple_of(chunk * KEY_CHUNK, KEY_CHUNK)
        mask = (query_in_tile < key_in_chunk) if j == 0 else (
            (key_in_chunk <= query_in_tile) if j == n_back else None)
        slots.append((kw_ref[0, 0, pl.ds(at, KEY_CHUNK), :], aux_base, vwt_ref[0, 0, chunk],
                      shifts(F32(WINDOW - j * KEY_CHUNK), cs >= 0), mask))
    scores, v_cat = scores_and_values(slots)
    for r in range(NSA_GROUP):
        _, total, weighted = softmax_step(slots, scores, v_cat, r, floor)
        ow_ref[:, r * qt:(r + 1) * qt] = weighted * jnp.where(total > 0.0, 1.0 / total, 0.0)

    slots = [selected_slot(i, None, key_in_chunk <= query_in_tile)] + [
        listed_slot(j) for j in range(ATTN_BATCH - 1)]
    scores, v_cat = scores_and_values(slots)
    for r in range(NSA_GROUP):
        m_ref[r], l_ref[r], acc_ref[:, r * qt:(r + 1) * qt] = softmax_step(slots, scores, v_cat, r, floor)

    def batch(it, carry):
        first = ATTN_BATCH - 1 + it * ATTN_BATCH
        slots = [listed_slot(first + j) for j in range(ATTN_BATCH)]
        scores, v_cat = scores_and_values(slots)
        for r in range(NSA_GROUP):
            cols = slice(r * qt, (r + 1) * qt)
            m_old = m_ref[r]
            m_new, total, weighted = softmax_step(slots, scores, v_cat, r, m_old)
            alpha = jnp.exp(m_old - m_new)
            l_ref[r] = alpha * l_ref[r] + total
            acc_ref[:, cols] = alpha * acc_ref[:, cols] + weighted
            m_ref[r] = m_new
        return carry

    n_rest = jnp.maximum(n_listed - (ATTN_BATCH - 1), 0)
    lax.fori_loop(0, (n_rest + ATTN_BATCH - 1) // ATTN_BATCH, batch, 0)

    def finalize(r):
        l = l_ref[r]
        return acc_ref[:, r * qt:(r + 1) * qt] * jnp.where(l > 0.0, 1.0 / l, 0.0)

    outs = []
    for r in range(NSA_GROUP):
        cols = slice(r * qt, (r + 1) * qt)
        col = (g * NSA_GROUP + r) * NSA_BRANCHES
        gate = lambda br: jax.nn.sigmoid(gt_ref[0, pl.ds(col + br, 1), :])
        outs.append(gate(0) * oct_ref[0, 0, 0, :, cols].astype(F32) + gate(1) * finalize(r)
                    + gate(2) * ow_ref[:, cols])
    out_ref[0] = jnp.concatenate(outs, axis=0).T.astype(out_ref.dtype)


def _nsa_slc(words, q_t, ks, vst, kw, vwt, selt, oct, gates_t, words_per_tile):
    b, _, t, kw_cols = ks.shape
    _, _, n_tiles, qrows, qcols = q_t.shape
    n_slc = selt.shape[3]
    n_chunks = vst.shape[2]
    gw = NSA_GROUP * HEAD_DIM
    once = dict(pipeline_mode=pl.Buffered(1))
    k_spec = pl.BlockSpec((1, 1, t, kw_cols), lambda bi, g, i, w: (bi, g, 0, 0), **once)
    vt_spec = pl.BlockSpec((1, 1, n_chunks, HEAD_DIM, KEY_CHUNK), lambda bi, g, i, w: (bi, g, 0, 0, 0), **once)
    tile5 = lambda rows, cols: pl.BlockSpec((1, 1, 1, rows, cols), lambda bi, g, i, w: (bi, g, i, 0, 0))
    grid_spec = pltpu.PrefetchScalarGridSpec(
        num_scalar_prefetch=1,
        grid=(b, NSA_KV_HEADS, n_tiles),
        in_specs=[
            tile5(qrows, qcols),
            k_spec, vt_spec, k_spec, vt_spec,
            tile5(n_slc, Q_TILE),
            tile5(HEAD_DIM, qcols),
            pl.BlockSpec((1, GATE_PAD, Q_TILE), lambda bi, g, i, w: (bi, 0, i)),
        ],
        out_specs=pl.BlockSpec((1, Q_TILE, gw), lambda bi, g, i, w: (bi, i, g)),
        scratch_shapes=[pltpu.VMEM((NSA_GROUP, 1, Q_TILE), F32), pltpu.VMEM((NSA_GROUP, 1, Q_TILE), F32),
                        pltpu.VMEM((HEAD_DIM, qcols), F32), pltpu.VMEM((HEAD_DIM, qcols), F32),
                        pltpu.SMEM((n_chunks,), jnp.int32)],
    )
    return pl.pallas_call(
        functools.partial(_nsa_slc_kernel, words_per_tile=words_per_tile),
        grid_spec=grid_spec,
        out_shape=jax.ShapeDtypeStruct((b, t, NSA_HEADS * HEAD_DIM), BF16),
        compiler_params=_params("parallel", "parallel", "parallel"),
        name="nsa_selected_window",
    )(words, q_t, ks, vst, kw, vwt, selt, oct, gates_t)


POOL_HALO = 16
CONV_HALO = 8


def _merge_kernel(h_ref, g_ref, nsa_ref, pool_ref, pool_halo_ref, conv_ref, conv_halo_ref,
                  wmg_ref, wnsa_ref, pool_bd_ref, pool_scale_ref, wpool_ref, convw_ref, wconv_ref, wo_ref,
                  out_ref, pool_ext, conv_ext, *, seq_len):
    i = pl.program_id(0)
    tm, d = h_ref.shape
    cw = pool_ref.shape[1]
    pos0 = (i * tm) % seq_len
    keep_halo = jnp.where(pos0 == 0, 0.0, 1.0)
    pos = pos0 + _iota((tm, 1), 0)

    u = pool_ref[...]
    pool_ext[0:POOL_HALO, :] = pool_halo_ref[...] * keep_halo
    pool_ext[POOL_HALO:, :] = u
    lane_group = _iota((1, cw), 1) // (cw // len(POOL_WINDOWS))
    total = u
    mean = jnp.zeros_like(u)
    done = 1
    for gi, win in enumerate(POOL_WINDOWS):
        for k in range(done, win):
            total = total + pool_ext[POOL_HALO - k:POOL_HALO - k + tm, :]
        done = win
        cnt = jnp.minimum(pos + 1, win).astype(F32)
        mean = jnp.where(lane_group == gi, total / cnt, mean)
    pooled = (mean - u).astype(BF16)
    mixed = _dot(pooled, pool_bd_ref[...]) * pool_scale_ref[...]
    y_pool = _dot(mixed.astype(BF16), wpool_ref[...])

    ch = conv_ref[:, 0:cw]
    cb = conv_ref[:, cw:2 * cw]
    cc = conv_ref[:, 2 * cw:3 * cw]
    conv_ext[0:CONV_HALO, :] = conv_halo_ref[:, 0:cw] * conv_halo_ref[:, 2 * cw:3 * cw] * keep_halo
    conv_ext[CONV_HALO:, :] = cc * ch
    y = jnp.zeros((tm, cw), F32)
    for k in range(CONV_K):
        off = CONV_HALO - (CONV_K - 1) + k
        y = y + convw_ref[k:k + 1, :] * conv_ext[off:off + tm, :]
    y_conv = _dot((cb * y).astype(BF16), wconv_ref[...])

    y_nsa = _dot(nsa_ref[...], wnsa_ref[...])

    h = h_ref[...]
    xn = _rms_norm(h, g_ref[...]).astype(BF16)
    merged = jnp.zeros((tm, d), F32)
    for br, y_br in enumerate((y_nsa, y_pool, y_conv)):
        mg = jax.nn.sigmoid(_dot(xn, wmg_ref[:, br * d:(br + 1) * d]))
        merged = merged + mg * y_br
    out_ref[...] = h + _dot(merged.astype(BF16), wo_ref[...])


def _merge(h, g, nsa, pool_u, conv, wmg, wnsa, pool_bd, pool_scale, wpool, convw, wconv, wo, seq_len):
    n, d = h.shape
    tm = TOKEN_TILE
    cw = pool_u.shape[1]
    row = lambda width: pl.BlockSpec((tm, width), lambda i: (i, 0))
    full = lambda a: pl.BlockSpec(a.shape, lambda i: (0,) * a.ndim)
    halo = lambda rows, width: pl.BlockSpec(
        (rows, width), lambda i: (jnp.maximum(i * (tm // rows) - 1, 0), 0))
    return pl.pallas_call(
        functools.partial(_merge_kernel, seq_len=seq_len),
        grid=(n // tm,),
        in_specs=[row(d), full(g), row(nsa.shape[1]), row(cw), halo(POOL_HALO, cw),
                  row(conv.shape[1]), halo(CONV_HALO, conv.shape[1]),
                  full(wmg), full(wnsa), full(pool_bd), full(pool_scale), full(wpool), full(convw),
                  full(wconv), full(wo)],
        out_specs=row(d),
        out_shape=jax.ShapeDtypeStruct((n, d), F32),
        scratch_shapes=[pltpu.VMEM((tm + POOL_HALO, cw), F32), pltpu.VMEM((tm + CONV_HALO, cw), F32)],
        compiler_params=_params("parallel"),
        name="merge",
    )(h, g, nsa, pool_u, pool_u, conv, conv, wmg, wnsa, pool_bd, pool_scale, wpool, convw, wconv, wo)


def _route(logits):
    lane = _iota(logits.shape, 1)
    lane_f = lane.astype(F32)
    big = F32(1e9)
    is_group = lane < N_EXPERT_GROUPS
    gl = jnp.where(is_group, logits, NEG_INF)
    g_max = jnp.max(gl, axis=1, keepdims=True)
    g_sel = jnp.min(jnp.where(gl == g_max, lane_f, big), axis=1, keepdims=True)
    g_prob = 1.0 / jnp.sum(jnp.where(is_group, jnp.exp(gl - g_max), 0.0), axis=1, keepdims=True)
    lo = N_EXPERT_GROUPS + EXPERTS_PER_GROUP * g_sel
    in_group = (lane_f >= lo) & (lane_f < lo + EXPERTS_PER_GROUP)
    el = jnp.where(in_group, logits, NEG_INF)
    v1 = jnp.max(el, axis=1, keepdims=True)
    i1 = jnp.min(jnp.where((el == v1) & in_group, lane_f, big), axis=1, keepdims=True)
    el2 = jnp.where(lane_f == i1, NEG_INF, el)
    rest = in_group & (lane_f != i1)
    v2 = jnp.max(el2, axis=1, keepdims=True)
    i2 = jnp.min(jnp.where((el2 == v2) & rest, lane_f, big), axis=1, keepdims=True)
    e2 = jnp.exp(v2 - v1)
    w1 = g_prob / (1.0 + e2)
    w2 = g_prob * e2 / (1.0 + e2)
    return jnp.where(lane_f == i1, w1, 0.0) + jnp.where(lane_f == i2, w2, 0.0), g_sel


GROUP_LANE = N_EXPERT_GROUPS + N_EXPERTS


RANK_LANE = GROUP_LANE + 1


def _router_kernel(h_ref, g_ref, wr_ref, br_ref, before_ref, xn_ref, comb_ref, count_ref):
    @pl.when(pl.program_id(0) == 0)
    def _():
        count_ref[...] = jnp.zeros(count_ref.shape, F32)

    xn = _rms_norm(h_ref[...], g_ref[...])
    logits = jnp.dot(xn, wr_ref[...], preferred_element_type=F32,
                     precision=lax.Precision.HIGHEST) + br_ref[...]
    comb, g_sel = _route(logits)
    lane = _iota(comb.shape, 1)
    chose = jnp.where(lane.astype(F32) == g_sel, 1.0, 0.0)
    earlier = _dot(before_ref[...], chose.astype(BF16)) + count_ref[0:1, :]
    rank = jnp.sum(chose * earlier, axis=1, keepdims=True)
    count_ref[0:1, :] = count_ref[0:1, :] + jnp.sum(chose, axis=0, keepdims=True)
    comb_ref[...] = jnp.where(lane == GROUP_LANE, g_sel, jnp.where(lane == RANK_LANE, rank, comb))
    xn_ref[...] = xn.astype(BF16)


def _router(h, g, wr, br, before):
    n, d = h.shape
    tm = MOE_TILE
    full = lambda a: pl.BlockSpec(a.shape, lambda i: (0,) * a.ndim)
    return pl.pallas_call(
        _router_kernel,
        grid=(n // tm,),
        in_specs=[pl.BlockSpec((tm, d), lambda i: (i, 0)), full(g), full(wr), full(br), full(before)],
        out_specs=[pl.BlockSpec((tm, d), lambda i: (i, 0)), pl.BlockSpec((tm, ROUTER_PAD), lambda i: (i, 0)),
                   pl.BlockSpec((8, ROUTER_PAD), lambda i: (0, 0))],
        out_shape=[jax.ShapeDtypeStruct((n, d), BF16), jax.ShapeDtypeStruct((n, ROUTER_PAD), F32),
                   jax.ShapeDtypeStruct((8, ROUTER_PAD), F32)],
        compiler_params=_params("arbitrary"),
        name="router",
    )(h, g, wr, br, before)


def _experts_kernel(tile_group_ref, n_active_ref, x_ref, comb_ref, w1_ref, w3_ref, w2_ref, out_ref):
    i = pl.program_id(0)

    @pl.when(i < n_active_ref[0])
    def _():
        x = x_ref[...]
        comb = comb_ref[...]
        lane = _iota(comb.shape, 1)
        first = N_EXPERT_GROUPS + tile_group_ref[i] * EXPERTS_PER_GROUP
        for e in range(EXPERTS_PER_GROUP):
            c_e = jnp.sum(jnp.where(lane == first + e, comb, 0.0), axis=1, keepdims=True)
            a = (jax.nn.silu(_dot(x, w1_ref[0, e].astype(BF16))) * _dot(x, w3_ref[0, e].astype(BF16))) * c_e
            y = _dot(a.astype(BF16), w2_ref[0, e].astype(BF16))
            if e == 0:
                out_ref[...] = y
            else:
                out_ref[...] += y


def _experts(tile_group, n_active, x_sorted, comb_sorted, w1, w3, w2):
    ns, d = x_sorted.shape
    tm = MOE_SORT_TILE
    group_w = lambda w: pl.BlockSpec((1,) + w.shape[1:], lambda i, tg, na: (tg[i], 0, 0, 0),
                                     pipeline_mode=pl.Buffered(1))
    grid_spec = pltpu.PrefetchScalarGridSpec(
        num_scalar_prefetch=2,
        grid=(ns // tm,),
        in_specs=[
            pl.BlockSpec((tm, d), lambda i, tg, na: (i, 0)),
            pl.BlockSpec((tm, ROUTER_PAD), lambda i, tg, na: (i, 0)),
            group_w(w1), group_w(w3), group_w(w2),
        ],
        out_specs=pl.BlockSpec((tm, d), lambda i, tg, na: (i, 0)),
    )
    return pl.pallas_call(
        _experts_kernel,
        grid_spec=grid_spec,
        out_shape=jax.ShapeDtypeStruct((ns, d), F32),
        compiler_params=_params("arbitrary"),
        name="experts",
    )(tile_group, n_active, x_sorted, comb_sorted, w1, w3, w2)


def _residual_kernel(h_ref, y_ref, gf_ref, out_ref, *, final_norm):
    out = h_ref[...] + y_ref[...]
    out_ref[...] = _rms_norm(out, gf_ref[...]) if final_norm else out


def _residual(h, y, gf, final_norm):
    n, d = h.shape
    tm = MOE_TILE
    row = pl.BlockSpec((tm, d), lambda i: (i, 0))
    return pl.pallas_call(
        functools.partial(_residual_kernel, final_norm=final_norm),
        grid=(n // tm,),
        in_specs=[row, row, pl.BlockSpec(gf.shape, lambda i: (0, 0))],
        out_specs=row,
        out_shape=jax.ShapeDtypeStruct((n, d), F32),
        compiler_params=_params("parallel"),
        name="residual",
    )(h, y, gf)


def _group_sort_plan(group_id, rank, counts, tile):
    n = group_id.shape[0]
    n_slots = n + N_EXPERT_GROUPS * tile
    padded = (counts + tile - 1) // tile * tile
    ends = jnp.cumsum(padded)
    slot = (ends - padded)[group_id] + rank
    source = jnp.zeros((n_slots,), jnp.int32).at[slot].set(jnp.arange(n, dtype=jnp.int32))
    tile_start = jnp.arange(n_slots // tile, dtype=jnp.int32) * tile
    tile_group = jnp.minimum(jnp.searchsorted(ends, tile_start, side="right"), N_EXPERT_GROUPS - 1)
    return slot, source, tile_group.astype(jnp.int32), (ends[-1:] // tile).astype(jnp.int32)


def _moe(h, g, wr, br, before, w1, w3, w2, gf, final_norm):
    xn, comb, counts = _router(h, g, wr, br, before)
    as_int = lambda a: a.astype(jnp.int32)
    slot, source, tile_group, n_active = _group_sort_plan(
        as_int(comb[:, GROUP_LANE]), as_int(comb[:, RANK_LANE]), as_int(counts[0, :N_EXPERT_GROUPS]),
        MOE_SORT_TILE)
    grouped = lambda w: w.reshape((N_EXPERT_GROUPS, EXPERTS_PER_GROUP) + w.shape[1:])
    y_sorted = _experts(tile_group, n_active, jnp.take(xn, source, axis=0), jnp.take(comb, source, axis=0),
                        grouped(w1), grouped(w3), grouped(w2))
    return _residual(h, jnp.take(y_sorted, slot, axis=0), gf, final_norm)


def _selection_constants(seq_len):
    ncp = seq_len // CMP_STRIDE
    n_slc = seq_len // SLC_BLOCK
    ratio = SLC_BLOCK // CMP_STRIDE
    lead = CMP_LEN // CMP_STRIDE - 1
    c = np.arange(ncp)[:, None]
    j = np.arange(n_slc)[None, :]
    pool_m = ((c >= ratio * j - lead) & (c < ratio * j + ratio)).astype(np.float32)
    blocks_per_chunk = KEY_CHUNK // SLC_BLOCK
    n_chunks = seq_len // KEY_CHUNK
    pair_m = np.zeros((n_slc, LANES * ((n_chunks + LANES - 1) // LANES)), np.float32)
    pair_m[np.arange(n_slc), np.arange(n_slc) // blocks_per_chunk] = 1.0
    n_words = (n_chunks + WORD_BITS - 1) // WORD_BITS
    bits_m = np.zeros((pair_m.shape[1], LANES), np.float32)
    ch = np.arange(n_chunks)
    bits_m[ch, ch // WORD_BITS] = 2.0 ** (ch % WORD_BITS)
    key_aux = np.zeros((TOKEN_TILE, LANES), np.float32)
    in_chunk = np.arange(TOKEN_TILE) % KEY_CHUNK
    key_aux[np.arange(TOKEN_TILE), HEAD_DIM + in_chunk // SLC_BLOCK] = 1.0
    key_aux[:, HEAD_DIM + AUX_SLOPE] = in_chunk
    cmp_aux = np.zeros((ncp, LANES), np.float32)
    cmp_aux[:, HEAD_DIM] = np.arange(ncp) // CMP_AUX_SPLIT
    cmp_aux[:, HEAD_DIM + 1] = np.arange(ncp) % CMP_AUX_SPLIT
    as_bf16 = lambda a: jnp.asarray(a, BF16)
    return as_bf16(pool_m.T), as_bf16(pair_m), as_bf16(bits_m), jnp.asarray(key_aux), jnp.asarray(cmp_aux), n_words


def kernel(x, norm1_g, w_in, cmp_pe, cmp_w1, cmp_w2, w_nsa_proj, pool_w, pool_scale, w_pool_proj, conv_w,
           w_conv_proj, w_o, norm2_g, router_group_w, router_group_b, router_expert_w, router_expert_b,
           expert_w1, expert_w3, expert_w2, final_norm_g):
    b, t, d = x.shape
    n = b * t
    depth = w_in.shape[0]
    dq = NSA_HEADS * HEAD_DIM
    dkv = 6 * NSA_KV_HEADS * HEAD_DIM
    dgate = NSA_HEADS * NSA_BRANCHES
    cw = d // 4
    assert t % TOKEN_TILE == 0 and n % MOE_TILE == 0 and t % Q_TILE == 0
    n_slc = t // SLC_BLOCK
    n_sel = min(SLC_TOPN, n_slc)
    n_chunks16 = t // CMP_STRIDE
    kvw = NSA_KV_HEADS * HEAD_DIM
    poolt_m, pair_m, bits_m, key_aux, cmp_aux, n_words = _selection_constants(t)
    assert Q_TILE == KEY_CHUNK and AUX_SLOPE < AUX_COLS and KEY_CHUNK <= 256
    assert n_chunks16 <= 256 * CMP_AUX_SPLIT and n_chunks16 % min(CMP_ROWS_STEP, n_chunks16) == 0
    assert n_words <= LANES and pair_m.shape[1] == LANES

    before = jnp.asarray(np.tril(np.ones((MOE_TILE, MOE_TILE), np.float32), -1), BF16)
    h = x.reshape(n, d)
    for l in range(depth):
        wl = w_in[l]
        o_gate = dq + dkv
        o_pool = o_gate + dgate
        o_merge = o_pool + cw + 3 * cw
        kv_cols = lambda kind: wl[:, dq + kind * kvw:dq + (kind + 1) * kvw]
        wq_t = (wl[:, :dq] * (HEAD_DIM ** -0.5)).T.astype(BF16)
        wv_t = jnp.concatenate([kv_cols(3), kv_cols(5)], axis=1).T.astype(BF16)
        wg_t = jnp.pad(wl[:, o_gate:o_pool], ((0, 0), (0, GATE_PAD - dgate))).T.astype(BF16)
        no_aux = jnp.zeros((d, LANES - HEAD_DIM), F32)
        wk = jnp.concatenate([piece for kind in (2, 4) for gi in range(NSA_KV_HEADS)
                              for piece in (kv_cols(kind)[:, gi * HEAD_DIM:(gi + 1) * HEAD_DIM], no_aux)],
                             axis=1).astype(BF16)
        wn = jnp.concatenate([kv_cols(0), kv_cols(1), wl[:, o_pool:o_merge]], axis=1).astype(BF16)
        wmg = wl[:, o_merge:].astype(BF16)
        pool_bd = jax.scipy.linalg.block_diag(*[pool_w[l, gi] for gi in range(pool_w.shape[1])]).astype(BF16)
        convw = jnp.pad(conv_w[l], ((0, 8 - CONV_K), (0, 0)))
        wr = jnp.pad(jnp.concatenate([router_group_w[l], router_expert_w[l]], axis=1),
                     ((0, 0), (0, ROUTER_PAD - N_EXPERT_GROUPS - N_EXPERTS)))
        br = jnp.pad(jnp.concatenate([router_group_b[l], router_expert_b[l]]),
                     (0, ROUTER_PAD - N_EXPERT_GROUPS - N_EXPERTS))[None, :]
        pe = jnp.broadcast_to(cmp_pe[l].reshape(2, 1, CMP_LEN * HEAD_DIM), (2, 8, CMP_LEN * HEAD_DIM)).astype(BF16)
        halves = CMP_LEN // CMP_STRIDE
        w1_bd = jnp.einsum("khldc,gq->khlgdqc",
                           cmp_w1[l].reshape(2, halves, CMP_STRIDE, HEAD_DIM, CMP_HIDDEN),
                           jnp.eye(NSA_KV_HEADS, dtype=F32))
        w1_bd = w1_bd.reshape(2, halves, CMP_STRIDE * kvw, NSA_KV_HEADS * CMP_HIDDEN).astype(BF16)
        w2_k = jnp.pad(cmp_w2[l, 0], ((0, 0), (0, LANES - HEAD_DIM))).astype(BF16)
        w2_vt = cmp_w2[l, 1].T.astype(BF16)

        q_t, vst, vwt, gates_t, ks, kw, cmp_src, pool_u, conv = _inproj(
            h, norm1_g[l][None, :], wq_t, wv_t, wg_t, wk, wn, key_aux, b)
        kc_aux, vc_t = _compress(cmp_src, pe, cmp_w1[l].astype(BF16), w1_bd, w2_k, w2_vt, cmp_aux)
        oc_t, sel_t, words = _nsa_cmp(q_t, kc_aux, vc_t, poolt_m, pair_m, bits_m, n_sel)
        words = words[:, :, :, 0, :n_words].reshape(-1)
        nsa = _nsa_slc(words, q_t, ks, vst, kw, vwt, sel_t, oc_t, gates_t, n_words)
        h = _merge(h, norm1_g[l][None, :], nsa.reshape(n, dq), pool_u, conv, wmg,
                   w_nsa_proj[l].astype(BF16), pool_bd, pool_scale[l][None, :], w_pool_proj[l].astype(BF16),
                   convw, w_conv_proj[l].astype(BF16), w_o[l].astype(BF16), t)
        h = _moe(h, norm2_g[l][None, :], wr, br, before, expert_w1[l], expert_w3[l], expert_w2[l],
                 final_norm_g[None, :], final_norm=(l == depth - 1))
    return h.reshape(b, t, d)
```

```python
import functools

import jax
import jax.numpy as jnp
import numpy as np
from jax import lax
from jax.experimental import pallas as pl
from jax.experimental.pallas import tpu as pltpu

F32 = jnp.float32
BF16 = jnp.bfloat16

HEAD_DIM = 64
NSA_HEADS = 8
NSA_KV_HEADS = 2
NSA_GROUP = NSA_HEADS // NSA_KV_HEADS
CMP_LEN = 32
CMP_STRIDE = 16
CMP_HIDDEN = 4 * HEAD_DIM
SLC_BLOCK = 64
SLC_TOPN = 16
WINDOW = 512
NSA_BRANCHES = 3
POOL_WINDOWS = (2, 4, 8, 16)
CONV_K = 3
N_BRANCHES = 3
N_EXPERT_GROUPS = 4
EXPERTS_PER_GROUP = 8
N_EXPERTS = N_EXPERT_GROUPS * EXPERTS_PER_GROUP
RMS_EPS = 1e-6
NEG_INF = -1e30
FORCE_SCORE = 1e30
ALIBI_SLOPES = tuple(float(2.0 ** (-8.0 * (h + 1) / NSA_HEADS)) for h in range(NSA_HEADS))

LANES = 128
VMEM_LIMIT = 56 * 1024 * 1024
TOKEN_TILE = 512
MOE_TILE = 1024
MOE_SORT_TILE = 512
Q_TILE = 128
KEY_CHUNK = 128
GATE_PAD = LANES
ROUTER_PAD = LANES
WORD_BITS = 16


def _params(*semantics):
    return pltpu.CompilerParams(dimension_semantics=semantics, vmem_limit_bytes=VMEM_LIMIT)


def _dot(a, b):
    return jnp.dot(a, b, preferred_element_type=F32)


def _dot_nt(a, b):
    return lax.dot_general(a, b, (((1,), (1,)), ((), ())), preferred_element_type=F32)


def _rms_norm(x, g):
    y = x * lax.rsqrt(jnp.mean(x * x, axis=-1, keepdims=True) + RMS_EPS)
    return y * g


def _iota(shape, dim):
    return lax.broadcasted_iota(jnp.int32, shape, dim)


def _inproj_kernel(x_ref, g_ref, wq_ref, wv_ref, wg_ref, wk_ref, wn_ref, kaux_ref,
                   q_ref, vs_ref, vw_ref, gate_ref, ks_ref, kw_ref, cmp_ref, pool_ref, conv_ref):
    xn = _rms_norm(x_ref[...], g_ref[...]).astype(BF16)
    sub_tiles = x_ref.shape[0] // Q_TILE
    q_t = _dot_nt(wq_ref[...], xn)
    for g in range(NSA_KV_HEADS):
        for j in range(sub_tiles):
            for r in range(NSA_GROUP):
                head = g * NSA_GROUP + r
                q_ref[0, g, j, :, r * Q_TILE:(r + 1) * Q_TILE] = q_t[
                    head * HEAD_DIM:(head + 1) * HEAD_DIM, j * Q_TILE:(j + 1) * Q_TILE].astype(BF16)
    v_t = _dot_nt(wv_ref[...], xn)
    k = _dot(xn, wk_ref[...])
    for branch, (v_ref, k_ref) in enumerate(((vs_ref, ks_ref), (vw_ref, kw_ref))):
        for g in range(NSA_KV_HEADS):
            slab = branch * NSA_KV_HEADS + g
            for j in range(sub_tiles):
                v_ref[0, g, j] = v_t[slab * HEAD_DIM:(slab + 1) * HEAD_DIM,
                                     j * KEY_CHUNK:(j + 1) * KEY_CHUNK].astype(BF16)
            k_ref[0, g] = (k[:, slab * LANES:(slab + 1) * LANES] + kaux_ref[...]).astype(BF16)
    gate_ref[0] = _dot_nt(wg_ref[...], xn)
    col = 0
    for ref in (cmp_ref, pool_ref, conv_ref):
        width = ref.shape[-1]
        ref[...] = _dot(xn, wn_ref[:, col:col + width]).reshape(ref.shape)
        col += width


def _inproj(h, g, wq_t, wv_t, wg_t, wk, wn, kaux, batch):
    n, d = h.shape
    t = n // batch
    tm = TOKEN_TILE
    steps = t // tm
    sub = tm // Q_TILE
    cw = d // 4
    gq = NSA_GROUP * Q_TILE
    full = lambda a: pl.BlockSpec(a.shape, lambda i: (0,) * a.ndim)
    row = lambda width: pl.BlockSpec((tm, width), lambda i: (i, 0))
    tiles = lambda rows, cols: pl.BlockSpec((1, NSA_KV_HEADS, sub, rows, cols),
                                            lambda i: (i // steps, 0, i % steps, 0, 0))
    keys = pl.BlockSpec((1, NSA_KV_HEADS, tm, LANES), lambda i: (i // steps, 0, i % steps, 0))
    sds = jax.ShapeDtypeStruct
    v_shape = sds((batch, NSA_KV_HEADS, t // KEY_CHUNK, HEAD_DIM, KEY_CHUNK), BF16)
    k_shape = sds((batch, NSA_KV_HEADS, t, LANES), BF16)
    return pl.pallas_call(
        _inproj_kernel,
        grid=(n // tm,),
        in_specs=[row(d), full(g), full(wq_t), full(wv_t), full(wg_t), full(wk), full(wn), full(kaux)],
        out_specs=[tiles(HEAD_DIM, gq), tiles(HEAD_DIM, KEY_CHUNK), tiles(HEAD_DIM, KEY_CHUNK),
                   pl.BlockSpec((1, GATE_PAD, tm), lambda i: (i // steps, 0, i % steps)),
                   keys, keys,
                   pl.BlockSpec((1, tm, 2 * NSA_KV_HEADS * HEAD_DIM), lambda i: (i // steps, i % steps, 0)),
                   row(cw), row(3 * cw)],
        out_shape=[sds((batch, NSA_KV_HEADS, t // Q_TILE, HEAD_DIM, gq), BF16), v_shape, v_shape,
                   sds((batch, GATE_PAD, t), F32), k_shape, k_shape,
                   sds((batch, t, 2 * NSA_KV_HEADS * HEAD_DIM), F32), sds((n, cw), F32), sds((n, 3 * cw), F32)],
        compiler_params=_params("parallel"),
        name="inproj",
    )(h, g, wq_t, wv_t, wg_t, wk, wn, kaux)


def _gelu_tanh(x):
    return 0.5 * x * (1.0 + jnp.tanh(0.7978845608028654 * (x + 0.044715 * x * x * x)))


def _compress_kernel(src_ref, pe_ref, w1_ref, w1bd_ref, w2k_ref, w2vt_ref, caux_ref, kc_ref, vct_ref):
    kind = pl.program_id(1)
    ncp = kc_ref.shape[2]
    hidden = w1_ref.shape[2]
    pieces = [src_ref[0, pl.ds(l, ncp, stride=CMP_STRIDE), :].astype(BF16) for l in range(CMP_STRIDE)]
    chunk = jnp.concatenate(pieces, axis=1)
    first = _dot(chunk, w1bd_ref[0, 0])
    second = _dot(chunk, w1bd_ref[0, 1])
    bias = _dot(pe_ref[0], w1_ref[0])[0:1, :]
    hid = first + pltpu.roll(second, ncp - 1, 0) + jnp.concatenate([bias] * NSA_KV_HEADS, axis=1)
    row = _iota((ncp, 1), 0)
    act = jnp.where(row < ncp - 1, _gelu_tanh(hid), 0.0).astype(BF16)
    for g in range(NSA_KV_HEADS):
        act_g = act[:, g * hidden:(g + 1) * hidden]

        @pl.when(kind == 0)
        def _():
            kc_ref[0, g] = (_dot(act_g, w2k_ref[...]) + caux_ref[...]).astype(BF16)

        @pl.when(kind == 1)
        def _():
            vct_ref[0, g * HEAD_DIM:(g + 1) * HEAD_DIM, :] = _dot_nt(w2vt_ref[...], act_g).astype(BF16)


def _compress(src, pe, w1, w1bd, w2k, w2vt, caux):
    b, t, _ = src.shape
    ncp = t // CMP_STRIDE
    gd = NSA_KV_HEADS * HEAD_DIM
    full = lambda a: pl.BlockSpec(a.shape, lambda bi, k: (0,) * a.ndim)
    per_kind = lambda a: pl.BlockSpec((1,) + a.shape[1:], lambda bi, k: (k,) + (0,) * (a.ndim - 1))
    return pl.pallas_call(
        _compress_kernel,
        grid=(b, 2),
        in_specs=[pl.BlockSpec((1, t, gd), lambda bi, k: (bi, 0, k)),
                  per_kind(pe), per_kind(w1), per_kind(w1bd), full(w2k), full(w2vt), full(caux)],
        out_specs=[pl.BlockSpec((1, NSA_KV_HEADS, ncp, LANES), lambda bi, k: (bi, 0, 0, 0)),
                   pl.BlockSpec((1, gd, ncp), lambda bi, k: (bi, 0, 0))],
        out_shape=[jax.ShapeDtypeStruct((b, NSA_KV_HEADS, ncp, LANES), BF16),
                   jax.ShapeDtypeStruct((b, gd, ncp), BF16)],
        compiler_params=_params("parallel", "arbitrary"),
        name="compress",
    )(src, pe, w1, w1bd, w2k, w2vt, caux)


SOFTMAX_FLOOR = -1e29
TAKEN = -3e38
ATTN_BATCH = 8
BLOCKS_PER_CHUNK = KEY_CHUNK // SLC_BLOCK
AUX_COLS = 16
AUX_SLOPE = BLOCKS_PER_CHUNK
ONES_ROWS = 16
CMP_ROWS_STEP = 256
CMP_AUX_SPLIT = 128


def _slope(g, r):
    if isinstance(g, int):
        return jnp.float32(ALIBI_SLOPES[g * NSA_GROUP + r])
    s = jnp.float32(ALIBI_SLOPES[r])
    for gi in range(1, NSA_KV_HEADS):
        s = jnp.where(g == gi, jnp.float32(ALIBI_SLOPES[gi * NSA_GROUP + r]), s)
    return s


def _nsa_cmp_kernel(qt_ref, kc_ref, vct_ref, poolt_ref, pair_ref, bits_ref, oct_ref, selt_ref, words_ref,
                    *, n_sel):
    i = pl.program_id(1)
    qt = selt_ref.shape[4]
    ncp = kc_ref.shape[2]
    n_slc = poolt_ref.shape[0]
    gq = qt_ref.shape[4]
    start = i * qt
    t = start + _iota((1, qt), 1)
    aux_row = _iota((AUX_COLS, gq), 0)
    col_head = _iota((1, gq), 1) // qt
    pad_rows = jnp.zeros((kc_ref.shape[3] - qt_ref.shape[3] - AUX_COLS, gq), BF16)

    def weights(g):
        slope_cols = jnp.zeros((1, gq), F32)
        for r in range(NSA_GROUP):
            slope_cols = jnp.where(col_head == r, _slope(g, r), slope_cols)
        aux = jnp.where(aux_row == 0, slope_cols * (CMP_STRIDE * CMP_AUX_SPLIT),
                        jnp.where(aux_row == 1, slope_cols * CMP_STRIDE, 0.0))
        return jnp.concatenate([qt_ref[0, g, 0], aux.astype(BF16), pad_rows], axis=0)

    def importance(g, nr, nb):
        s = _dot(kc_ref[0, g, :nr, :], weights(g))
        edge = min(nr, 2 * CMP_ROWS_STEP)
        cmp_end = ((nr - edge) + _iota((edge, 1), 0)) * CMP_STRIDE + (CMP_LEN - 1)
        visible = cmp_end <= t
        vct = vct_ref[0, g * HEAD_DIM:(g + 1) * HEAD_DIM, :nr]
        psum = jnp.zeros((nr, qt), F32)
        for r in range(NSA_GROUP):
            cols = slice(r * qt, (r + 1) * qt)
            sr = s[:, cols]
            tail = jnp.where(visible, sr[nr - edge:], NEG_INF)
            sr = tail if edge == nr else jnp.concatenate([sr[:nr - edge], tail], axis=0)
            m = jnp.maximum(jnp.max(sr, axis=0, keepdims=True), SOFTMAX_FLOOR)
            e = jnp.exp(sr - m)
            l = jnp.sum(e, axis=0, keepdims=True)
            inv = jnp.where(l > 0.0, 1.0 / l, 0.0)
            oct_ref[0, g, 0, :, cols] = (_dot(vct, e.astype(BF16)) * inv).astype(BF16)
            psum = psum + e * inv
        return _dot(poolt_ref[:nb, :nr], psum.astype(BF16))

    def visible_prefix(nr):
        nb = min(n_slc, nr * CMP_STRIDE // SLC_BLOCK)
        imp = jnp.concatenate([importance(g, nr, nb) for g in range(NSA_KV_HEADS)], axis=1)
        blk = _iota((nb, 1), 0)
        cur = jnp.concatenate([t // SLC_BLOCK] * NSA_KV_HEADS, axis=1)
        forced = (blk == 0) | (blk == cur) | (blk == cur - 1)
        score = jnp.where(forced, TAKEN, jnp.where(blk <= cur, imp, NEG_INF))
        n_forced = 1 + jnp.where(cur >= 1, 1, 0) + jnp.where(cur >= 2, 1, 0)
        blk_f = blk.astype(F32)

        def take_one(score, active):
            m = jnp.max(score, axis=0, keepdims=True)
            first = jnp.min(jnp.where(score == m, blk_f, F32(1e9)), axis=0, keepdims=True)
            hit = (blk_f == first) if active is None else ((blk_f == first) & active)
            return jnp.where(hit, TAKEN, score)

        common_rounds = max(n_sel - 3, 0)
        for _ in range(common_rounds):
            score = take_one(score, None)

        def early_rounds(score):
            for k in range(common_rounds, n_sel - 1):
                score = take_one(score, n_sel - n_forced > k)
            return score

        score = lax.cond(start < 2 * SLC_BLOCK, early_rounds, lambda sc: sc, score)
        for g in range(NSA_KV_HEADS):
            sel_g = score[:, g * qt:(g + 1) * qt] == TAKEN
            selt_ref[0, g, 0, :nb, :] = jnp.where(sel_g, 0.0, NEG_INF)
            if nb < n_slc:
                selt_ref[0, g, 0, nb:, :] = jnp.full((n_slc - nb, qt), NEG_INF, F32)
            count = _dot_nt(jnp.ones((8, qt), BF16), jnp.where(sel_g, 1.0, 0.0).astype(BF16))
            used = jnp.where(count > 0.0, 1.0, 0.0).astype(BF16)
            chunk_used = jnp.where(_dot(used, pair_ref[:nb, :]) > 0.0, 1.0, 0.0).astype(BF16)
            words_ref[0, g, 0] = _dot(chunk_used, bits_ref[...]).astype(jnp.int32)

    step = min(CMP_ROWS_STEP, ncp)
    rows_needed = jnp.minimum((start + qt - CMP_LEN) // CMP_STRIDE + 1, ncp)
    n_steps = (rows_needed + step - 1) // step
    for k in range(ncp // step):
        pl.when(n_steps == k + 1)(functools.partial(visible_prefix, (k + 1) * step))


def _nsa_cmp(q_t, kc, vct, poolt_m, pair_m, bits_m, n_sel):
    b, ng, n_tiles, qrows, qcols = q_t.shape
    ncp = kc.shape[2]
    n_slc = poolt_m.shape[0]
    const = lambda shape: pl.BlockSpec(shape, lambda bi, i: (0,) * len(shape))
    tile5 = lambda rows, cols: pl.BlockSpec((1, ng, 1, rows, cols), lambda bi, i: (bi, 0, i, 0, 0))
    return pl.pallas_call(
        functools.partial(_nsa_cmp_kernel, n_sel=n_sel),
        grid=(b, n_tiles),
        in_specs=[
            tile5(qrows, qcols),
            pl.BlockSpec((1, ng, ncp, kc.shape[3]), lambda bi, i: (bi, 0, 0, 0)),
            pl.BlockSpec((1, ng * HEAD_DIM, ncp), lambda bi, i: (bi, 0, 0)),
            const(poolt_m.shape), const(pair_m.shape), const(bits_m.shape),
        ],
        out_specs=[tile5(HEAD_DIM, qcols), tile5(n_slc, Q_TILE), tile5(8, LANES)],
        out_shape=[
            jax.ShapeDtypeStruct((b, ng, n_tiles, HEAD_DIM, qcols), BF16),
            jax.ShapeDtypeStruct((b, ng, n_tiles, n_slc, Q_TILE), F32),
            jax.ShapeDtypeStruct((b, ng, n_tiles, 8, LANES), jnp.int32),
        ],
        compiler_params=_params("parallel", "parallel"),
        name="nsa_compressed",
    )(q_t, kc, vct, poolt_m, pair_m, bits_m)


def _nsa_slc_kernel(words_ref, qt_ref, ks_ref, vst_ref, kw_ref, vwt_ref, selt_ref, oct_ref, gt_ref,
                    out_ref, m_ref, l_ref, acc_ref, ow_ref, list_ref, *, words_per_tile):
    bi = pl.program_id(0)
    g = pl.program_id(1)
    i = pl.program_id(2)
    n_tiles = pl.num_programs(2)
    qt = out_ref.shape[1]
    q_rows = qt_ref[0, 0, 0]
    gq = q_rows.shape[1]
    start = i * qt
    lane_f = _iota((1, qt), 1).astype(F32)
    key_in_chunk = _iota((KEY_CHUNK, qt), 0)
    query_in_tile = _iota((KEY_CHUNK, qt), 1)

    aux_row = _iota((AUX_COLS, gq), 0)
    col_head = _iota((1, gq), 1) // qt
    slope_cols = jnp.zeros((1, gq), F32)
    for r in range(NSA_GROUP):
        slope_cols = jnp.where(col_head == r, _slope(g, r), slope_cols)
    aux_base = jnp.where(aux_row == AUX_SLOPE, slope_cols, 0.0)
    pad_rows = jnp.zeros((ks_ref.shape[3] - q_rows.shape[0] - AUX_COLS, gq), BF16)

    def scores_and_values(slots):
        scores = [_dot(k, jnp.concatenate([q_rows, aux.astype(BF16), pad_rows], axis=0))
                  for k, aux, _, _, _ in slots]
        values = jnp.concatenate([v for _, _, v, _, _ in slots], axis=1)
        return scores, jnp.concatenate([values, jnp.ones((ONES_ROWS, values.shape[1]), BF16)], axis=0)

    def softmax_step(slots, scores, v_cat, r, m_old):
        cols = slice(r * qt, (r + 1) * qt)
        srs, tops = [], []
        for (_, _, _, shift, mask), s in zip(slots, scores):
            sr = s[:, cols] if mask is None else jnp.where(mask, s[:, cols], NEG_INF)
            srs.append(sr)
            tops.append(jnp.max(sr, axis=0, keepdims=True) + shift[r])
        m_new = functools.reduce(jnp.maximum, tops, m_old)
        ps = [jnp.exp((sr - (m_new - slot[3][r])).astype(BF16)) for slot, sr in zip(slots, srs)]
        weighted = _dot(v_cat, jnp.concatenate(ps, axis=0))
        return m_new, weighted[HEAD_DIM:HEAD_DIM + 1], weighted[:HEAD_DIM]

    def shifts(dist0, ok):
        rows = [-_slope(g, r) * (dist0 + lane_f) for r in range(NSA_GROUP)]
        return rows if ok is None else [jnp.where(ok, row, NEG_INF) for row in rows]

    word_base = ((bi * NSA_KV_HEADS + g) * n_tiles + i) * words_per_tile
    list_ref[0] = 0

    def scan_word(w, n):
        word = words_ref[word_base + w]

        def scan_bits(n):
            for bit in range(WORD_BITS):
                c = w * WORD_BITS + bit
                list_ref[n] = c
                n = n + jnp.where(c < i, (word >> bit) & 1, 0)
            return n

        return lax.cond(word != 0, scan_bits, lambda n: n, n)

    n_listed = lax.fori_loop(0, (i + WORD_BITS - 1) // WORD_BITS, scan_word, 0)

    def selected_slot(c, ok, mask):
        at = pl.multiple_of(c * KEY_CHUNK, KEY_CHUNK)
        bias = selt_ref[0, 0, 0, pl.ds(c * BLOCKS_PER_CHUNK, BLOCKS_PER_CHUNK), :]
        aux = aux_base
        for blk in range(BLOCKS_PER_CHUNK):
            aux = jnp.where(aux_row == blk, jnp.concatenate([bias[blk:blk + 1]] * NSA_GROUP, axis=1), aux)
        return (ks_ref[0, 0, pl.ds(at, KEY_CHUNK), :], aux, vst_ref[0, 0, c],
                shifts((start - c * KEY_CHUNK).astype(F32), ok), mask)

    def listed_slot(idx):
        ok = idx < n_listed
        c = jnp.where(ok, list_ref[jnp.minimum(idx, jnp.maximum(n_listed - 1, 0))], 0)
        return selected_slot(c, ok, None)

    floor = jnp.full((1, qt), SOFTMAX_FLOOR, F32)

    n_back = WINDOW // KEY_CHUNK
    slots = []
    for j in range(n_back + 1):
        cs = start - WINDOW + j * KEY_CHUNK
        chunk = jnp.maximum(cs, 0) // KEY_CHUNK
        at = pl.multiple_of(chunk * KEY_CHUNK, KEY_CHUNK)
        mask = (query_in_tile < key_in_chunk) if j == 0 else (
            (key_in_chunk <= query_in_tile) if j == n_back else None)
        slots.append((kw_ref[0, 0, pl.ds(at, KEY_CHUNK), :], aux_base, vwt_ref[0, 0, chunk],
                      shifts(F32(WINDOW - j * KEY_CHUNK), cs >= 0), mask))
    scores, v_cat = scores_and_values(slots)
    for r in range(NSA_GROUP):
        _, total, weighted = softmax_step(slots, scores, v_cat, r, floor)
        ow_ref[:, r * qt:(r + 1) * qt] = weighted * jnp.where(total > 0.0, 1.0 / total, 0.0)

    slots = [selected_slot(i, None, key_in_chunk <= query_in_tile)] + [
        listed_slot(j) for j in range(ATTN_BATCH - 1)]
    scores, v_cat = scores_and_values(slots)
    for r in range(NSA_GROUP):
        m_ref[r], l_ref[r], acc_ref[:, r * qt:(r + 1) * qt] = softmax_step(slots, scores, v_cat, r, floor)

    def batch(it, carry):
        first = ATTN_BATCH - 1 + it * ATTN_BATCH
        slots = [listed_slot(first + j) for j in range(ATTN_BATCH)]
        scores, v_cat = scores_and_values(slots)
        for r in range(NSA_GROUP):
            cols = slice(r * qt, (r + 1) * qt)
            m_old = m_ref[r]
            m_new, total, weighted = softmax_step(slots, scores, v_cat, r, m_old)
            alpha = jnp.exp(m_old - m_new)
            l_ref[r] = alpha * l_ref[r] + total
            acc_ref[:, cols] = alpha * acc_ref[:, cols] + weighted
            m_ref[r] = m_new
        return carry

    n_rest = jnp.maximum(n_listed - (ATTN_BATCH - 1), 0)
    lax.fori_loop(0, (n_rest + ATTN_BATCH - 1) // ATTN_BATCH, batch, 0)

    def finalize(r):
        l = l_ref[r]
        return acc_ref[:, r * qt:(r + 1) * qt] * jnp.where(l > 0.0, 1.0 / l, 0.0)

    outs = []
    for r in range(NSA_GROUP):
        cols = slice(r * qt, (r + 1) * qt)
        col = (g * NSA_GROUP + r) * NSA_BRANCHES
        gate = lambda br: jax.nn.sigmoid(gt_ref[0, pl.ds(col + br, 1), :])
        outs.append(gate(0) * oct_ref[0, 0, 0, :, cols].astype(F32) + gate(1) * finalize(r)
                    + gate(2) * ow_ref[:, cols])
    out_ref[0] = jnp.concatenate(outs, axis=0).T.astype(out_ref.dtype)


def _nsa_slc(words, q_t, ks, vst, kw, vwt, selt, oct, gates_t, words_per_tile):
    b, _, t, kw_cols = ks.shape
    _, _, n_tiles, qrows, qcols = q_t.shape
    n_slc = selt.shape[3]
    n_chunks = vst.shape[2]
    gw = NSA_GROUP * HEAD_DIM
    once = dict(pipeline_mode=pl.Buffered(1))
    k_spec = pl.BlockSpec((1, 1, t, kw_cols), lambda bi, g, i, w: (bi, g, 0, 0), **once)
    vt_spec = pl.BlockSpec((1, 1, n_chunks, HEAD_DIM, KEY_CHUNK), lambda bi, g, i, w: (bi, g, 0, 0, 0), **once)
    tile5 = lambda rows, cols: pl.BlockSpec((1, 1, 1, rows, cols), lambda bi, g, i, w: (bi, g, i, 0, 0))
    grid_spec = pltpu.PrefetchScalarGridSpec(
        num_scalar_prefetch=1,
        grid=(b, NSA_KV_HEADS, n_tiles),
        in_specs=[
            tile5(qrows, qcols),
            k_spec, vt_spec, k_spec, vt_spec,
            tile5(n_slc, Q_TILE),
            tile5(HEAD_DIM, qcols),
            pl.BlockSpec((1, GATE_PAD, Q_TILE), lambda bi, g, i, w: (bi, 0, i)),
        ],
        out_specs=pl.BlockSpec((1, Q_TILE, gw), lambda bi, g, i, w: (bi, i, g)),
        scratch_shapes=[pltpu.VMEM((NSA_GROUP, 1, Q_TILE), F32), pltpu.VMEM((NSA_GROUP, 1, Q_TILE), F32),
                        pltpu.VMEM((HEAD_DIM, qcols), F32), pltpu.VMEM((HEAD_DIM, qcols), F32),
                        pltpu.SMEM((n_chunks,), jnp.int32)],
    )
    return pl.pallas_call(
        functools.partial(_nsa_slc_kernel, words_per_tile=words_per_tile),
        grid_spec=grid_spec,
        out_shape=jax.ShapeDtypeStruct((b, t, NSA_HEADS * HEAD_DIM), BF16),
        compiler_params=_params("parallel", "parallel", "parallel"),
        name="nsa_selected_window",
    )(words, q_t, ks, vst, kw, vwt, selt, oct, gates_t)


POOL_HALO = 16
CONV_HALO = 8


def _merge_kernel(h_ref, g_ref, nsa_ref, pool_ref, pool_halo_ref, conv_ref, conv_halo_ref,
                  wmg_ref, wnsa_ref, pool_bd_ref, pool_scale_ref, wpool_ref, convw_ref, wconv_ref, wo_ref,
                  out_ref, pool_ext, conv_ext, *, seq_len):
    i = pl.program_id(0)
    tm, d = h_ref.shape
    cw = pool_ref.shape[1]
    pos0 = (i * tm) % seq_len
    keep_halo = jnp.where(pos0 == 0, 0.0, 1.0)
    pos = pos0 + _iota((tm, 1), 0)

    u = pool_ref[...]
    pool_ext[0:POOL_HALO, :] = pool_halo_ref[...] * keep_halo
    pool_ext[POOL_HALO:, :] = u
    lane_group = _iota((1, cw), 1) // (cw // len(POOL_WINDOWS))
    total = u
    mean = jnp.zeros_like(u)
    done = 1
    for gi, win in enumerate(POOL_WINDOWS):
        for k in range(done, win):
            total = total + pool_ext[POOL_HALO - k:POOL_HALO - k + tm, :]
        done = win
        cnt = jnp.minimum(pos + 1, win).astype(F32)
        mean = jnp.where(lane_group == gi, total / cnt, mean)
    pooled = (mean - u).astype(BF16)
    mixed = _dot(pooled, pool_bd_ref[...]) * pool_scale_ref[...]
    y_pool = _dot(mixed.astype(BF16), wpool_ref[...])

    ch = conv_ref[:, 0:cw]
    cb = conv_ref[:, cw:2 * cw]
    cc = conv_ref[:, 2 * cw:3 * cw]
    conv_ext[0:CONV_HALO, :] = conv_halo_ref[:, 0:cw] * conv_halo_ref[:, 2 * cw:3 * cw] * keep_halo
    conv_ext[CONV_HALO:, :] = cc * ch
    y = jnp.zeros((tm, cw), F32)
    for k in range(CONV_K):
        off = CONV_HALO - (CONV_K - 1) + k
        y = y + convw_ref[k:k + 1, :] * conv_ext[off:off + tm, :]
    y_conv = _dot((cb * y).astype(BF16), wconv_ref[...])

    y_nsa = _dot(nsa_ref[...], wnsa_ref[...])

    h = h_ref[...]
    xn = _rms_norm(h, g_ref[...]).astype(BF16)
    merged = jnp.zeros((tm, d), F32)
    for br, y_br in enumerate((y_nsa, y_pool, y_conv)):
        mg = jax.nn.sigmoid(_dot(xn, wmg_ref[:, br * d:(br + 1) * d]))
        merged = merged + mg * y_br
    out_ref[...] = h + _dot(merged.astype(BF16), wo_ref[...])


def _merge(h, g, nsa, pool_u, conv, wmg, wnsa, pool_bd, pool_scale, wpool, convw, wconv, wo, seq_len):
    n, d = h.shape
    tm = TOKEN_TILE
    cw = pool_u.shape[1]
    row = lambda width: pl.BlockSpec((tm, width), lambda i: (i, 0))
    full = lambda a: pl.BlockSpec(a.shape, lambda i: (0,) * a.ndim)
    halo = lambda rows, width: pl.BlockSpec(
        (rows, width), lambda i: (jnp.maximum(i * (tm // rows) - 1, 0), 0))
    return pl.pallas_call(
        functools.partial(_merge_kernel, seq_len=seq_len),
        grid=(n // tm,),
        in_specs=[row(d), full(g), row(nsa.shape[1]), row(cw), halo(POOL_HALO, cw),
                  row(conv.shape[1]), halo(CONV_HALO, conv.shape[1]),
                  full(wmg), full(wnsa), full(pool_bd), full(pool_scale), full(wpool), full(convw),
                  full(wconv), full(wo)],
        out_specs=row(d),
        out_shape=jax.ShapeDtypeStruct((n, d), F32),
        scratch_shapes=[pltpu.VMEM((tm + POOL_HALO, cw), F32), pltpu.VMEM((tm + CONV_HALO, cw), F32)],
        compiler_params=_params("parallel"),
        name="merge",
    )(h, g, nsa, pool_u, pool_u, conv, conv, wmg, wnsa, pool_bd, pool_scale, wpool, convw, wconv, wo)


def _route(logits):
    lane = _iota(logits.shape, 1)
    lane_f = lane.astype(F32)
    big = F32(1e9)
    is_group = lane < N_EXPERT_GROUPS
    gl = jnp.where(is_group, logits, NEG_INF)
    g_max = jnp.max(gl, axis=1, keepdims=True)
    g_sel = jnp.min(jnp.where(gl == g_max, lane_f, big), axis=1, keepdims=True)
    g_prob = 1.0 / jnp.sum(jnp.where(is_group, jnp.exp(gl - g_max), 0.0), axis=1, keepdims=True)
    lo = N_EXPERT_GROUPS + EXPERTS_PER_GROUP * g_sel
    in_group = (lane_f >= lo) & (lane_f < lo + EXPERTS_PER_GROUP)
    el = jnp.where(in_group, logits, NEG_INF)
    v1 = jnp.max(el, axis=1, keepdims=True)
    i1 = jnp.min(jnp.where((el == v1) & in_group, lane_f, big), axis=1, keepdims=True)
    el2 = jnp.where(lane_f == i1, NEG_INF, el)
    rest = in_group & (lane_f != i1)
    v2 = jnp.max(el2, axis=1, keepdims=True)
    i2 = jnp.min(jnp.where((el2 == v2) & rest, lane_f, big), axis=1, keepdims=True)
    e2 = jnp.exp(v2 - v1)
    w1 = g_prob / (1.0 + e2)
    w2 = g_prob * e2 / (1.0 + e2)
    return jnp.where(lane_f == i1, w1, 0.0) + jnp.where(lane_f == i2, w2, 0.0), g_sel


GROUP_LANE = N_EXPERT_GROUPS + N_EXPERTS


RANK_LANE = GROUP_LANE + 1


def _router_kernel(h_ref, g_ref, wr_ref, br_ref, before_ref, xn_ref, comb_ref, count_ref):
    @pl.when(pl.program_id(0) == 0)
    def _():
        count_ref[...] = jnp.zeros(count_ref.shape, F32)

    xn = _rms_norm(h_ref[...], g_ref[...])
    xn_hi = xn.astype(BF16)
    xn_lo = (xn - xn_hi.astype(F32)).astype(BF16)
    logits = (_dot(xn_hi, wr_ref[0]) + (_dot(xn_hi, wr_ref[1]) + _dot(xn_lo, wr_ref[0]))) + br_ref[...]
    comb, g_sel = _route(logits)
    lane = _iota(comb.shape, 1)
    chose = jnp.where(lane.astype(F32) == g_sel, 1.0, 0.0)
    earlier = _dot(before_ref[...], chose.astype(BF16)) + count_ref[0:1, :]
    rank = jnp.sum(chose * earlier, axis=1, keepdims=True)
    count_ref[0:1, :] = count_ref[0:1, :] + jnp.sum(chose, axis=0, keepdims=True)
    comb_ref[...] = jnp.where(lane == GROUP_LANE, g_sel, jnp.where(lane == RANK_LANE, rank, comb))
    xn_ref[...] = xn_hi


def _router(h, g, wr, br, before):
    n, d = h.shape
    tm = MOE_TILE
    full = lambda a: pl.BlockSpec(a.shape, lambda i: (0,) * a.ndim)
    return pl.pallas_call(
        _router_kernel,
        grid=(n // tm,),
        in_specs=[pl.BlockSpec((tm, d), lambda i: (i, 0)), full(g), full(wr), full(br), full(before)],
        out_specs=[pl.BlockSpec((tm, d), lambda i: (i, 0)), pl.BlockSpec((tm, ROUTER_PAD), lambda i: (i, 0)),
                   pl.BlockSpec((8, ROUTER_PAD), lambda i: (0, 0))],
        out_shape=[jax.ShapeDtypeStruct((n, d), BF16), jax.ShapeDtypeStruct((n, ROUTER_PAD), F32),
                   jax.ShapeDtypeStruct((8, ROUTER_PAD), F32)],
        compiler_params=_params("arbitrary"),
        name="router",
    )(h, g, wr, br, before)


def _experts_kernel(tile_group_ref, n_active_ref, x_ref, comb_ref, w1_ref, w3_ref, w2_ref, out_ref):
    i = pl.program_id(0)

    @pl.when(i < n_active_ref[0])
    def _():
        x = x_ref[...]
        comb = comb_ref[...]
        lane = _iota(comb.shape, 1)
        first = N_EXPERT_GROUPS + tile_group_ref[i] * EXPERTS_PER_GROUP
        for e in range(EXPERTS_PER_GROUP):
            c_e = jnp.sum(jnp.where(lane == first + e, comb, 0.0), axis=1, keepdims=True)
            a = (jax.nn.silu(_dot(x, w1_ref[0, e].astype(BF16))) * _dot(x, w3_ref[0, e].astype(BF16))) * c_e
            y = _dot(a.astype(BF16), w2_ref[0, e].astype(BF16))
            if e == 0:
                out_ref[...] = y
            else:
                out_ref[...] += y


def _experts(tile_group, n_active, x_sorted, comb_sorted, w1, w3, w2):
    ns, d = x_sorted.shape
    tm = MOE_SORT_TILE
    group_w = lambda w: pl.BlockSpec((1,) + w.shape[1:], lambda i, tg, na: (tg[i], 0, 0, 0),
                                     pipeline_mode=pl.Buffered(1))
    grid_spec = pltpu.PrefetchScalarGridSpec(
        num_scalar_prefetch=2,
        grid=(ns // tm,),
        in_specs=[
            pl.BlockSpec((tm, d), lambda i, tg, na: (i, 0)),
            pl.BlockSpec((tm, ROUTER_PAD), lambda i, tg, na: (i, 0)),
            group_w(w1), group_w(w3), group_w(w2),
        ],
        out_specs=pl.BlockSpec((tm, d), lambda i, tg, na: (i, 0)),
    )
    return pl.pallas_call(
        _experts_kernel,
        grid_spec=grid_spec,
        out_shape=jax.ShapeDtypeStruct((ns, d), F32),
        compiler_params=_params("arbitrary"),
        name="experts",
    )(tile_group, n_active, x_sorted, comb_sorted, w1, w3, w2)


def _residual_kernel(h_ref, y_ref, gf_ref, out_ref, *, final_norm):
    out = h_ref[...] + y_ref[...]
    out_ref[...] = _rms_norm(out, gf_ref[...]) if final_norm else out


def _residual(h, y, gf, final_norm):
    n, d = h.shape
    tm = MOE_TILE
    row = pl.BlockSpec((tm, d), lambda i: (i, 0))
    return pl.pallas_call(
        functools.partial(_residual_kernel, final_norm=final_norm),
        grid=(n // tm,),
        in_specs=[row, row, pl.BlockSpec(gf.shape, lambda i: (0, 0))],
        out_specs=row,
        out_shape=jax.ShapeDtypeStruct((n, d), F32),
        compiler_params=_params("parallel"),
        name="residual",
    )(h, y, gf)


def _group_sort_plan(group_id, rank, counts, tile):
    n = group_id.shape[0]
    n_slots = n + N_EXPERT_GROUPS * tile
    padded = (counts + tile - 1) // tile * tile
    ends = jnp.cumsum(padded)
    slot = (ends - padded)[group_id] + rank
    source = jnp.zeros((n_slots,), jnp.int32).at[slot].set(jnp.arange(n, dtype=jnp.int32))
    tile_start = jnp.arange(n_slots // tile, dtype=jnp.int32) * tile
    tile_group = jnp.minimum(jnp.searchsorted(ends, tile_start, side="right"), N_EXPERT_GROUPS - 1)
    return slot, source, tile_group.astype(jnp.int32), (ends[-1:] // tile).astype(jnp.int32)


def _moe(h, g, wr, br, before, w1, w3, w2, gf, final_norm):
    xn, comb, counts = _router(h, g, wr, br, before)
    as_int = lambda a: a.astype(jnp.int32)
    slot, source, tile_group, n_active = _group_sort_plan(
        as_int(comb[:, GROUP_LANE]), as_int(comb[:, RANK_LANE]), as_int(counts[0, :N_EXPERT_GROUPS]),
        MOE_SORT_TILE)
    grouped = lambda w: w.reshape((N_EXPERT_GROUPS, EXPERTS_PER_GROUP) + w.shape[1:])
    rows = lambda a, idx: jnp.take(a, idx, axis=0, mode="clip")
    y_sorted = _experts(tile_group, n_active, rows(xn, source), rows(comb, source),
                        grouped(w1), grouped(w3), grouped(w2))
    return _residual(h, rows(y_sorted, slot), gf, final_norm)


def _selection_constants(seq_len):
    ncp = seq_len // CMP_STRIDE
    n_slc = seq_len // SLC_BLOCK
    ratio = SLC_BLOCK // CMP_STRIDE
    lead = CMP_LEN // CMP_STRIDE - 1
    c = np.arange(ncp)[:, None]
    j = np.arange(n_slc)[None, :]
    pool_m = ((c >= ratio * j - lead) & (c < ratio * j + ratio)).astype(np.float32)
    blocks_per_chunk = KEY_CHUNK // SLC_BLOCK
    n_chunks = seq_len // KEY_CHUNK
    pair_m = np.zeros((n_slc, LANES * ((n_chunks + LANES - 1) // LANES)), np.float32)
    pair_m[np.arange(n_slc), np.arange(n_slc) // blocks_per_chunk] = 1.0
    n_words = (n_chunks + WORD_BITS - 1) // WORD_BITS
    bits_m = np.zeros((pair_m.shape[1], LANES), np.float32)
    ch = np.arange(n_chunks)
    bits_m[ch, ch // WORD_BITS] = 2.0 ** (ch % WORD_BITS)
    key_aux = np.zeros((TOKEN_TILE, LANES), np.float32)
    in_chunk = np.arange(TOKEN_TILE) % KEY_CHUNK
    key_aux[np.arange(TOKEN_TILE), HEAD_DIM + in_chunk // SLC_BLOCK] = 1.0
    key_aux[:, HEAD_DIM + AUX_SLOPE] = in_chunk
    cmp_aux = np.zeros((ncp, LANES), np.float32)
    cmp_aux[:, HEAD_DIM] = np.arange(ncp) // CMP_AUX_SPLIT
    cmp_aux[:, HEAD_DIM + 1] = np.arange(ncp) % CMP_AUX_SPLIT
    as_bf16 = lambda a: jnp.asarray(a, BF16)
    return as_bf16(pool_m.T), as_bf16(pair_m), as_bf16(bits_m), jnp.asarray(key_aux), jnp.asarray(cmp_aux), n_words


def kernel(x, norm1_g, w_in, cmp_pe, cmp_w1, cmp_w2, w_nsa_proj, pool_w, pool_scale, w_pool_proj, conv_w,
           w_conv_proj, w_o, norm2_g, router_group_w, router_group_b, router_expert_w, router_expert_b,
           expert_w1, expert_w3, expert_w2, final_norm_g):
    b, t, d = x.shape
    n = b * t
    depth = w_in.shape[0]
    dq = NSA_HEADS * HEAD_DIM
    dkv = 6 * NSA_KV_HEADS * HEAD_DIM
    dgate = NSA_HEADS * NSA_BRANCHES
    cw = d // 4
    assert t % TOKEN_TILE == 0 and n % MOE_TILE == 0 and t % Q_TILE == 0
    n_slc = t // SLC_BLOCK
    n_sel = min(SLC_TOPN, n_slc)
    n_chunks16 = t // CMP_STRIDE
    kvw = NSA_KV_HEADS * HEAD_DIM
    poolt_m, pair_m, bits_m, key_aux, cmp_aux, n_words = _selection_constants(t)
    assert Q_TILE == KEY_CHUNK and AUX_SLOPE < AUX_COLS and KEY_CHUNK <= 256
    assert n_chunks16 <= 256 * CMP_AUX_SPLIT and n_chunks16 % min(CMP_ROWS_STEP, n_chunks16) == 0
    assert n_words <= LANES and pair_m.shape[1] == LANES

    before = jnp.asarray(np.tril(np.ones((MOE_TILE, MOE_TILE), np.float32), -1), BF16)
    h = x.reshape(n, d)
    for l in range(depth):
        wl = w_in[l]
        o_gate = dq + dkv
        o_pool = o_gate + dgate
        o_merge = o_pool + cw + 3 * cw
        kv_cols = lambda kind: wl[:, dq + kind * kvw:dq + (kind + 1) * kvw]
        wq_t = (wl[:, :dq] * (HEAD_DIM ** -0.5)).T.astype(BF16)
        wv_t = jnp.concatenate([kv_cols(3), kv_cols(5)], axis=1).T.astype(BF16)
        wg_t = jnp.pad(wl[:, o_gate:o_pool], ((0, 0), (0, GATE_PAD - dgate))).T.astype(BF16)
        no_aux = jnp.zeros((d, LANES - HEAD_DIM), F32)
        wk = jnp.concatenate([piece for kind in (2, 4) for gi in range(NSA_KV_HEADS)
                              for piece in (kv_cols(kind)[:, gi * HEAD_DIM:(gi + 1) * HEAD_DIM], no_aux)],
                             axis=1).astype(BF16)
        wn = jnp.concatenate([kv_cols(0), kv_cols(1), wl[:, o_pool:o_merge]], axis=1).astype(BF16)
        wmg = wl[:, o_merge:].astype(BF16)
        pool_bd = jax.scipy.linalg.block_diag(*[pool_w[l, gi] for gi in range(pool_w.shape[1])]).astype(BF16)
        convw = jnp.pad(conv_w[l], ((0, 8 - CONV_K), (0, 0)))
        wr = jnp.pad(jnp.concatenate([router_group_w[l], router_expert_w[l]], axis=1),
                     ((0, 0), (0, ROUTER_PAD - N_EXPERT_GROUPS - N_EXPERTS)))
        wr_hi = wr.astype(BF16)
        wr = jnp.stack([wr_hi, (wr - wr_hi.astype(F32)).astype(BF16)])
        br = jnp.pad(jnp.concatenate([router_group_b[l], router_expert_b[l]]),
                     (0, ROUTER_PAD - N_EXPERT_GROUPS - N_EXPERTS))[None, :]
        pe = jnp.broadcast_to(cmp_pe[l].reshape(2, 1, CMP_LEN * HEAD_DIM), (2, 8, CMP_LEN * HEAD_DIM)).astype(BF16)
        halves = CMP_LEN // CMP_STRIDE
        w1_bd = jnp.einsum("khldc,gq->khlgdqc",
                           cmp_w1[l].reshape(2, halves, CMP_STRIDE, HEAD_DIM, CMP_HIDDEN),
                           jnp.eye(NSA_KV_HEADS, dtype=F32))
        w1_bd = w1_bd.reshape(2, halves, CMP_STRIDE * kvw, NSA_KV_HEADS * CMP_HIDDEN).astype(BF16)
        w2_k = jnp.pad(cmp_w2[l, 0], ((0, 0), (0, LANES - HEAD_DIM))).astype(BF16)
        w2_vt = cmp_w2[l, 1].T.astype(BF16)

        q_t, vst, vwt, gates_t, ks, kw, cmp_src, pool_u, conv = _inproj(
            h, norm1_g[l][None, :], wq_t, wv_t, wg_t, wk, wn, key_aux, b)
        kc_aux, vc_t = _compress(cmp_src, pe, cmp_w1[l].astype(BF16), w1_bd, w2_k, w2_vt, cmp_aux)
        oc_t, sel_t, words = _nsa_cmp(q_t, kc_aux, vc_t, poolt_m, pair_m, bits_m, n_sel)
        words = words[:, :, :, 0, :n_words].reshape(-1)
        nsa = _nsa_slc(words, q_t, ks, vst, kw, vwt, sel_t, oc_t, gates_t, n_words)
        h = _merge(h, norm1_g[l][None, :], nsa.reshape(n, dq), pool_u, conv, wmg,
                   w_nsa_proj[l].astype(BF16), pool_bd, pool_scale[l][None, :], w_pool_proj[l].astype(BF16),
                   convw, w_conv_proj[l].astype(BF16), w_o[l].astype(BF16), t)
        h = _moe(h, norm2_g[l][None, :], wr, br, before, expert_w1[l], expert_w3[l], expert_w2[l],
                 final_norm_g[None, :], final_norm=(l == depth - 1))
    return h.reshape(b, t, d)
```

```python
import functools

import jax
import jax.numpy as jnp
import numpy as np
from jax import lax
from jax.experimental import pallas as pl
from jax.experimental.pallas import tpu as pltpu

F32 = jnp.float32
BF16 = jnp.bfloat16

HEAD_DIM = 64
NSA_HEADS = 8
NSA_KV_HEADS = 2
NSA_GROUP = NSA_HEADS // NSA_KV_HEADS
CMP_LEN = 32
CMP_STRIDE = 16
CMP_HIDDEN = 4 * HEAD_DIM
SLC_BLOCK = 64
SLC_TOPN = 16
WINDOW = 512
NSA_BRANCHES = 3
POOL_WINDOWS = (2, 4, 8, 16)
CONV_K = 3
N_BRANCHES = 3
N_EXPERT_GROUPS = 4
EXPERTS_PER_GROUP = 8
N_EXPERTS = N_EXPERT_GROUPS * EXPERTS_PER_GROUP
RMS_EPS = 1e-6
NEG_INF = -1e30
FORCE_SCORE = 1e30
ALIBI_SLOPES = tuple(float(2.0 ** (-8.0 * (h + 1) / NSA_HEADS)) for h in range(NSA_HEADS))

LANES = 128
VMEM_LIMIT = 56 * 1024 * 1024
TOKEN_TILE = 512
MOE_TILE = 1024
MOE_SORT_TILE = 512
Q_TILE = 128
KEY_CHUNK = 128
GATE_PAD = LANES
ROUTER_PAD = LANES
WORD_BITS = 16


def _params(*semantics):
    return pltpu.CompilerParams(dimension_semantics=semantics, vmem_limit_bytes=VMEM_LIMIT)


def _dot(a, b):
    return jnp.dot(a, b, preferred_element_type=F32)


def _dot_nt(a, b):
    return lax.dot_general(a, b, (((1,), (1,)), ((), ())), preferred_element_type=F32)


def _rms_norm(x, g):
    y = x * lax.rsqrt(jnp.mean(x * x, axis=-1, keepdims=True) + RMS_EPS)
    return y * g


def _iota(shape, dim):
    return lax.broadcasted_iota(jnp.int32, shape, dim)


def _inproj_kernel(x_ref, g_ref, wq_ref, wv_ref, wg_ref, wk_ref, wn_ref, kaux_ref,
                   q_ref, vs_ref, vw_ref, gate_ref, ks_ref, kw_ref, cmp_ref, pool_ref, conv_ref):
    xn = _rms_norm(x_ref[...], g_ref[...]).astype(BF16)
    sub_tiles = x_ref.shape[0] // Q_TILE
    q_t = _dot_nt(wq_ref[...], xn)
    for g in range(NSA_KV_HEADS):
        for j in range(sub_tiles):
            for r in range(NSA_GROUP):
                head = g * NSA_GROUP + r
                q_ref[0, g, j, :, r * Q_TILE:(r + 1) * Q_TILE] = q_t[
                    head * HEAD_DIM:(head + 1) * HEAD_DIM, j * Q_TILE:(j + 1) * Q_TILE].astype(BF16)
    v_t = _dot_nt(wv_ref[...], xn)
    k = _dot(xn, wk_ref[...])
    for branch, (v_ref, k_ref) in enumerate(((vs_ref, ks_ref), (vw_ref, kw_ref))):
        for g in range(NSA_KV_HEADS):
            slab = branch * NSA_KV_HEADS + g
            for j in range(sub_tiles):
                v_ref[0, g, j] = v_t[slab * HEAD_DIM:(slab + 1) * HEAD_DIM,
                                     j * KEY_CHUNK:(j + 1) * KEY_CHUNK].astype(BF16)
            k_ref[0, g] = (k[:, slab * LANES:(slab + 1) * LANES] + kaux_ref[...]).astype(BF16)
    gate_ref[0] = _dot_nt(wg_ref[...], xn)
    col = 0
    for ref in (cmp_ref, pool_ref, conv_ref):
        width = ref.shape[-1]
        ref[...] = _dot(xn, wn_ref[:, col:col + width]).reshape(ref.shape)
        col += width


def _inproj(h, g, wq_t, wv_t, wg_t, wk, wn, kaux, batch):
    n, d = h.shape
    t = n // batch
    tm = TOKEN_TILE
    steps = t // tm
    sub = tm // Q_TILE
    cw = d // 4
    gq = NSA_GROUP * Q_TILE
    full = lambda a: pl.BlockSpec(a.shape, lambda i: (0,) * a.ndim)
    row = lambda width: pl.BlockSpec((tm, width), lambda i: (i, 0))
    tiles = lambda rows, cols: pl.BlockSpec((1, NSA_KV_HEADS, sub, rows, cols),
                                            lambda i: (i // steps, 0, i % steps, 0, 0))
    keys = pl.BlockSpec((1, NSA_KV_HEADS, tm, LANES), lambda i: (i // steps, 0, i % steps, 0))
    sds = jax.ShapeDtypeStruct
    v_shape = sds((batch, NSA_KV_HEADS, t // KEY_CHUNK, HEAD_DIM, KEY_CHUNK), BF16)
    k_shape = sds((batch, NSA_KV_HEADS, t, LANES), BF16)
    return pl.pallas_call(
        _inproj_kernel,
        grid=(n // tm,),
        in_specs=[row(d), full(g), full(wq_t), full(wv_t), full(wg_t), full(wk), full(wn), full(kaux)],
        out_specs=[tiles(HEAD_DIM, gq), tiles(HEAD_DIM, KEY_CHUNK), tiles(HEAD_DIM, KEY_CHUNK),
                   pl.BlockSpec((1, GATE_PAD, tm), lambda i: (i // steps, 0, i % steps)),
                   keys, keys,
                   pl.BlockSpec((1, tm, 2 * NSA_KV_HEADS * HEAD_DIM), lambda i: (i // steps, i % steps, 0)),
                   row(cw), row(3 * cw)],
        out_shape=[sds((batch, NSA_KV_HEADS, t // Q_TILE, HEAD_DIM, gq), BF16), v_shape, v_shape,
                   sds((batch, GATE_PAD, t), F32), k_shape, k_shape,
                   sds((batch, t, 2 * NSA_KV_HEADS * HEAD_DIM), F32), sds((n, cw), F32), sds((n, 3 * cw), F32)],
        compiler_params=_params("parallel"),
        name="inproj",
    )(h, g, wq_t, wv_t, wg_t, wk, wn, kaux)


def _gelu_tanh(x):
    return 0.5 * x * (1.0 + jnp.tanh(0.7978845608028654 * (x + 0.044715 * x * x * x)))


def _compress_kernel(src_ref, pe_ref, w1_ref, w1bd_ref, w2k_ref, w2vt_ref, caux_ref, kc_ref, vct_ref):
    kind = pl.program_id(1)
    ncp = kc_ref.shape[2]
    hidden = w1_ref.shape[2]
    pieces = [src_ref[0, pl.ds(l, ncp, stride=CMP_STRIDE), :].astype(BF16) for l in range(CMP_STRIDE)]
    chunk = jnp.concatenate(pieces, axis=1)
    first = _dot(chunk, w1bd_ref[0, 0])
    second = _dot(chunk, w1bd_ref[0, 1])
    bias = _dot(pe_ref[0], w1_ref[0])[0:1, :]
    hid = first + pltpu.roll(second, ncp - 1, 0) + jnp.concatenate([bias] * NSA_KV_HEADS, axis=1)
    row = _iota((ncp, 1), 0)
    act = jnp.where(row < ncp - 1, _gelu_tanh(hid), 0.0).astype(BF16)
    for g in range(NSA_KV_HEADS):
        act_g = act[:, g * hidden:(g + 1) * hidden]

        @pl.when(kind == 0)
        def _():
            kc_ref[0, g] = (_dot(act_g, w2k_ref[...]) + caux_ref[...]).astype(BF16)

        @pl.when(kind == 1)
        def _():
            vct_ref[0, g * HEAD_DIM:(g + 1) * HEAD_DIM, :] = _dot_nt(w2vt_ref[...], act_g).astype(BF16)


def _compress(src, pe, w1, w1bd, w2k, w2vt, caux):
    b, t, _ = src.shape
    ncp = t // CMP_STRIDE
    gd = NSA_KV_HEADS * HEAD_DIM
    full = lambda a: pl.BlockSpec(a.shape, lambda bi, k: (0,) * a.ndim)
    per_kind = lambda a: pl.BlockSpec((1,) + a.shape[1:], lambda bi, k: (k,) + (0,) * (a.ndim - 1))
    return pl.pallas_call(
        _compress_kernel,
        grid=(b, 2),
        in_specs=[pl.BlockSpec((1, t, gd), lambda bi, k: (bi, 0, k)),
                  per_kind(pe), per_kind(w1), per_kind(w1bd), full(w2k), full(w2vt), full(caux)],
        out_specs=[pl.BlockSpec((1, NSA_KV_HEADS, ncp, LANES), lambda bi, k: (bi, 0, 0, 0)),
                   pl.BlockSpec((1, gd, ncp), lambda bi, k: (bi, 0, 0))],
        out_shape=[jax.ShapeDtypeStruct((b, NSA_KV_HEADS, ncp, LANES), BF16),
                   jax.ShapeDtypeStruct((b, gd, ncp), BF16)],
        compiler_params=_params("parallel", "arbitrary"),
        name="compress",
    )(src, pe, w1, w1bd, w2k, w2vt, caux)


SOFTMAX_FLOOR = -1e29
TAKEN = -3e38
ATTN_BATCH = 8
BLOCKS_PER_CHUNK = KEY_CHUNK // SLC_BLOCK
AUX_COLS = 16
AUX_SLOPE = BLOCKS_PER_CHUNK
ONES_ROWS = 16
CMP_ROWS_STEP = 256
CMP_AUX_SPLIT = 128


def _slope(g, r):
    if isinstance(g, int):
        return jnp.float32(ALIBI_SLOPES[g * NSA_GROUP + r])
    s = jnp.float32(ALIBI_SLOPES[r])
    for gi in range(1, NSA_KV_HEADS):
        s = jnp.where(g == gi, jnp.float32(ALIBI_SLOPES[gi * NSA_GROUP + r]), s)
    return s


def _nsa_cmp_kernel(qt_ref, kc_ref, vct_ref, poolt_ref, pair_ref, bits_ref, oct_ref, selt_ref, words_ref,
                    *, n_sel):
    i = pl.program_id(1)
    qt = selt_ref.shape[4]
    ncp = kc_ref.shape[2]
    n_slc = poolt_ref.shape[0]
    gq = qt_ref.shape[4]
    start = i * qt
    t = start + _iota((1, qt), 1)
    aux_row = _iota((AUX_COLS, gq), 0)
    col_head = _iota((1, gq), 1) // qt
    pad_rows = jnp.zeros((kc_ref.shape[3] - qt_ref.shape[3] - AUX_COLS, gq), BF16)

    def weights(g):
        slope_cols = jnp.zeros((1, gq), F32)
        for r in range(NSA_GROUP):
            slope_cols = jnp.where(col_head == r, _slope(g, r), slope_cols)
        aux = jnp.where(aux_row == 0, slope_cols * (CMP_STRIDE * CMP_AUX_SPLIT),
                        jnp.where(aux_row == 1, slope_cols * CMP_STRIDE, 0.0))
        return jnp.concatenate([qt_ref[0, g, 0], aux.astype(BF16), pad_rows], axis=0)

    def importance(g, nr, nb):
        s = _dot(kc_ref[0, g, :nr, :], weights(g))
        edge = min(nr, 2 * CMP_ROWS_STEP)
        cmp_end = ((nr - edge) + _iota((edge, 1), 0)) * CMP_STRIDE + (CMP_LEN - 1)
        visible = cmp_end <= t
        vct = vct_ref[0, g * HEAD_DIM:(g + 1) * HEAD_DIM, :nr]
        psum = jnp.zeros((nr, qt), F32)
        for r in range(NSA_GROUP):
            cols = slice(r * qt, (r + 1) * qt)
            sr = s[:, cols]
            tail = jnp.where(visible, sr[nr - edge:], NEG_INF)
            sr = tail if edge == nr else jnp.concatenate([sr[:nr - edge], tail], axis=0)
            m = jnp.maximum(jnp.max(sr, axis=0, keepdims=True), SOFTMAX_FLOOR)
            e = jnp.exp(sr - m)
            l = jnp.sum(e, axis=0, keepdims=True)
            inv = jnp.where(l > 0.0, 1.0 / l, 0.0)
            oct_ref[0, g, 0, :, cols] = (_dot(vct, e.astype(BF16)) * inv).astype(BF16)
            psum = psum + e * inv
        return _dot(poolt_ref[:nb, :nr], psum.astype(BF16))

    def visible_prefix(nr):
        nb = min(n_slc, nr * CMP_STRIDE // SLC_BLOCK)
        imp = jnp.concatenate([importance(g, nr, nb) for g in range(NSA_KV_HEADS)], axis=1)
        blk = _iota((nb, 1), 0)
        cur = jnp.concatenate([t // SLC_BLOCK] * NSA_KV_HEADS, axis=1)
        forced = (blk == 0) | (blk == cur) | (blk == cur - 1)
        score = jnp.where(forced, TAKEN, jnp.where(blk <= cur, imp, NEG_INF))
        n_forced = 1 + jnp.where(cur >= 1, 1, 0) + jnp.where(cur >= 2, 1, 0)
        blk_f = blk.astype(F32)

        def take_one(score, active):
            m = jnp.max(score, axis=0, keepdims=True)
            first = jnp.min(jnp.where(score == m, blk_f, F32(1e9)), axis=0, keepdims=True)
            hit = (blk_f == first) if active is None else ((blk_f == first) & active)
            return jnp.where(hit, TAKEN, score)

        common_rounds = max(n_sel - 3, 0)
        for _ in range(common_rounds):
            score = take_one(score, None)

        def early_rounds(score):
            for k in range(common_rounds, n_sel - 1):
                score = take_one(score, n_sel - n_forced > k)
            return score

        score = lax.cond(start < 2 * SLC_BLOCK, early_rounds, lambda sc: sc, score)
        for g in range(NSA_KV_HEADS):
            sel_g = score[:, g * qt:(g + 1) * qt] == TAKEN
            selt_ref[0, g, 0, :nb, :] = jnp.where(sel_g, 0.0, NEG_INF)
            if nb < n_slc:
                selt_ref[0, g, 0, nb:, :] = jnp.full((n_slc - nb, qt), NEG_INF, F32)
            count = _dot_nt(jnp.ones((8, qt), BF16), jnp.where(sel_g, 1.0, 0.0).astype(BF16))
            used = jnp.where(count > 0.0, 1.0, 0.0).astype(BF16)
            chunk_used = jnp.where(_dot(used, pair_ref[:nb, :]) > 0.0, 1.0, 0.0).astype(BF16)
            words_ref[0, g, 0] = _dot(chunk_used, bits_ref[...]).astype(jnp.int32)

    step = min(CMP_ROWS_STEP, ncp)
    rows_needed = jnp.minimum((start + qt - CMP_LEN) // CMP_STRIDE + 1, ncp)
    n_steps = (rows_needed + step - 1) // step
    for k in range(ncp // step):
        pl.when(n_steps == k + 1)(functools.partial(visible_prefix, (k + 1) * step))


def _nsa_cmp(q_t, kc, vct, poolt_m, pair_m, bits_m, n_sel):
    b, ng, n_tiles, qrows, qcols = q_t.shape
    ncp = kc.shape[2]
    n_slc = poolt_m.shape[0]
    const = lambda shape: pl.BlockSpec(shape, lambda bi, i: (0,) * len(shape))
    tile5 = lambda rows, cols: pl.BlockSpec((1, ng, 1, rows, cols), lambda bi, i: (bi, 0, i, 0, 0))
    return pl.pallas_call(
        functools.partial(_nsa_cmp_kernel, n_sel=n_sel),
        grid=(b, n_tiles),
        in_specs=[
            tile5(qrows, qcols),
            pl.BlockSpec((1, ng, ncp, kc.shape[3]), lambda bi, i: (bi, 0, 0, 0)),
            pl.BlockSpec((1, ng * HEAD_DIM, ncp), lambda bi, i: (bi, 0, 0)),
            const(poolt_m.shape), const(pair_m.shape), const(bits_m.shape),
        ],
        out_specs=[tile5(HEAD_DIM, qcols), tile5(n_slc, Q_TILE), tile5(8, LANES)],
        out_shape=[
            jax.ShapeDtypeStruct((b, ng, n_tiles, HEAD_DIM, qcols), BF16),
            jax.ShapeDtypeStruct((b, ng, n_tiles, n_slc, Q_TILE), F32),
            jax.ShapeDtypeStruct((b, ng, n_tiles, 8, LANES), jnp.int32),
        ],
        compiler_params=_params("parallel", "parallel"),
        name="nsa_compressed",
    )(q_t, kc, vct, poolt_m, pair_m, bits_m)


def _nsa_slc_kernel(words_ref, qt_ref, ks_ref, vst_ref, kw_ref, vwt_ref, selt_ref, oct_ref, gt_ref,
                    out_ref, m_ref, l_ref, acc_ref, ow_ref, list_ref, *, words_per_tile):
    bi = pl.program_id(0)
    g = pl.program_id(1)
    i = pl.program_id(2)
    n_tiles = pl.num_programs(2)
    qt = out_ref.shape[1]
    q_rows = qt_ref[0, 0, 0]
    gq = q_rows.shape[1]
    start = i * qt
    lane_f = _iota((1, qt), 1).astype(F32)
    key_in_chunk = _iota((KEY_CHUNK, qt), 0)
    query_in_tile = _iota((KEY_CHUNK, qt), 1)

    aux_row = _iota((AUX_COLS, gq), 0)
    col_head = _iota((1, gq), 1) // qt
    slope_cols = jnp.zeros((1, gq), F32)
    for r in range(NSA_GROUP):
        slope_cols = jnp.where(col_head == r, _slope(g, r), slope_cols)
    aux_base = jnp.where(aux_row == AUX_SLOPE, slope_cols, 0.0)
    pad_rows = jnp.zeros((ks_ref.shape[3] - q_rows.shape[0] - AUX_COLS, gq), BF16)

    def scores_and_values(slots):
        scores = [_dot(k, jnp.concatenate([q_rows, aux.astype(BF16), pad_rows], axis=0))
                  for k, aux, _, _, _ in slots]
        values = jnp.concatenate([v for _, _, v, _, _ in slots], axis=1)
        return scores, jnp.concatenate([values, jnp.ones((ONES_ROWS, values.shape[1]), BF16)], axis=0)

    def softmax_step(slots, scores, v_cat, r, m_old):
        cols = slice(r * qt, (r + 1) * qt)
        srs, tops = [], []
        for (_, _, _, shift, mask), s in zip(slots, scores):
            sr = s[:, cols] if mask is None else jnp.where(mask, s[:, cols], NEG_INF)
            srs.append(sr)
            tops.append(jnp.max(sr, axis=0, keepdims=True) + shift[r])
        m_new = functools.reduce(jnp.maximum, tops, m_old)
        ps = [jnp.exp((sr - (m_new - slot[3][r])).astype(BF16)) for slot, sr in zip(slots, srs)]
        weighted = _dot(v_cat, jnp.concatenate(ps, axis=0))
        return m_new, weighted[HEAD_DIM:HEAD_DIM + 1], weighted[:HEAD_DIM]

    def shifts(dist0, ok):
        rows = [-_slope(g, r) * (dist0 + lane_f) for r in range(NSA_GROUP)]
        return rows if ok is None else [jnp.where(ok, row, NEG_INF) for row in rows]

    word_base = ((bi * NSA_KV_HEADS + g) * n_tiles + i) * words_per_tile
    list_ref[0] = 0

    def scan_word(w, n):
        word = words_ref[word_base + w]

        def scan_bits(n):
            for bit in range(WORD_BITS):
                c = w * WORD_BITS + bit
                list_ref[n] = c
                n = n + jnp.where(c < i, (word >> bit) & 1, 0)
            return n

        return lax.cond(word != 0, scan_bits, lambda n: n, n)

    n_listed = lax.fori_loop(0, (i + WORD_BITS - 1) // WORD_BITS, scan_word, 0)

    def selected_slot(c, ok, mask):
        at = pl.multiple_of(c * KEY_CHUNK, KEY_CHUNK)
        bias = selt_ref[0, 0, 0, pl.ds(c * BLOCKS_PER_CHUNK, BLOCKS_PER_CHUNK), :]
        aux = aux_base
        for blk in range(BLOCKS_PER_CHUNK):
            aux = jnp.where(aux_row == blk, jnp.concatenate([bias[blk:blk + 1]] * NSA_GROUP, axis=1), aux)
        return (ks_ref[0, 0, pl.ds(at, KEY_CHUNK), :], aux, vst_ref[0, 0, c],
                shifts((start - c * KEY_CHUNK).astype(F32), ok), mask)

    def listed_slot(idx):
        ok = idx < n_listed
        c = jnp.where(ok, list_ref[jnp.minimum(idx, jnp.maximum(n_listed - 1, 0))], 0)
        return selected_slot(c, ok, None)

    floor = jnp.full((1, qt), SOFTMAX_FLOOR, F32)

    n_back = WINDOW // KEY_CHUNK
    slots = []
    for j in range(n_back + 1):
        cs = start - WINDOW + j * KEY_CHUNK
        chunk = jnp.maximum(cs, 0) // KEY_CHUNK
        at = pl.multiple_of(chunk * KEY_CHUNK, KEY_CHUNK)
        mask = (query_in_tile < key_in_chunk) if j == 0 else (
            (key_in_chunk <= query_in_tile) if j == n_back else None)
        slots.append((kw_ref[0, 0, pl.ds(at, KEY_CHUNK), :], aux_base, vwt_ref[0, 0, chunk],
                      shifts(F32(WINDOW - j * KEY_CHUNK), cs >= 0), mask))
    scores, v_cat = scores_and_values(slots)
    for r in range(NSA_GROUP):
        _, total, weighted = softmax_step(slots, scores, v_cat, r, floor)
        ow_ref[:, r * qt:(r + 1) * qt] = weighted * jnp.where(total > 0.0, 1.0 / total, 0.0)

    slots = [selected_slot(i, None, key_in_chunk <= query_in_tile)] + [
        listed_slot(j) for j in range(ATTN_BATCH - 1)]
    scores, v_cat = scores_and_values(slots)
    for r in range(NSA_GROUP):
        m_ref[r], l_ref[r], acc_ref[:, r * qt:(r + 1) * qt] = softmax_step(slots, scores, v_cat, r, floor)

    def batch(it, carry):
        first = ATTN_BATCH - 1 + it * ATTN_BATCH
        slots = [listed_slot(first + j) for j in range(ATTN_BATCH)]
        scores, v_cat = scores_and_values(slots)
        for r in range(NSA_GROUP):
            cols = slice(r * qt, (r + 1) * qt)
            m_old = m_ref[r]
            m_new, total, weighted = softmax_step(slots, scores, v_cat, r, m_old)
            alpha = jnp.exp(m_old - m_new)
            l_ref[r] = alpha * l_ref[r] + total
            acc_ref[:, cols] = alpha * acc_ref[:, cols] + weighted
            m_ref[r] = m_new
        return carry

    n_rest = jnp.maximum(n_listed - (ATTN_BATCH - 1), 0)
    lax.fori_loop(0, (n_rest + ATTN_BATCH - 1) // ATTN_BATCH, batch, 0)

    def finalize(r):
        l = l_ref[r]
        return acc_ref[:, r * qt:(r + 1) * qt] * jnp.where(l > 0.0, 1.0 / l, 0.0)

    outs = []
    for r in range(NSA_GROUP):
        cols = slice(r * qt, (r + 1) * qt)
        col = (g * NSA_GROUP + r) * NSA_BRANCHES
        gate = lambda br: jax.nn.sigmoid(gt_ref[0, pl.ds(col + br, 1), :])
        outs.append(gate(0) * oct_ref[0, 0, 0, :, cols].astype(F32) + gate(1) * finalize(r)
                    + gate(2) * ow_ref[:, cols])
    out_ref[0] = jnp.concatenate(outs, axis=0).T.astype(out_ref.dtype)


def _nsa_slc(words, q_t, ks, vst, kw, vwt, selt, oct, gates_t, words_per_tile):
    b, _, t, kw_cols = ks.shape
    _, _, n_tiles, qrows, qcols = q_t.shape
    n_slc = selt.shape[3]
    n_chunks = vst.shape[2]
    gw = NSA_GROUP * HEAD_DIM
    once = dict(pipeline_mode=pl.Buffered(1))
    k_spec = pl.BlockSpec((1, 1, t, kw_cols), lambda bi, g, i, w: (bi, g, 0, 0), **once)
    vt_spec = pl.BlockSpec((1, 1, n_chunks, HEAD_DIM, KEY_CHUNK), lambda bi, g, i, w: (bi, g, 0, 0, 0), **once)
    tile5 = lambda rows, cols: pl.BlockSpec((1, 1, 1, rows, cols), lambda bi, g, i, w: (bi, g, i, 0, 0))
    grid_spec = pltpu.PrefetchScalarGridSpec(
        num_scalar_prefetch=1,
        grid=(b, NSA_KV_HEADS, n_tiles),
        in_specs=[
            tile5(qrows, qcols),
            k_spec, vt_spec, k_spec, vt_spec,
            tile5(n_slc, Q_TILE),
            tile5(HEAD_DIM, qcols),
            pl.BlockSpec((1, GATE_PAD, Q_TILE), lambda bi, g, i, w: (bi, 0, i)),
        ],
        out_specs=pl.BlockSpec((1, Q_TILE, gw), lambda bi, g, i, w: (bi, i, g)),
        scratch_shapes=[pltpu.VMEM((NSA_GROUP, 1, Q_TILE), F32), pltpu.VMEM((NSA_GROUP, 1, Q_TILE), F32),
                        pltpu.VMEM((HEAD_DIM, qcols), F32), pltpu.VMEM((HEAD_DIM, qcols), F32),
                        pltpu.SMEM((n_chunks,), jnp.int32)],
    )
    return pl.pallas_call(
        functools.partial(_nsa_slc_kernel, words_per_tile=words_per_tile),
        grid_spec=grid_spec,
        out_shape=jax.ShapeDtypeStruct((b, t, NSA_HEADS * HEAD_DIM), BF16),
        compiler_params=_params("parallel", "parallel", "parallel"),
        name="nsa_selected_window",
    )(words, q_t, ks, vst, kw, vwt, selt, oct, gates_t)


POOL_HALO = 16
CONV_HALO = 8


def _merge_kernel(h_ref, g_ref, nsa_ref, pool_ref, pool_halo_ref, conv_ref, conv_halo_ref,
                  wmg_ref, wnsa_ref, pool_bd_ref, pool_scale_ref, wpool_ref, convw_ref, wconv_ref, wo_ref,
                  out_ref, pool_ext, conv_ext, *, seq_len):
    i = pl.program_id(0)
    tm, d = h_ref.shape
    cw = pool_ref.shape[1]
    pos0 = (i * tm) % seq_len
    keep_halo = jnp.where(pos0 == 0, 0.0, 1.0)
    pos = pos0 + _iota((tm, 1), 0)

    u = pool_ref[...]
    pool_ext[0:POOL_HALO, :] = pool_halo_ref[...] * keep_halo
    pool_ext[POOL_HALO:, :] = u
    lane_group = _iota((1, cw), 1) // (cw // len(POOL_WINDOWS))
    total = u
    mean = jnp.zeros_like(u)
    done = 1
    for gi, win in enumerate(POOL_WINDOWS):
        for k in range(done, win):
            total = total + pool_ext[POOL_HALO - k:POOL_HALO - k + tm, :]
        done = win
        cnt = jnp.minimum(pos + 1, win).astype(F32)
        mean = jnp.where(lane_group == gi, total / cnt, mean)
    pooled = (mean - u).astype(BF16)
    mixed = _dot(pooled, pool_bd_ref[...]) * pool_scale_ref[...]
    y_pool = _dot(mixed.astype(BF16), wpool_ref[...])

    ch = conv_ref[:, 0:cw]
    cb = conv_ref[:, cw:2 * cw]
    cc = conv_ref[:, 2 * cw:3 * cw]
    conv_ext[0:CONV_HALO, :] = conv_halo_ref[:, 0:cw] * conv_halo_ref[:, 2 * cw:3 * cw] * keep_halo
    conv_ext[CONV_HALO:, :] = cc * ch
    y = jnp.zeros((tm, cw), F32)
    for k in range(CONV_K):
        off = CONV_HALO - (CONV_K - 1) + k
        y = y + convw_ref[k:k + 1, :] * conv_ext[off:off + tm, :]
    y_conv = _dot((cb * y).astype(BF16), wconv_ref[...])

    y_nsa = _dot(nsa_ref[...], wnsa_ref[...])

    h = h_ref[...]
    xn = _rms_norm(h, g_ref[...]).astype(BF16)
    merged = jnp.zeros((tm, d), F32)
    for br, y_br in enumerate((y_nsa, y_pool, y_conv)):
        mg = jax.nn.sigmoid(_dot(xn, wmg_ref[:, br * d:(br + 1) * d]))
        merged = merged + mg * y_br
    out_ref[...] = h + _dot(merged.astype(BF16), wo_ref[...])


def _merge(h, g, nsa, pool_u, conv, wmg, wnsa, pool_bd, pool_scale, wpool, convw, wconv, wo, seq_len):
    n, d = h.shape
    tm = TOKEN_TILE
    cw = pool_u.shape[1]
    row = lambda width: pl.BlockSpec((tm, width), lambda i: (i, 0))
    full = lambda a: pl.BlockSpec(a.shape, lambda i: (0,) * a.ndim)
    halo = lambda rows, width: pl.BlockSpec(
        (rows, width), lambda i: (jnp.maximum(i * (tm // rows) - 1, 0), 0))
    return pl.pallas_call(
        functools.partial(_merge_kernel, seq_len=seq_len),
        grid=(n // tm,),
        in_specs=[row(d), full(g), row(nsa.shape[1]), row(cw), halo(POOL_HALO, cw),
                  row(conv.shape[1]), halo(CONV_HALO, conv.shape[1]),
                  full(wmg), full(wnsa), full(pool_bd), full(pool_scale), full(wpool), full(convw),
                  full(wconv), full(wo)],
        out_specs=row(d),
        out_shape=jax.ShapeDtypeStruct((n, d), F32),
        scratch_shapes=[pltpu.VMEM((tm + POOL_HALO, cw), F32), pltpu.VMEM((tm + CONV_HALO, cw), F32)],
        compiler_params=_params("parallel"),
        name="merge",
    )(h, g, nsa, pool_u, pool_u, conv, conv, wmg, wnsa, pool_bd, pool_scale, wpool, convw, wconv, wo)


def _route(logits):
    lane = _iota(logits.shape, 1)
    lane_f = lane.astype(F32)
    big = F32(1e9)
    is_group = lane < N_EXPERT_GROUPS
    gl = jnp.where(is_group, logits, NEG_INF)
    g_max = jnp.max(gl, axis=1, keepdims=True)
    g_sel = jnp.min(jnp.where(gl == g_max, lane_f, big), axis=1, keepdims=True)
    g_prob = 1.0 / jnp.sum(jnp.where(is_group, jnp.exp(gl - g_max), 0.0), axis=1, keepdims=True)
    lo = N_EXPERT_GROUPS + EXPERTS_PER_GROUP * g_sel
    in_group = (lane_f >= lo) & (lane_f < lo + EXPERTS_PER_GROUP)
    el = jnp.where(in_group, logits, NEG_INF)
    v1 = jnp.max(el, axis=1, keepdims=True)
    i1 = jnp.min(jnp.where((el == v1) & in_group, lane_f, big), axis=1, keepdims=True)
    el2 = jnp.where(lane_f == i1, NEG_INF, el)
    rest = in_group & (lane_f != i1)
    v2 = jnp.max(el2, axis=1, keepdims=True)
    i2 = jnp.min(jnp.where((el2 == v2) & rest, lane_f, big), axis=1, keepdims=True)
    e2 = jnp.exp(v2 - v1)
    w1 = g_prob / (1.0 + e2)
    w2 = g_prob * e2 / (1.0 + e2)
    return jnp.where(lane_f == i1, w1, 0.0) + jnp.where(lane_f == i2, w2, 0.0), g_sel


GROUP_LANE = N_EXPERT_GROUPS + N_EXPERTS


RANK_LANE = GROUP_LANE + 1


def _router_kernel(h_ref, g_ref, wr_ref, br_ref, before_ref, xn_ref, comb_ref, count_ref):
    @pl.when(pl.program_id(0) == 0)
    def _():
        count_ref[...] = jnp.zeros(count_ref.shape, F32)

    xn = _rms_norm(h_ref[...], g_ref[...])
    xn_hi = xn.astype(BF16)
    xn_lo = (xn - xn_hi.astype(F32)).astype(BF16)
    logits = (_dot(xn_hi, wr_ref[0]) + (_dot(xn_hi, wr_ref[1]) + _dot(xn_lo, wr_ref[0]))) + br_ref[...]
    comb, g_sel = _route(logits)
    lane = _iota(comb.shape, 1)
    chose = jnp.where(lane.astype(F32) == g_sel, 1.0, 0.0)
    earlier = _dot(before_ref[...], chose.astype(BF16)) + count_ref[0:1, :]
    rank = jnp.sum(chose * earlier, axis=1, keepdims=True)
    count_ref[0:1, :] = count_ref[0:1, :] + jnp.sum(chose, axis=0, keepdims=True)
    comb_ref[...] = jnp.where(lane == GROUP_LANE, g_sel, jnp.where(lane == RANK_LANE, rank, comb))
    first = N_EXPERT_GROUPS + EXPERTS_PER_GROUP * g_sel
    local = jnp.zeros(comb.shape, F32)
    for e in range(EXPERTS_PER_GROUP):
        c_e = jnp.sum(jnp.where(lane.astype(F32) == first + e, comb, 0.0), axis=1, keepdims=True)
        local = jnp.where((lane == e) | (lane == EXPERTS_PER_GROUP + e), c_e, local)
    local_hi = local.astype(BF16)
    d = xn_hi.shape[1]
    xn_ref[:, :d] = xn_hi
    xn_ref[:, d:] = jnp.where(lane < EXPERTS_PER_GROUP, local_hi, (local - local_hi.astype(F32)).astype(BF16))


def _router(h, g, wr, br, before):
    n, d = h.shape
    tm = MOE_TILE
    full = lambda a: pl.BlockSpec(a.shape, lambda i: (0,) * a.ndim)
    return pl.pallas_call(
        _router_kernel,
        grid=(n // tm,),
        in_specs=[pl.BlockSpec((tm, d), lambda i: (i, 0)), full(g), full(wr), full(br), full(before)],
        out_specs=[pl.BlockSpec((tm, d + ROUTER_PAD), lambda i: (i, 0)),
                   pl.BlockSpec((tm, ROUTER_PAD), lambda i: (i, 0)),
                   pl.BlockSpec((8, ROUTER_PAD), lambda i: (0, 0))],
        out_shape=[jax.ShapeDtypeStruct((n, d + ROUTER_PAD), BF16), jax.ShapeDtypeStruct((n, ROUTER_PAD), F32),
                   jax.ShapeDtypeStruct((8, ROUTER_PAD), F32)],
        compiler_params=_params("arbitrary"),
        name="router",
    )(h, g, wr, br, before)


def _experts_kernel(tile_group_ref, n_active_ref, x_ref, w1_ref, w3_ref, w2_ref, out_ref, acc_ref):
    i = pl.program_id(0)

    @pl.when(i < n_active_ref[0])
    def _():
        d = out_ref.shape[1]
        x = x_ref[:, :d]
        comb = x_ref[:, d:].astype(F32)
        lane = _iota(comb.shape, 1)
        for e in range(EXPERTS_PER_GROUP):
            c_e = jnp.sum(jnp.where((lane == e) | (lane == EXPERTS_PER_GROUP + e), comb, 0.0),
                          axis=1, keepdims=True)
            a = (jax.nn.silu(_dot(x, w1_ref[0, e].astype(BF16))) * _dot(x, w3_ref[0, e].astype(BF16))) * c_e
            y = _dot(a.astype(BF16), w2_ref[0, e].astype(BF16))
            if e == 0:
                acc_ref[...] = y
            else:
                acc_ref[...] += y
        out_ref[...] = acc_ref[...].astype(out_ref.dtype)


def _experts(tile_group, n_active, x_sorted, w1, w3, w2):
    ns = x_sorted.shape[0]
    d = w1.shape[2]
    tm = MOE_SORT_TILE
    group_w = lambda w: pl.BlockSpec((1,) + w.shape[1:], lambda i, tg, na: (tg[i], 0, 0, 0),
                                     pipeline_mode=pl.Buffered(1))
    grid_spec = pltpu.PrefetchScalarGridSpec(
        num_scalar_prefetch=2,
        grid=(ns // tm,),
        in_specs=[
            pl.BlockSpec((tm, x_sorted.shape[1]), lambda i, tg, na: (i, 0)),
            group_w(w1), group_w(w3), group_w(w2),
        ],
        out_specs=pl.BlockSpec((tm, d), lambda i, tg, na: (i, 0)),
        scratch_shapes=[pltpu.VMEM((tm, d), F32)],
    )
    return pl.pallas_call(
        _experts_kernel,
        grid_spec=grid_spec,
        out_shape=jax.ShapeDtypeStruct((ns, d), BF16),
        compiler_params=_params("arbitrary"),
        name="experts",
    )(tile_group, n_active, x_sorted, w1, w3, w2)


def _residual_kernel(h_ref, y_ref, gf_ref, out_ref, *, final_norm):
    out = h_ref[...] + y_ref[...].astype(F32)
    out_ref[...] = _rms_norm(out, gf_ref[...]) if final_norm else out


def _residual(h, y, gf, final_norm):
    n, d = h.shape
    tm = MOE_TILE
    row = pl.BlockSpec((tm, d), lambda i: (i, 0))
    return pl.pallas_call(
        functools.partial(_residual_kernel, final_norm=final_norm),
        grid=(n // tm,),
        in_specs=[row, row, pl.BlockSpec(gf.shape, lambda i: (0, 0))],
        out_specs=row,
        out_shape=jax.ShapeDtypeStruct((n, d), F32),
        compiler_params=_params("parallel"),
        name="residual",
    )(h, y, gf)


def _group_sort_plan(group_id, rank, counts, tile):
    n = group_id.shape[0]
    n_slots = n + N_EXPERT_GROUPS * tile
    padded = (counts + tile - 1) // tile * tile
    ends = jnp.cumsum(padded)
    slot = (ends - padded)[group_id] + rank
    source = jnp.zeros((n_slots,), jnp.int32).at[slot].set(jnp.arange(n, dtype=jnp.int32))
    tile_start = jnp.arange(n_slots // tile, dtype=jnp.int32) * tile
    tile_group = jnp.minimum(jnp.searchsorted(ends, tile_start, side="right"), N_EXPERT_GROUPS - 1)
    return slot, source, tile_group.astype(jnp.int32), (ends[-1:] // tile).astype(jnp.int32)


def _moe(h, g, wr, br, before, w1, w3, w2, gf, final_norm):
    xn, comb, counts = _router(h, g, wr, br, before)
    as_int = lambda a: a.astype(jnp.int32)
    slot, source, tile_group, n_active = _group_sort_plan(
        as_int(comb[:, GROUP_LANE]), as_int(comb[:, RANK_LANE]), as_int(counts[0, :N_EXPERT_GROUPS]),
        MOE_SORT_TILE)
    grouped = lambda w: w.reshape((N_EXPERT_GROUPS, EXPERTS_PER_GROUP) + w.shape[1:])
    rows = lambda a, idx: jnp.take(a, idx, axis=0, mode="clip")
    y_sorted = _experts(tile_group, n_active, rows(xn, source), grouped(w1), grouped(w3), grouped(w2))
    return _residual(h, rows(y_sorted, slot), gf, final_norm)


def _selection_constants(seq_len):
    ncp = seq_len // CMP_STRIDE
    n_slc = seq_len // SLC_BLOCK
    ratio = SLC_BLOCK // CMP_STRIDE
    lead = CMP_LEN // CMP_STRIDE - 1
    c = np.arange(ncp)[:, None]
    j = np.arange(n_slc)[None, :]
    pool_m = ((c >= ratio * j - lead) & (c < ratio * j + ratio)).astype(np.float32)
    blocks_per_chunk = KEY_CHUNK // SLC_BLOCK
    n_chunks = seq_len // KEY_CHUNK
    pair_m = np.zeros((n_slc, LANES * ((n_chunks + LANES - 1) // LANES)), np.float32)
    pair_m[np.arange(n_slc), np.arange(n_slc) // blocks_per_chunk] = 1.0
    n_words = (n_chunks + WORD_BITS - 1) // WORD_BITS
    bits_m = np.zeros((pair_m.shape[1], LANES), np.float32)
    ch = np.arange(n_chunks)
    bits_m[ch, ch // WORD_BITS] = 2.0 ** (ch % WORD_BITS)
    key_aux = np.zeros((TOKEN_TILE, LANES), np.float32)
    in_chunk = np.arange(TOKEN_TILE) % KEY_CHUNK
    key_aux[np.arange(TOKEN_TILE), HEAD_DIM + in_chunk // SLC_BLOCK] = 1.0
    key_aux[:, HEAD_DIM + AUX_SLOPE] = in_chunk
    cmp_aux = np.zeros((ncp, LANES), np.float32)
    cmp_aux[:, HEAD_DIM] = np.arange(ncp) // CMP_AUX_SPLIT
    cmp_aux[:, HEAD_DIM + 1] = np.arange(ncp) % CMP_AUX_SPLIT
    as_bf16 = lambda a: jnp.asarray(a, BF16)
    return as_bf16(pool_m.T), as_bf16(pair_m), as_bf16(bits_m), jnp.asarray(key_aux), jnp.asarray(cmp_aux), n_words


def kernel(x, norm1_g, w_in, cmp_pe, cmp_w1, cmp_w2, w_nsa_proj, pool_w, pool_scale, w_pool_proj, conv_w,
           w_conv_proj, w_o, norm2_g, router_group_w, router_group_b, router_expert_w, router_expert_b,
           expert_w1, expert_w3, expert_w2, final_norm_g):
    b, t, d = x.shape
    n = b * t
    depth = w_in.shape[0]
    dq = NSA_HEADS * HEAD_DIM
    dkv = 6 * NSA_KV_HEADS * HEAD_DIM
    dgate = NSA_HEADS * NSA_BRANCHES
    cw = d // 4
    assert t % TOKEN_TILE == 0 and n % MOE_TILE == 0 and t % Q_TILE == 0
    n_slc = t // SLC_BLOCK
    n_sel = min(SLC_TOPN, n_slc)
    n_chunks16 = t // CMP_STRIDE
    kvw = NSA_KV_HEADS * HEAD_DIM
    poolt_m, pair_m, bits_m, key_aux, cmp_aux, n_words = _selection_constants(t)
    assert Q_TILE == KEY_CHUNK and AUX_SLOPE < AUX_COLS and KEY_CHUNK <= 256
    assert n_chunks16 <= 256 * CMP_AUX_SPLIT and n_chunks16 % min(CMP_ROWS_STEP, n_chunks16) == 0
    assert n_words <= LANES and pair_m.shape[1] == LANES

    before = jnp.asarray(np.tril(np.ones((MOE_TILE, MOE_TILE), np.float32), -1), BF16)
    h = x.reshape(n, d)
    for l in range(depth):
        wl = w_in[l]
        o_gate = dq + dkv
        o_pool = o_gate + dgate
        o_merge = o_pool + cw + 3 * cw
        kv_cols = lambda kind: wl[:, dq + kind * kvw:dq + (kind + 1) * kvw]
        wq_t = (wl[:, :dq] * (HEAD_DIM ** -0.5)).T.astype(BF16)
        wv_t = jnp.concatenate([kv_cols(3), kv_cols(5)], axis=1).T.astype(BF16)
        wg_t = jnp.pad(wl[:, o_gate:o_pool], ((0, 0), (0, GATE_PAD - dgate))).T.astype(BF16)
        no_aux = jnp.zeros((d, LANES - HEAD_DIM), F32)
        wk = jnp.concatenate([piece for kind in (2, 4) for gi in range(NSA_KV_HEADS)
                              for piece in (kv_cols(kind)[:, gi * HEAD_DIM:(gi + 1) * HEAD_DIM], no_aux)],
                             axis=1).astype(BF16)
        wn = jnp.concatenate([kv_cols(0), kv_cols(1), wl[:, o_pool:o_merge]], axis=1).astype(BF16)
        wmg = wl[:, o_merge:].astype(BF16)
        pool_bd = jax.scipy.linalg.block_diag(*[pool_w[l, gi] for gi in range(pool_w.shape[1])]).astype(BF16)
        convw = jnp.pad(conv_w[l], ((0, 8 - CONV_K), (0, 0)))
        wr = jnp.pad(jnp.concatenate([router_group_w[l], router_expert_w[l]], axis=1),
                     ((0, 0), (0, ROUTER_PAD - N_EXPERT_GROUPS - N_EXPERTS)))
        wr_hi = wr.astype(BF16)
        wr = jnp.stack([wr_hi, (wr - wr_hi.astype(F32)).astype(BF16)])
        br = jnp.pad(jnp.concatenate([router_group_b[l], router_expert_b[l]]),
                     (0, ROUTER_PAD - N_EXPERT_GROUPS - N_EXPERTS))[None, :]
        pe = jnp.broadcast_to(cmp_pe[l].reshape(2, 1, CMP_LEN * HEAD_DIM), (2, 8, CMP_LEN * HEAD_DIM)).astype(BF16)
        halves = CMP_LEN // CMP_STRIDE
        w1_bd = jnp.einsum("khldc,gq->khlgdqc",
                           cmp_w1[l].reshape(2, halves, CMP_STRIDE, HEAD_DIM, CMP_HIDDEN),
                           jnp.eye(NSA_KV_HEADS, dtype=F32))
        w1_bd = w1_bd.reshape(2, halves, CMP_STRIDE * kvw, NSA_KV_HEADS * CMP_HIDDEN).astype(BF16)
        w2_k = jnp.pad(cmp_w2[l, 0], ((0, 0), (0, LANES - HEAD_DIM))).astype(BF16)
        w2_vt = cmp_w2[l, 1].T.astype(BF16)

        q_t, vst, vwt, gates_t, ks, kw, cmp_src, pool_u, conv = _inproj(
            h, norm1_g[l][None, :], wq_t, wv_t, wg_t, wk, wn, key_aux, b)
        kc_aux, vc_t = _compress(cmp_src, pe, cmp_w1[l].astype(BF16), w1_bd, w2_k, w2_vt, cmp_aux)
        oc_t, sel_t, words = _nsa_cmp(q_t, kc_aux, vc_t, poolt_m, pair_m, bits_m, n_sel)
        words = words[:, :, :, 0, :n_words].reshape(-1)
        nsa = _nsa_slc(words, q_t, ks, vst, kw, vwt, sel_t, oc_t, gates_t, n_words)
        h = _merge(h, norm1_g[l][None, :], nsa.reshape(n, dq), pool_u, conv, wmg,
                   w_nsa_proj[l].astype(BF16), pool_bd, pool_scale[l][None, :], w_pool_proj[l].astype(BF16),
                   convw, w_conv_proj[l].astype(BF16), w_o[l].astype(BF16), t)
        h = _moe(h, norm2_g[l][None, :], wr, br, before, expert_w1[l], expert_w3[l], expert_w2[l],
                 final_norm_g[None, :], final_norm=(l == depth - 1))
    return h.reshape(b, t, d)
```

```python
import functools

import jax
import jax.numpy as jnp
import numpy as np
from jax import lax
from jax.experimental import pallas as pl
from jax.experimental.pallas import tpu as pltpu

F32 = jnp.float32
BF16 = jnp.bfloat16

HEAD_DIM = 64
NSA_HEADS = 8
NSA_KV_HEADS = 2
NSA_GROUP = NSA_HEADS // NSA_KV_HEADS
CMP_LEN = 32
CMP_STRIDE = 16
CMP_HIDDEN = 4 * HEAD_DIM
SLC_BLOCK = 64
SLC_TOPN = 16
WINDOW = 512
NSA_BRANCHES = 3
POOL_WINDOWS = (2, 4, 8, 16)
CONV_K = 3
N_BRANCHES = 3
N_EXPERT_GROUPS = 4
EXPERTS_PER_GROUP = 8
N_EXPERTS = N_EXPERT_GROUPS * EXPERTS_PER_GROUP
RMS_EPS = 1e-6
NEG_INF = -1e30
FORCE_SCORE = 1e30
ALIBI_SLOPES = tuple(float(2.0 ** (-8.0 * (h + 1) / NSA_HEADS)) for h in range(NSA_HEADS))

LANES = 128
VMEM_LIMIT = 56 * 1024 * 1024
TOKEN_TILE = 512
MOE_TILE = 1024
MOE_SORT_TILE = 512
Q_TILE = 128
KEY_CHUNK = 128
GATE_PAD = LANES
ROUTER_PAD = LANES
WORD_BITS = 16


def _params(*semantics):
    return pltpu.CompilerParams(dimension_semantics=semantics, vmem_limit_bytes=VMEM_LIMIT)


def _dot(a, b):
    return jnp.dot(a, b, preferred_element_type=F32)


def _dot_nt(a, b):
    return lax.dot_general(a, b, (((1,), (1,)), ((), ())), preferred_element_type=F32)


def _rms_norm(x, g):
    y = x * lax.rsqrt(jnp.mean(x * x, axis=-1, keepdims=True) + RMS_EPS)
    return y * g


def _iota(shape, dim):
    return lax.broadcasted_iota(jnp.int32, shape, dim)


def _inproj_kernel(x_ref, g_ref, wq_ref, wv_ref, wg_ref, wk_ref, wn_ref, kaux_ref,
                   q_ref, vs_ref, vw_ref, gate_ref, ks_ref, kw_ref, cmp_ref, pool_ref, conv_ref):
    xn = _rms_norm(x_ref[...], g_ref[...]).astype(BF16)
    sub_tiles = x_ref.shape[0] // Q_TILE
    q_t = _dot_nt(wq_ref[...], xn)
    for g in range(NSA_KV_HEADS):
        for j in range(sub_tiles):
            for r in range(NSA_GROUP):
                head = g * NSA_GROUP + r
                q_ref[0, g, j, :, r * Q_TILE:(r + 1) * Q_TILE] = q_t[
                    head * HEAD_DIM:(head + 1) * HEAD_DIM, j * Q_TILE:(j + 1) * Q_TILE].astype(BF16)
    v_t = _dot_nt(wv_ref[...], xn)
    k = _dot(xn, wk_ref[...])
    for branch, (v_ref, k_ref) in enumerate(((vs_ref, ks_ref), (vw_ref, kw_ref))):
        for g in range(NSA_KV_HEADS):
            slab = branch * NSA_KV_HEADS + g
            for j in range(sub_tiles):
                v_ref[0, g, j] = v_t[slab * HEAD_DIM:(slab + 1) * HEAD_DIM,
                                     j * KEY_CHUNK:(j + 1) * KEY_CHUNK].astype(BF16)
            k_ref[0, g] = (k[:, slab * LANES:(slab + 1) * LANES] + kaux_ref[...]).astype(BF16)
    gate_ref[0] = _dot_nt(wg_ref[...], xn)
    col = 0
    for ref in (cmp_ref, pool_ref, conv_ref):
        width = ref.shape[-1]
        ref[...] = _dot(xn, wn_ref[:, col:col + width]).reshape(ref.shape)
        col += width


def _inproj(h, g, wq_t, wv_t, wg_t, wk, wn, kaux, batch):
    n, d = h.shape
    t = n // batch
    tm = TOKEN_TILE
    steps = t // tm
    sub = tm // Q_TILE
    cw = d // 4
    gq = NSA_GROUP * Q_TILE
    full = lambda a: pl.BlockSpec(a.shape, lambda i: (0,) * a.ndim)
    row = lambda width: pl.BlockSpec((tm, width), lambda i: (i, 0))
    tiles = lambda rows, cols: pl.BlockSpec((1, NSA_KV_HEADS, sub, rows, cols),
                                            lambda i: (i // steps, 0, i % steps, 0, 0))
    keys = pl.BlockSpec((1, NSA_KV_HEADS, tm, LANES), lambda i: (i // steps, 0, i % steps, 0))
    sds = jax.ShapeDtypeStruct
    v_shape = sds((batch, NSA_KV_HEADS, t // KEY_CHUNK, HEAD_DIM, KEY_CHUNK), BF16)
    k_shape = sds((batch, NSA_KV_HEADS, t, LANES), BF16)
    return pl.pallas_call(
        _inproj_kernel,
        grid=(n // tm,),
        in_specs=[row(d), full(g), full(wq_t), full(wv_t), full(wg_t), full(wk), full(wn), full(kaux)],
        out_specs=[tiles(HEAD_DIM, gq), tiles(HEAD_DIM, KEY_CHUNK), tiles(HEAD_DIM, KEY_CHUNK),
                   pl.BlockSpec((1, GATE_PAD, tm), lambda i: (i // steps, 0, i % steps)),
                   keys, keys,
                   pl.BlockSpec((1, tm, 2 * NSA_KV_HEADS * HEAD_DIM), lambda i: (i // steps, i % steps, 0)),
                   row(cw), row(3 * cw)],
        out_shape=[sds((batch, NSA_KV_HEADS, t // Q_TILE, HEAD_DIM, gq), BF16), v_shape, v_shape,
                   sds((batch, GATE_PAD, t), F32), k_shape, k_shape,
                   sds((batch, t, 2 * NSA_KV_HEADS * HEAD_DIM), F32), sds((n, cw), F32), sds((n, 3 * cw), F32)],
        compiler_params=_params("parallel"),
        name="inproj",
    )(h, g, wq_t, wv_t, wg_t, wk, wn, kaux)


def _gelu_tanh(x):
    return 0.5 * x * (1.0 + jnp.tanh(0.7978845608028654 * (x + 0.044715 * x * x * x)))


def _compress_kernel(src_ref, pe_ref, w1_ref, w1bd_ref, w2k_ref, w2vt_ref, caux_ref, kc_ref, vct_ref):
    kind = pl.program_id(1)
    ncp = kc_ref.shape[2]
    hidden = w1_ref.shape[2]
    pieces = [src_ref[0, pl.ds(l, ncp, stride=CMP_STRIDE), :].astype(BF16) for l in range(CMP_STRIDE)]
    chunk = jnp.concatenate(pieces, axis=1)
    first = _dot(chunk, w1bd_ref[0, 0])
    second = _dot(chunk, w1bd_ref[0, 1])
    bias = _dot(pe_ref[0], w1_ref[0])[0:1, :]
    hid = first + pltpu.roll(second, ncp - 1, 0) + jnp.concatenate([bias] * NSA_KV_HEADS, axis=1)
    row = _iota((ncp, 1), 0)
    act = jnp.where(row < ncp - 1, _gelu_tanh(hid), 0.0).astype(BF16)
    for g in range(NSA_KV_HEADS):
        act_g = act[:, g * hidden:(g + 1) * hidden]

        @pl.when(kind == 0)
        def _():
            kc_ref[0, g] = (_dot(act_g, w2k_ref[...]) + caux_ref[...]).astype(BF16)

        @pl.when(kind == 1)
        def _():
            vct_ref[0, g * HEAD_DIM:(g + 1) * HEAD_DIM, :] = _dot_nt(w2vt_ref[...], act_g).astype(BF16)


def _compress(src, pe, w1, w1bd, w2k, w2vt, caux):
    b, t, _ = src.shape
    ncp = t // CMP_STRIDE
    gd = NSA_KV_HEADS * HEAD_DIM
    full = lambda a: pl.BlockSpec(a.shape, lambda bi, k: (0,) * a.ndim)
    per_kind = lambda a: pl.BlockSpec((1,) + a.shape[1:], lambda bi, k: (k,) + (0,) * (a.ndim - 1))
    return pl.pallas_call(
        _compress_kernel,
        grid=(b, 2),
        in_specs=[pl.BlockSpec((1, t, gd), lambda bi, k: (bi, 0, k)),
                  per_kind(pe), per_kind(w1), per_kind(w1bd), full(w2k), full(w2vt), full(caux)],
        out_specs=[pl.BlockSpec((1, NSA_KV_HEADS, ncp, LANES), lambda bi, k: (bi, 0, 0, 0)),
                   pl.BlockSpec((1, gd, ncp), lambda bi, k: (bi, 0, 0))],
        out_shape=[jax.ShapeDtypeStruct((b, NSA_KV_HEADS, ncp, LANES), BF16),
                   jax.ShapeDtypeStruct((b, gd, ncp), BF16)],
        compiler_params=_params("parallel", "arbitrary"),
        name="compress",
    )(src, pe, w1, w1bd, w2k, w2vt, caux)


SOFTMAX_FLOOR = -1e29
TAKEN = -3e38
ATTN_BATCH = 8
BLOCKS_PER_CHUNK = KEY_CHUNK // SLC_BLOCK
AUX_COLS = 16
AUX_SLOPE = BLOCKS_PER_CHUNK
ONES_ROWS = 16
CMP_ROWS_STEP = 256
CMP_AUX_SPLIT = 128


def _slope(g, r):
    if isinstance(g, int):
        return jnp.float32(ALIBI_SLOPES[g * NSA_GROUP + r])
    s = jnp.float32(ALIBI_SLOPES[r])
    for gi in range(1, NSA_KV_HEADS):
        s = jnp.where(g == gi, jnp.float32(ALIBI_SLOPES[gi * NSA_GROUP + r]), s)
    return s


def _nsa_cmp_kernel(qt_ref, kc_ref, vct_ref, poolt_ref, pair_ref, bits_ref, oct_ref, selt_ref, words_ref,
                    *, n_sel):
    i = pl.program_id(1)
    qt = selt_ref.shape[4]
    ncp = kc_ref.shape[2]
    n_slc = poolt_ref.shape[0]
    gq = qt_ref.shape[4]
    start = i * qt
    t = start + _iota((1, qt), 1)
    aux_row = _iota((AUX_COLS, gq), 0)
    col_head = _iota((1, gq), 1) // qt
    pad_rows = jnp.zeros((kc_ref.shape[3] - qt_ref.shape[3] - AUX_COLS, gq), BF16)

    def weights(g):
        slope_cols = jnp.zeros((1, gq), F32)
        for r in range(NSA_GROUP):
            slope_cols = jnp.where(col_head == r, _slope(g, r), slope_cols)
        aux = jnp.where(aux_row == 0, slope_cols * (CMP_STRIDE * CMP_AUX_SPLIT),
                        jnp.where(aux_row == 1, slope_cols * CMP_STRIDE, 0.0))
        return jnp.concatenate([qt_ref[0, g, 0], aux.astype(BF16), pad_rows], axis=0)

    def importance(g, nr, nb):
        s = _dot(kc_ref[0, g, :nr, :], weights(g))
        edge = min(nr, 2 * CMP_ROWS_STEP)
        cmp_end = ((nr - edge) + _iota((edge, 1), 0)) * CMP_STRIDE + (CMP_LEN - 1)
        visible = cmp_end <= t
        vct = vct_ref[0, g * HEAD_DIM:(g + 1) * HEAD_DIM, :nr]
        psum = jnp.zeros((nr, qt), F32)
        for r in range(NSA_GROUP):
            cols = slice(r * qt, (r + 1) * qt)
            sr = s[:, cols]
            tail = jnp.where(visible, sr[nr - edge:], NEG_INF)
            sr = tail if edge == nr else jnp.concatenate([sr[:nr - edge], tail], axis=0)
            m = jnp.maximum(jnp.max(sr, axis=0, keepdims=True), SOFTMAX_FLOOR)
            e = jnp.exp(sr - m)
            l = jnp.sum(e, axis=0, keepdims=True)
            inv = jnp.where(l > 0.0, 1.0 / l, 0.0)
            oct_ref[0, g, 0, :, cols] = (_dot(vct, e.astype(BF16)) * inv).astype(BF16)
            psum = psum + e * inv
        return _dot(poolt_ref[:nb, :nr], psum.astype(BF16))

    def visible_prefix(nr):
        nb = min(n_slc, nr * CMP_STRIDE // SLC_BLOCK)
        imp = jnp.concatenate([importance(g, nr, nb) for g in range(NSA_KV_HEADS)], axis=1)
        blk = _iota((nb, 1), 0)
        cur = jnp.concatenate([t // SLC_BLOCK] * NSA_KV_HEADS, axis=1)
        forced = (blk == 0) | (blk == cur) | (blk == cur - 1)
        score = jnp.where(forced, TAKEN, jnp.where(blk <= cur, imp, NEG_INF))
        n_forced = 1 + jnp.where(cur >= 1, 1, 0) + jnp.where(cur >= 2, 1, 0)
        blk_f = blk.astype(F32)

        def take_one(score, active):
            m = jnp.max(score, axis=0, keepdims=True)
            first = jnp.min(jnp.where(score == m, blk_f, F32(1e9)), axis=0, keepdims=True)
            hit = (blk_f == first) if active is None else ((blk_f == first) & active)
            return jnp.where(hit, TAKEN, score)

        common_rounds = max(n_sel - 3, 0)
        for _ in range(common_rounds):
            score = take_one(score, None)

        def early_rounds(score):
            for k in range(common_rounds, n_sel - 1):
                score = take_one(score, n_sel - n_forced > k)
            return score

        score = lax.cond(start < 2 * SLC_BLOCK, early_rounds, lambda sc: sc, score)
        for g in range(NSA_KV_HEADS):
            sel_g = score[:, g * qt:(g + 1) * qt] == TAKEN
            selt_ref[0, g, 0, :nb, :] = jnp.where(sel_g, 0.0, NEG_INF)
            if nb < n_slc:
                selt_ref[0, g, 0, nb:, :] = jnp.full((n_slc - nb, qt), NEG_INF, F32)
            count = _dot_nt(jnp.ones((8, qt), BF16), jnp.where(sel_g, 1.0, 0.0).astype(BF16))
            used = jnp.where(count > 0.0, 1.0, 0.0).astype(BF16)
            chunk_used = jnp.where(_dot(used, pair_ref[:nb, :]) > 0.0, 1.0, 0.0).astype(BF16)
            words_ref[0, g, 0] = _dot(chunk_used, bits_ref[...]).astype(jnp.int32)

    step = min(CMP_ROWS_STEP, ncp)
    rows_needed = jnp.minimum((start + qt - CMP_LEN) // CMP_STRIDE + 1, ncp)
    n_steps = (rows_needed + step - 1) // step
    for k in range(ncp // step):
        pl.when(n_steps == k + 1)(functools.partial(visible_prefix, (k + 1) * step))


def _nsa_cmp(q_t, kc, vct, poolt_m, pair_m, bits_m, n_sel):
    b, ng, n_tiles, qrows, qcols = q_t.shape
    ncp = kc.shape[2]
    n_slc = poolt_m.shape[0]
    const = lambda shape: pl.BlockSpec(shape, lambda bi, i: (0,) * len(shape))
    tile5 = lambda rows, cols: pl.BlockSpec((1, ng, 1, rows, cols), lambda bi, i: (bi, 0, i, 0, 0))
    return pl.pallas_call(
        functools.partial(_nsa_cmp_kernel, n_sel=n_sel),
        grid=(b, n_tiles),
        in_specs=[
            tile5(qrows, qcols),
            pl.BlockSpec((1, ng, ncp, kc.shape[3]), lambda bi, i: (bi, 0, 0, 0)),
            pl.BlockSpec((1, ng * HEAD_DIM, ncp), lambda bi, i: (bi, 0, 0)),
            const(poolt_m.shape), const(pair_m.shape), const(bits_m.shape),
        ],
        out_specs=[tile5(HEAD_DIM, qcols), tile5(n_slc, Q_TILE), tile5(8, LANES)],
        out_shape=[
            jax.ShapeDtypeStruct((b, ng, n_tiles, HEAD_DIM, qcols), BF16),
            jax.ShapeDtypeStruct((b, ng, n_tiles, n_slc, Q_TILE), F32),
            jax.ShapeDtypeStruct((b, ng, n_tiles, 8, LANES), jnp.int32),
        ],
        compiler_params=_params("parallel", "parallel"),
        name="nsa_compressed",
    )(q_t, kc, vct, poolt_m, pair_m, bits_m)


def _nsa_slc_kernel(words_ref, qt_ref, ks_ref, vst_ref, kw_ref, vwt_ref, selt_ref, oct_ref, gt_ref,
                    out_ref, m_ref, l_ref, acc_ref, ow_ref, list_ref, *, words_per_tile):
    bi = pl.program_id(0)
    g = pl.program_id(1)
    i = pl.program_id(2)
    n_tiles = pl.num_programs(2)
    qt = out_ref.shape[1]
    q_rows = qt_ref[0, 0, 0]
    gq = q_rows.shape[1]
    start = i * qt
    lane_f = _iota((1, qt), 1).astype(F32)
    key_in_chunk = _iota((KEY_CHUNK, qt), 0)
    query_in_tile = _iota((KEY_CHUNK, qt), 1)

    aux_row = _iota((AUX_COLS, gq), 0)
    col_head = _iota((1, gq), 1) // qt
    slope_cols = jnp.zeros((1, gq), F32)
    for r in range(NSA_GROUP):
        slope_cols = jnp.where(col_head == r, _slope(g, r), slope_cols)
    aux_base = jnp.where(aux_row == AUX_SLOPE, slope_cols, 0.0)
    pad_rows = jnp.zeros((ks_ref.shape[3] - q_rows.shape[0] - AUX_COLS, gq), BF16)

    def scores_and_values(slots):
        scores = [_dot(k, jnp.concatenate([q_rows, aux.astype(BF16), pad_rows], axis=0))
                  for k, aux, _, _, _ in slots]
        values = jnp.concatenate([v for _, _, v, _, _ in slots], axis=1)
        return scores, jnp.concatenate([values, jnp.ones((ONES_ROWS, values.shape[1]), BF16)], axis=0)

    def softmax_step(slots, scores, v_cat, r, m_old):
        cols = slice(r * qt, (r + 1) * qt)
        srs, tops = [], []
        for (_, _, _, shift, mask), s in zip(slots, scores):
            sr = s[:, cols] if mask is None else jnp.where(mask, s[:, cols], NEG_INF)
            srs.append(sr)
            tops.append(jnp.max(sr, axis=0, keepdims=True) + shift[r])
        m_new = functools.reduce(jnp.maximum, tops, m_old)
        ps = [jnp.exp((sr - (m_new - slot[3][r])).astype(BF16)) for slot, sr in zip(slots, srs)]
        weighted = _dot(v_cat, jnp.concatenate(ps, axis=0))
        return m_new, weighted[HEAD_DIM:HEAD_DIM + 1], weighted[:HEAD_DIM]

    def shifts(dist0, ok):
        rows = [-_slope(g, r) * (dist0 + lane_f) for r in range(NSA_GROUP)]
        return rows if ok is None else [jnp.where(ok, row, NEG_INF) for row in rows]

    word_base = ((bi * NSA_KV_HEADS + g) * n_tiles + i) * words_per_tile
    list_ref[0] = 0

    def scan_word(w, n):
        word = words_ref[word_base + w]

        def scan_bits(n):
            for bit in range(WORD_BITS):
                c = w * WORD_BITS + bit
                list_ref[n] = c
                n = n + jnp.where(c < i, (word >> bit) & 1, 0)
            return n

        return lax.cond(word != 0, scan_bits, lambda n: n, n)

    n_listed = lax.fori_loop(0, (i + WORD_BITS - 1) // WORD_BITS, scan_word, 0)

    def selected_slot(c, ok, mask):
        at = pl.multiple_of(c * KEY_CHUNK, KEY_CHUNK)
        bias = selt_ref[0, 0, 0, pl.ds(c * BLOCKS_PER_CHUNK, BLOCKS_PER_CHUNK), :]
        aux = aux_base
        for blk in range(BLOCKS_PER_CHUNK):
            aux = jnp.where(aux_row == blk, jnp.concatenate([bias[blk:blk + 1]] * NSA_GROUP, axis=1), aux)
        return (ks_ref[0, 0, pl.ds(at, KEY_CHUNK), :], aux, vst_ref[0, 0, c],
                shifts((start - c * KEY_CHUNK).astype(F32), ok), mask)

    def listed_slot(idx):
        ok = idx < n_listed
        c = jnp.where(ok, list_ref[jnp.minimum(idx, jnp.maximum(n_listed - 1, 0))], 0)
        return selected_slot(c, ok, None)

    floor = jnp.full((1, qt), SOFTMAX_FLOOR, F32)

    n_back = WINDOW // KEY_CHUNK
    slots = []
    for j in range(n_back + 1):
        cs = start - WINDOW + j * KEY_CHUNK
        chunk = jnp.maximum(cs, 0) // KEY_CHUNK
        at = pl.multiple_of(chunk * KEY_CHUNK, KEY_CHUNK)
        mask = (query_in_tile < key_in_chunk) if j == 0 else (
            (key_in_chunk <= query_in_tile) if j == n_back else None)
        slots.append((kw_ref[0, 0, pl.ds(at, KEY_CHUNK), :], aux_base, vwt_ref[0, 0, chunk],
                      shifts(F32(WINDOW - j * KEY_CHUNK), cs >= 0), mask))
    scores, v_cat = scores_and_values(slots)
    for r in range(NSA_GROUP):
        _, total, weighted = softmax_step(slots, scores, v_cat, r, floor)
        ow_ref[:, r * qt:(r + 1) * qt] = weighted * jnp.where(total > 0.0, 1.0 / total, 0.0)

    slots = [selected_slot(i, None, key_in_chunk <= query_in_tile)] + [
        listed_slot(j) for j in range(ATTN_BATCH - 1)]
    scores, v_cat = scores_and_values(slots)
    for r in range(NSA_GROUP):
        m_ref[r], l_ref[r], acc_ref[:, r * qt:(r + 1) * qt] = softmax_step(slots, scores, v_cat, r, floor)

    def batch(it, carry):
        first = ATTN_BATCH - 1 + it * ATTN_BATCH
        slots = [listed_slot(first + j) for j in range(ATTN_BATCH)]
        scores, v_cat = scores_and_values(slots)
        for r in range(NSA_GROUP):
            cols = slice(r * qt, (r + 1) * qt)
            m_old = m_ref[r]
            m_new, total, weighted = softmax_step(slots, scores, v_cat, r, m_old)
            alpha = jnp.exp(m_old - m_new)
            l_ref[r] = alpha * l_ref[r] + total
            acc_ref[:, cols] = alpha * acc_ref[:, cols] + weighted
            m_ref[r] = m_new
        return carry

    n_rest = jnp.maximum(n_listed - (ATTN_BATCH - 1), 0)
    lax.fori_loop(0, (n_rest + ATTN_BATCH - 1) // ATTN_BATCH, batch, 0)

    def finalize(r):
        l = l_ref[r]
        return acc_ref[:, r * qt:(r + 1) * qt] * jnp.where(l > 0.0, 1.0 / l, 0.0)

    outs = []
    for r in range(NSA_GROUP):
        cols = slice(r * qt, (r + 1) * qt)
        col = (g * NSA_GROUP + r) * NSA_BRANCHES
        gate = lambda br: jax.nn.sigmoid(gt_ref[0, pl.ds(col + br, 1), :])
        outs.append(gate(0) * oct_ref[0, 0, 0, :, cols].astype(F32) + gate(1) * finalize(r)
                    + gate(2) * ow_ref[:, cols])
    out_ref[0] = jnp.concatenate(outs, axis=0).T.astype(out_ref.dtype)


def _nsa_slc(words, q_t, ks, vst, kw, vwt, selt, oct, gates_t, words_per_tile):
    b, _, t, kw_cols = ks.shape
    _, _, n_tiles, qrows, qcols = q_t.shape
    n_slc = selt.shape[3]
    n_chunks = vst.shape[2]
    gw = NSA_GROUP * HEAD_DIM
    once = dict(pipeline_mode=pl.Buffered(1))
    k_spec = pl.BlockSpec((1, 1, t, kw_cols), lambda bi, g, i, w: (bi, g, 0, 0), **once)
    vt_spec = pl.BlockSpec((1, 1, n_chunks, HEAD_DIM, KEY_CHUNK), lambda bi, g, i, w: (bi, g, 0, 0, 0), **once)
    tile5 = lambda rows, cols: pl.BlockSpec((1, 1, 1, rows, cols), lambda bi, g, i, w: (bi, g, i, 0, 0))
    grid_spec = pltpu.PrefetchScalarGridSpec(
        num_scalar_prefetch=1,
        grid=(b, NSA_KV_HEADS, n_tiles),
        in_specs=[
            tile5(qrows, qcols),
            k_spec, vt_spec, k_spec, vt_spec,
            tile5(n_slc, Q_TILE),
            tile5(HEAD_DIM, qcols),
            pl.BlockSpec((1, GATE_PAD, Q_TILE), lambda bi, g, i, w: (bi, 0, i)),
        ],
        out_specs=pl.BlockSpec((1, Q_TILE, gw), lambda bi, g, i, w: (bi, i, g)),
        scratch_shapes=[pltpu.VMEM((NSA_GROUP, 1, Q_TILE), F32), pltpu.VMEM((NSA_GROUP, 1, Q_TILE), F32),
                        pltpu.VMEM((HEAD_DIM, qcols), F32), pltpu.VMEM((HEAD_DIM, qcols), F32),
                        pltpu.SMEM((n_chunks,), jnp.int32)],
    )
    return pl.pallas_call(
        functools.partial(_nsa_slc_kernel, words_per_tile=words_per_tile),
        grid_spec=grid_spec,
        out_shape=jax.ShapeDtypeStruct((b, t, NSA_HEADS * HEAD_DIM), BF16),
        compiler_params=_params("parallel", "parallel", "parallel"),
        name="nsa_selected_window",
    )(words, q_t, ks, vst, kw, vwt, selt, oct, gates_t)


POOL_HALO = 16
CONV_HALO = 8


def _merge_kernel(h_ref, g_ref, nsa_ref, pool_ref, pool_halo_ref, conv_ref, conv_halo_ref,
                  wmg_ref, wnsa_ref, pool_bd_ref, pool_scale_ref, wpool_ref, convw_ref, wconv_ref, wo_ref,
                  out_ref, pool_ext, conv_ext, *, seq_len):
    i = pl.program_id(0)
    tm, d = h_ref.shape
    cw = pool_ref.shape[1]
    pos0 = (i * tm) % seq_len
    keep_halo = jnp.where(pos0 == 0, 0.0, 1.0)
    pos = pos0 + _iota((tm, 1), 0)

    u = pool_ref[...]
    pool_ext[0:POOL_HALO, :] = pool_halo_ref[...] * keep_halo
    pool_ext[POOL_HALO:, :] = u
    lane_group = _iota((1, cw), 1) // (cw // len(POOL_WINDOWS))
    total = u
    mean = jnp.zeros_like(u)
    done = 1
    for gi, win in enumerate(POOL_WINDOWS):
        for k in range(done, win):
            total = total + pool_ext[POOL_HALO - k:POOL_HALO - k + tm, :]
        done = win
        cnt = jnp.minimum(pos + 1, win).astype(F32)
        mean = jnp.where(lane_group == gi, total / cnt, mean)
    pooled = (mean - u).astype(BF16)
    mixed = _dot(pooled, pool_bd_ref[...]) * pool_scale_ref[...]
    y_pool = _dot(mixed.astype(BF16), wpool_ref[...])

    ch = conv_ref[:, 0:cw]
    cb = conv_ref[:, cw:2 * cw]
    cc = conv_ref[:, 2 * cw:3 * cw]
    conv_ext[0:CONV_HALO, :] = conv_halo_ref[:, 0:cw] * conv_halo_ref[:, 2 * cw:3 * cw] * keep_halo
    conv_ext[CONV_HALO:, :] = cc * ch
    y = jnp.zeros((tm, cw), F32)
    for k in range(CONV_K):
        off = CONV_HALO - (CONV_K - 1) + k
        y = y + convw_ref[k:k + 1, :] * conv_ext[off:off + tm, :]
    y_conv = _dot((cb * y).astype(BF16), wconv_ref[...])

    y_nsa = _dot(nsa_ref[...], wnsa_ref[...])

    h = h_ref[...]
    xn = _rms_norm(h, g_ref[...]).astype(BF16)
    merged = jnp.zeros((tm, d), F32)
    for br, y_br in enumerate((y_nsa, y_pool, y_conv)):
        mg = jax.nn.sigmoid(_dot(xn, wmg_ref[:, br * d:(br + 1) * d]))
        merged = merged + mg * y_br
    out_ref[...] = h + _dot(merged.astype(BF16), wo_ref[...])


def _merge(h, g, nsa, pool_u, conv, wmg, wnsa, pool_bd, pool_scale, wpool, convw, wconv, wo, seq_len):
    n, d = h.shape
    tm = TOKEN_TILE
    cw = pool_u.shape[1]
    row = lambda width: pl.BlockSpec((tm, width), lambda i: (i, 0))
    full = lambda a: pl.BlockSpec(a.shape, lambda i: (0,) * a.ndim)
    halo = lambda rows, width: pl.BlockSpec(
        (rows, width), lambda i: (jnp.maximum(i * (tm // rows) - 1, 0), 0))
    return pl.pallas_call(
        functools.partial(_merge_kernel, seq_len=seq_len),
        grid=(n // tm,),
        in_specs=[row(d), full(g), row(nsa.shape[1]), row(cw), halo(POOL_HALO, cw),
                  row(conv.shape[1]), halo(CONV_HALO, conv.shape[1]),
                  full(wmg), full(wnsa), full(pool_bd), full(pool_scale), full(wpool), full(convw),
                  full(wconv), full(wo)],
        out_specs=row(d),
        out_shape=jax.ShapeDtypeStruct((n, d), F32),
        scratch_shapes=[pltpu.VMEM((tm + POOL_HALO, cw), F32), pltpu.VMEM((tm + CONV_HALO, cw), F32)],
        compiler_params=_params("parallel"),
        name="merge",
    )(h, g, nsa, pool_u, pool_u, conv, conv, wmg, wnsa, pool_bd, pool_scale, wpool, convw, wconv, wo)


def _route(logits):
    lane = _iota(logits.shape, 1)
    lane_f = lane.astype(F32)
    big = F32(1e9)
    is_group = lane < N_EXPERT_GROUPS
    gl = jnp.where(is_group, logits, NEG_INF)
    g_max = jnp.max(gl, axis=1, keepdims=True)
    g_sel = jnp.min(jnp.where(gl == g_max, lane_f, big), axis=1, keepdims=True)
    g_prob = 1.0 / jnp.sum(jnp.where(is_group, jnp.exp(gl - g_max), 0.0), axis=1, keepdims=True)
    lo = N_EXPERT_GROUPS + EXPERTS_PER_GROUP * g_sel
    in_group = (lane_f >= lo) & (lane_f < lo + EXPERTS_PER_GROUP)
    el = jnp.where(in_group, logits, NEG_INF)
    v1 = jnp.max(el, axis=1, keepdims=True)
    i1 = jnp.min(jnp.where((el == v1) & in_group, lane_f, big), axis=1, keepdims=True)
    el2 = jnp.where(lane_f == i1, NEG_INF, el)
    rest = in_group & (lane_f != i1)
    v2 = jnp.max(el2, axis=1, keepdims=True)
    i2 = jnp.min(jnp.where((el2 == v2) & rest, lane_f, big), axis=1, keepdims=True)
    e2 = jnp.exp(v2 - v1)
    w1 = g_prob / (1.0 + e2)
    w2 = g_prob * e2 / (1.0 + e2)
    return jnp.where(lane_f == i1, w1, 0.0) + jnp.where(lane_f == i2, w2, 0.0), g_sel


GROUP_LANE = N_EXPERT_GROUPS + N_EXPERTS


RANK_LANE = GROUP_LANE + 1


def _router_kernel(h_ref, g_ref, wr_ref, br_ref, before_ref, xn_ref, comb_ref, count_ref):
    @pl.when(pl.program_id(0) == 0)
    def _():
        count_ref[...] = jnp.zeros(count_ref.shape, F32)

    xn = _rms_norm(h_ref[...], g_ref[...])
    xn_hi = xn.astype(BF16)
    xn_lo = (xn - xn_hi.astype(F32)).astype(BF16)
    logits = (_dot(xn_hi, wr_ref[0]) + (_dot(xn_hi, wr_ref[1]) + _dot(xn_lo, wr_ref[0]))) + br_ref[...]
    comb, g_sel = _route(logits)
    lane = _iota(comb.shape, 1)
    chose = jnp.where(lane.astype(F32) == g_sel, 1.0, 0.0)
    earlier = _dot(before_ref[...], chose.astype(BF16)) + count_ref[0:1, :]
    rank = jnp.sum(chose * earlier, axis=1, keepdims=True)
    count_ref[0:1, :] = count_ref[0:1, :] + jnp.sum(chose, axis=0, keepdims=True)
    comb_ref[...] = jnp.where(lane == GROUP_LANE, g_sel, jnp.where(lane == RANK_LANE, rank, comb))
    first = N_EXPERT_GROUPS + EXPERTS_PER_GROUP * g_sel
    local = jnp.zeros(comb.shape, F32)
    for e in range(EXPERTS_PER_GROUP):
        c_e = jnp.sum(jnp.where(lane.astype(F32) == first + e, comb, 0.0), axis=1, keepdims=True)
        local = jnp.where((lane == e) | (lane == EXPERTS_PER_GROUP + e), c_e, local)
    local_hi = local.astype(BF16)
    d = xn_hi.shape[1]
    xn_ref[:, :d] = xn_hi
    xn_ref[:, d:] = jnp.where(lane < EXPERTS_PER_GROUP, local_hi, (local - local_hi.astype(F32)).astype(BF16))


def _router(h, g, wr, br, before):
    n, d = h.shape
    tm = MOE_TILE
    full = lambda a: pl.BlockSpec(a.shape, lambda i: (0,) * a.ndim)
    return pl.pallas_call(
        _router_kernel,
        grid=(n // tm,),
        in_specs=[pl.BlockSpec((tm, d), lambda i: (i, 0)), full(g), full(wr), full(br), full(before)],
        out_specs=[pl.BlockSpec((tm, d + ROUTER_PAD), lambda i: (i, 0)),
                   pl.BlockSpec((tm, ROUTER_PAD), lambda i: (i, 0)),
                   pl.BlockSpec((8, ROUTER_PAD), lambda i: (0, 0))],
        out_shape=[jax.ShapeDtypeStruct((n, d + ROUTER_PAD), BF16), jax.ShapeDtypeStruct((n, ROUTER_PAD), F32),
                   jax.ShapeDtypeStruct((8, ROUTER_PAD), F32)],
        compiler_params=_params("arbitrary"),
        name="router",
    )(h, g, wr, br, before)


def _experts_kernel(tile_group_ref, n_active_ref, x_ref, w1_ref, w3_ref, w2_ref, out_ref, acc_ref):
    i = pl.program_id(0)

    @pl.when(i < n_active_ref[0])
    def _():
        d = out_ref.shape[1]
        x = x_ref[:, :d]
        comb = x_ref[:, d:].astype(F32)
        lane = _iota(comb.shape, 1)
        for e in range(EXPERTS_PER_GROUP):
            c_e = jnp.sum(jnp.where((lane == e) | (lane == EXPERTS_PER_GROUP + e), comb, 0.0),
                          axis=1, keepdims=True)
            a = (jax.nn.silu(_dot(x, w1_ref[0, 0, e].astype(BF16)))
                 * _dot(x, w3_ref[0, 0, e].astype(BF16))) * c_e
            y = _dot(a.astype(BF16), w2_ref[0, 0, e].astype(BF16))
            if e == 0:
                acc_ref[...] = y
            else:
                acc_ref[...] += y
        out_ref[...] = acc_ref[...].astype(out_ref.dtype)


def _experts(tile_group, n_active, x_sorted, w1, w3, w2, layer):
    ns = x_sorted.shape[0]
    d = w1.shape[3]
    tm = MOE_SORT_TILE
    group_w = lambda w: pl.BlockSpec((1, 1) + w.shape[2:], lambda i, tg, na: (layer, tg[i], 0, 0, 0),
                                     pipeline_mode=pl.Buffered(1))
    grid_spec = pltpu.PrefetchScalarGridSpec(
        num_scalar_prefetch=2,
        grid=(ns // tm,),
        in_specs=[
            pl.BlockSpec((tm, x_sorted.shape[1]), lambda i, tg, na: (i, 0)),
            group_w(w1), group_w(w3), group_w(w2),
        ],
        out_specs=pl.BlockSpec((tm, d), lambda i, tg, na: (i, 0)),
        scratch_shapes=[pltpu.VMEM((tm, d), F32)],
    )
    return pl.pallas_call(
        _experts_kernel,
        grid_spec=grid_spec,
        out_shape=jax.ShapeDtypeStruct((ns, d), BF16),
        compiler_params=_params("arbitrary"),
        name="experts",
    )(tile_group, n_active, x_sorted, w1, w3, w2)


def _residual_kernel(h_ref, y_ref, gf_ref, out_ref, *, final_norm):
    out = h_ref[...] + y_ref[...].astype(F32)
    out_ref[...] = _rms_norm(out, gf_ref[...]) if final_norm else out


def _residual(h, y, gf, final_norm):
    n, d = h.shape
    tm = MOE_TILE
    row = pl.BlockSpec((tm, d), lambda i: (i, 0))
    return pl.pallas_call(
        functools.partial(_residual_kernel, final_norm=final_norm),
        grid=(n // tm,),
        in_specs=[row, row, pl.BlockSpec(gf.shape, lambda i: (0, 0))],
        out_specs=row,
        out_shape=jax.ShapeDtypeStruct((n, d), F32),
        compiler_params=_params("parallel"),
        name="residual",
    )(h, y, gf)


def _group_sort_plan(group_id, rank, counts, tile):
    n = group_id.shape[0]
    n_slots = n + N_EXPERT_GROUPS * tile
    padded = (counts + tile - 1) // tile * tile
    ends = jnp.cumsum(padded)
    slot = (ends - padded)[group_id] + rank
    source = jnp.zeros((n_slots,), jnp.int32).at[slot].set(jnp.arange(n, dtype=jnp.int32))
    tile_start = jnp.arange(n_slots // tile, dtype=jnp.int32) * tile
    tile_group = jnp.minimum(jnp.searchsorted(ends, tile_start, side="right"), N_EXPERT_GROUPS - 1)
    return slot, source, tile_group.astype(jnp.int32), (ends[-1:] // tile).astype(jnp.int32)


def _moe(h, g, wr, br, before, w1, w3, w2, layer, gf, final_norm):
    xn, comb, counts = _router(h, g, wr, br, before)
    as_int = lambda a: a.astype(jnp.int32)
    slot, source, tile_group, n_active = _group_sort_plan(
        as_int(comb[:, GROUP_LANE]), as_int(comb[:, RANK_LANE]), as_int(counts[0, :N_EXPERT_GROUPS]),
        MOE_SORT_TILE)
    grouped = lambda w: w.reshape((w.shape[0], N_EXPERT_GROUPS, EXPERTS_PER_GROUP) + w.shape[2:])
    rows = lambda a, idx: jnp.take(a, idx, axis=0, mode="clip")
    y_sorted = _experts(tile_group, n_active, rows(xn, source), grouped(w1), grouped(w3), grouped(w2), layer)
    return _residual(h, rows(y_sorted, slot), gf, final_norm)


def _selection_constants(seq_len):
    ncp = seq_len // CMP_STRIDE
    n_slc = seq_len // SLC_BLOCK
    ratio = SLC_BLOCK // CMP_STRIDE
    lead = CMP_LEN // CMP_STRIDE - 1
    c = np.arange(ncp)[:, None]
    j = np.arange(n_slc)[None, :]
    pool_m = ((c >= ratio * j - lead) & (c < ratio * j + ratio)).astype(np.float32)
    blocks_per_chunk = KEY_CHUNK // SLC_BLOCK
    n_chunks = seq_len // KEY_CHUNK
    pair_m = np.zeros((n_slc, LANES * ((n_chunks + LANES - 1) // LANES)), np.float32)
    pair_m[np.arange(n_slc), np.arange(n_slc) // blocks_per_chunk] = 1.0
    n_words = (n_chunks + WORD_BITS - 1) // WORD_BITS
    bits_m = np.zeros((pair_m.shape[1], LANES), np.float32)
    ch = np.arange(n_chunks)
    bits_m[ch, ch // WORD_BITS] = 2.0 ** (ch % WORD_BITS)
    key_aux = np.zeros((TOKEN_TILE, LANES), np.float32)
    in_chunk = np.arange(TOKEN_TILE) % KEY_CHUNK
    key_aux[np.arange(TOKEN_TILE), HEAD_DIM + in_chunk // SLC_BLOCK] = 1.0
    key_aux[:, HEAD_DIM + AUX_SLOPE] = in_chunk
    cmp_aux = np.zeros((ncp, LANES), np.float32)
    cmp_aux[:, HEAD_DIM] = np.arange(ncp) // CMP_AUX_SPLIT
    cmp_aux[:, HEAD_DIM + 1] = np.arange(ncp) % CMP_AUX_SPLIT
    as_bf16 = lambda a: jnp.asarray(a, BF16)
    return as_bf16(pool_m.T), as_bf16(pair_m), as_bf16(bits_m), jnp.asarray(key_aux), jnp.asarray(cmp_aux), n_words


def kernel(x, norm1_g, w_in, cmp_pe, cmp_w1, cmp_w2, w_nsa_proj, pool_w, pool_scale, w_pool_proj, conv_w,
           w_conv_proj, w_o, norm2_g, router_group_w, router_group_b, router_expert_w, router_expert_b,
           expert_w1, expert_w3, expert_w2, final_norm_g):
    b, t, d = x.shape
    n = b * t
    depth = w_in.shape[0]
    dq = NSA_HEADS * HEAD_DIM
    dkv = 6 * NSA_KV_HEADS * HEAD_DIM
    dgate = NSA_HEADS * NSA_BRANCHES
    cw = d // 4
    assert t % TOKEN_TILE == 0 and n % MOE_TILE == 0 and t % Q_TILE == 0
    n_slc = t // SLC_BLOCK
    n_sel = min(SLC_TOPN, n_slc)
    n_chunks16 = t // CMP_STRIDE
    kvw = NSA_KV_HEADS * HEAD_DIM
    poolt_m, pair_m, bits_m, key_aux, cmp_aux, n_words = _selection_constants(t)
    assert Q_TILE == KEY_CHUNK and AUX_SLOPE < AUX_COLS and KEY_CHUNK <= 256
    assert n_chunks16 <= 256 * CMP_AUX_SPLIT and n_chunks16 % min(CMP_ROWS_STEP, n_chunks16) == 0
    assert n_words <= LANES and pair_m.shape[1] == LANES

    before = jnp.asarray(np.tril(np.ones((MOE_TILE, MOE_TILE), np.float32), -1), BF16)
    h = x.reshape(n, d)
    for l in range(depth):
        wl = w_in[l]
        o_gate = dq + dkv
        o_pool = o_gate + dgate
        o_merge = o_pool + cw + 3 * cw
        kv_cols = lambda kind: wl[:, dq + kind * kvw:dq + (kind + 1) * kvw]
        wq_t = (wl[:, :dq] * (HEAD_DIM ** -0.5)).T.astype(BF16)
        wv_t = jnp.concatenate([kv_cols(3), kv_cols(5)], axis=1).T.astype(BF16)
        wg_t = jnp.pad(wl[:, o_gate:o_pool], ((0, 0), (0, GATE_PAD - dgate))).T.astype(BF16)
        no_aux = jnp.zeros((d, LANES - HEAD_DIM), F32)
        wk = jnp.concatenate([piece for kind in (2, 4) for gi in range(NSA_KV_HEADS)
                              for piece in (kv_cols(kind)[:, gi * HEAD_DIM:(gi + 1) * HEAD_DIM], no_aux)],
                             axis=1).astype(BF16)
        wn = jnp.concatenate([kv_cols(0), kv_cols(1), wl[:, o_pool:o_merge]], axis=1).astype(BF16)
        wmg = wl[:, o_merge:].astype(BF16)
        pool_bd = jax.scipy.linalg.block_diag(*[pool_w[l, gi] for gi in range(pool_w.shape[1])]).astype(BF16)
        convw = jnp.pad(conv_w[l], ((0, 8 - CONV_K), (0, 0)))
        wr = jnp.pad(jnp.concatenate([router_group_w[l], router_expert_w[l]], axis=1),
                     ((0, 0), (0, ROUTER_PAD - N_EXPERT_GROUPS - N_EXPERTS)))
        wr_hi = wr.astype(BF16)
        wr = jnp.stack([wr_hi, (wr - wr_hi.astype(F32)).astype(BF16)])
        br = jnp.pad(jnp.concatenate([router_group_b[l], router_expert_b[l]]),
                     (0, ROUTER_PAD - N_EXPERT_GROUPS - N_EXPERTS))[None, :]
        pe = jnp.broadcast_to(cmp_pe[l].reshape(2, 1, CMP_LEN * HEAD_DIM), (2, 8, CMP_LEN * HEAD_DIM)).astype(BF16)
        halves = CMP_LEN // CMP_STRIDE
        w1_bd = jnp.einsum("khldc,gq->khlgdqc",
                           cmp_w1[l].reshape(2, halves, CMP_STRIDE, HEAD_DIM, CMP_HIDDEN),
                           jnp.eye(NSA_KV_HEADS, dtype=F32))
        w1_bd = w1_bd.reshape(2, halves, CMP_STRIDE * kvw, NSA_KV_HEADS * CMP_HIDDEN).astype(BF16)
        w2_k = jnp.pad(cmp_w2[l, 0], ((0, 0), (0, LANES - HEAD_DIM))).astype(BF16)
        w2_vt = cmp_w2[l, 1].T.astype(BF16)

        q_t, vst, vwt, gates_t, ks, kw, cmp_src, pool_u, conv = _inproj(
            h, norm1_g[l][None, :], wq_t, wv_t, wg_t, wk, wn, key_aux, b)
        kc_aux, vc_t = _compress(cmp_src, pe, cmp_w1[l].astype(BF16), w1_bd, w2_k, w2_vt, cmp_aux)
        oc_t, sel_t, words = _nsa_cmp(q_t, kc_aux, vc_t, poolt_m, pair_m, bits_m, n_sel)
        words = words[:, :, :, 0, :n_words].reshape(-1)
        nsa = _nsa_slc(words, q_t, ks, vst, kw, vwt, sel_t, oc_t, gates_t, n_words)
        h = _merge(h, norm1_g[l][None, :], nsa.reshape(n, dq), pool_u, conv, wmg,
                   w_nsa_proj[l].astype(BF16), pool_bd, pool_scale[l][None, :], w_pool_proj[l].astype(BF16),
                   convw, w_conv_proj[l].astype(BF16), w_o[l].astype(BF16), t)
        h = _moe(h, norm2_g[l][None, :], wr, br, before, expert_w1, expert_w3, expert_w2, l,
                 final_norm_g[None, :], final_norm=(l == depth - 1))
    return h.reshape(b, t, d)
```

```python
import functools

import jax
import jax.numpy as jnp
import numpy as np
from jax import lax
from jax.experimental import pallas as pl
from jax.experimental.pallas import tpu as pltpu

F32 = jnp.float32
BF16 = jnp.bfloat16

HEAD_DIM = 64
NSA_HEADS = 8
NSA_KV_HEADS = 2
NSA_GROUP = NSA_HEADS // NSA_KV_HEADS
CMP_LEN = 32
CMP_STRIDE = 16
CMP_HIDDEN = 4 * HEAD_DIM
SLC_BLOCK = 64
SLC_TOPN = 16
WINDOW = 512
NSA_BRANCHES = 3
POOL_WINDOWS = (2, 4, 8, 16)
CONV_K = 3
N_BRANCHES = 3
N_EXPERT_GROUPS = 4
EXPERTS_PER_GROUP = 8
N_EXPERTS = N_EXPERT_GROUPS * EXPERTS_PER_GROUP
RMS_EPS = 1e-6
NEG_INF = -1e30
FORCE_SCORE = 1e30
ALIBI_SLOPES = tuple(float(2.0 ** (-8.0 * (h + 1) / NSA_HEADS)) for h in range(NSA_HEADS))

LANES = 128
VMEM_LIMIT = 56 * 1024 * 1024
TOKEN_TILE = 512
MOE_TILE = 1024
MOE_SORT_TILE = 512
Q_TILE = 128
KEY_CHUNK = 128
GATE_PAD = LANES
ROUTER_PAD = LANES
WORD_BITS = 16


def _params(*semantics):
    return pltpu.CompilerParams(dimension_semantics=semantics, vmem_limit_bytes=VMEM_LIMIT)


def _dot(a, b):
    return jnp.dot(a, b, preferred_element_type=F32)


def _dot_nt(a, b):
    return lax.dot_general(a, b, (((1,), (1,)), ((), ())), preferred_element_type=F32)


def _rms_norm(x, g):
    y = x * lax.rsqrt(jnp.mean(x * x, axis=-1, keepdims=True) + RMS_EPS)
    return y * g


def _iota(shape, dim):
    return lax.broadcasted_iota(jnp.int32, shape, dim)


def _inproj_kernel(x_ref, g_ref, wq_ref, wv_ref, wg_ref, wk_ref, wn_ref, kaux_ref,
                   q_ref, vs_ref, vw_ref, gate_ref, ks_ref, kw_ref, cmp_ref, pool_ref, conv_ref):
    xn = _rms_norm(x_ref[...], g_ref[...]).astype(BF16)
    sub_tiles = x_ref.shape[0] // Q_TILE
    q_t = _dot_nt(wq_ref[...], xn)
    for g in range(NSA_KV_HEADS):
        for j in range(sub_tiles):
            for r in range(NSA_GROUP):
                head = g * NSA_GROUP + r
                q_ref[0, g, j, :, r * Q_TILE:(r + 1) * Q_TILE] = q_t[
                    head * HEAD_DIM:(head + 1) * HEAD_DIM, j * Q_TILE:(j + 1) * Q_TILE].astype(BF16)
    v_t = _dot_nt(wv_ref[...], xn)
    k = _dot(xn, wk_ref[...])
    for branch, (v_ref, k_ref) in enumerate(((vs_ref, ks_ref), (vw_ref, kw_ref))):
        for g in range(NSA_KV_HEADS):
            slab = branch * NSA_KV_HEADS + g
            for j in range(sub_tiles):
                v_ref[0, g, j] = v_t[slab * HEAD_DIM:(slab + 1) * HEAD_DIM,
                                     j * KEY_CHUNK:(j + 1) * KEY_CHUNK].astype(BF16)
            k_ref[0, g] = (k[:, slab * LANES:(slab + 1) * LANES] + kaux_ref[...]).astype(BF16)
    gate_ref[0] = _dot_nt(wg_ref[...], xn)
    col = 0
    for ref in (cmp_ref, pool_ref, conv_ref):
        width = ref.shape[-1]
        ref[...] = _dot(xn, wn_ref[:, col:col + width]).reshape(ref.shape)
        col += width


def _inproj(h, g, wq_t, wv_t, wg_t, wk, wn, kaux, batch):
    n, d = h.shape
    t = n // batch
    tm = TOKEN_TILE
    steps = t // tm
    sub = tm // Q_TILE
    cw = d // 4
    gq = NSA_GROUP * Q_TILE
    full = lambda a: pl.BlockSpec(a.shape, lambda i: (0,) * a.ndim)
    row = lambda width: pl.BlockSpec((tm, width), lambda i: (i, 0))
    tiles = lambda rows, cols: pl.BlockSpec((1, NSA_KV_HEADS, sub, rows, cols),
                                            lambda i: (i // steps, 0, i % steps, 0, 0))
    keys = pl.BlockSpec((1, NSA_KV_HEADS, tm, LANES), lambda i: (i // steps, 0, i % steps, 0))
    sds = jax.ShapeDtypeStruct
    v_shape = sds((batch, NSA_KV_HEADS, t // KEY_CHUNK, HEAD_DIM, KEY_CHUNK), BF16)
    k_shape = sds((batch, NSA_KV_HEADS, t, LANES), BF16)
    return pl.pallas_call(
        _inproj_kernel,
        grid=(n // tm,),
        in_specs=[row(d), full(g), full(wq_t), full(wv_t), full(wg_t), full(wk), full(wn), full(kaux)],
        out_specs=[tiles(HEAD_DIM, gq), tiles(HEAD_DIM, KEY_CHUNK), tiles(HEAD_DIM, KEY_CHUNK),
                   pl.BlockSpec((1, GATE_PAD, tm), lambda i: (i // steps, 0, i % steps)),
                   keys, keys,
                   pl.BlockSpec((1, tm, 2 * NSA_KV_HEADS * HEAD_DIM), lambda i: (i // steps, i % steps, 0)),
                   row(cw), row(3 * cw)],
        out_shape=[sds((batch, NSA_KV_HEADS, t // Q_TILE, HEAD_DIM, gq), BF16), v_shape, v_shape,
                   sds((batch, GATE_PAD, t), F32), k_shape, k_shape,
                   sds((batch, t, 2 * NSA_KV_HEADS * HEAD_DIM), F32), sds((n, cw), F32), sds((n, 3 * cw), F32)],
        compiler_params=_params("parallel"),
        name="inproj",
    )(h, g, wq_t, wv_t, wg_t, wk, wn, kaux)


def _gelu_tanh(x):
    return 0.5 * x * (1.0 + jnp.tanh(0.7978845608028654 * (x + 0.044715 * x * x * x)))


def _compress_kernel(src_ref, pe_ref, w1_ref, w1bd_ref, w2k_ref, w2vt_ref, caux_ref, kc_ref, vct_ref):
    kind = pl.program_id(1)
    ncp = kc_ref.shape[2]
    hidden = w1_ref.shape[2]
    pieces = [src_ref[0, pl.ds(l, ncp, stride=CMP_STRIDE), :].astype(BF16) for l in range(CMP_STRIDE)]
    chunk = jnp.concatenate(pieces, axis=1)
    first = _dot(chunk, w1bd_ref[0, 0])
    second = _dot(chunk, w1bd_ref[0, 1])
    bias = _dot(pe_ref[0], w1_ref[0])[0:1, :]
    hid = first + pltpu.roll(second, ncp - 1, 0) + jnp.concatenate([bias] * NSA_KV_HEADS, axis=1)
    row = _iota((ncp, 1), 0)
    act = jnp.where(row < ncp - 1, _gelu_tanh(hid), 0.0).astype(BF16)
    for g in range(NSA_KV_HEADS):
        act_g = act[:, g * hidden:(g + 1) * hidden]

        @pl.when(kind == 0)
        def _():
            kc_ref[0, g] = (_dot(act_g, w2k_ref[...]) + caux_ref[...]).astype(BF16)

        @pl.when(kind == 1)
        def _():
            vct_ref[0, g * HEAD_DIM:(g + 1) * HEAD_DIM, :] = _dot_nt(w2vt_ref[...], act_g).astype(BF16)


def _compress(src, pe, w1, w1bd, w2k, w2vt, caux):
    b, t, _ = src.shape
    ncp = t // CMP_STRIDE
    gd = NSA_KV_HEADS * HEAD_DIM
    full = lambda a: pl.BlockSpec(a.shape, lambda bi, k: (0,) * a.ndim)
    per_kind = lambda a: pl.BlockSpec((1,) + a.shape[1:], lambda bi, k: (k,) + (0,) * (a.ndim - 1))
    return pl.pallas_call(
        _compress_kernel,
        grid=(b, 2),
        in_specs=[pl.BlockSpec((1, t, gd), lambda bi, k: (bi, 0, k)),
                  per_kind(pe), per_kind(w1), per_kind(w1bd), full(w2k), full(w2vt), full(caux)],
        out_specs=[pl.BlockSpec((1, NSA_KV_HEADS, ncp, LANES), lambda bi, k: (bi, 0, 0, 0)),
                   pl.BlockSpec((1, gd, ncp), lambda bi, k: (bi, 0, 0))],
        out_shape=[jax.ShapeDtypeStruct((b, NSA_KV_HEADS, ncp, LANES), BF16),
                   jax.ShapeDtypeStruct((b, gd, ncp), BF16)],
        compiler_params=_params("parallel", "arbitrary"),
        name="compress",
    )(src, pe, w1, w1bd, w2k, w2vt, caux)


SOFTMAX_FLOOR = -1e29
TAKEN = -3e38
ATTN_BATCH = 4
BLOCKS_PER_CHUNK = KEY_CHUNK // SLC_BLOCK
AUX_COLS = 16
AUX_SLOPE = BLOCKS_PER_CHUNK
ONES_ROWS = 16
CMP_ROWS_STEP = 256
CMP_AUX_SPLIT = 128


def _slope(g, r):
    if isinstance(g, int):
        return jnp.float32(ALIBI_SLOPES[g * NSA_GROUP + r])
    s = jnp.float32(ALIBI_SLOPES[r])
    for gi in range(1, NSA_KV_HEADS):
        s = jnp.where(g == gi, jnp.float32(ALIBI_SLOPES[gi * NSA_GROUP + r]), s)
    return s


def _nsa_cmp_kernel(qt_ref, kc_ref, vct_ref, poolt_ref, pair_ref, bits_ref, oct_ref, selt_ref, words_ref,
                    *, n_sel):
    i = pl.program_id(1)
    qt = selt_ref.shape[4]
    ncp = kc_ref.shape[2]
    n_slc = poolt_ref.shape[0]
    gq = qt_ref.shape[4]
    start = i * qt
    t = start + _iota((1, qt), 1)
    aux_row = _iota((AUX_COLS, gq), 0)
    col_head = _iota((1, gq), 1) // qt
    pad_rows = jnp.zeros((kc_ref.shape[3] - qt_ref.shape[3] - AUX_COLS, gq), BF16)

    def weights(g):
        slope_cols = jnp.zeros((1, gq), F32)
        for r in range(NSA_GROUP):
            slope_cols = jnp.where(col_head == r, _slope(g, r), slope_cols)
        aux = jnp.where(aux_row == 0, slope_cols * (CMP_STRIDE * CMP_AUX_SPLIT),
                        jnp.where(aux_row == 1, slope_cols * CMP_STRIDE, 0.0))
        return jnp.concatenate([qt_ref[0, g, 0], aux.astype(BF16), pad_rows], axis=0)

    def importance(g, nr, nb):
        s = _dot(kc_ref[0, g, :nr, :], weights(g))
        edge = min(nr, 2 * CMP_ROWS_STEP)
        cmp_end = ((nr - edge) + _iota((edge, 1), 0)) * CMP_STRIDE + (CMP_LEN - 1)
        visible = cmp_end <= t
        vct = vct_ref[0, g * HEAD_DIM:(g + 1) * HEAD_DIM, :nr]
        psum = jnp.zeros((nr, qt), F32)
        for r in range(NSA_GROUP):
            cols = slice(r * qt, (r + 1) * qt)
            sr = s[:, cols]
            tail = jnp.where(visible, sr[nr - edge:], NEG_INF)
            sr = tail if edge == nr else jnp.concatenate([sr[:nr - edge], tail], axis=0)
            m = jnp.maximum(jnp.max(sr, axis=0, keepdims=True), SOFTMAX_FLOOR)
            e = jnp.exp(sr - m)
            l = jnp.sum(e, axis=0, keepdims=True)
            inv = jnp.where(l > 0.0, 1.0 / l, 0.0)
            oct_ref[0, g, 0, :, cols] = (_dot(vct, e.astype(BF16)) * inv).astype(BF16)
            psum = psum + e * inv
        return _dot(poolt_ref[:nb, :nr], psum.astype(BF16))

    def visible_prefix(nr):
        nb = min(n_slc, nr * CMP_STRIDE // SLC_BLOCK)
        imp = jnp.concatenate([importance(g, nr, nb) for g in range(NSA_KV_HEADS)], axis=1)
        blk = _iota((nb, 1), 0)
        cur = jnp.concatenate([t // SLC_BLOCK] * NSA_KV_HEADS, axis=1)
        forced = (blk == 0) | (blk == cur) | (blk == cur - 1)
        score = jnp.where(forced, TAKEN, jnp.where(blk <= cur, imp, NEG_INF))
        n_forced = 1 + jnp.where(cur >= 1, 1, 0) + jnp.where(cur >= 2, 1, 0)
        blk_f = blk.astype(F32)

        def take_one(score, active):
            m = jnp.max(score, axis=0, keepdims=True)
            first = jnp.min(jnp.where(score == m, blk_f, F32(1e9)), axis=0, keepdims=True)
            hit = (blk_f == first) if active is None else ((blk_f == first) & active)
            return jnp.where(hit, TAKEN, score)

        common_rounds = max(n_sel - 3, 0)
        for _ in range(common_rounds):
            score = take_one(score, None)

        def early_rounds(score):
            for k in range(common_rounds, n_sel - 1):
                score = take_one(score, n_sel - n_forced > k)
            return score

        score = lax.cond(start < 2 * SLC_BLOCK, early_rounds, lambda sc: sc, score)
        for g in range(NSA_KV_HEADS):
            sel_g = score[:, g * qt:(g + 1) * qt] == TAKEN
            selt_ref[0, g, 0, :nb, :] = jnp.where(sel_g, 0.0, NEG_INF)
            if nb < n_slc:
                selt_ref[0, g, 0, nb:, :] = jnp.full((n_slc - nb, qt), NEG_INF, F32)
            count = _dot_nt(jnp.ones((8, qt), BF16), jnp.where(sel_g, 1.0, 0.0).astype(BF16))
            used = jnp.where(count > 0.0, 1.0, 0.0).astype(BF16)
            chunk_used = jnp.where(_dot(used, pair_ref[:nb, :]) > 0.0, 1.0, 0.0).astype(BF16)
            words_ref[0, g, 0] = _dot(chunk_used, bits_ref[...]).astype(jnp.int32)

    step = min(CMP_ROWS_STEP, ncp)
    rows_needed = jnp.minimum((start + qt - CMP_LEN) // CMP_STRIDE + 1, ncp)
    n_steps = (rows_needed + step - 1) // step
    for k in range(ncp // step):
        pl.when(n_steps == k + 1)(functools.partial(visible_prefix, (k + 1) * step))


def _nsa_cmp(q_t, kc, vct, poolt_m, pair_m, bits_m, n_sel):
    b, ng, n_tiles, qrows, qcols = q_t.shape
    ncp = kc.shape[2]
    n_slc = poolt_m.shape[0]
    const = lambda shape: pl.BlockSpec(shape, lambda bi, i: (0,) * len(shape))
    tile5 = lambda rows, cols: pl.BlockSpec((1, ng, 1, rows, cols), lambda bi, i: (bi, 0, i, 0, 0))
    return pl.pallas_call(
        functools.partial(_nsa_cmp_kernel, n_sel=n_sel),
        grid=(b, n_tiles),
        in_specs=[
            tile5(qrows, qcols),
            pl.BlockSpec((1, ng, ncp, kc.shape[3]), lambda bi, i: (bi, 0, 0, 0)),
            pl.BlockSpec((1, ng * HEAD_DIM, ncp), lambda bi, i: (bi, 0, 0)),
            const(poolt_m.shape), const(pair_m.shape), const(bits_m.shape),
        ],
        out_specs=[tile5(HEAD_DIM, qcols), tile5(n_slc, Q_TILE), tile5(8, LANES)],
        out_shape=[
            jax.ShapeDtypeStruct((b, ng, n_tiles, HEAD_DIM, qcols), BF16),
            jax.ShapeDtypeStruct((b, ng, n_tiles, n_slc, Q_TILE), F32),
            jax.ShapeDtypeStruct((b, ng, n_tiles, 8, LANES), jnp.int32),
        ],
        compiler_params=_params("parallel", "parallel"),
        name="nsa_compressed",
    )(q_t, kc, vct, poolt_m, pair_m, bits_m)


def _nsa_slc_kernel(words_ref, qt_ref, ks_ref, vst_ref, kw_ref, vwt_ref, selt_ref, oct_ref, gt_ref,
                    out_ref, m_ref, l_ref, acc_ref, ow_ref, sa_ref, sb_ref, list_ref, *, words_per_tile):
    bi = pl.program_id(0)
    g = pl.program_id(1)
    i = pl.program_id(2)
    n_tiles = pl.num_programs(2)
    qt = out_ref.shape[1]
    q_rows = qt_ref[0, 0, 0]
    gq = q_rows.shape[1]
    start = i * qt
    lane_f = _iota((1, qt), 1).astype(F32)
    key_in_chunk = _iota((KEY_CHUNK, qt), 0)
    query_in_tile = _iota((KEY_CHUNK, qt), 1)

    aux_row = _iota((AUX_COLS, gq), 0)
    col_head = _iota((1, gq), 1) // qt
    slope_cols = jnp.zeros((1, gq), F32)
    for r in range(NSA_GROUP):
        slope_cols = jnp.where(col_head == r, _slope(g, r), slope_cols)
    aux_base = jnp.where(aux_row == AUX_SLOPE, slope_cols, 0.0)
    pad_rows = jnp.zeros((ks_ref.shape[3] - q_rows.shape[0] - AUX_COLS, gq), BF16)

    def scores_of(slots):
        return [_dot(k, jnp.concatenate([q_rows, aux.astype(BF16), pad_rows], axis=0))
                for k, aux, _, _, _ in slots]

    def values_of(slots):
        values = jnp.concatenate([v for _, _, v, _, _ in slots], axis=1)
        return jnp.concatenate([values, jnp.ones((ONES_ROWS, values.shape[1]), BF16)], axis=0)

    def softmax_step(slots, scores, v_cat, r, m_old):
        cols = slice(r * qt, (r + 1) * qt)
        srs, tops = [], []
        for j, (_, _, _, shift, mask) in enumerate(slots):
            sr = scores[j, :, cols] if hasattr(scores, "at") else scores[j][:, cols]
            sr = sr if mask is None else jnp.where(mask, sr, NEG_INF)
            srs.append(sr)
            tops.append(jnp.max(sr, axis=0, keepdims=True) + shift[r])
        m_new = functools.reduce(jnp.maximum, tops, m_old)
        ps = [jnp.exp((sr - (m_new - slot[3][r])).astype(BF16)) for slot, sr in zip(slots, srs)]
        weighted = _dot(v_cat, jnp.concatenate(ps, axis=0))
        return m_new, weighted[HEAD_DIM:HEAD_DIM + 1], weighted[:HEAD_DIM]

    def shifts(dist0, ok):
        rows = [-_slope(g, r) * (dist0 + lane_f) for r in range(NSA_GROUP)]
        return rows if ok is None else [jnp.where(ok, row, NEG_INF) for row in rows]

    word_base = ((bi * NSA_KV_HEADS + g) * n_tiles + i) * words_per_tile
    list_ref[0] = 0

    def scan_word(w, n):
        word = words_ref[word_base + w]

        def scan_bits(n):
            for bit in range(WORD_BITS):
                c = w * WORD_BITS + bit
                list_ref[n] = c
                n = n + jnp.where(c < i, (word >> bit) & 1, 0)
            return n

        return lax.cond(word != 0, scan_bits, lambda n: n, n)

    n_listed = lax.fori_loop(0, (i + WORD_BITS - 1) // WORD_BITS, scan_word, 0)

    def selected_slot(c, ok, mask):
        at = pl.multiple_of(c * KEY_CHUNK, KEY_CHUNK)
        bias = selt_ref[0, 0, 0, pl.ds(c * BLOCKS_PER_CHUNK, BLOCKS_PER_CHUNK), :]
        aux = aux_base
        for blk in range(BLOCKS_PER_CHUNK):
            aux = jnp.where(aux_row == blk, jnp.concatenate([bias[blk:blk + 1]] * NSA_GROUP, axis=1), aux)
        return (ks_ref[0, 0, pl.ds(at, KEY_CHUNK), :], aux, vst_ref[0, 0, c],
                shifts((start - c * KEY_CHUNK).astype(F32), ok), mask)

    def listed_slot(idx):
        ok = idx < n_listed
        c = jnp.where(ok, list_ref[jnp.minimum(idx, jnp.maximum(n_listed - 1, 0))], 0)
        return selected_slot(c, ok, None)

    floor = jnp.full((1, qt), SOFTMAX_FLOOR, F32)

    n_back = WINDOW // KEY_CHUNK
    slots = []
    for j in range(n_back + 1):
        cs = start - WINDOW + j * KEY_CHUNK
        chunk = jnp.maximum(cs, 0) // KEY_CHUNK
        at = pl.multiple_of(chunk * KEY_CHUNK, KEY_CHUNK)
        mask = (query_in_tile < key_in_chunk) if j == 0 else (
            (key_in_chunk <= query_in_tile) if j == n_back else None)
        slots.append((kw_ref[0, 0, pl.ds(at, KEY_CHUNK), :], aux_base, vwt_ref[0, 0, chunk],
                      shifts(F32(WINDOW - j * KEY_CHUNK), cs >= 0), mask))
    scores, v_cat = scores_of(slots), values_of(slots)
    for r in range(NSA_GROUP):
        _, total, weighted = softmax_step(slots, scores, v_cat, r, floor)
        ow_ref[:, r * qt:(r + 1) * qt] = weighted * jnp.where(total > 0.0, 1.0 / total, 0.0)

    slots = [selected_slot(i, None, key_in_chunk <= query_in_tile)] + [
        listed_slot(j) for j in range(ATTN_BATCH - 1)]
    scores, v_cat = scores_of(slots), values_of(slots)
    for r in range(NSA_GROUP):
        m_ref[r], l_ref[r], acc_ref[:, r * qt:(r + 1) * qt] = softmax_step(slots, scores, v_cat, r, floor)

    def listed_batch(n):
        return [listed_slot(ATTN_BATCH - 1 + n * ATTN_BATCH + j) for j in range(ATTN_BATCH)]

    def issue(slots, dst_ref):
        for j, s in enumerate(scores_of(slots)):
            dst_ref[j] = s

    def consume(slots, src_ref):
        v_cat = values_of(slots)
        for r in range(NSA_GROUP):
            cols = slice(r * qt, (r + 1) * qt)
            m_old = m_ref[r]
            m_new, total, weighted = softmax_step(slots, src_ref, v_cat, r, m_old)
            alpha = jnp.exp(m_old - m_new)
            l_ref[r] = alpha * l_ref[r] + total
            acc_ref[:, cols] = alpha * acc_ref[:, cols] + weighted
            m_ref[r] = m_new

    issue(listed_batch(0), sa_ref)

    def batch_pair(it, carry):
        first, second, third = (listed_batch(2 * it + n) for n in range(3))
        issue(second, sb_ref)
        consume(first, sa_ref)
        issue(third, sa_ref)
        consume(second, sb_ref)
        return carry

    n_rest = jnp.maximum(n_listed - (ATTN_BATCH - 1), 0)
    n_batches = (n_rest + ATTN_BATCH - 1) // ATTN_BATCH
    lax.fori_loop(0, n_batches // 2, batch_pair, 0)

    @pl.when(n_batches % 2 == 1)
    def _():
        consume(listed_batch(n_batches - 1), sa_ref)

    def finalize(r):
        l = l_ref[r]
        return acc_ref[:, r * qt:(r + 1) * qt] * jnp.where(l > 0.0, 1.0 / l, 0.0)

    outs = []
    for r in range(NSA_GROUP):
        cols = slice(r * qt, (r + 1) * qt)
        col = (g * NSA_GROUP + r) * NSA_BRANCHES
        gate = lambda br: jax.nn.sigmoid(gt_ref[0, pl.ds(col + br, 1), :])
        outs.append(gate(0) * oct_ref[0, 0, 0, :, cols].astype(F32) + gate(1) * finalize(r)
                    + gate(2) * ow_ref[:, cols])
    out_ref[0] = jnp.concatenate(outs, axis=0).T.astype(out_ref.dtype)


def _nsa_slc(words, q_t, ks, vst, kw, vwt, selt, oct, gates_t, words_per_tile):
    b, _, t, kw_cols = ks.shape
    _, _, n_tiles, qrows, qcols = q_t.shape
    n_slc = selt.shape[3]
    n_chunks = vst.shape[2]
    gw = NSA_GROUP * HEAD_DIM
    once = dict(pipeline_mode=pl.Buffered(1))
    k_spec = pl.BlockSpec((1, 1, t, kw_cols), lambda bi, g, i, w: (bi, g, 0, 0), **once)
    vt_spec = pl.BlockSpec((1, 1, n_chunks, HEAD_DIM, KEY_CHUNK), lambda bi, g, i, w: (bi, g, 0, 0, 0), **once)
    tile5 = lambda rows, cols: pl.BlockSpec((1, 1, 1, rows, cols), lambda bi, g, i, w: (bi, g, i, 0, 0))
    grid_spec = pltpu.PrefetchScalarGridSpec(
        num_scalar_prefetch=1,
        grid=(b, NSA_KV_HEADS, n_tiles),
        in_specs=[
            tile5(qrows, qcols),
            k_spec, vt_spec, k_spec, vt_spec,
            tile5(n_slc, Q_TILE),
            tile5(HEAD_DIM, qcols),
            pl.BlockSpec((1, GATE_PAD, Q_TILE), lambda bi, g, i, w: (bi, 0, i)),
        ],
        out_specs=pl.BlockSpec((1, Q_TILE, gw), lambda bi, g, i, w: (bi, i, g)),
        scratch_shapes=[pltpu.VMEM((NSA_GROUP, 1, Q_TILE), F32), pltpu.VMEM((NSA_GROUP, 1, Q_TILE), F32),
                        pltpu.VMEM((HEAD_DIM, qcols), F32), pltpu.VMEM((HEAD_DIM, qcols), F32),
                        pltpu.VMEM((ATTN_BATCH, KEY_CHUNK, qcols), F32),
                        pltpu.VMEM((ATTN_BATCH, KEY_CHUNK, qcols), F32),
                        pltpu.SMEM((n_chunks,), jnp.int32)],
    )
    return pl.pallas_call(
        functools.partial(_nsa_slc_kernel, words_per_tile=words_per_tile),
        grid_spec=grid_spec,
        out_shape=jax.ShapeDtypeStruct((b, t, NSA_HEADS * HEAD_DIM), BF16),
        compiler_params=_params("parallel", "parallel", "parallel"),
        name="nsa_selected_window",
    )(words, q_t, ks, vst, kw, vwt, selt, oct, gates_t)


POOL_HALO = 16
CONV_HALO = 8


def _merge_kernel(h_ref, g_ref, nsa_ref, pool_ref, pool_halo_ref, conv_ref, conv_halo_ref,
                  wmg_ref, wnsa_ref, pool_bd_ref, pool_scale_ref, wpool_ref, convw_ref, wconv_ref, wo_ref,
                  out_ref, pool_ext, conv_ext, *, seq_len):
    i = pl.program_id(0)
    tm, d = h_ref.shape
    cw = pool_ref.shape[1]
    pos0 = (i * tm) % seq_len
    keep_halo = jnp.where(pos0 == 0, 0.0, 1.0)
    pos = pos0 + _iota((tm, 1), 0)

    u = pool_ref[...]
    pool_ext[0:POOL_HALO, :] = pool_halo_ref[...] * keep_halo
    pool_ext[POOL_HALO:, :] = u
    lane_group = _iota((1, cw), 1) // (cw // len(POOL_WINDOWS))
    total = u
    mean = jnp.zeros_like(u)
    done = 1
    for gi, win in enumerate(POOL_WINDOWS):
        for k in range(done, win):
            total = total + pool_ext[POOL_HALO - k:POOL_HALO - k + tm, :]
        done = win
        cnt = jnp.minimum(pos + 1, win).astype(F32)
        mean = jnp.where(lane_group == gi, total / cnt, mean)
    pooled = (mean - u).astype(BF16)
    mixed = _dot(pooled, pool_bd_ref[...]) * pool_scale_ref[...]
    y_pool = _dot(mixed.astype(BF16), wpool_ref[...])

    ch = conv_ref[:, 0:cw]
    cb = conv_ref[:, cw:2 * cw]
    cc = conv_ref[:, 2 * cw:3 * cw]
    conv_ext[0:CONV_HALO, :] = conv_halo_ref[:, 0:cw] * conv_halo_ref[:, 2 * cw:3 * cw] * keep_halo
    conv_ext[CONV_HALO:, :] = cc * ch
    y = jnp.zeros((tm, cw), F32)
    for k in range(CONV_K):
        off = CONV_HALO - (CONV_K - 1) + k
        y = y + convw_ref[k:k + 1, :] * conv_ext[off:off + tm, :]
    y_conv = _dot((cb * y).astype(BF16), wconv_ref[...])

    y_nsa = _dot(nsa_ref[...], wnsa_ref[...])

    h = h_ref[...]
    xn = _rms_norm(h, g_ref[...]).astype(BF16)
    merged = jnp.zeros((tm, d), F32)
    for br, y_br in enumerate((y_nsa, y_pool, y_conv)):
        mg = jax.nn.sigmoid(_dot(xn, wmg_ref[:, br * d:(br + 1) * d]))
        merged = merged + mg * y_br
    out_ref[...] = h + _dot(merged.astype(BF16), wo_ref[...])


def _merge(h, g, nsa, pool_u, conv, wmg, wnsa, pool_bd, pool_scale, wpool, convw, wconv, wo, seq_len):
    n, d = h.shape
    tm = TOKEN_TILE
    cw = pool_u.shape[1]
    row = lambda width: pl.BlockSpec((tm, width), lambda i: (i, 0))
    full = lambda a: pl.BlockSpec(a.shape, lambda i: (0,) * a.ndim)
    halo = lambda rows, width: pl.BlockSpec(
        (rows, width), lambda i: (jnp.maximum(i * (tm // rows) - 1, 0), 0))
    return pl.pallas_call(
        functools.partial(_merge_kernel, seq_len=seq_len),
        grid=(n // tm,),
        in_specs=[row(d), full(g), row(nsa.shape[1]), row(cw), halo(POOL_HALO, cw),
                  row(conv.shape[1]), halo(CONV_HALO, conv.shape[1]),
                  full(wmg), full(wnsa), full(pool_bd), full(pool_scale), full(wpool), full(convw),
                  full(wconv), full(wo)],
        out_specs=row(d),
        out_shape=jax.ShapeDtypeStruct((n, d), F32),
        scratch_shapes=[pltpu.VMEM((tm + POOL_HALO, cw), F32), pltpu.VMEM((tm + CONV_HALO, cw), F32)],
        compiler_params=_params("parallel"),
        name="merge",
    )(h, g, nsa, pool_u, pool_u, conv, conv, wmg, wnsa, pool_bd, pool_scale, wpool, convw, wconv, wo)


def _route(logits):
    lane = _iota(logits.shape, 1)
    lane_f = lane.astype(F32)
    big = F32(1e9)
    is_group = lane < N_EXPERT_GROUPS
    gl = jnp.where(is_group, logits, NEG_INF)
    g_max = jnp.max(gl, axis=1, keepdims=True)
    g_sel = jnp.min(jnp.where(gl == g_max, lane_f, big), axis=1, keepdims=True)
    g_prob = 1.0 / jnp.sum(jnp.where(is_group, jnp.exp(gl - g_max), 0.0), axis=1, keepdims=True)
    lo = N_EXPERT_GROUPS + EXPERTS_PER_GROUP * g_sel
    in_group = (lane_f >= lo) & (lane_f < lo + EXPERTS_PER_GROUP)
    el = jnp.where(in_group, logits, NEG_INF)
    v1 = jnp.max(el, axis=1, keepdims=True)
    i1 = jnp.min(jnp.where((el == v1) & in_group, lane_f, big), axis=1, keepdims=True)
    el2 = jnp.where(lane_f == i1, NEG_INF, el)
    rest = in_group & (lane_f != i1)
    v2 = jnp.max(el2, axis=1, keepdims=True)
    i2 = jnp.min(jnp.where((el2 == v2) & rest, lane_f, big), axis=1, keepdims=True)
    e2 = jnp.exp(v2 - v1)
    w1 = g_prob / (1.0 + e2)
    w2 = g_prob * e2 / (1.0 + e2)
    return jnp.where(lane_f == i1, w1, 0.0) + jnp.where(lane_f == i2, w2, 0.0), g_sel


GROUP_LANE = N_EXPERT_GROUPS + N_EXPERTS


RANK_LANE = GROUP_LANE + 1


def _router_kernel(h_ref, g_ref, wr_ref, br_ref, before_ref, xn_ref, comb_ref, count_ref):
    @pl.when(pl.program_id(0) == 0)
    def _():
        count_ref[...] = jnp.zeros(count_ref.shape, F32)

    xn = _rms_norm(h_ref[...], g_ref[...])
    xn_hi = xn.astype(BF16)
    xn_lo = (xn - xn_hi.astype(F32)).astype(BF16)
    logits = (_dot(xn_hi, wr_ref[0]) + (_dot(xn_hi, wr_ref[1]) + _dot(xn_lo, wr_ref[0]))) + br_ref[...]
    comb, g_sel = _route(logits)
    lane = _iota(comb.shape, 1)
    chose = jnp.where(lane.astype(F32) == g_sel, 1.0, 0.0)
    earlier = _dot(before_ref[...], chose.astype(BF16)) + count_ref[0:1, :]
    rank = jnp.sum(chose * earlier, axis=1, keepdims=True)
    count_ref[0:1, :] = count_ref[0:1, :] + jnp.sum(chose, axis=0, keepdims=True)
    comb_ref[...] = jnp.where(lane == GROUP_LANE, g_sel, jnp.where(lane == RANK_LANE, rank, comb))
    first = N_EXPERT_GROUPS + EXPERTS_PER_GROUP * g_sel
    local = jnp.zeros(comb.shape, F32)
    for e in range(EXPERTS_PER_GROUP):
        c_e = jnp.sum(jnp.where(lane.astype(F32) == first + e, comb, 0.0), axis=1, keepdims=True)
        local = jnp.where((lane == e) | (lane == EXPERTS_PER_GROUP + e), c_e, local)
    local_hi = local.astype(BF16)
    d = xn_hi.shape[1]
    xn_ref[:, :d] = xn_hi
    xn_ref[:, d:] = jnp.where(lane < EXPERTS_PER_GROUP, local_hi, (local - local_hi.astype(F32)).astype(BF16))


def _router(h, g, wr, br, before):
    n, d = h.shape
    tm = MOE_TILE
    full = lambda a: pl.BlockSpec(a.shape, lambda i: (0,) * a.ndim)
    return pl.pallas_call(
        _router_kernel,
        grid=(n // tm,),
        in_specs=[pl.BlockSpec((tm, d), lambda i: (i, 0)), full(g), full(wr), full(br), full(before)],
        out_specs=[pl.BlockSpec((tm, d + ROUTER_PAD), lambda i: (i, 0)),
                   pl.BlockSpec((tm, ROUTER_PAD), lambda i: (i, 0)),
                   pl.BlockSpec((8, ROUTER_PAD), lambda i: (0, 0))],
        out_shape=[jax.ShapeDtypeStruct((n, d + ROUTER_PAD), BF16), jax.ShapeDtypeStruct((n, ROUTER_PAD), F32),
                   jax.ShapeDtypeStruct((8, ROUTER_PAD), F32)],
        compiler_params=_params("arbitrary"),
        name="router",
    )(h, g, wr, br, before)


def _experts_kernel(tile_group_ref, n_active_ref, x_ref, w1_ref, w3_ref, w2_ref, out_ref, acc_ref):
    i = pl.program_id(0)

    @pl.when(i < n_active_ref[0])
    def _():
        d = out_ref.shape[1]
        x = x_ref[:, :d]
        comb = x_ref[:, d:].astype(F32)
        lane = _iota(comb.shape, 1)
        for e in range(EXPERTS_PER_GROUP):
            c_e = jnp.sum(jnp.where((lane == e) | (lane == EXPERTS_PER_GROUP + e), comb, 0.0),
                          axis=1, keepdims=True)
            a = (jax.nn.silu(_dot(x, w1_ref[0, 0, e].astype(BF16)))
                 * _dot(x, w3_ref[0, 0, e].astype(BF16))) * c_e
            y = _dot(a.astype(BF16), w2_ref[0, 0, e].astype(BF16))
            if e == 0:
                acc_ref[...] = y
            else:
                acc_ref[...] += y
        out_ref[...] = acc_ref[...].astype(out_ref.dtype)


def _experts(tile_group, n_active, x_sorted, w1, w3, w2, layer):
    ns = x_sorted.shape[0]
    d = w1.shape[3]
    tm = MOE_SORT_TILE
    group_w = lambda w: pl.BlockSpec((1, 1) + w.shape[2:], lambda i, tg, na: (layer, tg[i], 0, 0, 0),
                                     pipeline_mode=pl.Buffered(1))
    grid_spec = pltpu.PrefetchScalarGridSpec(
        num_scalar_prefetch=2,
        grid=(ns // tm,),
        in_specs=[
            pl.BlockSpec((tm, x_sorted.shape[1]), lambda i, tg, na: (i, 0)),
            group_w(w1), group_w(w3), group_w(w2),
        ],
        out_specs=pl.BlockSpec((tm, d), lambda i, tg, na: (i, 0)),
        scratch_shapes=[pltpu.VMEM((tm, d), F32)],
    )
    return pl.pallas_call(
        _experts_kernel,
        grid_spec=grid_spec,
        out_shape=jax.ShapeDtypeStruct((ns, d), BF16),
        compiler_params=_params("arbitrary"),
        name="experts",
    )(tile_group, n_active, x_sorted, w1, w3, w2)


def _residual_kernel(h_ref, y_ref, gf_ref, out_ref, *, final_norm):
    out = h_ref[...] + y_ref[...].astype(F32)
    out_ref[...] = _rms_norm(out, gf_ref[...]) if final_norm else out


def _residual(h, y, gf, final_norm):
    n, d = h.shape
    tm = MOE_TILE
    row = pl.BlockSpec((tm, d), lambda i: (i, 0))
    return pl.pallas_call(
        functools.partial(_residual_kernel, final_norm=final_norm),
        grid=(n // tm,),
        in_specs=[row, row, pl.BlockSpec(gf.shape, lambda i: (0, 0))],
        out_specs=row,
        out_shape=jax.ShapeDtypeStruct((n, d), F32),
        compiler_params=_params("parallel"),
        name="residual",
    )(h, y, gf)


def _group_sort_plan(group_id, rank, counts, tile):
    n = group_id.shape[0]
    n_slots = n + N_EXPERT_GROUPS * tile
    padded = (counts + tile - 1) // tile * tile
    ends = jnp.cumsum(padded)
    slot = (ends - padded)[group_id] + rank
    source = jnp.zeros((n_slots,), jnp.int32).at[slot].set(jnp.arange(n, dtype=jnp.int32))
    tile_start = jnp.arange(n_slots // tile, dtype=jnp.int32) * tile
    tile_group = jnp.minimum(jnp.searchsorted(ends, tile_start, side="right"), N_EXPERT_GROUPS - 1)
    return slot, source, tile_group.astype(jnp.int32), (ends[-1:] // tile).astype(jnp.int32)


def _moe(h, g, wr, br, before, w1, w3, w2, layer, gf, final_norm):
    xn, comb, counts = _router(h, g, wr, br, before)
    as_int = lambda a: a.astype(jnp.int32)
    slot, source, tile_group, n_active = _group_sort_plan(
        as_int(comb[:, GROUP_LANE]), as_int(comb[:, RANK_LANE]), as_int(counts[0, :N_EXPERT_GROUPS]),
        MOE_SORT_TILE)
    grouped = lambda w: w.reshape((w.shape[0], N_EXPERT_GROUPS, EXPERTS_PER_GROUP) + w.shape[2:])
    rows = lambda a, idx: jnp.take(a, idx, axis=0, mode="clip")
    y_sorted = _experts(tile_group, n_active, rows(xn, source), grouped(w1), grouped(w3), grouped(w2), layer)
    return _residual(h, rows(y_sorted, slot), gf, final_norm)


def _selection_constants(seq_len):
    ncp = seq_len // CMP_STRIDE
    n_slc = seq_len // SLC_BLOCK
    ratio = SLC_BLOCK // CMP_STRIDE
    lead = CMP_LEN // CMP_STRIDE - 1
    c = np.arange(ncp)[:, None]
    j = np.arange(n_slc)[None, :]
    pool_m = ((c >= ratio * j - lead) & (c < ratio * j + ratio)).astype(np.float32)
    blocks_per_chunk = KEY_CHUNK // SLC_BLOCK
    n_chunks = seq_len // KEY_CHUNK
    pair_m = np.zeros((n_slc, LANES * ((n_chunks + LANES - 1) // LANES)), np.float32)
    pair_m[np.arange(n_slc), np.arange(n_slc) // blocks_per_chunk] = 1.0
    n_words = (n_chunks + WORD_BITS - 1) // WORD_BITS
    bits_m = np.zeros((pair_m.shape[1], LANES), np.float32)
    ch = np.arange(n_chunks)
    bits_m[ch, ch // WORD_BITS] = 2.0 ** (ch % WORD_BITS)
    key_aux = np.zeros((TOKEN_TILE, LANES), np.float32)
    in_chunk = np.arange(TOKEN_TILE) % KEY_CHUNK
    key_aux[np.arange(TOKEN_TILE), HEAD_DIM + in_chunk // SLC_BLOCK] = 1.0
    key_aux[:, HEAD_DIM + AUX_SLOPE] = in_chunk
    cmp_aux = np.zeros((ncp, LANES), np.float32)
    cmp_aux[:, HEAD_DIM] = np.arange(ncp) // CMP_AUX_SPLIT
    cmp_aux[:, HEAD_DIM + 1] = np.arange(ncp) % CMP_AUX_SPLIT
    as_bf16 = lambda a: jnp.asarray(a, BF16)
    return as_bf16(pool_m.T), as_bf16(pair_m), as_bf16(bits_m), jnp.asarray(key_aux), jnp.asarray(cmp_aux), n_words


def kernel(x, norm1_g, w_in, cmp_pe, cmp_w1, cmp_w2, w_nsa_proj, pool_w, pool_scale, w_pool_proj, conv_w,
           w_conv_proj, w_o, norm2_g, router_group_w, router_group_b, router_expert_w, router_expert_b,
           expert_w1, expert_w3, expert_w2, final_norm_g):
    b, t, d = x.shape
    n = b * t
    depth = w_in.shape[0]
    dq = NSA_HEADS * HEAD_DIM
    dkv = 6 * NSA_KV_HEADS * HEAD_DIM
    dgate = NSA_HEADS * NSA_BRANCHES
    cw = d // 4
    assert t % TOKEN_TILE == 0 and n % MOE_TILE == 0 and t % Q_TILE == 0
    n_slc = t // SLC_BLOCK
    n_sel = min(SLC_TOPN, n_slc)
    n_chunks16 = t // CMP_STRIDE
    kvw = NSA_KV_HEADS * HEAD_DIM
    poolt_m, pair_m, bits_m, key_aux, cmp_aux, n_words = _selection_constants(t)
    assert Q_TILE == KEY_CHUNK and AUX_SLOPE < AUX_COLS and KEY_CHUNK <= 256
    assert n_chunks16 <= 256 * CMP_AUX_SPLIT and n_chunks16 % min(CMP_ROWS_STEP, n_chunks16) == 0
    assert n_words <= LANES and pair_m.shape[1] == LANES

    before = jnp.asarray(np.tril(np.ones((MOE_TILE, MOE_TILE), np.float32), -1), BF16)
    h = x.reshape(n, d)
    for l in range(depth):
        wl = w_in[l]
        o_gate = dq + dkv
        o_pool = o_gate + dgate
        o_merge = o_pool + cw + 3 * cw
        kv_cols = lambda kind: wl[:, dq + kind * kvw:dq + (kind + 1) * kvw]
        wq_t = (wl[:, :dq] * (HEAD_DIM ** -0.5)).T.astype(BF16)
        wv_t = jnp.concatenate([kv_cols(3), kv_cols(5)], axis=1).T.astype(BF16)
        wg_t = jnp.pad(wl[:, o_gate:o_pool], ((0, 0), (0, GATE_PAD - dgate))).T.astype(BF16)
        no_aux = jnp.zeros((d, LANES - HEAD_DIM), F32)
        wk = jnp.concatenate([piece for kind in (2, 4) for gi in range(NSA_KV_HEADS)
                              for piece in (kv_cols(kind)[:, gi * HEAD_DIM:(gi + 1) * HEAD_DIM], no_aux)],
                             axis=1).astype(BF16)
        wn = jnp.concatenate([kv_cols(0), kv_cols(1), wl[:, o_pool:o_merge]], axis=1).astype(BF16)
        wmg = wl[:, o_merge:].astype(BF16)
        pool_bd = jax.scipy.linalg.block_diag(*[pool_w[l, gi] for gi in range(pool_w.shape[1])]).astype(BF16)
        convw = jnp.pad(conv_w[l], ((0, 8 - CONV_K), (0, 0)))
        wr = jnp.pad(jnp.concatenate([router_group_w[l], router_expert_w[l]], axis=1),
                     ((0, 0), (0, ROUTER_PAD - N_EXPERT_GROUPS - N_EXPERTS)))
        wr_hi = wr.astype(BF16)
        wr = jnp.stack([wr_hi, (wr - wr_hi.astype(F32)).astype(BF16)])
        br = jnp.pad(jnp.concatenate([router_group_b[l], router_expert_b[l]]),
                     (0, ROUTER_PAD - N_EXPERT_GROUPS - N_EXPERTS))[None, :]
        pe = jnp.broadcast_to(cmp_pe[l].reshape(2, 1, CMP_LEN * HEAD_DIM), (2, 8, CMP_LEN * HEAD_DIM)).astype(BF16)
        halves = CMP_LEN // CMP_STRIDE
        w1_bd = jnp.einsum("khldc,gq->khlgdqc",
                           cmp_w1[l].reshape(2, halves, CMP_STRIDE, HEAD_DIM, CMP_HIDDEN),
                           jnp.eye(NSA_KV_HEADS, dtype=F32))
        w1_bd = w1_bd.reshape(2, halves, CMP_STRIDE * kvw, NSA_KV_HEADS * CMP_HIDDEN).astype(BF16)
        w2_k = jnp.pad(cmp_w2[l, 0], ((0, 0), (0, LANES - HEAD_DIM))).astype(BF16)
        w2_vt = cmp_w2[l, 1].T.astype(BF16)

        q_t, vst, vwt, gates_t, ks, kw, cmp_src, pool_u, conv = _inproj(
            h, norm1_g[l][None, :], wq_t, wv_t, wg_t, wk, wn, key_aux, b)
        kc_aux, vc_t = _compress(cmp_src, pe, cmp_w1[l].astype(BF16), w1_bd, w2_k, w2_vt, cmp_aux)
        oc_t, sel_t, words = _nsa_cmp(q_t, kc_aux, vc_t, poolt_m, pair_m, bits_m, n_sel)
        words = words[:, :, :, 0, :n_words].reshape(-1)
        nsa = _nsa_slc(words, q_t, ks, vst, kw, vwt, sel_t, oc_t, gates_t, n_words)
        h = _merge(h, norm1_g[l][None, :], nsa.reshape(n, dq), pool_u, conv, wmg,
                   w_nsa_proj[l].astype(BF16), pool_bd, pool_scale[l][None, :], w_pool_proj[l].astype(BF16),
                   convw, w_conv_proj[l].astype(BF16), w_o[l].astype(BF16), t)
        h = _moe(h, norm2_g[l][None, :], wr, br, before, expert_w1, expert_w3, expert_w2, l,
                 final_norm_g[None, :], final_norm=(l == depth - 1))
    return h.reshape(b, t, d)
```

```python
import functools

import jax
import jax.numpy as jnp
import numpy as np
from jax import lax
from jax.experimental import pallas as pl
from jax.experimental.pallas import tpu as pltpu

F32 = jnp.float32
BF16 = jnp.bfloat16

HEAD_DIM = 64
NSA_HEADS = 8
NSA_KV_HEADS = 2
NSA_GROUP = NSA_HEADS // NSA_KV_HEADS
CMP_LEN = 32
CMP_STRIDE = 16
CMP_HIDDEN = 4 * HEAD_DIM
SLC_BLOCK = 64
SLC_TOPN = 16
WINDOW = 512
NSA_BRANCHES = 3
POOL_WINDOWS = (2, 4, 8, 16)
CONV_K = 3
N_BRANCHES = 3
N_EXPERT_GROUPS = 4
EXPERTS_PER_GROUP = 8
N_EXPERTS = N_EXPERT_GROUPS * EXPERTS_PER_GROUP
RMS_EPS = 1e-6
NEG_INF = -1e30
FORCE_SCORE = 1e30
ALIBI_SLOPES = tuple(float(2.0 ** (-8.0 * (h + 1) / NSA_HEADS)) for h in range(NSA_HEADS))

LANES = 128
VMEM_LIMIT = 56 * 1024 * 1024
TOKEN_TILE = 512
MOE_TILE = 1024
MOE_SORT_TILE = 512
Q_TILE = 128
KEY_CHUNK = 128
GATE_PAD = LANES
ROUTER_PAD = LANES
WORD_BITS = 16


def _params(*semantics):
    return pltpu.CompilerParams(dimension_semantics=semantics, vmem_limit_bytes=VMEM_LIMIT)


def _dot(a, b):
    return jnp.dot(a, b, preferred_element_type=F32)


def _dot_nt(a, b):
    return lax.dot_general(a, b, (((1,), (1,)), ((), ())), preferred_element_type=F32)


def _rms_norm(x, g):
    y = x * lax.rsqrt(jnp.mean(x * x, axis=-1, keepdims=True) + RMS_EPS)
    return y * g


def _iota(shape, dim):
    return lax.broadcasted_iota(jnp.int32, shape, dim)


def _inproj_kernel(x_ref, g_ref, wq_ref, wv_ref, wg_ref, wk_ref, wn_ref, kaux_ref,
                   q_ref, vs_ref, vw_ref, gate_ref, ks_ref, kw_ref, cmp_ref, pool_ref, conv_ref):
    xn = _rms_norm(x_ref[...], g_ref[...]).astype(BF16)
    sub_tiles = x_ref.shape[0] // Q_TILE
    q_t = _dot_nt(wq_ref[...], xn)
    for g in range(NSA_KV_HEADS):
        for j in range(sub_tiles):
            for r in range(NSA_GROUP):
                head = g * NSA_GROUP + r
                q_ref[0, g, j, :, r * Q_TILE:(r + 1) * Q_TILE] = q_t[
                    head * HEAD_DIM:(head + 1) * HEAD_DIM, j * Q_TILE:(j + 1) * Q_TILE].astype(BF16)
    v_t = _dot_nt(wv_ref[...], xn)
    k = _dot(xn, wk_ref[...])
    for branch, (v_ref, k_ref) in enumerate(((vs_ref, ks_ref), (vw_ref, kw_ref))):
        for g in range(NSA_KV_HEADS):
            slab = branch * NSA_KV_HEADS + g
            for j in range(sub_tiles):
                v_ref[0, g, j] = v_t[slab * HEAD_DIM:(slab + 1) * HEAD_DIM,
                                     j * KEY_CHUNK:(j + 1) * KEY_CHUNK].astype(BF16)
            k_ref[0, g] = (k[:, slab * LANES:(slab + 1) * LANES] + kaux_ref[...]).astype(BF16)
    gate_ref[0] = _dot_nt(wg_ref[...], xn)
    col = 0
    for ref in (cmp_ref, pool_ref, conv_ref):
        width = ref.shape[-1]
        ref[...] = _dot(xn, wn_ref[:, col:col + width]).reshape(ref.shape)
        col += width


def _inproj(h, g, wq_t, wv_t, wg_t, wk, wn, kaux, batch):
    n, d = h.shape
    t = n // batch
    tm = TOKEN_TILE
    steps = t // tm
    sub = tm // Q_TILE
    cw = d // 4
    gq = NSA_GROUP * Q_TILE
    full = lambda a: pl.BlockSpec(a.shape, lambda i: (0,) * a.ndim)
    row = lambda width: pl.BlockSpec((tm, width), lambda i: (i, 0))
    tiles = lambda rows, cols: pl.BlockSpec((1, NSA_KV_HEADS, sub, rows, cols),
                                            lambda i: (i // steps, 0, i % steps, 0, 0))
    keys = pl.BlockSpec((1, NSA_KV_HEADS, tm, LANES), lambda i: (i // steps, 0, i % steps, 0))
    sds = jax.ShapeDtypeStruct
    v_shape = sds((batch, NSA_KV_HEADS, t // KEY_CHUNK, HEAD_DIM, KEY_CHUNK), BF16)
    k_shape = sds((batch, NSA_KV_HEADS, t, LANES), BF16)
    return pl.pallas_call(
        _inproj_kernel,
        grid=(n // tm,),
        in_specs=[row(d), full(g), full(wq_t), full(wv_t), full(wg_t), full(wk), full(wn), full(kaux)],
        out_specs=[tiles(HEAD_DIM, gq), tiles(HEAD_DIM, KEY_CHUNK), tiles(HEAD_DIM, KEY_CHUNK),
                   pl.BlockSpec((1, GATE_PAD, tm), lambda i: (i // steps, 0, i % steps)),
                   keys, keys,
                   pl.BlockSpec((1, tm, 2 * NSA_KV_HEADS * HEAD_DIM), lambda i: (i // steps, i % steps, 0)),
                   row(cw), row(3 * cw)],
        out_shape=[sds((batch, NSA_KV_HEADS, t // Q_TILE, HEAD_DIM, gq), BF16), v_shape, v_shape,
                   sds((batch, GATE_PAD, t), F32), k_shape, k_shape,
                   sds((batch, t, 2 * NSA_KV_HEADS * HEAD_DIM), F32), sds((n, cw), F32), sds((n, 3 * cw), F32)],
        compiler_params=_params("parallel"),
        name="inproj",
    )(h, g, wq_t, wv_t, wg_t, wk, wn, kaux)


def _gelu_tanh(x):
    return 0.5 * x * (1.0 + jnp.tanh(0.7978845608028654 * (x + 0.044715 * x * x * x)))


def _compress_kernel(src_ref, pe_ref, w1_ref, w1bd_ref, w2k_ref, w2vt_ref, caux_ref, kc_ref, vct_ref):
    kind = pl.program_id(1)
    ncp = kc_ref.shape[2]
    hidden = w1_ref.shape[2]
    pieces = [src_ref[0, pl.ds(l, ncp, stride=CMP_STRIDE), :].astype(BF16) for l in range(CMP_STRIDE)]
    chunk = jnp.concatenate(pieces, axis=1)
    first = _dot(chunk, w1bd_ref[0, 0])
    second = _dot(chunk, w1bd_ref[0, 1])
    bias = _dot(pe_ref[0], w1_ref[0])[0:1, :]
    hid = first + pltpu.roll(second, ncp - 1, 0) + jnp.concatenate([bias] * NSA_KV_HEADS, axis=1)
    row = _iota((ncp, 1), 0)
    act = jnp.where(row < ncp - 1, _gelu_tanh(hid), 0.0).astype(BF16)
    for g in range(NSA_KV_HEADS):
        act_g = act[:, g * hidden:(g + 1) * hidden]

        @pl.when(kind == 0)
        def _():
            kc_ref[0, g] = (_dot(act_g, w2k_ref[...]) + caux_ref[...]).astype(BF16)

        @pl.when(kind == 1)
        def _():
            vct_ref[0, g * HEAD_DIM:(g + 1) * HEAD_DIM, :] = _dot_nt(w2vt_ref[...], act_g).astype(BF16)


def _compress(src, pe, w1, w1bd, w2k, w2vt, caux):
    b, t, _ = src.shape
    ncp = t // CMP_STRIDE
    gd = NSA_KV_HEADS * HEAD_DIM
    full = lambda a: pl.BlockSpec(a.shape, lambda bi, k: (0,) * a.ndim)
    per_kind = lambda a: pl.BlockSpec((1,) + a.shape[1:], lambda bi, k: (k,) + (0,) * (a.ndim - 1))
    return pl.pallas_call(
        _compress_kernel,
        grid=(b, 2),
        in_specs=[pl.BlockSpec((1, t, gd), lambda bi, k: (bi, 0, k)),
                  per_kind(pe), per_kind(w1), per_kind(w1bd), full(w2k), full(w2vt), full(caux)],
        out_specs=[pl.BlockSpec((1, NSA_KV_HEADS, ncp, LANES), lambda bi, k: (bi, 0, 0, 0)),
                   pl.BlockSpec((1, gd, ncp), lambda bi, k: (bi, 0, 0))],
        out_shape=[jax.ShapeDtypeStruct((b, NSA_KV_HEADS, ncp, LANES), BF16),
                   jax.ShapeDtypeStruct((b, gd, ncp), BF16)],
        compiler_params=_params("parallel", "arbitrary"),
        name="compress",
    )(src, pe, w1, w1bd, w2k, w2vt, caux)


SOFTMAX_FLOOR = -1e29
TAKEN = -3e38
ATTN_BATCH = 4
BLOCKS_PER_CHUNK = KEY_CHUNK // SLC_BLOCK
AUX_COLS = 16
AUX_SLOPE = BLOCKS_PER_CHUNK
ONES_ROWS = 16
CMP_ROWS_STEP = 256
CMP_AUX_SPLIT = 128


def _slope(g, r):
    if isinstance(g, int):
        return jnp.float32(ALIBI_SLOPES[g * NSA_GROUP + r])
    s = jnp.float32(ALIBI_SLOPES[r])
    for gi in range(1, NSA_KV_HEADS):
        s = jnp.where(g == gi, jnp.float32(ALIBI_SLOPES[gi * NSA_GROUP + r]), s)
    return s


def _nsa_cmp_kernel(qt_ref, kc_ref, vct_ref, poolt_ref, pair_ref, bits_ref, oct_ref, selt_ref, words_ref,
                    *, n_sel):
    i = pl.program_id(1)
    qt = selt_ref.shape[4]
    ncp = kc_ref.shape[2]
    n_slc = poolt_ref.shape[0]
    gq = qt_ref.shape[4]
    start = i * qt
    t = start + _iota((1, qt), 1)
    aux_row = _iota((AUX_COLS, gq), 0)
    col_head = _iota((1, gq), 1) // qt
    pad_rows = jnp.zeros((kc_ref.shape[3] - qt_ref.shape[3] - AUX_COLS, gq), BF16)

    def weights(g):
        slope_cols = jnp.zeros((1, gq), F32)
        for r in range(NSA_GROUP):
            slope_cols = jnp.where(col_head == r, _slope(g, r), slope_cols)
        aux = jnp.where(aux_row == 0, slope_cols * (CMP_STRIDE * CMP_AUX_SPLIT),
                        jnp.where(aux_row == 1, slope_cols * CMP_STRIDE, 0.0))
        return jnp.concatenate([qt_ref[0, g, 0], aux.astype(BF16), pad_rows], axis=0)

    def importance(g, nr, nb):
        s = _dot(kc_ref[0, g, :nr, :], weights(g))
        edge = min(nr, 2 * CMP_ROWS_STEP)
        cmp_end = ((nr - edge) + _iota((edge, 1), 0)) * CMP_STRIDE + (CMP_LEN - 1)
        visible = cmp_end <= t
        vct = vct_ref[0, g * HEAD_DIM:(g + 1) * HEAD_DIM, :nr]
        psum = jnp.zeros((nr, qt), F32)
        for r in range(NSA_GROUP):
            cols = slice(r * qt, (r + 1) * qt)
            sr = s[:, cols]
            tail = jnp.where(visible, sr[nr - edge:], NEG_INF)
            sr = tail if edge == nr else jnp.concatenate([sr[:nr - edge], tail], axis=0)
            m = jnp.maximum(jnp.max(sr, axis=0, keepdims=True), SOFTMAX_FLOOR)
            e = jnp.exp(sr - m)
            l = jnp.sum(e, axis=0, keepdims=True)
            inv = jnp.where(l > 0.0, 1.0 / l, 0.0)
            oct_ref[0, g, 0, :, cols] = (_dot(vct, e.astype(BF16)) * inv).astype(BF16)
            psum = psum + e * inv
        return _dot(poolt_ref[:nb, :nr], psum.astype(BF16))

    def visible_prefix(nr):
        nb = min(n_slc, nr * CMP_STRIDE // SLC_BLOCK)
        imp = jnp.concatenate([importance(g, nr, nb) for g in range(NSA_KV_HEADS)], axis=1)
        blk = _iota((nb, 1), 0)
        cur = jnp.concatenate([t // SLC_BLOCK] * NSA_KV_HEADS, axis=1)
        forced = (blk == 0) | (blk == cur) | (blk == cur - 1)
        score = jnp.where(forced, TAKEN, jnp.where(blk <= cur, imp, NEG_INF))
        n_forced = 1 + jnp.where(cur >= 1, 1, 0) + jnp.where(cur >= 2, 1, 0)
        blk_f = blk.astype(F32)

        def take_one(score, active):
            m = jnp.max(score, axis=0, keepdims=True)
            first = jnp.min(jnp.where(score == m, blk_f, F32(1e9)), axis=0, keepdims=True)
            hit = (blk_f == first) if active is None else ((blk_f == first) & active)
            return jnp.where(hit, TAKEN, score)

        common_rounds = max(n_sel - 3, 0)
        for _ in range(common_rounds):
            score = take_one(score, None)

        def early_rounds(score):
            for k in range(common_rounds, n_sel - 1):
                score = take_one(score, n_sel - n_forced > k)
            return score

        score = lax.cond(start < 2 * SLC_BLOCK, early_rounds, lambda sc: sc, score)
        for g in range(NSA_KV_HEADS):
            sel_g = score[:, g * qt:(g + 1) * qt] == TAKEN
            selt_ref[0, g, 0, :nb, :] = jnp.where(sel_g, 0.0, NEG_INF)
            if nb < n_slc:
                selt_ref[0, g, 0, nb:, :] = jnp.full((n_slc - nb, qt), NEG_INF, F32)
            count = _dot_nt(jnp.ones((8, qt), BF16), jnp.where(sel_g, 1.0, 0.0).astype(BF16))
            used = jnp.where(count > 0.0, 1.0, 0.0).astype(BF16)
            chunk_used = jnp.where(_dot(used, pair_ref[:nb, :]) > 0.0, 1.0, 0.0).astype(BF16)
            words_ref[0, g, 0] = _dot(chunk_used, bits_ref[...]).astype(jnp.int32)

    step = min(CMP_ROWS_STEP, ncp)
    rows_needed = jnp.minimum((start + qt - CMP_LEN) // CMP_STRIDE + 1, ncp)
    n_steps = (rows_needed + step - 1) // step
    for k in range(ncp // step):
        pl.when(n_steps == k + 1)(functools.partial(visible_prefix, (k + 1) * step))


def _nsa_cmp(q_t, kc, vct, poolt_m, pair_m, bits_m, n_sel):
    b, ng, n_tiles, qrows, qcols = q_t.shape
    ncp = kc.shape[2]
    n_slc = poolt_m.shape[0]
    const = lambda shape: pl.BlockSpec(shape, lambda bi, i: (0,) * len(shape))
    tile5 = lambda rows, cols: pl.BlockSpec((1, ng, 1, rows, cols), lambda bi, i: (bi, 0, i, 0, 0))
    return pl.pallas_call(
        functools.partial(_nsa_cmp_kernel, n_sel=n_sel),
        grid=(b, n_tiles),
        in_specs=[
            tile5(qrows, qcols),
            pl.BlockSpec((1, ng, ncp, kc.shape[3]), lambda bi, i: (bi, 0, 0, 0)),
            pl.BlockSpec((1, ng * HEAD_DIM, ncp), lambda bi, i: (bi, 0, 0)),
            const(poolt_m.shape), const(pair_m.shape), const(bits_m.shape),
        ],
        out_specs=[tile5(HEAD_DIM, qcols), tile5(n_slc, Q_TILE), tile5(8, LANES)],
        out_shape=[
            jax.ShapeDtypeStruct((b, ng, n_tiles, HEAD_DIM, qcols), BF16),
            jax.ShapeDtypeStruct((b, ng, n_tiles, n_slc, Q_TILE), F32),
            jax.ShapeDtypeStruct((b, ng, n_tiles, 8, LANES), jnp.int32),
        ],
        compiler_params=_params("parallel", "parallel"),
        name="nsa_compressed",
    )(q_t, kc, vct, poolt_m, pair_m, bits_m)


def _nsa_slc_kernel(words_ref, qt_ref, ks_ref, vst_ref, kw_ref, vwt_ref, selt_ref, oct_ref, gt_ref,
                    out_ref, m_ref, l_ref, acc_ref, ow_ref, sa_ref, sb_ref, list_ref, *, words_per_tile):
    bi = pl.program_id(0)
    g = pl.program_id(1)
    i = pl.program_id(2)
    n_tiles = pl.num_programs(2)
    qt = out_ref.shape[1]
    q_rows = qt_ref[0, 0, 0]
    gq = q_rows.shape[1]
    start = i * qt
    lane_f = _iota((1, qt), 1).astype(F32)
    key_in_chunk = _iota((KEY_CHUNK, qt), 0)
    query_in_tile = _iota((KEY_CHUNK, qt), 1)

    aux_row = _iota((AUX_COLS, gq), 0)
    col_head = _iota((1, gq), 1) // qt
    slope_cols = jnp.zeros((1, gq), F32)
    for r in range(NSA_GROUP):
        slope_cols = jnp.where(col_head == r, _slope(g, r), slope_cols)
    aux_base = jnp.where(aux_row == AUX_SLOPE, slope_cols, 0.0)
    pad_rows = jnp.zeros((ks_ref.shape[3] - q_rows.shape[0] - AUX_COLS, gq), BF16)

    def scores_of(slots):
        return [_dot(k, jnp.concatenate([q_rows, aux.astype(BF16), pad_rows], axis=0))
                for k, aux, _, _, _ in slots]

    def values_of(slots):
        values = jnp.concatenate([v for _, _, v, _, _ in slots], axis=1)
        return jnp.concatenate([values, jnp.ones((ONES_ROWS, values.shape[1]), BF16)], axis=0)

    def softmax_step(slots, scores, v_cat, r, m_old):
        cols = slice(r * qt, (r + 1) * qt)
        srs, tops = [], []
        for j, (_, _, _, shift, mask) in enumerate(slots):
            sr = scores[j, :, cols] if hasattr(scores, "at") else scores[j][:, cols]
            sr = sr if mask is None else jnp.where(mask, sr, NEG_INF)
            srs.append(sr)
            tops.append(jnp.max(sr, axis=0, keepdims=True) + shift[r])
        m_new = functools.reduce(jnp.maximum, tops, m_old)
        ps = [jnp.exp((sr - (m_new - slot[3][r])).astype(BF16)) for slot, sr in zip(slots, srs)]
        weighted = _dot(v_cat, jnp.concatenate(ps, axis=0))
        return m_new, weighted[HEAD_DIM:HEAD_DIM + 1], weighted[:HEAD_DIM]

    def shifts(dist0, ok):
        rows = [-_slope(g, r) * (dist0 + lane_f) for r in range(NSA_GROUP)]
        return rows if ok is None else [jnp.where(ok, row, NEG_INF) for row in rows]

    word_base = ((bi * NSA_KV_HEADS + g) * n_tiles + i) * words_per_tile
    list_ref[0] = 0

    def scan_word(w, n):
        word = words_ref[word_base + w]

        def scan_bits(n):
            for bit in range(WORD_BITS):
                c = w * WORD_BITS + bit
                list_ref[n] = c
                n = n + jnp.where(c < i, (word >> bit) & 1, 0)
            return n

        return lax.cond(word != 0, scan_bits, lambda n: n, n)

    n_listed = lax.fori_loop(0, (i + WORD_BITS - 1) // WORD_BITS, scan_word, 0)

    def selected_slot(c, ok, mask):
        at = pl.multiple_of(c * KEY_CHUNK, KEY_CHUNK)
        bias = selt_ref[0, 0, 0, pl.ds(c * BLOCKS_PER_CHUNK, BLOCKS_PER_CHUNK), :]
        aux = aux_base
        for blk in range(BLOCKS_PER_CHUNK):
            aux = jnp.where(aux_row == blk, jnp.concatenate([bias[blk:blk + 1]] * NSA_GROUP, axis=1), aux)
        return (ks_ref[0, 0, pl.ds(at, KEY_CHUNK), :], aux, vst_ref[0, 0, c],
                shifts((start - c * KEY_CHUNK).astype(F32), ok), mask)

    def listed_slot(idx):
        ok = idx < n_listed
        c = jnp.where(ok, list_ref[jnp.minimum(idx, jnp.maximum(n_listed - 1, 0))], 0)
        return selected_slot(c, ok, None)

    floor = jnp.full((1, qt), SOFTMAX_FLOOR, F32)

    n_back = WINDOW // KEY_CHUNK
    slots = []
    for j in range(n_back + 1):
        cs = start - WINDOW + j * KEY_CHUNK
        chunk = jnp.maximum(cs, 0) // KEY_CHUNK
        at = pl.multiple_of(chunk * KEY_CHUNK, KEY_CHUNK)
        mask = (query_in_tile < key_in_chunk) if j == 0 else (
            (key_in_chunk <= query_in_tile) if j == n_back else None)
        slots.append((kw_ref[0, 0, pl.ds(at, KEY_CHUNK), :], aux_base, vwt_ref[0, 0, chunk],
                      shifts(F32(WINDOW - j * KEY_CHUNK), cs >= 0), mask))
    scores, v_cat = scores_of(slots), values_of(slots)
    for r in range(NSA_GROUP):
        _, total, weighted = softmax_step(slots, scores, v_cat, r, floor)
        ow_ref[:, r * qt:(r + 1) * qt] = weighted * jnp.where(total > 0.0, 1.0 / total, 0.0)

    slots = [selected_slot(i, None, key_in_chunk <= query_in_tile)] + [
        listed_slot(j) for j in range(ATTN_BATCH - 1)]
    scores, v_cat = scores_of(slots), values_of(slots)
    for r in range(NSA_GROUP):
        m_ref[r], l_ref[r], acc_ref[:, r * qt:(r + 1) * qt] = softmax_step(slots, scores, v_cat, r, floor)

    def listed_batch(n):
        return [listed_slot(ATTN_BATCH - 1 + n * ATTN_BATCH + j) for j in range(ATTN_BATCH)]

    def issue(slots, dst_ref):
        for j, s in enumerate(scores_of(slots)):
            dst_ref[j] = s

    def consume(slots, src_ref):
        v_cat = values_of(slots)
        for r in range(NSA_GROUP):
            cols = slice(r * qt, (r + 1) * qt)
            m_old = m_ref[r]
            m_new, total, weighted = softmax_step(slots, src_ref, v_cat, r, m_old)
            alpha = jnp.exp(m_old - m_new)
            l_ref[r] = alpha * l_ref[r] + total
            acc_ref[:, cols] = alpha * acc_ref[:, cols] + weighted
            m_ref[r] = m_new

    issue(listed_batch(0), sa_ref)

    def batch_pair(it, carry):
        first, second, third = (listed_batch(2 * it + n) for n in range(3))
        issue(second, sb_ref)
        consume(first, sa_ref)
        issue(third, sa_ref)
        consume(second, sb_ref)
        return carry

    n_rest = jnp.maximum(n_listed - (ATTN_BATCH - 1), 0)
    n_batches = (n_rest + ATTN_BATCH - 1) // ATTN_BATCH
    lax.fori_loop(0, n_batches // 2, batch_pair, 0)

    @pl.when(n_batches % 2 == 1)
    def _():
        consume(listed_batch(n_batches - 1), sa_ref)

    def finalize(r):
        l = l_ref[r]
        return acc_ref[:, r * qt:(r + 1) * qt] * jnp.where(l > 0.0, 1.0 / l, 0.0)

    outs = []
    for r in range(NSA_GROUP):
        cols = slice(r * qt, (r + 1) * qt)
        col = (g * NSA_GROUP + r) * NSA_BRANCHES
        gate = lambda br: jax.nn.sigmoid(gt_ref[0, pl.ds(col + br, 1), :])
        outs.append(gate(0) * oct_ref[0, 0, 0, :, cols].astype(F32) + gate(1) * finalize(r)
                    + gate(2) * ow_ref[:, cols])
    out_ref[0] = jnp.concatenate(outs, axis=0).T.astype(out_ref.dtype)


def _nsa_slc(words, q_t, ks, vst, kw, vwt, selt, oct, gates_t, words_per_tile):
    b, _, t, kw_cols = ks.shape
    _, _, n_tiles, qrows, qcols = q_t.shape
    n_slc = selt.shape[3]
    n_chunks = vst.shape[2]
    gw = NSA_GROUP * HEAD_DIM
    once = dict(pipeline_mode=pl.Buffered(1))
    k_spec = pl.BlockSpec((1, 1, t, kw_cols), lambda bi, g, i, w: (bi, g, 0, 0), **once)
    vt_spec = pl.BlockSpec((1, 1, n_chunks, HEAD_DIM, KEY_CHUNK), lambda bi, g, i, w: (bi, g, 0, 0, 0), **once)
    tile5 = lambda rows, cols: pl.BlockSpec((1, 1, 1, rows, cols), lambda bi, g, i, w: (bi, g, i, 0, 0))
    grid_spec = pltpu.PrefetchScalarGridSpec(
        num_scalar_prefetch=1,
        grid=(b, NSA_KV_HEADS, n_tiles),
        in_specs=[
            tile5(qrows, qcols),
            k_spec, vt_spec, k_spec, vt_spec,
            tile5(n_slc, Q_TILE),
            tile5(HEAD_DIM, qcols),
            pl.BlockSpec((1, GATE_PAD, Q_TILE), lambda bi, g, i, w: (bi, 0, i)),
        ],
        out_specs=pl.BlockSpec((1, Q_TILE, gw), lambda bi, g, i, w: (bi, i, g)),
        scratch_shapes=[pltpu.VMEM((NSA_GROUP, 1, Q_TILE), F32), pltpu.VMEM((NSA_GROUP, 1, Q_TILE), F32),
                        pltpu.VMEM((HEAD_DIM, qcols), F32), pltpu.VMEM((HEAD_DIM, qcols), F32),
                        pltpu.VMEM((ATTN_BATCH, KEY_CHUNK, qcols), F32),
                        pltpu.VMEM((ATTN_BATCH, KEY_CHUNK, qcols), F32),
                        pltpu.SMEM((n_chunks,), jnp.int32)],
    )
    return pl.pallas_call(
        functools.partial(_nsa_slc_kernel, words_per_tile=words_per_tile),
        grid_spec=grid_spec,
        out_shape=jax.ShapeDtypeStruct((b, t, NSA_HEADS * HEAD_DIM), BF16),
        compiler_params=_params("parallel", "parallel", "parallel"),
        name="nsa_selected_window",
    )(words, q_t, ks, vst, kw, vwt, selt, oct, gates_t)


POOL_HALO = 16
CONV_HALO = 8


def _merge_kernel(h_ref, g_ref, nsa_ref, pool_ref, pool_halo_ref, conv_ref, conv_halo_ref,
                  wmg_ref, wnsa_ref, pool_bd_ref, pool_scale_ref, wpool_ref, convw_ref, wconv_ref, wo_ref,
                  g2_ref, wr_ref, br_ref, before_ref,
                  out_ref, xn_ref, comb_ref, count_ref, pool_ext, conv_ext, *, seq_len):
    i = pl.program_id(0)
    tm, d = h_ref.shape
    cw = pool_ref.shape[1]
    pos0 = (i * tm) % seq_len
    keep_halo = jnp.where(pos0 == 0, 0.0, 1.0)
    pos = pos0 + _iota((tm, 1), 0)

    u = pool_ref[...]
    pool_ext[0:POOL_HALO, :] = pool_halo_ref[...] * keep_halo
    pool_ext[POOL_HALO:, :] = u
    lane_group = _iota((1, cw), 1) // (cw // len(POOL_WINDOWS))
    total = u
    mean = jnp.zeros_like(u)
    done = 1
    for gi, win in enumerate(POOL_WINDOWS):
        for k in range(done, win):
            total = total + pool_ext[POOL_HALO - k:POOL_HALO - k + tm, :]
        done = win
        cnt = jnp.minimum(pos + 1, win).astype(F32)
        mean = jnp.where(lane_group == gi, total / cnt, mean)
    pooled = (mean - u).astype(BF16)
    mixed = _dot(pooled, pool_bd_ref[...]) * pool_scale_ref[...]
    y_pool = _dot(mixed.astype(BF16), wpool_ref[...])

    ch = conv_ref[:, 0:cw]
    cb = conv_ref[:, cw:2 * cw]
    cc = conv_ref[:, 2 * cw:3 * cw]
    conv_ext[0:CONV_HALO, :] = conv_halo_ref[:, 0:cw] * conv_halo_ref[:, 2 * cw:3 * cw] * keep_halo
    conv_ext[CONV_HALO:, :] = cc * ch
    y = jnp.zeros((tm, cw), F32)
    for k in range(CONV_K):
        off = CONV_HALO - (CONV_K - 1) + k
        y = y + convw_ref[k:k + 1, :] * conv_ext[off:off + tm, :]
    y_conv = _dot((cb * y).astype(BF16), wconv_ref[...])

    y_nsa = _dot(nsa_ref[...], wnsa_ref[...])

    h = h_ref[...]
    xn = _rms_norm(h, g_ref[...]).astype(BF16)
    merged = jnp.zeros((tm, d), F32)
    for br, y_br in enumerate((y_nsa, y_pool, y_conv)):
        mg = jax.nn.sigmoid(_dot(xn, wmg_ref[:, br * d:(br + 1) * d]))
        merged = merged + mg * y_br
    h_new = h + _dot(merged.astype(BF16), wo_ref[...])
    out_ref[...] = h_new
    _route_tile(h_new, g2_ref, wr_ref, br_ref, before_ref, xn_ref, comb_ref, count_ref)


def _merge(h, g, nsa, pool_u, conv, wmg, wnsa, pool_bd, pool_scale, wpool, convw, wconv, wo,
           g2, wr, br, before, seq_len):
    n, d = h.shape
    tm = TOKEN_TILE
    cw = pool_u.shape[1]
    row = lambda width: pl.BlockSpec((tm, width), lambda i: (i, 0))
    full = lambda a: pl.BlockSpec(a.shape, lambda i: (0,) * a.ndim)
    halo = lambda rows, width: pl.BlockSpec(
        (rows, width), lambda i: (jnp.maximum(i * (tm // rows) - 1, 0), 0))
    return pl.pallas_call(
        functools.partial(_merge_kernel, seq_len=seq_len),
        grid=(n // tm,),
        in_specs=[row(d), full(g), row(nsa.shape[1]), row(cw), halo(POOL_HALO, cw),
                  row(conv.shape[1]), halo(CONV_HALO, conv.shape[1]),
                  full(wmg), full(wnsa), full(pool_bd), full(pool_scale), full(wpool), full(convw),
                  full(wconv), full(wo), full(g2), full(wr), full(br), full(before)],
        out_specs=[row(d), row(d + ROUTER_PAD), row(ROUTER_PAD), pl.BlockSpec((8, ROUTER_PAD), lambda i: (0, 0))],
        out_shape=[jax.ShapeDtypeStruct((n, d), F32), jax.ShapeDtypeStruct((n, d + ROUTER_PAD), BF16),
                   jax.ShapeDtypeStruct((n, ROUTER_PAD), F32), jax.ShapeDtypeStruct((8, ROUTER_PAD), F32)],
        scratch_shapes=[pltpu.VMEM((tm + POOL_HALO, cw), F32), pltpu.VMEM((tm + CONV_HALO, cw), F32)],
        compiler_params=_params("arbitrary"),
        name="merge",
    )(h, g, nsa, pool_u, pool_u, conv, conv, wmg, wnsa, pool_bd, pool_scale, wpool, convw, wconv, wo,
      g2, wr, br, before)


def _route(logits):
    lane = _iota(logits.shape, 1)
    lane_f = lane.astype(F32)
    big = F32(1e9)
    is_group = lane < N_EXPERT_GROUPS
    gl = jnp.where(is_group, logits, NEG_INF)
    g_max = jnp.max(gl, axis=1, keepdims=True)
    g_sel = jnp.min(jnp.where(gl == g_max, lane_f, big), axis=1, keepdims=True)
    g_prob = 1.0 / jnp.sum(jnp.where(is_group, jnp.exp(gl - g_max), 0.0), axis=1, keepdims=True)
    lo = N_EXPERT_GROUPS + EXPERTS_PER_GROUP * g_sel
    in_group = (lane_f >= lo) & (lane_f < lo + EXPERTS_PER_GROUP)
    el = jnp.where(in_group, logits, NEG_INF)
    v1 = jnp.max(el, axis=1, keepdims=True)
    i1 = jnp.min(jnp.where((el == v1) & in_group, lane_f, big), axis=1, keepdims=True)
    el2 = jnp.where(lane_f == i1, NEG_INF, el)
    rest = in_group & (lane_f != i1)
    v2 = jnp.max(el2, axis=1, keepdims=True)
    i2 = jnp.min(jnp.where((el2 == v2) & rest, lane_f, big), axis=1, keepdims=True)
    e2 = jnp.exp(v2 - v1)
    w1 = g_prob / (1.0 + e2)
    w2 = g_prob * e2 / (1.0 + e2)
    return jnp.where(lane_f == i1, w1, 0.0) + jnp.where(lane_f == i2, w2, 0.0), g_sel


GROUP_LANE = N_EXPERT_GROUPS + N_EXPERTS


RANK_LANE = GROUP_LANE + 1


def _route_tile(h, g_ref, wr_ref, br_ref, before_ref, xn_ref, comb_ref, count_ref):
    @pl.when(pl.program_id(0) == 0)
    def _():
        count_ref[...] = jnp.zeros(count_ref.shape, F32)

    xn = _rms_norm(h, g_ref[...])
    xn_hi = xn.astype(BF16)
    xn_lo = (xn - xn_hi.astype(F32)).astype(BF16)
    logits = (_dot(xn_hi, wr_ref[0]) + (_dot(xn_hi, wr_ref[1]) + _dot(xn_lo, wr_ref[0]))) + br_ref[...]
    comb, g_sel = _route(logits)
    lane = _iota(comb.shape, 1)
    chose = jnp.where(lane.astype(F32) == g_sel, 1.0, 0.0)
    earlier = _dot(before_ref[...], chose.astype(BF16)) + count_ref[0:1, :]
    rank = jnp.sum(chose * earlier, axis=1, keepdims=True)
    count_ref[0:1, :] = count_ref[0:1, :] + jnp.sum(chose, axis=0, keepdims=True)
    comb_ref[...] = jnp.where(lane == GROUP_LANE, g_sel, jnp.where(lane == RANK_LANE, rank, comb))
    first = N_EXPERT_GROUPS + EXPERTS_PER_GROUP * g_sel
    local = jnp.zeros(comb.shape, F32)
    for e in range(EXPERTS_PER_GROUP):
        c_e = jnp.sum(jnp.where(lane.astype(F32) == first + e, comb, 0.0), axis=1, keepdims=True)
        local = jnp.where((lane == e) | (lane == EXPERTS_PER_GROUP + e), c_e, local)
    local_hi = local.astype(BF16)
    d = xn_hi.shape[1]
    xn_ref[:, :d] = xn_hi
    xn_ref[:, d:] = jnp.where(lane < EXPERTS_PER_GROUP, local_hi, (local - local_hi.astype(F32)).astype(BF16))


def _experts_kernel(tile_group_ref, n_active_ref, x_ref, w1_ref, w3_ref, w2_ref, out_ref, acc_ref):
    i = pl.program_id(0)

    @pl.when(i < n_active_ref[0])
    def _():
        d = out_ref.shape[1]
        x = x_ref[:, :d]
        comb = x_ref[:, d:].astype(F32)
        lane = _iota(comb.shape, 1)
        for e in range(EXPERTS_PER_GROUP):
            c_e = jnp.sum(jnp.where((lane == e) | (lane == EXPERTS_PER_GROUP + e), comb, 0.0),
                          axis=1, keepdims=True)
            a = (jax.nn.silu(_dot(x, w1_ref[0, 0, e].astype(BF16)))
                 * _dot(x, w3_ref[0, 0, e].astype(BF16))) * c_e
            y = _dot(a.astype(BF16), w2_ref[0, 0, e].astype(BF16))
            if e == 0:
                acc_ref[...] = y
            else:
                acc_ref[...] += y
        out_ref[...] = acc_ref[...].astype(out_ref.dtype)


def _experts(tile_group, n_active, x_sorted, w1, w3, w2, layer):
    ns = x_sorted.shape[0]
    d = w1.shape[3]
    tm = MOE_SORT_TILE
    group_w = lambda w: pl.BlockSpec((1, 1) + w.shape[2:], lambda i, tg, na: (layer, tg[i], 0, 0, 0),
                                     pipeline_mode=pl.Buffered(1))
    grid_spec = pltpu.PrefetchScalarGridSpec(
        num_scalar_prefetch=2,
        grid=(ns // tm,),
        in_specs=[
            pl.BlockSpec((tm, x_sorted.shape[1]), lambda i, tg, na: (i, 0)),
            group_w(w1), group_w(w3), group_w(w2),
        ],
        out_specs=pl.BlockSpec((tm, d), lambda i, tg, na: (i, 0)),
        scratch_shapes=[pltpu.VMEM((tm, d), F32)],
    )
    return pl.pallas_call(
        _experts_kernel,
        grid_spec=grid_spec,
        out_shape=jax.ShapeDtypeStruct((ns, d), BF16),
        compiler_params=_params("arbitrary"),
        name="experts",
    )(tile_group, n_active, x_sorted, w1, w3, w2)


def _residual_kernel(h_ref, y_ref, gf_ref, out_ref, *, final_norm):
    out = h_ref[...] + y_ref[...].astype(F32)
    out_ref[...] = _rms_norm(out, gf_ref[...]) if final_norm else out


def _residual(h, y, gf, final_norm):
    n, d = h.shape
    tm = MOE_TILE
    row = pl.BlockSpec((tm, d), lambda i: (i, 0))
    return pl.pallas_call(
        functools.partial(_residual_kernel, final_norm=final_norm),
        grid=(n // tm,),
        in_specs=[row, row, pl.BlockSpec(gf.shape, lambda i: (0, 0))],
        out_specs=row,
        out_shape=jax.ShapeDtypeStruct((n, d), F32),
        compiler_params=_params("parallel"),
        name="residual",
    )(h, y, gf)


def _group_sort_plan(group_id, rank, counts, tile):
    n = group_id.shape[0]
    n_slots = n + N_EXPERT_GROUPS * tile
    padded = (counts + tile - 1) // tile * tile
    ends = jnp.cumsum(padded)
    slot = (ends - padded)[group_id] + rank
    source = jnp.zeros((n_slots,), jnp.int32).at[slot].set(jnp.arange(n, dtype=jnp.int32))
    tile_start = jnp.arange(n_slots // tile, dtype=jnp.int32) * tile
    tile_group = jnp.minimum(jnp.searchsorted(ends, tile_start, side="right"), N_EXPERT_GROUPS - 1)
    return slot, source, tile_group.astype(jnp.int32), (ends[-1:] // tile).astype(jnp.int32)


def _moe(h, xn, comb, counts, w1, w3, w2, layer, gf, final_norm):
    as_int = lambda a: a.astype(jnp.int32)
    slot, source, tile_group, n_active = _group_sort_plan(
        as_int(comb[:, GROUP_LANE]), as_int(comb[:, RANK_LANE]), as_int(counts[0, :N_EXPERT_GROUPS]),
        MOE_SORT_TILE)
    grouped = lambda w: w.reshape((w.shape[0], N_EXPERT_GROUPS, EXPERTS_PER_GROUP) + w.shape[2:])
    rows = lambda a, idx: jnp.take(a, idx, axis=0, mode="clip")
    y_sorted = _experts(tile_group, n_active, rows(xn, source), grouped(w1), grouped(w3), grouped(w2), layer)
    return _residual(h, rows(y_sorted, slot), gf, final_norm)


def _selection_constants(seq_len):
    ncp = seq_len // CMP_STRIDE
    n_slc = seq_len // SLC_BLOCK
    ratio = SLC_BLOCK // CMP_STRIDE
    lead = CMP_LEN // CMP_STRIDE - 1
    c = np.arange(ncp)[:, None]
    j = np.arange(n_slc)[None, :]
    pool_m = ((c >= ratio * j - lead) & (c < ratio * j + ratio)).astype(np.float32)
    blocks_per_chunk = KEY_CHUNK // SLC_BLOCK
    n_chunks = seq_len // KEY_CHUNK
    pair_m = np.zeros((n_slc, LANES * ((n_chunks + LANES - 1) // LANES)), np.float32)
    pair_m[np.arange(n_slc), np.arange(n_slc) // blocks_per_chunk] = 1.0
    n_words = (n_chunks + WORD_BITS - 1) // WORD_BITS
    bits_m = np.zeros((pair_m.shape[1], LANES), np.float32)
    ch = np.arange(n_chunks)
    bits_m[ch, ch // WORD_BITS] = 2.0 ** (ch % WORD_BITS)
    key_aux = np.zeros((TOKEN_TILE, LANES), np.float32)
    in_chunk = np.arange(TOKEN_TILE) % KEY_CHUNK
    key_aux[np.arange(TOKEN_TILE), HEAD_DIM + in_chunk // SLC_BLOCK] = 1.0
    key_aux[:, HEAD_DIM + AUX_SLOPE] = in_chunk
    cmp_aux = np.zeros((ncp, LANES), np.float32)
    cmp_aux[:, HEAD_DIM] = np.arange(ncp) // CMP_AUX_SPLIT
    cmp_aux[:, HEAD_DIM + 1] = np.arange(ncp) % CMP_AUX_SPLIT
    as_bf16 = lambda a: jnp.asarray(a, BF16)
    return as_bf16(pool_m.T), as_bf16(pair_m), as_bf16(bits_m), jnp.asarray(key_aux), jnp.asarray(cmp_aux), n_words


def kernel(x, norm1_g, w_in, cmp_pe, cmp_w1, cmp_w2, w_nsa_proj, pool_w, pool_scale, w_pool_proj, conv_w,
           w_conv_proj, w_o, norm2_g, router_group_w, router_group_b, router_expert_w, router_expert_b,
           expert_w1, expert_w3, expert_w2, final_norm_g):
    b, t, d = x.shape
    n = b * t
    depth = w_in.shape[0]
    dq = NSA_HEADS * HEAD_DIM
    dkv = 6 * NSA_KV_HEADS * HEAD_DIM
    dgate = NSA_HEADS * NSA_BRANCHES
    cw = d // 4
    assert t % TOKEN_TILE == 0 and n % MOE_TILE == 0 and t % Q_TILE == 0
    n_slc = t // SLC_BLOCK
    n_sel = min(SLC_TOPN, n_slc)
    n_chunks16 = t // CMP_STRIDE
    kvw = NSA_KV_HEADS * HEAD_DIM
    poolt_m, pair_m, bits_m, key_aux, cmp_aux, n_words = _selection_constants(t)
    assert Q_TILE == KEY_CHUNK and AUX_SLOPE < AUX_COLS and KEY_CHUNK <= 256
    assert n_chunks16 <= 256 * CMP_AUX_SPLIT and n_chunks16 % min(CMP_ROWS_STEP, n_chunks16) == 0
    assert n_words <= LANES and pair_m.shape[1] == LANES

    before = jnp.asarray(np.tril(np.ones((TOKEN_TILE, TOKEN_TILE), np.float32), -1), BF16)
    h = x.reshape(n, d)
    for l in range(depth):
        wl = w_in[l]
        o_gate = dq + dkv
        o_pool = o_gate + dgate
        o_merge = o_pool + cw + 3 * cw
        kv_cols = lambda kind: wl[:, dq + kind * kvw:dq + (kind + 1) * kvw]
        wq_t = (wl[:, :dq] * (HEAD_DIM ** -0.5)).T.astype(BF16)
        wv_t = jnp.concatenate([kv_cols(3), kv_cols(5)], axis=1).T.astype(BF16)
        wg_t = jnp.pad(wl[:, o_gate:o_pool], ((0, 0), (0, GATE_PAD - dgate))).T.astype(BF16)
        no_aux = jnp.zeros((d, LANES - HEAD_DIM), F32)
        wk = jnp.concatenate([piece for kind in (2, 4) for gi in range(NSA_KV_HEADS)
                              for piece in (kv_cols(kind)[:, gi * HEAD_DIM:(gi + 1) * HEAD_DIM], no_aux)],
                             axis=1).astype(BF16)
        wn = jnp.concatenate([kv_cols(0), kv_cols(1), wl[:, o_pool:o_merge]], axis=1).astype(BF16)
        wmg = wl[:, o_merge:].astype(BF16)
        pool_bd = jax.scipy.linalg.block_diag(*[pool_w[l, gi] for gi in range(pool_w.shape[1])]).astype(BF16)
        convw = jnp.pad(conv_w[l], ((0, 8 - CONV_K), (0, 0)))
        wr = jnp.pad(jnp.concatenate([router_group_w[l], router_expert_w[l]], axis=1),
                     ((0, 0), (0, ROUTER_PAD - N_EXPERT_GROUPS - N_EXPERTS)))
        wr_hi = wr.astype(BF16)
        wr = jnp.stack([wr_hi, (wr - wr_hi.astype(F32)).astype(BF16)])
        br = jnp.pad(jnp.concatenate([router_group_b[l], router_expert_b[l]]),
                     (0, ROUTER_PAD - N_EXPERT_GROUPS - N_EXPERTS))[None, :]
        pe = jnp.broadcast_to(cmp_pe[l].reshape(2, 1, CMP_LEN * HEAD_DIM), (2, 8, CMP_LEN * HEAD_DIM)).astype(BF16)
        halves = CMP_LEN // CMP_STRIDE
        w1_bd = jnp.einsum("khldc,gq->khlgdqc",
                           cmp_w1[l].reshape(2, halves, CMP_STRIDE, HEAD_DIM, CMP_HIDDEN),
                           jnp.eye(NSA_KV_HEADS, dtype=F32))
        w1_bd = w1_bd.reshape(2, halves, CMP_STRIDE * kvw, NSA_KV_HEADS * CMP_HIDDEN).astype(BF16)
        w2_k = jnp.pad(cmp_w2[l, 0], ((0, 0), (0, LANES - HEAD_DIM))).astype(BF16)
        w2_vt = cmp_w2[l, 1].T.astype(BF16)

        q_t, vst, vwt, gates_t, ks, kw, cmp_src, pool_u, conv = _inproj(
            h, norm1_g[l][None, :], wq_t, wv_t, wg_t, wk, wn, key_aux, b)
        kc_aux, vc_t = _compress(cmp_src, pe, cmp_w1[l].astype(BF16), w1_bd, w2_k, w2_vt, cmp_aux)
        oc_t, sel_t, words = _nsa_cmp(q_t, kc_aux, vc_t, poolt_m, pair_m, bits_m, n_sel)
        words = words[:, :, :, 0, :n_words].reshape(-1)
        nsa = _nsa_slc(words, q_t, ks, vst, kw, vwt, sel_t, oc_t, gates_t, n_words)
        h, xn, comb, counts = _merge(
            h, norm1_g[l][None, :], nsa.reshape(n, dq), pool_u, conv, wmg,
            w_nsa_proj[l].astype(BF16), pool_bd, pool_scale[l][None, :], w_pool_proj[l].astype(BF16),
            convw, w_conv_proj[l].astype(BF16), w_o[l].astype(BF16), norm2_g[l][None, :], wr, br, before, t)
        h = _moe(h, xn, comb, counts, expert_w1, expert_w3, expert_w2, l,
                 final_norm_g[None, :], final_norm=(l == depth - 1))
    return h.reshape(b, t, d)
```

```python
import functools

import jax
import jax.numpy as jnp
import numpy as np
from jax import lax
from jax.experimental import pallas as pl
from jax.experimental.pallas import tpu as pltpu

F32 = jnp.float32
BF16 = jnp.bfloat16

HEAD_DIM = 64
NSA_HEADS = 8
NSA_KV_HEADS = 2
NSA_GROUP = NSA_HEADS // NSA_KV_HEADS
CMP_LEN = 32
CMP_STRIDE = 16
CMP_HIDDEN = 4 * HEAD_DIM
SLC_BLOCK = 64
SLC_TOPN = 16
WINDOW = 512
NSA_BRANCHES = 3
POOL_WINDOWS = (2, 4, 8, 16)
CONV_K = 3
N_BRANCHES = 3
N_EXPERT_GROUPS = 4
EXPERTS_PER_GROUP = 8
N_EXPERTS = N_EXPERT_GROUPS * EXPERTS_PER_GROUP
RMS_EPS = 1e-6
NEG_INF = -1e30
FORCE_SCORE = 1e30
ALIBI_SLOPES = tuple(float(2.0 ** (-8.0 * (h + 1) / NSA_HEADS)) for h in range(NSA_HEADS))

LANES = 128
VMEM_LIMIT = 56 * 1024 * 1024
TOKEN_TILE = 512
MOE_TILE = 1024
MOE_SORT_TILE = 512
Q_TILE = 128
KEY_CHUNK = 128
GATE_PAD = LANES
ROUTER_PAD = LANES
WORD_BITS = 16


def _params(*semantics):
    return pltpu.CompilerParams(dimension_semantics=semantics, vmem_limit_bytes=VMEM_LIMIT)


def _dot(a, b):
    return jnp.dot(a, b, preferred_element_type=F32)


def _dot_nt(a, b):
    return lax.dot_general(a, b, (((1,), (1,)), ((), ())), preferred_element_type=F32)


def _rms_norm(x, g):
    y = x * lax.rsqrt(jnp.mean(x * x, axis=-1, keepdims=True) + RMS_EPS)
    return y * g


def _iota(shape, dim):
    return lax.broadcasted_iota(jnp.int32, shape, dim)


def _inproj_kernel(x_ref, g_ref, wq_ref, wv_ref, wg_ref, wk_ref, wn_ref, kaux_ref,
                   q_ref, vs_ref, vw_ref, gate_ref, ks_ref, kw_ref, cmp_ref, pool_ref, conv_ref):
    xn = _rms_norm(x_ref[...], g_ref[...]).astype(BF16)
    sub_tiles = x_ref.shape[0] // Q_TILE
    q_t = _dot_nt(wq_ref[...], xn)
    for g in range(NSA_KV_HEADS):
        for j in range(sub_tiles):
            for r in range(NSA_GROUP):
                head = g * NSA_GROUP + r
                q_ref[0, g, j, :, r * Q_TILE:(r + 1) * Q_TILE] = q_t[
                    head * HEAD_DIM:(head + 1) * HEAD_DIM, j * Q_TILE:(j + 1) * Q_TILE].astype(BF16)
    v_t = _dot_nt(wv_ref[...], xn)
    k = _dot(xn, wk_ref[...])
    for branch, (v_ref, k_ref) in enumerate(((vs_ref, ks_ref), (vw_ref, kw_ref))):
        for g in range(NSA_KV_HEADS):
            slab = branch * NSA_KV_HEADS + g
            for j in range(sub_tiles):
                v_ref[0, g, j] = v_t[slab * HEAD_DIM:(slab + 1) * HEAD_DIM,
                                     j * KEY_CHUNK:(j + 1) * KEY_CHUNK].astype(BF16)
            k_ref[0, g] = (k[:, slab * LANES:(slab + 1) * LANES] + kaux_ref[...]).astype(BF16)
    gate_ref[0] = _dot_nt(wg_ref[...], xn)
    col = 0
    for ref in (cmp_ref, pool_ref, conv_ref):
        width = ref.shape[-1]
        ref[...] = _dot(xn, wn_ref[:, col:col + width]).reshape(ref.shape)
        col += width


def _inproj(h, g, wq_t, wv_t, wg_t, wk, wn, kaux, batch):
    n, d = h.shape
    t = n // batch
    tm = TOKEN_TILE
    steps = t // tm
    sub = tm // Q_TILE
    cw = d // 4
    gq = NSA_GROUP * Q_TILE
    full = lambda a: pl.BlockSpec(a.shape, lambda i: (0,) * a.ndim)
    row = lambda width: pl.BlockSpec((tm, width), lambda i: (i, 0))
    tiles = lambda rows, cols: pl.BlockSpec((1, NSA_KV_HEADS, sub, rows, cols),
                                            lambda i: (i // steps, 0, i % steps, 0, 0))
    keys = pl.BlockSpec((1, NSA_KV_HEADS, tm, LANES), lambda i: (i // steps, 0, i % steps, 0))
    sds = jax.ShapeDtypeStruct
    v_shape = sds((batch, NSA_KV_HEADS, t // KEY_CHUNK, HEAD_DIM, KEY_CHUNK), BF16)
    k_shape = sds((batch, NSA_KV_HEADS, t, LANES), BF16)
    return pl.pallas_call(
        _inproj_kernel,
        grid=(n // tm,),
        in_specs=[row(d), full(g), full(wq_t), full(wv_t), full(wg_t), full(wk), full(wn), full(kaux)],
        out_specs=[tiles(HEAD_DIM, gq), tiles(HEAD_DIM, KEY_CHUNK), tiles(HEAD_DIM, KEY_CHUNK),
                   pl.BlockSpec((1, GATE_PAD, tm), lambda i: (i // steps, 0, i % steps)),
                   keys, keys,
                   pl.BlockSpec((1, tm, 2 * NSA_KV_HEADS * HEAD_DIM), lambda i: (i // steps, i % steps, 0)),
                   row(cw), row(3 * cw)],
        out_shape=[sds((batch, NSA_KV_HEADS, t // Q_TILE, HEAD_DIM, gq), BF16), v_shape, v_shape,
                   sds((batch, GATE_PAD, t), F32), k_shape, k_shape,
                   sds((batch, t, 2 * NSA_KV_HEADS * HEAD_DIM), F32), sds((n, cw), F32), sds((n, 3 * cw), F32)],
        compiler_params=_params("parallel"),
        name="inproj",
    )(h, g, wq_t, wv_t, wg_t, wk, wn, kaux)


def _gelu_tanh(x):
    return 0.5 * x * (1.0 + jnp.tanh(0.7978845608028654 * (x + 0.044715 * x * x * x)))


def _compress_kernel(src_ref, pe_ref, w1_ref, w1bd_ref, w2k_ref, w2vt_ref, caux_ref, kc_ref, vct_ref):
    kind = pl.program_id(1)
    ncp = kc_ref.shape[2]
    hidden = w1_ref.shape[2]
    pieces = [src_ref[0, pl.ds(l, ncp, stride=CMP_STRIDE), :].astype(BF16) for l in range(CMP_STRIDE)]
    chunk = jnp.concatenate(pieces, axis=1)
    first = _dot(chunk, w1bd_ref[0, 0])
    second = _dot(chunk, w1bd_ref[0, 1])
    bias = _dot(pe_ref[0], w1_ref[0])[0:1, :]
    hid = first + pltpu.roll(second, ncp - 1, 0) + jnp.concatenate([bias] * NSA_KV_HEADS, axis=1)
    row = _iota((ncp, 1), 0)
    act = jnp.where(row < ncp - 1, _gelu_tanh(hid), 0.0).astype(BF16)
    for g in range(NSA_KV_HEADS):
        act_g = act[:, g * hidden:(g + 1) * hidden]

        @pl.when(kind == 0)
        def _():
            kc_ref[0, g] = (_dot(act_g, w2k_ref[...]) + caux_ref[...]).astype(BF16)

        @pl.when(kind == 1)
        def _():
            vct_ref[0, g * HEAD_DIM:(g + 1) * HEAD_DIM, :] = _dot_nt(w2vt_ref[...], act_g).astype(BF16)


def _compress(src, pe, w1, w1bd, w2k, w2vt, caux):
    b, t, _ = src.shape
    ncp = t // CMP_STRIDE
    gd = NSA_KV_HEADS * HEAD_DIM
    full = lambda a: pl.BlockSpec(a.shape, lambda bi, k: (0,) * a.ndim)
    per_kind = lambda a: pl.BlockSpec((1,) + a.shape[1:], lambda bi, k: (k,) + (0,) * (a.ndim - 1))
    return pl.pallas_call(
        _compress_kernel,
        grid=(b, 2),
        in_specs=[pl.BlockSpec((1, t, gd), lambda bi, k: (bi, 0, k)),
                  per_kind(pe), per_kind(w1), per_kind(w1bd), full(w2k), full(w2vt), full(caux)],
        out_specs=[pl.BlockSpec((1, NSA_KV_HEADS, ncp, LANES), lambda bi, k: (bi, 0, 0, 0)),
                   pl.BlockSpec((1, gd, ncp), lambda bi, k: (bi, 0, 0))],
        out_shape=[jax.ShapeDtypeStruct((b, NSA_KV_HEADS, ncp, LANES), BF16),
                   jax.ShapeDtypeStruct((b, gd, ncp), BF16)],
        compiler_params=_params("parallel", "arbitrary"),
        name="compress",
    )(src, pe, w1, w1bd, w2k, w2vt, caux)


SOFTMAX_FLOOR = -1e29
TAKEN = -3e38
ATTN_BATCH = 4
BLOCKS_PER_CHUNK = KEY_CHUNK // SLC_BLOCK
AUX_COLS = 16
AUX_SLOPE = BLOCKS_PER_CHUNK
ONES_ROWS = 16
CMP_ROWS_STEP = 128
CMP_AUX_SPLIT = 128


def _slope(g, r):
    if isinstance(g, int):
        return jnp.float32(ALIBI_SLOPES[g * NSA_GROUP + r])
    s = jnp.float32(ALIBI_SLOPES[r])
    for gi in range(1, NSA_KV_HEADS):
        s = jnp.where(g == gi, jnp.float32(ALIBI_SLOPES[gi * NSA_GROUP + r]), s)
    return s


def _nsa_cmp_kernel(qt_ref, kc_ref, vct_ref, poolt_ref, pair_ref, bits_ref, oct_ref, selt_ref, words_ref,
                    *, n_sel):
    i = pl.program_id(1)
    qt = selt_ref.shape[4]
    ncp = kc_ref.shape[2]
    n_slc = poolt_ref.shape[0]
    gq = qt_ref.shape[4]
    start = i * qt
    t = start + _iota((1, qt), 1)
    aux_row = _iota((AUX_COLS, gq), 0)
    col_head = _iota((1, gq), 1) // qt
    pad_rows = jnp.zeros((kc_ref.shape[3] - qt_ref.shape[3] - AUX_COLS, gq), BF16)

    def weights(g):
        slope_cols = jnp.zeros((1, gq), F32)
        for r in range(NSA_GROUP):
            slope_cols = jnp.where(col_head == r, _slope(g, r), slope_cols)
        aux = jnp.where(aux_row == 0, slope_cols * (CMP_STRIDE * CMP_AUX_SPLIT),
                        jnp.where(aux_row == 1, slope_cols * CMP_STRIDE, 0.0))
        return jnp.concatenate([qt_ref[0, g, 0], aux.astype(BF16), pad_rows], axis=0)

    def importance(g, nr, nb):
        s = _dot(kc_ref[0, g, :nr, :], weights(g))
        edge = min(nr, 2 * CMP_ROWS_STEP)
        cmp_end = ((nr - edge) + _iota((edge, 1), 0)) * CMP_STRIDE + (CMP_LEN - 1)
        visible = cmp_end <= t
        vct = vct_ref[0, g * HEAD_DIM:(g + 1) * HEAD_DIM, :nr]
        psum = jnp.zeros((nr, qt), F32)
        for r in range(NSA_GROUP):
            cols = slice(r * qt, (r + 1) * qt)
            sr = s[:, cols]
            tail = jnp.where(visible, sr[nr - edge:], NEG_INF)
            sr = tail if edge == nr else jnp.concatenate([sr[:nr - edge], tail], axis=0)
            m = jnp.maximum(jnp.max(sr, axis=0, keepdims=True), SOFTMAX_FLOOR)
            e = jnp.exp(sr - m)
            l = jnp.sum(e, axis=0, keepdims=True)
            inv = jnp.where(l > 0.0, 1.0 / l, 0.0)
            oct_ref[0, g, 0, :, cols] = (_dot(vct, e.astype(BF16)) * inv).astype(BF16)
            psum = psum + e * inv
        return _dot(poolt_ref[:nb, :nr], psum.astype(BF16))

    def visible_prefix(nr):
        nb = min(n_slc, nr * CMP_STRIDE // SLC_BLOCK)
        imp = jnp.concatenate([importance(g, nr, nb) for g in range(NSA_KV_HEADS)], axis=1)
        blk = _iota((nb, 1), 0)
        cur = jnp.concatenate([t // SLC_BLOCK] * NSA_KV_HEADS, axis=1)
        forced = (blk == 0) | (blk == cur) | (blk == cur - 1)
        score = jnp.where(forced, TAKEN, jnp.where(blk <= cur, imp, NEG_INF))
        n_forced = 1 + jnp.where(cur >= 1, 1, 0) + jnp.where(cur >= 2, 1, 0)
        blk_f = blk.astype(F32)

        def take_one(score, active):
            m = jnp.max(score, axis=0, keepdims=True)
            first = jnp.min(jnp.where(score == m, blk_f, F32(1e9)), axis=0, keepdims=True)
            hit = (blk_f == first) if active is None else ((blk_f == first) & active)
            return jnp.where(hit, TAKEN, score)

        common_rounds = max(n_sel - 3, 0)
        for _ in range(common_rounds):
            score = take_one(score, None)

        def early_rounds(score):
            for k in range(common_rounds, n_sel - 1):
                score = take_one(score, n_sel - n_forced > k)
            return score

        score = lax.cond(start < 2 * SLC_BLOCK, early_rounds, lambda sc: sc, score)
        for g in range(NSA_KV_HEADS):
            sel_g = score[:, g * qt:(g + 1) * qt] == TAKEN
            selt_ref[0, g, 0, :nb, :] = jnp.where(sel_g, 0.0, NEG_INF)
            if nb < n_slc:
                selt_ref[0, g, 0, nb:, :] = jnp.full((n_slc - nb, qt), NEG_INF, F32)
            count = _dot_nt(jnp.ones((8, qt), BF16), jnp.where(sel_g, 1.0, 0.0).astype(BF16))
            used = jnp.where(count > 0.0, 1.0, 0.0).astype(BF16)
            chunk_used = jnp.where(_dot(used, pair_ref[:nb, :]) > 0.0, 1.0, 0.0).astype(BF16)
            words_ref[0, g, 0] = _dot(chunk_used, bits_ref[...]).astype(jnp.int32)

    step = min(CMP_ROWS_STEP, ncp)
    rows_needed = jnp.minimum((start + qt - CMP_LEN) // CMP_STRIDE + 1, ncp)
    n_steps = (rows_needed + step - 1) // step
    for k in range(ncp // step):
        pl.when(n_steps == k + 1)(functools.partial(visible_prefix, (k + 1) * step))


def _nsa_cmp(q_t, kc, vct, poolt_m, pair_m, bits_m, n_sel):
    b, ng, n_tiles, qrows, qcols = q_t.shape
    ncp = kc.shape[2]
    n_slc = poolt_m.shape[0]
    const = lambda shape: pl.BlockSpec(shape, lambda bi, i: (0,) * len(shape))
    tile5 = lambda rows, cols: pl.BlockSpec((1, ng, 1, rows, cols), lambda bi, i: (bi, 0, i, 0, 0))
    return pl.pallas_call(
        functools.partial(_nsa_cmp_kernel, n_sel=n_sel),
        grid=(b, n_tiles),
        in_specs=[
            tile5(qrows, qcols),
            pl.BlockSpec((1, ng, ncp, kc.shape[3]), lambda bi, i: (bi, 0, 0, 0)),
            pl.BlockSpec((1, ng * HEAD_DIM, ncp), lambda bi, i: (bi, 0, 0)),
            const(poolt_m.shape), const(pair_m.shape), const(bits_m.shape),
        ],
        out_specs=[tile5(HEAD_DIM, qcols), tile5(n_slc, Q_TILE), tile5(8, LANES)],
        out_shape=[
            jax.ShapeDtypeStruct((b, ng, n_tiles, HEAD_DIM, qcols), BF16),
            jax.ShapeDtypeStruct((b, ng, n_tiles, n_slc, Q_TILE), F32),
            jax.ShapeDtypeStruct((b, ng, n_tiles, 8, LANES), jnp.int32),
        ],
        compiler_params=_params("parallel", "parallel"),
        name="nsa_compressed",
    )(q_t, kc, vct, poolt_m, pair_m, bits_m)


def _nsa_slc_kernel(words_ref, qt_ref, ks_ref, vst_ref, kw_ref, vwt_ref, selt_ref, oct_ref, gt_ref,
                    out_ref, m_ref, l_ref, acc_ref, ow_ref, sa_ref, sb_ref, sw_ref, list_ref, *, words_per_tile):
    bi = pl.program_id(0)
    g = pl.program_id(1)
    i = pl.program_id(2)
    n_tiles = pl.num_programs(2)
    qt = out_ref.shape[1]
    q_rows = qt_ref[0, 0, 0]
    gq = q_rows.shape[1]
    start = i * qt
    lane_f = _iota((1, qt), 1).astype(F32)
    key_in_chunk = _iota((KEY_CHUNK, qt), 0)
    query_in_tile = _iota((KEY_CHUNK, qt), 1)

    aux_row = _iota((AUX_COLS, gq), 0)
    col_head = _iota((1, gq), 1) // qt
    slope_cols = jnp.zeros((1, gq), F32)
    for r in range(NSA_GROUP):
        slope_cols = jnp.where(col_head == r, _slope(g, r), slope_cols)
    aux_base = jnp.where(aux_row == AUX_SLOPE, slope_cols, 0.0)
    pad_rows = jnp.zeros((ks_ref.shape[3] - q_rows.shape[0] - AUX_COLS, gq), BF16)

    def scores_of(slots):
        return [_dot(k, jnp.concatenate([q_rows, aux.astype(BF16), pad_rows], axis=0))
                for k, aux, _, _, _ in slots]

    def values_of(slots):
        values = jnp.concatenate([v for _, _, v, _, _ in slots], axis=1)
        return jnp.concatenate([values, jnp.ones((ONES_ROWS, values.shape[1]), BF16)], axis=0)

    def softmax_step(slots, scores, v_cat, r, m_old):
        cols = slice(r * qt, (r + 1) * qt)
        srs, tops = [], []
        for j, (_, _, _, shift, mask) in enumerate(slots):
            sr = scores[j, :, cols] if hasattr(scores, "at") else scores[j][:, cols]
            sr = sr if mask is None else jnp.where(mask, sr, NEG_INF)
            srs.append(sr)
            tops.append(jnp.max(sr, axis=0, keepdims=True) + shift[r])
        m_new = functools.reduce(jnp.maximum, tops, m_old)
        ps = [jnp.exp((sr - (m_new - slot[3][r])).astype(BF16)) for slot, sr in zip(slots, srs)]
        weighted = _dot(v_cat, jnp.concatenate(ps, axis=0))
        return m_new, weighted[HEAD_DIM:HEAD_DIM + 1], weighted[:HEAD_DIM]

    def shifts(dist0, ok):
        rows = [-_slope(g, r) * (dist0 + lane_f) for r in range(NSA_GROUP)]
        return rows if ok is None else [jnp.where(ok, row, NEG_INF) for row in rows]

    word_base = ((bi * NSA_KV_HEADS + g) * n_tiles + i) * words_per_tile
    list_ref[0] = 0

    def scan_word(w, n):
        word = words_ref[word_base + w]

        def scan_bits(n):
            for bit in range(WORD_BITS):
                c = w * WORD_BITS + bit
                list_ref[n] = c
                n = n + jnp.where(c < i, (word >> bit) & 1, 0)
            return n

        return lax.cond(word != 0, scan_bits, lambda n: n, n)

    n_listed = lax.fori_loop(0, (i + WORD_BITS - 1) // WORD_BITS, scan_word, 0)

    def selected_slot(c, ok, mask):
        at = pl.multiple_of(c * KEY_CHUNK, KEY_CHUNK)
        bias = selt_ref[0, 0, 0, pl.ds(c * BLOCKS_PER_CHUNK, BLOCKS_PER_CHUNK), :]
        aux = aux_base
        for blk in range(BLOCKS_PER_CHUNK):
            aux = jnp.where(aux_row == blk, jnp.concatenate([bias[blk:blk + 1]] * NSA_GROUP, axis=1), aux)
        return (ks_ref[0, 0, pl.ds(at, KEY_CHUNK), :], aux, vst_ref[0, 0, c],
                shifts((start - c * KEY_CHUNK).astype(F32), ok), mask)

    def listed_slot(idx):
        ok = idx < n_listed
        c = jnp.where(ok, list_ref[jnp.minimum(idx, jnp.maximum(n_listed - 1, 0))], 0)
        return selected_slot(c, ok, None)

    floor = jnp.full((1, qt), SOFTMAX_FLOOR, F32)

    n_back = WINDOW // KEY_CHUNK
    slots = []
    for j in range(n_back + 1):
        cs = start - WINDOW + j * KEY_CHUNK
        chunk = jnp.maximum(cs, 0) // KEY_CHUNK
        at = pl.multiple_of(chunk * KEY_CHUNK, KEY_CHUNK)
        mask = (query_in_tile < key_in_chunk) if j == 0 else (
            (key_in_chunk <= query_in_tile) if j == n_back else None)
        slots.append((kw_ref[0, 0, pl.ds(at, KEY_CHUNK), :], aux_base, vwt_ref[0, 0, chunk],
                      shifts(F32(WINDOW - j * KEY_CHUNK), cs >= 0), mask))
    window_slots = slots
    first_slots = [selected_slot(i, None, key_in_chunk <= query_in_tile)] + [
        listed_slot(j) for j in range(ATTN_BATCH - 1)]

    def listed_batch(n):
        return [listed_slot(ATTN_BATCH - 1 + n * ATTN_BATCH + j) for j in range(ATTN_BATCH)]

    def issue(slots, dst_ref):
        for j, s in enumerate(scores_of(slots)):
            dst_ref[j] = s

    issue(window_slots, sw_ref)
    issue(first_slots, sb_ref)
    v_cat = values_of(window_slots)
    for r in range(NSA_GROUP):
        _, total, weighted = softmax_step(window_slots, sw_ref, v_cat, r, floor)
        ow_ref[:, r * qt:(r + 1) * qt] = weighted * jnp.where(total > 0.0, 1.0 / total, 0.0)

    def consume(slots, src_ref):
        v_cat = values_of(slots)
        for r in range(NSA_GROUP):
            cols = slice(r * qt, (r + 1) * qt)
            m_old = m_ref[r]
            m_new, total, weighted = softmax_step(slots, src_ref, v_cat, r, m_old)
            alpha = jnp.exp(m_old - m_new)
            l_ref[r] = alpha * l_ref[r] + total
            acc_ref[:, cols] = alpha * acc_ref[:, cols] + weighted
            m_ref[r] = m_new

    issue(listed_batch(0), sa_ref)
    v_cat = values_of(first_slots)
    for r in range(NSA_GROUP):
        m_ref[r], l_ref[r], acc_ref[:, r * qt:(r + 1) * qt] = softmax_step(first_slots, sb_ref, v_cat, r, floor)

    def batch_pair(it, carry):
        first, second, third = (listed_batch(2 * it + n) for n in range(3))
        issue(second, sb_ref)
        consume(first, sa_ref)
        issue(third, sa_ref)
        consume(second, sb_ref)
        return carry

    n_rest = jnp.maximum(n_listed - (ATTN_BATCH - 1), 0)
    n_batches = (n_rest + ATTN_BATCH - 1) // ATTN_BATCH
    lax.fori_loop(0, n_batches // 2, batch_pair, 0)

    @pl.when(n_batches % 2 == 1)
    def _():
        consume(listed_batch(n_batches - 1), sa_ref)

    def finalize(r):
        l = l_ref[r]
        return acc_ref[:, r * qt:(r + 1) * qt] * jnp.where(l > 0.0, 1.0 / l, 0.0)

    outs = []
    for r in range(NSA_GROUP):
        cols = slice(r * qt, (r + 1) * qt)
        col = (g * NSA_GROUP + r) * NSA_BRANCHES
        gate = lambda br: jax.nn.sigmoid(gt_ref[0, pl.ds(col + br, 1), :])
        outs.append(gate(0) * oct_ref[0, 0, 0, :, cols].astype(F32) + gate(1) * finalize(r)
                    + gate(2) * ow_ref[:, cols])
    out_ref[0] = jnp.concatenate(outs, axis=0).T.astype(out_ref.dtype)


def _nsa_slc(words, q_t, ks, vst, kw, vwt, selt, oct, gates_t, words_per_tile):
    b, _, t, kw_cols = ks.shape
    _, _, n_tiles, qrows, qcols = q_t.shape
    n_slc = selt.shape[3]
    n_chunks = vst.shape[2]
    gw = NSA_GROUP * HEAD_DIM
    once = dict(pipeline_mode=pl.Buffered(1))
    k_spec = pl.BlockSpec((1, 1, t, kw_cols), lambda bi, g, i, w: (bi, g, 0, 0), **once)
    vt_spec = pl.BlockSpec((1, 1, n_chunks, HEAD_DIM, KEY_CHUNK), lambda bi, g, i, w: (bi, g, 0, 0, 0), **once)
    tile5 = lambda rows, cols: pl.BlockSpec((1, 1, 1, rows, cols), lambda bi, g, i, w: (bi, g, i, 0, 0))
    grid_spec = pltpu.PrefetchScalarGridSpec(
        num_scalar_prefetch=1,
        grid=(b, NSA_KV_HEADS, n_tiles),
        in_specs=[
            tile5(qrows, qcols),
            k_spec, vt_spec, k_spec, vt_spec,
            tile5(n_slc, Q_TILE),
            tile5(HEAD_DIM, qcols),
            pl.BlockSpec((1, GATE_PAD, Q_TILE), lambda bi, g, i, w: (bi, 0, i)),
        ],
        out_specs=pl.BlockSpec((1, Q_TILE, gw), lambda bi, g, i, w: (bi, i, g)),
        scratch_shapes=[pltpu.VMEM((NSA_GROUP, 1, Q_TILE), F32), pltpu.VMEM((NSA_GROUP, 1, Q_TILE), F32),
                        pltpu.VMEM((HEAD_DIM, qcols), F32), pltpu.VMEM((HEAD_DIM, qcols), F32),
                        pltpu.VMEM((ATTN_BATCH, KEY_CHUNK, qcols), F32),
                        pltpu.VMEM((ATTN_BATCH, KEY_CHUNK, qcols), F32),
                        pltpu.VMEM((WINDOW // KEY_CHUNK + 1, KEY_CHUNK, qcols), F32),
                        pltpu.SMEM((n_chunks,), jnp.int32)],
    )
    return pl.pallas_call(
        functools.partial(_nsa_slc_kernel, words_per_tile=words_per_tile),
        grid_spec=grid_spec,
        out_shape=jax.ShapeDtypeStruct((b, t, NSA_HEADS * HEAD_DIM), BF16),
        compiler_params=_params("parallel", "parallel", "parallel"),
        name="nsa_selected_window",
    )(words, q_t, ks, vst, kw, vwt, selt, oct, gates_t)


POOL_HALO = 16
CONV_HALO = 8


def _merge_kernel(h_ref, g_ref, nsa_ref, pool_ref, pool_halo_ref, conv_ref, conv_halo_ref,
                  wmg_ref, wnsa_ref, pool_bd_ref, pool_scale_ref, wpool_ref, convw_ref, wconv_ref, wo_ref,
                  g2_ref, wr_ref, br_ref, before_ref,
                  out_ref, xn_ref, comb_ref, count_ref, pool_ext, conv_ext, *, seq_len):
    i = pl.program_id(0)
    tm, d = h_ref.shape
    cw = pool_ref.shape[1]
    pos0 = (i * tm) % seq_len
    keep_halo = jnp.where(pos0 == 0, 0.0, 1.0)
    pos = pos0 + _iota((tm, 1), 0)

    u = pool_ref[...]
    pool_ext[0:POOL_HALO, :] = pool_halo_ref[...] * keep_halo
    pool_ext[POOL_HALO:, :] = u
    lane_group = _iota((1, cw), 1) // (cw // len(POOL_WINDOWS))
    total = u
    mean = jnp.zeros_like(u)
    done = 1
    for gi, win in enumerate(POOL_WINDOWS):
        for k in range(done, win):
            total = total + pool_ext[POOL_HALO - k:POOL_HALO - k + tm, :]
        done = win
        cnt = jnp.minimum(pos + 1, win).astype(F32)
        mean = jnp.where(lane_group == gi, total / cnt, mean)
    pooled = (mean - u).astype(BF16)
    mixed = _dot(pooled, pool_bd_ref[...]) * pool_scale_ref[...]
    y_pool = _dot(mixed.astype(BF16), wpool_ref[...])

    ch = conv_ref[:, 0:cw]
    cb = conv_ref[:, cw:2 * cw]
    cc = conv_ref[:, 2 * cw:3 * cw]
    conv_ext[0:CONV_HALO, :] = conv_halo_ref[:, 0:cw] * conv_halo_ref[:, 2 * cw:3 * cw] * keep_halo
    conv_ext[CONV_HALO:, :] = cc * ch
    y = jnp.zeros((tm, cw), F32)
    for k in range(CONV_K):
        off = CONV_HALO - (CONV_K - 1) + k
        y = y + convw_ref[k:k + 1, :] * conv_ext[off:off + tm, :]
    y_conv = _dot((cb * y).astype(BF16), wconv_ref[...])

    y_nsa = _dot(nsa_ref[...], wnsa_ref[...])

    h = h_ref[...]
    xn = _rms_norm(h, g_ref[...]).astype(BF16)
    merged = jnp.zeros((tm, d), F32)
    for br, y_br in enumerate((y_nsa, y_pool, y_conv)):
        mg = jax.nn.sigmoid(_dot(xn, wmg_ref[:, br * d:(br + 1) * d]))
        merged = merged + mg * y_br
    h_new = h + _dot(merged.astype(BF16), wo_ref[...])
    out_ref[...] = h_new
    _route_tile(h_new, g2_ref, wr_ref, br_ref, before_ref, xn_ref, comb_ref, count_ref)


def _merge(h, g, nsa, pool_u, conv, wmg, wnsa, pool_bd, pool_scale, wpool, convw, wconv, wo,
           g2, wr, br, before, seq_len):
    n, d = h.shape
    tm = TOKEN_TILE
    cw = pool_u.shape[1]
    row = lambda width: pl.BlockSpec((tm, width), lambda i: (i, 0))
    full = lambda a: pl.BlockSpec(a.shape, lambda i: (0,) * a.ndim)
    halo = lambda rows, width: pl.BlockSpec(
        (rows, width), lambda i: (jnp.maximum(i * (tm // rows) - 1, 0), 0))
    return pl.pallas_call(
        functools.partial(_merge_kernel, seq_len=seq_len),
        grid=(n // tm,),
        in_specs=[row(d), full(g), row(nsa.shape[1]), row(cw), halo(POOL_HALO, cw),
                  row(conv.shape[1]), halo(CONV_HALO, conv.shape[1]),
                  full(wmg), full(wnsa), full(pool_bd), full(pool_scale), full(wpool), full(convw),
                  full(wconv), full(wo), full(g2), full(wr), full(br), full(before)],
        out_specs=[row(d), row(d + ROUTER_PAD), row(ROUTER_PAD), pl.BlockSpec((8, ROUTER_PAD), lambda i: (0, 0))],
        out_shape=[jax.ShapeDtypeStruct((n, d), F32), jax.ShapeDtypeStruct((n, d + ROUTER_PAD), BF16),
                   jax.ShapeDtypeStruct((n, ROUTER_PAD), F32), jax.ShapeDtypeStruct((8, ROUTER_PAD), F32)],
        scratch_shapes=[pltpu.VMEM((tm + POOL_HALO, cw), F32), pltpu.VMEM((tm + CONV_HALO, cw), F32)],
        compiler_params=_params("arbitrary"),
        name="merge",
    )(h, g, nsa, pool_u, pool_u, conv, conv, wmg, wnsa, pool_bd, pool_scale, wpool, convw, wconv, wo,
      g2, wr, br, before)


def _route(logits):
    lane = _iota(logits.shape, 1)
    lane_f = lane.astype(F32)
    big = F32(1e9)
    is_group = lane < N_EXPERT_GROUPS
    gl = jnp.where(is_group, logits, NEG_INF)
    g_max = jnp.max(gl, axis=1, keepdims=True)
    g_sel = jnp.min(jnp.where(gl == g_max, lane_f, big), axis=1, keepdims=True)
    g_prob = 1.0 / jnp.sum(jnp.where(is_group, jnp.exp(gl - g_max), 0.0), axis=1, keepdims=True)
    lo = N_EXPERT_GROUPS + EXPERTS_PER_GROUP * g_sel
    in_group = (lane_f >= lo) & (lane_f < lo + EXPERTS_PER_GROUP)
    el = jnp.where(in_group, logits, NEG_INF)
    v1 = jnp.max(el, axis=1, keepdims=True)
    i1 = jnp.min(jnp.where((el == v1) & in_group, lane_f, big), axis=1, keepdims=True)
    el2 = jnp.where(lane_f == i1, NEG_INF, el)
    rest = in_group & (lane_f != i1)
    v2 = jnp.max(el2, axis=1, keepdims=True)
    i2 = jnp.min(jnp.where((el2 == v2) & rest, lane_f, big), axis=1, keepdims=True)
    e2 = jnp.exp(v2 - v1)
    w1 = g_prob / (1.0 + e2)
    w2 = g_prob * e2 / (1.0 + e2)
    return jnp.where(lane_f == i1, w1, 0.0) + jnp.where(lane_f == i2, w2, 0.0), g_sel


GROUP_LANE = N_EXPERT_GROUPS + N_EXPERTS


RANK_LANE = GROUP_LANE + 1


def _route_tile(h, g_ref, wr_ref, br_ref, before_ref, xn_ref, comb_ref, count_ref):
    @pl.when(pl.program_id(0) == 0)
    def _():
        count_ref[...] = jnp.zeros(count_ref.shape, F32)

    xn = _rms_norm(h, g_ref[...])
    xn_hi = xn.astype(BF16)
    xn_lo = (xn - xn_hi.astype(F32)).astype(BF16)
    logits = (_dot(xn_hi, wr_ref[0]) + (_dot(xn_hi, wr_ref[1]) + _dot(xn_lo, wr_ref[0]))) + br_ref[...]
    comb, g_sel = _route(logits)
    lane = _iota(comb.shape, 1)
    chose = jnp.where(lane.astype(F32) == g_sel, 1.0, 0.0)
    earlier = _dot(before_ref[...], chose.astype(BF16)) + count_ref[0:1, :]
    rank = jnp.sum(chose * earlier, axis=1, keepdims=True)
    count_ref[0:1, :] = count_ref[0:1, :] + jnp.sum(chose, axis=0, keepdims=True)
    comb_ref[...] = jnp.where(lane == GROUP_LANE, g_sel, jnp.where(lane == RANK_LANE, rank, comb))
    first = N_EXPERT_GROUPS + EXPERTS_PER_GROUP * g_sel
    local = jnp.zeros(comb.shape, F32)
    for e in range(EXPERTS_PER_GROUP):
        c_e = jnp.sum(jnp.where(lane.astype(F32) == first + e, comb, 0.0), axis=1, keepdims=True)
        local = jnp.where((lane == e) | (lane == EXPERTS_PER_GROUP + e), c_e, local)
    local_hi = local.astype(BF16)
    d = xn_hi.shape[1]
    xn_ref[:, :d] = xn_hi
    xn_ref[:, d:] = jnp.where(lane < EXPERTS_PER_GROUP, local_hi, (local - local_hi.astype(F32)).astype(BF16))


def _experts_kernel(tile_group_ref, n_active_ref, x_ref, w1_ref, w3_ref, w2_ref, out_ref, acc_ref):
    i = pl.program_id(0)

    @pl.when(i < n_active_ref[0])
    def _():
        d = out_ref.shape[1]
        x = x_ref[:, :d]
        comb = x_ref[:, d:].astype(F32)
        lane = _iota(comb.shape, 1)
        for e in range(EXPERTS_PER_GROUP):
            c_e = jnp.sum(jnp.where((lane == e) | (lane == EXPERTS_PER_GROUP + e), comb, 0.0),
                          axis=1, keepdims=True)
            a = (jax.nn.silu(_dot(x, w1_ref[0, 0, e].astype(BF16)))
                 * _dot(x, w3_ref[0, 0, e].astype(BF16))) * c_e
            y = _dot(a.astype(BF16), w2_ref[0, 0, e].astype(BF16))
            if e == 0:
                acc_ref[...] = y
            else:
                acc_ref[...] += y
        out_ref[...] = acc_ref[...].astype(out_ref.dtype)


def _experts(tile_group, n_active, x_sorted, w1, w3, w2, layer):
    ns = x_sorted.shape[0]
    d = w1.shape[3]
    tm = MOE_SORT_TILE
    group_w = lambda w: pl.BlockSpec((1, 1) + w.shape[2:], lambda i, tg, na: (layer, tg[i], 0, 0, 0),
                                     pipeline_mode=pl.Buffered(1))
    grid_spec = pltpu.PrefetchScalarGridSpec(
        num_scalar_prefetch=2,
        grid=(ns // tm,),
        in_specs=[
            pl.BlockSpec((tm, x_sorted.shape[1]), lambda i, tg, na: (i, 0)),
            group_w(w1), group_w(w3), group_w(w2),
        ],
        out_specs=pl.BlockSpec((tm, d), lambda i, tg, na: (i, 0)),
        scratch_shapes=[pltpu.VMEM((tm, d), F32)],
    )
    return pl.pallas_call(
        _experts_kernel,
        grid_spec=grid_spec,
        out_shape=jax.ShapeDtypeStruct((ns, d), BF16),
        compiler_params=_params("arbitrary"),
        name="experts",
    )(tile_group, n_active, x_sorted, w1, w3, w2)


def _residual_kernel(h_ref, y_ref, gf_ref, out_ref, *, final_norm):
    out = h_ref[...] + y_ref[...].astype(F32)
    out_ref[...] = _rms_norm(out, gf_ref[...]) if final_norm else out


def _residual(h, y, gf, final_norm):
    n, d = h.shape
    tm = MOE_TILE
    row = pl.BlockSpec((tm, d), lambda i: (i, 0))
    return pl.pallas_call(
        functools.partial(_residual_kernel, final_norm=final_norm),
        grid=(n // tm,),
        in_specs=[row, row, pl.BlockSpec(gf.shape, lambda i: (0, 0))],
        out_specs=row,
        out_shape=jax.ShapeDtypeStruct((n, d), F32),
        compiler_params=_params("parallel"),
        name="residual",
    )(h, y, gf)


def _group_sort_plan(group_id, rank, counts, tile):
    n = group_id.shape[0]
    n_slots = n + N_EXPERT_GROUPS * tile
    padded = (counts + tile - 1) // tile * tile
    ends = jnp.cumsum(padded)
    slot = (ends - padded)[group_id] + rank
    source = jnp.zeros((n_slots,), jnp.int32).at[slot].set(jnp.arange(n, dtype=jnp.int32))
    tile_start = jnp.arange(n_slots // tile, dtype=jnp.int32) * tile
    tile_group = jnp.minimum(jnp.searchsorted(ends, tile_start, side="right"), N_EXPERT_GROUPS - 1)
    return slot, source, tile_group.astype(jnp.int32), (ends[-1:] // tile).astype(jnp.int32)


def _moe(h, xn, comb, counts, w1, w3, w2, layer, gf, final_norm):
    as_int = lambda a: a.astype(jnp.int32)
    slot, source, tile_group, n_active = _group_sort_plan(
        as_int(comb[:, GROUP_LANE]), as_int(comb[:, RANK_LANE]), as_int(counts[0, :N_EXPERT_GROUPS]),
        MOE_SORT_TILE)
    grouped = lambda w: w.reshape((w.shape[0], N_EXPERT_GROUPS, EXPERTS_PER_GROUP) + w.shape[2:])
    rows = lambda a, idx: jnp.take(a, idx, axis=0, mode="clip")
    y_sorted = _experts(tile_group, n_active, rows(xn, source), grouped(w1), grouped(w3), grouped(w2), layer)
    return _residual(h, rows(y_sorted, slot), gf, final_norm)


def _selection_constants(seq_len):
    ncp = seq_len // CMP_STRIDE
    n_slc = seq_len // SLC_BLOCK
    ratio = SLC_BLOCK // CMP_STRIDE
    lead = CMP_LEN // CMP_STRIDE - 1
    c = np.arange(ncp)[:, None]
    j = np.arange(n_slc)[None, :]
    pool_m = ((c >= ratio * j - lead) & (c < ratio * j + ratio)).astype(np.float32)
    blocks_per_chunk = KEY_CHUNK // SLC_BLOCK
    n_chunks = seq_len // KEY_CHUNK
    pair_m = np.zeros((n_slc, LANES * ((n_chunks + LANES - 1) // LANES)), np.float32)
    pair_m[np.arange(n_slc), np.arange(n_slc) // blocks_per_chunk] = 1.0
    n_words = (n_chunks + WORD_BITS - 1) // WORD_BITS
    bits_m = np.zeros((pair_m.shape[1], LANES), np.float32)
    ch = np.arange(n_chunks)
    bits_m[ch, ch // WORD_BITS] = 2.0 ** (ch % WORD_BITS)
    key_aux = np.zeros((TOKEN_TILE, LANES), np.float32)
    in_chunk = np.arange(TOKEN_TILE) % KEY_CHUNK
    key_aux[np.arange(TOKEN_TILE), HEAD_DIM + in_chunk // SLC_BLOCK] = 1.0
    key_aux[:, HEAD_DIM + AUX_SLOPE] = in_chunk
    cmp_aux = np.zeros((ncp, LANES), np.float32)
    cmp_aux[:, HEAD_DIM] = np.arange(ncp) // CMP_AUX_SPLIT
    cmp_aux[:, HEAD_DIM + 1] = np.arange(ncp) % CMP_AUX_SPLIT
    as_bf16 = lambda a: jnp.asarray(a, BF16)
    return as_bf16(pool_m.T), as_bf16(pair_m), as_bf16(bits_m), jnp.asarray(key_aux), jnp.asarray(cmp_aux), n_words


def kernel(x, norm1_g, w_in, cmp_pe, cmp_w1, cmp_w2, w_nsa_proj, pool_w, pool_scale, w_pool_proj, conv_w,
           w_conv_proj, w_o, norm2_g, router_group_w, router_group_b, router_expert_w, router_expert_b,
           expert_w1, expert_w3, expert_w2, final_norm_g):
    b, t, d = x.shape
    n = b * t
    depth = w_in.shape[0]
    dq = NSA_HEADS * HEAD_DIM
    dkv = 6 * NSA_KV_HEADS * HEAD_DIM
    dgate = NSA_HEADS * NSA_BRANCHES
    cw = d // 4
    assert t % TOKEN_TILE == 0 and n % MOE_TILE == 0 and t % Q_TILE == 0
    n_slc = t // SLC_BLOCK
    n_sel = min(SLC_TOPN, n_slc)
    n_chunks16 = t // CMP_STRIDE
    kvw = NSA_KV_HEADS * HEAD_DIM
    poolt_m, pair_m, bits_m, key_aux, cmp_aux, n_words = _selection_constants(t)
    assert Q_TILE == KEY_CHUNK and AUX_SLOPE < AUX_COLS and KEY_CHUNK <= 256
    assert n_chunks16 <= 256 * CMP_AUX_SPLIT and n_chunks16 % min(CMP_ROWS_STEP, n_chunks16) == 0
    assert n_words <= LANES and pair_m.shape[1] == LANES

    before = jnp.asarray(np.tril(np.ones((TOKEN_TILE, TOKEN_TILE), np.float32), -1), BF16)
    h = x.reshape(n, d)
    for l in range(depth):
        wl = w_in[l]
        o_gate = dq + dkv
        o_pool = o_gate + dgate
        o_merge = o_pool + cw + 3 * cw
        kv_cols = lambda kind: wl[:, dq + kind * kvw:dq + (kind + 1) * kvw]
        wq_t = (wl[:, :dq] * (HEAD_DIM ** -0.5)).T.astype(BF16)
        wv_t = jnp.concatenate([kv_cols(3), kv_cols(5)], axis=1).T.astype(BF16)
        wg_t = jnp.pad(wl[:, o_gate:o_pool], ((0, 0), (0, GATE_PAD - dgate))).T.astype(BF16)
        no_aux = jnp.zeros((d, LANES - HEAD_DIM), F32)
        wk = jnp.concatenate([piece for kind in (2, 4) for gi in range(NSA_KV_HEADS)
                              for piece in (kv_cols(kind)[:, gi * HEAD_DIM:(gi + 1) * HEAD_DIM], no_aux)],
                             axis=1).astype(BF16)
        wn = jnp.concatenate([kv_cols(0), kv_cols(1), wl[:, o_pool:o_merge]], axis=1).astype(BF16)
        wmg = wl[:, o_merge:].astype(BF16)
        pool_bd = jax.scipy.linalg.block_diag(*[pool_w[l, gi] for gi in range(pool_w.shape[1])]).astype(BF16)
        convw = jnp.pad(conv_w[l], ((0, 8 - CONV_K), (0, 0)))
        wr = jnp.pad(jnp.concatenate([router_group_w[l], router_expert_w[l]], axis=1),
                     ((0, 0), (0, ROUTER_PAD - N_EXPERT_GROUPS - N_EXPERTS)))
        wr_hi = wr.astype(BF16)
        wr = jnp.stack([wr_hi, (wr - wr_hi.astype(F32)).astype(BF16)])
        br = jnp.pad(jnp.concatenate([router_group_b[l], router_expert_b[l]]),
                     (0, ROUTER_PAD - N_EXPERT_GROUPS - N_EXPERTS))[None, :]
        pe = jnp.broadcast_to(cmp_pe[l].reshape(2, 1, CMP_LEN * HEAD_DIM), (2, 8, CMP_LEN * HEAD_DIM)).astype(BF16)
        halves = CMP_LEN // CMP_STRIDE
        w1_bd = jnp.einsum("khldc,gq->khlgdqc",
                           cmp_w1[l].reshape(2, halves, CMP_STRIDE, HEAD_DIM, CMP_HIDDEN),
                           jnp.eye(NSA_KV_HEADS, dtype=F32))
        w1_bd = w1_bd.reshape(2, halves, CMP_STRIDE * kvw, NSA_KV_HEADS * CMP_HIDDEN).astype(BF16)
        w2_k = jnp.pad(cmp_w2[l, 0], ((0, 0), (0, LANES - HEAD_DIM))).astype(BF16)
        w2_vt = cmp_w2[l, 1].T.astype(BF16)

        q_t, vst, vwt, gates_t, ks, kw, cmp_src, pool_u, conv = _inproj(
            h, norm1_g[l][None, :], wq_t, wv_t, wg_t, wk, wn, key_aux, b)
        kc_aux, vc_t = _compress(cmp_src, pe, cmp_w1[l].astype(BF16), w1_bd, w2_k, w2_vt, cmp_aux)
        oc_t, sel_t, words = _nsa_cmp(q_t, kc_aux, vc_t, poolt_m, pair_m, bits_m, n_sel)
        words = words[:, :, :, 0, :n_words].reshape(-1)
        nsa = _nsa_slc(words, q_t, ks, vst, kw, vwt, sel_t, oc_t, gates_t, n_words)
        h, xn, comb, counts = _merge(
            h, norm1_g[l][None, :], nsa.reshape(n, dq), pool_u, conv, wmg,
            w_nsa_proj[l].astype(BF16), pool_bd, pool_scale[l][None, :], w_pool_proj[l].astype(BF16),
            convw, w_conv_proj[l].astype(BF16), w_o[l].astype(BF16), norm2_g[l][None, :], wr, br, before, t)
        h = _moe(h, xn, comb, counts, expert_w1, expert_w3, expert_w2, l,
                 final_norm_g[None, :], final_norm=(l == depth - 1))
    return h.reshape(b, t, d)
```

```python
import functools

import jax
import jax.numpy as jnp
import numpy as np
from jax import lax
from jax.experimental import pallas as pl
from jax.experimental.pallas import tpu as pltpu

F32 = jnp.float32
BF16 = jnp.bfloat16

HEAD_DIM = 64
NSA_HEADS = 8
NSA_KV_HEADS = 2
NSA_GROUP = NSA_HEADS // NSA_KV_HEADS
CMP_LEN = 32
CMP_STRIDE = 16
CMP_HIDDEN = 4 * HEAD_DIM
SLC_BLOCK = 64
SLC_TOPN = 16
WINDOW = 512
NSA_BRANCHES = 3
POOL_WINDOWS = (2, 4, 8, 16)
CONV_K = 3
N_EXPERT_GROUPS = 4
EXPERTS_PER_GROUP = 8
N_EXPERTS = N_EXPERT_GROUPS * EXPERTS_PER_GROUP
RMS_EPS = 1e-6
NEG_INF = -1e30
ALIBI_SLOPES = tuple(float(2.0 ** (-8.0 * (h + 1) / NSA_HEADS)) for h in range(NSA_HEADS))

LANES = 128
VMEM_LIMIT = 56 * 1024 * 1024
TOKEN_TILE = 512
MOE_TILE = 1024
MOE_SORT_TILE = 512
Q_TILE = 128
KEY_CHUNK = 128
GATE_PAD = LANES
ROUTER_PAD = LANES
WORD_BITS = 16


def _params(*semantics):
    return pltpu.CompilerParams(dimension_semantics=semantics, vmem_limit_bytes=VMEM_LIMIT)


def _dot(a, b):
    return jnp.dot(a, b, preferred_element_type=F32)


def _dot_nt(a, b):
    return lax.dot_general(a, b, (((1,), (1,)), ((), ())), preferred_element_type=F32)


def _rms_norm(x, g):
    y = x * lax.rsqrt(jnp.mean(x * x, axis=-1, keepdims=True) + RMS_EPS)
    return y * g


def _iota(shape, dim):
    return lax.broadcasted_iota(jnp.int32, shape, dim)


def _inproj_kernel(x_ref, g_ref, wq_ref, wv_ref, wg_ref, wk_ref, wn_ref, kaux_ref,
                   q_ref, vs_ref, vw_ref, gate_ref, ks_ref, kw_ref, cmp_ref, pool_ref, conv_ref):
    xn = _rms_norm(x_ref[...], g_ref[...]).astype(BF16)
    sub_tiles = x_ref.shape[0] // Q_TILE
    q_t = _dot_nt(wq_ref[...], xn)
    for g in range(NSA_KV_HEADS):
        for j in range(sub_tiles):
            for r in range(NSA_GROUP):
                head = g * NSA_GROUP + r
                q_ref[0, g, j, :, r * Q_TILE:(r + 1) * Q_TILE] = q_t[
                    head * HEAD_DIM:(head + 1) * HEAD_DIM, j * Q_TILE:(j + 1) * Q_TILE].astype(BF16)
    v_t = _dot_nt(wv_ref[...], xn)
    k = _dot(xn, wk_ref[...])
    for branch, (v_ref, k_ref) in enumerate(((vs_ref, ks_ref), (vw_ref, kw_ref))):
        for g in range(NSA_KV_HEADS):
            slab = branch * NSA_KV_HEADS + g
            for j in range(sub_tiles):
                v_ref[0, g, j] = v_t[slab * HEAD_DIM:(slab + 1) * HEAD_DIM,
                                     j * KEY_CHUNK:(j + 1) * KEY_CHUNK].astype(BF16)
            k_ref[0, g] = (k[:, slab * LANES:(slab + 1) * LANES] + kaux_ref[...]).astype(BF16)
    gate_ref[0] = _dot_nt(wg_ref[...], xn)
    col = 0
    for ref in (cmp_ref, pool_ref, conv_ref):
        width = ref.shape[-1]
        ref[...] = _dot(xn, wn_ref[:, col:col + width]).reshape(ref.shape)
        col += width


def _inproj(h, g, wq_t, wv_t, wg_t, wk, wn, kaux, batch):
    n, d = h.shape
    t = n // batch
    tm = TOKEN_TILE
    steps = t // tm
    sub = tm // Q_TILE
    cw = d // 4
    gq = NSA_GROUP * Q_TILE
    full = lambda a: pl.BlockSpec(a.shape, lambda i: (0,) * a.ndim)
    row = lambda width: pl.BlockSpec((tm, width), lambda i: (i, 0))
    tiles = lambda rows, cols: pl.BlockSpec((1, NSA_KV_HEADS, sub, rows, cols),
                                            lambda i: (i // steps, 0, i % steps, 0, 0))
    keys = pl.BlockSpec((1, NSA_KV_HEADS, tm, LANES), lambda i: (i // steps, 0, i % steps, 0))
    sds = jax.ShapeDtypeStruct
    v_shape = sds((batch, NSA_KV_HEADS, t // KEY_CHUNK, HEAD_DIM, KEY_CHUNK), BF16)
    k_shape = sds((batch, NSA_KV_HEADS, t, LANES), BF16)
    return pl.pallas_call(
        _inproj_kernel,
        grid=(n // tm,),
        in_specs=[row(d), full(g), full(wq_t), full(wv_t), full(wg_t), full(wk), full(wn), full(kaux)],
        out_specs=[tiles(HEAD_DIM, gq), tiles(HEAD_DIM, KEY_CHUNK), tiles(HEAD_DIM, KEY_CHUNK),
                   pl.BlockSpec((1, GATE_PAD, tm), lambda i: (i // steps, 0, i % steps)),
                   keys, keys,
                   pl.BlockSpec((1, tm, 2 * NSA_KV_HEADS * HEAD_DIM), lambda i: (i // steps, i % steps, 0)),
                   row(cw), row(3 * cw)],
        out_shape=[sds((batch, NSA_KV_HEADS, t // Q_TILE, HEAD_DIM, gq), BF16), v_shape, v_shape,
                   sds((batch, GATE_PAD, t), F32), k_shape, k_shape,
                   sds((batch, t, 2 * NSA_KV_HEADS * HEAD_DIM), F32), sds((n, cw), F32), sds((n, 3 * cw), F32)],
        compiler_params=_params("parallel"),
        name="inproj",
    )(h, g, wq_t, wv_t, wg_t, wk, wn, kaux)


def _gelu_tanh(x):
    return 0.5 * x * (1.0 + jnp.tanh(0.7978845608028654 * (x + 0.044715 * x * x * x)))


def _compress_kernel(src_ref, pe_ref, w1_ref, w1bd_ref, w2k_ref, w2vt_ref, caux_ref, kc_ref, vct_ref):
    kind = pl.program_id(1)
    ncp = kc_ref.shape[2]
    hidden = w1_ref.shape[2]
    pieces = [src_ref[0, pl.ds(l, ncp, stride=CMP_STRIDE), :].astype(BF16) for l in range(CMP_STRIDE)]
    chunk = jnp.concatenate(pieces, axis=1)
    first = _dot(chunk, w1bd_ref[0, 0])
    second = _dot(chunk, w1bd_ref[0, 1])
    bias = _dot(pe_ref[0], w1_ref[0])[0:1, :]
    hid = first + pltpu.roll(second, ncp - 1, 0) + jnp.concatenate([bias] * NSA_KV_HEADS, axis=1)
    row = _iota((ncp, 1), 0)
    act = jnp.where(row < ncp - 1, _gelu_tanh(hid), 0.0).astype(BF16)
    for g in range(NSA_KV_HEADS):
        act_g = act[:, g * hidden:(g + 1) * hidden]

        @pl.when(kind == 0)
        def _():
            kc_ref[0, g] = (_dot(act_g, w2k_ref[...]) + caux_ref[...]).astype(BF16)

        @pl.when(kind == 1)
        def _():
            vct_ref[0, g * HEAD_DIM:(g + 1) * HEAD_DIM, :] = _dot_nt(w2vt_ref[...], act_g).astype(BF16)


def _compress(src, pe, w1, w1bd, w2k, w2vt, caux):
    b, t, _ = src.shape
    ncp = t // CMP_STRIDE
    gd = NSA_KV_HEADS * HEAD_DIM
    full = lambda a: pl.BlockSpec(a.shape, lambda bi, k: (0,) * a.ndim)
    per_kind = lambda a: pl.BlockSpec((1,) + a.shape[1:], lambda bi, k: (k,) + (0,) * (a.ndim - 1))
    return pl.pallas_call(
        _compress_kernel,
        grid=(b, 2),
        in_specs=[pl.BlockSpec((1, t, gd), lambda bi, k: (bi, 0, k)),
                  per_kind(pe), per_kind(w1), per_kind(w1bd), full(w2k), full(w2vt), full(caux)],
        out_specs=[pl.BlockSpec((1, NSA_KV_HEADS, ncp, LANES), lambda bi, k: (bi, 0, 0, 0)),
                   pl.BlockSpec((1, gd, ncp), lambda bi, k: (bi, 0, 0))],
        out_shape=[jax.ShapeDtypeStruct((b, NSA_KV_HEADS, ncp, LANES), BF16),
                   jax.ShapeDtypeStruct((b, gd, ncp), BF16)],
        compiler_params=_params("parallel", "arbitrary"),
        name="compress",
    )(src, pe, w1, w1bd, w2k, w2vt, caux)


SOFTMAX_FLOOR = -1e29
TAKEN = -3e38
ATTN_BATCH = 4
BLOCKS_PER_CHUNK = KEY_CHUNK // SLC_BLOCK
AUX_COLS = 16
AUX_SLOPE = BLOCKS_PER_CHUNK
ONES_ROWS = 16
CMP_ROWS_STEP = 128
CMP_AUX_SPLIT = 128


def _slope(g, r):
    if isinstance(g, int):
        return jnp.float32(ALIBI_SLOPES[g * NSA_GROUP + r])
    s = jnp.float32(ALIBI_SLOPES[r])
    for gi in range(1, NSA_KV_HEADS):
        s = jnp.where(g == gi, jnp.float32(ALIBI_SLOPES[gi * NSA_GROUP + r]), s)
    return s


def _nsa_cmp_kernel(qt_ref, kc_ref, vct_ref, poolt_ref, pair_ref, bits_ref, oct_ref, selt_ref, words_ref,
                    *, n_sel):
    i = pl.program_id(1)
    qt = selt_ref.shape[4]
    ncp = kc_ref.shape[2]
    n_slc = poolt_ref.shape[0]
    gq = qt_ref.shape[4]
    start = i * qt
    t = start + _iota((1, qt), 1)
    aux_row = _iota((AUX_COLS, gq), 0)
    col_head = _iota((1, gq), 1) // qt
    pad_rows = jnp.zeros((kc_ref.shape[3] - qt_ref.shape[3] - AUX_COLS, gq), BF16)

    def weights(g):
        slope_cols = jnp.zeros((1, gq), F32)
        for r in range(NSA_GROUP):
            slope_cols = jnp.where(col_head == r, _slope(g, r), slope_cols)
        aux = jnp.where(aux_row == 0, slope_cols * (CMP_STRIDE * CMP_AUX_SPLIT),
                        jnp.where(aux_row == 1, slope_cols * CMP_STRIDE, 0.0))
        return jnp.concatenate([qt_ref[0, g, 0], aux.astype(BF16), pad_rows], axis=0)

    def importance(g, nr, nb):
        s = _dot(kc_ref[0, g, :nr, :], weights(g))
        edge = min(nr, 2 * CMP_ROWS_STEP)
        cmp_end = ((nr - edge) + _iota((edge, 1), 0)) * CMP_STRIDE + (CMP_LEN - 1)
        visible = cmp_end <= t
        vct = vct_ref[0, g * HEAD_DIM:(g + 1) * HEAD_DIM, :nr]
        psum = jnp.zeros((nr, qt), F32)
        for r in range(NSA_GROUP):
            cols = slice(r * qt, (r + 1) * qt)
            sr = s[:, cols]
            tail = jnp.where(visible, sr[nr - edge:], NEG_INF)
            sr = tail if edge == nr else jnp.concatenate([sr[:nr - edge], tail], axis=0)
            m = jnp.maximum(jnp.max(sr, axis=0, keepdims=True), SOFTMAX_FLOOR)
            e = jnp.exp(sr - m)
            l = jnp.sum(e, axis=0, keepdims=True)
            inv = jnp.where(l > 0.0, 1.0 / l, 0.0)
            oct_ref[0, g, 0, :, cols] = (_dot(vct, e.astype(BF16)) * inv).astype(BF16)
            psum = psum + e * inv
        return _dot(poolt_ref[:nb, :nr], psum.astype(BF16))

    def visible_prefix(nr):
        nb = min(n_slc, nr * CMP_STRIDE // SLC_BLOCK)
        imp = jnp.concatenate([importance(g, nr, nb) for g in range(NSA_KV_HEADS)], axis=1)
        blk = _iota((nb, 1), 0)
        cur = jnp.concatenate([t // SLC_BLOCK] * NSA_KV_HEADS, axis=1)
        forced = (blk == 0) | (blk == cur) | (blk == cur - 1)
        score = jnp.where(forced, TAKEN, jnp.where(blk <= cur, imp, NEG_INF))
        n_forced = 1 + jnp.where(cur >= 1, 1, 0) + jnp.where(cur >= 2, 1, 0)
        blk_f = blk.astype(F32)

        def take_one(score, active):
            m = jnp.max(score, axis=0, keepdims=True)
            first = jnp.min(jnp.where(score == m, blk_f, F32(1e9)), axis=0, keepdims=True)
            hit = (blk_f == first) if active is None else ((blk_f == first) & active)
            return jnp.where(hit, TAKEN, score)

        common_rounds = max(n_sel - 3, 0)
        for _ in range(common_rounds):
            score = take_one(score, None)

        def early_rounds(score):
            for k in range(common_rounds, n_sel - 1):
                score = take_one(score, n_sel - n_forced > k)
            return score

        score = lax.cond(start < 2 * SLC_BLOCK, early_rounds, lambda sc: sc, score)
        for g in range(NSA_KV_HEADS):
            sel_g = score[:, g * qt:(g + 1) * qt] == TAKEN
            selt_ref[0, g, 0, :nb, :] = jnp.where(sel_g, 0.0, NEG_INF)
            if nb < n_slc:
                selt_ref[0, g, 0, nb:, :] = jnp.full((n_slc - nb, qt), NEG_INF, F32)
            count = _dot_nt(jnp.ones((8, qt), BF16), jnp.where(sel_g, 1.0, 0.0).astype(BF16))
            used = jnp.where(count > 0.0, 1.0, 0.0).astype(BF16)
            chunk_used = jnp.where(_dot(used, pair_ref[:nb, :]) > 0.0, 1.0, 0.0).astype(BF16)
            words_ref[0, g, 0] = _dot(chunk_used, bits_ref[...]).astype(jnp.int32)

    step = min(CMP_ROWS_STEP, ncp)
    rows_needed = jnp.minimum((start + qt - CMP_LEN) // CMP_STRIDE + 1, ncp)
    n_steps = (rows_needed + step - 1) // step
    for k in range(ncp // step):
        pl.when(n_steps == k + 1)(functools.partial(visible_prefix, (k + 1) * step))


def _nsa_cmp(q_t, kc, vct, poolt_m, pair_m, bits_m, n_sel):
    b, ng, n_tiles, qrows, qcols = q_t.shape
    ncp = kc.shape[2]
    n_slc = poolt_m.shape[0]
    const = lambda shape: pl.BlockSpec(shape, lambda bi, i: (0,) * len(shape))
    tile5 = lambda rows, cols: pl.BlockSpec((1, ng, 1, rows, cols), lambda bi, i: (bi, 0, i, 0, 0))
    return pl.pallas_call(
        functools.partial(_nsa_cmp_kernel, n_sel=n_sel),
        grid=(b, n_tiles),
        in_specs=[
            tile5(qrows, qcols),
            pl.BlockSpec((1, ng, ncp, kc.shape[3]), lambda bi, i: (bi, 0, 0, 0)),
            pl.BlockSpec((1, ng * HEAD_DIM, ncp), lambda bi, i: (bi, 0, 0)),
            const(poolt_m.shape), const(pair_m.shape), const(bits_m.shape),
        ],
        out_specs=[tile5(HEAD_DIM, qcols), tile5(n_slc, Q_TILE), tile5(8, LANES)],
        out_shape=[
            jax.ShapeDtypeStruct((b, ng, n_tiles, HEAD_DIM, qcols), BF16),
            jax.ShapeDtypeStruct((b, ng, n_tiles, n_slc, Q_TILE), F32),
            jax.ShapeDtypeStruct((b, ng, n_tiles, 8, LANES), jnp.int32),
        ],
        compiler_params=_params("parallel", "parallel"),
        name="nsa_compressed",
    )(q_t, kc, vct, poolt_m, pair_m, bits_m)


def _nsa_slc_kernel(words_ref, qt_ref, ks_ref, vst_ref, kw_ref, vwt_ref, selt_ref, oct_ref, gt_ref,
                    out_ref, m_ref, l_ref, acc_ref, ow_ref, sa_ref, sb_ref, sw_ref, list_ref, *, words_per_tile):
    bi = pl.program_id(0)
    g = pl.program_id(1)
    i = pl.program_id(2)
    n_tiles = pl.num_programs(2)
    qt = out_ref.shape[1]
    q_rows = qt_ref[0, 0, 0]
    gq = q_rows.shape[1]
    start = i * qt
    lane_f = _iota((1, qt), 1).astype(F32)
    key_in_chunk = _iota((KEY_CHUNK, qt), 0)
    query_in_tile = _iota((KEY_CHUNK, qt), 1)

    aux_row = _iota((AUX_COLS, gq), 0)
    col_head = _iota((1, gq), 1) // qt
    slope_cols = jnp.zeros((1, gq), F32)
    for r in range(NSA_GROUP):
        slope_cols = jnp.where(col_head == r, _slope(g, r), slope_cols)
    aux_base = jnp.where(aux_row == AUX_SLOPE, slope_cols, 0.0)
    pad_rows = jnp.zeros((ks_ref.shape[3] - q_rows.shape[0] - AUX_COLS, gq), BF16)

    def scores_of(slots):
        return [_dot(k, jnp.concatenate([q_rows, aux.astype(BF16), pad_rows], axis=0))
                for k, aux, _, _, _ in slots]

    def values_of(slots):
        values = jnp.concatenate([v for _, _, v, _, _ in slots], axis=1)
        return jnp.concatenate([values, jnp.ones((ONES_ROWS, values.shape[1]), BF16)], axis=0)

    def softmax_step(slots, scores, v_cat, r, m_old):
        cols = slice(r * qt, (r + 1) * qt)
        srs, tops = [], []
        for j, (_, _, _, shift, mask) in enumerate(slots):
            sr = scores[j, :, cols] if hasattr(scores, "at") else scores[j][:, cols]
            sr = sr if mask is None else jnp.where(mask, sr, NEG_INF)
            srs.append(sr)
            tops.append(jnp.max(sr, axis=0, keepdims=True) + shift[r])
        m_new = functools.reduce(jnp.maximum, tops, m_old)
        ps = [jnp.exp((sr - (m_new - slot[3][r])).astype(BF16)) for slot, sr in zip(slots, srs)]
        weighted = _dot(v_cat, jnp.concatenate(ps, axis=0))
        return m_new, weighted[HEAD_DIM:HEAD_DIM + 1], weighted[:HEAD_DIM]

    def shifts(dist0, ok):
        rows = [-_slope(g, r) * (dist0 + lane_f) for r in range(NSA_GROUP)]
        return rows if ok is None else [jnp.where(ok, row, NEG_INF) for row in rows]

    word_base = ((bi * NSA_KV_HEADS + g) * n_tiles + i) * words_per_tile
    list_ref[0] = 0

    def scan_word(w, n):
        word = words_ref[word_base + w]

        def scan_bits(n):
            for bit in range(WORD_BITS):
                c = w * WORD_BITS + bit
                list_ref[n] = c
                n = n + jnp.where(c < i, (word >> bit) & 1, 0)
            return n

        return lax.cond(word != 0, scan_bits, lambda n: n, n)

    n_listed = lax.fori_loop(0, (i + WORD_BITS - 1) // WORD_BITS, scan_word, 0)

    def selected_slot(c, ok, mask):
        at = pl.multiple_of(c * KEY_CHUNK, KEY_CHUNK)
        bias = selt_ref[0, 0, 0, pl.ds(c * BLOCKS_PER_CHUNK, BLOCKS_PER_CHUNK), :]
        aux = aux_base
        for blk in range(BLOCKS_PER_CHUNK):
            aux = jnp.where(aux_row == blk, jnp.concatenate([bias[blk:blk + 1]] * NSA_GROUP, axis=1), aux)
        return (ks_ref[0, 0, pl.ds(at, KEY_CHUNK), :], aux, vst_ref[0, 0, c],
                shifts((start - c * KEY_CHUNK).astype(F32), ok), mask)

    def listed_slot(idx):
        ok = idx < n_listed
        c = jnp.where(ok, list_ref[jnp.minimum(idx, jnp.maximum(n_listed - 1, 0))], 0)
        return selected_slot(c, ok, None)

    floor = jnp.full((1, qt), SOFTMAX_FLOOR, F32)

    n_back = WINDOW // KEY_CHUNK
    slots = []
    for j in range(n_back + 1):
        cs = start - WINDOW + j * KEY_CHUNK
        chunk = jnp.maximum(cs, 0) // KEY_CHUNK
        at = pl.multiple_of(chunk * KEY_CHUNK, KEY_CHUNK)
        mask = (query_in_tile < key_in_chunk) if j == 0 else (
            (key_in_chunk <= query_in_tile) if j == n_back else None)
        slots.append((kw_ref[0, 0, pl.ds(at, KEY_CHUNK), :], aux_base, vwt_ref[0, 0, chunk],
                      shifts(F32(WINDOW - j * KEY_CHUNK), cs >= 0), mask))
    window_slots = slots
    first_slots = [selected_slot(i, None, key_in_chunk <= query_in_tile)] + [
        listed_slot(j) for j in range(ATTN_BATCH - 1)]

    def listed_batch(n):
        return [listed_slot(ATTN_BATCH - 1 + n * ATTN_BATCH + j) for j in range(ATTN_BATCH)]

    def issue(slots, dst_ref):
        for j, s in enumerate(scores_of(slots)):
            dst_ref[j] = s

    issue(window_slots, sw_ref)
    issue(first_slots, sb_ref)
    v_cat = values_of(window_slots)
    for r in range(NSA_GROUP):
        _, total, weighted = softmax_step(window_slots, sw_ref, v_cat, r, floor)
        ow_ref[:, r * qt:(r + 1) * qt] = weighted * jnp.where(total > 0.0, 1.0 / total, 0.0)

    def consume(slots, src_ref):
        v_cat = values_of(slots)
        for r in range(NSA_GROUP):
            cols = slice(r * qt, (r + 1) * qt)
            m_old = m_ref[r]
            m_new, total, weighted = softmax_step(slots, src_ref, v_cat, r, m_old)
            alpha = jnp.exp(m_old - m_new)
            l_ref[r] = alpha * l_ref[r] + total
            acc_ref[:, cols] = alpha * acc_ref[:, cols] + weighted
            m_ref[r] = m_new

    issue(listed_batch(0), sa_ref)
    v_cat = values_of(first_slots)
    for r in range(NSA_GROUP):
        m_ref[r], l_ref[r], acc_ref[:, r * qt:(r + 1) * qt] = softmax_step(first_slots, sb_ref, v_cat, r, floor)

    def batch_pair(it, carry):
        first, second, third = (listed_batch(2 * it + n) for n in range(3))
        issue(second, sb_ref)
        consume(first, sa_ref)
        issue(third, sa_ref)
        consume(second, sb_ref)
        return carry

    n_rest = jnp.maximum(n_listed - (ATTN_BATCH - 1), 0)
    n_batches = (n_rest + ATTN_BATCH - 1) // ATTN_BATCH
    lax.fori_loop(0, n_batches // 2, batch_pair, 0)

    @pl.when(n_batches % 2 == 1)
    def _():
        consume(listed_batch(n_batches - 1), sa_ref)

    def finalize(r):
        l = l_ref[r]
        return acc_ref[:, r * qt:(r + 1) * qt] * jnp.where(l > 0.0, 1.0 / l, 0.0)

    outs = []
    for r in range(NSA_GROUP):
        cols = slice(r * qt, (r + 1) * qt)
        col = (g * NSA_GROUP + r) * NSA_BRANCHES
        gate = lambda br: jax.nn.sigmoid(gt_ref[0, pl.ds(col + br, 1), :])
        outs.append(gate(0) * oct_ref[0, 0, 0, :, cols].astype(F32) + gate(1) * finalize(r)
                    + gate(2) * ow_ref[:, cols])
    out_ref[0] = jnp.concatenate(outs, axis=0).T.astype(out_ref.dtype)


def _nsa_slc(words, q_t, ks, vst, kw, vwt, selt, oct, gates_t, words_per_tile):
    b, _, t, kw_cols = ks.shape
    _, _, n_tiles, qrows, qcols = q_t.shape
    n_slc = selt.shape[3]
    n_chunks = vst.shape[2]
    gw = NSA_GROUP * HEAD_DIM
    once = dict(pipeline_mode=pl.Buffered(1))
    k_spec = pl.BlockSpec((1, 1, t, kw_cols), lambda bi, g, i, w: (bi, g, 0, 0), **once)
    vt_spec = pl.BlockSpec((1, 1, n_chunks, HEAD_DIM, KEY_CHUNK), lambda bi, g, i, w: (bi, g, 0, 0, 0), **once)
    tile5 = lambda rows, cols: pl.BlockSpec((1, 1, 1, rows, cols), lambda bi, g, i, w: (bi, g, i, 0, 0))
    grid_spec = pltpu.PrefetchScalarGridSpec(
        num_scalar_prefetch=1,
        grid=(b, NSA_KV_HEADS, n_tiles),
        in_specs=[
            tile5(qrows, qcols),
            k_spec, vt_spec, k_spec, vt_spec,
            tile5(n_slc, Q_TILE),
            tile5(HEAD_DIM, qcols),
            pl.BlockSpec((1, GATE_PAD, Q_TILE), lambda bi, g, i, w: (bi, 0, i)),
        ],
        out_specs=pl.BlockSpec((1, Q_TILE, gw), lambda bi, g, i, w: (bi, i, g)),
        scratch_shapes=[pltpu.VMEM((NSA_GROUP, 1, Q_TILE), F32), pltpu.VMEM((NSA_GROUP, 1, Q_TILE), F32),
                        pltpu.VMEM((HEAD_DIM, qcols), F32), pltpu.VMEM((HEAD_DIM, qcols), F32),
                        pltpu.VMEM((ATTN_BATCH, KEY_CHUNK, qcols), F32),
                        pltpu.VMEM((ATTN_BATCH, KEY_CHUNK, qcols), F32),
                        pltpu.VMEM((WINDOW // KEY_CHUNK + 1, KEY_CHUNK, qcols), F32),
                        pltpu.SMEM((n_chunks,), jnp.int32)],
    )
    return pl.pallas_call(
        functools.partial(_nsa_slc_kernel, words_per_tile=words_per_tile),
        grid_spec=grid_spec,
        out_shape=jax.ShapeDtypeStruct((b, t, NSA_HEADS * HEAD_DIM), BF16),
        compiler_params=_params("parallel", "parallel", "parallel"),
        name="nsa_selected_window",
    )(words, q_t, ks, vst, kw, vwt, selt, oct, gates_t)


POOL_HALO = 16
CONV_HALO = 8


def _merge_kernel(h_ref, g_ref, nsa_ref, pool_ref, pool_halo_ref, conv_ref, conv_halo_ref,
                  wmg_ref, wnsa_ref, pool_bd_ref, pool_scale_ref, wpool_ref, convw_ref, wconv_ref, wo_ref,
                  g2_ref, wr_ref, br_ref, before_ref,
                  out_ref, xn_ref, comb_ref, count_ref, pool_ext, conv_ext, *, seq_len):
    i = pl.program_id(0)
    tm, d = h_ref.shape
    cw = pool_ref.shape[1]
    pos0 = (i * tm) % seq_len
    keep_halo = jnp.where(pos0 == 0, 0.0, 1.0)
    pos = pos0 + _iota((tm, 1), 0)

    u = pool_ref[...]
    pool_ext[0:POOL_HALO, :] = pool_halo_ref[...] * keep_halo
    pool_ext[POOL_HALO:, :] = u
    lane_group = _iota((1, cw), 1) // (cw // len(POOL_WINDOWS))
    total = u
    mean = jnp.zeros_like(u)
    done = 1
    for gi, win in enumerate(POOL_WINDOWS):
        for k in range(done, win):
            total = total + pool_ext[POOL_HALO - k:POOL_HALO - k + tm, :]
        done = win
        cnt = jnp.minimum(pos + 1, win).astype(F32)
        mean = jnp.where(lane_group == gi, total / cnt, mean)
    pooled = (mean - u).astype(BF16)
    mixed = _dot(pooled, pool_bd_ref[...]) * pool_scale_ref[...]
    y_pool = _dot(mixed.astype(BF16), wpool_ref[...])

    ch = conv_ref[:, 0:cw]
    cb = conv_ref[:, cw:2 * cw]
    cc = conv_ref[:, 2 * cw:3 * cw]
    conv_ext[0:CONV_HALO, :] = conv_halo_ref[:, 0:cw] * conv_halo_ref[:, 2 * cw:3 * cw] * keep_halo
    conv_ext[CONV_HALO:, :] = cc * ch
    y = jnp.zeros((tm, cw), F32)
    for k in range(CONV_K):
        off = CONV_HALO - (CONV_K - 1) + k
        y = y + convw_ref[k:k + 1, :] * conv_ext[off:off + tm, :]
    y_conv = _dot((cb * y).astype(BF16), wconv_ref[...])

    y_nsa = _dot(nsa_ref[...], wnsa_ref[...])

    h = h_ref[...]
    xn = _rms_norm(h, g_ref[...]).astype(BF16)
    merged = jnp.zeros((tm, d), F32)
    for br, y_br in enumerate((y_nsa, y_pool, y_conv)):
        mg = jax.nn.sigmoid(_dot(xn, wmg_ref[:, br * d:(br + 1) * d]))
        merged = merged + mg * y_br
    h_new = h + _dot(merged.astype(BF16), wo_ref[...])
    out_ref[...] = h_new
    _route_tile(h_new, g2_ref, wr_ref, br_ref, before_ref, xn_ref, comb_ref, count_ref)


def _merge(h, g, nsa, pool_u, conv, wmg, wnsa, pool_bd, pool_scale, wpool, convw, wconv, wo,
           g2, wr, br, before, seq_len):
    n, d = h.shape
    tm = TOKEN_TILE
    cw = pool_u.shape[1]
    row = lambda width: pl.BlockSpec((tm, width), lambda i: (i, 0))
    full = lambda a: pl.BlockSpec(a.shape, lambda i: (0,) * a.ndim)
    halo = lambda rows, width: pl.BlockSpec(
        (rows, width), lambda i: (jnp.maximum(i * (tm // rows) - 1, 0), 0))
    return pl.pallas_call(
        functools.partial(_merge_kernel, seq_len=seq_len),
        grid=(n // tm,),
        in_specs=[row(d), full(g), row(nsa.shape[1]), row(cw), halo(POOL_HALO, cw),
                  row(conv.shape[1]), halo(CONV_HALO, conv.shape[1]),
                  full(wmg), full(wnsa), full(pool_bd), full(pool_scale), full(wpool), full(convw),
                  full(wconv), full(wo), full(g2), full(wr), full(br), full(before)],
        out_specs=[row(d), row(d + ROUTER_PAD), row(ROUTER_PAD), pl.BlockSpec((8, ROUTER_PAD), lambda i: (0, 0))],
        out_shape=[jax.ShapeDtypeStruct((n, d), F32), jax.ShapeDtypeStruct((n, d + ROUTER_PAD), BF16),
                   jax.ShapeDtypeStruct((n, ROUTER_PAD), F32), jax.ShapeDtypeStruct((8, ROUTER_PAD), F32)],
        scratch_shapes=[pltpu.VMEM((tm + POOL_HALO, cw), F32), pltpu.VMEM((tm + CONV_HALO, cw), F32)],
        compiler_params=_params("arbitrary"),
        name="merge",
    )(h, g, nsa, pool_u, pool_u, conv, conv, wmg, wnsa, pool_bd, pool_scale, wpool, convw, wconv, wo,
      g2, wr, br, before)


def _route(logits):
    lane = _iota(logits.shape, 1)
    lane_f = lane.astype(F32)
    big = F32(1e9)
    is_group = lane < N_EXPERT_GROUPS
    gl = jnp.where(is_group, logits, NEG_INF)
    g_max = jnp.max(gl, axis=1, keepdims=True)
    g_sel = jnp.min(jnp.where(gl == g_max, lane_f, big), axis=1, keepdims=True)
    g_prob = 1.0 / jnp.sum(jnp.where(is_group, jnp.exp(gl - g_max), 0.0), axis=1, keepdims=True)
    lo = N_EXPERT_GROUPS + EXPERTS_PER_GROUP * g_sel
    in_group = (lane_f >= lo) & (lane_f < lo + EXPERTS_PER_GROUP)
    el = jnp.where(in_group, logits, NEG_INF)
    v1 = jnp.max(el, axis=1, keepdims=True)
    i1 = jnp.min(jnp.where((el == v1) & in_group, lane_f, big), axis=1, keepdims=True)
    el2 = jnp.where(lane_f == i1, NEG_INF, el)
    rest = in_group & (lane_f != i1)
    v2 = jnp.max(el2, axis=1, keepdims=True)
    i2 = jnp.min(jnp.where((el2 == v2) & rest, lane_f, big), axis=1, keepdims=True)
    e2 = jnp.exp(v2 - v1)
    w1 = g_prob / (1.0 + e2)
    w2 = g_prob * e2 / (1.0 + e2)
    return jnp.where(lane_f == i1, w1, 0.0) + jnp.where(lane_f == i2, w2, 0.0), g_sel


GROUP_LANE = N_EXPERT_GROUPS + N_EXPERTS


RANK_LANE = GROUP_LANE + 1


def _route_tile(h, g_ref, wr_ref, br_ref, before_ref, xn_ref, comb_ref, count_ref):
    @pl.when(pl.program_id(0) == 0)
    def _():
        count_ref[...] = jnp.zeros(count_ref.shape, F32)

    xn = _rms_norm(h, g_ref[...])
    xn_hi = xn.astype(BF16)
    xn_lo = (xn - xn_hi.astype(F32)).astype(BF16)
    logits = (_dot(xn_hi, wr_ref[0]) + (_dot(xn_hi, wr_ref[1]) + _dot(xn_lo, wr_ref[0]))) + br_ref[...]
    comb, g_sel = _route(logits)
    lane = _iota(comb.shape, 1)
    chose = jnp.where(lane.astype(F32) == g_sel, 1.0, 0.0)
    earlier = _dot(before_ref[...], chose.astype(BF16)) + count_ref[0:1, :]
    rank = jnp.sum(chose * earlier, axis=1, keepdims=True)
    count_ref[0:1, :] = count_ref[0:1, :] + jnp.sum(chose, axis=0, keepdims=True)
    comb_ref[...] = jnp.where(lane == GROUP_LANE, g_sel, jnp.where(lane == RANK_LANE, rank, comb))
    first = N_EXPERT_GROUPS + EXPERTS_PER_GROUP * g_sel
    local = jnp.zeros(comb.shape, F32)
    for e in range(EXPERTS_PER_GROUP):
        c_e = jnp.sum(jnp.where(lane.astype(F32) == first + e, comb, 0.0), axis=1, keepdims=True)
        local = jnp.where((lane == e) | (lane == EXPERTS_PER_GROUP + e), c_e, local)
    local_hi = local.astype(BF16)
    d = xn_hi.shape[1]
    xn_ref[:, :d] = xn_hi
    xn_ref[:, d:] = jnp.where(lane < EXPERTS_PER_GROUP, local_hi, (local - local_hi.astype(F32)).astype(BF16))


def _experts_kernel(tile_group_ref, n_active_ref, x_ref, w1_ref, w3_ref, w2_ref, out_ref, acc_ref):
    i = pl.program_id(0)

    @pl.when(i < n_active_ref[0])
    def _():
        d = out_ref.shape[1]
        x = x_ref[:, :d]
        comb = x_ref[:, d:].astype(F32)
        lane = _iota(comb.shape, 1)
        for e in range(EXPERTS_PER_GROUP):
            c_e = jnp.sum(jnp.where((lane == e) | (lane == EXPERTS_PER_GROUP + e), comb, 0.0),
                          axis=1, keepdims=True)
            a = (jax.nn.silu(_dot(x, w1_ref[0, 0, e].astype(BF16)))
                 * _dot(x, w3_ref[0, 0, e].astype(BF16))) * c_e
            y = _dot(a.astype(BF16), w2_ref[0, 0, e].astype(BF16))
            if e == 0:
                acc_ref[...] = y
            else:
                acc_ref[...] += y
        out_ref[...] = acc_ref[...].astype(out_ref.dtype)

    @pl.when(i >= n_active_ref[0])
    def _():
        out_ref[...] = jnp.zeros(out_ref.shape, out_ref.dtype)


def _experts(tile_group, n_active, x_sorted, w1, w3, w2, layer):
    ns = x_sorted.shape[0]
    d = w1.shape[3]
    tm = MOE_SORT_TILE
    group_w = lambda w: pl.BlockSpec((1, 1) + w.shape[2:], lambda i, tg, na: (layer, tg[i], 0, 0, 0),
                                     pipeline_mode=pl.Buffered(1))
    grid_spec = pltpu.PrefetchScalarGridSpec(
        num_scalar_prefetch=2,
        grid=(ns // tm,),
        in_specs=[
            pl.BlockSpec((tm, x_sorted.shape[1]), lambda i, tg, na: (i, 0)),
            group_w(w1), group_w(w3), group_w(w2),
        ],
        out_specs=pl.BlockSpec((tm, d), lambda i, tg, na: (i, 0)),
        scratch_shapes=[pltpu.VMEM((tm, d), F32)],
    )
    return pl.pallas_call(
        _experts_kernel,
        grid_spec=grid_spec,
        out_shape=jax.ShapeDtypeStruct((ns, d), BF16),
        compiler_params=_params("arbitrary"),
        name="experts",
    )(tile_group, n_active, x_sorted, w1, w3, w2)


def _residual_kernel(h_ref, y_ref, gf_ref, out_ref, *, final_norm):
    out = h_ref[...] + y_ref[...].astype(F32)
    out_ref[...] = _rms_norm(out, gf_ref[...]) if final_norm else out


def _residual(h, y, gf, final_norm):
    n, d = h.shape
    tm = MOE_TILE
    row = pl.BlockSpec((tm, d), lambda i: (i, 0))
    return pl.pallas_call(
        functools.partial(_residual_kernel, final_norm=final_norm),
        grid=(n // tm,),
        in_specs=[row, row, pl.BlockSpec(gf.shape, lambda i: (0, 0))],
        out_specs=row,
        out_shape=jax.ShapeDtypeStruct((n, d), F32),
        compiler_params=_params("parallel"),
        name="residual",
    )(h, y, gf)


def _group_sort_plan(group_id, rank, counts, tile):
    n = group_id.shape[0]
    n_slots = n + N_EXPERT_GROUPS * tile
    padded = (counts + tile - 1) // tile * tile
    ends = jnp.cumsum(padded)
    slot = (ends - padded)[group_id] + rank
    source = jnp.zeros((n_slots,), jnp.int32).at[slot].set(jnp.arange(n, dtype=jnp.int32))
    tile_start = jnp.arange(n_slots // tile, dtype=jnp.int32) * tile
    tile_group = jnp.minimum(jnp.searchsorted(ends, tile_start, side="right"), N_EXPERT_GROUPS - 1)
    return slot, source, tile_group.astype(jnp.int32), (ends[-1:] // tile).astype(jnp.int32)


def _moe(h, xn, comb, counts, w1, w3, w2, layer, gf, final_norm):
    as_int = lambda a: a.astype(jnp.int32)
    slot, source, tile_group, n_active = _group_sort_plan(
        as_int(comb[:, GROUP_LANE]), as_int(comb[:, RANK_LANE]), as_int(counts[0, :N_EXPERT_GROUPS]),
        MOE_SORT_TILE)
    grouped = lambda w: w.reshape((w.shape[0], N_EXPERT_GROUPS, EXPERTS_PER_GROUP) + w.shape[2:])
    rows = lambda a, idx: jnp.take(a, idx, axis=0, mode="clip")
    y_sorted = _experts(tile_group, n_active, rows(xn, source), grouped(w1), grouped(w3), grouped(w2), layer)
    return _residual(h, rows(y_sorted, slot), gf, final_norm)


def _selection_constants(seq_len):
    ncp = seq_len // CMP_STRIDE
    n_slc = seq_len // SLC_BLOCK
    ratio = SLC_BLOCK // CMP_STRIDE
    lead = CMP_LEN // CMP_STRIDE - 1
    c = np.arange(ncp)[:, None]
    j = np.arange(n_slc)[None, :]
    pool_m = ((c >= ratio * j - lead) & (c < ratio * j + ratio)).astype(np.float32)
    blocks_per_chunk = KEY_CHUNK // SLC_BLOCK
    n_chunks = seq_len // KEY_CHUNK
    pair_m = np.zeros((n_slc, LANES * ((n_chunks + LANES - 1) // LANES)), np.float32)
    pair_m[np.arange(n_slc), np.arange(n_slc) // blocks_per_chunk] = 1.0
    n_words = (n_chunks + WORD_BITS - 1) // WORD_BITS
    bits_m = np.zeros((pair_m.shape[1], LANES), np.float32)
    ch = np.arange(n_chunks)
    bits_m[ch, ch // WORD_BITS] = 2.0 ** (ch % WORD_BITS)
    key_aux = np.zeros((TOKEN_TILE, LANES), np.float32)
    in_chunk = np.arange(TOKEN_TILE) % KEY_CHUNK
    key_aux[np.arange(TOKEN_TILE), HEAD_DIM + in_chunk // SLC_BLOCK] = 1.0
    key_aux[:, HEAD_DIM + AUX_SLOPE] = in_chunk
    cmp_aux = np.zeros((ncp, LANES), np.float32)
    cmp_aux[:, HEAD_DIM] = np.arange(ncp) // CMP_AUX_SPLIT
    cmp_aux[:, HEAD_DIM + 1] = np.arange(ncp) % CMP_AUX_SPLIT
    as_bf16 = lambda a: jnp.asarray(a, BF16)
    return as_bf16(pool_m.T), as_bf16(pair_m), as_bf16(bits_m), jnp.asarray(key_aux), jnp.asarray(cmp_aux), n_words


def kernel(x, norm1_g, w_in, cmp_pe, cmp_w1, cmp_w2, w_nsa_proj, pool_w, pool_scale, w_pool_proj, conv_w,
           w_conv_proj, w_o, norm2_g, router_group_w, router_group_b, router_expert_w, router_expert_b,
           expert_w1, expert_w3, expert_w2, final_norm_g):
    b, t, d = x.shape
    n = b * t
    depth = w_in.shape[0]
    dq = NSA_HEADS * HEAD_DIM
    dkv = 6 * NSA_KV_HEADS * HEAD_DIM
    dgate = NSA_HEADS * NSA_BRANCHES
    cw = d // 4
    assert t % TOKEN_TILE == 0 and n % MOE_TILE == 0 and t % Q_TILE == 0
    n_slc = t // SLC_BLOCK
    n_sel = min(SLC_TOPN, n_slc)
    n_chunks16 = t // CMP_STRIDE
    kvw = NSA_KV_HEADS * HEAD_DIM
    poolt_m, pair_m, bits_m, key_aux, cmp_aux, n_words = _selection_constants(t)
    assert Q_TILE == KEY_CHUNK and AUX_SLOPE < AUX_COLS and KEY_CHUNK <= 256
    assert n_chunks16 <= 256 * CMP_AUX_SPLIT and n_chunks16 % min(CMP_ROWS_STEP, n_chunks16) == 0
    assert n_words <= LANES and pair_m.shape[1] == LANES

    before = jnp.asarray(np.tril(np.ones((TOKEN_TILE, TOKEN_TILE), np.float32), -1), BF16)
    h = x.reshape(n, d)
    for l in range(depth):
        wl = w_in[l]
        o_gate = dq + dkv
        o_pool = o_gate + dgate
        o_merge = o_pool + cw + 3 * cw
        kv_cols = lambda kind: wl[:, dq + kind * kvw:dq + (kind + 1) * kvw]
        wq_t = (wl[:, :dq] * (HEAD_DIM ** -0.5)).T.astype(BF16)
        wv_t = jnp.concatenate([kv_cols(3), kv_cols(5)], axis=1).T.astype(BF16)
        wg_t = jnp.pad(wl[:, o_gate:o_pool], ((0, 0), (0, GATE_PAD - dgate))).T.astype(BF16)
        no_aux = jnp.zeros((d, LANES - HEAD_DIM), F32)
        wk = jnp.concatenate([piece for kind in (2, 4) for gi in range(NSA_KV_HEADS)
                              for piece in (kv_cols(kind)[:, gi * HEAD_DIM:(gi + 1) * HEAD_DIM], no_aux)],
                             axis=1).astype(BF16)
        wn = jnp.concatenate([kv_cols(0), kv_cols(1), wl[:, o_pool:o_merge]], axis=1).astype(BF16)
        wmg = wl[:, o_merge:].astype(BF16)
        pool_bd = jax.scipy.linalg.block_diag(*[pool_w[l, gi] for gi in range(pool_w.shape[1])]).astype(BF16)
        convw = jnp.pad(conv_w[l], ((0, 8 - CONV_K), (0, 0)))
        wr = jnp.pad(jnp.concatenate([router_group_w[l], router_expert_w[l]], axis=1),
                     ((0, 0), (0, ROUTER_PAD - N_EXPERT_GROUPS - N_EXPERTS)))
        wr_hi = wr.astype(BF16)
        wr = jnp.stack([wr_hi, (wr - wr_hi.astype(F32)).astype(BF16)])
        br = jnp.pad(jnp.concatenate([router_group_b[l], router_expert_b[l]]),
                     (0, ROUTER_PAD - N_EXPERT_GROUPS - N_EXPERTS))[None, :]
        pe = jnp.broadcast_to(cmp_pe[l].reshape(2, 1, CMP_LEN * HEAD_DIM), (2, 8, CMP_LEN * HEAD_DIM)).astype(BF16)
        halves = CMP_LEN // CMP_STRIDE
        w1_bd = jnp.einsum("khldc,gq->khlgdqc",
                           cmp_w1[l].reshape(2, halves, CMP_STRIDE, HEAD_DIM, CMP_HIDDEN),
                           jnp.eye(NSA_KV_HEADS, dtype=F32))
        w1_bd = w1_bd.reshape(2, halves, CMP_STRIDE * kvw, NSA_KV_HEADS * CMP_HIDDEN).astype(BF16)
        w2_k = jnp.pad(cmp_w2[l, 0], ((0, 0), (0, LANES - HEAD_DIM))).astype(BF16)
        w2_vt = cmp_w2[l, 1].T.astype(BF16)

        q_t, vst, vwt, gates_t, ks, kw, cmp_src, pool_u, conv = _inproj(
            h, norm1_g[l][None, :], wq_t, wv_t, wg_t, wk, wn, key_aux, b)
        kc_aux, vc_t = _compress(cmp_src, pe, cmp_w1[l].astype(BF16), w1_bd, w2_k, w2_vt, cmp_aux)
        oc_t, sel_t, words = _nsa_cmp(q_t, kc_aux, vc_t, poolt_m, pair_m, bits_m, n_sel)
        words = words[:, :, :, 0, :n_words].reshape(-1)
        nsa = _nsa_slc(words, q_t, ks, vst, kw, vwt, sel_t, oc_t, gates_t, n_words)
        h, xn, comb, counts = _merge(
            h, norm1_g[l][None, :], nsa.reshape(n, dq), pool_u, conv, wmg,
            w_nsa_proj[l].astype(BF16), pool_bd, pool_scale[l][None, :], w_pool_proj[l].astype(BF16),
            convw, w_conv_proj[l].astype(BF16), w_o[l].astype(BF16), norm2_g[l][None, :], wr, br, before, t)
        h = _moe(h, xn, comb, counts, expert_w1, expert_w3, expert_w2, l,
                 final_norm_g[None, :], final_norm=(l == depth - 1))
    return h.reshape(b, t, d)
```

```python
import functools

import jax
import jax.numpy as jnp
import numpy as np
from jax import lax
from jax.experimental import pallas as pl
from jax.experimental.pallas import tpu as pltpu

F32 = jnp.float32
BF16 = jnp.bfloat16

HEAD_DIM = 64
NSA_HEADS = 8
NSA_KV_HEADS = 2
NSA_GROUP = NSA_HEADS // NSA_KV_HEADS
CMP_LEN = 32
CMP_STRIDE = 16
CMP_HIDDEN = 4 * HEAD_DIM
SLC_BLOCK = 64
SLC_TOPN = 16
WINDOW = 512
NSA_BRANCHES = 3
POOL_WINDOWS = (2, 4, 8, 16)
CONV_K = 3
N_EXPERT_GROUPS = 4
EXPERTS_PER_GROUP = 8
N_EXPERTS = N_EXPERT_GROUPS * EXPERTS_PER_GROUP
RMS_EPS = 1e-6
NEG_INF = -1e30
ALIBI_SLOPES = tuple(float(2.0 ** (-8.0 * (h + 1) / NSA_HEADS)) for h in range(NSA_HEADS))

LANES = 128
VMEM_LIMIT = 56 * 1024 * 1024
TOKEN_TILE = 512
MOE_TILE = 1024
MOE_SORT_TILE = 512
Q_TILE = 128
KEY_CHUNK = 128
GATE_PAD = LANES
ROUTER_PAD = LANES
WORD_BITS = 16


def _params(*semantics):
    return pltpu.CompilerParams(dimension_semantics=semantics, vmem_limit_bytes=VMEM_LIMIT)


def _dot(a, b):
    return jnp.dot(a, b, preferred_element_type=F32)


def _dot_nt(a, b):
    return lax.dot_general(a, b, (((1,), (1,)), ((), ())), preferred_element_type=F32)


def _rms_norm(x, g):
    y = x * lax.rsqrt(jnp.mean(x * x, axis=-1, keepdims=True) + RMS_EPS)
    return y * g


def _iota(shape, dim):
    return lax.broadcasted_iota(jnp.int32, shape, dim)


def _inproj_kernel(x_ref, g_ref, wq_ref, wv_ref, wg_ref, wk_ref, wn_ref, kaux_ref,
                   q_ref, vs_ref, vw_ref, gate_ref, ks_ref, kw_ref, cmp_ref, pool_ref, conv_ref):
    xn = _rms_norm(x_ref[...], g_ref[...]).astype(BF16)
    sub_tiles = x_ref.shape[0] // Q_TILE
    q_t = _dot_nt(wq_ref[...], xn)
    for g in range(NSA_KV_HEADS):
        for j in range(sub_tiles):
            for r in range(NSA_GROUP):
                head = g * NSA_GROUP + r
                q_ref[0, g, j, :, r * Q_TILE:(r + 1) * Q_TILE] = q_t[
                    head * HEAD_DIM:(head + 1) * HEAD_DIM, j * Q_TILE:(j + 1) * Q_TILE].astype(BF16)
    v_t = _dot_nt(wv_ref[...], xn)
    k = _dot(xn, wk_ref[...])
    for branch, (v_ref, k_ref) in enumerate(((vs_ref, ks_ref), (vw_ref, kw_ref))):
        for g in range(NSA_KV_HEADS):
            slab = branch * NSA_KV_HEADS + g
            for j in range(sub_tiles):
                v_ref[0, g, j] = v_t[slab * HEAD_DIM:(slab + 1) * HEAD_DIM,
                                     j * KEY_CHUNK:(j + 1) * KEY_CHUNK].astype(BF16)
            k_ref[0, g] = (k[:, slab * LANES:(slab + 1) * LANES] + kaux_ref[...]).astype(BF16)
    gate_ref[0] = _dot_nt(wg_ref[...], xn)
    col = 0
    for ref in (cmp_ref, pool_ref, conv_ref):
        width = ref.shape[-1]
        ref[...] = _dot(xn, wn_ref[:, col:col + width]).reshape(ref.shape)
        col += width


def _inproj(h, g, wq_t, wv_t, wg_t, wk, wn, kaux, batch):
    n, d = h.shape
    t = n // batch
    tm = TOKEN_TILE
    steps = t // tm
    sub = tm // Q_TILE
    cw = d // 4
    gq = NSA_GROUP * Q_TILE
    full = lambda a: pl.BlockSpec(a.shape, lambda i: (0,) * a.ndim)
    row = lambda width: pl.BlockSpec((tm, width), lambda i: (i, 0))
    tiles = lambda rows, cols: pl.BlockSpec((1, NSA_KV_HEADS, sub, rows, cols),
                                            lambda i: (i // steps, 0, i % steps, 0, 0))
    keys = pl.BlockSpec((1, NSA_KV_HEADS, tm, LANES), lambda i: (i // steps, 0, i % steps, 0))
    sds = jax.ShapeDtypeStruct
    v_shape = sds((batch, NSA_KV_HEADS, t // KEY_CHUNK, HEAD_DIM, KEY_CHUNK), BF16)
    k_shape = sds((batch, NSA_KV_HEADS, t, LANES), BF16)
    return pl.pallas_call(
        _inproj_kernel,
        grid=(n // tm,),
        in_specs=[row(d), full(g), full(wq_t), full(wv_t), full(wg_t), full(wk), full(wn), full(kaux)],
        out_specs=[tiles(HEAD_DIM, gq), tiles(HEAD_DIM, KEY_CHUNK), tiles(HEAD_DIM, KEY_CHUNK),
                   pl.BlockSpec((1, GATE_PAD, tm), lambda i: (i // steps, 0, i % steps)),
                   keys, keys,
                   pl.BlockSpec((1, tm, 2 * NSA_KV_HEADS * HEAD_DIM), lambda i: (i // steps, i % steps, 0)),
                   row(cw), row(3 * cw)],
        out_shape=[sds((batch, NSA_KV_HEADS, t // Q_TILE, HEAD_DIM, gq), BF16), v_shape, v_shape,
                   sds((batch, GATE_PAD, t), F32), k_shape, k_shape,
                   sds((batch, t, 2 * NSA_KV_HEADS * HEAD_DIM), F32), sds((n, cw), F32), sds((n, 3 * cw), F32)],
        compiler_params=_params("parallel"),
        name="inproj",
    )(h, g, wq_t, wv_t, wg_t, wk, wn, kaux)


def _gelu_tanh(x):
    return 0.5 * x * (1.0 + jnp.tanh(0.7978845608028654 * (x + 0.044715 * x * x * x)))


def _compress_kernel(src_ref, pe_ref, w1_ref, w1bd_ref, w2k_ref, w2vt_ref, caux_ref, kc_ref, vct_ref):
    kind = pl.program_id(1)
    ncp = kc_ref.shape[2]
    hidden = w1_ref.shape[2]
    pieces = [src_ref[0, pl.ds(l, ncp, stride=CMP_STRIDE), :].astype(BF16) for l in range(CMP_STRIDE)]
    chunk = jnp.concatenate(pieces, axis=1)
    first = _dot(chunk, w1bd_ref[0, 0])
    second = _dot(chunk, w1bd_ref[0, 1])
    bias = _dot(pe_ref[0], w1_ref[0])[0:1, :]
    hid = first + pltpu.roll(second, ncp - 1, 0) + jnp.concatenate([bias] * NSA_KV_HEADS, axis=1)
    row = _iota((ncp, 1), 0)
    act = jnp.where(row < ncp - 1, _gelu_tanh(hid), 0.0).astype(BF16)
    for g in range(NSA_KV_HEADS):
        act_g = act[:, g * hidden:(g + 1) * hidden]

        @pl.when(kind == 0)
        def _():
            kc_ref[0, g] = (_dot(act_g, w2k_ref[...]) + caux_ref[...]).astype(BF16)

        @pl.when(kind == 1)
        def _():
            vct_ref[0, g * HEAD_DIM:(g + 1) * HEAD_DIM, :] = _dot_nt(w2vt_ref[...], act_g).astype(BF16)


def _compress(src, pe, w1, w1bd, w2k, w2vt, caux):
    b, t, _ = src.shape
    ncp = t // CMP_STRIDE
    gd = NSA_KV_HEADS * HEAD_DIM
    full = lambda a: pl.BlockSpec(a.shape, lambda bi, k: (0,) * a.ndim)
    per_kind = lambda a: pl.BlockSpec((1,) + a.shape[1:], lambda bi, k: (k,) + (0,) * (a.ndim - 1))
    return pl.pallas_call(
        _compress_kernel,
        grid=(b, 2),
        in_specs=[pl.BlockSpec((1, t, gd), lambda bi, k: (bi, 0, k)),
                  per_kind(pe), per_kind(w1), per_kind(w1bd), full(w2k), full(w2vt), full(caux)],
        out_specs=[pl.BlockSpec((1, NSA_KV_HEADS, ncp, LANES), lambda bi, k: (bi, 0, 0, 0)),
                   pl.BlockSpec((1, gd, ncp), lambda bi, k: (bi, 0, 0))],
        out_shape=[jax.ShapeDtypeStruct((b, NSA_KV_HEADS, ncp, LANES), BF16),
                   jax.ShapeDtypeStruct((b, gd, ncp), BF16)],
        compiler_params=_params("parallel", "arbitrary"),
        name="compress",
    )(src, pe, w1, w1bd, w2k, w2vt, caux)


SOFTMAX_FLOOR = -1e29
TAKEN = -3e38
ATTN_BATCH = 4
BLOCKS_PER_CHUNK = KEY_CHUNK // SLC_BLOCK
AUX_COLS = 16
AUX_SLOPE = BLOCKS_PER_CHUNK
ONES_ROWS = 16
CMP_ROWS_STEP = 128
CMP_AUX_SPLIT = 128


def _slope(g, r):
    if isinstance(g, int):
        return jnp.float32(ALIBI_SLOPES[g * NSA_GROUP + r])
    s = jnp.float32(ALIBI_SLOPES[r])
    for gi in range(1, NSA_KV_HEADS):
        s = jnp.where(g == gi, jnp.float32(ALIBI_SLOPES[gi * NSA_GROUP + r]), s)
    return s


def _nsa_cmp_kernel(qt_ref, kc_ref, vct_ref, poolt_ref, pair_ref, bits_ref, oct_ref, selt_ref, words_ref,
                    *, n_sel):
    i = pl.program_id(1)
    qt = selt_ref.shape[4]
    ncp = kc_ref.shape[2]
    n_slc = poolt_ref.shape[0]
    gq = qt_ref.shape[4]
    start = i * qt
    t = start + _iota((1, qt), 1)
    aux_row = _iota((AUX_COLS, gq), 0)
    col_head = _iota((1, gq), 1) // qt
    pad_rows = jnp.zeros((kc_ref.shape[3] - qt_ref.shape[3] - AUX_COLS, gq), BF16)

    def weights(g):
        slope_cols = jnp.zeros((1, gq), F32)
        for r in range(NSA_GROUP):
            slope_cols = jnp.where(col_head == r, _slope(g, r), slope_cols)
        aux = jnp.where(aux_row == 0, slope_cols * (CMP_STRIDE * CMP_AUX_SPLIT),
                        jnp.where(aux_row == 1, slope_cols * CMP_STRIDE, 0.0))
        return jnp.concatenate([qt_ref[0, g, 0], aux.astype(BF16), pad_rows], axis=0)

    def importance(g, nr, nb):
        s = _dot(kc_ref[0, g, :nr, :], weights(g))
        edge = min(nr, 2 * CMP_ROWS_STEP)
        cmp_end = ((nr - edge) + _iota((edge, 1), 0)) * CMP_STRIDE + (CMP_LEN - 1)
        visible = cmp_end <= t
        vct = vct_ref[0, g * HEAD_DIM:(g + 1) * HEAD_DIM, :nr]
        psum = jnp.zeros((nr, qt), F32)
        for r in range(NSA_GROUP):
            cols = slice(r * qt, (r + 1) * qt)
            sr = s[:, cols]
            tail = jnp.where(visible, sr[nr - edge:], NEG_INF)
            sr = tail if edge == nr else jnp.concatenate([sr[:nr - edge], tail], axis=0)
            m = jnp.maximum(jnp.max(sr, axis=0, keepdims=True), SOFTMAX_FLOOR)
            e = jnp.exp(sr - m)
            l = jnp.sum(e, axis=0, keepdims=True)
            inv = jnp.where(l > 0.0, 1.0 / l, 0.0)
            oct_ref[0, g, 0, :, cols] = (_dot(vct, e.astype(BF16)) * inv).astype(BF16)
            psum = psum + e * inv
        return _dot(poolt_ref[:nb, :nr], psum.astype(BF16))

    def visible_prefix(nr):
        nb = min(n_slc, nr * CMP_STRIDE // SLC_BLOCK)
        imp = jnp.concatenate([importance(g, nr, nb) for g in range(NSA_KV_HEADS)], axis=1)
        blk = _iota((nb, 1), 0)
        cur = jnp.concatenate([t // SLC_BLOCK] * NSA_KV_HEADS, axis=1)
        forced = (blk == 0) | (blk == cur) | (blk == cur - 1)
        score = jnp.where(forced, TAKEN, jnp.where(blk <= cur, imp, NEG_INF))
        n_forced = 1 + jnp.where(cur >= 1, 1, 0) + jnp.where(cur >= 2, 1, 0)
        blk_f = blk.astype(F32)

        def take_one(score, active):
            m = jnp.max(score, axis=0, keepdims=True)
            first = jnp.min(jnp.where(score == m, blk_f, F32(1e9)), axis=0, keepdims=True)
            hit = (blk_f == first) if active is None else ((blk_f == first) & active)
            return jnp.where(hit, TAKEN, score)

        common_rounds = max(n_sel - 3, 0)
        for _ in range(common_rounds):
            score = take_one(score, None)

        def early_rounds(score):
            for k in range(common_rounds, n_sel - 1):
                score = take_one(score, n_sel - n_forced > k)
            return score

        score = lax.cond(start < 2 * SLC_BLOCK, early_rounds, lambda sc: sc, score)
        for g in range(NSA_KV_HEADS):
            sel_g = score[:, g * qt:(g + 1) * qt] == TAKEN
            selt_ref[0, g, 0, :nb, :] = jnp.where(sel_g, 0.0, NEG_INF)
            if nb < n_slc:
                selt_ref[0, g, 0, nb:, :] = jnp.full((n_slc - nb, qt), NEG_INF, F32)
            count = _dot_nt(jnp.ones((8, qt), BF16), jnp.where(sel_g, 1.0, 0.0).astype(BF16))
            used = jnp.where(count > 0.0, 1.0, 0.0).astype(BF16)
            chunk_used = jnp.where(_dot(used, pair_ref[:nb, :]) > 0.0, 1.0, 0.0).astype(BF16)
            words_ref[0, g, 0] = _dot(chunk_used, bits_ref[...]).astype(jnp.int32)

    step = min(CMP_ROWS_STEP, ncp)
    rows_needed = jnp.minimum((start + qt - CMP_LEN) // CMP_STRIDE + 1, ncp)
    n_steps = (rows_needed + step - 1) // step
    for k in range(ncp // step):
        pl.when(n_steps == k + 1)(functools.partial(visible_prefix, (k + 1) * step))


def _nsa_cmp(q_t, kc, vct, poolt_m, pair_m, bits_m, n_sel):
    b, ng, n_tiles, qrows, qcols = q_t.shape
    ncp = kc.shape[2]
    n_slc = poolt_m.shape[0]
    const = lambda shape: pl.BlockSpec(shape, lambda bi, i: (0,) * len(shape))
    tile5 = lambda rows, cols: pl.BlockSpec((1, ng, 1, rows, cols), lambda bi, i: (bi, 0, i, 0, 0))
    return pl.pallas_call(
        functools.partial(_nsa_cmp_kernel, n_sel=n_sel),
        grid=(b, n_tiles),
        in_specs=[
            tile5(qrows, qcols),
            pl.BlockSpec((1, ng, ncp, kc.shape[3]), lambda bi, i: (bi, 0, 0, 0)),
            pl.BlockSpec((1, ng * HEAD_DIM, ncp), lambda bi, i: (bi, 0, 0)),
            const(poolt_m.shape), const(pair_m.shape), const(bits_m.shape),
        ],
        out_specs=[tile5(HEAD_DIM, qcols), tile5(n_slc, Q_TILE), tile5(8, LANES)],
        out_shape=[
            jax.ShapeDtypeStruct((b, ng, n_tiles, HEAD_DIM, qcols), BF16),
            jax.ShapeDtypeStruct((b, ng, n_tiles, n_slc, Q_TILE), F32),
            jax.ShapeDtypeStruct((b, ng, n_tiles, 8, LANES), jnp.int32),
        ],
        compiler_params=_params("parallel", "parallel"),
        name="nsa_compressed",
    )(q_t, kc, vct, poolt_m, pair_m, bits_m)


def _nsa_slc_kernel(words_ref, qt_ref, ks_ref, vst_ref, kw_ref, vwt_ref, selt_ref, oct_ref, gt_ref,
                    out_ref, m_ref, l_ref, acc_ref, ow_ref, sa_ref, sb_ref, sw_ref, list_ref, *, words_per_tile):
    bi = pl.program_id(0)
    g = pl.program_id(1)
    i = pl.program_id(2)
    n_tiles = pl.num_programs(2)
    qt = out_ref.shape[1]
    q_rows = qt_ref[0, 0, 0]
    gq = q_rows.shape[1]
    start = i * qt
    lane_f = _iota((1, qt), 1).astype(F32)
    key_in_chunk = _iota((KEY_CHUNK, qt), 0)
    query_in_tile = _iota((KEY_CHUNK, qt), 1)

    aux_row = _iota((AUX_COLS, gq), 0)
    col_head = _iota((1, gq), 1) // qt
    slope_cols = jnp.zeros((1, gq), F32)
    for r in range(NSA_GROUP):
        slope_cols = jnp.where(col_head == r, _slope(g, r), slope_cols)
    aux_base = jnp.where(aux_row == AUX_SLOPE, slope_cols, 0.0)
    pad_rows = jnp.zeros((ks_ref.shape[3] - q_rows.shape[0] - AUX_COLS, gq), BF16)

    def scores_of(slots):
        return [_dot(k, jnp.concatenate([q_rows, aux.astype(BF16), pad_rows], axis=0))
                for k, aux, _, _, _ in slots]

    def values_of(slots):
        values = jnp.concatenate([v for _, _, v, _, _ in slots], axis=1)
        return jnp.concatenate([values, jnp.ones((ONES_ROWS, values.shape[1]), BF16)], axis=0)

    def softmax_step(slots, scores, v_cat, r, m_old):
        cols = slice(r * qt, (r + 1) * qt)
        srs, tops = [], []
        for j, (_, _, _, shift, mask) in enumerate(slots):
            sr = scores[j, :, cols] if hasattr(scores, "at") else scores[j][:, cols]
            sr = sr if mask is None else jnp.where(mask, sr, NEG_INF)
            srs.append(sr)
            tops.append(jnp.max(sr, axis=0, keepdims=True) + shift[r])
        m_new = functools.reduce(jnp.maximum, tops, m_old)
        ps = [jnp.exp((sr - (m_new - slot[3][r])).astype(BF16)) for slot, sr in zip(slots, srs)]
        weighted = _dot(v_cat, jnp.concatenate(ps, axis=0))
        return m_new, weighted[HEAD_DIM:HEAD_DIM + 1], weighted[:HEAD_DIM]

    def shifts(dist0, ok):
        rows = [-_slope(g, r) * (dist0 + lane_f) for r in range(NSA_GROUP)]
        return rows if ok is None else [jnp.where(ok, row, NEG_INF) for row in rows]

    word_base = ((bi * NSA_KV_HEADS + g) * n_tiles + i) * words_per_tile
    list_ref[0] = 0

    def scan_word(w, n):
        word = words_ref[word_base + w]

        def scan_bits(n):
            for bit in range(WORD_BITS):
                c = w * WORD_BITS + bit
                list_ref[n] = c
                n = n + jnp.where(c < i, (word >> bit) & 1, 0)
            return n

        return lax.cond(word != 0, scan_bits, lambda n: n, n)

    n_listed = lax.fori_loop(0, (i + WORD_BITS - 1) // WORD_BITS, scan_word, 0)

    def selected_slot(c, ok, mask):
        at = pl.multiple_of(c * KEY_CHUNK, KEY_CHUNK)
        bias = selt_ref[0, 0, 0, pl.ds(c * BLOCKS_PER_CHUNK, BLOCKS_PER_CHUNK), :]
        aux = aux_base
        for blk in range(BLOCKS_PER_CHUNK):
            aux = jnp.where(aux_row == blk, jnp.concatenate([bias[blk:blk + 1]] * NSA_GROUP, axis=1), aux)
        return (ks_ref[0, 0, pl.ds(at, KEY_CHUNK), :], aux, vst_ref[0, 0, c],
                shifts((start - c * KEY_CHUNK).astype(F32), ok), mask)

    def listed_slot(idx):
        ok = idx < n_listed
        c = jnp.where(ok, list_ref[jnp.minimum(idx, jnp.maximum(n_listed - 1, 0))], 0)
        return selected_slot(c, ok, None)

    floor = jnp.full((1, qt), SOFTMAX_FLOOR, F32)

    n_back = WINDOW // KEY_CHUNK
    slots = []
    for j in range(n_back + 1):
        cs = start - WINDOW + j * KEY_CHUNK
        chunk = jnp.maximum(cs, 0) // KEY_CHUNK
        at = pl.multiple_of(chunk * KEY_CHUNK, KEY_CHUNK)
        mask = (query_in_tile < key_in_chunk) if j == 0 else (
            (key_in_chunk <= query_in_tile) if j == n_back else None)
        slots.append((kw_ref[0, 0, pl.ds(at, KEY_CHUNK), :], aux_base, vwt_ref[0, 0, chunk],
                      shifts(F32(WINDOW - j * KEY_CHUNK), cs >= 0), mask))
    window_slots = slots
    first_slots = [selected_slot(i, None, key_in_chunk <= query_in_tile)] + [
        listed_slot(j) for j in range(ATTN_BATCH - 1)]

    def listed_batch(n):
        return [listed_slot(ATTN_BATCH - 1 + n * ATTN_BATCH + j) for j in range(ATTN_BATCH)]

    def issue(slots, dst_ref):
        for j, s in enumerate(scores_of(slots)):
            dst_ref[j] = s

    issue(window_slots, sw_ref)
    issue(first_slots, sb_ref)
    v_cat = values_of(window_slots)
    for r in range(NSA_GROUP):
        _, total, weighted = softmax_step(window_slots, sw_ref, v_cat, r, floor)
        ow_ref[:, r * qt:(r + 1) * qt] = weighted * jnp.where(total > 0.0, 1.0 / total, 0.0)

    def consume(slots, src_ref):
        v_cat = values_of(slots)
        for r in range(NSA_GROUP):
            cols = slice(r * qt, (r + 1) * qt)
            m_old = m_ref[r]
            m_new, total, weighted = softmax_step(slots, src_ref, v_cat, r, m_old)
            alpha = jnp.exp(m_old - m_new)
            l_ref[r] = alpha * l_ref[r] + total
            acc_ref[:, cols] = alpha * acc_ref[:, cols] + weighted
            m_ref[r] = m_new

    issue(listed_batch(0), sa_ref)
    v_cat = values_of(first_slots)
    for r in range(NSA_GROUP):
        m_ref[r], l_ref[r], acc_ref[:, r * qt:(r + 1) * qt] = softmax_step(first_slots, sb_ref, v_cat, r, floor)

    def batch_pair(it, carry):
        first, second, third = (listed_batch(2 * it + n) for n in range(3))
        issue(second, sb_ref)
        consume(first, sa_ref)
        issue(third, sa_ref)
        consume(second, sb_ref)
        return carry

    n_rest = jnp.maximum(n_listed - (ATTN_BATCH - 1), 0)
    n_batches = (n_rest + ATTN_BATCH - 1) // ATTN_BATCH
    lax.fori_loop(0, n_batches // 2, batch_pair, 0)

    @pl.when(n_batches % 2 == 1)
    def _():
        consume(listed_batch(n_batches - 1), sa_ref)

    def finalize(r):
        l = l_ref[r]
        return acc_ref[:, r * qt:(r + 1) * qt] * jnp.where(l > 0.0, 1.0 / l, 0.0)

    outs = []
    for r in range(NSA_GROUP):
        cols = slice(r * qt, (r + 1) * qt)
        col = (g * NSA_GROUP + r) * NSA_BRANCHES
        gate = lambda br: jax.nn.sigmoid(gt_ref[0, pl.ds(col + br, 1), :])
        outs.append(gate(0) * oct_ref[0, 0, 0, :, cols].astype(F32) + gate(1) * finalize(r)
                    + gate(2) * ow_ref[:, cols])
    out_ref[0] = jnp.concatenate(outs, axis=0).T.astype(out_ref.dtype)


def _nsa_slc(words, q_t, ks, vst, kw, vwt, selt, oct, gates_t, words_per_tile):
    b, _, t, kw_cols = ks.shape
    _, _, n_tiles, qrows, qcols = q_t.shape
    n_slc = selt.shape[3]
    n_chunks = vst.shape[2]
    gw = NSA_GROUP * HEAD_DIM
    once = dict(pipeline_mode=pl.Buffered(1))
    k_spec = pl.BlockSpec((1, 1, t, kw_cols), lambda bi, g, i, w: (bi, g, 0, 0), **once)
    vt_spec = pl.BlockSpec((1, 1, n_chunks, HEAD_DIM, KEY_CHUNK), lambda bi, g, i, w: (bi, g, 0, 0, 0), **once)
    tile5 = lambda rows, cols: pl.BlockSpec((1, 1, 1, rows, cols), lambda bi, g, i, w: (bi, g, i, 0, 0))
    grid_spec = pltpu.PrefetchScalarGridSpec(
        num_scalar_prefetch=1,
        grid=(b, NSA_KV_HEADS, n_tiles),
        in_specs=[
            tile5(qrows, qcols),
            k_spec, vt_spec, k_spec, vt_spec,
            tile5(n_slc, Q_TILE),
            tile5(HEAD_DIM, qcols),
            pl.BlockSpec((1, GATE_PAD, Q_TILE), lambda bi, g, i, w: (bi, 0, i)),
        ],
        out_specs=pl.BlockSpec((1, Q_TILE, gw), lambda bi, g, i, w: (bi, i, g)),
        scratch_shapes=[pltpu.VMEM((NSA_GROUP, 1, Q_TILE), F32), pltpu.VMEM((NSA_GROUP, 1, Q_TILE), F32),
                        pltpu.VMEM((HEAD_DIM, qcols), F32), pltpu.VMEM((HEAD_DIM, qcols), F32),
                        pltpu.VMEM((ATTN_BATCH, KEY_CHUNK, qcols), F32),
                        pltpu.VMEM((ATTN_BATCH, KEY_CHUNK, qcols), F32),
                        pltpu.VMEM((WINDOW // KEY_CHUNK + 1, KEY_CHUNK, qcols), F32),
                        pltpu.SMEM((n_chunks,), jnp.int32)],
    )
    return pl.pallas_call(
        functools.partial(_nsa_slc_kernel, words_per_tile=words_per_tile),
        grid_spec=grid_spec,
        out_shape=jax.ShapeDtypeStruct((b, t, NSA_HEADS * HEAD_DIM), BF16),
        compiler_params=_params("parallel", "parallel", "parallel"),
        name="nsa_selected_window",
    )(words, q_t, ks, vst, kw, vwt, selt, oct, gates_t)


POOL_HALO = 16
CONV_HALO = 8


def _merge_kernel(h_ref, g_ref, nsa_ref, pool_ref, pool_halo_ref, conv_ref, conv_halo_ref,
                  wmg_ref, wnsa_ref, pool_bd_ref, pool_scale_ref, wpool_ref, convw_ref, wconv_ref, wo_ref,
                  g2_ref, wr_ref, br_ref, before_ref,
                  out_ref, xn_ref, comb_ref, count_ref, pool_ext, conv_ext, *, seq_len):
    i = pl.program_id(0)
    tm, d = h_ref.shape
    cw = pool_ref.shape[1]
    pos0 = (i * tm) % seq_len
    keep_halo = jnp.where(pos0 == 0, 0.0, 1.0)
    pos = pos0 + _iota((tm, 1), 0)

    u = pool_ref[...]
    pool_ext[0:POOL_HALO, :] = pool_halo_ref[...] * keep_halo
    pool_ext[POOL_HALO:, :] = u
    lane_group = _iota((1, cw), 1) // (cw // len(POOL_WINDOWS))
    total = u
    mean = jnp.zeros_like(u)
    done = 1
    for gi, win in enumerate(POOL_WINDOWS):
        for k in range(done, win):
            total = total + pool_ext[POOL_HALO - k:POOL_HALO - k + tm, :]
        done = win
        cnt = jnp.minimum(pos + 1, win).astype(F32)
        mean = jnp.where(lane_group == gi, total / cnt, mean)
    pooled = (mean - u).astype(BF16)
    mixed = _dot(pooled, pool_bd_ref[...]) * pool_scale_ref[...]
    y_pool = _dot(mixed.astype(BF16), wpool_ref[...])

    ch = conv_ref[:, 0:cw]
    cb = conv_ref[:, cw:2 * cw]
    cc = conv_ref[:, 2 * cw:3 * cw]
    conv_ext[0:CONV_HALO, :] = conv_halo_ref[:, 0:cw] * conv_halo_ref[:, 2 * cw:3 * cw] * keep_halo
    conv_ext[CONV_HALO:, :] = cc * ch
    y = jnp.zeros((tm, cw), F32)
    for k in range(CONV_K):
        off = CONV_HALO - (CONV_K - 1) + k
        y = y + convw_ref[k:k + 1, :] * conv_ext[off:off + tm, :]
    y_conv = _dot((cb * y).astype(BF16), wconv_ref[...])

    y_nsa = _dot(nsa_ref[...], wnsa_ref[...])

    h = h_ref[...]
    xn = _rms_norm(h, g_ref[...]).astype(BF16)
    merged = jnp.zeros((tm, d), F32)
    for br, y_br in enumerate((y_nsa, y_pool, y_conv)):
        mg = jax.nn.sigmoid(_dot(xn, wmg_ref[:, br * d:(br + 1) * d]))
        merged = merged + mg * y_br
    h_new = h + _dot(merged.astype(BF16), wo_ref[...])
    out_ref[...] = h_new
    _route_tile(h_new, pos0 == 0, g2_ref, wr_ref, br_ref, before_ref, xn_ref, comb_ref, count_ref)


def _merge(h, g, nsa, pool_u, conv, wmg, wnsa, pool_bd, pool_scale, wpool, convw, wconv, wo,
           g2, wr, br, before, seq_len):
    n, d = h.shape
    tm = TOKEN_TILE
    cw = pool_u.shape[1]
    row = lambda width: pl.BlockSpec((tm, width), lambda i: (i, 0))
    full = lambda a: pl.BlockSpec(a.shape, lambda i: (0,) * a.ndim)
    halo = lambda rows, width: pl.BlockSpec(
        (rows, width), lambda i: (jnp.maximum(i * (tm // rows) - 1, 0), 0))
    return pl.pallas_call(
        functools.partial(_merge_kernel, seq_len=seq_len),
        grid=(n // tm,),
        in_specs=[row(d), full(g), row(nsa.shape[1]), row(cw), halo(POOL_HALO, cw),
                  row(conv.shape[1]), halo(CONV_HALO, conv.shape[1]),
                  full(wmg), full(wnsa), full(pool_bd), full(pool_scale), full(wpool), full(convw),
                  full(wconv), full(wo), full(g2), full(wr), full(br), full(before)],
        out_specs=[row(d), row(d + ROUTER_PAD), row(ROUTER_PAD),
                   pl.BlockSpec((1, 8, ROUTER_PAD), lambda i: (i // (seq_len // tm), 0, 0))],
        out_shape=[jax.ShapeDtypeStruct((n, d), F32), jax.ShapeDtypeStruct((n, d + ROUTER_PAD), BF16),
                   jax.ShapeDtypeStruct((n, ROUTER_PAD), F32),
                   jax.ShapeDtypeStruct((n // seq_len, 8, ROUTER_PAD), F32)],
        scratch_shapes=[pltpu.VMEM((tm + POOL_HALO, cw), F32), pltpu.VMEM((tm + CONV_HALO, cw), F32)],
        compiler_params=_params("arbitrary"),
        name="merge",
    )(h, g, nsa, pool_u, pool_u, conv, conv, wmg, wnsa, pool_bd, pool_scale, wpool, convw, wconv, wo,
      g2, wr, br, before)


def _route(logits):
    lane = _iota(logits.shape, 1)
    lane_f = lane.astype(F32)
    big = F32(1e9)
    is_group = lane < N_EXPERT_GROUPS
    gl = jnp.where(is_group, logits, NEG_INF)
    g_max = jnp.max(gl, axis=1, keepdims=True)
    g_sel = jnp.min(jnp.where(gl == g_max, lane_f, big), axis=1, keepdims=True)
    g_prob = 1.0 / jnp.sum(jnp.where(is_group, jnp.exp(gl - g_max), 0.0), axis=1, keepdims=True)
    lo = N_EXPERT_GROUPS + EXPERTS_PER_GROUP * g_sel
    in_group = (lane_f >= lo) & (lane_f < lo + EXPERTS_PER_GROUP)
    el = jnp.where(in_group, logits, NEG_INF)
    v1 = jnp.max(el, axis=1, keepdims=True)
    i1 = jnp.min(jnp.where((el == v1) & in_group, lane_f, big), axis=1, keepdims=True)
    el2 = jnp.where(lane_f == i1, NEG_INF, el)
    rest = in_group & (lane_f != i1)
    v2 = jnp.max(el2, axis=1, keepdims=True)
    i2 = jnp.min(jnp.where((el2 == v2) & rest, lane_f, big), axis=1, keepdims=True)
    e2 = jnp.exp(v2 - v1)
    w1 = g_prob / (1.0 + e2)
    w2 = g_prob * e2 / (1.0 + e2)
    return jnp.where(lane_f == i1, w1, 0.0) + jnp.where(lane_f == i2, w2, 0.0), g_sel


GROUP_LANE = N_EXPERT_GROUPS + N_EXPERTS


RANK_LANE = GROUP_LANE + 1


def _route_tile(h, first_of_sequence, g_ref, wr_ref, br_ref, before_ref, xn_ref, comb_ref, count_ref):
    @pl.when(first_of_sequence)
    def _():
        count_ref[...] = jnp.zeros(count_ref.shape, F32)

    xn = _rms_norm(h, g_ref[...])
    xn_hi = xn.astype(BF16)
    xn_lo = (xn - xn_hi.astype(F32)).astype(BF16)
    logits = (_dot(xn_hi, wr_ref[0]) + (_dot(xn_hi, wr_ref[1]) + _dot(xn_lo, wr_ref[0]))) + br_ref[...]
    comb, g_sel = _route(logits)
    lane = _iota(comb.shape, 1)
    chose = jnp.where(lane.astype(F32) == g_sel, 1.0, 0.0)
    earlier = _dot(before_ref[...], chose.astype(BF16)) + count_ref[0, 0:1, :]
    rank = jnp.sum(chose * earlier, axis=1, keepdims=True)
    count_ref[0, 0:1, :] = count_ref[0, 0:1, :] + jnp.sum(chose, axis=0, keepdims=True)
    comb_ref[...] = jnp.where(lane == GROUP_LANE, g_sel, jnp.where(lane == RANK_LANE, rank, comb))
    first = N_EXPERT_GROUPS + EXPERTS_PER_GROUP * g_sel
    local = jnp.zeros(comb.shape, F32)
    for e in range(EXPERTS_PER_GROUP):
        c_e = jnp.sum(jnp.where(lane.astype(F32) == first + e, comb, 0.0), axis=1, keepdims=True)
        local = jnp.where((lane == e) | (lane == EXPERTS_PER_GROUP + e), c_e, local)
    local_hi = local.astype(BF16)
    d = xn_hi.shape[1]
    xn_ref[:, :d] = xn_hi
    xn_ref[:, d:] = jnp.where(lane < EXPERTS_PER_GROUP, local_hi, (local - local_hi.astype(F32)).astype(BF16))


def _experts_kernel(tile_group_ref, n_active_ref, x_ref, w1_ref, w3_ref, w2_ref, out_ref, acc_ref):
    i = pl.program_id(0)

    @pl.when(i < n_active_ref[0])
    def _():
        d = out_ref.shape[1]
        x = x_ref[:, :d]
        comb = x_ref[:, d:].astype(F32)
        lane = _iota(comb.shape, 1)
        for e in range(EXPERTS_PER_GROUP):
            c_e = jnp.sum(jnp.where((lane == e) | (lane == EXPERTS_PER_GROUP + e), comb, 0.0),
                          axis=1, keepdims=True)
            a = (jax.nn.silu(_dot(x, w1_ref[0, 0, e].astype(BF16)))
                 * _dot(x, w3_ref[0, 0, e].astype(BF16))) * c_e
            y = _dot(a.astype(BF16), w2_ref[0, 0, e].astype(BF16))
            if e == 0:
                acc_ref[...] = y
            else:
                acc_ref[...] += y
        out_ref[...] = acc_ref[...].astype(out_ref.dtype)

    @pl.when(i >= n_active_ref[0])
    def _():
        out_ref[...] = jnp.zeros(out_ref.shape, out_ref.dtype)


def _experts(tile_group, n_active, x_sorted, w1, w3, w2, layer):
    ns = x_sorted.shape[0]
    d = w1.shape[3]
    tm = MOE_SORT_TILE
    group_w = lambda w: pl.BlockSpec((1, 1) + w.shape[2:], lambda i, tg, na: (layer, tg[i], 0, 0, 0),
                                     pipeline_mode=pl.Buffered(1))
    grid_spec = pltpu.PrefetchScalarGridSpec(
        num_scalar_prefetch=2,
        grid=(ns // tm,),
        in_specs=[
            pl.BlockSpec((tm, x_sorted.shape[1]), lambda i, tg, na: (i, 0)),
            group_w(w1), group_w(w3), group_w(w2),
        ],
        out_specs=pl.BlockSpec((tm, d), lambda i, tg, na: (i, 0)),
        scratch_shapes=[pltpu.VMEM((tm, d), F32)],
    )
    return pl.pallas_call(
        _experts_kernel,
        grid_spec=grid_spec,
        out_shape=jax.ShapeDtypeStruct((ns, d), BF16),
        compiler_params=_params("arbitrary"),
        name="experts",
    )(tile_group, n_active, x_sorted, w1, w3, w2)


def _residual_kernel(h_ref, gf_ref, *refs, steps, final_norm):
    y_refs, out_ref = refs[:-1], refs[-1]
    sequence = pl.program_id(0) // steps
    y = y_refs[0][...]
    for si in range(1, len(y_refs)):
        y = jnp.where(sequence == si, y_refs[si][...], y)
    out = h_ref[...] + y.astype(F32)
    out_ref[...] = _rms_norm(out, gf_ref[...]) if final_norm else out


def _residual(h, ys, gf, final_norm):
    n, d = h.shape
    tm = MOE_TILE
    steps = ys[0].shape[0] // tm
    row = pl.BlockSpec((tm, d), lambda i: (i, 0))
    of_sequence = lambda si: pl.BlockSpec((tm, d), lambda i: (jnp.clip(i - si * steps, 0, steps - 1), 0))
    return pl.pallas_call(
        functools.partial(_residual_kernel, steps=steps, final_norm=final_norm),
        grid=(n // tm,),
        in_specs=[row, pl.BlockSpec(gf.shape, lambda i: (0, 0))] + [of_sequence(si) for si in range(len(ys))],
        out_specs=row,
        out_shape=jax.ShapeDtypeStruct((n, d), F32),
        compiler_params=_params("parallel"),
        name="residual",
    )(h, gf, *ys)


def _group_sort_plan(group_id, rank, counts, tile):
    n = group_id.shape[0]
    n_slots = n + N_EXPERT_GROUPS * tile
    padded = (counts + tile - 1) // tile * tile
    ends = jnp.cumsum(padded)
    slot = (ends - padded)[group_id] + rank
    source = jnp.zeros((n_slots,), jnp.int32).at[slot].set(jnp.arange(n, dtype=jnp.int32))
    tile_start = jnp.arange(n_slots // tile, dtype=jnp.int32) * tile
    tile_group = jnp.minimum(jnp.searchsorted(ends, tile_start, side="right"), N_EXPERT_GROUPS - 1)
    return slot, source, tile_group.astype(jnp.int32), (ends[-1:] // tile).astype(jnp.int32)


def _moe(h, xn, comb, counts, w1, w3, w2, layer, gf, final_norm):
    as_int = lambda a: a.astype(jnp.int32)
    grouped = lambda w: w.reshape((w.shape[0], N_EXPERT_GROUPS, EXPERTS_PER_GROUP) + w.shape[2:])
    rows = lambda a, idx: jnp.take(a, idx, axis=0, mode="clip")
    t = h.shape[0] // counts.shape[0]
    ys = []
    for si in range(counts.shape[0]):
        seq = slice(si * t, (si + 1) * t)
        slot, source, tile_group, n_active = _group_sort_plan(
            as_int(comb[seq, GROUP_LANE]), as_int(comb[seq, RANK_LANE]), as_int(counts[si, 0, :N_EXPERT_GROUPS]),
            MOE_SORT_TILE)
        y_sorted = _experts(tile_group, n_active, rows(xn[seq], source),
                            grouped(w1), grouped(w3), grouped(w2), layer)
        ys.append(rows(y_sorted, slot))
    return _residual(h, ys, gf, final_norm)


def _selection_constants(seq_len):
    ncp = seq_len // CMP_STRIDE
    n_slc = seq_len // SLC_BLOCK
    ratio = SLC_BLOCK // CMP_STRIDE
    lead = CMP_LEN // CMP_STRIDE - 1
    c = np.arange(ncp)[:, None]
    j = np.arange(n_slc)[None, :]
    pool_m = ((c >= ratio * j - lead) & (c < ratio * j + ratio)).astype(np.float32)
    blocks_per_chunk = KEY_CHUNK // SLC_BLOCK
    n_chunks = seq_len // KEY_CHUNK
    pair_m = np.zeros((n_slc, LANES * ((n_chunks + LANES - 1) // LANES)), np.float32)
    pair_m[np.arange(n_slc), np.arange(n_slc) // blocks_per_chunk] = 1.0
    n_words = (n_chunks + WORD_BITS - 1) // WORD_BITS
    bits_m = np.zeros((pair_m.shape[1], LANES), np.float32)
    ch = np.arange(n_chunks)
    bits_m[ch, ch // WORD_BITS] = 2.0 ** (ch % WORD_BITS)
    key_aux = np.zeros((TOKEN_TILE, LANES), np.float32)
    in_chunk = np.arange(TOKEN_TILE) % KEY_CHUNK
    key_aux[np.arange(TOKEN_TILE), HEAD_DIM + in_chunk // SLC_BLOCK] = 1.0
    key_aux[:, HEAD_DIM + AUX_SLOPE] = in_chunk
    cmp_aux = np.zeros((ncp, LANES), np.float32)
    cmp_aux[:, HEAD_DIM] = np.arange(ncp) // CMP_AUX_SPLIT
    cmp_aux[:, HEAD_DIM + 1] = np.arange(ncp) % CMP_AUX_SPLIT
    as_bf16 = lambda a: jnp.asarray(a, BF16)
    return as_bf16(pool_m.T), as_bf16(pair_m), as_bf16(bits_m), jnp.asarray(key_aux), jnp.asarray(cmp_aux), n_words


def kernel(x, norm1_g, w_in, cmp_pe, cmp_w1, cmp_w2, w_nsa_proj, pool_w, pool_scale, w_pool_proj, conv_w,
           w_conv_proj, w_o, norm2_g, router_group_w, router_group_b, router_expert_w, router_expert_b,
           expert_w1, expert_w3, expert_w2, final_norm_g):
    b, t, d = x.shape
    n = b * t
    depth = w_in.shape[0]
    dq = NSA_HEADS * HEAD_DIM
    dkv = 6 * NSA_KV_HEADS * HEAD_DIM
    dgate = NSA_HEADS * NSA_BRANCHES
    cw = d // 4
    assert t % TOKEN_TILE == 0 and n % MOE_TILE == 0 and t % Q_TILE == 0
    n_slc = t // SLC_BLOCK
    n_sel = min(SLC_TOPN, n_slc)
    n_chunks16 = t // CMP_STRIDE
    kvw = NSA_KV_HEADS * HEAD_DIM
    poolt_m, pair_m, bits_m, key_aux, cmp_aux, n_words = _selection_constants(t)
    assert Q_TILE == KEY_CHUNK and AUX_SLOPE < AUX_COLS and KEY_CHUNK <= 256
    assert n_chunks16 <= 256 * CMP_AUX_SPLIT and n_chunks16 % min(CMP_ROWS_STEP, n_chunks16) == 0
    assert n_words <= LANES and pair_m.shape[1] == LANES

    before = jnp.asarray(np.tril(np.ones((TOKEN_TILE, TOKEN_TILE), np.float32), -1), BF16)
    h = x.reshape(n, d)
    for l in range(depth):
        wl = w_in[l]
        o_gate = dq + dkv
        o_pool = o_gate + dgate
        o_merge = o_pool + cw + 3 * cw
        kv_cols = lambda kind: wl[:, dq + kind * kvw:dq + (kind + 1) * kvw]
        wq_t = (wl[:, :dq] * (HEAD_DIM ** -0.5)).T.astype(BF16)
        wv_t = jnp.concatenate([kv_cols(3), kv_cols(5)], axis=1).T.astype(BF16)
        wg_t = jnp.pad(wl[:, o_gate:o_pool], ((0, 0), (0, GATE_PAD - dgate))).T.astype(BF16)
        no_aux = jnp.zeros((d, LANES - HEAD_DIM), F32)
        wk = jnp.concatenate([piece for kind in (2, 4) for gi in range(NSA_KV_HEADS)
                              for piece in (kv_cols(kind)[:, gi * HEAD_DIM:(gi + 1) * HEAD_DIM], no_aux)],
                             axis=1).astype(BF16)
        wn = jnp.concatenate([kv_cols(0), kv_cols(1), wl[:, o_pool:o_merge]], axis=1).astype(BF16)
        wmg = wl[:, o_merge:].astype(BF16)
        pool_bd = jax.scipy.linalg.block_diag(*[pool_w[l, gi] for gi in range(pool_w.shape[1])]).astype(BF16)
        convw = jnp.pad(conv_w[l], ((0, 8 - CONV_K), (0, 0)))
        wr = jnp.pad(jnp.concatenate([router_group_w[l], router_expert_w[l]], axis=1),
                     ((0, 0), (0, ROUTER_PAD - N_EXPERT_GROUPS - N_EXPERTS)))
        wr_hi = wr.astype(BF16)
        wr = jnp.stack([wr_hi, (wr - wr_hi.astype(F32)).astype(BF16)])
        br = jnp.pad(jnp.concatenate([router_group_b[l], router_expert_b[l]]),
                     (0, ROUTER_PAD - N_EXPERT_GROUPS - N_EXPERTS))[None, :]
        pe = jnp.broadcast_to(cmp_pe[l].reshape(2, 1, CMP_LEN * HEAD_DIM), (2, 8, CMP_LEN * HEAD_DIM)).astype(BF16)
        halves = CMP_LEN // CMP_STRIDE
        w1_bd = jnp.einsum("khldc,gq->khlgdqc",
                           cmp_w1[l].reshape(2, halves, CMP_STRIDE, HEAD_DIM, CMP_HIDDEN),
                           jnp.eye(NSA_KV_HEADS, dtype=F32))
        w1_bd = w1_bd.reshape(2, halves, CMP_STRIDE * kvw, NSA_KV_HEADS * CMP_HIDDEN).astype(BF16)
        w2_k = jnp.pad(cmp_w2[l, 0], ((0, 0), (0, LANES - HEAD_DIM))).astype(BF16)
        w2_vt = cmp_w2[l, 1].T.astype(BF16)

        q_t, vst, vwt, gates_t, ks, kw, cmp_src, pool_u, conv = _inproj(
            h, norm1_g[l][None, :], wq_t, wv_t, wg_t, wk, wn, key_aux, b)
        kc_aux, vc_t = _compress(cmp_src, pe, cmp_w1[l].astype(BF16), w1_bd, w2_k, w2_vt, cmp_aux)
        oc_t, sel_t, words = _nsa_cmp(q_t, kc_aux, vc_t, poolt_m, pair_m, bits_m, n_sel)
        words = words[:, :, :, 0, :n_words].reshape(-1)
        nsa = _nsa_slc(words, q_t, ks, vst, kw, vwt, sel_t, oc_t, gates_t, n_words)
        h, xn, comb, counts = _merge(
            h, norm1_g[l][None, :], nsa.reshape(n, dq), pool_u, conv, wmg,
            w_nsa_proj[l].astype(BF16), pool_bd, pool_scale[l][None, :], w_pool_proj[l].astype(BF16),
            convw, w_conv_proj[l].astype(BF16), w_o[l].astype(BF16), norm2_g[l][None, :], wr, br, before, t)
        h = _moe(h, xn, comb, counts, expert_w1, expert_w3, expert_w2, l,
                 final_norm_g[None, :], final_norm=(l == depth - 1))
    return h.reshape(b, t, d)
```

```python
import functools

import jax
import jax.numpy as jnp
import numpy as np
from jax import lax
from jax.experimental import pallas as pl
from jax.experimental.pallas import tpu as pltpu

F32 = jnp.float32
BF16 = jnp.bfloat16

HEAD_DIM = 64
NSA_HEADS = 8
NSA_KV_HEADS = 2
NSA_GROUP = NSA_HEADS // NSA_KV_HEADS
CMP_LEN = 32
CMP_STRIDE = 16
CMP_HIDDEN = 4 * HEAD_DIM
SLC_BLOCK = 64
SLC_TOPN = 16
WINDOW = 512
NSA_BRANCHES = 3
POOL_WINDOWS = (2, 4, 8, 16)
CONV_K = 3
N_EXPERT_GROUPS = 4
EXPERTS_PER_GROUP = 8
N_EXPERTS = N_EXPERT_GROUPS * EXPERTS_PER_GROUP
RMS_EPS = 1e-6
NEG_INF = -1e30
ALIBI_SLOPES = tuple(float(2.0 ** (-8.0 * (h + 1) / NSA_HEADS)) for h in range(NSA_HEADS))

LANES = 128
VMEM_LIMIT = 56 * 1024 * 1024
TOKEN_TILE = 512
MOE_TILE = 1024
MOE_SORT_TILE = 512
Q_TILE = 128
KEY_CHUNK = 128
GATE_PAD = LANES
ROUTER_PAD = LANES


def _params(*semantics):
    return pltpu.CompilerParams(dimension_semantics=semantics, vmem_limit_bytes=VMEM_LIMIT)


def _dot(a, b):
    return jnp.dot(a, b, preferred_element_type=F32)


def _dot_nt(a, b):
    return lax.dot_general(a, b, (((1,), (1,)), ((), ())), preferred_element_type=F32)


def _rms_norm(x, g):
    y = x * lax.rsqrt(jnp.mean(x * x, axis=-1, keepdims=True) + RMS_EPS)
    return y * g


def _iota(shape, dim):
    return lax.broadcasted_iota(jnp.int32, shape, dim)


def _inproj_kernel(x_ref, g_ref, wq_ref, wv_ref, wg_ref, wk_ref, wn_ref, kaux_ref,
                   q_ref, vs_ref, vw_ref, gate_ref, ks_ref, kw_ref, cmp_ref, pool_ref, conv_ref):
    xn = _rms_norm(x_ref[...], g_ref[...]).astype(BF16)
    sub_tiles = x_ref.shape[0] // Q_TILE
    q_t = _dot_nt(wq_ref[...], xn)
    for g in range(NSA_KV_HEADS):
        for j in range(sub_tiles):
            for r in range(NSA_GROUP):
                head = g * NSA_GROUP + r
                q_ref[0, g, j, :, r * Q_TILE:(r + 1) * Q_TILE] = q_t[
                    head * HEAD_DIM:(head + 1) * HEAD_DIM, j * Q_TILE:(j + 1) * Q_TILE].astype(BF16)
    v_t = _dot_nt(wv_ref[...], xn)
    k = _dot(xn, wk_ref[...])
    for branch, (v_ref, k_ref) in enumerate(((vs_ref, ks_ref), (vw_ref, kw_ref))):
        for g in range(NSA_KV_HEADS):
            slab = branch * NSA_KV_HEADS + g
            for j in range(sub_tiles):
                v_ref[0, g, j] = v_t[slab * HEAD_DIM:(slab + 1) * HEAD_DIM,
                                     j * KEY_CHUNK:(j + 1) * KEY_CHUNK].astype(BF16)
            k_ref[0, g] = (k[:, slab * LANES:(slab + 1) * LANES] + kaux_ref[...]).astype(BF16)
    gate_ref[0] = _dot_nt(wg_ref[...], xn)
    col = 0
    for ref in (cmp_ref, pool_ref, conv_ref):
        width = ref.shape[-1]
        ref[...] = _dot(xn, wn_ref[:, col:col + width]).reshape(ref.shape)
        col += width


def _inproj(h, g, wq_t, wv_t, wg_t, wk, wn, kaux, batch):
    n, d = h.shape
    t = n // batch
    tm = TOKEN_TILE
    steps = t // tm
    sub = tm // Q_TILE
    cw = d // 4
    gq = NSA_GROUP * Q_TILE
    full = lambda a: pl.BlockSpec(a.shape, lambda i: (0,) * a.ndim)
    row = lambda width: pl.BlockSpec((tm, width), lambda i: (i, 0))
    tiles = lambda rows, cols: pl.BlockSpec((1, NSA_KV_HEADS, sub, rows, cols),
                                            lambda i: (i // steps, 0, i % steps, 0, 0))
    keys = pl.BlockSpec((1, NSA_KV_HEADS, tm, LANES), lambda i: (i // steps, 0, i % steps, 0))
    sds = jax.ShapeDtypeStruct
    v_shape = sds((batch, NSA_KV_HEADS, t // KEY_CHUNK, HEAD_DIM, KEY_CHUNK), BF16)
    k_shape = sds((batch, NSA_KV_HEADS, t, LANES), BF16)
    return pl.pallas_call(
        _inproj_kernel,
        grid=(n // tm,),
        in_specs=[row(d), full(g), full(wq_t), full(wv_t), full(wg_t), full(wk), full(wn), full(kaux)],
        out_specs=[tiles(HEAD_DIM, gq), tiles(HEAD_DIM, KEY_CHUNK), tiles(HEAD_DIM, KEY_CHUNK),
                   pl.BlockSpec((1, GATE_PAD, tm), lambda i: (i // steps, 0, i % steps)),
                   keys, keys,
                   pl.BlockSpec((1, tm, 2 * NSA_KV_HEADS * HEAD_DIM), lambda i: (i // steps, i % steps, 0)),
                   row(cw), row(3 * cw)],
        out_shape=[sds((batch, NSA_KV_HEADS, t // Q_TILE, HEAD_DIM, gq), BF16), v_shape, v_shape,
                   sds((batch, GATE_PAD, t), F32), k_shape, k_shape,
                   sds((batch, t, 2 * NSA_KV_HEADS * HEAD_DIM), F32), sds((n, cw), F32), sds((n, 3 * cw), F32)],
        compiler_params=_params("parallel"),
        name="inproj",
    )(h, g, wq_t, wv_t, wg_t, wk, wn, kaux)


def _gelu_tanh(x):
    return 0.5 * x * (1.0 + jnp.tanh(0.7978845608028654 * (x + 0.044715 * x * x * x)))


def _compress_kernel(src_ref, pe_ref, w1_ref, w1bd_ref, w2k_ref, w2vt_ref, caux_ref, kc_ref, vct_ref):
    kind = pl.program_id(1)
    ncp = kc_ref.shape[2]
    hidden = w1_ref.shape[2]
    pieces = [src_ref[0, pl.ds(l, ncp, stride=CMP_STRIDE), :].astype(BF16) for l in range(CMP_STRIDE)]
    chunk = jnp.concatenate(pieces, axis=1)
    first = _dot(chunk, w1bd_ref[0, 0])
    second = _dot(chunk, w1bd_ref[0, 1])
    bias = _dot(pe_ref[0], w1_ref[0])[0:1, :]
    hid = first + pltpu.roll(second, ncp - 1, 0) + jnp.concatenate([bias] * NSA_KV_HEADS, axis=1)
    row = _iota((ncp, 1), 0)
    act = jnp.where(row < ncp - 1, _gelu_tanh(hid), 0.0).astype(BF16)
    for g in range(NSA_KV_HEADS):
        act_g = act[:, g * hidden:(g + 1) * hidden]

        @pl.when(kind == 0)
        def _():
            kc_ref[0, g] = (_dot(act_g, w2k_ref[...]) + caux_ref[...]).astype(BF16)

        @pl.when(kind == 1)
        def _():
            vct_ref[0, g * HEAD_DIM:(g + 1) * HEAD_DIM, :] = _dot_nt(w2vt_ref[...], act_g).astype(BF16)


def _compress(src, pe, w1, w1bd, w2k, w2vt, caux):
    b, t, _ = src.shape
    ncp = t // CMP_STRIDE
    gd = NSA_KV_HEADS * HEAD_DIM
    full = lambda a: pl.BlockSpec(a.shape, lambda bi, k: (0,) * a.ndim)
    per_kind = lambda a: pl.BlockSpec((1,) + a.shape[1:], lambda bi, k: (k,) + (0,) * (a.ndim - 1))
    return pl.pallas_call(
        _compress_kernel,
        grid=(b, 2),
        in_specs=[pl.BlockSpec((1, t, gd), lambda bi, k: (bi, 0, k)),
                  per_kind(pe), per_kind(w1), per_kind(w1bd), full(w2k), full(w2vt), full(caux)],
        out_specs=[pl.BlockSpec((1, NSA_KV_HEADS, ncp, LANES), lambda bi, k: (bi, 0, 0, 0)),
                   pl.BlockSpec((1, gd, ncp), lambda bi, k: (bi, 0, 0))],
        out_shape=[jax.ShapeDtypeStruct((b, NSA_KV_HEADS, ncp, LANES), BF16),
                   jax.ShapeDtypeStruct((b, gd, ncp), BF16)],
        compiler_params=_params("parallel", "arbitrary"),
        name="compress",
    )(src, pe, w1, w1bd, w2k, w2vt, caux)


SOFTMAX_FLOOR = -1e29
TAKEN = -3e38
ATTN_BATCH = 4
BLOCKS_PER_CHUNK = KEY_CHUNK // SLC_BLOCK
AUX_COLS = 16
AUX_SLOPE = BLOCKS_PER_CHUNK
ONES_ROWS = 16
CMP_ROWS_STEP = 128
CMP_AUX_SPLIT = 128


def _slope(g, r):
    if isinstance(g, int):
        return jnp.float32(ALIBI_SLOPES[g * NSA_GROUP + r])
    s = jnp.float32(ALIBI_SLOPES[r])
    for gi in range(1, NSA_KV_HEADS):
        s = jnp.where(g == gi, jnp.float32(ALIBI_SLOPES[gi * NSA_GROUP + r]), s)
    return s


def _nsa_cmp_kernel(qt_ref, kc_ref, vct_ref, poolt_ref, pair_ref, earlier_ref, oct_ref, selt_ref, list_ref,
                    *, n_sel):
    i = pl.program_id(1)
    qt = selt_ref.shape[4]
    ncp = kc_ref.shape[2]
    n_slc = poolt_ref.shape[0]
    gq = qt_ref.shape[4]
    start = i * qt
    t = start + _iota((1, qt), 1)
    aux_row = _iota((AUX_COLS, gq), 0)
    col_head = _iota((1, gq), 1) // qt
    pad_rows = jnp.zeros((kc_ref.shape[3] - qt_ref.shape[3] - AUX_COLS, gq), BF16)

    def weights(g):
        slope_cols = jnp.zeros((1, gq), F32)
        for r in range(NSA_GROUP):
            slope_cols = jnp.where(col_head == r, _slope(g, r), slope_cols)
        aux = jnp.where(aux_row == 0, slope_cols * (CMP_STRIDE * CMP_AUX_SPLIT),
                        jnp.where(aux_row == 1, slope_cols * CMP_STRIDE, 0.0))
        return jnp.concatenate([qt_ref[0, g, 0], aux.astype(BF16), pad_rows], axis=0)

    def importance(g, nr, nb):
        s = _dot(kc_ref[0, g, :nr, :], weights(g))
        edge = min(nr, 2 * CMP_ROWS_STEP)
        cmp_end = ((nr - edge) + _iota((edge, 1), 0)) * CMP_STRIDE + (CMP_LEN - 1)
        visible = cmp_end <= t
        vct = vct_ref[0, g * HEAD_DIM:(g + 1) * HEAD_DIM, :nr]
        psum = jnp.zeros((nr, qt), F32)
        for r in range(NSA_GROUP):
            cols = slice(r * qt, (r + 1) * qt)
            sr = s[:, cols]
            tail = jnp.where(visible, sr[nr - edge:], NEG_INF)
            sr = tail if edge == nr else jnp.concatenate([sr[:nr - edge], tail], axis=0)
            m = jnp.maximum(jnp.max(sr, axis=0, keepdims=True), SOFTMAX_FLOOR)
            e = jnp.exp(sr - m)
            l = jnp.sum(e, axis=0, keepdims=True)
            inv = jnp.where(l > 0.0, 1.0 / l, 0.0)
            oct_ref[0, g, 0, :, cols] = (_dot(vct, e.astype(BF16)) * inv).astype(BF16)
            psum = psum + e * inv
        return _dot(poolt_ref[:nb, :nr], psum.astype(BF16))

    def visible_prefix(nr):
        nb = min(n_slc, nr * CMP_STRIDE // SLC_BLOCK)
        imp = jnp.concatenate([importance(g, nr, nb) for g in range(NSA_KV_HEADS)], axis=1)
        blk = _iota((nb, 1), 0)
        cur = jnp.concatenate([t // SLC_BLOCK] * NSA_KV_HEADS, axis=1)
        forced = (blk == 0) | (blk == cur) | (blk == cur - 1)
        score = jnp.where(forced, TAKEN, jnp.where(blk <= cur, imp, NEG_INF))
        n_forced = 1 + jnp.where(cur >= 1, 1, 0) + jnp.where(cur >= 2, 1, 0)
        blk_f = blk.astype(F32)

        def take_one(score, active):
            m = jnp.max(score, axis=0, keepdims=True)
            first = jnp.min(jnp.where(score == m, blk_f, F32(1e9)), axis=0, keepdims=True)
            hit = (blk_f == first) if active is None else ((blk_f == first) & active)
            return jnp.where(hit, TAKEN, score)

        common_rounds = max(n_sel - 3, 0)
        for _ in range(common_rounds):
            score = take_one(score, None)

        def early_rounds(score):
            for k in range(common_rounds, n_sel - 1):
                score = take_one(score, n_sel - n_forced > k)
            return score

        score = lax.cond(start < 2 * SLC_BLOCK, early_rounds, lambda sc: sc, score)
        for g in range(NSA_KV_HEADS):
            sel_g = score[:, g * qt:(g + 1) * qt] == TAKEN
            selt_ref[0, g, 0, :nb, :] = jnp.where(sel_g, 0.0, NEG_INF)
            if nb < n_slc:
                selt_ref[0, g, 0, nb:, :] = jnp.full((n_slc - nb, qt), NEG_INF, F32)
            count = _dot_nt(jnp.ones((8, qt), BF16), jnp.where(sel_g, 1.0, 0.0).astype(BF16))
            used = jnp.where(count > 0.0, 1.0, 0.0).astype(BF16)
            chunk = _iota((8, LANES), 1)
            flagged = jnp.where((_dot(used, pair_ref[:nb, :]) > 0.0) & (chunk < i), 1.0, 0.0).astype(BF16)
            place = _dot(flagged, earlier_ref[...])
            lands = (_iota((LANES, LANES), 0).astype(F32) == place[0:1, :]) & (flagged[0:1, :] > 0)
            listed = _dot_nt(chunk.astype(BF16), jnp.where(lands, 1.0, 0.0).astype(BF16))
            n_listed = _dot(flagged, jnp.ones((LANES, LANES), BF16))
            list_ref[0, g, 0] = jnp.where(_iota((8, LANES), 0) < 4, listed, n_listed).astype(jnp.int32)

    step = min(CMP_ROWS_STEP, ncp)
    rows_needed = jnp.minimum((start + qt - CMP_LEN) // CMP_STRIDE + 1, ncp)
    n_steps = (rows_needed + step - 1) // step
    for k in range(ncp // step):
        pl.when(n_steps == k + 1)(functools.partial(visible_prefix, (k + 1) * step))


def _nsa_cmp(q_t, kc, vct, poolt_m, pair_m, earlier_m, n_sel):
    b, ng, n_tiles, qrows, qcols = q_t.shape
    ncp = kc.shape[2]
    n_slc = poolt_m.shape[0]
    const = lambda shape: pl.BlockSpec(shape, lambda bi, i: (0,) * len(shape))
    tile5 = lambda rows, cols: pl.BlockSpec((1, ng, 1, rows, cols), lambda bi, i: (bi, 0, i, 0, 0))
    return pl.pallas_call(
        functools.partial(_nsa_cmp_kernel, n_sel=n_sel),
        grid=(b, n_tiles),
        in_specs=[
            tile5(qrows, qcols),
            pl.BlockSpec((1, ng, ncp, kc.shape[3]), lambda bi, i: (bi, 0, 0, 0)),
            pl.BlockSpec((1, ng * HEAD_DIM, ncp), lambda bi, i: (bi, 0, 0)),
            const(poolt_m.shape), const(pair_m.shape), const(earlier_m.shape),
        ],
        out_specs=[tile5(HEAD_DIM, qcols), tile5(n_slc, Q_TILE), tile5(8, LANES)],
        out_shape=[
            jax.ShapeDtypeStruct((b, ng, n_tiles, HEAD_DIM, qcols), BF16),
            jax.ShapeDtypeStruct((b, ng, n_tiles, n_slc, Q_TILE), F32),
            jax.ShapeDtypeStruct((b, ng, n_tiles, 8, LANES), jnp.int32),
        ],
        compiler_params=_params("parallel", "parallel"),
        name="nsa_compressed",
    )(q_t, kc, vct, poolt_m, pair_m, earlier_m)


def _nsa_slc_kernel(lists_ref, counts_ref, qt_ref, ks_ref, vst_ref, kw_ref, vwt_ref, selt_ref, oct_ref, gt_ref,
                    out_ref, m_ref, l_ref, acc_ref, ow_ref, sa_ref, sb_ref, sw_ref):
    bi = pl.program_id(0)
    g = pl.program_id(1)
    i = pl.program_id(2)
    n_tiles = pl.num_programs(2)
    qt = out_ref.shape[1]
    q_rows = qt_ref[0, 0, 0]
    gq = q_rows.shape[1]
    start = i * qt
    lane_f = _iota((1, qt), 1).astype(F32)
    key_in_chunk = _iota((KEY_CHUNK, qt), 0)
    query_in_tile = _iota((KEY_CHUNK, qt), 1)

    aux_row = _iota((AUX_COLS, gq), 0)
    col_head = _iota((1, gq), 1) // qt
    slope_cols = jnp.zeros((1, gq), F32)
    for r in range(NSA_GROUP):
        slope_cols = jnp.where(col_head == r, _slope(g, r), slope_cols)
    aux_base = jnp.where(aux_row == AUX_SLOPE, slope_cols, 0.0)
    pad_rows = jnp.zeros((ks_ref.shape[3] - q_rows.shape[0] - AUX_COLS, gq), BF16)

    def scores_of(slots):
        return [_dot(k, jnp.concatenate([q_rows, aux.astype(BF16), pad_rows], axis=0))
                for k, aux, _, _, _ in slots]

    def values_of(slots):
        values = jnp.concatenate([v for _, _, v, _, _ in slots], axis=1)
        return jnp.concatenate([values, jnp.ones((ONES_ROWS, values.shape[1]), BF16)], axis=0)

    def softmax_step(slots, scores, v_cat, r, m_old):
        cols = slice(r * qt, (r + 1) * qt)
        srs, tops = [], []
        for j, (_, _, _, shift, mask) in enumerate(slots):
            sr = scores[j, :, cols] if hasattr(scores, "at") else scores[j][:, cols]
            sr = sr if mask is None else jnp.where(mask, sr, NEG_INF)
            srs.append(sr)
            tops.append(jnp.max(sr, axis=0, keepdims=True) + shift[r])
        m_new = functools.reduce(jnp.maximum, tops, m_old)
        ps = [jnp.exp((sr - (m_new - slot[3][r])).astype(BF16)) for slot, sr in zip(slots, srs)]
        weighted = _dot(v_cat, jnp.concatenate(ps, axis=0))
        return m_new, weighted[HEAD_DIM:HEAD_DIM + 1], weighted[:HEAD_DIM]

    def shifts(dist0, ok):
        rows = [-_slope(g, r) * (dist0 + lane_f) for r in range(NSA_GROUP)]
        return rows if ok is None else [jnp.where(ok, row, NEG_INF) for row in rows]

    tile_id = (bi * NSA_KV_HEADS + g) * n_tiles + i
    n_listed = counts_ref[tile_id]

    def selected_slot(c, ok, mask):
        at = pl.multiple_of(c * KEY_CHUNK, KEY_CHUNK)
        bias = selt_ref[0, 0, 0, pl.ds(c * BLOCKS_PER_CHUNK, BLOCKS_PER_CHUNK), :]
        aux = aux_base
        for blk in range(BLOCKS_PER_CHUNK):
            aux = jnp.where(aux_row == blk, jnp.concatenate([bias[blk:blk + 1]] * NSA_GROUP, axis=1), aux)
        return (ks_ref[0, 0, pl.ds(at, KEY_CHUNK), :], aux, vst_ref[0, 0, c],
                shifts((start - c * KEY_CHUNK).astype(F32), ok), mask)

    def listed_slot(idx):
        ok = idx < n_listed
        c = jnp.where(ok, lists_ref[tile_id * LANES + jnp.minimum(idx, jnp.maximum(n_listed - 1, 0))], 0)
        return selected_slot(c, ok, None)

    floor = jnp.full((1, qt), SOFTMAX_FLOOR, F32)

    n_back = WINDOW // KEY_CHUNK
    slots = []
    for j in range(n_back + 1):
        cs = start - WINDOW + j * KEY_CHUNK
        chunk = jnp.maximum(cs, 0) // KEY_CHUNK
        at = pl.multiple_of(chunk * KEY_CHUNK, KEY_CHUNK)
        mask = (query_in_tile < key_in_chunk) if j == 0 else (
            (key_in_chunk <= query_in_tile) if j == n_back else None)
        slots.append((kw_ref[0, 0, pl.ds(at, KEY_CHUNK), :], aux_base, vwt_ref[0, 0, chunk],
                      shifts(F32(WINDOW - j * KEY_CHUNK), cs >= 0), mask))
    window_slots = slots
    first_slots = [selected_slot(i, None, key_in_chunk <= query_in_tile)] + [
        listed_slot(j) for j in range(ATTN_BATCH - 1)]

    def listed_batch(n):
        return [listed_slot(ATTN_BATCH - 1 + n * ATTN_BATCH + j) for j in range(ATTN_BATCH)]

    def issue(slots, dst_ref):
        for j, s in enumerate(scores_of(slots)):
            dst_ref[j] = s

    issue(window_slots, sw_ref)
    issue(first_slots, sb_ref)
    v_cat = values_of(window_slots)
    for r in range(NSA_GROUP):
        _, total, weighted = softmax_step(window_slots, sw_ref, v_cat, r, floor)
        ow_ref[:, r * qt:(r + 1) * qt] = weighted * jnp.where(total > 0.0, 1.0 / total, 0.0)

    def consume(slots, src_ref):
        v_cat = values_of(slots)
        for r in range(NSA_GROUP):
            cols = slice(r * qt, (r + 1) * qt)
            m_old = m_ref[r]
            m_new, total, weighted = softmax_step(slots, src_ref, v_cat, r, m_old)
            alpha = jnp.exp(m_old - m_new)
            l_ref[r] = alpha * l_ref[r] + total
            acc_ref[:, cols] = alpha * acc_ref[:, cols] + weighted
            m_ref[r] = m_new

    issue(listed_batch(0), sa_ref)
    v_cat = values_of(first_slots)
    for r in range(NSA_GROUP):
        m_ref[r], l_ref[r], acc_ref[:, r * qt:(r + 1) * qt] = softmax_step(first_slots, sb_ref, v_cat, r, floor)

    def batch_pair(it, carry):
        first, second, third = (listed_batch(2 * it + n) for n in range(3))
        issue(second, sb_ref)
        consume(first, sa_ref)
        issue(third, sa_ref)
        consume(second, sb_ref)
        return carry

    n_rest = jnp.maximum(n_listed - (ATTN_BATCH - 1), 0)
    n_batches = (n_rest + ATTN_BATCH - 1) // ATTN_BATCH
    lax.fori_loop(0, n_batches // 2, batch_pair, 0)

    @pl.when(n_batches % 2 == 1)
    def _():
        consume(listed_batch(n_batches - 1), sa_ref)

    def finalize(r):
        l = l_ref[r]
        return acc_ref[:, r * qt:(r + 1) * qt] * jnp.where(l > 0.0, 1.0 / l, 0.0)

    outs = []
    for r in range(NSA_GROUP):
        cols = slice(r * qt, (r + 1) * qt)
        col = (g * NSA_GROUP + r) * NSA_BRANCHES
        gate = lambda br: jax.nn.sigmoid(gt_ref[0, pl.ds(col + br, 1), :])
        outs.append(gate(0) * oct_ref[0, 0, 0, :, cols].astype(F32) + gate(1) * finalize(r)
                    + gate(2) * ow_ref[:, cols])
    out_ref[0] = jnp.concatenate(outs, axis=0).T.astype(out_ref.dtype)


def _nsa_slc(lists, counts, q_t, ks, vst, kw, vwt, selt, oct, gates_t):
    b, _, t, kw_cols = ks.shape
    _, _, n_tiles, qrows, qcols = q_t.shape
    n_slc = selt.shape[3]
    n_chunks = vst.shape[2]
    gw = NSA_GROUP * HEAD_DIM
    once = dict(pipeline_mode=pl.Buffered(1))
    k_spec = pl.BlockSpec((1, 1, t, kw_cols), lambda bi, g, i, ls, ns: (bi, g, 0, 0), **once)
    vt_spec = pl.BlockSpec((1, 1, n_chunks, HEAD_DIM, KEY_CHUNK), lambda bi, g, i, ls, ns: (bi, g, 0, 0, 0), **once)
    tile5 = lambda rows, cols: pl.BlockSpec((1, 1, 1, rows, cols), lambda bi, g, i, ls, ns: (bi, g, i, 0, 0))
    grid_spec = pltpu.PrefetchScalarGridSpec(
        num_scalar_prefetch=2,
        grid=(b, NSA_KV_HEADS, n_tiles),
        in_specs=[
            tile5(qrows, qcols),
            k_spec, vt_spec, k_spec, vt_spec,
            tile5(n_slc, Q_TILE),
            tile5(HEAD_DIM, qcols),
            pl.BlockSpec((1, GATE_PAD, Q_TILE), lambda bi, g, i, ls, ns: (bi, 0, i)),
        ],
        out_specs=pl.BlockSpec((1, Q_TILE, gw), lambda bi, g, i, ls, ns: (bi, i, g)),
        scratch_shapes=[pltpu.VMEM((NSA_GROUP, 1, Q_TILE), F32), pltpu.VMEM((NSA_GROUP, 1, Q_TILE), F32),
                        pltpu.VMEM((HEAD_DIM, qcols), F32), pltpu.VMEM((HEAD_DIM, qcols), F32),
                        pltpu.VMEM((ATTN_BATCH, KEY_CHUNK, qcols), F32),
                        pltpu.VMEM((ATTN_BATCH, KEY_CHUNK, qcols), F32),
                        pltpu.VMEM((WINDOW // KEY_CHUNK + 1, KEY_CHUNK, qcols), F32)],
    )
    return pl.pallas_call(
        _nsa_slc_kernel,
        grid_spec=grid_spec,
        out_shape=jax.ShapeDtypeStruct((b, t, NSA_HEADS * HEAD_DIM), BF16),
        compiler_params=_params("parallel", "parallel", "parallel"),
        name="nsa_selected_window",
    )(lists, counts, q_t, ks, vst, kw, vwt, selt, oct, gates_t)


POOL_HALO = 16
CONV_HALO = 8


def _merge_kernel(h_ref, g_ref, nsa_ref, pool_ref, pool_halo_ref, conv_ref, conv_halo_ref,
                  wmg_ref, wnsa_ref, pool_bd_ref, pool_scale_ref, wpool_ref, convw_ref, wconv_ref, wo_ref,
                  g2_ref, wr_ref, br_ref, before_ref,
                  out_ref, xn_ref, comb_ref, count_ref, pool_ext, conv_ext, *, seq_len):
    i = pl.program_id(0)
    tm, d = h_ref.shape
    cw = pool_ref.shape[1]
    pos0 = (i * tm) % seq_len
    keep_halo = jnp.where(pos0 == 0, 0.0, 1.0)
    pos = pos0 + _iota((tm, 1), 0)

    u = pool_ref[...]
    pool_ext[0:POOL_HALO, :] = pool_halo_ref[...] * keep_halo
    pool_ext[POOL_HALO:, :] = u
    lane_group = _iota((1, cw), 1) // (cw // len(POOL_WINDOWS))
    total = u
    mean = jnp.zeros_like(u)
    done = 1
    for gi, win in enumerate(POOL_WINDOWS):
        for k in range(done, win):
            total = total + pool_ext[POOL_HALO - k:POOL_HALO - k + tm, :]
        done = win
        cnt = jnp.minimum(pos + 1, win).astype(F32)
        mean = jnp.where(lane_group == gi, total / cnt, mean)
    pooled = (mean - u).astype(BF16)
    mixed = _dot(pooled, pool_bd_ref[...]) * pool_scale_ref[...]
    y_pool = _dot(mixed.astype(BF16), wpool_ref[...])

    ch = conv_ref[:, 0:cw]
    cb = conv_ref[:, cw:2 * cw]
    cc = conv_ref[:, 2 * cw:3 * cw]
    conv_ext[0:CONV_HALO, :] = conv_halo_ref[:, 0:cw] * conv_halo_ref[:, 2 * cw:3 * cw] * keep_halo
    conv_ext[CONV_HALO:, :] = cc * ch
    y = jnp.zeros((tm, cw), F32)
    for k in range(CONV_K):
        off = CONV_HALO - (CONV_K - 1) + k
        y = y + convw_ref[k:k + 1, :] * conv_ext[off:off + tm, :]
    y_conv = _dot((cb * y).astype(BF16), wconv_ref[...])

    y_nsa = _dot(nsa_ref[...], wnsa_ref[...])

    h = h_ref[...]
    xn = _rms_norm(h, g_ref[...]).astype(BF16)
    merged = jnp.zeros((tm, d), F32)
    for br, y_br in enumerate((y_nsa, y_pool, y_conv)):
        mg = jax.nn.sigmoid(_dot(xn, wmg_ref[:, br * d:(br + 1) * d]))
        merged = merged + mg * y_br
    h_new = h + _dot(merged.astype(BF16), wo_ref[...])
    out_ref[...] = h_new
    _route_tile(h_new, g2_ref, wr_ref, br_ref, before_ref, xn_ref, comb_ref, count_ref)


def _merge(h, g, nsa, pool_u, conv, wmg, wnsa, pool_bd, pool_scale, wpool, convw, wconv, wo,
           g2, wr, br, before, seq_len):
    n, d = h.shape
    tm = TOKEN_TILE
    cw = pool_u.shape[1]
    row = lambda width: pl.BlockSpec((tm, width), lambda i: (i, 0))
    full = lambda a: pl.BlockSpec(a.shape, lambda i: (0,) * a.ndim)
    halo = lambda rows, width: pl.BlockSpec(
        (rows, width), lambda i: (jnp.maximum(i * (tm // rows) - 1, 0), 0))
    return pl.pallas_call(
        functools.partial(_merge_kernel, seq_len=seq_len),
        grid=(n // tm,),
        in_specs=[row(d), full(g), row(nsa.shape[1]), row(cw), halo(POOL_HALO, cw),
                  row(conv.shape[1]), halo(CONV_HALO, conv.shape[1]),
                  full(wmg), full(wnsa), full(pool_bd), full(pool_scale), full(wpool), full(convw),
                  full(wconv), full(wo), full(g2), full(wr), full(br), full(before)],
        out_specs=[row(d), row(d + ROUTER_PAD), row(ROUTER_PAD), pl.BlockSpec((8, ROUTER_PAD), lambda i: (0, 0))],
        out_shape=[jax.ShapeDtypeStruct((n, d), F32), jax.ShapeDtypeStruct((n, d + ROUTER_PAD), BF16),
                   jax.ShapeDtypeStruct((n, ROUTER_PAD), F32), jax.ShapeDtypeStruct((8, ROUTER_PAD), F32)],
        scratch_shapes=[pltpu.VMEM((tm + POOL_HALO, cw), F32), pltpu.VMEM((tm + CONV_HALO, cw), F32)],
        compiler_params=_params("arbitrary"),
        name="merge",
    )(h, g, nsa, pool_u, pool_u, conv, conv, wmg, wnsa, pool_bd, pool_scale, wpool, convw, wconv, wo,
      g2, wr, br, before)


def _route(logits):
    lane = _iota(logits.shape, 1)
    lane_f = lane.astype(F32)
    big = F32(1e9)
    is_group = lane < N_EXPERT_GROUPS
    gl = jnp.where(is_group, logits, NEG_INF)
    g_max = jnp.max(gl, axis=1, keepdims=True)
    g_sel = jnp.min(jnp.where(gl == g_max, lane_f, big), axis=1, keepdims=True)
    g_prob = 1.0 / jnp.sum(jnp.where(is_group, jnp.exp(gl - g_max), 0.0), axis=1, keepdims=True)
    lo = N_EXPERT_GROUPS + EXPERTS_PER_GROUP * g_sel
    in_group = (lane_f >= lo) & (lane_f < lo + EXPERTS_PER_GROUP)
    el = jnp.where(in_group, logits, NEG_INF)
    v1 = jnp.max(el, axis=1, keepdims=True)
    i1 = jnp.min(jnp.where((el == v1) & in_group, lane_f, big), axis=1, keepdims=True)
    el2 = jnp.where(lane_f == i1, NEG_INF, el)
    rest = in_group & (lane_f != i1)
    v2 = jnp.max(el2, axis=1, keepdims=True)
    i2 = jnp.min(jnp.where((el2 == v2) & rest, lane_f, big), axis=1, keepdims=True)
    e2 = jnp.exp(v2 - v1)
    w1 = g_prob / (1.0 + e2)
    w2 = g_prob * e2 / (1.0 + e2)
    return jnp.where(lane_f == i1, w1, 0.0) + jnp.where(lane_f == i2, w2, 0.0), g_sel


GROUP_LANE = N_EXPERT_GROUPS + N_EXPERTS


RANK_LANE = GROUP_LANE + 1


def _route_tile(h, g_ref, wr_ref, br_ref, before_ref, xn_ref, comb_ref, count_ref):
    @pl.when(pl.program_id(0) == 0)
    def _():
        count_ref[...] = jnp.zeros(count_ref.shape, F32)

    xn = _rms_norm(h, g_ref[...])
    xn_hi = xn.astype(BF16)
    xn_lo = (xn - xn_hi.astype(F32)).astype(BF16)
    logits = (_dot(xn_hi, wr_ref[0]) + (_dot(xn_hi, wr_ref[1]) + _dot(xn_lo, wr_ref[0]))) + br_ref[...]
    comb, g_sel = _route(logits)
    lane = _iota(comb.shape, 1)
    chose = jnp.where(lane.astype(F32) == g_sel, 1.0, 0.0)
    earlier = _dot(before_ref[...], chose.astype(BF16)) + count_ref[0:1, :]
    rank = jnp.sum(chose * earlier, axis=1, keepdims=True)
    count_ref[0:1, :] = count_ref[0:1, :] + jnp.sum(chose, axis=0, keepdims=True)
    comb_ref[...] = jnp.where(lane == GROUP_LANE, g_sel, jnp.where(lane == RANK_LANE, rank, comb))
    first = N_EXPERT_GROUPS + EXPERTS_PER_GROUP * g_sel
    local = jnp.zeros(comb.shape, F32)
    for e in range(EXPERTS_PER_GROUP):
        c_e = jnp.sum(jnp.where(lane.astype(F32) == first + e, comb, 0.0), axis=1, keepdims=True)
        local = jnp.where((lane == e) | (lane == EXPERTS_PER_GROUP + e), c_e, local)
    local_hi = local.astype(BF16)
    d = xn_hi.shape[1]
    xn_ref[:, :d] = xn_hi
    xn_ref[:, d:] = jnp.where(lane < EXPERTS_PER_GROUP, local_hi, (local - local_hi.astype(F32)).astype(BF16))


def _experts_kernel(tile_group_ref, n_active_ref, x_ref, w1_ref, w3_ref, w2_ref, out_ref, acc_ref):
    i = pl.program_id(0)

    @pl.when(i < n_active_ref[0])
    def _():
        d = out_ref.shape[1]
        x = x_ref[:, :d]
        comb = x_ref[:, d:].astype(F32)
        lane = _iota(comb.shape, 1)
        for e in range(EXPERTS_PER_GROUP):
            c_e = jnp.sum(jnp.where((lane == e) | (lane == EXPERTS_PER_GROUP + e), comb, 0.0),
                          axis=1, keepdims=True)
            a = (jax.nn.silu(_dot(x, w1_ref[0, 0, e].astype(BF16)))
                 * _dot(x, w3_ref[0, 0, e].astype(BF16))) * c_e
            y = _dot(a.astype(BF16), w2_ref[0, 0, e].astype(BF16))
            if e == 0:
                acc_ref[...] = y
            else:
                acc_ref[...] += y
        out_ref[...] = acc_ref[...].astype(out_ref.dtype)

    @pl.when(i >= n_active_ref[0])
    def _():
        out_ref[...] = jnp.zeros(out_ref.shape, out_ref.dtype)


def _experts(tile_group, n_active, x_sorted, w1, w3, w2, layer):
    ns = x_sorted.shape[0]
    d = w1.shape[3]
    tm = MOE_SORT_TILE
    group_w = lambda w: pl.BlockSpec((1, 1) + w.shape[2:], lambda i, tg, na: (layer, tg[i], 0, 0, 0),
                                     pipeline_mode=pl.Buffered(1))
    grid_spec = pltpu.PrefetchScalarGridSpec(
        num_scalar_prefetch=2,
        grid=(ns // tm,),
        in_specs=[
            pl.BlockSpec((tm, x_sorted.shape[1]), lambda i, tg, na: (i, 0)),
            group_w(w1), group_w(w3), group_w(w2),
        ],
        out_specs=pl.BlockSpec((tm, d), lambda i, tg, na: (i, 0)),
        scratch_shapes=[pltpu.VMEM((tm, d), F32)],
    )
    return pl.pallas_call(
        _experts_kernel,
        grid_spec=grid_spec,
        out_shape=jax.ShapeDtypeStruct((ns, d), BF16),
        compiler_params=_params("arbitrary"),
        name="experts",
    )(tile_group, n_active, x_sorted, w1, w3, w2)


def _residual_kernel(h_ref, y_ref, gf_ref, out_ref, *, final_norm):
    out = h_ref[...] + y_ref[...].astype(F32)
    out_ref[...] = _rms_norm(out, gf_ref[...]) if final_norm else out


def _residual(h, y, gf, final_norm):
    n, d = h.shape
    tm = MOE_TILE
    row = pl.BlockSpec((tm, d), lambda i: (i, 0))
    return pl.pallas_call(
        functools.partial(_residual_kernel, final_norm=final_norm),
        grid=(n // tm,),
        in_specs=[row, row, pl.BlockSpec(gf.shape, lambda i: (0, 0))],
        out_specs=row,
        out_shape=jax.ShapeDtypeStruct((n, d), F32),
        compiler_params=_params("parallel"),
        name="residual",
    )(h, y, gf)


def _group_sort_plan(group_id, rank, counts, tile):
    n = group_id.shape[0]
    n_slots = n + N_EXPERT_GROUPS * tile
    padded = (counts + tile - 1) // tile * tile
    ends = jnp.cumsum(padded)
    slot = (ends - padded)[group_id] + rank
    source = jnp.zeros((n_slots,), jnp.int32).at[slot].set(jnp.arange(n, dtype=jnp.int32))
    tile_start = jnp.arange(n_slots // tile, dtype=jnp.int32) * tile
    tile_group = jnp.minimum(jnp.searchsorted(ends, tile_start, side="right"), N_EXPERT_GROUPS - 1)
    return slot, source, tile_group.astype(jnp.int32), (ends[-1:] // tile).astype(jnp.int32)


def _moe(h, xn, comb, counts, w1, w3, w2, layer, gf, final_norm):
    as_int = lambda a: a.astype(jnp.int32)
    slot, source, tile_group, n_active = _group_sort_plan(
        as_int(comb[:, GROUP_LANE]), as_int(comb[:, RANK_LANE]), as_int(counts[0, :N_EXPERT_GROUPS]),
        MOE_SORT_TILE)
    grouped = lambda w: w.reshape((w.shape[0], N_EXPERT_GROUPS, EXPERTS_PER_GROUP) + w.shape[2:])
    rows = lambda a, idx: jnp.take(a, idx, axis=0, mode="clip")
    y_sorted = _experts(tile_group, n_active, rows(xn, source), grouped(w1), grouped(w3), grouped(w2), layer)
    return _residual(h, rows(y_sorted, slot), gf, final_norm)


def _selection_constants(seq_len):
    ncp = seq_len // CMP_STRIDE
    n_slc = seq_len // SLC_BLOCK
    ratio = SLC_BLOCK // CMP_STRIDE
    lead = CMP_LEN // CMP_STRIDE - 1
    c = np.arange(ncp)[:, None]
    j = np.arange(n_slc)[None, :]
    pool_m = ((c >= ratio * j - lead) & (c < ratio * j + ratio)).astype(np.float32)
    blocks_per_chunk = KEY_CHUNK // SLC_BLOCK
    n_chunks = seq_len // KEY_CHUNK
    pair_m = np.zeros((n_slc, LANES * ((n_chunks + LANES - 1) // LANES)), np.float32)
    pair_m[np.arange(n_slc), np.arange(n_slc) // blocks_per_chunk] = 1.0
    earlier_m = np.triu(np.ones((LANES, LANES), np.float32), 1)
    key_aux = np.zeros((TOKEN_TILE, LANES), np.float32)
    in_chunk = np.arange(TOKEN_TILE) % KEY_CHUNK
    key_aux[np.arange(TOKEN_TILE), HEAD_DIM + in_chunk // SLC_BLOCK] = 1.0
    key_aux[:, HEAD_DIM + AUX_SLOPE] = in_chunk
    cmp_aux = np.zeros((ncp, LANES), np.float32)
    cmp_aux[:, HEAD_DIM] = np.arange(ncp) // CMP_AUX_SPLIT
    cmp_aux[:, HEAD_DIM + 1] = np.arange(ncp) % CMP_AUX_SPLIT
    as_bf16 = lambda a: jnp.asarray(a, BF16)
    return as_bf16(pool_m.T), as_bf16(pair_m), as_bf16(earlier_m), jnp.asarray(key_aux), jnp.asarray(cmp_aux)


def kernel(x, norm1_g, w_in, cmp_pe, cmp_w1, cmp_w2, w_nsa_proj, pool_w, pool_scale, w_pool_proj, conv_w,
           w_conv_proj, w_o, norm2_g, router_group_w, router_group_b, router_expert_w, router_expert_b,
           expert_w1, expert_w3, expert_w2, final_norm_g):
    b, t, d = x.shape
    n = b * t
    depth = w_in.shape[0]
    dq = NSA_HEADS * HEAD_DIM
    dkv = 6 * NSA_KV_HEADS * HEAD_DIM
    dgate = NSA_HEADS * NSA_BRANCHES
    cw = d // 4
    assert t % TOKEN_TILE == 0 and n % MOE_TILE == 0 and t % Q_TILE == 0
    n_slc = t // SLC_BLOCK
    n_sel = min(SLC_TOPN, n_slc)
    n_chunks16 = t // CMP_STRIDE
    kvw = NSA_KV_HEADS * HEAD_DIM
    poolt_m, pair_m, earlier_m, key_aux, cmp_aux = _selection_constants(t)
    assert Q_TILE == KEY_CHUNK and AUX_SLOPE < AUX_COLS and KEY_CHUNK <= 256
    assert n_chunks16 <= 256 * CMP_AUX_SPLIT and n_chunks16 % min(CMP_ROWS_STEP, n_chunks16) == 0
    assert pair_m.shape[1] == LANES

    before = jnp.asarray(np.tril(np.ones((TOKEN_TILE, TOKEN_TILE), np.float32), -1), BF16)
    h = x.reshape(n, d)
    for l in range(depth):
        wl = w_in[l]
        o_gate = dq + dkv
        o_pool = o_gate + dgate
        o_merge = o_pool + cw + 3 * cw
        kv_cols = lambda kind: wl[:, dq + kind * kvw:dq + (kind + 1) * kvw]
        wq_t = (wl[:, :dq] * (HEAD_DIM ** -0.5)).T.astype(BF16)
        wv_t = jnp.concatenate([kv_cols(3), kv_cols(5)], axis=1).T.astype(BF16)
        wg_t = jnp.pad(wl[:, o_gate:o_pool], ((0, 0), (0, GATE_PAD - dgate))).T.astype(BF16)
        no_aux = jnp.zeros((d, LANES - HEAD_DIM), F32)
        wk = jnp.concatenate([piece for kind in (2, 4) for gi in range(NSA_KV_HEADS)
                              for piece in (kv_cols(kind)[:, gi * HEAD_DIM:(gi + 1) * HEAD_DIM], no_aux)],
                             axis=1).astype(BF16)
        wn = jnp.concatenate([kv_cols(0), kv_cols(1), wl[:, o_pool:o_merge]], axis=1).astype(BF16)
        wmg = wl[:, o_merge:].astype(BF16)
        pool_bd = jax.scipy.linalg.block_diag(*[pool_w[l, gi] for gi in range(pool_w.shape[1])]).astype(BF16)
        convw = jnp.pad(conv_w[l], ((0, 8 - CONV_K), (0, 0)))
        wr = jnp.pad(jnp.concatenate([router_group_w[l], router_expert_w[l]], axis=1),
                     ((0, 0), (0, ROUTER_PAD - N_EXPERT_GROUPS - N_EXPERTS)))
        wr_hi = wr.astype(BF16)
        wr = jnp.stack([wr_hi, (wr - wr_hi.astype(F32)).astype(BF16)])
        br = jnp.pad(jnp.concatenate([router_group_b[l], router_expert_b[l]]),
                     (0, ROUTER_PAD - N_EXPERT_GROUPS - N_EXPERTS))[None, :]
        pe = jnp.broadcast_to(cmp_pe[l].reshape(2, 1, CMP_LEN * HEAD_DIM), (2, 8, CMP_LEN * HEAD_DIM)).astype(BF16)
        halves = CMP_LEN // CMP_STRIDE
        w1_bd = jnp.einsum("khldc,gq->khlgdqc",
                           cmp_w1[l].reshape(2, halves, CMP_STRIDE, HEAD_DIM, CMP_HIDDEN),
                           jnp.eye(NSA_KV_HEADS, dtype=F32))
        w1_bd = w1_bd.reshape(2, halves, CMP_STRIDE * kvw, NSA_KV_HEADS * CMP_HIDDEN).astype(BF16)
        w2_k = jnp.pad(cmp_w2[l, 0], ((0, 0), (0, LANES - HEAD_DIM))).astype(BF16)
        w2_vt = cmp_w2[l, 1].T.astype(BF16)

        q_t, vst, vwt, gates_t, ks, kw, cmp_src, pool_u, conv = _inproj(
            h, norm1_g[l][None, :], wq_t, wv_t, wg_t, wk, wn, key_aux, b)
        kc_aux, vc_t = _compress(cmp_src, pe, cmp_w1[l].astype(BF16), w1_bd, w2_k, w2_vt, cmp_aux)
        oc_t, sel_t, listed = _nsa_cmp(q_t, kc_aux, vc_t, poolt_m, pair_m, earlier_m, n_sel)
        nsa = _nsa_slc(listed[:, :, :, 0, :].reshape(-1), listed[:, :, :, 4, 0].reshape(-1),
                       q_t, ks, vst, kw, vwt, sel_t, oc_t, gates_t)
        h, xn, comb, counts = _merge(
            h, norm1_g[l][None, :], nsa.reshape(n, dq), pool_u, conv, wmg,
            w_nsa_proj[l].astype(BF16), pool_bd, pool_scale[l][None, :], w_pool_proj[l].astype(BF16),
            convw, w_conv_proj[l].astype(BF16), w_o[l].astype(BF16), norm2_g[l][None, :], wr, br, before, t)
        h = _moe(h, xn, comb, counts, expert_w1, expert_w3, expert_w2, l,
                 final_norm_g[None, :], final_norm=(l == depth - 1))
    return h.reshape(b, t, d)
```

```python
import functools

import jax
import jax.numpy as jnp
import numpy as np
from jax import lax
from jax.experimental import pallas as pl
from jax.experimental.pallas import tpu as pltpu

F32 = jnp.float32
BF16 = jnp.bfloat16

HEAD_DIM = 64
NSA_HEADS = 8
NSA_KV_HEADS = 2
NSA_GROUP = NSA_HEADS // NSA_KV_HEADS
CMP_LEN = 32
CMP_STRIDE = 16
CMP_HIDDEN = 4 * HEAD_DIM
SLC_BLOCK = 64
SLC_TOPN = 16
WINDOW = 512
NSA_BRANCHES = 3
POOL_WINDOWS = (2, 4, 8, 16)
CONV_K = 3
N_EXPERT_GROUPS = 4
EXPERTS_PER_GROUP = 8
N_EXPERTS = N_EXPERT_GROUPS * EXPERTS_PER_GROUP
RMS_EPS = 1e-6
NEG_INF = -1e30
ALIBI_SLOPES = tuple(float(2.0 ** (-8.0 * (h + 1) / NSA_HEADS)) for h in range(NSA_HEADS))

LANES = 128
VMEM_LIMIT = 56 * 1024 * 1024
TOKEN_TILE = 512
MOE_TILE = 1024
MOE_SORT_TILE = 512
Q_TILE = 128
KEY_CHUNK = 128
GATE_PAD = LANES
ROUTER_PAD = LANES


def _params(*semantics):
    return pltpu.CompilerParams(dimension_semantics=semantics, vmem_limit_bytes=VMEM_LIMIT)


def _dot(a, b):
    return jnp.dot(a, b, preferred_element_type=F32)


def _dot_nt(a, b):
    return lax.dot_general(a, b, (((1,), (1,)), ((), ())), preferred_element_type=F32)


def _rms_norm(x, g):
    y = x * lax.rsqrt(jnp.mean(x * x, axis=-1, keepdims=True) + RMS_EPS)
    return y * g


def _iota(shape, dim):
    return lax.broadcasted_iota(jnp.int32, shape, dim)


def _inproj_kernel(x_ref, g_ref, wq_ref, wv_ref, wg_ref, wk_ref, wn_ref, kaux_ref,
                   q_ref, vs_ref, vw_ref, gate_ref, ks_ref, kw_ref, cmp_ref, pool_ref, conv_ref):
    xn = _rms_norm(x_ref[...], g_ref[...]).astype(BF16)
    sub_tiles = x_ref.shape[0] // Q_TILE
    q_t = _dot_nt(wq_ref[...], xn)
    for g in range(NSA_KV_HEADS):
        for j in range(sub_tiles):
            for r in range(NSA_GROUP):
                head = g * NSA_GROUP + r
                q_ref[0, g, j, :, r * Q_TILE:(r + 1) * Q_TILE] = q_t[
                    head * HEAD_DIM:(head + 1) * HEAD_DIM, j * Q_TILE:(j + 1) * Q_TILE].astype(BF16)
    v_t = _dot_nt(wv_ref[...], xn)
    k = _dot(xn, wk_ref[...])
    for branch, (v_ref, k_ref) in enumerate(((vs_ref, ks_ref), (vw_ref, kw_ref))):
        for g in range(NSA_KV_HEADS):
            slab = branch * NSA_KV_HEADS + g
            for j in range(sub_tiles):
                v_ref[0, g, j] = v_t[slab * HEAD_DIM:(slab + 1) * HEAD_DIM,
                                     j * KEY_CHUNK:(j + 1) * KEY_CHUNK].astype(BF16)
            k_ref[0, g] = (k[:, slab * LANES:(slab + 1) * LANES] + kaux_ref[...]).astype(BF16)
    gate_ref[0] = _dot_nt(wg_ref[...], xn)
    col = 0
    for ref in (cmp_ref, pool_ref, conv_ref):
        width = ref.shape[-1]
        ref[...] = _dot(xn, wn_ref[:, col:col + width]).reshape(ref.shape)
        col += width


def _inproj(h, g, wq_t, wv_t, wg_t, wk, wn, kaux, batch):
    n, d = h.shape
    t = n // batch
    tm = TOKEN_TILE
    steps = t // tm
    sub = tm // Q_TILE
    cw = d // 4
    gq = NSA_GROUP * Q_TILE
    full = lambda a: pl.BlockSpec(a.shape, lambda i: (0,) * a.ndim)
    row = lambda width: pl.BlockSpec((tm, width), lambda i: (i, 0))
    tiles = lambda rows, cols: pl.BlockSpec((1, NSA_KV_HEADS, sub, rows, cols),
                                            lambda i: (i // steps, 0, i % steps, 0, 0))
    keys = pl.BlockSpec((1, NSA_KV_HEADS, tm, LANES), lambda i: (i // steps, 0, i % steps, 0))
    sds = jax.ShapeDtypeStruct
    v_shape = sds((batch, NSA_KV_HEADS, t // KEY_CHUNK, HEAD_DIM, KEY_CHUNK), BF16)
    k_shape = sds((batch, NSA_KV_HEADS, t, LANES), BF16)
    return pl.pallas_call(
        _inproj_kernel,
        grid=(n // tm,),
        in_specs=[row(d), full(g), full(wq_t), full(wv_t), full(wg_t), full(wk), full(wn), full(kaux)],
        out_specs=[tiles(HEAD_DIM, gq), tiles(HEAD_DIM, KEY_CHUNK), tiles(HEAD_DIM, KEY_CHUNK),
                   pl.BlockSpec((1, GATE_PAD, tm), lambda i: (i // steps, 0, i % steps)),
                   keys, keys,
                   pl.BlockSpec((1, tm, 2 * NSA_KV_HEADS * HEAD_DIM), lambda i: (i // steps, i % steps, 0)),
                   row(cw), row(3 * cw)],
        out_shape=[sds((batch, NSA_KV_HEADS, t // Q_TILE, HEAD_DIM, gq), BF16), v_shape, v_shape,
                   sds((batch, GATE_PAD, t), F32), k_shape, k_shape,
                   sds((batch, t, 2 * NSA_KV_HEADS * HEAD_DIM), F32), sds((n, cw), F32), sds((n, 3 * cw), F32)],
        compiler_params=_params("parallel"),
        name="inproj",
    )(h, g, wq_t, wv_t, wg_t, wk, wn, kaux)


def _gelu_tanh(x):
    return 0.5 * x * (1.0 + jnp.tanh(0.7978845608028654 * (x + 0.044715 * x * x * x)))


def _compress_kernel(src_ref, pe_ref, w1_ref, w1bd_ref, w2k_ref, w2vt_ref, caux_ref, kc_ref, vct_ref):
    kind = pl.program_id(1)
    ncp = kc_ref.shape[2]
    hidden = w1_ref.shape[2]
    pieces = [src_ref[0, pl.ds(l, ncp, stride=CMP_STRIDE), :].astype(BF16) for l in range(CMP_STRIDE)]
    chunk = jnp.concatenate(pieces, axis=1)
    first = _dot(chunk, w1bd_ref[0, 0])
    second = _dot(chunk, w1bd_ref[0, 1])
    bias = _dot(pe_ref[0], w1_ref[0])[0:1, :]
    hid = first + pltpu.roll(second, ncp - 1, 0) + jnp.concatenate([bias] * NSA_KV_HEADS, axis=1)
    row = _iota((ncp, 1), 0)
    act = jnp.where(row < ncp - 1, _gelu_tanh(hid), 0.0).astype(BF16)
    for g in range(NSA_KV_HEADS):
        act_g = act[:, g * hidden:(g + 1) * hidden]

        @pl.when(kind == 0)
        def _():
            kc_ref[0, g] = (_dot(act_g, w2k_ref[...]) + caux_ref[...]).astype(BF16)

        @pl.when(kind == 1)
        def _():
            vct_ref[0, g * HEAD_DIM:(g + 1) * HEAD_DIM, :] = _dot_nt(w2vt_ref[...], act_g).astype(BF16)


def _compress(src, pe, w1, w1bd, w2k, w2vt, caux):
    b, t, _ = src.shape
    ncp = t // CMP_STRIDE
    gd = NSA_KV_HEADS * HEAD_DIM
    full = lambda a: pl.BlockSpec(a.shape, lambda bi, k: (0,) * a.ndim)
    per_kind = lambda a: pl.BlockSpec((1,) + a.shape[1:], lambda bi, k: (k,) + (0,) * (a.ndim - 1))
    return pl.pallas_call(
        _compress_kernel,
        grid=(b, 2),
        in_specs=[pl.BlockSpec((1, t, gd), lambda bi, k: (bi, 0, k)),
                  per_kind(pe), per_kind(w1), per_kind(w1bd), full(w2k), full(w2vt), full(caux)],
        out_specs=[pl.BlockSpec((1, NSA_KV_HEADS, ncp, LANES), lambda bi, k: (bi, 0, 0, 0)),
                   pl.BlockSpec((1, gd, ncp), lambda bi, k: (bi, 0, 0))],
        out_shape=[jax.ShapeDtypeStruct((b, NSA_KV_HEADS, ncp, LANES), BF16),
                   jax.ShapeDtypeStruct((b, gd, ncp), BF16)],
        compiler_params=_params("parallel", "arbitrary"),
        name="compress",
    )(src, pe, w1, w1bd, w2k, w2vt, caux)


SOFTMAX_FLOOR = -1e29
TAKEN = -3e38
ATTN_BATCH = 4
BLOCKS_PER_CHUNK = KEY_CHUNK // SLC_BLOCK
AUX_COLS = 16
AUX_SLOPE = BLOCKS_PER_CHUNK
ONES_ROWS = 16
CMP_ROWS_STEP = 128
CMP_TILES = 2
CMP_AUX_SPLIT = 128


def _slope(g, r):
    if isinstance(g, int):
        return jnp.float32(ALIBI_SLOPES[g * NSA_GROUP + r])
    s = jnp.float32(ALIBI_SLOPES[r])
    for gi in range(1, NSA_KV_HEADS):
        s = jnp.where(g == gi, jnp.float32(ALIBI_SLOPES[gi * NSA_GROUP + r]), s)
    return s


def _nsa_cmp_kernel(qt_ref, kc_ref, vct_ref, poolt_ref, pair_ref, earlier_ref, oct_ref, selt_ref, list_ref,
                    *, n_sel):
    first_tile = pl.program_id(1) * CMP_TILES
    qt = selt_ref.shape[4]
    ncp = kc_ref.shape[2]
    n_slc = poolt_ref.shape[0]
    gq = qt_ref.shape[4]
    units = [(g, j) for j in range(CMP_TILES) for g in range(NSA_KV_HEADS)]
    times = [(first_tile + j) * qt + _iota((1, qt), 1) for j in range(CMP_TILES)]
    aux_row = _iota((AUX_COLS, gq), 0)
    col_head = _iota((1, gq), 1) // qt
    pad_rows = jnp.zeros((kc_ref.shape[3] - qt_ref.shape[3] - AUX_COLS, gq), BF16)

    def weights(g, j):
        slope_cols = jnp.zeros((1, gq), F32)
        for r in range(NSA_GROUP):
            slope_cols = jnp.where(col_head == r, _slope(g, r), slope_cols)
        aux = jnp.where(aux_row == 0, slope_cols * (CMP_STRIDE * CMP_AUX_SPLIT),
                        jnp.where(aux_row == 1, slope_cols * CMP_STRIDE, 0.0))
        return jnp.concatenate([qt_ref[0, g, j], aux.astype(BF16), pad_rows], axis=0)

    def importance(g, j, nr, nb):
        t = times[j]
        s = _dot(kc_ref[0, g, :nr, :], weights(g, j))
        edge = min(nr, 2 * CMP_ROWS_STEP)
        cmp_end = ((nr - edge) + _iota((edge, 1), 0)) * CMP_STRIDE + (CMP_LEN - 1)
        visible = cmp_end <= t
        vct = vct_ref[0, g * HEAD_DIM:(g + 1) * HEAD_DIM, :nr]
        psum = jnp.zeros((nr, qt), F32)
        for r in range(NSA_GROUP):
            cols = slice(r * qt, (r + 1) * qt)
            sr = s[:, cols]
            tail = jnp.where(visible, sr[nr - edge:], NEG_INF)
            sr = tail if edge == nr else jnp.concatenate([sr[:nr - edge], tail], axis=0)
            m = jnp.maximum(jnp.max(sr, axis=0, keepdims=True), SOFTMAX_FLOOR)
            e = jnp.exp(sr - m)
            l = jnp.sum(e, axis=0, keepdims=True)
            inv = jnp.where(l > 0.0, 1.0 / l, 0.0)
            oct_ref[0, g, j, :, cols] = (_dot(vct, e.astype(BF16)) * inv).astype(BF16)
            psum = psum + e * inv
        return _dot(poolt_ref[:nb, :nr], psum.astype(BF16))

    def visible_prefix(nr):
        nb = min(n_slc, nr * CMP_STRIDE // SLC_BLOCK)
        imp = jnp.concatenate([importance(g, j, nr, nb) for g, j in units], axis=1)
        blk = _iota((nb, 1), 0)
        cur = jnp.concatenate([times[j] // SLC_BLOCK for _, j in units], axis=1)
        forced = (blk == 0) | (blk == cur) | (blk == cur - 1)
        score = jnp.where(forced, TAKEN, jnp.where(blk <= cur, imp, NEG_INF))
        n_forced = 1 + jnp.where(cur >= 1, 1, 0) + jnp.where(cur >= 2, 1, 0)
        blk_f = blk.astype(F32)

        def take_one(score, active):
            m = jnp.max(score, axis=0, keepdims=True)
            first = jnp.min(jnp.where(score == m, blk_f, F32(1e9)), axis=0, keepdims=True)
            hit = (blk_f == first) if active is None else ((blk_f == first) & active)
            return jnp.where(hit, TAKEN, score)

        common_rounds = max(n_sel - 3, 0)
        for _ in range(common_rounds):
            score = take_one(score, None)

        def early_rounds(score):
            for k in range(common_rounds, n_sel - 1):
                score = take_one(score, n_sel - n_forced > k)
            return score

        score = lax.cond(first_tile * qt < 2 * SLC_BLOCK, early_rounds, lambda sc: sc, score)
        for u, (g, j) in enumerate(units):
            sel_g = score[:, u * qt:(u + 1) * qt] == TAKEN
            selt_ref[0, g, j, :nb, :] = jnp.where(sel_g, 0.0, NEG_INF)
            if nb < n_slc:
                selt_ref[0, g, j, nb:, :] = jnp.full((n_slc - nb, qt), NEG_INF, F32)
            count = _dot_nt(jnp.ones((8, qt), BF16), jnp.where(sel_g, 1.0, 0.0).astype(BF16))
            used = jnp.where(count > 0.0, 1.0, 0.0).astype(BF16)
            chunk = _iota((8, LANES), 1)
            flagged = jnp.where((_dot(used, pair_ref[:nb, :]) > 0.0) & (chunk < first_tile + j), 1.0, 0.0).astype(BF16)
            place = _dot(flagged, earlier_ref[...])
            lands = (_iota((LANES, LANES), 0).astype(F32) == place[0:1, :]) & (flagged[0:1, :] > 0)
            listed = _dot_nt(chunk.astype(BF16), jnp.where(lands, 1.0, 0.0).astype(BF16))
            n_listed = _dot(flagged, jnp.ones((LANES, LANES), BF16))
            list_ref[0, g, j] = jnp.where(_iota((8, LANES), 0) < 4, listed, n_listed).astype(jnp.int32)

    step = min(CMP_ROWS_STEP, ncp)
    rows_needed = jnp.minimum(((first_tile + CMP_TILES) * qt - CMP_LEN) // CMP_STRIDE + 1, ncp)
    n_steps = (rows_needed + step - 1) // step
    for k in range(ncp // step):
        pl.when(n_steps == k + 1)(functools.partial(visible_prefix, (k + 1) * step))


def _nsa_cmp(q_t, kc, vct, poolt_m, pair_m, earlier_m, n_sel):
    b, ng, n_tiles, qrows, qcols = q_t.shape
    ncp = kc.shape[2]
    n_slc = poolt_m.shape[0]
    const = lambda shape: pl.BlockSpec(shape, lambda bi, i: (0,) * len(shape))
    tile5 = lambda rows, cols: pl.BlockSpec((1, ng, CMP_TILES, rows, cols), lambda bi, i: (bi, 0, i, 0, 0))
    return pl.pallas_call(
        functools.partial(_nsa_cmp_kernel, n_sel=n_sel),
        grid=(b, n_tiles // CMP_TILES),
        in_specs=[
            tile5(qrows, qcols),
            pl.BlockSpec((1, ng, ncp, kc.shape[3]), lambda bi, i: (bi, 0, 0, 0)),
            pl.BlockSpec((1, ng * HEAD_DIM, ncp), lambda bi, i: (bi, 0, 0)),
            const(poolt_m.shape), const(pair_m.shape), const(earlier_m.shape),
        ],
        out_specs=[tile5(HEAD_DIM, qcols), tile5(n_slc, Q_TILE), tile5(8, LANES)],
        out_shape=[
            jax.ShapeDtypeStruct((b, ng, n_tiles, HEAD_DIM, qcols), BF16),
            jax.ShapeDtypeStruct((b, ng, n_tiles, n_slc, Q_TILE), F32),
            jax.ShapeDtypeStruct((b, ng, n_tiles, 8, LANES), jnp.int32),
        ],
        compiler_params=_params("parallel", "parallel"),
        name="nsa_compressed",
    )(q_t, kc, vct, poolt_m, pair_m, earlier_m)


def _nsa_slc_kernel(lists_ref, counts_ref, qt_ref, ks_ref, vst_ref, kw_ref, vwt_ref, selt_ref, oct_ref, gt_ref,
                    out_ref, m_ref, l_ref, acc_ref, ow_ref, sa_ref, sb_ref, sw_ref):
    bi = pl.program_id(0)
    g = pl.program_id(1)
    i = pl.program_id(2)
    n_tiles = pl.num_programs(2)
    qt = out_ref.shape[1]
    q_rows = qt_ref[0, 0, 0]
    gq = q_rows.shape[1]
    start = i * qt
    lane_f = _iota((1, qt), 1).astype(F32)
    key_in_chunk = _iota((KEY_CHUNK, qt), 0)
    query_in_tile = _iota((KEY_CHUNK, qt), 1)

    aux_row = _iota((AUX_COLS, gq), 0)
    col_head = _iota((1, gq), 1) // qt
    slope_cols = jnp.zeros((1, gq), F32)
    for r in range(NSA_GROUP):
        slope_cols = jnp.where(col_head == r, _slope(g, r), slope_cols)
    aux_base = jnp.where(aux_row == AUX_SLOPE, slope_cols, 0.0)
    pad_rows = jnp.zeros((ks_ref.shape[3] - q_rows.shape[0] - AUX_COLS, gq), BF16)

    def scores_of(slots):
        return [_dot(k, jnp.concatenate([q_rows, aux.astype(BF16), pad_rows], axis=0))
                for k, aux, _, _, _ in slots]

    def values_of(slots):
        values = jnp.concatenate([v for _, _, v, _, _ in slots], axis=1)
        return jnp.concatenate([values, jnp.ones((ONES_ROWS, values.shape[1]), BF16)], axis=0)

    def softmax_step(slots, scores, v_cat, r, m_old):
        cols = slice(r * qt, (r + 1) * qt)
        srs, tops = [], []
        for j, (_, _, _, shift, mask) in enumerate(slots):
            sr = scores[j, :, cols] if hasattr(scores, "at") else scores[j][:, cols]
            sr = sr if mask is None else jnp.where(mask, sr, NEG_INF)
            srs.append(sr)
            tops.append(jnp.max(sr, axis=0, keepdims=True) + shift[r])
        m_new = functools.reduce(jnp.maximum, tops, m_old)
        ps = [jnp.exp((sr - (m_new - slot[3][r])).astype(BF16)) for slot, sr in zip(slots, srs)]
        weighted = _dot(v_cat, jnp.concatenate(ps, axis=0))
        return m_new, weighted[HEAD_DIM:HEAD_DIM + 1], weighted[:HEAD_DIM]

    def shifts(dist0, ok):
        rows = [-_slope(g, r) * (dist0 + lane_f) for r in range(NSA_GROUP)]
        return rows if ok is None else [jnp.where(ok, row, NEG_INF) for row in rows]

    tile_id = (bi * NSA_KV_HEADS + g) * n_tiles + i
    n_listed = counts_ref[tile_id]

    def selected_slot(c, ok, mask):
        at = pl.multiple_of(c * KEY_CHUNK, KEY_CHUNK)
        bias = selt_ref[0, 0, 0, pl.ds(c * BLOCKS_PER_CHUNK, BLOCKS_PER_CHUNK), :]
        aux = aux_base
        for blk in range(BLOCKS_PER_CHUNK):
            aux = jnp.where(aux_row == blk, jnp.concatenate([bias[blk:blk + 1]] * NSA_GROUP, axis=1), aux)
        return (ks_ref[0, 0, pl.ds(at, KEY_CHUNK), :], aux, vst_ref[0, 0, c],
                shifts((start - c * KEY_CHUNK).astype(F32), ok), mask)

    def listed_slot(idx):
        ok = idx < n_listed
        c = jnp.where(ok, lists_ref[tile_id * LANES + jnp.minimum(idx, jnp.maximum(n_listed - 1, 0))], 0)
        return selected_slot(c, ok, None)

    floor = jnp.full((1, qt), SOFTMAX_FLOOR, F32)

    n_back = WINDOW // KEY_CHUNK
    slots = []
    for j in range(n_back + 1):
        cs = start - WINDOW + j * KEY_CHUNK
        chunk = jnp.maximum(cs, 0) // KEY_CHUNK
        at = pl.multiple_of(chunk * KEY_CHUNK, KEY_CHUNK)
        mask = (query_in_tile < key_in_chunk) if j == 0 else (
            (key_in_chunk <= query_in_tile) if j == n_back else None)
        slots.append((kw_ref[0, 0, pl.ds(at, KEY_CHUNK), :], aux_base, vwt_ref[0, 0, chunk],
                      shifts(F32(WINDOW - j * KEY_CHUNK), cs >= 0), mask))
    window_slots = slots
    first_slots = [selected_slot(i, None, key_in_chunk <= query_in_tile)] + [
        listed_slot(j) for j in range(ATTN_BATCH - 1)]

    def listed_batch(n):
        return [listed_slot(ATTN_BATCH - 1 + n * ATTN_BATCH + j) for j in range(ATTN_BATCH)]

    def issue(slots, dst_ref):
        for j, s in enumerate(scores_of(slots)):
            dst_ref[j] = s

    issue(window_slots, sw_ref)
    issue(first_slots, sb_ref)
    v_cat = values_of(window_slots)
    for r in range(NSA_GROUP):
        _, total, weighted = softmax_step(window_slots, sw_ref, v_cat, r, floor)
        ow_ref[:, r * qt:(r + 1) * qt] = weighted * jnp.where(total > 0.0, 1.0 / total, 0.0)

    def consume(slots, src_ref):
        v_cat = values_of(slots)
        for r in range(NSA_GROUP):
            cols = slice(r * qt, (r + 1) * qt)
            m_old = m_ref[r]
            m_new, total, weighted = softmax_step(slots, src_ref, v_cat, r, m_old)
            alpha = jnp.exp(m_old - m_new)
            l_ref[r] = alpha * l_ref[r] + total
            acc_ref[:, cols] = alpha * acc_ref[:, cols] + weighted
            m_ref[r] = m_new

    issue(listed_batch(0), sa_ref)
    v_cat = values_of(first_slots)
    for r in range(NSA_GROUP):
        m_ref[r], l_ref[r], acc_ref[:, r * qt:(r + 1) * qt] = softmax_step(first_slots, sb_ref, v_cat, r, floor)

    def batch_pair(it, carry):
        first, second, third = (listed_batch(2 * it + n) for n in range(3))
        issue(second, sb_ref)
        consume(first, sa_ref)
        issue(third, sa_ref)
        consume(second, sb_ref)
        return carry

    n_rest = jnp.maximum(n_listed - (ATTN_BATCH - 1), 0)
    n_batches = (n_rest + ATTN_BATCH - 1) // ATTN_BATCH
    lax.fori_loop(0, n_batches // 2, batch_pair, 0)

    @pl.when(n_batches % 2 == 1)
    def _():
        consume(listed_batch(n_batches - 1), sa_ref)

    def finalize(r):
        l = l_ref[r]
        return acc_ref[:, r * qt:(r + 1) * qt] * jnp.where(l > 0.0, 1.0 / l, 0.0)

    outs = []
    for r in range(NSA_GROUP):
        cols = slice(r * qt, (r + 1) * qt)
        col = (g * NSA_GROUP + r) * NSA_BRANCHES
        gate = lambda br: jax.nn.sigmoid(gt_ref[0, pl.ds(col + br, 1), :])
        outs.append(gate(0) * oct_ref[0, 0, 0, :, cols].astype(F32) + gate(1) * finalize(r)
                    + gate(2) * ow_ref[:, cols])
    out_ref[0] = jnp.concatenate(outs, axis=0).T.astype(out_ref.dtype)


def _nsa_slc(lists, counts, q_t, ks, vst, kw, vwt, selt, oct, gates_t):
    b, _, t, kw_cols = ks.shape
    _, _, n_tiles, qrows, qcols = q_t.shape
    n_slc = selt.shape[3]
    n_chunks = vst.shape[2]
    gw = NSA_GROUP * HEAD_DIM
    once = dict(pipeline_mode=pl.Buffered(1))
    k_spec = pl.BlockSpec((1, 1, t, kw_cols), lambda bi, g, i, ls, ns: (bi, g, 0, 0), **once)
    vt_spec = pl.BlockSpec((1, 1, n_chunks, HEAD_DIM, KEY_CHUNK), lambda bi, g, i, ls, ns: (bi, g, 0, 0, 0), **once)
    tile5 = lambda rows, cols: pl.BlockSpec((1, 1, 1, rows, cols), lambda bi, g, i, ls, ns: (bi, g, i, 0, 0))
    grid_spec = pltpu.PrefetchScalarGridSpec(
        num_scalar_prefetch=2,
        grid=(b, NSA_KV_HEADS, n_tiles),
        in_specs=[
            tile5(qrows, qcols),
            k_spec, vt_spec, k_spec, vt_spec,
            tile5(n_slc, Q_TILE),
            tile5(HEAD_DIM, qcols),
            pl.BlockSpec((1, GATE_PAD, Q_TILE), lambda bi, g, i, ls, ns: (bi, 0, i)),
        ],
        out_specs=pl.BlockSpec((1, Q_TILE, gw), lambda bi, g, i, ls, ns: (bi, i, g)),
        scratch_shapes=[pltpu.VMEM((NSA_GROUP, 1, Q_TILE), F32), pltpu.VMEM((NSA_GROUP, 1, Q_TILE), F32),
                        pltpu.VMEM((HEAD_DIM, qcols), F32), pltpu.VMEM((HEAD_DIM, qcols), F32),
                        pltpu.VMEM((ATTN_BATCH, KEY_CHUNK, qcols), F32),
                        pltpu.VMEM((ATTN_BATCH, KEY_CHUNK, qcols), F32),
                        pltpu.VMEM((WINDOW // KEY_CHUNK + 1, KEY_CHUNK, qcols), F32)],
    )
    return pl.pallas_call(
        _nsa_slc_kernel,
        grid_spec=grid_spec,
        out_shape=jax.ShapeDtypeStruct((b, t, NSA_HEADS * HEAD_DIM), BF16),
        compiler_params=_params("parallel", "parallel", "parallel"),
        name="nsa_selected_window",
    )(lists, counts, q_t, ks, vst, kw, vwt, selt, oct, gates_t)


POOL_HALO = 16
CONV_HALO = 8


def _merge_kernel(h_ref, g_ref, nsa_ref, pool_ref, pool_halo_ref, conv_ref, conv_halo_ref,
                  wmg_ref, wnsa_ref, pool_bd_ref, pool_scale_ref, wpool_ref, convw_ref, wconv_ref, wo_ref,
                  g2_ref, wr_ref, br_ref, before_ref,
                  out_ref, xn_ref, comb_ref, count_ref, pool_ext, conv_ext, *, seq_len):
    i = pl.program_id(0)
    tm, d = h_ref.shape
    cw = pool_ref.shape[1]
    pos0 = (i * tm) % seq_len
    keep_halo = jnp.where(pos0 == 0, 0.0, 1.0)
    pos = pos0 + _iota((tm, 1), 0)

    u = pool_ref[...]
    pool_ext[0:POOL_HALO, :] = pool_halo_ref[...] * keep_halo
    pool_ext[POOL_HALO:, :] = u
    lane_group = _iota((1, cw), 1) // (cw // len(POOL_WINDOWS))
    total = u
    mean = jnp.zeros_like(u)
    done = 1
    for gi, win in enumerate(POOL_WINDOWS):
        for k in range(done, win):
            total = total + pool_ext[POOL_HALO - k:POOL_HALO - k + tm, :]
        done = win
        cnt = jnp.minimum(pos + 1, win).astype(F32)
        mean = jnp.where(lane_group == gi, total / cnt, mean)
    pooled = (mean - u).astype(BF16)
    mixed = _dot(pooled, pool_bd_ref[...]) * pool_scale_ref[...]
    y_pool = _dot(mixed.astype(BF16), wpool_ref[...])

    ch = conv_ref[:, 0:cw]
    cb = conv_ref[:, cw:2 * cw]
    cc = conv_ref[:, 2 * cw:3 * cw]
    conv_ext[0:CONV_HALO, :] = conv_halo_ref[:, 0:cw] * conv_halo_ref[:, 2 * cw:3 * cw] * keep_halo
    conv_ext[CONV_HALO:, :] = cc * ch
    y = jnp.zeros((tm, cw), F32)
    for k in range(CONV_K):
        off = CONV_HALO - (CONV_K - 1) + k
        y = y + convw_ref[k:k + 1, :] * conv_ext[off:off + tm, :]
    y_conv = _dot((cb * y).astype(BF16), wconv_ref[...])

    y_nsa = _dot(nsa_ref[...], wnsa_ref[...])

    h = h_ref[...]
    xn = _rms_norm(h, g_ref[...]).astype(BF16)
    merged = jnp.zeros((tm, d), F32)
    for br, y_br in enumerate((y_nsa, y_pool, y_conv)):
        mg = jax.nn.sigmoid(_dot(xn, wmg_ref[:, br * d:(br + 1) * d]))
        merged = merged + mg * y_br
    h_new = h + _dot(merged.astype(BF16), wo_ref[...])
    out_ref[...] = h_new
    _route_tile(h_new, g2_ref, wr_ref, br_ref, before_ref, xn_ref, comb_ref, count_ref)


def _merge(h, g, nsa, pool_u, conv, wmg, wnsa, pool_bd, pool_scale, wpool, convw, wconv, wo,
           g2, wr, br, before, seq_len):
    n, d = h.shape
    tm = TOKEN_TILE
    cw = pool_u.shape[1]
    row = lambda width: pl.BlockSpec((tm, width), lambda i: (i, 0))
    full = lambda a: pl.BlockSpec(a.shape, lambda i: (0,) * a.ndim)
    halo = lambda rows, width: pl.BlockSpec(
        (rows, width), lambda i: (jnp.maximum(i * (tm // rows) - 1, 0), 0))
    return pl.pallas_call(
        functools.partial(_merge_kernel, seq_len=seq_len),
        grid=(n // tm,),
        in_specs=[row(d), full(g), row(nsa.shape[1]), row(cw), halo(POOL_HALO, cw),
                  row(conv.shape[1]), halo(CONV_HALO, conv.shape[1]),
                  full(wmg), full(wnsa), full(pool_bd), full(pool_scale), full(wpool), full(convw),
                  full(wconv), full(wo), full(g2), full(wr), full(br), full(before)],
        out_specs=[row(d), row(d + ROUTER_PAD), row(ROUTER_PAD), pl.BlockSpec((8, ROUTER_PAD), lambda i: (0, 0))],
        out_shape=[jax.ShapeDtypeStruct((n, d), F32), jax.ShapeDtypeStruct((n, d + ROUTER_PAD), BF16),
                   jax.ShapeDtypeStruct((n, ROUTER_PAD), F32), jax.ShapeDtypeStruct((8, ROUTER_PAD), F32)],
        scratch_shapes=[pltpu.VMEM((tm + POOL_HALO, cw), F32), pltpu.VMEM((tm + CONV_HALO, cw), F32)],
        compiler_params=_params("arbitrary"),
        name="merge",
    )(h, g, nsa, pool_u, pool_u, conv, conv, wmg, wnsa, pool_bd, pool_scale, wpool, convw, wconv, wo,
      g2, wr, br, before)


def _route(logits):
    lane = _iota(logits.shape, 1)
    lane_f = lane.astype(F32)
    big = F32(1e9)
    is_group = lane < N_EXPERT_GROUPS
    gl = jnp.where(is_group, logits, NEG_INF)
    g_max = jnp.max(gl, axis=1, keepdims=True)
    g_sel = jnp.min(jnp.where(gl == g_max, lane_f, big), axis=1, keepdims=True)
    g_prob = 1.0 / jnp.sum(jnp.where(is_group, jnp.exp(gl - g_max), 0.0), axis=1, keepdims=True)
    lo = N_EXPERT_GROUPS + EXPERTS_PER_GROUP * g_sel
    in_group = (lane_f >= lo) & (lane_f < lo + EXPERTS_PER_GROUP)
    el = jnp.where(in_group, logits, NEG_INF)
    v1 = jnp.max(el, axis=1, keepdims=True)
    i1 = jnp.min(jnp.where((el == v1) & in_group, lane_f, big), axis=1, keepdims=True)
    el2 = jnp.where(lane_f == i1, NEG_INF, el)
    rest = in_group & (lane_f != i1)
    v2 = jnp.max(el2, axis=1, keepdims=True)
    i2 = jnp.min(jnp.where((el2 == v2) & rest, lane_f, big), axis=1, keepdims=True)
    e2 = jnp.exp(v2 - v1)
    w1 = g_prob / (1.0 + e2)
    w2 = g_prob * e2 / (1.0 + e2)
    return jnp.where(lane_f == i1, w1, 0.0) + jnp.where(lane_f == i2, w2, 0.0), g_sel


GROUP_LANE = N_EXPERT_GROUPS + N_EXPERTS


RANK_LANE = GROUP_LANE + 1


def _route_tile(h, g_ref, wr_ref, br_ref, before_ref, xn_ref, comb_ref, count_ref):
    @pl.when(pl.program_id(0) == 0)
    def _():
        count_ref[...] = jnp.zeros(count_ref.shape, F32)

    xn = _rms_norm(h, g_ref[...])
    xn_hi = xn.astype(BF16)
    xn_lo = (xn - xn_hi.astype(F32)).astype(BF16)
    logits = (_dot(xn_hi, wr_ref[0]) + (_dot(xn_hi, wr_ref[1]) + _dot(xn_lo, wr_ref[0]))) + br_ref[...]
    comb, g_sel = _route(logits)
    lane = _iota(comb.shape, 1)
    chose = jnp.where(lane.astype(F32) == g_sel, 1.0, 0.0)
    earlier = _dot(before_ref[...], chose.astype(BF16)) + count_ref[0:1, :]
    rank = jnp.sum(chose * earlier, axis=1, keepdims=True)
    count_ref[0:1, :] = count_ref[0:1, :] + jnp.sum(chose, axis=0, keepdims=True)
    comb_ref[...] = jnp.where(lane == GROUP_LANE, g_sel, jnp.where(lane == RANK_LANE, rank, comb))
    first = N_EXPERT_GROUPS + EXPERTS_PER_GROUP * g_sel
    local = jnp.zeros(comb.shape, F32)
    for e in range(EXPERTS_PER_GROUP):
        c_e = jnp.sum(jnp.where(lane.astype(F32) == first + e, comb, 0.0), axis=1, keepdims=True)
        local = jnp.where((lane == e) | (lane == EXPERTS_PER_GROUP + e), c_e, local)
    local_hi = local.astype(BF16)
    d = xn_hi.shape[1]
    xn_ref[:, :d] = xn_hi
    xn_ref[:, d:] = jnp.where(lane < EXPERTS_PER_GROUP, local_hi, (local - local_hi.astype(F32)).astype(BF16))


def _experts_kernel(tile_group_ref, n_active_ref, x_ref, w1_ref, w3_ref, w2_ref, out_ref, acc_ref):
    i = pl.program_id(0)

    @pl.when(i < n_active_ref[0])
    def _():
        d = out_ref.shape[1]
        x = x_ref[:, :d]
        comb = x_ref[:, d:].astype(F32)
        lane = _iota(comb.shape, 1)
        for e in range(EXPERTS_PER_GROUP):
            c_e = jnp.sum(jnp.where((lane == e) | (lane == EXPERTS_PER_GROUP + e), comb, 0.0),
                          axis=1, keepdims=True)
            a = (jax.nn.silu(_dot(x, w1_ref[0, 0, e].astype(BF16)))
                 * _dot(x, w3_ref[0, 0, e].astype(BF16))) * c_e
            y = _dot(a.astype(BF16), w2_ref[0, 0, e].astype(BF16))
            if e == 0:
                acc_ref[...] = y
            else:
                acc_ref[...] += y
        out_ref[...] = acc_ref[...].astype(out_ref.dtype)

    @pl.when(i >= n_active_ref[0])
    def _():
        out_ref[...] = jnp.zeros(out_ref.shape, out_ref.dtype)


def _experts(tile_group, n_active, x_sorted, w1, w3, w2, layer):
    ns = x_sorted.shape[0]
    d = w1.shape[3]
    tm = MOE_SORT_TILE
    group_w = lambda w: pl.BlockSpec((1, 1) + w.shape[2:], lambda i, tg, na: (layer, tg[i], 0, 0, 0),
                                     pipeline_mode=pl.Buffered(1))
    grid_spec = pltpu.PrefetchScalarGridSpec(
        num_scalar_prefetch=2,
        grid=(ns // tm,),
        in_specs=[
            pl.BlockSpec((tm, x_sorted.shape[1]), lambda i, tg, na: (i, 0)),
            group_w(w1), group_w(w3), group_w(w2),
        ],
        out_specs=pl.BlockSpec((tm, d), lambda i, tg, na: (i, 0)),
        scratch_shapes=[pltpu.VMEM((tm, d), F32)],
    )
    return pl.pallas_call(
        _experts_kernel,
        grid_spec=grid_spec,
        out_shape=jax.ShapeDtypeStruct((ns, d), BF16),
        compiler_params=_params("arbitrary"),
        name="experts",
    )(tile_group, n_active, x_sorted, w1, w3, w2)


def _residual_kernel(h_ref, y_ref, gf_ref, out_ref, *, final_norm):
    out = h_ref[...] + y_ref[...].astype(F32)
    out_ref[...] = _rms_norm(out, gf_ref[...]) if final_norm else out


def _residual(h, y, gf, final_norm):
    n, d = h.shape
    tm = MOE_TILE
    row = pl.BlockSpec((tm, d), lambda i: (i, 0))
    return pl.pallas_call(
        functools.partial(_residual_kernel, final_norm=final_norm),
        grid=(n // tm,),
        in_specs=[row, row, pl.BlockSpec(gf.shape, lambda i: (0, 0))],
        out_specs=row,
        out_shape=jax.ShapeDtypeStruct((n, d), F32),
        compiler_params=_params("parallel"),
        name="residual",
    )(h, y, gf)


def _group_sort_plan(group_id, rank, counts, tile):
    n = group_id.shape[0]
    n_slots = n + N_EXPERT_GROUPS * tile
    padded = (counts + tile - 1) // tile * tile
    ends = jnp.cumsum(padded)
    slot = (ends - padded)[group_id] + rank
    source = jnp.zeros((n_slots,), jnp.int32).at[slot].set(jnp.arange(n, dtype=jnp.int32))
    tile_start = jnp.arange(n_slots // tile, dtype=jnp.int32) * tile
    tile_group = jnp.minimum(jnp.searchsorted(ends, tile_start, side="right"), N_EXPERT_GROUPS - 1)
    return slot, source, tile_group.astype(jnp.int32), (ends[-1:] // tile).astype(jnp.int32)


def _moe(h, xn, comb, counts, w1, w3, w2, layer, gf, final_norm):
    as_int = lambda a: a.astype(jnp.int32)
    slot, source, tile_group, n_active = _group_sort_plan(
        as_int(comb[:, GROUP_LANE]), as_int(comb[:, RANK_LANE]), as_int(counts[0, :N_EXPERT_GROUPS]),
        MOE_SORT_TILE)
    grouped = lambda w: w.reshape((w.shape[0], N_EXPERT_GROUPS, EXPERTS_PER_GROUP) + w.shape[2:])
    rows = lambda a, idx: jnp.take(a, idx, axis=0, mode="clip")
    y_sorted = _experts(tile_group, n_active, rows(xn, source), grouped(w1), grouped(w3), grouped(w2), layer)
    return _residual(h, rows(y_sorted, slot), gf, final_norm)


def _selection_constants(seq_len):
    ncp = seq_len // CMP_STRIDE
    n_slc = seq_len // SLC_BLOCK
    ratio = SLC_BLOCK // CMP_STRIDE
    lead = CMP_LEN // CMP_STRIDE - 1
    c = np.arange(ncp)[:, None]
    j = np.arange(n_slc)[None, :]
    pool_m = ((c >= ratio * j - lead) & (c < ratio * j + ratio)).astype(np.float32)
    blocks_per_chunk = KEY_CHUNK // SLC_BLOCK
    n_chunks = seq_len // KEY_CHUNK
    pair_m = np.zeros((n_slc, LANES * ((n_chunks + LANES - 1) // LANES)), np.float32)
    pair_m[np.arange(n_slc), np.arange(n_slc) // blocks_per_chunk] = 1.0
    earlier_m = np.triu(np.ones((LANES, LANES), np.float32), 1)
    key_aux = np.zeros((TOKEN_TILE, LANES), np.float32)
    in_chunk = np.arange(TOKEN_TILE) % KEY_CHUNK
    key_aux[np.arange(TOKEN_TILE), HEAD_DIM + in_chunk // SLC_BLOCK] = 1.0
    key_aux[:, HEAD_DIM + AUX_SLOPE] = in_chunk
    cmp_aux = np.zeros((ncp, LANES), np.float32)
    cmp_aux[:, HEAD_DIM] = np.arange(ncp) // CMP_AUX_SPLIT
    cmp_aux[:, HEAD_DIM + 1] = np.arange(ncp) % CMP_AUX_SPLIT
    as_bf16 = lambda a: jnp.asarray(a, BF16)
    return as_bf16(pool_m.T), as_bf16(pair_m), as_bf16(earlier_m), jnp.asarray(key_aux), jnp.asarray(cmp_aux)


def kernel(x, norm1_g, w_in, cmp_pe, cmp_w1, cmp_w2, w_nsa_proj, pool_w, pool_scale, w_pool_proj, conv_w,
           w_conv_proj, w_o, norm2_g, router_group_w, router_group_b, router_expert_w, router_expert_b,
           expert_w1, expert_w3, expert_w2, final_norm_g):
    b, t, d = x.shape
    n = b * t
    depth = w_in.shape[0]
    dq = NSA_HEADS * HEAD_DIM
    dkv = 6 * NSA_KV_HEADS * HEAD_DIM
    dgate = NSA_HEADS * NSA_BRANCHES
    cw = d // 4
    assert t % TOKEN_TILE == 0 and n % MOE_TILE == 0 and t % (Q_TILE * CMP_TILES) == 0
    n_slc = t // SLC_BLOCK
    n_sel = min(SLC_TOPN, n_slc)
    n_chunks16 = t // CMP_STRIDE
    kvw = NSA_KV_HEADS * HEAD_DIM
    poolt_m, pair_m, earlier_m, key_aux, cmp_aux = _selection_constants(t)
    assert Q_TILE == KEY_CHUNK and AUX_SLOPE < AUX_COLS and KEY_CHUNK <= 256
    assert n_chunks16 <= 256 * CMP_AUX_SPLIT and n_chunks16 % min(CMP_ROWS_STEP, n_chunks16) == 0
    assert pair_m.shape[1] == LANES

    before = jnp.asarray(np.tril(np.ones((TOKEN_TILE, TOKEN_TILE), np.float32), -1), BF16)
    h = x.reshape(n, d)
    for l in range(depth):
        wl = w_in[l]
        o_gate = dq + dkv
        o_pool = o_gate + dgate
        o_merge = o_pool + cw + 3 * cw
        kv_cols = lambda kind: wl[:, dq + kind * kvw:dq + (kind + 1) * kvw]
        wq_t = (wl[:, :dq] * (HEAD_DIM ** -0.5)).T.astype(BF16)
        wv_t = jnp.concatenate([kv_cols(3), kv_cols(5)], axis=1).T.astype(BF16)
        wg_t = jnp.pad(wl[:, o_gate:o_pool], ((0, 0), (0, GATE_PAD - dgate))).T.astype(BF16)
        no_aux = jnp.zeros((d, LANES - HEAD_DIM), F32)
        wk = jnp.concatenate([piece for kind in (2, 4) for gi in range(NSA_KV_HEADS)
                              for piece in (kv_cols(kind)[:, gi * HEAD_DIM:(gi + 1) * HEAD_DIM], no_aux)],
                             axis=1).astype(BF16)
        wn = jnp.concatenate([kv_cols(0), kv_cols(1), wl[:, o_pool:o_merge]], axis=1).astype(BF16)
        wmg = wl[:, o_merge:].astype(BF16)
        pool_bd = jax.scipy.linalg.block_diag(*[pool_w[l, gi] for gi in range(pool_w.shape[1])]).astype(BF16)
        convw = jnp.pad(conv_w[l], ((0, 8 - CONV_K), (0, 0)))
        wr = jnp.pad(jnp.concatenate([router_group_w[l], router_expert_w[l]], axis=1),
                     ((0, 0), (0, ROUTER_PAD - N_EXPERT_GROUPS - N_EXPERTS)))
        wr_hi = wr.astype(BF16)
        wr = jnp.stack([wr_hi, (wr - wr_hi.astype(F32)).astype(BF16)])
        br = jnp.pad(jnp.concatenate([router_group_b[l], router_expert_b[l]]),
                     (0, ROUTER_PAD - N_EXPERT_GROUPS - N_EXPERTS))[None, :]
        pe = jnp.broadcast_to(cmp_pe[l].reshape(2, 1, CMP_LEN * HEAD_DIM), (2, 8, CMP_LEN * HEAD_DIM)).astype(BF16)
        halves = CMP_LEN // CMP_STRIDE
        w1_bd = jnp.einsum("khldc,gq->khlgdqc",
                           cmp_w1[l].reshape(2, halves, CMP_STRIDE, HEAD_DIM, CMP_HIDDEN),
                           jnp.eye(NSA_KV_HEADS, dtype=F32))
        w1_bd = w1_bd.reshape(2, halves, CMP_STRIDE * kvw, NSA_KV_HEADS * CMP_HIDDEN).astype(BF16)
        w2_k = jnp.pad(cmp_w2[l, 0], ((0, 0), (0, LANES - HEAD_DIM))).astype(BF16)
        w2_vt = cmp_w2[l, 1].T.astype(BF16)

        q_t, vst, vwt, gates_t, ks, kw, cmp_src, pool_u, conv = _inproj(
            h, norm1_g[l][None, :], wq_t, wv_t, wg_t, wk, wn, key_aux, b)
        kc_aux, vc_t = _compress(cmp_src, pe, cmp_w1[l].astype(BF16), w1_bd, w2_k, w2_vt, cmp_aux)
        oc_t, sel_t, listed = _nsa_cmp(q_t, kc_aux, vc_t, poolt_m, pair_m, earlier_m, n_sel)
        nsa = _nsa_slc(listed[:, :, :, 0, :].reshape(-1), listed[:, :, :, 4, 0].reshape(-1),
                       q_t, ks, vst, kw, vwt, sel_t, oc_t, gates_t)
        h, xn, comb, counts = _merge(
            h, norm1_g[l][None, :], nsa.reshape(n, dq), pool_u, conv, wmg,
            w_nsa_proj[l].astype(BF16), pool_bd, pool_scale[l][None, :], w_pool_proj[l].astype(BF16),
            convw, w_conv_proj[l].astype(BF16), w_o[l].astype(BF16), norm2_g[l][None, :], wr, br, before, t)
        h = _moe(h, xn, comb, counts, expert_w1, expert_w3, expert_w2, l,
                 final_norm_g[None, :], final_norm=(l == depth - 1))
    return h.reshape(b, t, d)
```

```python
import functools

import jax
import jax.numpy as jnp
import numpy as np
from jax import lax
from jax.experimental import pallas as pl
from jax.experimental.pallas import tpu as pltpu

F32 = jnp.float32
BF16 = jnp.bfloat16

HEAD_DIM = 64
NSA_HEADS = 8
NSA_KV_HEADS = 2
NSA_GROUP = NSA_HEADS // NSA_KV_HEADS
CMP_LEN = 32
CMP_STRIDE = 16
CMP_HIDDEN = 4 * HEAD_DIM
SLC_BLOCK = 64
SLC_TOPN = 16
WINDOW = 512
NSA_BRANCHES = 3
POOL_WINDOWS = (2, 4, 8, 16)
CONV_K = 3
N_EXPERT_GROUPS = 4
EXPERTS_PER_GROUP = 8
N_EXPERTS = N_EXPERT_GROUPS * EXPERTS_PER_GROUP
RMS_EPS = 1e-6
NEG_INF = -1e30
ALIBI_SLOPES = tuple(float(2.0 ** (-8.0 * (h + 1) / NSA_HEADS)) for h in range(NSA_HEADS))

LANES = 128
VMEM_LIMIT = 56 * 1024 * 1024
TOKEN_TILE = 512
MOE_TILE = 1024
MOE_SORT_TILE = 512
Q_TILE = 128
KEY_CHUNK = 128
GATE_PAD = LANES
ROUTER_PAD = LANES


def _params(*semantics):
    return pltpu.CompilerParams(dimension_semantics=semantics, vmem_limit_bytes=VMEM_LIMIT)


def _dot(a, b):
    return jnp.dot(a, b, preferred_element_type=F32)


def _dot_nt(a, b):
    return lax.dot_general(a, b, (((1,), (1,)), ((), ())), preferred_element_type=F32)


def _rms_norm(x, g):
    y = x * lax.rsqrt(jnp.mean(x * x, axis=-1, keepdims=True) + RMS_EPS)
    return y * g


def _iota(shape, dim):
    return lax.broadcasted_iota(jnp.int32, shape, dim)


def _inproj_kernel(x_ref, g_ref, wq_ref, wv_ref, wg_ref, wk_ref, wn_ref, kaux_ref,
                   q_ref, vs_ref, vw_ref, gate_ref, ks_ref, kw_ref, cmp_ref, pool_ref, conv_ref):
    xn = _rms_norm(x_ref[...], g_ref[...]).astype(BF16)
    sub_tiles = x_ref.shape[0] // Q_TILE
    q_t = _dot_nt(wq_ref[...], xn)
    for g in range(NSA_KV_HEADS):
        for j in range(sub_tiles):
            for r in range(NSA_GROUP):
                head = g * NSA_GROUP + r
                q_ref[0, g, j, :, r * Q_TILE:(r + 1) * Q_TILE] = q_t[
                    head * HEAD_DIM:(head + 1) * HEAD_DIM, j * Q_TILE:(j + 1) * Q_TILE].astype(BF16)
    v_t = _dot_nt(wv_ref[...], xn)
    k = _dot(xn, wk_ref[...])
    for branch, (v_ref, k_ref) in enumerate(((vs_ref, ks_ref), (vw_ref, kw_ref))):
        for g in range(NSA_KV_HEADS):
            slab = branch * NSA_KV_HEADS + g
            for j in range(sub_tiles):
                v_ref[0, g, j] = v_t[slab * HEAD_DIM:(slab + 1) * HEAD_DIM,
                                     j * KEY_CHUNK:(j + 1) * KEY_CHUNK].astype(BF16)
            k_ref[0, g] = (k[:, slab * LANES:(slab + 1) * LANES] + kaux_ref[...]).astype(BF16)
    gate_ref[0] = _dot_nt(wg_ref[...], xn)
    col = 0
    for ref in (cmp_ref, pool_ref, conv_ref):
        width = ref.shape[-1]
        ref[...] = _dot(xn, wn_ref[:, col:col + width]).reshape(ref.shape)
        col += width


def _inproj(h, g, wq_t, wv_t, wg_t, wk, wn, kaux, batch):
    n, d = h.shape
    t = n // batch
    tm = TOKEN_TILE
    steps = t // tm
    sub = tm // Q_TILE
    cw = d // 4
    gq = NSA_GROUP * Q_TILE
    full = lambda a: pl.BlockSpec(a.shape, lambda i: (0,) * a.ndim)
    row = lambda width: pl.BlockSpec((tm, width), lambda i: (i, 0))
    tiles = lambda rows, cols: pl.BlockSpec((1, NSA_KV_HEADS, sub, rows, cols),
                                            lambda i: (i // steps, 0, i % steps, 0, 0))
    keys = pl.BlockSpec((1, NSA_KV_HEADS, tm, LANES), lambda i: (i // steps, 0, i % steps, 0))
    sds = jax.ShapeDtypeStruct
    v_shape = sds((batch, NSA_KV_HEADS, t // KEY_CHUNK, HEAD_DIM, KEY_CHUNK), BF16)
    k_shape = sds((batch, NSA_KV_HEADS, t, LANES), BF16)
    return pl.pallas_call(
        _inproj_kernel,
        grid=(n // tm,),
        in_specs=[row(d), full(g), full(wq_t), full(wv_t), full(wg_t), full(wk), full(wn), full(kaux)],
        out_specs=[tiles(HEAD_DIM, gq), tiles(HEAD_DIM, KEY_CHUNK), tiles(HEAD_DIM, KEY_CHUNK),
                   pl.BlockSpec((1, GATE_PAD, tm), lambda i: (i // steps, 0, i % steps)),
                   keys, keys,
                   pl.BlockSpec((1, tm, 2 * NSA_KV_HEADS * HEAD_DIM), lambda i: (i // steps, i % steps, 0)),
                   row(cw), row(3 * cw)],
        out_shape=[sds((batch, NSA_KV_HEADS, t // Q_TILE, HEAD_DIM, gq), BF16), v_shape, v_shape,
                   sds((batch, GATE_PAD, t), F32), k_shape, k_shape,
                   sds((batch, t, 2 * NSA_KV_HEADS * HEAD_DIM), F32), sds((n, cw), F32), sds((n, 3 * cw), F32)],
        compiler_params=_params("parallel"),
        name="inproj",
    )(h, g, wq_t, wv_t, wg_t, wk, wn, kaux)


def _gelu_tanh(x):
    return 0.5 * x * (1.0 + jnp.tanh(0.7978845608028654 * (x + 0.044715 * x * x * x)))


def _compress_kernel(src_ref, pe_ref, w1_ref, w1bd_ref, w2k_ref, w2vt_ref, caux_ref, kc_ref, vct_ref):
    kind = pl.program_id(1)
    ncp = kc_ref.shape[2]
    hidden = w1_ref.shape[2]
    pieces = [src_ref[0, pl.ds(l, ncp, stride=CMP_STRIDE), :].astype(BF16) for l in range(CMP_STRIDE)]
    chunk = jnp.concatenate(pieces, axis=1)
    first = _dot(chunk, w1bd_ref[0, 0])
    second = _dot(chunk, w1bd_ref[0, 1])
    bias = _dot(pe_ref[0], w1_ref[0])[0:1, :]
    hid = first + pltpu.roll(second, ncp - 1, 0) + jnp.concatenate([bias] * NSA_KV_HEADS, axis=1)
    row = _iota((ncp, 1), 0)
    act = jnp.where(row < ncp - 1, _gelu_tanh(hid), 0.0).astype(BF16)
    for g in range(NSA_KV_HEADS):
        act_g = act[:, g * hidden:(g + 1) * hidden]

        @pl.when(kind == 0)
        def _():
            kc_ref[0, g] = (_dot(act_g, w2k_ref[...]) + caux_ref[...]).astype(BF16)

        @pl.when(kind == 1)
        def _():
            vct_ref[0, g * HEAD_DIM:(g + 1) * HEAD_DIM, :] = _dot_nt(w2vt_ref[...], act_g).astype(BF16)


def _compress(src, pe, w1, w1bd, w2k, w2vt, caux):
    b, t, _ = src.shape
    ncp = t // CMP_STRIDE
    gd = NSA_KV_HEADS * HEAD_DIM
    full = lambda a: pl.BlockSpec(a.shape, lambda bi, k: (0,) * a.ndim)
    per_kind = lambda a: pl.BlockSpec((1,) + a.shape[1:], lambda bi, k: (k,) + (0,) * (a.ndim - 1))
    return pl.pallas_call(
        _compress_kernel,
        grid=(b, 2),
        in_specs=[pl.BlockSpec((1, t, gd), lambda bi, k: (bi, 0, k)),
                  per_kind(pe), per_kind(w1), per_kind(w1bd), full(w2k), full(w2vt), full(caux)],
        out_specs=[pl.BlockSpec((1, NSA_KV_HEADS, ncp, LANES), lambda bi, k: (bi, 0, 0, 0)),
                   pl.BlockSpec((1, gd, ncp), lambda bi, k: (bi, 0, 0))],
        out_shape=[jax.ShapeDtypeStruct((b, NSA_KV_HEADS, ncp, LANES), BF16),
                   jax.ShapeDtypeStruct((b, gd, ncp), BF16)],
        compiler_params=_params("parallel", "arbitrary"),
        name="compress",
    )(src, pe, w1, w1bd, w2k, w2vt, caux)


SOFTMAX_FLOOR = -1e29
TAKEN = -3e38
ATTN_BATCH = 4
BLOCKS_PER_CHUNK = KEY_CHUNK // SLC_BLOCK
AUX_COLS = 16
AUX_SLOPE = BLOCKS_PER_CHUNK
ONES_ROWS = 16
CMP_ROWS_STEP = 128
CMP_TILES = 4
CMP_AUX_SPLIT = 128


def _slope(g, r):
    if isinstance(g, int):
        return jnp.float32(ALIBI_SLOPES[g * NSA_GROUP + r])
    s = jnp.float32(ALIBI_SLOPES[r])
    for gi in range(1, NSA_KV_HEADS):
        s = jnp.where(g == gi, jnp.float32(ALIBI_SLOPES[gi * NSA_GROUP + r]), s)
    return s


def _nsa_cmp_kernel(qt_ref, kc_ref, vct_ref, poolt_ref, pair_ref, earlier_ref, oct_ref, selt_ref, list_ref,
                    *, n_sel):
    first_tile = pl.program_id(1) * CMP_TILES
    qt = selt_ref.shape[4]
    ncp = kc_ref.shape[2]
    n_slc = poolt_ref.shape[0]
    gq = qt_ref.shape[4]
    units = [(g, j) for j in range(CMP_TILES) for g in range(NSA_KV_HEADS)]
    times = [(first_tile + j) * qt + _iota((1, qt), 1) for j in range(CMP_TILES)]
    aux_row = _iota((AUX_COLS, gq), 0)
    col_head = _iota((1, gq), 1) // qt
    pad_rows = jnp.zeros((kc_ref.shape[3] - qt_ref.shape[3] - AUX_COLS, gq), BF16)

    def weights(g, j):
        slope_cols = jnp.zeros((1, gq), F32)
        for r in range(NSA_GROUP):
            slope_cols = jnp.where(col_head == r, _slope(g, r), slope_cols)
        aux = jnp.where(aux_row == 0, slope_cols * (CMP_STRIDE * CMP_AUX_SPLIT),
                        jnp.where(aux_row == 1, slope_cols * CMP_STRIDE, 0.0))
        return jnp.concatenate([qt_ref[0, g, j], aux.astype(BF16), pad_rows], axis=0)

    def importance(g, j, nr, nb):
        t = times[j]
        s = _dot(kc_ref[0, g, :nr, :], weights(g, j))
        edge = min(nr, 2 * CMP_ROWS_STEP)
        cmp_end = ((nr - edge) + _iota((edge, 1), 0)) * CMP_STRIDE + (CMP_LEN - 1)
        visible = cmp_end <= t
        vct = vct_ref[0, g * HEAD_DIM:(g + 1) * HEAD_DIM, :nr]
        psum = jnp.zeros((nr, qt), F32)
        for r in range(NSA_GROUP):
            cols = slice(r * qt, (r + 1) * qt)
            sr = s[:, cols]
            tail = jnp.where(visible, sr[nr - edge:], NEG_INF)
            sr = tail if edge == nr else jnp.concatenate([sr[:nr - edge], tail], axis=0)
            m = jnp.maximum(jnp.max(sr, axis=0, keepdims=True), SOFTMAX_FLOOR)
            e = jnp.exp(sr - m)
            l = jnp.sum(e, axis=0, keepdims=True)
            inv = jnp.where(l > 0.0, 1.0 / l, 0.0)
            oct_ref[0, g, j, :, cols] = (_dot(vct, e.astype(BF16)) * inv).astype(BF16)
            psum = psum + e * inv
        return _dot(poolt_ref[:nb, :nr], psum.astype(BF16))

    def visible_prefix(nr):
        nb = min(n_slc, nr * CMP_STRIDE // SLC_BLOCK)
        imp = jnp.concatenate([importance(g, j, nr, nb) for g, j in units], axis=1)
        blk = _iota((nb, 1), 0)
        cur = jnp.concatenate([times[j] // SLC_BLOCK for _, j in units], axis=1)
        forced = (blk == 0) | (blk == cur) | (blk == cur - 1)
        score = jnp.where(forced, TAKEN, jnp.where(blk <= cur, imp, NEG_INF))
        n_forced = 1 + jnp.where(cur >= 1, 1, 0) + jnp.where(cur >= 2, 1, 0)
        blk_f = blk.astype(F32)

        def take_one(score, active):
            m = jnp.max(score, axis=0, keepdims=True)
            first = jnp.min(jnp.where(score == m, blk_f, F32(1e9)), axis=0, keepdims=True)
            hit = (blk_f == first) if active is None else ((blk_f == first) & active)
            return jnp.where(hit, TAKEN, score)

        common_rounds = max(n_sel - 3, 0)
        for _ in range(common_rounds):
            score = take_one(score, None)

        def early_rounds(score):
            for k in range(common_rounds, n_sel - 1):
                score = take_one(score, n_sel - n_forced > k)
            return score

        score = lax.cond(first_tile * qt < 2 * SLC_BLOCK, early_rounds, lambda sc: sc, score)
        for u, (g, j) in enumerate(units):
            sel_g = score[:, u * qt:(u + 1) * qt] == TAKEN
            selt_ref[0, g, j, :nb, :] = jnp.where(sel_g, 0.0, NEG_INF)
            if nb < n_slc:
                selt_ref[0, g, j, nb:, :] = jnp.full((n_slc - nb, qt), NEG_INF, F32)
            count = _dot_nt(jnp.ones((8, qt), BF16), jnp.where(sel_g, 1.0, 0.0).astype(BF16))
            used = jnp.where(count > 0.0, 1.0, 0.0).astype(BF16)
            chunk = _iota((8, LANES), 1)
            flagged = jnp.where((_dot(used, pair_ref[:nb, :]) > 0.0) & (chunk < first_tile + j), 1.0, 0.0).astype(BF16)
            place = _dot(flagged, earlier_ref[...])
            lands = (_iota((LANES, LANES), 0).astype(F32) == place[0:1, :]) & (flagged[0:1, :] > 0)
            listed = _dot_nt(chunk.astype(BF16), jnp.where(lands, 1.0, 0.0).astype(BF16))
            n_listed = _dot(flagged, jnp.ones((LANES, LANES), BF16))
            list_ref[0, g, j] = jnp.where(_iota((8, LANES), 0) < 4, listed, n_listed).astype(jnp.int32)

    step = min(CMP_ROWS_STEP, ncp)
    rows_needed = jnp.minimum(((first_tile + CMP_TILES) * qt - CMP_LEN) // CMP_STRIDE + 1, ncp)
    n_steps = (rows_needed + step - 1) // step
    for k in range(ncp // step):
        pl.when(n_steps == k + 1)(functools.partial(visible_prefix, (k + 1) * step))


def _nsa_cmp(q_t, kc, vct, poolt_m, pair_m, earlier_m, n_sel):
    b, ng, n_tiles, qrows, qcols = q_t.shape
    ncp = kc.shape[2]
    n_slc = poolt_m.shape[0]
    const = lambda shape: pl.BlockSpec(shape, lambda bi, i: (0,) * len(shape))
    tile5 = lambda rows, cols: pl.BlockSpec((1, ng, CMP_TILES, rows, cols), lambda bi, i: (bi, 0, i, 0, 0))
    return pl.pallas_call(
        functools.partial(_nsa_cmp_kernel, n_sel=n_sel),
        grid=(b, n_tiles // CMP_TILES),
        in_specs=[
            tile5(qrows, qcols),
            pl.BlockSpec((1, ng, ncp, kc.shape[3]), lambda bi, i: (bi, 0, 0, 0)),
            pl.BlockSpec((1, ng * HEAD_DIM, ncp), lambda bi, i: (bi, 0, 0)),
            const(poolt_m.shape), const(pair_m.shape), const(earlier_m.shape),
        ],
        out_specs=[tile5(HEAD_DIM, qcols), tile5(n_slc, Q_TILE), tile5(8, LANES)],
        out_shape=[
            jax.ShapeDtypeStruct((b, ng, n_tiles, HEAD_DIM, qcols), BF16),
            jax.ShapeDtypeStruct((b, ng, n_tiles, n_slc, Q_TILE), F32),
            jax.ShapeDtypeStruct((b, ng, n_tiles, 8, LANES), jnp.int32),
        ],
        compiler_params=_params("parallel", "parallel"),
        name="nsa_compressed",
    )(q_t, kc, vct, poolt_m, pair_m, earlier_m)


def _nsa_slc_kernel(lists_ref, counts_ref, qt_ref, ks_ref, vst_ref, kw_ref, vwt_ref, selt_ref, oct_ref, gt_ref,
                    out_ref, m_ref, l_ref, acc_ref, ow_ref, sa_ref, sb_ref, sw_ref):
    bi = pl.program_id(0)
    g = pl.program_id(1)
    i = pl.program_id(2)
    n_tiles = pl.num_programs(2)
    qt = out_ref.shape[1]
    q_rows = qt_ref[0, 0, 0]
    gq = q_rows.shape[1]
    start = i * qt
    lane_f = _iota((1, qt), 1).astype(F32)
    key_in_chunk = _iota((KEY_CHUNK, qt), 0)
    query_in_tile = _iota((KEY_CHUNK, qt), 1)

    aux_row = _iota((AUX_COLS, gq), 0)
    col_head = _iota((1, gq), 1) // qt
    slope_cols = jnp.zeros((1, gq), F32)
    for r in range(NSA_GROUP):
        slope_cols = jnp.where(col_head == r, _slope(g, r), slope_cols)
    aux_base = jnp.where(aux_row == AUX_SLOPE, slope_cols, 0.0)
    pad_rows = jnp.zeros((ks_ref.shape[3] - q_rows.shape[0] - AUX_COLS, gq), BF16)

    def scores_of(slots):
        return [_dot(k, jnp.concatenate([q_rows, aux.astype(BF16), pad_rows], axis=0))
                for k, aux, _, _, _ in slots]

    def values_of(slots):
        values = jnp.concatenate([v for _, _, v, _, _ in slots], axis=1)
        return jnp.concatenate([values, jnp.ones((ONES_ROWS, values.shape[1]), BF16)], axis=0)

    def softmax_step(slots, scores, v_cat, r, m_old):
        cols = slice(r * qt, (r + 1) * qt)
        srs, tops = [], []
        for j, (_, _, _, shift, mask) in enumerate(slots):
            sr = scores[j, :, cols] if hasattr(scores, "at") else scores[j][:, cols]
            sr = sr if mask is None else jnp.where(mask, sr, NEG_INF)
            srs.append(sr)
            tops.append(jnp.max(sr, axis=0, keepdims=True) + shift[r])
        m_new = functools.reduce(jnp.maximum, tops, m_old)
        ps = [jnp.exp((sr - (m_new - slot[3][r])).astype(BF16)) for slot, sr in zip(slots, srs)]
        weighted = _dot(v_cat, jnp.concatenate(ps, axis=0))
        return m_new, weighted[HEAD_DIM:HEAD_DIM + 1], weighted[:HEAD_DIM]

    def shifts(dist0, ok):
        rows = [-_slope(g, r) * (dist0 + lane_f) for r in range(NSA_GROUP)]
        return rows if ok is None else [jnp.where(ok, row, NEG_INF) for row in rows]

    tile_id = (bi * NSA_KV_HEADS + g) * n_tiles + i
    n_listed = counts_ref[tile_id]

    def selected_slot(c, ok, mask):
        at = pl.multiple_of(c * KEY_CHUNK, KEY_CHUNK)
        bias = selt_ref[0, 0, 0, pl.ds(c * BLOCKS_PER_CHUNK, BLOCKS_PER_CHUNK), :]
        aux = aux_base
        for blk in range(BLOCKS_PER_CHUNK):
            aux = jnp.where(aux_row == blk, jnp.concatenate([bias[blk:blk + 1]] * NSA_GROUP, axis=1), aux)
        return (ks_ref[0, 0, pl.ds(at, KEY_CHUNK), :], aux, vst_ref[0, 0, c],
                shifts((start - c * KEY_CHUNK).astype(F32), ok), mask)

    def listed_slot(idx):
        ok = idx < n_listed
        c = jnp.where(ok, lists_ref[tile_id * LANES + jnp.minimum(idx, jnp.maximum(n_listed - 1, 0))], 0)
        return selected_slot(c, ok, None)

    floor = jnp.full((1, qt), SOFTMAX_FLOOR, F32)

    n_back = WINDOW // KEY_CHUNK
    slots = []
    for j in range(n_back + 1):
        cs = start - WINDOW + j * KEY_CHUNK
        chunk = jnp.maximum(cs, 0) // KEY_CHUNK
        at = pl.multiple_of(chunk * KEY_CHUNK, KEY_CHUNK)
        mask = (query_in_tile < key_in_chunk) if j == 0 else (
            (key_in_chunk <= query_in_tile) if j == n_back else None)
        slots.append((kw_ref[0, 0, pl.ds(at, KEY_CHUNK), :], aux_base, vwt_ref[0, 0, chunk],
                      shifts(F32(WINDOW - j * KEY_CHUNK), cs >= 0), mask))
    window_slots = slots
    first_slots = [selected_slot(i, None, key_in_chunk <= query_in_tile)] + [
        listed_slot(j) for j in range(ATTN_BATCH - 1)]

    def listed_batch(n):
        return [listed_slot(ATTN_BATCH - 1 + n * ATTN_BATCH + j) for j in range(ATTN_BATCH)]

    def issue(slots, dst_ref):
        for j, s in enumerate(scores_of(slots)):
            dst_ref[j] = s

    issue(window_slots, sw_ref)
    issue(first_slots, sb_ref)
    v_cat = values_of(window_slots)
    for r in range(NSA_GROUP):
        _, total, weighted = softmax_step(window_slots, sw_ref, v_cat, r, floor)
        ow_ref[:, r * qt:(r + 1) * qt] = weighted * jnp.where(total > 0.0, 1.0 / total, 0.0)

    def consume(slots, src_ref):
        v_cat = values_of(slots)
        for r in range(NSA_GROUP):
            cols = slice(r * qt, (r + 1) * qt)
            m_old = m_ref[r]
            m_new, total, weighted = softmax_step(slots, src_ref, v_cat, r, m_old)
            alpha = jnp.exp(m_old - m_new)
            l_ref[r] = alpha * l_ref[r] + total
            acc_ref[:, cols] = alpha * acc_ref[:, cols] + weighted
            m_ref[r] = m_new

    issue(listed_batch(0), sa_ref)
    v_cat = values_of(first_slots)
    for r in range(NSA_GROUP):
        m_ref[r], l_ref[r], acc_ref[:, r * qt:(r + 1) * qt] = softmax_step(first_slots, sb_ref, v_cat, r, floor)

    def batch_pair(it, carry):
        first, second, third = (listed_batch(2 * it + n) for n in range(3))
        issue(second, sb_ref)
        consume(first, sa_ref)
        issue(third, sa_ref)
        consume(second, sb_ref)
        return carry

    n_rest = jnp.maximum(n_listed - (ATTN_BATCH - 1), 0)
    n_batches = (n_rest + ATTN_BATCH - 1) // ATTN_BATCH
    lax.fori_loop(0, n_batches // 2, batch_pair, 0)

    @pl.when(n_batches % 2 == 1)
    def _():
        consume(listed_batch(n_batches - 1), sa_ref)

    def finalize(r):
        l = l_ref[r]
        return acc_ref[:, r * qt:(r + 1) * qt] * jnp.where(l > 0.0, 1.0 / l, 0.0)

    outs = []
    for r in range(NSA_GROUP):
        cols = slice(r * qt, (r + 1) * qt)
        col = (g * NSA_GROUP + r) * NSA_BRANCHES
        gate = lambda br: jax.nn.sigmoid(gt_ref[0, pl.ds(col + br, 1), :])
        outs.append(gate(0) * oct_ref[0, 0, 0, :, cols].astype(F32) + gate(1) * finalize(r)
                    + gate(2) * ow_ref[:, cols])
    out_ref[0] = jnp.concatenate(outs, axis=0).T.astype(out_ref.dtype)


def _nsa_slc(lists, counts, q_t, ks, vst, kw, vwt, selt, oct, gates_t):
    b, _, t, kw_cols = ks.shape
    _, _, n_tiles, qrows, qcols = q_t.shape
    n_slc = selt.shape[3]
    n_chunks = vst.shape[2]
    gw = NSA_GROUP * HEAD_DIM
    once = dict(pipeline_mode=pl.Buffered(1))
    k_spec = pl.BlockSpec((1, 1, t, kw_cols), lambda bi, g, i, ls, ns: (bi, g, 0, 0), **once)
    vt_spec = pl.BlockSpec((1, 1, n_chunks, HEAD_DIM, KEY_CHUNK), lambda bi, g, i, ls, ns: (bi, g, 0, 0, 0), **once)
    tile5 = lambda rows, cols: pl.BlockSpec((1, 1, 1, rows, cols), lambda bi, g, i, ls, ns: (bi, g, i, 0, 0))
    grid_spec = pltpu.PrefetchScalarGridSpec(
        num_scalar_prefetch=2,
        grid=(b, NSA_KV_HEADS, n_tiles),
        in_specs=[
            tile5(qrows, qcols),
            k_spec, vt_spec, k_spec, vt_spec,
            tile5(n_slc, Q_TILE),
            tile5(HEAD_DIM, qcols),
            pl.BlockSpec((1, GATE_PAD, Q_TILE), lambda bi, g, i, ls, ns: (bi, 0, i)),
        ],
        out_specs=pl.BlockSpec((1, Q_TILE, gw), lambda bi, g, i, ls, ns: (bi, i, g)),
        scratch_shapes=[pltpu.VMEM((NSA_GROUP, 1, Q_TILE), F32), pltpu.VMEM((NSA_GROUP, 1, Q_TILE), F32),
                        pltpu.VMEM((HEAD_DIM, qcols), F32), pltpu.VMEM((HEAD_DIM, qcols), F32),
                        pltpu.VMEM((ATTN_BATCH, KEY_CHUNK, qcols), F32),
                        pltpu.VMEM((ATTN_BATCH, KEY_CHUNK, qcols), F32),
                        pltpu.VMEM((WINDOW // KEY_CHUNK + 1, KEY_CHUNK, qcols), F32)],
    )
    return pl.pallas_call(
        _nsa_slc_kernel,
        grid_spec=grid_spec,
        out_shape=jax.ShapeDtypeStruct((b, t, NSA_HEADS * HEAD_DIM), BF16),
        compiler_params=_params("parallel", "parallel", "parallel"),
        name="nsa_selected_window",
    )(lists, counts, q_t, ks, vst, kw, vwt, selt, oct, gates_t)


POOL_HALO = 16
CONV_HALO = 8


def _merge_kernel(h_ref, g_ref, nsa_ref, pool_ref, pool_halo_ref, conv_ref, conv_halo_ref,
                  wmg_ref, wnsa_ref, pool_bd_ref, pool_scale_ref, wpool_ref, convw_ref, wconv_ref, wo_ref,
                  g2_ref, wr_ref, br_ref, before_ref,
                  out_ref, xn_ref, comb_ref, count_ref, pool_ext, conv_ext, *, seq_len):
    i = pl.program_id(0)
    tm, d = h_ref.shape
    cw = pool_ref.shape[1]
    pos0 = (i * tm) % seq_len
    keep_halo = jnp.where(pos0 == 0, 0.0, 1.0)
    pos = pos0 + _iota((tm, 1), 0)

    u = pool_ref[...]
    pool_ext[0:POOL_HALO, :] = pool_halo_ref[...] * keep_halo
    pool_ext[POOL_HALO:, :] = u
    lane_group = _iota((1, cw), 1) // (cw // len(POOL_WINDOWS))
    total = u
    mean = jnp.zeros_like(u)
    done = 1
    for gi, win in enumerate(POOL_WINDOWS):
        for k in range(done, win):
            total = total + pool_ext[POOL_HALO - k:POOL_HALO - k + tm, :]
        done = win
        cnt = jnp.minimum(pos + 1, win).astype(F32)
        mean = jnp.where(lane_group == gi, total / cnt, mean)
    pooled = (mean - u).astype(BF16)
    mixed = _dot(pooled, pool_bd_ref[...]) * pool_scale_ref[...]
    y_pool = _dot(mixed.astype(BF16), wpool_ref[...])

    ch = conv_ref[:, 0:cw]
    cb = conv_ref[:, cw:2 * cw]
    cc = conv_ref[:, 2 * cw:3 * cw]
    conv_ext[0:CONV_HALO, :] = conv_halo_ref[:, 0:cw] * conv_halo_ref[:, 2 * cw:3 * cw] * keep_halo
    conv_ext[CONV_HALO:, :] = cc * ch
    y = jnp.zeros((tm, cw), F32)
    for k in range(CONV_K):
        off = CONV_HALO - (CONV_K - 1) + k
        y = y + convw_ref[k:k + 1, :] * conv_ext[off:off + tm, :]
    y_conv = _dot((cb * y).astype(BF16), wconv_ref[...])

    y_nsa = _dot(nsa_ref[...], wnsa_ref[...])

    h = h_ref[...]
    xn = _rms_norm(h, g_ref[...]).astype(BF16)
    merged = jnp.zeros((tm, d), F32)
    for br, y_br in enumerate((y_nsa, y_pool, y_conv)):
        mg = jax.nn.sigmoid(_dot(xn, wmg_ref[:, br * d:(br + 1) * d]))
        merged = merged + mg * y_br
    h_new = h + _dot(merged.astype(BF16), wo_ref[...])
    out_ref[...] = h_new
    _route_tile(h_new, g2_ref, wr_ref, br_ref, before_ref, xn_ref, comb_ref, count_ref)


def _merge(h, g, nsa, pool_u, conv, wmg, wnsa, pool_bd, pool_scale, wpool, convw, wconv, wo,
           g2, wr, br, before, seq_len):
    n, d = h.shape
    tm = TOKEN_TILE
    cw = pool_u.shape[1]
    row = lambda width: pl.BlockSpec((tm, width), lambda i: (i, 0))
    full = lambda a: pl.BlockSpec(a.shape, lambda i: (0,) * a.ndim)
    halo = lambda rows, width: pl.BlockSpec(
        (rows, width), lambda i: (jnp.maximum(i * (tm // rows) - 1, 0), 0))
    return pl.pallas_call(
        functools.partial(_merge_kernel, seq_len=seq_len),
        grid=(n // tm,),
        in_specs=[row(d), full(g), row(nsa.shape[1]), row(cw), halo(POOL_HALO, cw),
                  row(conv.shape[1]), halo(CONV_HALO, conv.shape[1]),
                  full(wmg), full(wnsa), full(pool_bd), full(pool_scale), full(wpool), full(convw),
                  full(wconv), full(wo), full(g2), full(wr), full(br), full(before)],
        out_specs=[row(d), row(d + ROUTER_PAD), row(ROUTER_PAD), pl.BlockSpec((8, ROUTER_PAD), lambda i: (0, 0))],
        out_shape=[jax.ShapeDtypeStruct((n, d), F32), jax.ShapeDtypeStruct((n, d + ROUTER_PAD), BF16),
                   jax.ShapeDtypeStruct((n, ROUTER_PAD), F32), jax.ShapeDtypeStruct((8, ROUTER_PAD), F32)],
        scratch_shapes=[pltpu.VMEM((tm + POOL_HALO, cw), F32), pltpu.VMEM((tm + CONV_HALO, cw), F32)],
        compiler_params=_params("arbitrary"),
        name="merge",
    )(h, g, nsa, pool_u, pool_u, conv, conv, wmg, wnsa, pool_bd, pool_scale, wpool, convw, wconv, wo,
      g2, wr, br, before)


def _route(logits):
    lane = _iota(logits.shape, 1)
    lane_f = lane.astype(F32)
    big = F32(1e9)
    is_group = lane < N_EXPERT_GROUPS
    gl = jnp.where(is_group, logits, NEG_INF)
    g_max = jnp.max(gl, axis=1, keepdims=True)
    g_sel = jnp.min(jnp.where(gl == g_max, lane_f, big), axis=1, keepdims=True)
    g_prob = 1.0 / jnp.sum(jnp.where(is_group, jnp.exp(gl - g_max), 0.0), axis=1, keepdims=True)
    lo = N_EXPERT_GROUPS + EXPERTS_PER_GROUP * g_sel
    in_group = (lane_f >= lo) & (lane_f < lo + EXPERTS_PER_GROUP)
    el = jnp.where(in_group, logits, NEG_INF)
    v1 = jnp.max(el, axis=1, keepdims=True)
    i1 = jnp.min(jnp.where((el == v1) & in_group, lane_f, big), axis=1, keepdims=True)
    el2 = jnp.where(lane_f == i1, NEG_INF, el)
    rest = in_group & (lane_f != i1)
    v2 = jnp.max(el2, axis=1, keepdims=True)
    i2 = jnp.min(jnp.where((el2 == v2) & rest, lane_f, big), axis=1, keepdims=True)
    e2 = jnp.exp(v2 - v1)
    w1 = g_prob / (1.0 + e2)
    w2 = g_prob * e2 / (1.0 + e2)
    return jnp.where(lane_f == i1, w1, 0.0) + jnp.where(lane_f == i2, w2, 0.0), g_sel


GROUP_LANE = N_EXPERT_GROUPS + N_EXPERTS


RANK_LANE = GROUP_LANE + 1


def _route_tile(h, g_ref, wr_ref, br_ref, before_ref, xn_ref, comb_ref, count_ref):
    @pl.when(pl.program_id(0) == 0)
    def _():
        count_ref[...] = jnp.zeros(count_ref.shape, F32)

    xn = _rms_norm(h, g_ref[...])
    xn_hi = xn.astype(BF16)
    xn_lo = (xn - xn_hi.astype(F32)).astype(BF16)
    logits = (_dot(xn_hi, wr_ref[0]) + (_dot(xn_hi, wr_ref[1]) + _dot(xn_lo, wr_ref[0]))) + br_ref[...]
    comb, g_sel = _route(logits)
    lane = _iota(comb.shape, 1)
    chose = jnp.where(lane.astype(F32) == g_sel, 1.0, 0.0)
    earlier = _dot(before_ref[...], chose.astype(BF16)) + count_ref[0:1, :]
    rank = jnp.sum(chose * earlier, axis=1, keepdims=True)
    count_ref[0:1, :] = count_ref[0:1, :] + jnp.sum(chose, axis=0, keepdims=True)
    comb_ref[...] = jnp.where(lane == GROUP_LANE, g_sel, jnp.where(lane == RANK_LANE, rank, comb))
    first = N_EXPERT_GROUPS + EXPERTS_PER_GROUP * g_sel
    local = jnp.zeros(comb.shape, F32)
    for e in range(EXPERTS_PER_GROUP):
        c_e = jnp.sum(jnp.where(lane.astype(F32) == first + e, comb, 0.0), axis=1, keepdims=True)
        local = jnp.where((lane == e) | (lane == EXPERTS_PER_GROUP + e), c_e, local)
    local_hi = local.astype(BF16)
    d = xn_hi.shape[1]
    xn_ref[:, :d] = xn_hi
    xn_ref[:, d:] = jnp.where(lane < EXPERTS_PER_GROUP, local_hi, (local - local_hi.astype(F32)).astype(BF16))


def _experts_kernel(tile_group_ref, n_active_ref, x_ref, w1_ref, w3_ref, w2_ref, out_ref, acc_ref):
    i = pl.program_id(0)

    @pl.when(i < n_active_ref[0])
    def _():
        d = out_ref.shape[1]
        x = x_ref[:, :d]
        comb = x_ref[:, d:].astype(F32)
        lane = _iota(comb.shape, 1)
        for e in range(EXPERTS_PER_GROUP):
            c_e = jnp.sum(jnp.where((lane == e) | (lane == EXPERTS_PER_GROUP + e), comb, 0.0),
                          axis=1, keepdims=True)
            a = (jax.nn.silu(_dot(x, w1_ref[0, 0, e].astype(BF16)))
                 * _dot(x, w3_ref[0, 0, e].astype(BF16))) * c_e
            y = _dot(a.astype(BF16), w2_ref[0, 0, e].astype(BF16))
            if e == 0:
                acc_ref[...] = y
            else:
                acc_ref[...] += y
        out_ref[...] = acc_ref[...].astype(out_ref.dtype)

    @pl.when(i >= n_active_ref[0])
    def _():
        out_ref[...] = jnp.zeros(out_ref.shape, out_ref.dtype)


def _experts(tile_group, n_active, x_sorted, w1, w3, w2, layer):
    ns = x_sorted.shape[0]
    d = w1.shape[3]
    tm = MOE_SORT_TILE
    group_w = lambda w: pl.BlockSpec((1, 1) + w.shape[2:], lambda i, tg, na: (layer, tg[i], 0, 0, 0),
                                     pipeline_mode=pl.Buffered(1))
    grid_spec = pltpu.PrefetchScalarGridSpec(
        num_scalar_prefetch=2,
        grid=(ns // tm,),
        in_specs=[
            pl.BlockSpec((tm, x_sorted.shape[1]), lambda i, tg, na: (i, 0)),
            group_w(w1), group_w(w3), group_w(w2),
        ],
        out_specs=pl.BlockSpec((tm, d), lambda i, tg, na: (i, 0)),
        scratch_shapes=[pltpu.VMEM((tm, d), F32)],
    )
    return pl.pallas_call(
        _experts_kernel,
        grid_spec=grid_spec,
        out_shape=jax.ShapeDtypeStruct((ns, d), BF16),
        compiler_params=_params("arbitrary"),
        name="experts",
    )(tile_group, n_active, x_sorted, w1, w3, w2)


def _residual_kernel(h_ref, y_ref, gf_ref, out_ref, *, final_norm):
    out = h_ref[...] + y_ref[...].astype(F32)
    out_ref[...] = _rms_norm(out, gf_ref[...]) if final_norm else out


def _residual(h, y, gf, final_norm):
    n, d = h.shape
    tm = MOE_TILE
    row = pl.BlockSpec((tm, d), lambda i: (i, 0))
    return pl.pallas_call(
        functools.partial(_residual_kernel, final_norm=final_norm),
        grid=(n // tm,),
        in_specs=[row, row, pl.BlockSpec(gf.shape, lambda i: (0, 0))],
        out_specs=row,
        out_shape=jax.ShapeDtypeStruct((n, d), F32),
        compiler_params=_params("parallel"),
        name="residual",
    )(h, y, gf)


def _group_sort_plan(group_id, rank, counts, tile):
    n = group_id.shape[0]
    n_slots = n + N_EXPERT_GROUPS * tile
    padded = (counts + tile - 1) // tile * tile
    ends = jnp.cumsum(padded)
    slot = (ends - padded)[group_id] + rank
    source = jnp.zeros((n_slots,), jnp.int32).at[slot].set(jnp.arange(n, dtype=jnp.int32))
    tile_start = jnp.arange(n_slots // tile, dtype=jnp.int32) * tile
    tile_group = jnp.minimum(jnp.searchsorted(ends, tile_start, side="right"), N_EXPERT_GROUPS - 1)
    return slot, source, tile_group.astype(jnp.int32), (ends[-1:] // tile).astype(jnp.int32)


def _moe(h, xn, comb, counts, w1, w3, w2, layer, gf, final_norm):
    as_int = lambda a: a.astype(jnp.int32)
    slot, source, tile_group, n_active = _group_sort_plan(
        as_int(comb[:, GROUP_LANE]), as_int(comb[:, RANK_LANE]), as_int(counts[0, :N_EXPERT_GROUPS]),
        MOE_SORT_TILE)
    grouped = lambda w: w.reshape((w.shape[0], N_EXPERT_GROUPS, EXPERTS_PER_GROUP) + w.shape[2:])
    rows = lambda a, idx: jnp.take(a, idx, axis=0, mode="clip")
    y_sorted = _experts(tile_group, n_active, rows(xn, source), grouped(w1), grouped(w3), grouped(w2), layer)
    return _residual(h, rows(y_sorted, slot), gf, final_norm)


def _selection_constants(seq_len):
    ncp = seq_len // CMP_STRIDE
    n_slc = seq_len // SLC_BLOCK
    ratio = SLC_BLOCK // CMP_STRIDE
    lead = CMP_LEN // CMP_STRIDE - 1
    c = np.arange(ncp)[:, None]
    j = np.arange(n_slc)[None, :]
    pool_m = ((c >= ratio * j - lead) & (c < ratio * j + ratio)).astype(np.float32)
    blocks_per_chunk = KEY_CHUNK // SLC_BLOCK
    n_chunks = seq_len // KEY_CHUNK
    pair_m = np.zeros((n_slc, LANES * ((n_chunks + LANES - 1) // LANES)), np.float32)
    pair_m[np.arange(n_slc), np.arange(n_slc) // blocks_per_chunk] = 1.0
    earlier_m = np.triu(np.ones((LANES, LANES), np.float32), 1)
    key_aux = np.zeros((TOKEN_TILE, LANES), np.float32)
    in_chunk = np.arange(TOKEN_TILE) % KEY_CHUNK
    key_aux[np.arange(TOKEN_TILE), HEAD_DIM + in_chunk // SLC_BLOCK] = 1.0
    key_aux[:, HEAD_DIM + AUX_SLOPE] = in_chunk
    cmp_aux = np.zeros((ncp, LANES), np.float32)
    cmp_aux[:, HEAD_DIM] = np.arange(ncp) // CMP_AUX_SPLIT
    cmp_aux[:, HEAD_DIM + 1] = np.arange(ncp) % CMP_AUX_SPLIT
    as_bf16 = lambda a: jnp.asarray(a, BF16)
    return as_bf16(pool_m.T), as_bf16(pair_m), as_bf16(earlier_m), jnp.asarray(key_aux), jnp.asarray(cmp_aux)


def kernel(x, norm1_g, w_in, cmp_pe, cmp_w1, cmp_w2, w_nsa_proj, pool_w, pool_scale, w_pool_proj, conv_w,
           w_conv_proj, w_o, norm2_g, router_group_w, router_group_b, router_expert_w, router_expert_b,
           expert_w1, expert_w3, expert_w2, final_norm_g):
    b, t, d = x.shape
    n = b * t
    depth = w_in.shape[0]
    dq = NSA_HEADS * HEAD_DIM
    dkv = 6 * NSA_KV_HEADS * HEAD_DIM
    dgate = NSA_HEADS * NSA_BRANCHES
    cw = d // 4
    assert t % TOKEN_TILE == 0 and n % MOE_TILE == 0 and t % (Q_TILE * CMP_TILES) == 0
    n_slc = t // SLC_BLOCK
    n_sel = min(SLC_TOPN, n_slc)
    n_chunks16 = t // CMP_STRIDE
    kvw = NSA_KV_HEADS * HEAD_DIM
    poolt_m, pair_m, earlier_m, key_aux, cmp_aux = _selection_constants(t)
    assert Q_TILE == KEY_CHUNK and AUX_SLOPE < AUX_COLS and KEY_CHUNK <= 256
    assert n_chunks16 <= 256 * CMP_AUX_SPLIT and n_chunks16 % min(CMP_ROWS_STEP, n_chunks16) == 0
    assert pair_m.shape[1] == LANES

    before = jnp.asarray(np.tril(np.ones((TOKEN_TILE, TOKEN_TILE), np.float32), -1), BF16)
    h = x.reshape(n, d)
    for l in range(depth):
        wl = w_in[l]
        o_gate = dq + dkv
        o_pool = o_gate + dgate
        o_merge = o_pool + cw + 3 * cw
        kv_cols = lambda kind: wl[:, dq + kind * kvw:dq + (kind + 1) * kvw]
        wq_t = (wl[:, :dq] * (HEAD_DIM ** -0.5)).T.astype(BF16)
        wv_t = jnp.concatenate([kv_cols(3), kv_cols(5)], axis=1).T.astype(BF16)
        wg_t = jnp.pad(wl[:, o_gate:o_pool], ((0, 0), (0, GATE_PAD - dgate))).T.astype(BF16)
        no_aux = jnp.zeros((d, LANES - HEAD_DIM), F32)
        wk = jnp.concatenate([piece for kind in (2, 4) for gi in range(NSA_KV_HEADS)
                              for piece in (kv_cols(kind)[:, gi * HEAD_DIM:(gi + 1) * HEAD_DIM], no_aux)],
                             axis=1).astype(BF16)
        wn = jnp.concatenate([kv_cols(0), kv_cols(1), wl[:, o_pool:o_merge]], axis=1).astype(BF16)
        wmg = wl[:, o_merge:].astype(BF16)
        pool_bd = jax.scipy.linalg.block_diag(*[pool_w[l, gi] for gi in range(pool_w.shape[1])]).astype(BF16)
        convw = jnp.pad(conv_w[l], ((0, 8 - CONV_K), (0, 0)))
        wr = jnp.pad(jnp.concatenate([router_group_w[l], router_expert_w[l]], axis=1),
                     ((0, 0), (0, ROUTER_PAD - N_EXPERT_GROUPS - N_EXPERTS)))
        wr_hi = wr.astype(BF16)
        wr = jnp.stack([wr_hi, (wr - wr_hi.astype(F32)).astype(BF16)])
        br = jnp.pad(jnp.concatenate([router_group_b[l], router_expert_b[l]]),
                     (0, ROUTER_PAD - N_EXPERT_GROUPS - N_EXPERTS))[None, :]
        pe = jnp.broadcast_to(cmp_pe[l].reshape(2, 1, CMP_LEN * HEAD_DIM), (2, 8, CMP_LEN * HEAD_DIM)).astype(BF16)
        halves = CMP_LEN // CMP_STRIDE
        w1_bd = jnp.einsum("khldc,gq->khlgdqc",
                           cmp_w1[l].reshape(2, halves, CMP_STRIDE, HEAD_DIM, CMP_HIDDEN),
                           jnp.eye(NSA_KV_HEADS, dtype=F32))
        w1_bd = w1_bd.reshape(2, halves, CMP_STRIDE * kvw, NSA_KV_HEADS * CMP_HIDDEN).astype(BF16)
        w2_k = jnp.pad(cmp_w2[l, 0], ((0, 0), (0, LANES - HEAD_DIM))).astype(BF16)
        w2_vt = cmp_w2[l, 1].T.astype(BF16)

        q_t, vst, vwt, gates_t, ks, kw, cmp_src, pool_u, conv = _inproj(
            h, norm1_g[l][None, :], wq_t, wv_t, wg_t, wk, wn, key_aux, b)
        kc_aux, vc_t = _compress(cmp_src, pe, cmp_w1[l].astype(BF16), w1_bd, w2_k, w2_vt, cmp_aux)
        oc_t, sel_t, listed = _nsa_cmp(q_t, kc_aux, vc_t, poolt_m, pair_m, earlier_m, n_sel)
        nsa = _nsa_slc(listed[:, :, :, 0, :].reshape(-1), listed[:, :, :, 4, 0].reshape(-1),
                       q_t, ks, vst, kw, vwt, sel_t, oc_t, gates_t)
        h, xn, comb, counts = _merge(
            h, norm1_g[l][None, :], nsa.reshape(n, dq), pool_u, conv, wmg,
            w_nsa_proj[l].astype(BF16), pool_bd, pool_scale[l][None, :], w_pool_proj[l].astype(BF16),
            convw, w_conv_proj[l].astype(BF16), w_o[l].astype(BF16), norm2_g[l][None, :], wr, br, before, t)
        h = _moe(h, xn, comb, counts, expert_w1, expert_w3, expert_w2, l,
                 final_norm_g[None, :], final_norm=(l == depth - 1))
    return h.reshape(b, t, d)
```

```python
import functools

import jax
import jax.numpy as jnp
import numpy as np
from jax import lax
from jax.experimental import pallas as pl
from jax.experimental.pallas import tpu as pltpu

F32 = jnp.float32
BF16 = jnp.bfloat16

HEAD_DIM = 64
NSA_HEADS = 8
NSA_KV_HEADS = 2
NSA_GROUP = NSA_HEADS // NSA_KV_HEADS
CMP_LEN = 32
CMP_STRIDE = 16
CMP_HIDDEN = 4 * HEAD_DIM
SLC_BLOCK = 64
SLC_TOPN = 16
WINDOW = 512
NSA_BRANCHES = 3
POOL_WINDOWS = (2, 4, 8, 16)
CONV_K = 3
N_EXPERT_GROUPS = 4
EXPERTS_PER_GROUP = 8
N_EXPERTS = N_EXPERT_GROUPS * EXPERTS_PER_GROUP
RMS_EPS = 1e-6
NEG_INF = -1e30
ALIBI_SLOPES = tuple(float(2.0 ** (-8.0 * (h + 1) / NSA_HEADS)) for h in range(NSA_HEADS))

LANES = 128
VMEM_LIMIT = 56 * 1024 * 1024
TOKEN_TILE = 512
MOE_TILE = 1024
MOE_SORT_TILE = 512
Q_TILE = 128
KEY_CHUNK = 128
GATE_PAD = LANES
ROUTER_PAD = LANES


def _params(*semantics):
    return pltpu.CompilerParams(dimension_semantics=semantics, vmem_limit_bytes=VMEM_LIMIT)


def _dot(a, b):
    return jnp.dot(a, b, preferred_element_type=F32)


def _dot_nt(a, b):
    return lax.dot_general(a, b, (((1,), (1,)), ((), ())), preferred_element_type=F32)


def _rms_norm(x, g):
    y = x * lax.rsqrt(jnp.mean(x * x, axis=-1, keepdims=True) + RMS_EPS)
    return y * g


def _iota(shape, dim):
    return lax.broadcasted_iota(jnp.int32, shape, dim)


def _inproj_kernel(x_ref, g_ref, wq_ref, wv_ref, wg_ref, wk_ref, wn_ref, kaux_ref,
                   q_ref, vs_ref, vw_ref, gate_ref, ks_ref, kw_ref, cmp_ref, pool_ref, conv_ref):
    xn = _rms_norm(x_ref[...], g_ref[...]).astype(BF16)
    sub_tiles = x_ref.shape[0] // Q_TILE
    q_t = _dot_nt(wq_ref[...], xn)
    for g in range(NSA_KV_HEADS):
        for j in range(sub_tiles):
            for r in range(NSA_GROUP):
                head = g * NSA_GROUP + r
                q_ref[0, g, j, :, r * Q_TILE:(r + 1) * Q_TILE] = q_t[
                    head * HEAD_DIM:(head + 1) * HEAD_DIM, j * Q_TILE:(j + 1) * Q_TILE].astype(BF16)
    v_t = _dot_nt(wv_ref[...], xn)
    k = _dot(xn, wk_ref[...])
    for branch, (v_ref, k_ref) in enumerate(((vs_ref, ks_ref), (vw_ref, kw_ref))):
        for g in range(NSA_KV_HEADS):
            slab = branch * NSA_KV_HEADS + g
            for j in range(sub_tiles):
                v_ref[0, g, j] = v_t[slab * HEAD_DIM:(slab + 1) * HEAD_DIM,
                                     j * KEY_CHUNK:(j + 1) * KEY_CHUNK].astype(BF16)
            k_ref[0, g] = (k[:, slab * LANES:(slab + 1) * LANES] + kaux_ref[...]).astype(BF16)
    gate_ref[0] = _dot_nt(wg_ref[...], xn)
    col = 0
    for ref in (cmp_ref, pool_ref, conv_ref):
        width = ref.shape[-1]
        ref[...] = _dot(xn, wn_ref[:, col:col + width]).reshape(ref.shape)
        col += width


def _inproj(h, g, wq_t, wv_t, wg_t, wk, wn, kaux, batch):
    n, d = h.shape
    t = n // batch
    tm = TOKEN_TILE
    steps = t // tm
    sub = tm // Q_TILE
    cw = d // 4
    gq = NSA_GROUP * Q_TILE
    full = lambda a: pl.BlockSpec(a.shape, lambda i: (0,) * a.ndim)
    row = lambda width: pl.BlockSpec((tm, width), lambda i: (i, 0))
    tiles = lambda rows, cols: pl.BlockSpec((1, NSA_KV_HEADS, sub, rows, cols),
                                            lambda i: (i // steps, 0, i % steps, 0, 0))
    keys = pl.BlockSpec((1, NSA_KV_HEADS, tm, LANES), lambda i: (i // steps, 0, i % steps, 0))
    sds = jax.ShapeDtypeStruct
    v_shape = sds((batch, NSA_KV_HEADS, t // KEY_CHUNK, HEAD_DIM, KEY_CHUNK), BF16)
    k_shape = sds((batch, NSA_KV_HEADS, t, LANES), BF16)
    return pl.pallas_call(
        _inproj_kernel,
        grid=(n // tm,),
        in_specs=[row(d), full(g), full(wq_t), full(wv_t), full(wg_t), full(wk), full(wn), full(kaux)],
        out_specs=[tiles(HEAD_DIM, gq), tiles(HEAD_DIM, KEY_CHUNK), tiles(HEAD_DIM, KEY_CHUNK),
                   pl.BlockSpec((1, GATE_PAD, tm), lambda i: (i // steps, 0, i % steps)),
                   keys, keys,
                   pl.BlockSpec((1, tm, 2 * NSA_KV_HEADS * HEAD_DIM), lambda i: (i // steps, i % steps, 0)),
                   row(cw), row(3 * cw)],
        out_shape=[sds((batch, NSA_KV_HEADS, t // Q_TILE, HEAD_DIM, gq), BF16), v_shape, v_shape,
                   sds((batch, GATE_PAD, t), F32), k_shape, k_shape,
                   sds((batch, t, 2 * NSA_KV_HEADS * HEAD_DIM), F32), sds((n, cw), F32), sds((n, 3 * cw), F32)],
        compiler_params=_params("parallel"),
        name="inproj",
    )(h, g, wq_t, wv_t, wg_t, wk, wn, kaux)


def _gelu_tanh(x):
    return 0.5 * x * (1.0 + jnp.tanh(0.7978845608028654 * (x + 0.044715 * x * x * x)))


def _compress_kernel(src_ref, pe_ref, w1_ref, w1bd_ref, w2k_ref, w2vt_ref, caux_ref, kc_ref, vct_ref):
    kind = pl.program_id(1)
    ncp = kc_ref.shape[2]
    hidden = w1_ref.shape[2]
    pieces = [src_ref[0, pl.ds(l, ncp, stride=CMP_STRIDE), :].astype(BF16) for l in range(CMP_STRIDE)]
    chunk = jnp.concatenate(pieces, axis=1)
    first = _dot(chunk, w1bd_ref[0, 0])
    second = _dot(chunk, w1bd_ref[0, 1])
    bias = _dot(pe_ref[0], w1_ref[0])[0:1, :]
    hid = first + pltpu.roll(second, ncp - 1, 0) + jnp.concatenate([bias] * NSA_KV_HEADS, axis=1)
    row = _iota((ncp, 1), 0)
    act = jnp.where(row < ncp - 1, _gelu_tanh(hid), 0.0).astype(BF16)
    for g in range(NSA_KV_HEADS):
        act_g = act[:, g * hidden:(g + 1) * hidden]

        @pl.when(kind == 0)
        def _():
            kc_ref[0, g] = (_dot(act_g, w2k_ref[...]) + caux_ref[...]).astype(BF16)

        @pl.when(kind == 1)
        def _():
            vct_ref[0, g * HEAD_DIM:(g + 1) * HEAD_DIM, :] = _dot_nt(w2vt_ref[...], act_g).astype(BF16)


def _compress(src, pe, w1, w1bd, w2k, w2vt, caux):
    b, t, _ = src.shape
    ncp = t // CMP_STRIDE
    gd = NSA_KV_HEADS * HEAD_DIM
    full = lambda a: pl.BlockSpec(a.shape, lambda bi, k: (0,) * a.ndim)
    per_kind = lambda a: pl.BlockSpec((1,) + a.shape[1:], lambda bi, k: (k,) + (0,) * (a.ndim - 1))
    return pl.pallas_call(
        _compress_kernel,
        grid=(b, 2),
        in_specs=[pl.BlockSpec((1, t, gd), lambda bi, k: (bi, 0, k)),
                  per_kind(pe), per_kind(w1), per_kind(w1bd), full(w2k), full(w2vt), full(caux)],
        out_specs=[pl.BlockSpec((1, NSA_KV_HEADS, ncp, LANES), lambda bi, k: (bi, 0, 0, 0)),
                   pl.BlockSpec((1, gd, ncp), lambda bi, k: (bi, 0, 0))],
        out_shape=[jax.ShapeDtypeStruct((b, NSA_KV_HEADS, ncp, LANES), BF16),
                   jax.ShapeDtypeStruct((b, gd, ncp), BF16)],
        compiler_params=_params("parallel", "arbitrary"),
        name="compress",
    )(src, pe, w1, w1bd, w2k, w2vt, caux)


SOFTMAX_FLOOR = -1e29
TAKEN = -3e38
ATTN_BATCH = 5
BLOCKS_PER_CHUNK = KEY_CHUNK // SLC_BLOCK
AUX_COLS = 16
AUX_SLOPE = BLOCKS_PER_CHUNK
ONES_ROWS = 16
CMP_ROWS_STEP = 128
CMP_TILES = 2
CMP_AUX_SPLIT = 128


def _slope(g, r):
    if isinstance(g, int):
        return jnp.float32(ALIBI_SLOPES[g * NSA_GROUP + r])
    s = jnp.float32(ALIBI_SLOPES[r])
    for gi in range(1, NSA_KV_HEADS):
        s = jnp.where(g == gi, jnp.float32(ALIBI_SLOPES[gi * NSA_GROUP + r]), s)
    return s


def _nsa_cmp_kernel(qt_ref, kc_ref, vct_ref, poolt_ref, pair_ref, earlier_ref, oct_ref, selt_ref, list_ref,
                    *, n_sel):
    first_tile = pl.program_id(1) * CMP_TILES
    qt = selt_ref.shape[4]
    ncp = kc_ref.shape[2]
    n_slc = poolt_ref.shape[0]
    gq = qt_ref.shape[4]
    units = [(g, j) for j in range(CMP_TILES) for g in range(NSA_KV_HEADS)]
    times = [(first_tile + j) * qt + _iota((1, qt), 1) for j in range(CMP_TILES)]
    aux_row = _iota((AUX_COLS, gq), 0)
    col_head = _iota((1, gq), 1) // qt
    pad_rows = jnp.zeros((kc_ref.shape[3] - qt_ref.shape[3] - AUX_COLS, gq), BF16)

    def weights(g, j):
        slope_cols = jnp.zeros((1, gq), F32)
        for r in range(NSA_GROUP):
            slope_cols = jnp.where(col_head == r, _slope(g, r), slope_cols)
        aux = jnp.where(aux_row == 0, slope_cols * (CMP_STRIDE * CMP_AUX_SPLIT),
                        jnp.where(aux_row == 1, slope_cols * CMP_STRIDE, 0.0))
        return jnp.concatenate([qt_ref[0, g, j], aux.astype(BF16), pad_rows], axis=0)

    def importance(g, j, nr, nb):
        t = times[j]
        s = _dot(kc_ref[0, g, :nr, :], weights(g, j))
        edge = min(nr, 2 * CMP_ROWS_STEP)
        cmp_end = ((nr - edge) + _iota((edge, 1), 0)) * CMP_STRIDE + (CMP_LEN - 1)
        visible = cmp_end <= t
        vct = vct_ref[0, g * HEAD_DIM:(g + 1) * HEAD_DIM, :nr]
        psum = jnp.zeros((nr, qt), F32)
        for r in range(NSA_GROUP):
            cols = slice(r * qt, (r + 1) * qt)
            sr = s[:, cols]
            tail = jnp.where(visible, sr[nr - edge:], NEG_INF)
            sr = tail if edge == nr else jnp.concatenate([sr[:nr - edge], tail], axis=0)
            m = jnp.maximum(jnp.max(sr, axis=0, keepdims=True), SOFTMAX_FLOOR)
            e = jnp.exp(sr - m)
            l = jnp.sum(e, axis=0, keepdims=True)
            inv = jnp.where(l > 0.0, 1.0 / l, 0.0)
            oct_ref[0, g, j, :, cols] = (_dot(vct, e.astype(BF16)) * inv).astype(BF16)
            psum = psum + e * inv
        return _dot(poolt_ref[:nb, :nr], psum.astype(BF16))

    def visible_prefix(nr):
        nb = min(n_slc, nr * CMP_STRIDE // SLC_BLOCK)
        imp = jnp.concatenate([importance(g, j, nr, nb) for g, j in units], axis=1)
        blk = _iota((nb, 1), 0)
        cur = jnp.concatenate([times[j] // SLC_BLOCK for _, j in units], axis=1)
        forced = (blk == 0) | (blk == cur) | (blk == cur - 1)
        score = jnp.where(forced, TAKEN, jnp.where(blk <= cur, imp, NEG_INF))
        n_forced = 1 + jnp.where(cur >= 1, 1, 0) + jnp.where(cur >= 2, 1, 0)
        blk_f = blk.astype(F32)

        def take_one(score, active):
            m = jnp.max(score, axis=0, keepdims=True)
            first = jnp.min(jnp.where(score == m, blk_f, F32(1e9)), axis=0, keepdims=True)
            hit = (blk_f == first) if active is None else ((blk_f == first) & active)
            return jnp.where(hit, TAKEN, score)

        common_rounds = max(n_sel - 3, 0)
        for _ in range(common_rounds):
            score = take_one(score, None)

        def early_rounds(score):
            for k in range(common_rounds, n_sel - 1):
                score = take_one(score, n_sel - n_forced > k)
            return score

        score = lax.cond(first_tile * qt < 2 * SLC_BLOCK, early_rounds, lambda sc: sc, score)
        for u, (g, j) in enumerate(units):
            sel_g = score[:, u * qt:(u + 1) * qt] == TAKEN
            selt_ref[0, g, j, :nb, :] = jnp.where(sel_g, 0.0, NEG_INF)
            if nb < n_slc:
                selt_ref[0, g, j, nb:, :] = jnp.full((n_slc - nb, qt), NEG_INF, F32)
            count = _dot_nt(jnp.ones((8, qt), BF16), jnp.where(sel_g, 1.0, 0.0).astype(BF16))
            used = jnp.where(count > 0.0, 1.0, 0.0).astype(BF16)
            chunk = _iota((8, LANES), 1)
            flagged = jnp.where((_dot(used, pair_ref[:nb, :]) > 0.0) & (chunk < first_tile + j), 1.0, 0.0).astype(BF16)
            place = _dot(flagged, earlier_ref[...])
            lands = (_iota((LANES, LANES), 0).astype(F32) == place[0:1, :]) & (flagged[0:1, :] > 0)
            listed = _dot_nt(chunk.astype(BF16), jnp.where(lands, 1.0, 0.0).astype(BF16))
            n_listed = _dot(flagged, jnp.ones((LANES, LANES), BF16))
            list_ref[0, g, j] = jnp.where(_iota((8, LANES), 0) < 4, listed, n_listed).astype(jnp.int32)

    step = min(CMP_ROWS_STEP, ncp)
    rows_needed = jnp.minimum(((first_tile + CMP_TILES) * qt - CMP_LEN) // CMP_STRIDE + 1, ncp)
    n_steps = (rows_needed + step - 1) // step
    for k in range(ncp // step):
        pl.when(n_steps == k + 1)(functools.partial(visible_prefix, (k + 1) * step))


def _nsa_cmp(q_t, kc, vct, poolt_m, pair_m, earlier_m, n_sel):
    b, ng, n_tiles, qrows, qcols = q_t.shape
    ncp = kc.shape[2]
    n_slc = poolt_m.shape[0]
    const = lambda shape: pl.BlockSpec(shape, lambda bi, i: (0,) * len(shape))
    tile5 = lambda rows, cols: pl.BlockSpec((1, ng, CMP_TILES, rows, cols), lambda bi, i: (bi, 0, i, 0, 0))
    return pl.pallas_call(
        functools.partial(_nsa_cmp_kernel, n_sel=n_sel),
        grid=(b, n_tiles // CMP_TILES),
        in_specs=[
            tile5(qrows, qcols),
            pl.BlockSpec((1, ng, ncp, kc.shape[3]), lambda bi, i: (bi, 0, 0, 0)),
            pl.BlockSpec((1, ng * HEAD_DIM, ncp), lambda bi, i: (bi, 0, 0)),
            const(poolt_m.shape), const(pair_m.shape), const(earlier_m.shape),
        ],
        out_specs=[tile5(HEAD_DIM, qcols), tile5(n_slc, Q_TILE), tile5(8, LANES)],
        out_shape=[
            jax.ShapeDtypeStruct((b, ng, n_tiles, HEAD_DIM, qcols), BF16),
            jax.ShapeDtypeStruct((b, ng, n_tiles, n_slc, Q_TILE), F32),
            jax.ShapeDtypeStruct((b, ng, n_tiles, 8, LANES), jnp.int32),
        ],
        compiler_params=_params("parallel", "parallel"),
        name="nsa_compressed",
    )(q_t, kc, vct, poolt_m, pair_m, earlier_m)


def _nsa_slc_kernel(lists_ref, counts_ref, qt_ref, ks_ref, vst_ref, kw_ref, vwt_ref, selt_ref, oct_ref, gt_ref,
                    out_ref, m_ref, l_ref, acc_ref, ow_ref, sa_ref, sb_ref, sw_ref):
    bi = pl.program_id(0)
    g = pl.program_id(1)
    i = pl.program_id(2)
    n_tiles = pl.num_programs(2)
    qt = out_ref.shape[1]
    q_rows = qt_ref[0, 0, 0]
    gq = q_rows.shape[1]
    start = i * qt
    lane_f = _iota((1, qt), 1).astype(F32)
    key_in_chunk = _iota((KEY_CHUNK, qt), 0)
    query_in_tile = _iota((KEY_CHUNK, qt), 1)

    aux_row = _iota((AUX_COLS, gq), 0)
    col_head = _iota((1, gq), 1) // qt
    slope_cols = jnp.zeros((1, gq), F32)
    for r in range(NSA_GROUP):
        slope_cols = jnp.where(col_head == r, _slope(g, r), slope_cols)
    aux_base = jnp.where(aux_row == AUX_SLOPE, slope_cols, 0.0)
    pad_rows = jnp.zeros((ks_ref.shape[3] - q_rows.shape[0] - AUX_COLS, gq), BF16)

    def scores_of(slots):
        return [_dot(k, jnp.concatenate([q_rows, aux.astype(BF16), pad_rows], axis=0))
                for k, aux, _, _, _ in slots]

    def values_of(slots):
        values = jnp.concatenate([v for _, _, v, _, _ in slots], axis=1)
        return jnp.concatenate([values, jnp.ones((ONES_ROWS, values.shape[1]), BF16)], axis=0)

    def softmax_step(slots, scores, v_cat, r, m_old):
        cols = slice(r * qt, (r + 1) * qt)
        srs, tops = [], []
        for j, (_, _, _, shift, mask) in enumerate(slots):
            sr = scores[j, :, cols] if hasattr(scores, "at") else scores[j][:, cols]
            sr = sr if mask is None else jnp.where(mask, sr, NEG_INF)
            srs.append(sr)
            tops.append(jnp.max(sr, axis=0, keepdims=True) + shift[r])
        m_new = functools.reduce(jnp.maximum, tops, m_old)
        ps = [jnp.exp((sr - (m_new - slot[3][r])).astype(BF16)) for slot, sr in zip(slots, srs)]
        weighted = _dot(v_cat, jnp.concatenate(ps, axis=0))
        return m_new, weighted[HEAD_DIM:HEAD_DIM + 1], weighted[:HEAD_DIM]

    def shifts(dist0, ok):
        rows = [-_slope(g, r) * (dist0 + lane_f) for r in range(NSA_GROUP)]
        return rows if ok is None else [jnp.where(ok, row, NEG_INF) for row in rows]

    tile_id = (bi * NSA_KV_HEADS + g) * n_tiles + i
    n_listed = counts_ref[tile_id]

    def selected_slot(c, ok, mask):
        at = pl.multiple_of(c * KEY_CHUNK, KEY_CHUNK)
        bias = selt_ref[0, 0, 0, pl.ds(c * BLOCKS_PER_CHUNK, BLOCKS_PER_CHUNK), :]
        aux = aux_base
        for blk in range(BLOCKS_PER_CHUNK):
            aux = jnp.where(aux_row == blk, jnp.concatenate([bias[blk:blk + 1]] * NSA_GROUP, axis=1), aux)
        return (ks_ref[0, 0, pl.ds(at, KEY_CHUNK), :], aux, vst_ref[0, 0, c],
                shifts((start - c * KEY_CHUNK).astype(F32), ok), mask)

    def listed_slot(idx):
        ok = idx < n_listed
        c = jnp.where(ok, lists_ref[tile_id * LANES + jnp.minimum(idx, jnp.maximum(n_listed - 1, 0))], 0)
        return selected_slot(c, ok, None)

    floor = jnp.full((1, qt), SOFTMAX_FLOOR, F32)

    n_back = WINDOW // KEY_CHUNK
    slots = []
    for j in range(n_back + 1):
        cs = start - WINDOW + j * KEY_CHUNK
        chunk = jnp.maximum(cs, 0) // KEY_CHUNK
        at = pl.multiple_of(chunk * KEY_CHUNK, KEY_CHUNK)
        mask = (query_in_tile < key_in_chunk) if j == 0 else (
            (key_in_chunk <= query_in_tile) if j == n_back else None)
        slots.append((kw_ref[0, 0, pl.ds(at, KEY_CHUNK), :], aux_base, vwt_ref[0, 0, chunk],
                      shifts(F32(WINDOW - j * KEY_CHUNK), cs >= 0), mask))
    window_slots = slots
    first_slots = [selected_slot(i, None, key_in_chunk <= query_in_tile)] + [
        listed_slot(j) for j in range(ATTN_BATCH - 1)]

    def listed_batch(n):
        return [listed_slot(ATTN_BATCH - 1 + n * ATTN_BATCH + j) for j in range(ATTN_BATCH)]

    def issue(slots, dst_ref):
        for j, s in enumerate(scores_of(slots)):
            dst_ref[j] = s

    issue(window_slots, sw_ref)
    issue(first_slots, sb_ref)
    v_cat = values_of(window_slots)
    for r in range(NSA_GROUP):
        _, total, weighted = softmax_step(window_slots, sw_ref, v_cat, r, floor)
        ow_ref[:, r * qt:(r + 1) * qt] = weighted * jnp.where(total > 0.0, 1.0 / total, 0.0)

    def consume(slots, src_ref):
        v_cat = values_of(slots)
        for r in range(NSA_GROUP):
            cols = slice(r * qt, (r + 1) * qt)
            m_old = m_ref[r]
            m_new, total, weighted = softmax_step(slots, src_ref, v_cat, r, m_old)
            alpha = jnp.exp(m_old - m_new)
            l_ref[r] = alpha * l_ref[r] + total
            acc_ref[:, cols] = alpha * acc_ref[:, cols] + weighted
            m_ref[r] = m_new

    issue(listed_batch(0), sa_ref)
    v_cat = values_of(first_slots)
    for r in range(NSA_GROUP):
        m_ref[r], l_ref[r], acc_ref[:, r * qt:(r + 1) * qt] = softmax_step(first_slots, sb_ref, v_cat, r, floor)

    def batch_pair(it, carry):
        first, second, third = (listed_batch(2 * it + n) for n in range(3))
        issue(second, sb_ref)
        consume(first, sa_ref)
        issue(third, sa_ref)
        consume(second, sb_ref)
        return carry

    n_rest = jnp.maximum(n_listed - (ATTN_BATCH - 1), 0)
    n_batches = (n_rest + ATTN_BATCH - 1) // ATTN_BATCH
    lax.fori_loop(0, n_batches // 2, batch_pair, 0)

    @pl.when(n_batches % 2 == 1)
    def _():
        consume(listed_batch(n_batches - 1), sa_ref)

    def finalize(r):
        l = l_ref[r]
        return acc_ref[:, r * qt:(r + 1) * qt] * jnp.where(l > 0.0, 1.0 / l, 0.0)

    outs = []
    for r in range(NSA_GROUP):
        cols = slice(r * qt, (r + 1) * qt)
        col = (g * NSA_GROUP + r) * NSA_BRANCHES
        gate = lambda br: jax.nn.sigmoid(gt_ref[0, pl.ds(col + br, 1), :])
        outs.append(gate(0) * oct_ref[0, 0, 0, :, cols].astype(F32) + gate(1) * finalize(r)
                    + gate(2) * ow_ref[:, cols])
    out_ref[0] = jnp.concatenate(outs, axis=0).T.astype(out_ref.dtype)


def _nsa_slc(lists, counts, q_t, ks, vst, kw, vwt, selt, oct, gates_t):
    b, _, t, kw_cols = ks.shape
    _, _, n_tiles, qrows, qcols = q_t.shape
    n_slc = selt.shape[3]
    n_chunks = vst.shape[2]
    gw = NSA_GROUP * HEAD_DIM
    once = dict(pipeline_mode=pl.Buffered(1))
    k_spec = pl.BlockSpec((1, 1, t, kw_cols), lambda bi, g, i, ls, ns: (bi, g, 0, 0), **once)
    vt_spec = pl.BlockSpec((1, 1, n_chunks, HEAD_DIM, KEY_CHUNK), lambda bi, g, i, ls, ns: (bi, g, 0, 0, 0), **once)
    tile5 = lambda rows, cols: pl.BlockSpec((1, 1, 1, rows, cols), lambda bi, g, i, ls, ns: (bi, g, i, 0, 0))
    grid_spec = pltpu.PrefetchScalarGridSpec(
        num_scalar_prefetch=2,
        grid=(b, NSA_KV_HEADS, n_tiles),
        in_specs=[
            tile5(qrows, qcols),
            k_spec, vt_spec, k_spec, vt_spec,
            tile5(n_slc, Q_TILE),
            tile5(HEAD_DIM, qcols),
            pl.BlockSpec((1, GATE_PAD, Q_TILE), lambda bi, g, i, ls, ns: (bi, 0, i)),
        ],
        out_specs=pl.BlockSpec((1, Q_TILE, gw), lambda bi, g, i, ls, ns: (bi, i, g)),
        scratch_shapes=[pltpu.VMEM((NSA_GROUP, 1, Q_TILE), F32), pltpu.VMEM((NSA_GROUP, 1, Q_TILE), F32),
                        pltpu.VMEM((HEAD_DIM, qcols), F32), pltpu.VMEM((HEAD_DIM, qcols), F32),
                        pltpu.VMEM((ATTN_BATCH, KEY_CHUNK, qcols), F32),
                        pltpu.VMEM((ATTN_BATCH, KEY_CHUNK, qcols), F32),
                        pltpu.VMEM((WINDOW // KEY_CHUNK + 1, KEY_CHUNK, qcols), F32)],
    )
    return pl.pallas_call(
        _nsa_slc_kernel,
        grid_spec=grid_spec,
        out_shape=jax.ShapeDtypeStruct((b, t, NSA_HEADS * HEAD_DIM), BF16),
        compiler_params=_params("parallel", "parallel", "parallel"),
        name="nsa_selected_window",
    )(lists, counts, q_t, ks, vst, kw, vwt, selt, oct, gates_t)


POOL_HALO = 16
CONV_HALO = 8


def _merge_kernel(h_ref, g_ref, nsa_ref, pool_ref, pool_halo_ref, conv_ref, conv_halo_ref,
                  wmg_ref, wnsa_ref, pool_bd_ref, pool_scale_ref, wpool_ref, convw_ref, wconv_ref, wo_ref,
                  g2_ref, wr_ref, br_ref, before_ref,
                  out_ref, xn_ref, comb_ref, count_ref, pool_ext, conv_ext, *, seq_len):
    i = pl.program_id(0)
    tm, d = h_ref.shape
    cw = pool_ref.shape[1]
    pos0 = (i * tm) % seq_len
    keep_halo = jnp.where(pos0 == 0, 0.0, 1.0)
    pos = pos0 + _iota((tm, 1), 0)

    u = pool_ref[...]
    pool_ext[0:POOL_HALO, :] = pool_halo_ref[...] * keep_halo
    pool_ext[POOL_HALO:, :] = u
    lane_group = _iota((1, cw), 1) // (cw // len(POOL_WINDOWS))
    total = u
    mean = jnp.zeros_like(u)
    done = 1
    for gi, win in enumerate(POOL_WINDOWS):
        for k in range(done, win):
            total = total + pool_ext[POOL_HALO - k:POOL_HALO - k + tm, :]
        done = win
        cnt = jnp.minimum(pos + 1, win).astype(F32)
        mean = jnp.where(lane_group == gi, total / cnt, mean)
    pooled = (mean - u).astype(BF16)
    mixed = _dot(pooled, pool_bd_ref[...]) * pool_scale_ref[...]
    y_pool = _dot(mixed.astype(BF16), wpool_ref[...])

    ch = conv_ref[:, 0:cw]
    cb = conv_ref[:, cw:2 * cw]
    cc = conv_ref[:, 2 * cw:3 * cw]
    conv_ext[0:CONV_HALO, :] = conv_halo_ref[:, 0:cw] * conv_halo_ref[:, 2 * cw:3 * cw] * keep_halo
    conv_ext[CONV_HALO:, :] = cc * ch
    y = jnp.zeros((tm, cw), F32)
    for k in range(CONV_K):
        off = CONV_HALO - (CONV_K - 1) + k
        y = y + convw_ref[k:k + 1, :] * conv_ext[off:off + tm, :]
    y_conv = _dot((cb * y).astype(BF16), wconv_ref[...])

    y_nsa = _dot(nsa_ref[...], wnsa_ref[...])

    h = h_ref[...]
    xn = _rms_norm(h, g_ref[...]).astype(BF16)
    merged = jnp.zeros((tm, d), F32)
    for br, y_br in enumerate((y_nsa, y_pool, y_conv)):
        mg = jax.nn.sigmoid(_dot(xn, wmg_ref[:, br * d:(br + 1) * d]))
        merged = merged + mg * y_br
    h_new = h + _dot(merged.astype(BF16), wo_ref[...])
    out_ref[...] = h_new
    _route_tile(h_new, g2_ref, wr_ref, br_ref, before_ref, xn_ref, comb_ref, count_ref)


def _merge(h, g, nsa, pool_u, conv, wmg, wnsa, pool_bd, pool_scale, wpool, convw, wconv, wo,
           g2, wr, br, before, seq_len):
    n, d = h.shape
    tm = TOKEN_TILE
    cw = pool_u.shape[1]
    row = lambda width: pl.BlockSpec((tm, width), lambda i: (i, 0))
    full = lambda a: pl.BlockSpec(a.shape, lambda i: (0,) * a.ndim)
    halo = lambda rows, width: pl.BlockSpec(
        (rows, width), lambda i: (jnp.maximum(i * (tm // rows) - 1, 0), 0))
    return pl.pallas_call(
        functools.partial(_merge_kernel, seq_len=seq_len),
        grid=(n // tm,),
        in_specs=[row(d), full(g), row(nsa.shape[1]), row(cw), halo(POOL_HALO, cw),
                  row(conv.shape[1]), halo(CONV_HALO, conv.shape[1]),
                  full(wmg), full(wnsa), full(pool_bd), full(pool_scale), full(wpool), full(convw),
                  full(wconv), full(wo), full(g2), full(wr), full(br), full(before)],
        out_specs=[row(d), row(d + ROUTER_PAD), row(ROUTER_PAD), pl.BlockSpec((8, ROUTER_PAD), lambda i: (0, 0))],
        out_shape=[jax.ShapeDtypeStruct((n, d), F32), jax.ShapeDtypeStruct((n, d + ROUTER_PAD), BF16),
                   jax.ShapeDtypeStruct((n, ROUTER_PAD), F32), jax.ShapeDtypeStruct((8, ROUTER_PAD), F32)],
        scratch_shapes=[pltpu.VMEM((tm + POOL_HALO, cw), F32), pltpu.VMEM((tm + CONV_HALO, cw), F32)],
        compiler_params=_params("arbitrary"),
        name="merge",
    )(h, g, nsa, pool_u, pool_u, conv, conv, wmg, wnsa, pool_bd, pool_scale, wpool, convw, wconv, wo,
      g2, wr, br, before)


def _route(logits):
    lane = _iota(logits.shape, 1)
    lane_f = lane.astype(F32)
    big = F32(1e9)
    is_group = lane < N_EXPERT_GROUPS
    gl = jnp.where(is_group, logits, NEG_INF)
    g_max = jnp.max(gl, axis=1, keepdims=True)
    g_sel = jnp.min(jnp.where(gl == g_max, lane_f, big), axis=1, keepdims=True)
    g_prob = 1.0 / jnp.sum(jnp.where(is_group, jnp.exp(gl - g_max), 0.0), axis=1, keepdims=True)
    lo = N_EXPERT_GROUPS + EXPERTS_PER_GROUP * g_sel
    in_group = (lane_f >= lo) & (lane_f < lo + EXPERTS_PER_GROUP)
    el = jnp.where(in_group, logits, NEG_INF)
    v1 = jnp.max(el, axis=1, keepdims=True)
    i1 = jnp.min(jnp.where((el == v1) & in_group, lane_f, big), axis=1, keepdims=True)
    el2 = jnp.where(lane_f == i1, NEG_INF, el)
    rest = in_group & (lane_f != i1)
    v2 = jnp.max(el2, axis=1, keepdims=True)
    i2 = jnp.min(jnp.where((el2 == v2) & rest, lane_f, big), axis=1, keepdims=True)
    e2 = jnp.exp(v2 - v1)
    w1 = g_prob / (1.0 + e2)
    w2 = g_prob * e2 / (1.0 + e2)
    return jnp.where(lane_f == i1, w1, 0.0) + jnp.where(lane_f == i2, w2, 0.0), g_sel


GROUP_LANE = N_EXPERT_GROUPS + N_EXPERTS


RANK_LANE = GROUP_LANE + 1


def _route_tile(h, g_ref, wr_ref, br_ref, before_ref, xn_ref, comb_ref, count_ref):
    @pl.when(pl.program_id(0) == 0)
    def _():
        count_ref[...] = jnp.zeros(count_ref.shape, F32)

    xn = _rms_norm(h, g_ref[...])
    xn_hi = xn.astype(BF16)
    xn_lo = (xn - xn_hi.astype(F32)).astype(BF16)
    logits = (_dot(xn_hi, wr_ref[0]) + (_dot(xn_hi, wr_ref[1]) + _dot(xn_lo, wr_ref[0]))) + br_ref[...]
    comb, g_sel = _route(logits)
    lane = _iota(comb.shape, 1)
    chose = jnp.where(lane.astype(F32) == g_sel, 1.0, 0.0)
    earlier = _dot(before_ref[...], chose.astype(BF16)) + count_ref[0:1, :]
    rank = jnp.sum(chose * earlier, axis=1, keepdims=True)
    count_ref[0:1, :] = count_ref[0:1, :] + jnp.sum(chose, axis=0, keepdims=True)
    comb_ref[...] = jnp.where(lane == GROUP_LANE, g_sel, jnp.where(lane == RANK_LANE, rank, comb))
    first = N_EXPERT_GROUPS + EXPERTS_PER_GROUP * g_sel
    local = jnp.zeros(comb.shape, F32)
    for e in range(EXPERTS_PER_GROUP):
        c_e = jnp.sum(jnp.where(lane.astype(F32) == first + e, comb, 0.0), axis=1, keepdims=True)
        local = jnp.where((lane == e) | (lane == EXPERTS_PER_GROUP + e), c_e, local)
    local_hi = local.astype(BF16)
    d = xn_hi.shape[1]
    xn_ref[:, :d] = xn_hi
    xn_ref[:, d:] = jnp.where(lane < EXPERTS_PER_GROUP, local_hi, (local - local_hi.astype(F32)).astype(BF16))


def _experts_kernel(tile_group_ref, n_active_ref, x_ref, w1_ref, w3_ref, w2_ref, out_ref, acc_ref):
    i = pl.program_id(0)

    @pl.when(i < n_active_ref[0])
    def _():
        d = out_ref.shape[1]
        x = x_ref[:, :d]
        comb = x_ref[:, d:].astype(F32)
        lane = _iota(comb.shape, 1)
        for e in range(EXPERTS_PER_GROUP):
            c_e = jnp.sum(jnp.where((lane == e) | (lane == EXPERTS_PER_GROUP + e), comb, 0.0),
                          axis=1, keepdims=True)
            a = (jax.nn.silu(_dot(x, w1_ref[0, 0, e].astype(BF16)))
                 * _dot(x, w3_ref[0, 0, e].astype(BF16))) * c_e
            y = _dot(a.astype(BF16), w2_ref[0, 0, e].astype(BF16))
            if e == 0:
                acc_ref[...] = y
            else:
                acc_ref[...] += y
        out_ref[...] = acc_ref[...].astype(out_ref.dtype)

    @pl.when(i >= n_active_ref[0])
    def _():
        out_ref[...] = jnp.zeros(out_ref.shape, out_ref.dtype)


def _experts(tile_group, n_active, x_sorted, w1, w3, w2, layer):
    ns = x_sorted.shape[0]
    d = w1.shape[3]
    tm = MOE_SORT_TILE
    group_w = lambda w: pl.BlockSpec((1, 1) + w.shape[2:], lambda i, tg, na: (layer, tg[i], 0, 0, 0),
                                     pipeline_mode=pl.Buffered(1))
    grid_spec = pltpu.PrefetchScalarGridSpec(
        num_scalar_prefetch=2,
        grid=(ns // tm,),
        in_specs=[
            pl.BlockSpec((tm, x_sorted.shape[1]), lambda i, tg, na: (i, 0)),
            group_w(w1), group_w(w3), group_w(w2),
        ],
        out_specs=pl.BlockSpec((tm, d), lambda i, tg, na: (i, 0)),
        scratch_shapes=[pltpu.VMEM((tm, d), F32)],
    )
    return pl.pallas_call(
        _experts_kernel,
        grid_spec=grid_spec,
        out_shape=jax.ShapeDtypeStruct((ns, d), BF16),
        compiler_params=_params("arbitrary"),
        name="experts",
    )(tile_group, n_active, x_sorted, w1, w3, w2)


def _residual_kernel(h_ref, y_ref, gf_ref, out_ref, *, final_norm):
    out = h_ref[...] + y_ref[...].astype(F32)
    out_ref[...] = _rms_norm(out, gf_ref[...]) if final_norm else out


def _residual(h, y, gf, final_norm):
    n, d = h.shape
    tm = MOE_TILE
    row = pl.BlockSpec((tm, d), lambda i: (i, 0))
    return pl.pallas_call(
        functools.partial(_residual_kernel, final_norm=final_norm),
        grid=(n // tm,),
        in_specs=[row, row, pl.BlockSpec(gf.shape, lambda i: (0, 0))],
        out_specs=row,
        out_shape=jax.ShapeDtypeStruct((n, d), F32),
        compiler_params=_params("parallel"),
        name="residual",
    )(h, y, gf)


def _group_sort_plan(group_id, rank, counts, tile):
    n = group_id.shape[0]
    n_slots = n + N_EXPERT_GROUPS * tile
    padded = (counts + tile - 1) // tile * tile
    ends = jnp.cumsum(padded)
    slot = (ends - padded)[group_id] + rank
    source = jnp.zeros((n_slots,), jnp.int32).at[slot].set(jnp.arange(n, dtype=jnp.int32))
    tile_start = jnp.arange(n_slots // tile, dtype=jnp.int32) * tile
    tile_group = jnp.minimum(jnp.searchsorted(ends, tile_start, side="right"), N_EXPERT_GROUPS - 1)
    return slot, source, tile_group.astype(jnp.int32), (ends[-1:] // tile).astype(jnp.int32)


def _moe(h, xn, comb, counts, w1, w3, w2, layer, gf, final_norm):
    as_int = lambda a: a.astype(jnp.int32)
    slot, source, tile_group, n_active = _group_sort_plan(
        as_int(comb[:, GROUP_LANE]), as_int(comb[:, RANK_LANE]), as_int(counts[0, :N_EXPERT_GROUPS]),
        MOE_SORT_TILE)
    grouped = lambda w: w.reshape((w.shape[0], N_EXPERT_GROUPS, EXPERTS_PER_GROUP) + w.shape[2:])
    rows = lambda a, idx: jnp.take(a, idx, axis=0, mode="clip")
    y_sorted = _experts(tile_group, n_active, rows(xn, source), grouped(w1), grouped(w3), grouped(w2), layer)
    return _residual(h, rows(y_sorted, slot), gf, final_norm)


def _selection_constants(seq_len):
    ncp = seq_len // CMP_STRIDE
    n_slc = seq_len // SLC_BLOCK
    ratio = SLC_BLOCK // CMP_STRIDE
    lead = CMP_LEN // CMP_STRIDE - 1
    c = np.arange(ncp)[:, None]
    j = np.arange(n_slc)[None, :]
    pool_m = ((c >= ratio * j - lead) & (c < ratio * j + ratio)).astype(np.float32)
    blocks_per_chunk = KEY_CHUNK // SLC_BLOCK
    n_chunks = seq_len // KEY_CHUNK
    pair_m = np.zeros((n_slc, LANES * ((n_chunks + LANES - 1) // LANES)), np.float32)
    pair_m[np.arange(n_slc), np.arange(n_slc) // blocks_per_chunk] = 1.0
    earlier_m = np.triu(np.ones((LANES, LANES), np.float32), 1)
    key_aux = np.zeros((TOKEN_TILE, LANES), np.float32)
    in_chunk = np.arange(TOKEN_TILE) % KEY_CHUNK
    key_aux[np.arange(TOKEN_TILE), HEAD_DIM + in_chunk // SLC_BLOCK] = 1.0
    key_aux[:, HEAD_DIM + AUX_SLOPE] = in_chunk
    cmp_aux = np.zeros((ncp, LANES), np.float32)
    cmp_aux[:, HEAD_DIM] = np.arange(ncp) // CMP_AUX_SPLIT
    cmp_aux[:, HEAD_DIM + 1] = np.arange(ncp) % CMP_AUX_SPLIT
    as_bf16 = lambda a: jnp.asarray(a, BF16)
    return as_bf16(pool_m.T), as_bf16(pair_m), as_bf16(earlier_m), jnp.asarray(key_aux), jnp.asarray(cmp_aux)


def kernel(x, norm1_g, w_in, cmp_pe, cmp_w1, cmp_w2, w_nsa_proj, pool_w, pool_scale, w_pool_proj, conv_w,
           w_conv_proj, w_o, norm2_g, router_group_w, router_group_b, router_expert_w, router_expert_b,
           expert_w1, expert_w3, expert_w2, final_norm_g):
    b, t, d = x.shape
    n = b * t
    depth = w_in.shape[0]
    dq = NSA_HEADS * HEAD_DIM
    dkv = 6 * NSA_KV_HEADS * HEAD_DIM
    dgate = NSA_HEADS * NSA_BRANCHES
    cw = d // 4
    assert t % TOKEN_TILE == 0 and n % MOE_TILE == 0 and t % (Q_TILE * CMP_TILES) == 0
    n_slc = t // SLC_BLOCK
    n_sel = min(SLC_TOPN, n_slc)
    n_chunks16 = t // CMP_STRIDE
    kvw = NSA_KV_HEADS * HEAD_DIM
    poolt_m, pair_m, earlier_m, key_aux, cmp_aux = _selection_constants(t)
    assert Q_TILE == KEY_CHUNK and AUX_SLOPE < AUX_COLS and KEY_CHUNK <= 256
    assert n_chunks16 <= 256 * CMP_AUX_SPLIT and n_chunks16 % min(CMP_ROWS_STEP, n_chunks16) == 0
    assert pair_m.shape[1] == LANES

    before = jnp.asarray(np.tril(np.ones((TOKEN_TILE, TOKEN_TILE), np.float32), -1), BF16)
    h = x.reshape(n, d)
    for l in range(depth):
        wl = w_in[l]
        o_gate = dq + dkv
        o_pool = o_gate + dgate
        o_merge = o_pool + cw + 3 * cw
        kv_cols = lambda kind: wl[:, dq + kind * kvw:dq + (kind + 1) * kvw]
        wq_t = (wl[:, :dq] * (HEAD_DIM ** -0.5)).T.astype(BF16)
        wv_t = jnp.concatenate([kv_cols(3), kv_cols(5)], axis=1).T.astype(BF16)
        wg_t = jnp.pad(wl[:, o_gate:o_pool], ((0, 0), (0, GATE_PAD - dgate))).T.astype(BF16)
        no_aux = jnp.zeros((d, LANES - HEAD_DIM), F32)
        wk = jnp.concatenate([piece for kind in (2, 4) for gi in range(NSA_KV_HEADS)
                              for piece in (kv_cols(kind)[:, gi * HEAD_DIM:(gi + 1) * HEAD_DIM], no_aux)],
                             axis=1).astype(BF16)
        wn = jnp.concatenate([kv_cols(0), kv_cols(1), wl[:, o_pool:o_merge]], axis=1).astype(BF16)
        wmg = wl[:, o_merge:].astype(BF16)
        pool_bd = jax.scipy.linalg.block_diag(*[pool_w[l, gi] for gi in range(pool_w.shape[1])]).astype(BF16)
        convw = jnp.pad(conv_w[l], ((0, 8 - CONV_K), (0, 0)))
        wr = jnp.pad(jnp.concatenate([router_group_w[l], router_expert_w[l]], axis=1),
                     ((0, 0), (0, ROUTER_PAD - N_EXPERT_GROUPS - N_EXPERTS)))
        wr_hi = wr.astype(BF16)
        wr = jnp.stack([wr_hi, (wr - wr_hi.astype(F32)).astype(BF16)])
        br = jnp.pad(jnp.concatenate([router_group_b[l], router_expert_b[l]]),
                     (0, ROUTER_PAD - N_EXPERT_GROUPS - N_EXPERTS))[None, :]
        pe = jnp.broadcast_to(cmp_pe[l].reshape(2, 1, CMP_LEN * HEAD_DIM), (2, 8, CMP_LEN * HEAD_DIM)).astype(BF16)
        halves = CMP_LEN // CMP_STRIDE
        w1_bd = jnp.einsum("khldc,gq->khlgdqc",
                           cmp_w1[l].reshape(2, halves, CMP_STRIDE, HEAD_DIM, CMP_HIDDEN),
                           jnp.eye(NSA_KV_HEADS, dtype=F32))
        w1_bd = w1_bd.reshape(2, halves, CMP_STRIDE * kvw, NSA_KV_HEADS * CMP_HIDDEN).astype(BF16)
        w2_k = jnp.pad(cmp_w2[l, 0], ((0, 0), (0, LANES - HEAD_DIM))).astype(BF16)
        w2_vt = cmp_w2[l, 1].T.astype(BF16)

        q_t, vst, vwt, gates_t, ks, kw, cmp_src, pool_u, conv = _inproj(
            h, norm1_g[l][None, :], wq_t, wv_t, wg_t, wk, wn, key_aux, b)
        kc_aux, vc_t = _compress(cmp_src, pe, cmp_w1[l].astype(BF16), w1_bd, w2_k, w2_vt, cmp_aux)
        oc_t, sel_t, listed = _nsa_cmp(q_t, kc_aux, vc_t, poolt_m, pair_m, earlier_m, n_sel)
        nsa = _nsa_slc(listed[:, :, :, 0, :].reshape(-1), listed[:, :, :, 4, 0].reshape(-1),
                       q_t, ks, vst, kw, vwt, sel_t, oc_t, gates_t)
        h, xn, comb, counts = _merge(
            h, norm1_g[l][None, :], nsa.reshape(n, dq), pool_u, conv, wmg,
            w_nsa_proj[l].astype(BF16), pool_bd, pool_scale[l][None, :], w_pool_proj[l].astype(BF16),
            convw, w_conv_proj[l].astype(BF16), w_o[l].astype(BF16), norm2_g[l][None, :], wr, br, before, t)
        h = _moe(h, xn, comb, counts, expert_w1, expert_w3, expert_w2, l,
                 final_norm_g[None, :], final_norm=(l == depth - 1))
    return h.reshape(b, t, d)
```

```python
import functools

import jax
import jax.numpy as jnp
import numpy as np
from jax import lax
from jax.experimental import pallas as pl
from jax.experimental.pallas import tpu as pltpu

F32 = jnp.float32
BF16 = jnp.bfloat16

HEAD_DIM = 64
NSA_HEADS = 8
NSA_KV_HEADS = 2
NSA_GROUP = NSA_HEADS // NSA_KV_HEADS
CMP_LEN = 32
CMP_STRIDE = 16
CMP_HIDDEN = 4 * HEAD_DIM
SLC_BLOCK = 64
SLC_TOPN = 16
WINDOW = 512
NSA_BRANCHES = 3
POOL_WINDOWS = (2, 4, 8, 16)
CONV_K = 3
N_EXPERT_GROUPS = 4
EXPERTS_PER_GROUP = 8
N_EXPERTS = N_EXPERT_GROUPS * EXPERTS_PER_GROUP
RMS_EPS = 1e-6
NEG_INF = -1e30
ALIBI_SLOPES = tuple(float(2.0 ** (-8.0 * (h + 1) / NSA_HEADS)) for h in range(NSA_HEADS))

LANES = 128
VMEM_LIMIT = 56 * 1024 * 1024
TOKEN_TILE = 512
MOE_TILE = 1024
MOE_SORT_TILE = 512
Q_TILE = 128
KEY_CHUNK = 128
GATE_PAD = LANES
ROUTER_PAD = LANES


def _params(*semantics):
    return pltpu.CompilerParams(dimension_semantics=semantics, vmem_limit_bytes=VMEM_LIMIT)


def _dot(a, b):
    return jnp.dot(a, b, preferred_element_type=F32)


def _dot_nt(a, b):
    return lax.dot_general(a, b, (((1,), (1,)), ((), ())), preferred_element_type=F32)


def _rms_norm(x, g):
    y = x * lax.rsqrt(jnp.mean(x * x, axis=-1, keepdims=True) + RMS_EPS)
    return y * g


def _iota(shape, dim):
    return lax.broadcasted_iota(jnp.int32, shape, dim)


def _inproj_kernel(x_ref, g_ref, wq_ref, wv_ref, wg_ref, wk_ref, wn_ref, kaux_ref,
                   q_ref, vs_ref, vw_ref, gate_ref, ks_ref, kw_ref, cmp_ref, pool_ref, conv_ref):
    xn = _rms_norm(x_ref[...], g_ref[...]).astype(BF16)
    sub_tiles = x_ref.shape[0] // Q_TILE
    q_t = _dot_nt(wq_ref[...], xn)
    for g in range(NSA_KV_HEADS):
        for j in range(sub_tiles):
            for r in range(NSA_GROUP):
                head = g * NSA_GROUP + r
                q_ref[0, g, j, :, r * Q_TILE:(r + 1) * Q_TILE] = q_t[
                    head * HEAD_DIM:(head + 1) * HEAD_DIM, j * Q_TILE:(j + 1) * Q_TILE].astype(BF16)
    v_t = _dot_nt(wv_ref[...], xn)
    k = _dot(xn, wk_ref[...])
    for branch, (v_ref, k_ref) in enumerate(((vs_ref, ks_ref), (vw_ref, kw_ref))):
        for g in range(NSA_KV_HEADS):
            slab = branch * NSA_KV_HEADS + g
            for j in range(sub_tiles):
                v_ref[0, g, j] = v_t[slab * HEAD_DIM:(slab + 1) * HEAD_DIM,
                                     j * KEY_CHUNK:(j + 1) * KEY_CHUNK].astype(BF16)
            k_ref[0, g] = (k[:, slab * LANES:(slab + 1) * LANES] + kaux_ref[...]).astype(BF16)
    gate_ref[0] = _dot_nt(wg_ref[...], xn)
    col = 0
    for ref in (cmp_ref, pool_ref, conv_ref):
        width = ref.shape[-1]
        ref[...] = _dot(xn, wn_ref[:, col:col + width]).reshape(ref.shape)
        col += width


def _inproj(h, g, wq_t, wv_t, wg_t, wk, wn, kaux, batch):
    n, d = h.shape
    t = n // batch
    tm = TOKEN_TILE
    steps = t // tm
    sub = tm // Q_TILE
    cw = d // 4
    gq = NSA_GROUP * Q_TILE
    full = lambda a: pl.BlockSpec(a.shape, lambda i: (0,) * a.ndim)
    row = lambda width: pl.BlockSpec((tm, width), lambda i: (i, 0))
    tiles = lambda rows, cols: pl.BlockSpec((1, NSA_KV_HEADS, sub, rows, cols),
                                            lambda i: (i // steps, 0, i % steps, 0, 0))
    keys = pl.BlockSpec((1, NSA_KV_HEADS, tm, LANES), lambda i: (i // steps, 0, i % steps, 0))
    sds = jax.ShapeDtypeStruct
    v_shape = sds((batch, NSA_KV_HEADS, t // KEY_CHUNK, HEAD_DIM, KEY_CHUNK), BF16)
    k_shape = sds((batch, NSA_KV_HEADS, t, LANES), BF16)
    return pl.pallas_call(
        _inproj_kernel,
        grid=(n // tm,),
        in_specs=[row(d), full(g), full(wq_t), full(wv_t), full(wg_t), full(wk), full(wn), full(kaux)],
        out_specs=[tiles(HEAD_DIM, gq), tiles(HEAD_DIM, KEY_CHUNK), tiles(HEAD_DIM, KEY_CHUNK),
                   pl.BlockSpec((1, GATE_PAD, tm), lambda i: (i // steps, 0, i % steps)),
                   keys, keys,
                   pl.BlockSpec((1, tm, 2 * NSA_KV_HEADS * HEAD_DIM), lambda i: (i // steps, i % steps, 0)),
                   row(cw), row(3 * cw)],
        out_shape=[sds((batch, NSA_KV_HEADS, t // Q_TILE, HEAD_DIM, gq), BF16), v_shape, v_shape,
                   sds((batch, GATE_PAD, t), F32), k_shape, k_shape,
                   sds((batch, t, 2 * NSA_KV_HEADS * HEAD_DIM), F32), sds((n, cw), F32), sds((n, 3 * cw), F32)],
        compiler_params=_params("parallel"),
        name="inproj",
    )(h, g, wq_t, wv_t, wg_t, wk, wn, kaux)


def _gelu_tanh(x):
    return 0.5 * x * (1.0 + jnp.tanh(0.7978845608028654 * (x + 0.044715 * x * x * x)))


def _compress_kernel(src_ref, pe_ref, w1_ref, w1bd_ref, w2k_ref, w2vt_ref, caux_ref, kc_ref, vct_ref):
    kind = pl.program_id(1)
    ncp = kc_ref.shape[2]
    hidden = w1_ref.shape[2]
    pieces = [src_ref[0, pl.ds(l, ncp, stride=CMP_STRIDE), :].astype(BF16) for l in range(CMP_STRIDE)]
    chunk = jnp.concatenate(pieces, axis=1)
    first = _dot(chunk, w1bd_ref[0, 0])
    second = _dot(chunk, w1bd_ref[0, 1])
    bias = _dot(pe_ref[0], w1_ref[0])[0:1, :]
    hid = first + pltpu.roll(second, ncp - 1, 0) + jnp.concatenate([bias] * NSA_KV_HEADS, axis=1)
    row = _iota((ncp, 1), 0)
    act = jnp.where(row < ncp - 1, _gelu_tanh(hid), 0.0).astype(BF16)
    for g in range(NSA_KV_HEADS):
        act_g = act[:, g * hidden:(g + 1) * hidden]

        @pl.when(kind == 0)
        def _():
            kc_ref[0, g] = (_dot(act_g, w2k_ref[...]) + caux_ref[...]).astype(BF16)

        @pl.when(kind == 1)
        def _():
            vct_ref[0, g * HEAD_DIM:(g + 1) * HEAD_DIM, :] = _dot_nt(w2vt_ref[...], act_g).astype(BF16)


def _compress(src, pe, w1, w1bd, w2k, w2vt, caux):
    b, t, _ = src.shape
    ncp = t // CMP_STRIDE
    gd = NSA_KV_HEADS * HEAD_DIM
    full = lambda a: pl.BlockSpec(a.shape, lambda bi, k: (0,) * a.ndim)
    per_kind = lambda a: pl.BlockSpec((1,) + a.shape[1:], lambda bi, k: (k,) + (0,) * (a.ndim - 1))
    return pl.pallas_call(
        _compress_kernel,
        grid=(b, 2),
        in_specs=[pl.BlockSpec((1, t, gd), lambda bi, k: (bi, 0, k)),
                  per_kind(pe), per_kind(w1), per_kind(w1bd), full(w2k), full(w2vt), full(caux)],
        out_specs=[pl.BlockSpec((1, NSA_KV_HEADS, ncp, LANES), lambda bi, k: (bi, 0, 0, 0)),
                   pl.BlockSpec((1, gd, ncp), lambda bi, k: (bi, 0, 0))],
        out_shape=[jax.ShapeDtypeStruct((b, NSA_KV_HEADS, ncp, LANES), BF16),
                   jax.ShapeDtypeStruct((b, gd, ncp), BF16)],
        compiler_params=_params("parallel", "arbitrary"),
        name="compress",
    )(src, pe, w1, w1bd, w2k, w2vt, caux)


SOFTMAX_FLOOR = -1e29
TAKEN = -3e38
ATTN_BATCH = 6
BLOCKS_PER_CHUNK = KEY_CHUNK // SLC_BLOCK
AUX_COLS = 16
AUX_SLOPE = BLOCKS_PER_CHUNK
ONES_ROWS = 16
CMP_ROWS_STEP = 128
CMP_TILES = 2
CMP_AUX_SPLIT = 128


def _slope(g, r):
    if isinstance(g, int):
        return jnp.float32(ALIBI_SLOPES[g * NSA_GROUP + r])
    s = jnp.float32(ALIBI_SLOPES[r])
    for gi in range(1, NSA_KV_HEADS):
        s = jnp.where(g == gi, jnp.float32(ALIBI_SLOPES[gi * NSA_GROUP + r]), s)
    return s


def _nsa_cmp_kernel(qt_ref, kc_ref, vct_ref, poolt_ref, pair_ref, earlier_ref, oct_ref, selt_ref, list_ref,
                    *, n_sel):
    first_tile = pl.program_id(1) * CMP_TILES
    qt = selt_ref.shape[4]
    ncp = kc_ref.shape[2]
    n_slc = poolt_ref.shape[0]
    gq = qt_ref.shape[4]
    units = [(g, j) for j in range(CMP_TILES) for g in range(NSA_KV_HEADS)]
    times = [(first_tile + j) * qt + _iota((1, qt), 1) for j in range(CMP_TILES)]
    aux_row = _iota((AUX_COLS, gq), 0)
    col_head = _iota((1, gq), 1) // qt
    pad_rows = jnp.zeros((kc_ref.shape[3] - qt_ref.shape[3] - AUX_COLS, gq), BF16)

    def weights(g, j):
        slope_cols = jnp.zeros((1, gq), F32)
        for r in range(NSA_GROUP):
            slope_cols = jnp.where(col_head == r, _slope(g, r), slope_cols)
        aux = jnp.where(aux_row == 0, slope_cols * (CMP_STRIDE * CMP_AUX_SPLIT),
                        jnp.where(aux_row == 1, slope_cols * CMP_STRIDE, 0.0))
        return jnp.concatenate([qt_ref[0, g, j], aux.astype(BF16), pad_rows], axis=0)

    def importance(g, j, nr, nb):
        t = times[j]
        s = _dot(kc_ref[0, g, :nr, :], weights(g, j))
        edge = min(nr, 2 * CMP_ROWS_STEP)
        cmp_end = ((nr - edge) + _iota((edge, 1), 0)) * CMP_STRIDE + (CMP_LEN - 1)
        visible = cmp_end <= t
        vct = vct_ref[0, g * HEAD_DIM:(g + 1) * HEAD_DIM, :nr]
        psum = jnp.zeros((nr, qt), F32)
        for r in range(NSA_GROUP):
            cols = slice(r * qt, (r + 1) * qt)
            sr = s[:, cols]
            tail = jnp.where(visible, sr[nr - edge:], NEG_INF)
            sr = tail if edge == nr else jnp.concatenate([sr[:nr - edge], tail], axis=0)
            m = jnp.maximum(jnp.max(sr, axis=0, keepdims=True), SOFTMAX_FLOOR)
            e = jnp.exp(sr - m)
            l = jnp.sum(e, axis=0, keepdims=True)
            inv = jnp.where(l > 0.0, 1.0 / l, 0.0)
            oct_ref[0, g, j, :, cols] = (_dot(vct, e.astype(BF16)) * inv).astype(BF16)
            psum = psum + e * inv
        return _dot(poolt_ref[:nb, :nr], psum.astype(BF16))

    def visible_prefix(nr):
        nb = min(n_slc, nr * CMP_STRIDE // SLC_BLOCK)
        imp = jnp.concatenate([importance(g, j, nr, nb) for g, j in units], axis=1)
        blk = _iota((nb, 1), 0)
        cur = jnp.concatenate([times[j] // SLC_BLOCK for _, j in units], axis=1)
        forced = (blk == 0) | (blk == cur) | (blk == cur - 1)
        score = jnp.where(forced, TAKEN, jnp.where(blk <= cur, imp, NEG_INF))
        n_forced = 1 + jnp.where(cur >= 1, 1, 0) + jnp.where(cur >= 2, 1, 0)
        blk_f = blk.astype(F32)

        def take_one(score, active):
            m = jnp.max(score, axis=0, keepdims=True)
            first = jnp.min(jnp.where(score == m, blk_f, F32(1e9)), axis=0, keepdims=True)
            hit = (blk_f == first) if active is None else ((blk_f == first) & active)
            return jnp.where(hit, TAKEN, score)

        common_rounds = max(n_sel - 3, 0)
        for _ in range(common_rounds):
            score = take_one(score, None)

        def early_rounds(score):
            for k in range(common_rounds, n_sel - 1):
                score = take_one(score, n_sel - n_forced > k)
            return score

        score = lax.cond(first_tile * qt < 2 * SLC_BLOCK, early_rounds, lambda sc: sc, score)
        for u, (g, j) in enumerate(units):
            sel_g = score[:, u * qt:(u + 1) * qt] == TAKEN
            selt_ref[0, g, j, :nb, :] = jnp.where(sel_g, 0.0, NEG_INF)
            if nb < n_slc:
                selt_ref[0, g, j, nb:, :] = jnp.full((n_slc - nb, qt), NEG_INF, F32)
            count = _dot_nt(jnp.ones((8, qt), BF16), jnp.where(sel_g, 1.0, 0.0).astype(BF16))
            used = jnp.where(count > 0.0, 1.0, 0.0).astype(BF16)
            chunk = _iota((8, LANES), 1)
            flagged = jnp.where((_dot(used, pair_ref[:nb, :]) > 0.0) & (chunk < first_tile + j), 1.0, 0.0).astype(BF16)
            place = _dot(flagged, earlier_ref[...])
            lands = (_iota((LANES, LANES), 0).astype(F32) == place[0:1, :]) & (flagged[0:1, :] > 0)
            listed = _dot_nt(chunk.astype(BF16), jnp.where(lands, 1.0, 0.0).astype(BF16))
            n_listed = _dot(flagged, jnp.ones((LANES, LANES), BF16))
            list_ref[0, g, j] = jnp.where(_iota((8, LANES), 0) < 4, listed, n_listed).astype(jnp.int32)

    step = min(CMP_ROWS_STEP, ncp)
    rows_needed = jnp.minimum(((first_tile + CMP_TILES) * qt - CMP_LEN) // CMP_STRIDE + 1, ncp)
    n_steps = (rows_needed + step - 1) // step
    for k in range(ncp // step):
        pl.when(n_steps == k + 1)(functools.partial(visible_prefix, (k + 1) * step))


def _nsa_cmp(q_t, kc, vct, poolt_m, pair_m, earlier_m, n_sel):
    b, ng, n_tiles, qrows, qcols = q_t.shape
    ncp = kc.shape[2]
    n_slc = poolt_m.shape[0]
    const = lambda shape: pl.BlockSpec(shape, lambda bi, i: (0,) * len(shape))
    tile5 = lambda rows, cols: pl.BlockSpec((1, ng, CMP_TILES, rows, cols), lambda bi, i: (bi, 0, i, 0, 0))
    return pl.pallas_call(
        functools.partial(_nsa_cmp_kernel, n_sel=n_sel),
        grid=(b, n_tiles // CMP_TILES),
        in_specs=[
            tile5(qrows, qcols),
            pl.BlockSpec((1, ng, ncp, kc.shape[3]), lambda bi, i: (bi, 0, 0, 0)),
            pl.BlockSpec((1, ng * HEAD_DIM, ncp), lambda bi, i: (bi, 0, 0)),
            const(poolt_m.shape), const(pair_m.shape), const(earlier_m.shape),
        ],
        out_specs=[tile5(HEAD_DIM, qcols), tile5(n_slc, Q_TILE), tile5(8, LANES)],
        out_shape=[
            jax.ShapeDtypeStruct((b, ng, n_tiles, HEAD_DIM, qcols), BF16),
            jax.ShapeDtypeStruct((b, ng, n_tiles, n_slc, Q_TILE), F32),
            jax.ShapeDtypeStruct((b, ng, n_tiles, 8, LANES), jnp.int32),
        ],
        compiler_params=_params("parallel", "parallel"),
        name="nsa_compressed",
    )(q_t, kc, vct, poolt_m, pair_m, earlier_m)


def _nsa_slc_kernel(lists_ref, counts_ref, qt_ref, ks_ref, vst_ref, kw_ref, vwt_ref, selt_ref, oct_ref, gt_ref,
                    out_ref, m_ref, l_ref, acc_ref, ow_ref, sa_ref, sb_ref, sw_ref):
    bi = pl.program_id(0)
    g = pl.program_id(1)
    i = pl.program_id(2)
    n_tiles = pl.num_programs(2)
    qt = out_ref.shape[1]
    q_rows = qt_ref[0, 0, 0]
    gq = q_rows.shape[1]
    start = i * qt
    lane_f = _iota((1, qt), 1).astype(F32)
    key_in_chunk = _iota((KEY_CHUNK, qt), 0)
    query_in_tile = _iota((KEY_CHUNK, qt), 1)

    aux_row = _iota((AUX_COLS, gq), 0)
    col_head = _iota((1, gq), 1) // qt
    slope_cols = jnp.zeros((1, gq), F32)
    for r in range(NSA_GROUP):
        slope_cols = jnp.where(col_head == r, _slope(g, r), slope_cols)
    aux_base = jnp.where(aux_row == AUX_SLOPE, slope_cols, 0.0)
    pad_rows = jnp.zeros((ks_ref.shape[3] - q_rows.shape[0] - AUX_COLS, gq), BF16)

    def scores_of(slots):
        return [_dot(k, jnp.concatenate([q_rows, aux.astype(BF16), pad_rows], axis=0))
                for k, aux, _, _, _ in slots]

    def values_of(slots):
        values = jnp.concatenate([v for _, _, v, _, _ in slots], axis=1)
        return jnp.concatenate([values, jnp.ones((ONES_ROWS, values.shape[1]), BF16)], axis=0)

    def softmax_step(slots, scores, v_cat, r, m_old):
        cols = slice(r * qt, (r + 1) * qt)
        srs, tops = [], []
        for j, (_, _, _, shift, mask) in enumerate(slots):
            sr = scores[j, :, cols] if hasattr(scores, "at") else scores[j][:, cols]
            sr = sr if mask is None else jnp.where(mask, sr, NEG_INF)
            srs.append(sr)
            tops.append(jnp.max(sr, axis=0, keepdims=True) + shift[r])
        m_new = functools.reduce(jnp.maximum, tops, m_old)
        ps = [jnp.exp((sr - (m_new - slot[3][r])).astype(BF16)) for slot, sr in zip(slots, srs)]
        weighted = _dot(v_cat, jnp.concatenate(ps, axis=0))
        return m_new, weighted[HEAD_DIM:HEAD_DIM + 1], weighted[:HEAD_DIM]

    def shifts(dist0, ok):
        rows = [-_slope(g, r) * (dist0 + lane_f) for r in range(NSA_GROUP)]
        return rows if ok is None else [jnp.where(ok, row, NEG_INF) for row in rows]

    tile_id = (bi * NSA_KV_HEADS + g) * n_tiles + i
    n_listed = counts_ref[tile_id]

    def selected_slot(c, ok, mask):
        at = pl.multiple_of(c * KEY_CHUNK, KEY_CHUNK)
        bias = selt_ref[0, 0, 0, pl.ds(c * BLOCKS_PER_CHUNK, BLOCKS_PER_CHUNK), :]
        aux = aux_base
        for blk in range(BLOCKS_PER_CHUNK):
            aux = jnp.where(aux_row == blk, jnp.concatenate([bias[blk:blk + 1]] * NSA_GROUP, axis=1), aux)
        return (ks_ref[0, 0, pl.ds(at, KEY_CHUNK), :], aux, vst_ref[0, 0, c],
                shifts((start - c * KEY_CHUNK).astype(F32), ok), mask)

    def listed_slot(idx):
        ok = idx < n_listed
        c = jnp.where(ok, lists_ref[tile_id * LANES + jnp.minimum(idx, jnp.maximum(n_listed - 1, 0))], 0)
        return selected_slot(c, ok, None)

    floor = jnp.full((1, qt), SOFTMAX_FLOOR, F32)

    n_back = WINDOW // KEY_CHUNK
    slots = []
    for j in range(n_back + 1):
        cs = start - WINDOW + j * KEY_CHUNK
        chunk = jnp.maximum(cs, 0) // KEY_CHUNK
        at = pl.multiple_of(chunk * KEY_CHUNK, KEY_CHUNK)
        mask = (query_in_tile < key_in_chunk) if j == 0 else (
            (key_in_chunk <= query_in_tile) if j == n_back else None)
        slots.append((kw_ref[0, 0, pl.ds(at, KEY_CHUNK), :], aux_base, vwt_ref[0, 0, chunk],
                      shifts(F32(WINDOW - j * KEY_CHUNK), cs >= 0), mask))
    window_slots = slots
    first_slots = [selected_slot(i, None, key_in_chunk <= query_in_tile)] + [
        listed_slot(j) for j in range(ATTN_BATCH - 1)]

    def listed_batch(n):
        return [listed_slot(ATTN_BATCH - 1 + n * ATTN_BATCH + j) for j in range(ATTN_BATCH)]

    def issue(slots, dst_ref):
        for j, s in enumerate(scores_of(slots)):
            dst_ref[j] = s

    issue(window_slots, sw_ref)
    issue(first_slots, sb_ref)
    v_cat = values_of(window_slots)
    for r in range(NSA_GROUP):
        _, total, weighted = softmax_step(window_slots, sw_ref, v_cat, r, floor)
        ow_ref[:, r * qt:(r + 1) * qt] = weighted * jnp.where(total > 0.0, 1.0 / total, 0.0)

    def consume(slots, src_ref):
        v_cat = values_of(slots)
        for r in range(NSA_GROUP):
            cols = slice(r * qt, (r + 1) * qt)
            m_old = m_ref[r]
            m_new, total, weighted = softmax_step(slots, src_ref, v_cat, r, m_old)
            alpha = jnp.exp(m_old - m_new)
            l_ref[r] = alpha * l_ref[r] + total
            acc_ref[:, cols] = alpha * acc_ref[:, cols] + weighted
            m_ref[r] = m_new

    issue(listed_batch(0), sa_ref)
    v_cat = values_of(first_slots)
    for r in range(NSA_GROUP):
        m_ref[r], l_ref[r], acc_ref[:, r * qt:(r + 1) * qt] = softmax_step(first_slots, sb_ref, v_cat, r, floor)

    def batch_pair(it, carry):
        first, second, third = (listed_batch(2 * it + n) for n in range(3))
        issue(second, sb_ref)
        consume(first, sa_ref)
        issue(third, sa_ref)
        consume(second, sb_ref)
        return carry

    n_rest = jnp.maximum(n_listed - (ATTN_BATCH - 1), 0)
    n_batches = (n_rest + ATTN_BATCH - 1) // ATTN_BATCH
    lax.fori_loop(0, n_batches // 2, batch_pair, 0)

    @pl.when(n_batches % 2 == 1)
    def _():
        consume(listed_batch(n_batches - 1), sa_ref)

    def finalize(r):
        l = l_ref[r]
        return acc_ref[:, r * qt:(r + 1) * qt] * jnp.where(l > 0.0, 1.0 / l, 0.0)

    outs = []
    for r in range(NSA_GROUP):
        cols = slice(r * qt, (r + 1) * qt)
        col = (g * NSA_GROUP + r) * NSA_BRANCHES
        gate = lambda br: jax.nn.sigmoid(gt_ref[0, pl.ds(col + br, 1), :])
        outs.append(gate(0) * oct_ref[0, 0, 0, :, cols].astype(F32) + gate(1) * finalize(r)
                    + gate(2) * ow_ref[:, cols])
    out_ref[0] = jnp.concatenate(outs, axis=0).T.astype(out_ref.dtype)


def _nsa_slc(lists, counts, q_t, ks, vst, kw, vwt, selt, oct, gates_t):
    b, _, t, kw_cols = ks.shape
    _, _, n_tiles, qrows, qcols = q_t.shape
    n_slc = selt.shape[3]
    n_chunks = vst.shape[2]
    gw = NSA_GROUP * HEAD_DIM
    once = dict(pipeline_mode=pl.Buffered(1))
    k_spec = pl.BlockSpec((1, 1, t, kw_cols), lambda bi, g, i, ls, ns: (bi, g, 0, 0), **once)
    vt_spec = pl.BlockSpec((1, 1, n_chunks, HEAD_DIM, KEY_CHUNK), lambda bi, g, i, ls, ns: (bi, g, 0, 0, 0), **once)
    tile5 = lambda rows, cols: pl.BlockSpec((1, 1, 1, rows, cols), lambda bi, g, i, ls, ns: (bi, g, i, 0, 0))
    grid_spec = pltpu.PrefetchScalarGridSpec(
        num_scalar_prefetch=2,
        grid=(b, NSA_KV_HEADS, n_tiles),
        in_specs=[
            tile5(qrows, qcols),
            k_spec, vt_spec, k_spec, vt_spec,
            tile5(n_slc, Q_TILE),
            tile5(HEAD_DIM, qcols),
            pl.BlockSpec((1, GATE_PAD, Q_TILE), lambda bi, g, i, ls, ns: (bi, 0, i)),
        ],
        out_specs=pl.BlockSpec((1, Q_TILE, gw), lambda bi, g, i, ls, ns: (bi, i, g)),
        scratch_shapes=[pltpu.VMEM((NSA_GROUP, 1, Q_TILE), F32), pltpu.VMEM((NSA_GROUP, 1, Q_TILE), F32),
                        pltpu.VMEM((HEAD_DIM, qcols), F32), pltpu.VMEM((HEAD_DIM, qcols), F32),
                        pltpu.VMEM((ATTN_BATCH, KEY_CHUNK, qcols), F32),
                        pltpu.VMEM((ATTN_BATCH, KEY_CHUNK, qcols), F32),
                        pltpu.VMEM((WINDOW // KEY_CHUNK + 1, KEY_CHUNK, qcols), F32)],
    )
    return pl.pallas_call(
        _nsa_slc_kernel,
        grid_spec=grid_spec,
        out_shape=jax.ShapeDtypeStruct((b, t, NSA_HEADS * HEAD_DIM), BF16),
        compiler_params=_params("parallel", "parallel", "parallel"),
        name="nsa_selected_window",
    )(lists, counts, q_t, ks, vst, kw, vwt, selt, oct, gates_t)


POOL_HALO = 16
CONV_HALO = 8


def _merge_kernel(h_ref, g_ref, nsa_ref, pool_ref, pool_halo_ref, conv_ref, conv_halo_ref,
                  wmg_ref, wnsa_ref, pool_bd_ref, pool_scale_ref, wpool_ref, convw_ref, wconv_ref, wo_ref,
                  g2_ref, wr_ref, br_ref, before_ref,
                  out_ref, xn_ref, comb_ref, count_ref, pool_ext, conv_ext, *, seq_len):
    i = pl.program_id(0)
    tm, d = h_ref.shape
    cw = pool_ref.shape[1]
    pos0 = (i * tm) % seq_len
    keep_halo = jnp.where(pos0 == 0, 0.0, 1.0)
    pos = pos0 + _iota((tm, 1), 0)

    u = pool_ref[...]
    pool_ext[0:POOL_HALO, :] = pool_halo_ref[...] * keep_halo
    pool_ext[POOL_HALO:, :] = u
    lane_group = _iota((1, cw), 1) // (cw // len(POOL_WINDOWS))
    total = u
    mean = jnp.zeros_like(u)
    done = 1
    for gi, win in enumerate(POOL_WINDOWS):
        for k in range(done, win):
            total = total + pool_ext[POOL_HALO - k:POOL_HALO - k + tm, :]
        done = win
        cnt = jnp.minimum(pos + 1, win).astype(F32)
        mean = jnp.where(lane_group == gi, total / cnt, mean)
    pooled = (mean - u).astype(BF16)
    mixed = _dot(pooled, pool_bd_ref[...]) * pool_scale_ref[...]
    y_pool = _dot(mixed.astype(BF16), wpool_ref[...])

    ch = conv_ref[:, 0:cw]
    cb = conv_ref[:, cw:2 * cw]
    cc = conv_ref[:, 2 * cw:3 * cw]
    conv_ext[0:CONV_HALO, :] = conv_halo_ref[:, 0:cw] * conv_halo_ref[:, 2 * cw:3 * cw] * keep_halo
    conv_ext[CONV_HALO:, :] = cc * ch
    y = jnp.zeros((tm, cw), F32)
    for k in range(CONV_K):
        off = CONV_HALO - (CONV_K - 1) + k
        y = y + convw_ref[k:k + 1, :] * conv_ext[off:off + tm, :]
    y_conv = _dot((cb * y).astype(BF16), wconv_ref[...])

    y_nsa = _dot(nsa_ref[...], wnsa_ref[...])

    h = h_ref[...]
    xn = _rms_norm(h, g_ref[...]).astype(BF16)
    merged = jnp.zeros((tm, d), F32)
    for br, y_br in enumerate((y_nsa, y_pool, y_conv)):
        mg = jax.nn.sigmoid(_dot(xn, wmg_ref[:, br * d:(br + 1) * d]))
        merged = merged + mg * y_br
    h_new = h + _dot(merged.astype(BF16), wo_ref[...])
    out_ref[...] = h_new
    _route_tile(h_new, g2_ref, wr_ref, br_ref, before_ref, xn_ref, comb_ref, count_ref)


def _merge(h, g, nsa, pool_u, conv, wmg, wnsa, pool_bd, pool_scale, wpool, convw, wconv, wo,
           g2, wr, br, before, seq_len):
    n, d = h.shape
    tm = TOKEN_TILE
    cw = pool_u.shape[1]
    row = lambda width: pl.BlockSpec((tm, width), lambda i: (i, 0))
    full = lambda a: pl.BlockSpec(a.shape, lambda i: (0,) * a.ndim)
    halo = lambda rows, width: pl.BlockSpec(
        (rows, width), lambda i: (jnp.maximum(i * (tm // rows) - 1, 0), 0))
    return pl.pallas_call(
        functools.partial(_merge_kernel, seq_len=seq_len),
        grid=(n // tm,),
        in_specs=[row(d), full(g), row(nsa.shape[1]), row(cw), halo(POOL_HALO, cw),
                  row(conv.shape[1]), halo(CONV_HALO, conv.shape[1]),
                  full(wmg), full(wnsa), full(pool_bd), full(pool_scale), full(wpool), full(convw),
                  full(wconv), full(wo), full(g2), full(wr), full(br), full(before)],
        out_specs=[row(d), row(d + ROUTER_PAD), row(ROUTER_PAD), pl.BlockSpec((8, ROUTER_PAD), lambda i: (0, 0))],
        out_shape=[jax.ShapeDtypeStruct((n, d), F32), jax.ShapeDtypeStruct((n, d + ROUTER_PAD), BF16),
                   jax.ShapeDtypeStruct((n, ROUTER_PAD), F32), jax.ShapeDtypeStruct((8, ROUTER_PAD), F32)],
        scratch_shapes=[pltpu.VMEM((tm + POOL_HALO, cw), F32), pltpu.VMEM((tm + CONV_HALO, cw), F32)],
        compiler_params=_params("arbitrary"),
        name="merge",
    )(h, g, nsa, pool_u, pool_u, conv, conv, wmg, wnsa, pool_bd, pool_scale, wpool, convw, wconv, wo,
      g2, wr, br, before)


def _route(logits):
    lane = _iota(logits.shape, 1)
    lane_f = lane.astype(F32)
    big = F32(1e9)
    is_group = lane < N_EXPERT_GROUPS
    gl = jnp.where(is_group, logits, NEG_INF)
    g_max = jnp.max(gl, axis=1, keepdims=True)
    g_sel = jnp.min(jnp.where(gl == g_max, lane_f, big), axis=1, keepdims=True)
    g_prob = 1.0 / jnp.sum(jnp.where(is_group, jnp.exp(gl - g_max), 0.0), axis=1, keepdims=True)
    lo = N_EXPERT_GROUPS + EXPERTS_PER_GROUP * g_sel
    in_group = (lane_f >= lo) & (lane_f < lo + EXPERTS_PER_GROUP)
    el = jnp.where(in_group, logits, NEG_INF)
    v1 = jnp.max(el, axis=1, keepdims=True)
    i1 = jnp.min(jnp.where((el == v1) & in_group, lane_f, big), axis=1, keepdims=True)
    el2 = jnp.where(lane_f == i1, NEG_INF, el)
    rest = in_group & (lane_f != i1)
    v2 = jnp.max(el2, axis=1, keepdims=True)
    i2 = jnp.min(jnp.where((el2 == v2) & rest, lane_f, big), axis=1, keepdims=True)
    e2 = jnp.exp(v2 - v1)
    w1 = g_prob / (1.0 + e2)
    w2 = g_prob * e2 / (1.0 + e2)
    return jnp.where(lane_f == i1, w1, 0.0) + jnp.where(lane_f == i2, w2, 0.0), g_sel


GROUP_LANE = N_EXPERT_GROUPS + N_EXPERTS


RANK_LANE = GROUP_LANE + 1


def _route_tile(h, g_ref, wr_ref, br_ref, before_ref, xn_ref, comb_ref, count_ref):
    @pl.when(pl.program_id(0) == 0)
    def _():
        count_ref[...] = jnp.zeros(count_ref.shape, F32)

    xn = _rms_norm(h, g_ref[...])
    xn_hi = xn.astype(BF16)
    xn_lo = (xn - xn_hi.astype(F32)).astype(BF16)
    logits = (_dot(xn_hi, wr_ref[0]) + (_dot(xn_hi, wr_ref[1]) + _dot(xn_lo, wr_ref[0]))) + br_ref[...]
    comb, g_sel = _route(logits)
    lane = _iota(comb.shape, 1)
    chose = jnp.where(lane.astype(F32) == g_sel, 1.0, 0.0)
    earlier = _dot(before_ref[...], chose.astype(BF16)) + count_ref[0:1, :]
    rank = jnp.sum(chose * earlier, axis=1, keepdims=True)
    count_ref[0:1, :] = count_ref[0:1, :] + jnp.sum(chose, axis=0, keepdims=True)
    comb_ref[...] = jnp.where(lane == GROUP_LANE, g_sel, jnp.where(lane == RANK_LANE, rank, comb))
    first = N_EXPERT_GROUPS + EXPERTS_PER_GROUP * g_sel
    local = jnp.zeros(comb.shape, F32)
    for e in range(EXPERTS_PER_GROUP):
        c_e = jnp.sum(jnp.where(lane.astype(F32) == first + e, comb, 0.0), axis=1, keepdims=True)
        local = jnp.where((lane == e) | (lane == EXPERTS_PER_GROUP + e), c_e, local)
    local_hi = local.astype(BF16)
    d = xn_hi.shape[1]
    xn_ref[:, :d] = xn_hi
    xn_ref[:, d:] = jnp.where(lane < EXPERTS_PER_GROUP, local_hi, (local - local_hi.astype(F32)).astype(BF16))


def _experts_kernel(tile_group_ref, n_active_ref, x_ref, w1_ref, w3_ref, w2_ref, out_ref, acc_ref):
    i = pl.program_id(0)

    @pl.when(i < n_active_ref[0])
    def _():
        d = out_ref.shape[1]
        x = x_ref[:, :d]
        comb = x_ref[:, d:].astype(F32)
        lane = _iota(comb.shape, 1)
        for e in range(EXPERTS_PER_GROUP):
            c_e = jnp.sum(jnp.where((lane == e) | (lane == EXPERTS_PER_GROUP + e), comb, 0.0),
                          axis=1, keepdims=True)
            a = (jax.nn.silu(_dot(x, w1_ref[0, 0, e].astype(BF16)))
                 * _dot(x, w3_ref[0, 0, e].astype(BF16))) * c_e
            y = _dot(a.astype(BF16), w2_ref[0, 0, e].astype(BF16))
            if e == 0:
                acc_ref[...] = y
            else:
                acc_ref[...] += y
        out_ref[...] = acc_ref[...].astype(out_ref.dtype)

    @pl.when(i >= n_active_ref[0])
    def _():
        out_ref[...] = jnp.zeros(out_ref.shape, out_ref.dtype)


def _experts(tile_group, n_active, x_sorted, w1, w3, w2, layer):
    ns = x_sorted.shape[0]
    d = w1.shape[3]
    tm = MOE_SORT_TILE
    group_w = lambda w: pl.BlockSpec((1, 1) + w.shape[2:], lambda i, tg, na: (layer, tg[i], 0, 0, 0),
                                     pipeline_mode=pl.Buffered(1))
    grid_spec = pltpu.PrefetchScalarGridSpec(
        num_scalar_prefetch=2,
        grid=(ns // tm,),
        in_specs=[
            pl.BlockSpec((tm, x_sorted.shape[1]), lambda i, tg, na: (i, 0)),
            group_w(w1), group_w(w3), group_w(w2),
        ],
        out_specs=pl.BlockSpec((tm, d), lambda i, tg, na: (i, 0)),
        scratch_shapes=[pltpu.VMEM((tm, d), F32)],
    )
    return pl.pallas_call(
        _experts_kernel,
        grid_spec=grid_spec,
        out_shape=jax.ShapeDtypeStruct((ns, d), BF16),
        compiler_params=_params("arbitrary"),
        name="experts",
    )(tile_group, n_active, x_sorted, w1, w3, w2)


def _residual_kernel(h_ref, y_ref, gf_ref, out_ref, *, final_norm):
    out = h_ref[...] + y_ref[...].astype(F32)
    out_ref[...] = _rms_norm(out, gf_ref[...]) if final_norm else out


def _residual(h, y, gf, final_norm):
    n, d = h.shape
    tm = MOE_TILE
    row = pl.BlockSpec((tm, d), lambda i: (i, 0))
    return pl.pallas_call(
        functools.partial(_residual_kernel, final_norm=final_norm),
        grid=(n // tm,),
        in_specs=[row, row, pl.BlockSpec(gf.shape, lambda i: (0, 0))],
        out_specs=row,
        out_shape=jax.ShapeDtypeStruct((n, d), F32),
        compiler_params=_params("parallel"),
        name="residual",
    )(h, y, gf)


def _group_sort_plan(group_id, rank, counts, tile):
    n = group_id.shape[0]
    n_slots = n + N_EXPERT_GROUPS * tile
    padded = (counts + tile - 1) // tile * tile
    ends = jnp.cumsum(padded)
    slot = (ends - padded)[group_id] + rank
    source = jnp.zeros((n_slots,), jnp.int32).at[slot].set(jnp.arange(n, dtype=jnp.int32))
    tile_start = jnp.arange(n_slots // tile, dtype=jnp.int32) * tile
    tile_group = jnp.minimum(jnp.searchsorted(ends, tile_start, side="right"), N_EXPERT_GROUPS - 1)
    return slot, source, tile_group.astype(jnp.int32), (ends[-1:] // tile).astype(jnp.int32)


def _moe(h, xn, comb, counts, w1, w3, w2, layer, gf, final_norm):
    as_int = lambda a: a.astype(jnp.int32)
    slot, source, tile_group, n_active = _group_sort_plan(
        as_int(comb[:, GROUP_LANE]), as_int(comb[:, RANK_LANE]), as_int(counts[0, :N_EXPERT_GROUPS]),
        MOE_SORT_TILE)
    grouped = lambda w: w.reshape((w.shape[0], N_EXPERT_GROUPS, EXPERTS_PER_GROUP) + w.shape[2:])
    rows = lambda a, idx: jnp.take(a, idx, axis=0, mode="clip")
    y_sorted = _experts(tile_group, n_active, rows(xn, source), grouped(w1), grouped(w3), grouped(w2), layer)
    return _residual(h, rows(y_sorted, slot), gf, final_norm)


def _selection_constants(seq_len):
    ncp = seq_len // CMP_STRIDE
    n_slc = seq_len // SLC_BLOCK
    ratio = SLC_BLOCK // CMP_STRIDE
    lead = CMP_LEN // CMP_STRIDE - 1
    c = np.arange(ncp)[:, None]
    j = np.arange(n_slc)[None, :]
    pool_m = ((c >= ratio * j - lead) & (c < ratio * j + ratio)).astype(np.float32)
    blocks_per_chunk = KEY_CHUNK // SLC_BLOCK
    n_chunks = seq_len // KEY_CHUNK
    pair_m = np.zeros((n_slc, LANES * ((n_chunks + LANES - 1) // LANES)), np.float32)
    pair_m[np.arange(n_slc), np.arange(n_slc) // blocks_per_chunk] = 1.0
    earlier_m = np.triu(np.ones((LANES, LANES), np.float32), 1)
    key_aux = np.zeros((TOKEN_TILE, LANES), np.float32)
    in_chunk = np.arange(TOKEN_TILE) % KEY_CHUNK
    key_aux[np.arange(TOKEN_TILE), HEAD_DIM + in_chunk // SLC_BLOCK] = 1.0
    key_aux[:, HEAD_DIM + AUX_SLOPE] = in_chunk
    cmp_aux = np.zeros((ncp, LANES), np.float32)
    cmp_aux[:, HEAD_DIM] = np.arange(ncp) // CMP_AUX_SPLIT
    cmp_aux[:, HEAD_DIM + 1] = np.arange(ncp) % CMP_AUX_SPLIT
    as_bf16 = lambda a: jnp.asarray(a, BF16)
    return as_bf16(pool_m.T), as_bf16(pair_m), as_bf16(earlier_m), jnp.asarray(key_aux), jnp.asarray(cmp_aux)


def kernel(x, norm1_g, w_in, cmp_pe, cmp_w1, cmp_w2, w_nsa_proj, pool_w, pool_scale, w_pool_proj, conv_w,
           w_conv_proj, w_o, norm2_g, router_group_w, router_group_b, router_expert_w, router_expert_b,
           expert_w1, expert_w3, expert_w2, final_norm_g):
    b, t, d = x.shape
    n = b * t
    depth = w_in.shape[0]
    dq = NSA_HEADS * HEAD_DIM
    dkv = 6 * NSA_KV_HEADS * HEAD_DIM
    dgate = NSA_HEADS * NSA_BRANCHES
    cw = d // 4
    assert t % TOKEN_TILE == 0 and n % MOE_TILE == 0 and t % (Q_TILE * CMP_TILES) == 0
    n_slc = t // SLC_BLOCK
    n_sel = min(SLC_TOPN, n_slc)
    n_chunks16 = t // CMP_STRIDE
    kvw = NSA_KV_HEADS * HEAD_DIM
    poolt_m, pair_m, earlier_m, key_aux, cmp_aux = _selection_constants(t)
    assert Q_TILE == KEY_CHUNK and AUX_SLOPE < AUX_COLS and KEY_CHUNK <= 256
    assert n_chunks16 <= 256 * CMP_AUX_SPLIT and n_chunks16 % min(CMP_ROWS_STEP, n_chunks16) == 0
    assert pair_m.shape[1] == LANES

    before = jnp.asarray(np.tril(np.ones((TOKEN_TILE, TOKEN_TILE), np.float32), -1), BF16)
    h = x.reshape(n, d)
    for l in range(depth):
        wl = w_in[l]
        o_gate = dq + dkv
        o_pool = o_gate + dgate
        o_merge = o_pool + cw + 3 * cw
        kv_cols = lambda kind: wl[:, dq + kind * kvw:dq + (kind + 1) * kvw]
        wq_t = (wl[:, :dq] * (HEAD_DIM ** -0.5)).T.astype(BF16)
        wv_t = jnp.concatenate([kv_cols(3), kv_cols(5)], axis=1).T.astype(BF16)
        wg_t = jnp.pad(wl[:, o_gate:o_pool], ((0, 0), (0, GATE_PAD - dgate))).T.astype(BF16)
        no_aux = jnp.zeros((d, LANES - HEAD_DIM), F32)
        wk = jnp.concatenate([piece for kind in (2, 4) for gi in range(NSA_KV_HEADS)
                              for piece in (kv_cols(kind)[:, gi * HEAD_DIM:(gi + 1) * HEAD_DIM], no_aux)],
                             axis=1).astype(BF16)
        wn = jnp.concatenate([kv_cols(0), kv_cols(1), wl[:, o_pool:o_merge]], axis=1).astype(BF16)
        wmg = wl[:, o_merge:].astype(BF16)
        pool_bd = jax.scipy.linalg.block_diag(*[pool_w[l, gi] for gi in range(pool_w.shape[1])]).astype(BF16)
        convw = jnp.pad(conv_w[l], ((0, 8 - CONV_K), (0, 0)))
        wr = jnp.pad(jnp.concatenate([router_group_w[l], router_expert_w[l]], axis=1),
                     ((0, 0), (0, ROUTER_PAD - N_EXPERT_GROUPS - N_EXPERTS)))
        wr_hi = wr.astype(BF16)
        wr = jnp.stack([wr_hi, (wr - wr_hi.astype(F32)).astype(BF16)])
        br = jnp.pad(jnp.concatenate([router_group_b[l], router_expert_b[l]]),
                     (0, ROUTER_PAD - N_EXPERT_GROUPS - N_EXPERTS))[None, :]
        pe = jnp.broadcast_to(cmp_pe[l].reshape(2, 1, CMP_LEN * HEAD_DIM), (2, 8, CMP_LEN * HEAD_DIM)).astype(BF16)
        halves = CMP_LEN // CMP_STRIDE
        w1_bd = jnp.einsum("khldc,gq->khlgdqc",
                           cmp_w1[l].reshape(2, halves, CMP_STRIDE, HEAD_DIM, CMP_HIDDEN),
                           jnp.eye(NSA_KV_HEADS, dtype=F32))
        w1_bd = w1_bd.reshape(2, halves, CMP_STRIDE * kvw, NSA_KV_HEADS * CMP_HIDDEN).astype(BF16)
        w2_k = jnp.pad(cmp_w2[l, 0], ((0, 0), (0, LANES - HEAD_DIM))).astype(BF16)
        w2_vt = cmp_w2[l, 1].T.astype(BF16)

        q_t, vst, vwt, gates_t, ks, kw, cmp_src, pool_u, conv = _inproj(
            h, norm1_g[l][None, :], wq_t, wv_t, wg_t, wk, wn, key_aux, b)
        kc_aux, vc_t = _compress(cmp_src, pe, cmp_w1[l].astype(BF16), w1_bd, w2_k, w2_vt, cmp_aux)
        oc_t, sel_t, listed = _nsa_cmp(q_t, kc_aux, vc_t, poolt_m, pair_m, earlier_m, n_sel)
        nsa = _nsa_slc(listed[:, :, :, 0, :].reshape(-1), listed[:, :, :, 4, 0].reshape(-1),
                       q_t, ks, vst, kw, vwt, sel_t, oc_t, gates_t)
        h, xn, comb, counts = _merge(
            h, norm1_g[l][None, :], nsa.reshape(n, dq), pool_u, conv, wmg,
            w_nsa_proj[l].astype(BF16), pool_bd, pool_scale[l][None, :], w_pool_proj[l].astype(BF16),
            convw, w_conv_proj[l].astype(BF16), w_o[l].astype(BF16), norm2_g[l][None, :], wr, br, before, t)
        h = _moe(h, xn, comb, counts, expert_w1, expert_w3, expert_w2, l,
                 final_norm_g[None, :], final_norm=(l == depth - 1))
    return h.reshape(b, t, d)
```

```python
import functools

import jax
import jax.numpy as jnp
import numpy as np
from jax import lax
from jax.experimental import pallas as pl
from jax.experimental.pallas import tpu as pltpu

F32 = jnp.float32
BF16 = jnp.bfloat16

HEAD_DIM = 64
NSA_HEADS = 8
NSA_KV_HEADS = 2
NSA_GROUP = NSA_HEADS // NSA_KV_HEADS
CMP_LEN = 32
CMP_STRIDE = 16
CMP_HIDDEN = 4 * HEAD_DIM
SLC_BLOCK = 64
SLC_TOPN = 16
WINDOW = 512
NSA_BRANCHES = 3
POOL_WINDOWS = (2, 4, 8, 16)
CONV_K = 3
N_EXPERT_GROUPS = 4
EXPERTS_PER_GROUP = 8
N_EXPERTS = N_EXPERT_GROUPS * EXPERTS_PER_GROUP
RMS_EPS = 1e-6
NEG_INF = -1e30
ALIBI_SLOPES = tuple(float(2.0 ** (-8.0 * (h + 1) / NSA_HEADS)) for h in range(NSA_HEADS))

LANES = 128
VMEM_LIMIT = 56 * 1024 * 1024
TOKEN_TILE = 512
MOE_TILE = 1024
MOE_SORT_TILE = 512
Q_TILE = 128
KEY_CHUNK = 128
GATE_PAD = LANES
ROUTER_PAD = LANES


def _params(*semantics):
    return pltpu.CompilerParams(dimension_semantics=semantics, vmem_limit_bytes=VMEM_LIMIT)


def _dot(a, b):
    return jnp.dot(a, b, preferred_element_type=F32)


def _dot_nt(a, b):
    return lax.dot_general(a, b, (((1,), (1,)), ((), ())), preferred_element_type=F32)


def _rms_norm(x, g):
    y = x * lax.rsqrt(jnp.mean(x * x, axis=-1, keepdims=True) + RMS_EPS)
    return y * g


def _iota(shape, dim):
    return lax.broadcasted_iota(jnp.int32, shape, dim)


def _inproj_kernel(x_ref, g_ref, wq_ref, wv_ref, wg_ref, wk_ref, wn_ref, kaux_ref,
                   q_ref, vs_ref, vw_ref, gate_ref, ks_ref, kw_ref, cmp_ref, pool_ref, conv_ref):
    xn = _rms_norm(x_ref[...], g_ref[...]).astype(BF16)
    sub_tiles = x_ref.shape[0] // Q_TILE
    q_t = _dot_nt(wq_ref[...], xn)
    for g in range(NSA_KV_HEADS):
        for j in range(sub_tiles):
            for r in range(NSA_GROUP):
                head = g * NSA_GROUP + r
                q_ref[0, g, j, :, r * Q_TILE:(r + 1) * Q_TILE] = q_t[
                    head * HEAD_DIM:(head + 1) * HEAD_DIM, j * Q_TILE:(j + 1) * Q_TILE].astype(BF16)
    v_t = _dot_nt(wv_ref[...], xn)
    k = _dot(xn, wk_ref[...])
    for branch, (v_ref, k_ref) in enumerate(((vs_ref, ks_ref), (vw_ref, kw_ref))):
        for g in range(NSA_KV_HEADS):
            slab = branch * NSA_KV_HEADS + g
            for j in range(sub_tiles):
                v_ref[0, g, j] = v_t[slab * HEAD_DIM:(slab + 1) * HEAD_DIM,
                                     j * KEY_CHUNK:(j + 1) * KEY_CHUNK].astype(BF16)
            k_ref[0, g] = (k[:, slab * LANES:(slab + 1) * LANES] + kaux_ref[...]).astype(BF16)
    gate_ref[0] = _dot_nt(wg_ref[...], xn)
    col = 0
    for ref in (cmp_ref, pool_ref, conv_ref):
        width = ref.shape[-1]
        ref[...] = _dot(xn, wn_ref[:, col:col + width]).reshape(ref.shape)
        col += width


def _inproj(h, g, wq_t, wv_t, wg_t, wk, wn, kaux, batch):
    n, d = h.shape
    t = n // batch
    tm = TOKEN_TILE
    steps = t // tm
    sub = tm // Q_TILE
    cw = d // 4
    gq = NSA_GROUP * Q_TILE
    full = lambda a: pl.BlockSpec(a.shape, lambda i: (0,) * a.ndim)
    row = lambda width: pl.BlockSpec((tm, width), lambda i: (i, 0))
    tiles = lambda rows, cols: pl.BlockSpec((1, NSA_KV_HEADS, sub, rows, cols),
                                            lambda i: (i // steps, 0, i % steps, 0, 0))
    keys = pl.BlockSpec((1, NSA_KV_HEADS, tm, LANES), lambda i: (i // steps, 0, i % steps, 0))
    sds = jax.ShapeDtypeStruct
    v_shape = sds((batch, NSA_KV_HEADS, t // KEY_CHUNK, HEAD_DIM, KEY_CHUNK), BF16)
    k_shape = sds((batch, NSA_KV_HEADS, t, LANES), BF16)
    return pl.pallas_call(
        _inproj_kernel,
        grid=(n // tm,),
        in_specs=[row(d), full(g), full(wq_t), full(wv_t), full(wg_t), full(wk), full(wn), full(kaux)],
        out_specs=[tiles(HEAD_DIM, gq), tiles(HEAD_DIM, KEY_CHUNK), tiles(HEAD_DIM, KEY_CHUNK),
                   pl.BlockSpec((1, GATE_PAD, tm), lambda i: (i // steps, 0, i % steps)),
                   keys, keys,
                   pl.BlockSpec((1, tm, 2 * NSA_KV_HEADS * HEAD_DIM), lambda i: (i // steps, i % steps, 0)),
                   row(cw), row(3 * cw)],
        out_shape=[sds((batch, NSA_KV_HEADS, t // Q_TILE, HEAD_DIM, gq), BF16), v_shape, v_shape,
                   sds((batch, GATE_PAD, t), F32), k_shape, k_shape,
                   sds((batch, t, 2 * NSA_KV_HEADS * HEAD_DIM), F32), sds((n, cw), F32), sds((n, 3 * cw), F32)],
        compiler_params=_params("parallel"),
        name="inproj",
    )(h, g, wq_t, wv_t, wg_t, wk, wn, kaux)


def _gelu_tanh(x):
    return 0.5 * x * (1.0 + jnp.tanh(0.7978845608028654 * (x + 0.044715 * x * x * x)))


def _compress_kernel(src_ref, pe_ref, w1_ref, w1bd_ref, w2k_ref, w2vt_ref, caux_ref, kc_ref, vct_ref):
    kind = pl.program_id(1)
    ncp = kc_ref.shape[2]
    hidden = w1_ref.shape[2]
    pieces = [src_ref[0, pl.ds(l, ncp, stride=CMP_STRIDE), :].astype(BF16) for l in range(CMP_STRIDE)]
    chunk = jnp.concatenate(pieces, axis=1)
    first = _dot(chunk, w1bd_ref[0, 0])
    second = _dot(chunk, w1bd_ref[0, 1])
    bias = _dot(pe_ref[0], w1_ref[0])[0:1, :]
    hid = first + pltpu.roll(second, ncp - 1, 0) + jnp.concatenate([bias] * NSA_KV_HEADS, axis=1)
    row = _iota((ncp, 1), 0)
    act = jnp.where(row < ncp - 1, _gelu_tanh(hid), 0.0).astype(BF16)
    for g in range(NSA_KV_HEADS):
        act_g = act[:, g * hidden:(g + 1) * hidden]

        @pl.when(kind == 0)
        def _():
            kc_ref[0, g] = (_dot(act_g, w2k_ref[...]) + caux_ref[...]).astype(BF16)

        @pl.when(kind == 1)
        def _():
            vct_ref[0, g * HEAD_DIM:(g + 1) * HEAD_DIM, :] = _dot_nt(w2vt_ref[...], act_g).astype(BF16)


def _compress(src, pe, w1, w1bd, w2k, w2vt, caux):
    b, t, _ = src.shape
    ncp = t // CMP_STRIDE
    gd = NSA_KV_HEADS * HEAD_DIM
    full = lambda a: pl.BlockSpec(a.shape, lambda bi, k: (0,) * a.ndim)
    per_kind = lambda a: pl.BlockSpec((1,) + a.shape[1:], lambda bi, k: (k,) + (0,) * (a.ndim - 1))
    return pl.pallas_call(
        _compress_kernel,
        grid=(b, 2),
        in_specs=[pl.BlockSpec((1, t, gd), lambda bi, k: (bi, 0, k)),
                  per_kind(pe), per_kind(w1), per_kind(w1bd), full(w2k), full(w2vt), full(caux)],
        out_specs=[pl.BlockSpec((1, NSA_KV_HEADS, ncp, LANES), lambda bi, k: (bi, 0, 0, 0)),
                   pl.BlockSpec((1, gd, ncp), lambda bi, k: (bi, 0, 0))],
        out_shape=[jax.ShapeDtypeStruct((b, NSA_KV_HEADS, ncp, LANES), BF16),
                   jax.ShapeDtypeStruct((b, gd, ncp), BF16)],
        compiler_params=_params("parallel", "arbitrary"),
        name="compress",
    )(src, pe, w1, w1bd, w2k, w2vt, caux)


SOFTMAX_FLOOR = -1e29
TAKEN = -3e38
ATTN_BATCH = 5
BLOCKS_PER_CHUNK = KEY_CHUNK // SLC_BLOCK
AUX_COLS = 16
AUX_SLOPE = BLOCKS_PER_CHUNK
ONES_ROWS = 16
CMP_ROWS_STEP = 128
CMP_TILES = 2
CMP_AUX_SPLIT = 128


def _slope(g, r):
    if isinstance(g, int):
        return jnp.float32(ALIBI_SLOPES[g * NSA_GROUP + r])
    s = jnp.float32(ALIBI_SLOPES[r])
    for gi in range(1, NSA_KV_HEADS):
        s = jnp.where(g == gi, jnp.float32(ALIBI_SLOPES[gi * NSA_GROUP + r]), s)
    return s


def _nsa_cmp_kernel(qt_ref, kc_ref, vct_ref, poolt_ref, pair_ref, earlier_ref, oct_ref, selt_ref, list_ref,
                    *, n_sel):
    first_tile = pl.program_id(1) * CMP_TILES
    qt = selt_ref.shape[4]
    ncp = kc_ref.shape[2]
    n_slc = poolt_ref.shape[0]
    gq = qt_ref.shape[4]
    units = [(g, j) for j in range(CMP_TILES) for g in range(NSA_KV_HEADS)]
    times = [(first_tile + j) * qt + _iota((1, qt), 1) for j in range(CMP_TILES)]
    aux_row = _iota((AUX_COLS, gq), 0)
    col_head = _iota((1, gq), 1) // qt
    pad_rows = jnp.zeros((kc_ref.shape[3] - qt_ref.shape[3] - AUX_COLS, gq), BF16)

    def weights(g, j):
        slope_cols = jnp.zeros((1, gq), F32)
        for r in range(NSA_GROUP):
            slope_cols = jnp.where(col_head == r, _slope(g, r), slope_cols)
        aux = jnp.where(aux_row == 0, slope_cols * (CMP_STRIDE * CMP_AUX_SPLIT),
                        jnp.where(aux_row == 1, slope_cols * CMP_STRIDE, 0.0))
        return jnp.concatenate([qt_ref[0, g, j], aux.astype(BF16), pad_rows], axis=0)

    def importance(g, j, nr, nb):
        t = times[j]
        s = _dot(kc_ref[0, g, :nr, :], weights(g, j))
        edge = min(nr, 2 * CMP_ROWS_STEP)
        cmp_end = ((nr - edge) + _iota((edge, 1), 0)) * CMP_STRIDE + (CMP_LEN - 1)
        visible = cmp_end <= t
        vct = vct_ref[0, g * HEAD_DIM:(g + 1) * HEAD_DIM, :nr]
        psum = jnp.zeros((nr, qt), F32)
        for r in range(NSA_GROUP):
            cols = slice(r * qt, (r + 1) * qt)
            sr = s[:, cols]
            tail = jnp.where(visible, sr[nr - edge:], NEG_INF)
            sr = tail if edge == nr else jnp.concatenate([sr[:nr - edge], tail], axis=0)
            m = jnp.maximum(jnp.max(sr, axis=0, keepdims=True), SOFTMAX_FLOOR)
            e = jnp.exp(sr - m)
            l = jnp.sum(e, axis=0, keepdims=True)
            inv = jnp.where(l > 0.0, 1.0 / l, 0.0)
            oct_ref[0, g, j, :, cols] = (_dot(vct, e.astype(BF16)) * inv).astype(BF16)
            psum = psum + e * inv
        return _dot(poolt_ref[:nb, :nr], psum.astype(BF16))

    def visible_prefix(nr):
        nb = min(n_slc, nr * CMP_STRIDE // SLC_BLOCK)
        imp = jnp.concatenate([importance(g, j, nr, nb) for g, j in units], axis=1)
        blk = _iota((nb, 1), 0)
        cur = jnp.concatenate([times[j] // SLC_BLOCK for _, j in units], axis=1)
        forced = (blk == 0) | (blk == cur) | (blk == cur - 1)
        score = jnp.where(forced, TAKEN, jnp.where(blk <= cur, imp, NEG_INF))
        n_forced = 1 + jnp.where(cur >= 1, 1, 0) + jnp.where(cur >= 2, 1, 0)
        blk_f = blk.astype(F32)

        def take_one(score, active):
            m = jnp.max(score, axis=0, keepdims=True)
            first = jnp.min(jnp.where(score == m, blk_f, F32(1e9)), axis=0, keepdims=True)
            hit = (blk_f == first) if active is None else ((blk_f == first) & active)
            return jnp.where(hit, TAKEN, score)

        common_rounds = max(n_sel - 3, 0)
        for _ in range(common_rounds):
            score = take_one(score, None)

        def early_rounds(score):
            for k in range(common_rounds, n_sel - 1):
                score = take_one(score, n_sel - n_forced > k)
            return score

        score = lax.cond(first_tile * qt < 2 * SLC_BLOCK, early_rounds, lambda sc: sc, score)
        for u, (g, j) in enumerate(units):
            sel_g = score[:, u * qt:(u + 1) * qt] == TAKEN
            selt_ref[0, g, j, :nb, :] = jnp.where(sel_g, 0.0, NEG_INF)
            if nb < n_slc:
                selt_ref[0, g, j, nb:, :] = jnp.full((n_slc - nb, qt), NEG_INF, F32)
            count = _dot_nt(jnp.ones((8, qt), BF16), jnp.where(sel_g, 1.0, 0.0).astype(BF16))
            used = jnp.where(count > 0.0, 1.0, 0.0).astype(BF16)
            chunk = _iota((8, LANES), 1)
            flagged = jnp.where((_dot(used, pair_ref[:nb, :]) > 0.0) & (chunk < first_tile + j), 1.0, 0.0).astype(BF16)
            place = _dot(flagged, earlier_ref[...])
            lands = (_iota((LANES, LANES), 0).astype(F32) == place[0:1, :]) & (flagged[0:1, :] > 0)
            listed = _dot_nt(chunk.astype(BF16), jnp.where(lands, 1.0, 0.0).astype(BF16))
            n_listed = _dot(flagged, jnp.ones((LANES, LANES), BF16))
            list_ref[0, g, j] = jnp.where(_iota((8, LANES), 0) < 4, listed, n_listed).astype(jnp.int32)

    step = min(CMP_ROWS_STEP, ncp)
    rows_needed = jnp.minimum(((first_tile + CMP_TILES) * qt - CMP_LEN) // CMP_STRIDE + 1, ncp)
    n_steps = (rows_needed + step - 1) // step
    for k in range(ncp // step):
        pl.when(n_steps == k + 1)(functools.partial(visible_prefix, (k + 1) * step))


def _nsa_cmp(q_t, kc, vct, poolt_m, pair_m, earlier_m, n_sel):
    b, ng, n_tiles, qrows, qcols = q_t.shape
    ncp = kc.shape[2]
    n_slc = poolt_m.shape[0]
    const = lambda shape: pl.BlockSpec(shape, lambda bi, i: (0,) * len(shape))
    tile5 = lambda rows, cols: pl.BlockSpec((1, ng, CMP_TILES, rows, cols), lambda bi, i: (bi, 0, i, 0, 0))
    return pl.pallas_call(
        functools.partial(_nsa_cmp_kernel, n_sel=n_sel),
        grid=(b, n_tiles // CMP_TILES),
        in_specs=[
            tile5(qrows, qcols),
            pl.BlockSpec((1, ng, ncp, kc.shape[3]), lambda bi, i: (bi, 0, 0, 0)),
            pl.BlockSpec((1, ng * HEAD_DIM, ncp), lambda bi, i: (bi, 0, 0)),
            const(poolt_m.shape), const(pair_m.shape), const(earlier_m.shape),
        ],
        out_specs=[tile5(HEAD_DIM, qcols), tile5(n_slc, Q_TILE), tile5(8, LANES)],
        out_shape=[
            jax.ShapeDtypeStruct((b, ng, n_tiles, HEAD_DIM, qcols), BF16),
            jax.ShapeDtypeStruct((b, ng, n_tiles, n_slc, Q_TILE), F32),
            jax.ShapeDtypeStruct((b, ng, n_tiles, 8, LANES), jnp.int32),
        ],
        compiler_params=_params("parallel", "parallel"),
        name="nsa_compressed",
    )(q_t, kc, vct, poolt_m, pair_m, earlier_m)


def _nsa_slc_kernel(lists_ref, counts_ref, qt_ref, ks_ref, vst_ref, kw_ref, vwt_ref, selt_ref, oct_ref, gt_ref,
                    out_ref, m_ref, l_ref, acc_ref, ow_ref, sa_ref, sb_ref, sw_ref):
    bi = pl.program_id(0)
    g = pl.program_id(1)
    i = pl.program_id(2)
    n_tiles = pl.num_programs(2)
    qt = out_ref.shape[1]
    q_rows = qt_ref[0, 0, 0]
    gq = q_rows.shape[1]
    start = i * qt
    lane_f = _iota((1, qt), 1).astype(F32)
    key_in_chunk = _iota((KEY_CHUNK, qt), 0)
    query_in_tile = _iota((KEY_CHUNK, qt), 1)

    aux_row = _iota((AUX_COLS, gq), 0)
    col_head = _iota((1, gq), 1) // qt
    slope_cols = jnp.zeros((1, gq), F32)
    for r in range(NSA_GROUP):
        slope_cols = jnp.where(col_head == r, _slope(g, r), slope_cols)
    aux_base = jnp.where(aux_row == AUX_SLOPE, slope_cols, 0.0)
    pad_rows = jnp.zeros((ks_ref.shape[3] - q_rows.shape[0] - AUX_COLS, gq), BF16)

    def scores_of(slots):
        return [_dot(k, jnp.concatenate([q_rows, aux.astype(BF16), pad_rows], axis=0))
                for k, aux, _, _, _ in slots]

    def values_of(slots):
        values = jnp.concatenate([v for _, _, v, _, _ in slots], axis=1)
        return jnp.concatenate([values, jnp.ones((ONES_ROWS, values.shape[1]), BF16)], axis=0)

    def softmax_step(slots, scores, v_cat, r, m_old):
        cols = slice(r * qt, (r + 1) * qt)
        srs, tops = [], []
        for j, (_, _, _, shift, mask) in enumerate(slots):
            sr = scores[j, :, cols] if hasattr(scores, "at") else scores[j][:, cols]
            sr = sr if mask is None else jnp.where(mask, sr, NEG_INF)
            srs.append(sr)
            tops.append(jnp.max(sr, axis=0, keepdims=True) + shift[r])
        m_new = functools.reduce(jnp.maximum, tops, m_old)
        ps = [jnp.exp((sr - (m_new - slot[3][r])).astype(BF16)) for slot, sr in zip(slots, srs)]
        weighted = _dot(v_cat, jnp.concatenate(ps, axis=0))
        return m_new, weighted[HEAD_DIM:HEAD_DIM + 1], weighted[:HEAD_DIM]

    def shifts(dist0, ok):
        rows = [-_slope(g, r) * (dist0 + lane_f) for r in range(NSA_GROUP)]
        return rows if ok is None else [jnp.where(ok, row, NEG_INF) for row in rows]

    tile_id = (bi * NSA_KV_HEADS + g) * n_tiles + i
    n_listed = counts_ref[tile_id]

    def selected_slot(c, ok, mask):
        at = pl.multiple_of(c * KEY_CHUNK, KEY_CHUNK)
        bias = selt_ref[0, 0, 0, pl.ds(c * BLOCKS_PER_CHUNK, BLOCKS_PER_CHUNK), :]
        aux = aux_base
        for blk in range(BLOCKS_PER_CHUNK):
            aux = jnp.where(aux_row == blk, jnp.concatenate([bias[blk:blk + 1]] * NSA_GROUP, axis=1), aux)
        return (ks_ref[0, 0, pl.ds(at, KEY_CHUNK), :], aux, vst_ref[0, 0, c],
                shifts((start - c * KEY_CHUNK).astype(F32), ok), mask)

    def listed_slot(idx):
        ok = idx < n_listed
        c = jnp.where(ok, lists_ref[tile_id * LANES + jnp.minimum(idx, jnp.maximum(n_listed - 1, 0))], 0)
        return selected_slot(c, ok, None)

    floor = jnp.full((1, qt), SOFTMAX_FLOOR, F32)

    n_back = WINDOW // KEY_CHUNK
    slots = []
    for j in range(n_back + 1):
        cs = start - WINDOW + j * KEY_CHUNK
        chunk = jnp.maximum(cs, 0) // KEY_CHUNK
        at = pl.multiple_of(chunk * KEY_CHUNK, KEY_CHUNK)
        mask = (query_in_tile < key_in_chunk) if j == 0 else (
            (key_in_chunk <= query_in_tile) if j == n_back else None)
        slots.append((kw_ref[0, 0, pl.ds(at, KEY_CHUNK), :], aux_base, vwt_ref[0, 0, chunk],
                      shifts(F32(WINDOW - j * KEY_CHUNK), cs >= 0), mask))
    window_slots = slots
    first_slots = [selected_slot(i, None, key_in_chunk <= query_in_tile)] + [
        listed_slot(j) for j in range(ATTN_BATCH - 1)]

    def listed_batch(n):
        return [listed_slot(ATTN_BATCH - 1 + n * ATTN_BATCH + j) for j in range(ATTN_BATCH)]

    def issue(slots, dst_ref):
        for j, s in enumerate(scores_of(slots)):
            dst_ref[j] = s

    issue(window_slots, sw_ref)
    issue(first_slots, sb_ref)
    v_cat = values_of(window_slots)
    for r in range(NSA_GROUP):
        _, total, weighted = softmax_step(window_slots, sw_ref, v_cat, r, floor)
        ow_ref[:, r * qt:(r + 1) * qt] = weighted * jnp.where(total > 0.0, 1.0 / total, 0.0)

    def consume(slots, src_ref):
        v_cat = values_of(slots)
        for r in range(NSA_GROUP):
            cols = slice(r * qt, (r + 1) * qt)
            m_old = m_ref[r]
            m_new, total, weighted = softmax_step(slots, src_ref, v_cat, r, m_old)
            alpha = jnp.exp(m_old - m_new)
            l_ref[r] = alpha * l_ref[r] + total
            acc_ref[:, cols] = alpha * acc_ref[:, cols] + weighted
            m_ref[r] = m_new

    issue(listed_batch(0), sa_ref)
    v_cat = values_of(first_slots)
    for r in range(NSA_GROUP):
        m_ref[r], l_ref[r], acc_ref[:, r * qt:(r + 1) * qt] = softmax_step(first_slots, sb_ref, v_cat, r, floor)

    def batch_pair(it, carry):
        first, second, third = (listed_batch(2 * it + n) for n in range(3))
        issue(second, sb_ref)
        consume(first, sa_ref)
        issue(third, sa_ref)
        consume(second, sb_ref)
        return carry

    n_rest = jnp.maximum(n_listed - (ATTN_BATCH - 1), 0)
    n_batches = (n_rest + ATTN_BATCH - 1) // ATTN_BATCH
    lax.fori_loop(0, n_batches // 2, batch_pair, 0)

    @pl.when(n_batches % 2 == 1)
    def _():
        consume(listed_batch(n_batches - 1), sa_ref)

    def finalize(r):
        l = l_ref[r]
        return acc_ref[:, r * qt:(r + 1) * qt] * jnp.where(l > 0.0, 1.0 / l, 0.0)

    outs = []
    for r in range(NSA_GROUP):
        cols = slice(r * qt, (r + 1) * qt)
        col = (g * NSA_GROUP + r) * NSA_BRANCHES
        gate = lambda br: jax.nn.sigmoid(gt_ref[0, pl.ds(col + br, 1), :])
        outs.append(gate(0) * oct_ref[0, 0, 0, :, cols].astype(F32) + gate(1) * finalize(r)
                    + gate(2) * ow_ref[:, cols])
    out_ref[0] = jnp.concatenate(outs, axis=0).T.astype(out_ref.dtype)


def _nsa_slc(lists, counts, q_t, ks, vst, kw, vwt, selt, oct, gates_t):
    b, _, t, kw_cols = ks.shape
    _, _, n_tiles, qrows, qcols = q_t.shape
    n_slc = selt.shape[3]
    n_chunks = vst.shape[2]
    gw = NSA_GROUP * HEAD_DIM
    k_spec = pl.BlockSpec((1, 1, t, kw_cols), lambda bi, g, i, ls, ns: (bi, g, 0, 0))
    vt_spec = pl.BlockSpec((1, 1, n_chunks, HEAD_DIM, KEY_CHUNK), lambda bi, g, i, ls, ns: (bi, g, 0, 0, 0))
    tile5 = lambda rows, cols: pl.BlockSpec((1, 1, 1, rows, cols), lambda bi, g, i, ls, ns: (bi, g, i, 0, 0))
    grid_spec = pltpu.PrefetchScalarGridSpec(
        num_scalar_prefetch=2,
        grid=(b, NSA_KV_HEADS, n_tiles),
        in_specs=[
            tile5(qrows, qcols),
            k_spec, vt_spec, k_spec, vt_spec,
            tile5(n_slc, Q_TILE),
            tile5(HEAD_DIM, qcols),
            pl.BlockSpec((1, GATE_PAD, Q_TILE), lambda bi, g, i, ls, ns: (bi, 0, i)),
        ],
        out_specs=pl.BlockSpec((1, Q_TILE, gw), lambda bi, g, i, ls, ns: (bi, i, g)),
        scratch_shapes=[pltpu.VMEM((NSA_GROUP, 1, Q_TILE), F32), pltpu.VMEM((NSA_GROUP, 1, Q_TILE), F32),
                        pltpu.VMEM((HEAD_DIM, qcols), F32), pltpu.VMEM((HEAD_DIM, qcols), F32),
                        pltpu.VMEM((ATTN_BATCH, KEY_CHUNK, qcols), F32),
                        pltpu.VMEM((ATTN_BATCH, KEY_CHUNK, qcols), F32),
                        pltpu.VMEM((WINDOW // KEY_CHUNK + 1, KEY_CHUNK, qcols), F32)],
    )
    return pl.pallas_call(
        _nsa_slc_kernel,
        grid_spec=grid_spec,
        out_shape=jax.ShapeDtypeStruct((b, t, NSA_HEADS * HEAD_DIM), BF16),
        compiler_params=_params("parallel", "parallel", "parallel"),
        name="nsa_selected_window",
    )(lists, counts, q_t, ks, vst, kw, vwt, selt, oct, gates_t)


POOL_HALO = 16
CONV_HALO = 8


def _merge_kernel(h_ref, g_ref, nsa_ref, pool_ref, pool_halo_ref, conv_ref, conv_halo_ref,
                  wmg_ref, wnsa_ref, pool_bd_ref, pool_scale_ref, wpool_ref, convw_ref, wconv_ref, wo_ref,
                  g2_ref, wr_ref, br_ref, before_ref, fold_ref,
                  out_ref, xn_ref, comb_ref, count_ref, pool_ext, conv_ext, *, seq_len):
    i = pl.program_id(0)
    tm, d = h_ref.shape
    cw = pool_ref.shape[1]
    pos0 = (i * tm) % seq_len
    keep_halo = jnp.where(pos0 == 0, 0.0, 1.0)
    pos = pos0 + _iota((tm, 1), 0)

    u = pool_ref[...]
    pool_ext[0:POOL_HALO, :] = pool_halo_ref[...] * keep_halo
    pool_ext[POOL_HALO:, :] = u
    lane_group = _iota((1, cw), 1) // (cw // len(POOL_WINDOWS))
    total = u
    mean = jnp.zeros_like(u)
    done = 1
    for gi, win in enumerate(POOL_WINDOWS):
        for k in range(done, win):
            total = total + pool_ext[POOL_HALO - k:POOL_HALO - k + tm, :]
        done = win
        cnt = jnp.minimum(pos + 1, win).astype(F32)
        mean = jnp.where(lane_group == gi, total / cnt, mean)
    pooled = (mean - u).astype(BF16)
    mixed = _dot(pooled, pool_bd_ref[...]) * pool_scale_ref[...]
    y_pool = _dot(mixed.astype(BF16), wpool_ref[...])

    ch = conv_ref[:, 0:cw]
    cb = conv_ref[:, cw:2 * cw]
    cc = conv_ref[:, 2 * cw:3 * cw]
    conv_ext[0:CONV_HALO, :] = conv_halo_ref[:, 0:cw] * conv_halo_ref[:, 2 * cw:3 * cw] * keep_halo
    conv_ext[CONV_HALO:, :] = cc * ch
    y = jnp.zeros((tm, cw), F32)
    for k in range(CONV_K):
        off = CONV_HALO - (CONV_K - 1) + k
        y = y + convw_ref[k:k + 1, :] * conv_ext[off:off + tm, :]
    y_conv = _dot((cb * y).astype(BF16), wconv_ref[...])

    y_nsa = _dot(nsa_ref[...], wnsa_ref[...])

    h = h_ref[...]
    xn = _rms_norm(h, g_ref[...]).astype(BF16)
    merged = jnp.zeros((tm, d), F32)
    for br, y_br in enumerate((y_nsa, y_pool, y_conv)):
        mg = jax.nn.sigmoid(_dot(xn, wmg_ref[:, br * d:(br + 1) * d]))
        merged = merged + mg * y_br
    h_new = h + _dot(merged.astype(BF16), wo_ref[...])
    out_ref[...] = h_new
    _route_tile(h_new, g2_ref, wr_ref, br_ref, before_ref, fold_ref, xn_ref, comb_ref, count_ref)


def _merge(h, g, nsa, pool_u, conv, wmg, wnsa, pool_bd, pool_scale, wpool, convw, wconv, wo,
           g2, wr, br, before, fold, seq_len):
    n, d = h.shape
    tm = TOKEN_TILE
    cw = pool_u.shape[1]
    row = lambda width: pl.BlockSpec((tm, width), lambda i: (i, 0))
    full = lambda a: pl.BlockSpec(a.shape, lambda i: (0,) * a.ndim)
    halo = lambda rows, width: pl.BlockSpec(
        (rows, width), lambda i: (jnp.maximum(i * (tm // rows) - 1, 0), 0))
    return pl.pallas_call(
        functools.partial(_merge_kernel, seq_len=seq_len),
        grid=(n // tm,),
        in_specs=[row(d), full(g), row(nsa.shape[1]), row(cw), halo(POOL_HALO, cw),
                  row(conv.shape[1]), halo(CONV_HALO, conv.shape[1]),
                  full(wmg), full(wnsa), full(pool_bd), full(pool_scale), full(wpool), full(convw),
                  full(wconv), full(wo), full(g2), full(wr), full(br), full(before), full(fold)],
        out_specs=[row(d), row(d + ROUTER_PAD), row(ROUTER_PAD), pl.BlockSpec((8, ROUTER_PAD), lambda i: (0, 0))],
        out_shape=[jax.ShapeDtypeStruct((n, d), F32), jax.ShapeDtypeStruct((n, d + ROUTER_PAD), BF16),
                   jax.ShapeDtypeStruct((n, ROUTER_PAD), F32), jax.ShapeDtypeStruct((8, ROUTER_PAD), F32)],
        scratch_shapes=[pltpu.VMEM((tm + POOL_HALO, cw), F32), pltpu.VMEM((tm + CONV_HALO, cw), F32)],
        compiler_params=_params("arbitrary"),
        name="merge",
    )(h, g, nsa, pool_u, pool_u, conv, conv, wmg, wnsa, pool_bd, pool_scale, wpool, convw, wconv, wo,
      g2, wr, br, before, fold)


def _route(logits):
    lane = _iota(logits.shape, 1)
    lane_f = lane.astype(F32)
    big = F32(1e9)
    is_group = lane < N_EXPERT_GROUPS
    gl = jnp.where(is_group, logits, NEG_INF)
    g_max = jnp.max(gl, axis=1, keepdims=True)
    g_sel = jnp.min(jnp.where(gl == g_max, lane_f, big), axis=1, keepdims=True)
    g_prob = 1.0 / jnp.sum(jnp.where(is_group, jnp.exp(gl - g_max), 0.0), axis=1, keepdims=True)
    lo = N_EXPERT_GROUPS + EXPERTS_PER_GROUP * g_sel
    in_group = (lane_f >= lo) & (lane_f < lo + EXPERTS_PER_GROUP)
    el = jnp.where(in_group, logits, NEG_INF)
    v1 = jnp.max(el, axis=1, keepdims=True)
    i1 = jnp.min(jnp.where((el == v1) & in_group, lane_f, big), axis=1, keepdims=True)
    el2 = jnp.where(lane_f == i1, NEG_INF, el)
    rest = in_group & (lane_f != i1)
    v2 = jnp.max(el2, axis=1, keepdims=True)
    i2 = jnp.min(jnp.where((el2 == v2) & rest, lane_f, big), axis=1, keepdims=True)
    e2 = jnp.exp(v2 - v1)
    w1 = g_prob / (1.0 + e2)
    w2 = g_prob * e2 / (1.0 + e2)
    return jnp.where(lane_f == i1, w1, 0.0) + jnp.where(lane_f == i2, w2, 0.0), g_sel


GROUP_LANE = N_EXPERT_GROUPS + N_EXPERTS


RANK_LANE = GROUP_LANE + 1


def _route_tile(h, g_ref, wr_ref, br_ref, before_ref, fold_ref, xn_ref, comb_ref, count_ref):
    @pl.when(pl.program_id(0) == 0)
    def _():
        count_ref[...] = jnp.zeros(count_ref.shape, F32)

    xn = _rms_norm(h, g_ref[...])
    xn_hi = xn.astype(BF16)
    xn_lo = (xn - xn_hi.astype(F32)).astype(BF16)
    logits = (_dot(xn_hi, wr_ref[0]) + (_dot(xn_hi, wr_ref[1]) + _dot(xn_lo, wr_ref[0]))) + br_ref[...]
    comb, g_sel = _route(logits)
    lane = _iota(comb.shape, 1)
    chose = jnp.where(lane.astype(F32) == g_sel, 1.0, 0.0)
    earlier = _dot(before_ref[...], chose.astype(BF16)) + count_ref[0:1, :]
    rank = jnp.sum(chose * earlier, axis=1, keepdims=True)
    count_ref[0:1, :] = count_ref[0:1, :] + jnp.sum(chose, axis=0, keepdims=True)
    comb_ref[...] = jnp.where(lane == GROUP_LANE, g_sel, jnp.where(lane == RANK_LANE, rank, comb))
    comb_hi = comb.astype(BF16)
    comb_lo = (comb - comb_hi.astype(F32)).astype(BF16)
    d = xn_hi.shape[1]
    xn_ref[:, :d] = xn_hi
    xn_ref[:, d:] = (_dot(comb_hi, fold_ref[0]) + _dot(comb_lo, fold_ref[1])).astype(BF16)


def _experts_kernel(tile_group_ref, n_active_ref, x_ref, w1_ref, w3_ref, w2_ref, out_ref, acc_ref):
    i = pl.program_id(0)

    @pl.when(i < n_active_ref[0])
    def _():
        d = out_ref.shape[1]
        x = x_ref[:, :d]
        comb = x_ref[:, d:].astype(F32)
        lane = _iota(comb.shape, 1)
        for e in range(EXPERTS_PER_GROUP):
            c_e = jnp.sum(jnp.where((lane == e) | (lane == EXPERTS_PER_GROUP + e), comb, 0.0),
                          axis=1, keepdims=True)
            a = (jax.nn.silu(_dot(x, w1_ref[0, 0, e].astype(BF16)))
                 * _dot(x, w3_ref[0, 0, e].astype(BF16))) * c_e
            y = _dot(a.astype(BF16), w2_ref[0, 0, e].astype(BF16))
            if e == 0:
                acc_ref[...] = y
            else:
                acc_ref[...] += y
        out_ref[...] = acc_ref[...].astype(out_ref.dtype)

    @pl.when(i >= n_active_ref[0])
    def _():
        out_ref[...] = jnp.zeros(out_ref.shape, out_ref.dtype)


def _experts(tile_group, n_active, x_sorted, w1, w3, w2, layer):
    ns = x_sorted.shape[0]
    d = w1.shape[3]
    tm = MOE_SORT_TILE
    group_w = lambda w: pl.BlockSpec((1, 1) + w.shape[2:], lambda i, tg, na: (layer, tg[i], 0, 0, 0),
                                     pipeline_mode=pl.Buffered(1))
    grid_spec = pltpu.PrefetchScalarGridSpec(
        num_scalar_prefetch=2,
        grid=(ns // tm,),
        in_specs=[
            pl.BlockSpec((tm, x_sorted.shape[1]), lambda i, tg, na: (i, 0)),
            group_w(w1), group_w(w3), group_w(w2),
        ],
        out_specs=pl.BlockSpec((tm, d), lambda i, tg, na: (i, 0)),
        scratch_shapes=[pltpu.VMEM((tm, d), F32)],
    )
    return pl.pallas_call(
        _experts_kernel,
        grid_spec=grid_spec,
        out_shape=jax.ShapeDtypeStruct((ns, d), BF16),
        compiler_params=_params("arbitrary"),
        name="experts",
    )(tile_group, n_active, x_sorted, w1, w3, w2)


def _residual_kernel(h_ref, y_ref, gf_ref, out_ref, *, final_norm):
    out = h_ref[...] + y_ref[...].astype(F32)
    out_ref[...] = _rms_norm(out, gf_ref[...]) if final_norm else out


def _residual(h, y, gf, final_norm):
    n, d = h.shape
    tm = MOE_TILE
    row = pl.BlockSpec((tm, d), lambda i: (i, 0))
    return pl.pallas_call(
        functools.partial(_residual_kernel, final_norm=final_norm),
        grid=(n // tm,),
        in_specs=[row, row, pl.BlockSpec(gf.shape, lambda i: (0, 0))],
        out_specs=row,
        out_shape=jax.ShapeDtypeStruct((n, d), F32),
        compiler_params=_params("parallel"),
        name="residual",
    )(h, y, gf)


def _group_sort_plan(group_id, rank, counts, tile):
    n = group_id.shape[0]
    n_slots = n + N_EXPERT_GROUPS * tile
    padded = (counts + tile - 1) // tile * tile
    ends = jnp.cumsum(padded)
    slot = (ends - padded)[group_id] + rank
    source = jnp.zeros((n_slots,), jnp.int32).at[slot].set(jnp.arange(n, dtype=jnp.int32))
    tile_start = jnp.arange(n_slots // tile, dtype=jnp.int32) * tile
    tile_group = jnp.minimum(jnp.searchsorted(ends, tile_start, side="right"), N_EXPERT_GROUPS - 1)
    return slot, source, tile_group.astype(jnp.int32), (ends[-1:] // tile).astype(jnp.int32)


def _moe(h, xn, comb, counts, w1, w3, w2, layer, gf, final_norm):
    as_int = lambda a: a.astype(jnp.int32)
    slot, source, tile_group, n_active = _group_sort_plan(
        as_int(comb[:, GROUP_LANE]), as_int(comb[:, RANK_LANE]), as_int(counts[0, :N_EXPERT_GROUPS]),
        MOE_SORT_TILE)
    grouped = lambda w: w.reshape((w.shape[0], N_EXPERT_GROUPS, EXPERTS_PER_GROUP) + w.shape[2:])
    rows = lambda a, idx: jnp.take(a, idx, axis=0, mode="clip")
    y_sorted = _experts(tile_group, n_active, rows(xn, source), grouped(w1), grouped(w3), grouped(w2), layer)
    return _residual(h, rows(y_sorted, slot), gf, final_norm)


def _selection_constants(seq_len):
    ncp = seq_len // CMP_STRIDE
    n_slc = seq_len // SLC_BLOCK
    ratio = SLC_BLOCK // CMP_STRIDE
    lead = CMP_LEN // CMP_STRIDE - 1
    c = np.arange(ncp)[:, None]
    j = np.arange(n_slc)[None, :]
    pool_m = ((c >= ratio * j - lead) & (c < ratio * j + ratio)).astype(np.float32)
    blocks_per_chunk = KEY_CHUNK // SLC_BLOCK
    n_chunks = seq_len // KEY_CHUNK
    pair_m = np.zeros((n_slc, LANES * ((n_chunks + LANES - 1) // LANES)), np.float32)
    pair_m[np.arange(n_slc), np.arange(n_slc) // blocks_per_chunk] = 1.0
    earlier_m = np.triu(np.ones((LANES, LANES), np.float32), 1)
    key_aux = np.zeros((TOKEN_TILE, LANES), np.float32)
    in_chunk = np.arange(TOKEN_TILE) % KEY_CHUNK
    key_aux[np.arange(TOKEN_TILE), HEAD_DIM + in_chunk // SLC_BLOCK] = 1.0
    key_aux[:, HEAD_DIM + AUX_SLOPE] = in_chunk
    cmp_aux = np.zeros((ncp, LANES), np.float32)
    cmp_aux[:, HEAD_DIM] = np.arange(ncp) // CMP_AUX_SPLIT
    cmp_aux[:, HEAD_DIM + 1] = np.arange(ncp) % CMP_AUX_SPLIT
    as_bf16 = lambda a: jnp.asarray(a, BF16)
    return as_bf16(pool_m.T), as_bf16(pair_m), as_bf16(earlier_m), jnp.asarray(key_aux), jnp.asarray(cmp_aux)


def kernel(x, norm1_g, w_in, cmp_pe, cmp_w1, cmp_w2, w_nsa_proj, pool_w, pool_scale, w_pool_proj, conv_w,
           w_conv_proj, w_o, norm2_g, router_group_w, router_group_b, router_expert_w, router_expert_b,
           expert_w1, expert_w3, expert_w2, final_norm_g):
    b, t, d = x.shape
    n = b * t
    depth = w_in.shape[0]
    dq = NSA_HEADS * HEAD_DIM
    dkv = 6 * NSA_KV_HEADS * HEAD_DIM
    dgate = NSA_HEADS * NSA_BRANCHES
    cw = d // 4
    assert t % TOKEN_TILE == 0 and n % MOE_TILE == 0 and t % (Q_TILE * CMP_TILES) == 0
    n_slc = t // SLC_BLOCK
    n_sel = min(SLC_TOPN, n_slc)
    n_chunks16 = t // CMP_STRIDE
    kvw = NSA_KV_HEADS * HEAD_DIM
    poolt_m, pair_m, earlier_m, key_aux, cmp_aux = _selection_constants(t)
    assert Q_TILE == KEY_CHUNK and AUX_SLOPE < AUX_COLS and KEY_CHUNK <= 256
    assert n_chunks16 <= 256 * CMP_AUX_SPLIT and n_chunks16 % min(CMP_ROWS_STEP, n_chunks16) == 0
    assert pair_m.shape[1] == LANES

    before = jnp.asarray(np.tril(np.ones((TOKEN_TILE, TOKEN_TILE), np.float32), -1), BF16)
    fold_np = np.zeros((2, ROUTER_PAD, ROUTER_PAD), np.float32)
    for lane in range(N_EXPERTS):
        fold_np[0, N_EXPERT_GROUPS + lane, lane % EXPERTS_PER_GROUP] = 1.0
        fold_np[1, N_EXPERT_GROUPS + lane, EXPERTS_PER_GROUP + lane % EXPERTS_PER_GROUP] = 1.0
    fold = jnp.asarray(fold_np, BF16)
    h = x.reshape(n, d)
    for l in range(depth):
        wl = w_in[l]
        o_gate = dq + dkv
        o_pool = o_gate + dgate
        o_merge = o_pool + cw + 3 * cw
        kv_cols = lambda kind: wl[:, dq + kind * kvw:dq + (kind + 1) * kvw]
        wq_t = (wl[:, :dq] * (HEAD_DIM ** -0.5)).T.astype(BF16)
        wv_t = jnp.concatenate([kv_cols(3), kv_cols(5)], axis=1).T.astype(BF16)
        wg_t = jnp.pad(wl[:, o_gate:o_pool], ((0, 0), (0, GATE_PAD - dgate))).T.astype(BF16)
        no_aux = jnp.zeros((d, LANES - HEAD_DIM), F32)
        wk = jnp.concatenate([piece for kind in (2, 4) for gi in range(NSA_KV_HEADS)
                              for piece in (kv_cols(kind)[:, gi * HEAD_DIM:(gi + 1) * HEAD_DIM], no_aux)],
                             axis=1).astype(BF16)
        wn = jnp.concatenate([kv_cols(0), kv_cols(1), wl[:, o_pool:o_merge]], axis=1).astype(BF16)
        wmg = wl[:, o_merge:].astype(BF16)
        pool_bd = jax.scipy.linalg.block_diag(*[pool_w[l, gi] for gi in range(pool_w.shape[1])]).astype(BF16)
        convw = jnp.pad(conv_w[l], ((0, 8 - CONV_K), (0, 0)))
        wr = jnp.pad(jnp.concatenate([router_group_w[l], router_expert_w[l]], axis=1),
                     ((0, 0), (0, ROUTER_PAD - N_EXPERT_GROUPS - N_EXPERTS)))
        wr_hi = wr.astype(BF16)
        wr = jnp.stack([wr_hi, (wr - wr_hi.astype(F32)).astype(BF16)])
        br = jnp.pad(jnp.concatenate([router_group_b[l], router_expert_b[l]]),
                     (0, ROUTER_PAD - N_EXPERT_GROUPS - N_EXPERTS))[None, :]
        pe = jnp.broadcast_to(cmp_pe[l].reshape(2, 1, CMP_LEN * HEAD_DIM), (2, 8, CMP_LEN * HEAD_DIM)).astype(BF16)
        halves = CMP_LEN // CMP_STRIDE
        w1_bd = jnp.einsum("khldc,gq->khlgdqc",
                           cmp_w1[l].reshape(2, halves, CMP_STRIDE, HEAD_DIM, CMP_HIDDEN),
                           jnp.eye(NSA_KV_HEADS, dtype=F32))
        w1_bd = w1_bd.reshape(2, halves, CMP_STRIDE * kvw, NSA_KV_HEADS * CMP_HIDDEN).astype(BF16)
        w2_k = jnp.pad(cmp_w2[l, 0], ((0, 0), (0, LANES - HEAD_DIM))).astype(BF16)
        w2_vt = cmp_w2[l, 1].T.astype(BF16)

        q_t, vst, vwt, gates_t, ks, kw, cmp_src, pool_u, conv = _inproj(
            h, norm1_g[l][None, :], wq_t, wv_t, wg_t, wk, wn, key_aux, b)
        kc_aux, vc_t = _compress(cmp_src, pe, cmp_w1[l].astype(BF16), w1_bd, w2_k, w2_vt, cmp_aux)
        oc_t, sel_t, listed = _nsa_cmp(q_t, kc_aux, vc_t, poolt_m, pair_m, earlier_m, n_sel)
        nsa = _nsa_slc(listed[:, :, :, 0, :].reshape(-1), listed[:, :, :, 4, 0].reshape(-1),
                       q_t, ks, vst, kw, vwt, sel_t, oc_t, gates_t)
        h, xn, comb, counts = _merge(
            h, norm1_g[l][None, :], nsa.reshape(n, dq), pool_u, conv, wmg,
            w_nsa_proj[l].astype(BF16), pool_bd, pool_scale[l][None, :], w_pool_proj[l].astype(BF16),
            convw, w_conv_proj[l].astype(BF16), w_o[l].astype(BF16), norm2_g[l][None, :], wr, br, before, fold, t)
        h = _moe(h, xn, comb, counts, expert_w1, expert_w3, expert_w2, l,
                 final_norm_g[None, :], final_norm=(l == depth - 1))
    return h.reshape(b, t, d)
```

```python
import functools

import jax
import jax.numpy as jnp
import numpy as np
from jax import lax
from jax.experimental import pallas as pl
from jax.experimental.pallas import tpu as pltpu

F32 = jnp.float32
BF16 = jnp.bfloat16

HEAD_DIM = 64
NSA_HEADS = 8
NSA_KV_HEADS = 2
NSA_GROUP = NSA_HEADS // NSA_KV_HEADS
CMP_LEN = 32
CMP_STRIDE = 16
CMP_HIDDEN = 4 * HEAD_DIM
SLC_BLOCK = 64
SLC_TOPN = 16
WINDOW = 512
NSA_BRANCHES = 3
POOL_WINDOWS = (2, 4, 8, 16)
CONV_K = 3
N_EXPERT_GROUPS = 4
EXPERTS_PER_GROUP = 8
N_EXPERTS = N_EXPERT_GROUPS * EXPERTS_PER_GROUP
RMS_EPS = 1e-6
NEG_INF = -1e30
ALIBI_SLOPES = tuple(float(2.0 ** (-8.0 * (h + 1) / NSA_HEADS)) for h in range(NSA_HEADS))

LANES = 128
VMEM_LIMIT = 56 * 1024 * 1024
TOKEN_TILE = 512
MOE_TILE = 1024
MOE_SORT_TILE = 512
Q_TILE = 128
KEY_CHUNK = 128
GATE_PAD = LANES
ROUTER_PAD = LANES


def _params(*semantics):
    return pltpu.CompilerParams(dimension_semantics=semantics, vmem_limit_bytes=VMEM_LIMIT)


def _dot(a, b):
    return jnp.dot(a, b, preferred_element_type=F32)


def _dot_nt(a, b):
    return lax.dot_general(a, b, (((1,), (1,)), ((), ())), preferred_element_type=F32)


def _rms_norm(x, g):
    y = x * lax.rsqrt(jnp.mean(x * x, axis=-1, keepdims=True) + RMS_EPS)
    return y * g


def _iota(shape, dim):
    return lax.broadcasted_iota(jnp.int32, shape, dim)


def _inproj_kernel(x_ref, g_ref, wq_ref, wv_ref, wg_ref, wk_ref, wn_ref, kaux_ref,
                   q_ref, vs_ref, vw_ref, gate_ref, ks_ref, kw_ref, cmp_ref, pool_ref, conv_ref):
    xn = _rms_norm(x_ref[...], g_ref[...]).astype(BF16)
    sub_tiles = x_ref.shape[0] // Q_TILE
    q_t = _dot_nt(wq_ref[...], xn)
    for g in range(NSA_KV_HEADS):
        for j in range(sub_tiles):
            for r in range(NSA_GROUP):
                head = g * NSA_GROUP + r
                q_ref[0, g, j, :, r * Q_TILE:(r + 1) * Q_TILE] = q_t[
                    head * HEAD_DIM:(head + 1) * HEAD_DIM, j * Q_TILE:(j + 1) * Q_TILE].astype(BF16)
    v_t = _dot_nt(wv_ref[...], xn)
    k = _dot(xn, wk_ref[...])
    for branch, (v_ref, k_ref) in enumerate(((vs_ref, ks_ref), (vw_ref, kw_ref))):
        for g in range(NSA_KV_HEADS):
            slab = branch * NSA_KV_HEADS + g
            for j in range(sub_tiles):
                v_ref[0, g, j] = v_t[slab * HEAD_DIM:(slab + 1) * HEAD_DIM,
                                     j * KEY_CHUNK:(j + 1) * KEY_CHUNK].astype(BF16)
            k_ref[0, g] = (k[:, slab * LANES:(slab + 1) * LANES] + kaux_ref[...]).astype(BF16)
    gate_ref[0] = _dot_nt(wg_ref[...], xn)
    col = 0
    for ref in (cmp_ref, pool_ref, conv_ref):
        width = ref.shape[-1]
        ref[...] = _dot(xn, wn_ref[:, col:col + width]).reshape(ref.shape)
        col += width


def _inproj(h, g, wq_t, wv_t, wg_t, wk, wn, kaux, batch):
    n, d = h.shape
    t = n // batch
    tm = TOKEN_TILE
    steps = t // tm
    sub = tm // Q_TILE
    cw = d // 4
    gq = NSA_GROUP * Q_TILE
    full = lambda a: pl.BlockSpec(a.shape, lambda i: (0,) * a.ndim)
    row = lambda width: pl.BlockSpec((tm, width), lambda i: (i, 0))
    tiles = lambda rows, cols: pl.BlockSpec((1, NSA_KV_HEADS, sub, rows, cols),
                                            lambda i: (i // steps, 0, i % steps, 0, 0))
    keys = pl.BlockSpec((1, NSA_KV_HEADS, tm, LANES), lambda i: (i // steps, 0, i % steps, 0))
    sds = jax.ShapeDtypeStruct
    v_shape = sds((batch, NSA_KV_HEADS, t // KEY_CHUNK, HEAD_DIM, KEY_CHUNK), BF16)
    k_shape = sds((batch, NSA_KV_HEADS, t, LANES), BF16)
    return pl.pallas_call(
        _inproj_kernel,
        grid=(n // tm,),
        in_specs=[row(d), full(g), full(wq_t), full(wv_t), full(wg_t), full(wk), full(wn), full(kaux)],
        out_specs=[tiles(HEAD_DIM, gq), tiles(HEAD_DIM, KEY_CHUNK), tiles(HEAD_DIM, KEY_CHUNK),
                   pl.BlockSpec((1, GATE_PAD, tm), lambda i: (i // steps, 0, i % steps)),
                   keys, keys,
                   pl.BlockSpec((1, tm, 2 * NSA_KV_HEADS * HEAD_DIM), lambda i: (i // steps, i % steps, 0)),
                   row(cw), row(3 * cw)],
        out_shape=[sds((batch, NSA_KV_HEADS, t // Q_TILE, HEAD_DIM, gq), BF16), v_shape, v_shape,
                   sds((batch, GATE_PAD, t), F32), k_shape, k_shape,
                   sds((batch, t, 2 * NSA_KV_HEADS * HEAD_DIM), F32), sds((n, cw), F32), sds((n, 3 * cw), F32)],
        compiler_params=_params("parallel"),
        name="inproj",
    )(h, g, wq_t, wv_t, wg_t, wk, wn, kaux)


def _gelu_tanh(x):
    return 0.5 * x * (1.0 + jnp.tanh(0.7978845608028654 * (x + 0.044715 * x * x * x)))


def _compress_kernel(src_ref, pe_ref, w1_ref, w1bd_ref, w2k_ref, w2vt_ref, caux_ref, kc_ref, vct_ref):
    kind = pl.program_id(1)
    ncp = kc_ref.shape[2]
    hidden = w1_ref.shape[2]
    pieces = [src_ref[0, pl.ds(l, ncp, stride=CMP_STRIDE), :].astype(BF16) for l in range(CMP_STRIDE)]
    chunk = jnp.concatenate(pieces, axis=1)
    first = _dot(chunk, w1bd_ref[0, 0])
    second = _dot(chunk, w1bd_ref[0, 1])
    bias = _dot(pe_ref[0], w1_ref[0])[0:1, :]
    hid = first + pltpu.roll(second, ncp - 1, 0) + jnp.concatenate([bias] * NSA_KV_HEADS, axis=1)
    row = _iota((ncp, 1), 0)
    act = jnp.where(row < ncp - 1, _gelu_tanh(hid), 0.0).astype(BF16)
    for g in range(NSA_KV_HEADS):
        act_g = act[:, g * hidden:(g + 1) * hidden]

        @pl.when(kind == 0)
        def _():
            kc_ref[0, g] = (_dot(act_g, w2k_ref[...]) + caux_ref[...]).astype(BF16)

        @pl.when(kind == 1)
        def _():
            vct_ref[0, g * HEAD_DIM:(g + 1) * HEAD_DIM, :] = _dot_nt(w2vt_ref[...], act_g).astype(BF16)


def _compress(src, pe, w1, w1bd, w2k, w2vt, caux):
    b, t, _ = src.shape
    ncp = t // CMP_STRIDE
    gd = NSA_KV_HEADS * HEAD_DIM
    full = lambda a: pl.BlockSpec(a.shape, lambda bi, k: (0,) * a.ndim)
    per_kind = lambda a: pl.BlockSpec((1,) + a.shape[1:], lambda bi, k: (k,) + (0,) * (a.ndim - 1))
    return pl.pallas_call(
        _compress_kernel,
        grid=(b, 2),
        in_specs=[pl.BlockSpec((1, t, gd), lambda bi, k: (bi, 0, k)),
                  per_kind(pe), per_kind(w1), per_kind(w1bd), full(w2k), full(w2vt), full(caux)],
        out_specs=[pl.BlockSpec((1, NSA_KV_HEADS, ncp, LANES), lambda bi, k: (bi, 0, 0, 0)),
                   pl.BlockSpec((1, gd, ncp), lambda bi, k: (bi, 0, 0))],
        out_shape=[jax.ShapeDtypeStruct((b, NSA_KV_HEADS, ncp, LANES), BF16),
                   jax.ShapeDtypeStruct((b, gd, ncp), BF16)],
        compiler_params=_params("parallel", "arbitrary"),
        name="compress",
    )(src, pe, w1, w1bd, w2k, w2vt, caux)


SOFTMAX_FLOOR = -1e29
TAKEN = -3e38
ATTN_BATCH = 5
BLOCKS_PER_CHUNK = KEY_CHUNK // SLC_BLOCK
AUX_COLS = 16
AUX_SLOPE = BLOCKS_PER_CHUNK
ONES_ROWS = 16
CMP_ROWS_STEP = 256
CMP_TILES = 2
CMP_AUX_SPLIT = 128


def _slope(g, r):
    if isinstance(g, int):
        return jnp.float32(ALIBI_SLOPES[g * NSA_GROUP + r])
    s = jnp.float32(ALIBI_SLOPES[r])
    for gi in range(1, NSA_KV_HEADS):
        s = jnp.where(g == gi, jnp.float32(ALIBI_SLOPES[gi * NSA_GROUP + r]), s)
    return s


def _nsa_cmp_kernel(qt_ref, kc_ref, vct_ref, poolt_ref, pair_ref, earlier_ref, oct_ref, selt_ref, list_ref,
                    *, n_sel):
    first_tile = pl.program_id(1) * CMP_TILES
    qt = selt_ref.shape[4]
    ncp = kc_ref.shape[2]
    n_slc = poolt_ref.shape[0]
    gq = qt_ref.shape[4]
    units = [(g, j) for j in range(CMP_TILES) for g in range(NSA_KV_HEADS)]
    times = [(first_tile + j) * qt + _iota((1, qt), 1) for j in range(CMP_TILES)]
    aux_row = _iota((AUX_COLS, gq), 0)
    col_head = _iota((1, gq), 1) // qt
    pad_rows = jnp.zeros((kc_ref.shape[3] - qt_ref.shape[3] - AUX_COLS, gq), BF16)

    def weights(g, j):
        slope_cols = jnp.zeros((1, gq), F32)
        for r in range(NSA_GROUP):
            slope_cols = jnp.where(col_head == r, _slope(g, r), slope_cols)
        aux = jnp.where(aux_row == 0, slope_cols * (CMP_STRIDE * CMP_AUX_SPLIT),
                        jnp.where(aux_row == 1, slope_cols * CMP_STRIDE, 0.0))
        return jnp.concatenate([qt_ref[0, g, j], aux.astype(BF16), pad_rows], axis=0)

    def importance(g, j, nr, nb):
        t = times[j]
        s = _dot(kc_ref[0, g, :nr, :], weights(g, j))
        edge = min(nr, 2 * CMP_ROWS_STEP)
        cmp_end = ((nr - edge) + _iota((edge, 1), 0)) * CMP_STRIDE + (CMP_LEN - 1)
        visible = cmp_end <= t
        vct = vct_ref[0, g * HEAD_DIM:(g + 1) * HEAD_DIM, :nr]
        psum = jnp.zeros((nr, qt), F32)
        for r in range(NSA_GROUP):
            cols = slice(r * qt, (r + 1) * qt)
            sr = s[:, cols]
            tail = jnp.where(visible, sr[nr - edge:], NEG_INF)
            sr = tail if edge == nr else jnp.concatenate([sr[:nr - edge], tail], axis=0)
            m = jnp.maximum(jnp.max(sr, axis=0, keepdims=True), SOFTMAX_FLOOR)
            e = jnp.exp(sr - m)
            l = jnp.sum(e, axis=0, keepdims=True)
            inv = jnp.where(l > 0.0, 1.0 / l, 0.0)
            oct_ref[0, g, j, :, cols] = (_dot(vct, e.astype(BF16)) * inv).astype(BF16)
            psum = psum + e * inv
        return _dot(poolt_ref[:nb, :nr], psum.astype(BF16))

    def visible_prefix(nr):
        nb = min(n_slc, nr * CMP_STRIDE // SLC_BLOCK)
        imp = jnp.concatenate([importance(g, j, nr, nb) for g, j in units], axis=1)
        blk = _iota((nb, 1), 0)
        cur = jnp.concatenate([times[j] // SLC_BLOCK for _, j in units], axis=1)
        forced = (blk == 0) | (blk == cur) | (blk == cur - 1)
        score = jnp.where(forced, TAKEN, jnp.where(blk <= cur, imp, NEG_INF))
        n_forced = 1 + jnp.where(cur >= 1, 1, 0) + jnp.where(cur >= 2, 1, 0)
        blk_f = blk.astype(F32)

        def take_one(score, active):
            m = jnp.max(score, axis=0, keepdims=True)
            first = jnp.min(jnp.where(score == m, blk_f, F32(1e9)), axis=0, keepdims=True)
            hit = (blk_f == first) if active is None else ((blk_f == first) & active)
            return jnp.where(hit, TAKEN, score)

        common_rounds = max(n_sel - 3, 0)
        for _ in range(common_rounds):
            score = take_one(score, None)

        def early_rounds(score):
            for k in range(common_rounds, n_sel - 1):
                score = take_one(score, n_sel - n_forced > k)
            return score

        score = lax.cond(first_tile * qt < 2 * SLC_BLOCK, early_rounds, lambda sc: sc, score)
        for u, (g, j) in enumerate(units):
            sel_g = score[:, u * qt:(u + 1) * qt] == TAKEN
            selt_ref[0, g, j, :nb, :] = jnp.where(sel_g, 0.0, NEG_INF)
            if nb < n_slc:
                selt_ref[0, g, j, nb:, :] = jnp.full((n_slc - nb, qt), NEG_INF, F32)
            count = _dot_nt(jnp.ones((8, qt), BF16), jnp.where(sel_g, 1.0, 0.0).astype(BF16))
            used = jnp.where(count > 0.0, 1.0, 0.0).astype(BF16)
            chunk = _iota((8, LANES), 1)
            flagged = jnp.where((_dot(used, pair_ref[:nb, :]) > 0.0) & (chunk < first_tile + j), 1.0, 0.0).astype(BF16)
            place = _dot(flagged, earlier_ref[...])
            lands = (_iota((LANES, LANES), 0).astype(F32) == place[0:1, :]) & (flagged[0:1, :] > 0)
            listed = _dot_nt(chunk.astype(BF16), jnp.where(lands, 1.0, 0.0).astype(BF16))
            n_listed = _dot(flagged, jnp.ones((LANES, LANES), BF16))
            list_ref[0, g, j] = jnp.where(_iota((8, LANES), 0) < 4, listed, n_listed).astype(jnp.int32)

    step = min(CMP_ROWS_STEP, ncp)
    rows_needed = jnp.minimum(((first_tile + CMP_TILES) * qt - CMP_LEN) // CMP_STRIDE + 1, ncp)
    n_steps = (rows_needed + step - 1) // step
    for k in range(ncp // step):
        pl.when(n_steps == k + 1)(functools.partial(visible_prefix, (k + 1) * step))


def _nsa_cmp(q_t, kc, vct, poolt_m, pair_m, earlier_m, n_sel):
    b, ng, n_tiles, qrows, qcols = q_t.shape
    ncp = kc.shape[2]
    n_slc = poolt_m.shape[0]
    const = lambda shape: pl.BlockSpec(shape, lambda bi, i: (0,) * len(shape))
    tile5 = lambda rows, cols: pl.BlockSpec((1, ng, CMP_TILES, rows, cols), lambda bi, i: (bi, 0, i, 0, 0))
    return pl.pallas_call(
        functools.partial(_nsa_cmp_kernel, n_sel=n_sel),
        grid=(b, n_tiles // CMP_TILES),
        in_specs=[
            tile5(qrows, qcols),
            pl.BlockSpec((1, ng, ncp, kc.shape[3]), lambda bi, i: (bi, 0, 0, 0)),
            pl.BlockSpec((1, ng * HEAD_DIM, ncp), lambda bi, i: (bi, 0, 0)),
            const(poolt_m.shape), const(pair_m.shape), const(earlier_m.shape),
        ],
        out_specs=[tile5(HEAD_DIM, qcols), tile5(n_slc, Q_TILE), tile5(8, LANES)],
        out_shape=[
            jax.ShapeDtypeStruct((b, ng, n_tiles, HEAD_DIM, qcols), BF16),
            jax.ShapeDtypeStruct((b, ng, n_tiles, n_slc, Q_TILE), F32),
            jax.ShapeDtypeStruct((b, ng, n_tiles, 8, LANES), jnp.int32),
        ],
        compiler_params=_params("parallel", "parallel"),
        name="nsa_compressed",
    )(q_t, kc, vct, poolt_m, pair_m, earlier_m)


def _nsa_slc_kernel(lists_ref, counts_ref, qt_ref, ks_ref, vst_ref, kw_ref, vwt_ref, selt_ref, oct_ref, gt_ref,
                    out_ref, m_ref, l_ref, acc_ref, ow_ref, sa_ref, sb_ref, sw_ref):
    bi = pl.program_id(0)
    g = pl.program_id(1)
    i = pl.program_id(2)
    n_tiles = pl.num_programs(2)
    qt = out_ref.shape[1]
    q_rows = qt_ref[0, 0, 0]
    gq = q_rows.shape[1]
    start = i * qt
    lane_f = _iota((1, qt), 1).astype(F32)
    key_in_chunk = _iota((KEY_CHUNK, qt), 0)
    query_in_tile = _iota((KEY_CHUNK, qt), 1)

    aux_row = _iota((AUX_COLS, gq), 0)
    col_head = _iota((1, gq), 1) // qt
    slope_cols = jnp.zeros((1, gq), F32)
    for r in range(NSA_GROUP):
        slope_cols = jnp.where(col_head == r, _slope(g, r), slope_cols)
    aux_base = jnp.where(aux_row == AUX_SLOPE, slope_cols, 0.0)
    pad_rows = jnp.zeros((ks_ref.shape[3] - q_rows.shape[0] - AUX_COLS, gq), BF16)

    def scores_of(slots):
        return [_dot(k, jnp.concatenate([q_rows, aux.astype(BF16), pad_rows], axis=0))
                for k, aux, _, _, _ in slots]

    def values_of(slots):
        values = jnp.concatenate([v for _, _, v, _, _ in slots], axis=1)
        return jnp.concatenate([values, jnp.ones((ONES_ROWS, values.shape[1]), BF16)], axis=0)

    def softmax_step(slots, scores, v_cat, r, m_old):
        cols = slice(r * qt, (r + 1) * qt)
        srs, tops = [], []
        for j, (_, _, _, shift, mask) in enumerate(slots):
            sr = scores[j, :, cols] if hasattr(scores, "at") else scores[j][:, cols]
            sr = sr if mask is None else jnp.where(mask, sr, NEG_INF)
            srs.append(sr)
            tops.append(jnp.max(sr, axis=0, keepdims=True) + shift[r])
        m_new = functools.reduce(jnp.maximum, tops, m_old)
        ps = [jnp.exp((sr - (m_new - slot[3][r])).astype(BF16)) for slot, sr in zip(slots, srs)]
        weighted = _dot(v_cat, jnp.concatenate(ps, axis=0))
        return m_new, weighted[HEAD_DIM:HEAD_DIM + 1], weighted[:HEAD_DIM]

    def shifts(dist0, ok):
        rows = [-_slope(g, r) * (dist0 + lane_f) for r in range(NSA_GROUP)]
        return rows if ok is None else [jnp.where(ok, row, NEG_INF) for row in rows]

    tile_id = (bi * NSA_KV_HEADS + g) * n_tiles + i
    n_listed = counts_ref[tile_id]

    def selected_slot(c, ok, mask):
        at = pl.multiple_of(c * KEY_CHUNK, KEY_CHUNK)
        bias = selt_ref[0, 0, 0, pl.ds(c * BLOCKS_PER_CHUNK, BLOCKS_PER_CHUNK), :]
        aux = aux_base
        for blk in range(BLOCKS_PER_CHUNK):
            aux = jnp.where(aux_row == blk, jnp.concatenate([bias[blk:blk + 1]] * NSA_GROUP, axis=1), aux)
        return (ks_ref[0, 0, pl.ds(at, KEY_CHUNK), :], aux, vst_ref[0, 0, c],
                shifts((start - c * KEY_CHUNK).astype(F32), ok), mask)

    def listed_slot(idx):
        ok = idx < n_listed
        c = jnp.where(ok, lists_ref[tile_id * LANES + jnp.minimum(idx, jnp.maximum(n_listed - 1, 0))], 0)
        return selected_slot(c, ok, None)

    floor = jnp.full((1, qt), SOFTMAX_FLOOR, F32)

    n_back = WINDOW // KEY_CHUNK
    slots = []
    for j in range(n_back + 1):
        cs = start - WINDOW + j * KEY_CHUNK
        chunk = jnp.maximum(cs, 0) // KEY_CHUNK
        at = pl.multiple_of(chunk * KEY_CHUNK, KEY_CHUNK)
        mask = (query_in_tile < key_in_chunk) if j == 0 else (
            (key_in_chunk <= query_in_tile) if j == n_back else None)
        slots.append((kw_ref[0, 0, pl.ds(at, KEY_CHUNK), :], aux_base, vwt_ref[0, 0, chunk],
                      shifts(F32(WINDOW - j * KEY_CHUNK), cs >= 0), mask))
    window_slots = slots
    first_slots = [selected_slot(i, None, key_in_chunk <= query_in_tile)] + [
        listed_slot(j) for j in range(ATTN_BATCH - 1)]

    def listed_batch(n):
        return [listed_slot(ATTN_BATCH - 1 + n * ATTN_BATCH + j) for j in range(ATTN_BATCH)]

    def issue(slots, dst_ref):
        for j, s in enumerate(scores_of(slots)):
            dst_ref[j] = s

    issue(window_slots, sw_ref)
    issue(first_slots, sb_ref)
    v_cat = values_of(window_slots)
    for r in range(NSA_GROUP):
        _, total, weighted = softmax_step(window_slots, sw_ref, v_cat, r, floor)
        ow_ref[:, r * qt:(r + 1) * qt] = weighted * jnp.where(total > 0.0, 1.0 / total, 0.0)

    def consume(slots, src_ref):
        v_cat = values_of(slots)
        for r in range(NSA_GROUP):
            cols = slice(r * qt, (r + 1) * qt)
            m_old = m_ref[r]
            m_new, total, weighted = softmax_step(slots, src_ref, v_cat, r, m_old)
            alpha = jnp.exp(m_old - m_new)
            l_ref[r] = alpha * l_ref[r] + total
            acc_ref[:, cols] = alpha * acc_ref[:, cols] + weighted
            m_ref[r] = m_new

    issue(listed_batch(0), sa_ref)
    v_cat = values_of(first_slots)
    for r in range(NSA_GROUP):
        m_ref[r], l_ref[r], acc_ref[:, r * qt:(r + 1) * qt] = softmax_step(first_slots, sb_ref, v_cat, r, floor)

    def batch_pair(it, carry):
        first, second, third = (listed_batch(2 * it + n) for n in range(3))
        issue(second, sb_ref)
        consume(first, sa_ref)
        issue(third, sa_ref)
        consume(second, sb_ref)
        return carry

    n_rest = jnp.maximum(n_listed - (ATTN_BATCH - 1), 0)
    n_batches = (n_rest + ATTN_BATCH - 1) // ATTN_BATCH
    lax.fori_loop(0, n_batches // 2, batch_pair, 0)

    @pl.when(n_batches % 2 == 1)
    def _():
        consume(listed_batch(n_batches - 1), sa_ref)

    def finalize(r):
        l = l_ref[r]
        return acc_ref[:, r * qt:(r + 1) * qt] * jnp.where(l > 0.0, 1.0 / l, 0.0)

    outs = []
    for r in range(NSA_GROUP):
        cols = slice(r * qt, (r + 1) * qt)
        col = (g * NSA_GROUP + r) * NSA_BRANCHES
        gate = lambda br: jax.nn.sigmoid(gt_ref[0, pl.ds(col + br, 1), :])
        outs.append(gate(0) * oct_ref[0, 0, 0, :, cols].astype(F32) + gate(1) * finalize(r)
                    + gate(2) * ow_ref[:, cols])
    out_ref[0] = jnp.concatenate(outs, axis=0).T.astype(out_ref.dtype)


def _nsa_slc(lists, counts, q_t, ks, vst, kw, vwt, selt, oct, gates_t):
    b, _, t, kw_cols = ks.shape
    _, _, n_tiles, qrows, qcols = q_t.shape
    n_slc = selt.shape[3]
    n_chunks = vst.shape[2]
    gw = NSA_GROUP * HEAD_DIM
    once = dict(pipeline_mode=pl.Buffered(1))
    k_spec = pl.BlockSpec((1, 1, t, kw_cols), lambda bi, g, i, ls, ns: (bi, g, 0, 0), **once)
    vt_spec = pl.BlockSpec((1, 1, n_chunks, HEAD_DIM, KEY_CHUNK), lambda bi, g, i, ls, ns: (bi, g, 0, 0, 0), **once)
    tile5 = lambda rows, cols: pl.BlockSpec((1, 1, 1, rows, cols), lambda bi, g, i, ls, ns: (bi, g, i, 0, 0))
    grid_spec = pltpu.PrefetchScalarGridSpec(
        num_scalar_prefetch=2,
        grid=(b, NSA_KV_HEADS, n_tiles),
        in_specs=[
            tile5(qrows, qcols),
            k_spec, vt_spec, k_spec, vt_spec,
            tile5(n_slc, Q_TILE),
            tile5(HEAD_DIM, qcols),
            pl.BlockSpec((1, GATE_PAD, Q_TILE), lambda bi, g, i, ls, ns: (bi, 0, i)),
        ],
        out_specs=pl.BlockSpec((1, Q_TILE, gw), lambda bi, g, i, ls, ns: (bi, i, g)),
        scratch_shapes=[pltpu.VMEM((NSA_GROUP, 1, Q_TILE), F32), pltpu.VMEM((NSA_GROUP, 1, Q_TILE), F32),
                        pltpu.VMEM((HEAD_DIM, qcols), F32), pltpu.VMEM((HEAD_DIM, qcols), F32),
                        pltpu.VMEM((ATTN_BATCH, KEY_CHUNK, qcols), F32),
                        pltpu.VMEM((ATTN_BATCH, KEY_CHUNK, qcols), F32),
                        pltpu.VMEM((WINDOW // KEY_CHUNK + 1, KEY_CHUNK, qcols), F32)],
    )
    return pl.pallas_call(
        _nsa_slc_kernel,
        grid_spec=grid_spec,
        out_shape=jax.ShapeDtypeStruct((b, t, NSA_HEADS * HEAD_DIM), BF16),
        compiler_params=_params("parallel", "parallel", "parallel"),
        name="nsa_selected_window",
    )(lists, counts, q_t, ks, vst, kw, vwt, selt, oct, gates_t)


POOL_HALO = 16
CONV_HALO = 8


def _merge_kernel(h_ref, g_ref, nsa_ref, pool_ref, pool_halo_ref, conv_ref, conv_halo_ref,
                  wmg_ref, wnsa_ref, pool_bd_ref, pool_scale_ref, wpool_ref, convw_ref, wconv_ref, wo_ref,
                  g2_ref, wr_ref, br_ref, before_ref,
                  out_ref, xn_ref, comb_ref, count_ref, pool_ext, conv_ext, *, seq_len):
    i = pl.program_id(0)
    tm, d = h_ref.shape
    cw = pool_ref.shape[1]
    pos0 = (i * tm) % seq_len
    keep_halo = jnp.where(pos0 == 0, 0.0, 1.0)
    pos = pos0 + _iota((tm, 1), 0)

    u = pool_ref[...]
    pool_ext[0:POOL_HALO, :] = pool_halo_ref[...] * keep_halo
    pool_ext[POOL_HALO:, :] = u
    lane_group = _iota((1, cw), 1) // (cw // len(POOL_WINDOWS))
    total = u
    mean = jnp.zeros_like(u)
    done = 1
    for gi, win in enumerate(POOL_WINDOWS):
        for k in range(done, win):
            total = total + pool_ext[POOL_HALO - k:POOL_HALO - k + tm, :]
        done = win
        cnt = jnp.minimum(pos + 1, win).astype(F32)
        mean = jnp.where(lane_group == gi, total / cnt, mean)
    pooled = (mean - u).astype(BF16)
    mixed = _dot(pooled, pool_bd_ref[...]) * pool_scale_ref[...]
    y_pool = _dot(mixed.astype(BF16), wpool_ref[...])

    ch = conv_ref[:, 0:cw]
    cb = conv_ref[:, cw:2 * cw]
    cc = conv_ref[:, 2 * cw:3 * cw]
    conv_ext[0:CONV_HALO, :] = conv_halo_ref[:, 0:cw] * conv_halo_ref[:, 2 * cw:3 * cw] * keep_halo
    conv_ext[CONV_HALO:, :] = cc * ch
    y = jnp.zeros((tm, cw), F32)
    for k in range(CONV_K):
        off = CONV_HALO - (CONV_K - 1) + k
        y = y + convw_ref[k:k + 1, :] * conv_ext[off:off + tm, :]
    y_conv = _dot((cb * y).astype(BF16), wconv_ref[...])

    y_nsa = _dot(nsa_ref[...], wnsa_ref[...])

    h = h_ref[...]
    xn = _rms_norm(h, g_ref[...]).astype(BF16)
    merged = jnp.zeros((tm, d), F32)
    for br, y_br in enumerate((y_nsa, y_pool, y_conv)):
        mg = jax.nn.sigmoid(_dot(xn, wmg_ref[:, br * d:(br + 1) * d]))
        merged = merged + mg * y_br
    h_new = h + _dot(merged.astype(BF16), wo_ref[...])
    out_ref[...] = h_new
    _route_tile(h_new, g2_ref, wr_ref, br_ref, before_ref, xn_ref, comb_ref, count_ref)


def _merge(h, g, nsa, pool_u, conv, wmg, wnsa, pool_bd, pool_scale, wpool, convw, wconv, wo,
           g2, wr, br, before, seq_len):
    n, d = h.shape
    tm = TOKEN_TILE
    cw = pool_u.shape[1]
    row = lambda width: pl.BlockSpec((tm, width), lambda i: (i, 0))
    full = lambda a: pl.BlockSpec(a.shape, lambda i: (0,) * a.ndim)
    halo = lambda rows, width: pl.BlockSpec(
        (rows, width), lambda i: (jnp.maximum(i * (tm // rows) - 1, 0), 0))
    return pl.pallas_call(
        functools.partial(_merge_kernel, seq_len=seq_len),
        grid=(n // tm,),
        in_specs=[row(d), full(g), row(nsa.shape[1]), row(cw), halo(POOL_HALO, cw),
                  row(conv.shape[1]), halo(CONV_HALO, conv.shape[1]),
                  full(wmg), full(wnsa), full(pool_bd), full(pool_scale), full(wpool), full(convw),
                  full(wconv), full(wo), full(g2), full(wr), full(br), full(before)],
        out_specs=[row(d), row(d + ROUTER_PAD), row(ROUTER_PAD), pl.BlockSpec((8, ROUTER_PAD), lambda i: (0, 0))],
        out_shape=[jax.ShapeDtypeStruct((n, d), F32), jax.ShapeDtypeStruct((n, d + ROUTER_PAD), BF16),
                   jax.ShapeDtypeStruct((n, ROUTER_PAD), F32), jax.ShapeDtypeStruct((8, ROUTER_PAD), F32)],
        scratch_shapes=[pltpu.VMEM((tm + POOL_HALO, cw), F32), pltpu.VMEM((tm + CONV_HALO, cw), F32)],
        compiler_params=_params("arbitrary"),
        name="merge",
    )(h, g, nsa, pool_u, pool_u, conv, conv, wmg, wnsa, pool_bd, pool_scale, wpool, convw, wconv, wo,
      g2, wr, br, before)


def _route(logits):
    lane = _iota(logits.shape, 1)
    lane_f = lane.astype(F32)
    big = F32(1e9)
    is_group = lane < N_EXPERT_GROUPS
    gl = jnp.where(is_group, logits, NEG_INF)
    g_max = jnp.max(gl, axis=1, keepdims=True)
    g_sel = jnp.min(jnp.where(gl == g_max, lane_f, big), axis=1, keepdims=True)
    g_prob = 1.0 / jnp.sum(jnp.where(is_group, jnp.exp(gl - g_max), 0.0), axis=1, keepdims=True)
    lo = N_EXPERT_GROUPS + EXPERTS_PER_GROUP * g_sel
    in_group = (lane_f >= lo) & (lane_f < lo + EXPERTS_PER_GROUP)
    el = jnp.where(in_group, logits, NEG_INF)
    v1 = jnp.max(el, axis=1, keepdims=True)
    i1 = jnp.min(jnp.where((el == v1) & in_group, lane_f, big), axis=1, keepdims=True)
    el2 = jnp.where(lane_f == i1, NEG_INF, el)
    rest = in_group & (lane_f != i1)
    v2 = jnp.max(el2, axis=1, keepdims=True)
    i2 = jnp.min(jnp.where((el2 == v2) & rest, lane_f, big), axis=1, keepdims=True)
    e2 = jnp.exp(v2 - v1)
    w1 = g_prob / (1.0 + e2)
    w2 = g_prob * e2 / (1.0 + e2)
    return jnp.where(lane_f == i1, w1, 0.0) + jnp.where(lane_f == i2, w2, 0.0), g_sel


GROUP_LANE = N_EXPERT_GROUPS + N_EXPERTS


RANK_LANE = GROUP_LANE + 1


def _route_tile(h, g_ref, wr_ref, br_ref, before_ref, xn_ref, comb_ref, count_ref):
    @pl.when(pl.program_id(0) == 0)
    def _():
        count_ref[...] = jnp.zeros(count_ref.shape, F32)

    xn = _rms_norm(h, g_ref[...])
    xn_hi = xn.astype(BF16)
    xn_lo = (xn - xn_hi.astype(F32)).astype(BF16)
    logits = (_dot(xn_hi, wr_ref[0]) + (_dot(xn_hi, wr_ref[1]) + _dot(xn_lo, wr_ref[0]))) + br_ref[...]
    comb, g_sel = _route(logits)
    lane = _iota(comb.shape, 1)
    chose = jnp.where(lane.astype(F32) == g_sel, 1.0, 0.0)
    earlier = _dot(before_ref[...], chose.astype(BF16)) + count_ref[0:1, :]
    rank = jnp.sum(chose * earlier, axis=1, keepdims=True)
    count_ref[0:1, :] = count_ref[0:1, :] + jnp.sum(chose, axis=0, keepdims=True)
    comb_ref[...] = jnp.where(lane == GROUP_LANE, g_sel, jnp.where(lane == RANK_LANE, rank, comb))
    first = N_EXPERT_GROUPS + EXPERTS_PER_GROUP * g_sel
    local = jnp.zeros(comb.shape, F32)
    for e in range(EXPERTS_PER_GROUP):
        c_e = jnp.sum(jnp.where(lane.astype(F32) == first + e, comb, 0.0), axis=1, keepdims=True)
        local = jnp.where((lane == e) | (lane == EXPERTS_PER_GROUP + e), c_e, local)
    local_hi = local.astype(BF16)
    d = xn_hi.shape[1]
    xn_ref[:, :d] = xn_hi
    xn_ref[:, d:] = jnp.where(lane < EXPERTS_PER_GROUP, local_hi, (local - local_hi.astype(F32)).astype(BF16))


def _experts_kernel(tile_group_ref, n_active_ref, x_ref, w1_ref, w3_ref, w2_ref, out_ref, acc_ref):
    i = pl.program_id(0)

    @pl.when(i < n_active_ref[0])
    def _():
        d = out_ref.shape[1]
        x = x_ref[:, :d]
        comb = x_ref[:, d:].astype(F32)
        lane = _iota(comb.shape, 1)
        for e in range(EXPERTS_PER_GROUP):
            c_e = jnp.sum(jnp.where((lane == e) | (lane == EXPERTS_PER_GROUP + e), comb, 0.0),
                          axis=1, keepdims=True)
            a = (jax.nn.silu(_dot(x, w1_ref[0, 0, e].astype(BF16)))
                 * _dot(x, w3_ref[0, 0, e].astype(BF16))) * c_e
            y = _dot(a.astype(BF16), w2_ref[0, 0, e].astype(BF16))
            if e == 0:
                acc_ref[...] = y
            else:
                acc_ref[...] += y
        out_ref[...] = acc_ref[...].astype(out_ref.dtype)

    @pl.when(i >= n_active_ref[0])
    def _():
        out_ref[...] = jnp.zeros(out_ref.shape, out_ref.dtype)


def _experts(tile_group, n_active, x_sorted, w1, w3, w2, layer):
    ns = x_sorted.shape[0]
    d = w1.shape[3]
    tm = MOE_SORT_TILE
    group_w = lambda w: pl.BlockSpec((1, 1) + w.shape[2:], lambda i, tg, na: (layer, tg[i], 0, 0, 0),
                                     pipeline_mode=pl.Buffered(1))
    grid_spec = pltpu.PrefetchScalarGridSpec(
        num_scalar_prefetch=2,
        grid=(ns // tm,),
        in_specs=[
            pl.BlockSpec((tm, x_sorted.shape[1]), lambda i, tg, na: (i, 0)),
            group_w(w1), group_w(w3), group_w(w2),
        ],
        out_specs=pl.BlockSpec((tm, d), lambda i, tg, na: (i, 0)),
        scratch_shapes=[pltpu.VMEM((tm, d), F32)],
    )
    return pl.pallas_call(
        _experts_kernel,
        grid_spec=grid_spec,
        out_shape=jax.ShapeDtypeStruct((ns, d), BF16),
        compiler_params=_params("arbitrary"),
        name="experts",
    )(tile_group, n_active, x_sorted, w1, w3, w2)


def _residual_kernel(h_ref, y_ref, gf_ref, out_ref, *, final_norm):
    out = h_ref[...] + y_ref[...].astype(F32)
    out_ref[...] = _rms_norm(out, gf_ref[...]) if final_norm else out


def _residual(h, y, gf, final_norm):
    n, d = h.shape
    tm = MOE_TILE
    row = pl.BlockSpec((tm, d), lambda i: (i, 0))
    return pl.pallas_call(
        functools.partial(_residual_kernel, final_norm=final_norm),
        grid=(n // tm,),
        in_specs=[row, row, pl.BlockSpec(gf.shape, lambda i: (0, 0))],
        out_specs=row,
        out_shape=jax.ShapeDtypeStruct((n, d), F32),
        compiler_params=_params("parallel"),
        name="residual",
    )(h, y, gf)


def _group_sort_plan(group_id, rank, counts, tile):
    n = group_id.shape[0]
    n_slots = n + N_EXPERT_GROUPS * tile
    padded = (counts + tile - 1) // tile * tile
    ends = jnp.cumsum(padded)
    slot = (ends - padded)[group_id] + rank
    source = jnp.zeros((n_slots,), jnp.int32).at[slot].set(jnp.arange(n, dtype=jnp.int32))
    tile_start = jnp.arange(n_slots // tile, dtype=jnp.int32) * tile
    tile_group = jnp.minimum(jnp.searchsorted(ends, tile_start, side="right"), N_EXPERT_GROUPS - 1)
    return slot, source, tile_group.astype(jnp.int32), (ends[-1:] // tile).astype(jnp.int32)


def _moe(h, xn, comb, counts, w1, w3, w2, layer, gf, final_norm):
    as_int = lambda a: a.astype(jnp.int32)
    slot, source, tile_group, n_active = _group_sort_plan(
        as_int(comb[:, GROUP_LANE]), as_int(comb[:, RANK_LANE]), as_int(counts[0, :N_EXPERT_GROUPS]),
        MOE_SORT_TILE)
    grouped = lambda w: w.reshape((w.shape[0], N_EXPERT_GROUPS, EXPERTS_PER_GROUP) + w.shape[2:])
    rows = lambda a, idx: jnp.take(a, idx, axis=0, mode="clip")
    y_sorted = _experts(tile_group, n_active, rows(xn, source), grouped(w1), grouped(w3), grouped(w2), layer)
    return _residual(h, rows(y_sorted, slot), gf, final_norm)


def _selection_constants(seq_len):
    ncp = seq_len // CMP_STRIDE
    n_slc = seq_len // SLC_BLOCK
    ratio = SLC_BLOCK // CMP_STRIDE
    lead = CMP_LEN // CMP_STRIDE - 1
    c = np.arange(ncp)[:, None]
    j = np.arange(n_slc)[None, :]
    pool_m = ((c >= ratio * j - lead) & (c < ratio * j + ratio)).astype(np.float32)
    blocks_per_chunk = KEY_CHUNK // SLC_BLOCK
    n_chunks = seq_len // KEY_CHUNK
    pair_m = np.zeros((n_slc, LANES * ((n_chunks + LANES - 1) // LANES)), np.float32)
    pair_m[np.arange(n_slc), np.arange(n_slc) // blocks_per_chunk] = 1.0
    earlier_m = np.triu(np.ones((LANES, LANES), np.float32), 1)
    key_aux = np.zeros((TOKEN_TILE, LANES), np.float32)
    in_chunk = np.arange(TOKEN_TILE) % KEY_CHUNK
    key_aux[np.arange(TOKEN_TILE), HEAD_DIM + in_chunk // SLC_BLOCK] = 1.0
    key_aux[:, HEAD_DIM + AUX_SLOPE] = in_chunk
    cmp_aux = np.zeros((ncp, LANES), np.float32)
    cmp_aux[:, HEAD_DIM] = np.arange(ncp) // CMP_AUX_SPLIT
    cmp_aux[:, HEAD_DIM + 1] = np.arange(ncp) % CMP_AUX_SPLIT
    as_bf16 = lambda a: jnp.asarray(a, BF16)
    return as_bf16(pool_m.T), as_bf16(pair_m), as_bf16(earlier_m), jnp.asarray(key_aux), jnp.asarray(cmp_aux)


def kernel(x, norm1_g, w_in, cmp_pe, cmp_w1, cmp_w2, w_nsa_proj, pool_w, pool_scale, w_pool_proj, conv_w,
           w_conv_proj, w_o, norm2_g, router_group_w, router_group_b, router_expert_w, router_expert_b,
           expert_w1, expert_w3, expert_w2, final_norm_g):
    b, t, d = x.shape
    n = b * t
    depth = w_in.shape[0]
    dq = NSA_HEADS * HEAD_DIM
    dkv = 6 * NSA_KV_HEADS * HEAD_DIM
    dgate = NSA_HEADS * NSA_BRANCHES
    cw = d // 4
    assert t % TOKEN_TILE == 0 and n % MOE_TILE == 0 and t % (Q_TILE * CMP_TILES) == 0
    n_slc = t // SLC_BLOCK
    n_sel = min(SLC_TOPN, n_slc)
    n_chunks16 = t // CMP_STRIDE
    kvw = NSA_KV_HEADS * HEAD_DIM
    poolt_m, pair_m, earlier_m, key_aux, cmp_aux = _selection_constants(t)
    assert Q_TILE == KEY_CHUNK and AUX_SLOPE < AUX_COLS and KEY_CHUNK <= 256
    assert n_chunks16 <= 256 * CMP_AUX_SPLIT and n_chunks16 % min(CMP_ROWS_STEP, n_chunks16) == 0
    assert pair_m.shape[1] == LANES

    before = jnp.asarray(np.tril(np.ones((TOKEN_TILE, TOKEN_TILE), np.float32), -1), BF16)
    h = x.reshape(n, d)
    for l in range(depth):
        wl = w_in[l]
        o_gate = dq + dkv
        o_pool = o_gate + dgate
        o_merge = o_pool + cw + 3 * cw
        kv_cols = lambda kind: wl[:, dq + kind * kvw:dq + (kind + 1) * kvw]
        wq_t = (wl[:, :dq] * (HEAD_DIM ** -0.5)).T.astype(BF16)
        wv_t = jnp.concatenate([kv_cols(3), kv_cols(5)], axis=1).T.astype(BF16)
        wg_t = jnp.pad(wl[:, o_gate:o_pool], ((0, 0), (0, GATE_PAD - dgate))).T.astype(BF16)
        no_aux = jnp.zeros((d, LANES - HEAD_DIM), F32)
        wk = jnp.concatenate([piece for kind in (2, 4) for gi in range(NSA_KV_HEADS)
                              for piece in (kv_cols(kind)[:, gi * HEAD_DIM:(gi + 1) * HEAD_DIM], no_aux)],
                             axis=1).astype(BF16)
        wn = jnp.concatenate([kv_cols(0), kv_cols(1), wl[:, o_pool:o_merge]], axis=1).astype(BF16)
        wmg = wl[:, o_merge:].astype(BF16)
        pool_bd = jax.scipy.linalg.block_diag(*[pool_w[l, gi] for gi in range(pool_w.shape[1])]).astype(BF16)
        convw = jnp.pad(conv_w[l], ((0, 8 - CONV_K), (0, 0)))
        wr = jnp.pad(jnp.concatenate([router_group_w[l], router_expert_w[l]], axis=1),
                     ((0, 0), (0, ROUTER_PAD - N_EXPERT_GROUPS - N_EXPERTS)))
        wr_hi = wr.astype(BF16)
        wr = jnp.stack([wr_hi, (wr - wr_hi.astype(F32)).astype(BF16)])
        br = jnp.pad(jnp.concatenate([router_group_b[l], router_expert_b[l]]),
                     (0, ROUTER_PAD - N_EXPERT_GROUPS - N_EXPERTS))[None, :]
        pe = jnp.broadcast_to(cmp_pe[l].reshape(2, 1, CMP_LEN * HEAD_DIM), (2, 8, CMP_LEN * HEAD_DIM)).astype(BF16)
        halves = CMP_LEN // CMP_STRIDE
        w1_bd = jnp.einsum("khldc,gq->khlgdqc",
                           cmp_w1[l].reshape(2, halves, CMP_STRIDE, HEAD_DIM, CMP_HIDDEN),
                           jnp.eye(NSA_KV_HEADS, dtype=F32))
        w1_bd = w1_bd.reshape(2, halves, CMP_STRIDE * kvw, NSA_KV_HEADS * CMP_HIDDEN).astype(BF16)
        w2_k = jnp.pad(cmp_w2[l, 0], ((0, 0), (0, LANES - HEAD_DIM))).astype(BF16)
        w2_vt = cmp_w2[l, 1].T.astype(BF16)

        q_t, vst, vwt, gates_t, ks, kw, cmp_src, pool_u, conv = _inproj(
            h, norm1_g[l][None, :], wq_t, wv_t, wg_t, wk, wn, key_aux, b)
        kc_aux, vc_t = _compress(cmp_src, pe, cmp_w1[l].astype(BF16), w1_bd, w2_k, w2_vt, cmp_aux)
        oc_t, sel_t, listed = _nsa_cmp(q_t, kc_aux, vc_t, poolt_m, pair_m, earlier_m, n_sel)
        nsa = _nsa_slc(listed[:, :, :, 0, :].reshape(-1), listed[:, :, :, 4, 0].reshape(-1),
                       q_t, ks, vst, kw, vwt, sel_t, oc_t, gates_t)
        h, xn, comb, counts = _merge(
            h, norm1_g[l][None, :], nsa.reshape(n, dq), pool_u, conv, wmg,
            w_nsa_proj[l].astype(BF16), pool_bd, pool_scale[l][None, :], w_pool_proj[l].astype(BF16),
            convw, w_conv_proj[l].astype(BF16), w_o[l].astype(BF16), norm2_g[l][None, :], wr, br, before, t)
        h = _moe(h, xn, comb, counts, expert_w1, expert_w3, expert_w2, l,
                 final_norm_g[None, :], final_norm=(l == depth - 1))
    return h.reshape(b, t, d)
```

```python
import functools

import jax
import jax.numpy as jnp
import numpy as np
from jax import lax
from jax.experimental import pallas as pl
from jax.experimental.pallas import tpu as pltpu

F32 = jnp.float32
BF16 = jnp.bfloat16

HEAD_DIM = 64
NSA_HEADS = 8
NSA_KV_HEADS = 2
NSA_GROUP = NSA_HEADS // NSA_KV_HEADS
CMP_LEN = 32
CMP_STRIDE = 16
CMP_HIDDEN = 4 * HEAD_DIM
SLC_BLOCK = 64
SLC_TOPN = 16
WINDOW = 512
NSA_BRANCHES = 3
POOL_WINDOWS = (2, 4, 8, 16)
CONV_K = 3
N_EXPERT_GROUPS = 4
EXPERTS_PER_GROUP = 8
N_EXPERTS = N_EXPERT_GROUPS * EXPERTS_PER_GROUP
RMS_EPS = 1e-6
NEG_INF = -1e30
ALIBI_SLOPES = tuple(float(2.0 ** (-8.0 * (h + 1) / NSA_HEADS)) for h in range(NSA_HEADS))

LANES = 128
VMEM_LIMIT = 56 * 1024 * 1024
TOKEN_TILE = 512
MOE_TILE = 1024
MOE_SORT_TILE = 512
Q_TILE = 128
KEY_CHUNK = 128
GATE_PAD = LANES
ROUTER_PAD = LANES


def _params(*semantics):
    return pltpu.CompilerParams(dimension_semantics=semantics, vmem_limit_bytes=VMEM_LIMIT)


def _dot(a, b):
    return jnp.dot(a, b, preferred_element_type=F32)


def _dot_nt(a, b):
    return lax.dot_general(a, b, (((1,), (1,)), ((), ())), preferred_element_type=F32)


def _rms_norm(x, g):
    y = x * lax.rsqrt(jnp.mean(x * x, axis=-1, keepdims=True) + RMS_EPS)
    return y * g


def _iota(shape, dim):
    return lax.broadcasted_iota(jnp.int32, shape, dim)


def _inproj_kernel(*refs, add_expert_output):
    if add_expert_output:
        x_ref, y_ref, *refs = refs
        *refs, h_ref = refs
        x = x_ref[...] + y_ref[...].astype(F32)
        h_ref[...] = x
    else:
        x_ref, *refs = refs
        x = x_ref[...]
    (g_ref, wq_ref, wv_ref, wg_ref, wk_ref, wn_ref, kaux_ref,
     q_ref, vs_ref, vw_ref, gate_ref, ks_ref, kw_ref, cmp_ref, pool_ref, conv_ref) = refs
    xn = _rms_norm(x, g_ref[...]).astype(BF16)
    sub_tiles = x.shape[0] // Q_TILE
    q_t = _dot_nt(wq_ref[...], xn)
    for g in range(NSA_KV_HEADS):
        for j in range(sub_tiles):
            for r in range(NSA_GROUP):
                head = g * NSA_GROUP + r
                q_ref[0, g, j, :, r * Q_TILE:(r + 1) * Q_TILE] = q_t[
                    head * HEAD_DIM:(head + 1) * HEAD_DIM, j * Q_TILE:(j + 1) * Q_TILE].astype(BF16)
    v_t = _dot_nt(wv_ref[...], xn)
    k = _dot(xn, wk_ref[...])
    for branch, (v_ref, k_ref) in enumerate(((vs_ref, ks_ref), (vw_ref, kw_ref))):
        for g in range(NSA_KV_HEADS):
            slab = branch * NSA_KV_HEADS + g
            for j in range(sub_tiles):
                v_ref[0, g, j] = v_t[slab * HEAD_DIM:(slab + 1) * HEAD_DIM,
                                     j * KEY_CHUNK:(j + 1) * KEY_CHUNK].astype(BF16)
            k_ref[0, g] = (k[:, slab * LANES:(slab + 1) * LANES] + kaux_ref[...]).astype(BF16)
    gate_ref[0] = _dot_nt(wg_ref[...], xn)
    col = 0
    for ref in (cmp_ref, pool_ref, conv_ref):
        width = ref.shape[-1]
        ref[...] = _dot(xn, wn_ref[:, col:col + width]).reshape(ref.shape)
        col += width


def _inproj(h, y, g, wq_t, wv_t, wg_t, wk, wn, kaux, batch):
    n, d = h.shape
    t = n // batch
    tm = TOKEN_TILE
    steps = t // tm
    sub = tm // Q_TILE
    cw = d // 4
    gq = NSA_GROUP * Q_TILE
    full = lambda a: pl.BlockSpec(a.shape, lambda i: (0,) * a.ndim)
    row = lambda width: pl.BlockSpec((tm, width), lambda i: (i, 0))
    tiles = lambda rows, cols: pl.BlockSpec((1, NSA_KV_HEADS, sub, rows, cols),
                                            lambda i: (i // steps, 0, i % steps, 0, 0))
    keys = pl.BlockSpec((1, NSA_KV_HEADS, tm, LANES), lambda i: (i // steps, 0, i % steps, 0))
    sds = jax.ShapeDtypeStruct
    v_shape = sds((batch, NSA_KV_HEADS, t // KEY_CHUNK, HEAD_DIM, KEY_CHUNK), BF16)
    k_shape = sds((batch, NSA_KV_HEADS, t, LANES), BF16)
    add = y is not None
    outs = pl.pallas_call(
        functools.partial(_inproj_kernel, add_expert_output=add),
        grid=(n // tm,),
        in_specs=[row(d)] * (2 if add else 1)
        + [full(g), full(wq_t), full(wv_t), full(wg_t), full(wk), full(wn), full(kaux)],
        out_specs=[tiles(HEAD_DIM, gq), tiles(HEAD_DIM, KEY_CHUNK), tiles(HEAD_DIM, KEY_CHUNK),
                   pl.BlockSpec((1, GATE_PAD, tm), lambda i: (i // steps, 0, i % steps)),
                   keys, keys,
                   pl.BlockSpec((1, tm, 2 * NSA_KV_HEADS * HEAD_DIM), lambda i: (i // steps, i % steps, 0)),
                   row(cw), row(3 * cw)] + ([row(d)] if add else []),
        out_shape=[sds((batch, NSA_KV_HEADS, t // Q_TILE, HEAD_DIM, gq), BF16), v_shape, v_shape,
                   sds((batch, GATE_PAD, t), F32), k_shape, k_shape,
                   sds((batch, t, 2 * NSA_KV_HEADS * HEAD_DIM), F32), sds((n, cw), F32), sds((n, 3 * cw), F32)]
        + ([sds((n, d), F32)] if add else []),
        compiler_params=_params("parallel"),
        name="inproj",
    )(*((h, y) if add else (h,)), g, wq_t, wv_t, wg_t, wk, wn, kaux)
    return (outs[-1], *outs[:-1]) if add else (h, *outs)


def _gelu_tanh(x):
    return 0.5 * x * (1.0 + jnp.tanh(0.7978845608028654 * (x + 0.044715 * x * x * x)))


def _compress_kernel(src_ref, pe_ref, w1_ref, w1bd_ref, w2k_ref, w2vt_ref, caux_ref, kc_ref, vct_ref):
    kind = pl.program_id(1)
    ncp = kc_ref.shape[2]
    hidden = w1_ref.shape[2]
    pieces = [src_ref[0, pl.ds(l, ncp, stride=CMP_STRIDE), :].astype(BF16) for l in range(CMP_STRIDE)]
    chunk = jnp.concatenate(pieces, axis=1)
    first = _dot(chunk, w1bd_ref[0, 0])
    second = _dot(chunk, w1bd_ref[0, 1])
    bias = _dot(pe_ref[0], w1_ref[0])[0:1, :]
    hid = first + pltpu.roll(second, ncp - 1, 0) + jnp.concatenate([bias] * NSA_KV_HEADS, axis=1)
    row = _iota((ncp, 1), 0)
    act = jnp.where(row < ncp - 1, _gelu_tanh(hid), 0.0).astype(BF16)
    for g in range(NSA_KV_HEADS):
        act_g = act[:, g * hidden:(g + 1) * hidden]

        @pl.when(kind == 0)
        def _():
            kc_ref[0, g] = (_dot(act_g, w2k_ref[...]) + caux_ref[...]).astype(BF16)

        @pl.when(kind == 1)
        def _():
            vct_ref[0, g * HEAD_DIM:(g + 1) * HEAD_DIM, :] = _dot_nt(w2vt_ref[...], act_g).astype(BF16)


def _compress(src, pe, w1, w1bd, w2k, w2vt, caux):
    b, t, _ = src.shape
    ncp = t // CMP_STRIDE
    gd = NSA_KV_HEADS * HEAD_DIM
    full = lambda a: pl.BlockSpec(a.shape, lambda bi, k: (0,) * a.ndim)
    per_kind = lambda a: pl.BlockSpec((1,) + a.shape[1:], lambda bi, k: (k,) + (0,) * (a.ndim - 1))
    return pl.pallas_call(
        _compress_kernel,
        grid=(b, 2),
        in_specs=[pl.BlockSpec((1, t, gd), lambda bi, k: (bi, 0, k)),
                  per_kind(pe), per_kind(w1), per_kind(w1bd), full(w2k), full(w2vt), full(caux)],
        out_specs=[pl.BlockSpec((1, NSA_KV_HEADS, ncp, LANES), lambda bi, k: (bi, 0, 0, 0)),
                   pl.BlockSpec((1, gd, ncp), lambda bi, k: (bi, 0, 0))],
        out_shape=[jax.ShapeDtypeStruct((b, NSA_KV_HEADS, ncp, LANES), BF16),
                   jax.ShapeDtypeStruct((b, gd, ncp), BF16)],
        compiler_params=_params("parallel", "arbitrary"),
        name="compress",
    )(src, pe, w1, w1bd, w2k, w2vt, caux)


SOFTMAX_FLOOR = -1e29
TAKEN = -3e38
ATTN_BATCH = 5
BLOCKS_PER_CHUNK = KEY_CHUNK // SLC_BLOCK
AUX_COLS = 16
AUX_SLOPE = BLOCKS_PER_CHUNK
ONES_ROWS = 16
CMP_ROWS_STEP = 128
CMP_TILES = 2
CMP_AUX_SPLIT = 128


def _slope(g, r):
    if isinstance(g, int):
        return jnp.float32(ALIBI_SLOPES[g * NSA_GROUP + r])
    s = jnp.float32(ALIBI_SLOPES[r])
    for gi in range(1, NSA_KV_HEADS):
        s = jnp.where(g == gi, jnp.float32(ALIBI_SLOPES[gi * NSA_GROUP + r]), s)
    return s


def _nsa_cmp_kernel(qt_ref, kc_ref, vct_ref, poolt_ref, pair_ref, earlier_ref, oct_ref, selt_ref, list_ref,
                    *, n_sel):
    first_tile = pl.program_id(1) * CMP_TILES
    qt = selt_ref.shape[4]
    ncp = kc_ref.shape[2]
    n_slc = poolt_ref.shape[0]
    gq = qt_ref.shape[4]
    units = [(g, j) for j in range(CMP_TILES) for g in range(NSA_KV_HEADS)]
    times = [(first_tile + j) * qt + _iota((1, qt), 1) for j in range(CMP_TILES)]
    aux_row = _iota((AUX_COLS, gq), 0)
    col_head = _iota((1, gq), 1) // qt
    pad_rows = jnp.zeros((kc_ref.shape[3] - qt_ref.shape[3] - AUX_COLS, gq), BF16)

    def weights(g, j):
        slope_cols = jnp.zeros((1, gq), F32)
        for r in range(NSA_GROUP):
            slope_cols = jnp.where(col_head == r, _slope(g, r), slope_cols)
        aux = jnp.where(aux_row == 0, slope_cols * (CMP_STRIDE * CMP_AUX_SPLIT),
                        jnp.where(aux_row == 1, slope_cols * CMP_STRIDE, 0.0))
        return jnp.concatenate([qt_ref[0, g, j], aux.astype(BF16), pad_rows], axis=0)

    def importance(g, j, nr, nb):
        t = times[j]
        s = _dot(kc_ref[0, g, :nr, :], weights(g, j))
        edge = min(nr, 2 * CMP_ROWS_STEP)
        cmp_end = ((nr - edge) + _iota((edge, 1), 0)) * CMP_STRIDE + (CMP_LEN - 1)
        visible = cmp_end <= t
        vct = vct_ref[0, g * HEAD_DIM:(g + 1) * HEAD_DIM, :nr]
        psum = jnp.zeros((nr, qt), F32)
        for r in range(NSA_GROUP):
            cols = slice(r * qt, (r + 1) * qt)
            sr = s[:, cols]
            tail = jnp.where(visible, sr[nr - edge:], NEG_INF)
            sr = tail if edge == nr else jnp.concatenate([sr[:nr - edge], tail], axis=0)
            m = jnp.maximum(jnp.max(sr, axis=0, keepdims=True), SOFTMAX_FLOOR)
            e = jnp.exp(sr - m)
            l = jnp.sum(e, axis=0, keepdims=True)
            inv = jnp.where(l > 0.0, 1.0 / l, 0.0)
            oct_ref[0, g, j, :, cols] = (_dot(vct, e.astype(BF16)) * inv).astype(BF16)
            psum = psum + e * inv
        return _dot(poolt_ref[:nb, :nr], psum.astype(BF16))

    def visible_prefix(nr):
        nb = min(n_slc, nr * CMP_STRIDE // SLC_BLOCK)
        imp = jnp.concatenate([importance(g, j, nr, nb) for g, j in units], axis=1)
        blk = _iota((nb, 1), 0)
        cur = jnp.concatenate([times[j] // SLC_BLOCK for _, j in units], axis=1)
        forced = (blk == 0) | (blk == cur) | (blk == cur - 1)
        score = jnp.where(forced, TAKEN, jnp.where(blk <= cur, imp, NEG_INF))
        n_forced = 1 + jnp.where(cur >= 1, 1, 0) + jnp.where(cur >= 2, 1, 0)
        blk_f = blk.astype(F32)

        def take_one(score, active):
            m = jnp.max(score, axis=0, keepdims=True)
            first = jnp.min(jnp.where(score == m, blk_f, F32(1e9)), axis=0, keepdims=True)
            hit = (blk_f == first) if active is None else ((blk_f == first) & active)
            return jnp.where(hit, TAKEN, score)

        common_rounds = max(n_sel - 3, 0)
        for _ in range(common_rounds):
            score = take_one(score, None)

        def early_rounds(score):
            for k in range(common_rounds, n_sel - 1):
                score = take_one(score, n_sel - n_forced > k)
            return score

        score = lax.cond(first_tile * qt < 2 * SLC_BLOCK, early_rounds, lambda sc: sc, score)
        for u, (g, j) in enumerate(units):
            sel_g = score[:, u * qt:(u + 1) * qt] == TAKEN
            selt_ref[0, g, j, :nb, :] = jnp.where(sel_g, 0.0, NEG_INF)
            if nb < n_slc:
                selt_ref[0, g, j, nb:, :] = jnp.full((n_slc - nb, qt), NEG_INF, F32)
            count = _dot_nt(jnp.ones((8, qt), BF16), jnp.where(sel_g, 1.0, 0.0).astype(BF16))
            used = jnp.where(count > 0.0, 1.0, 0.0).astype(BF16)
            chunk = _iota((8, LANES), 1)
            flagged = jnp.where((_dot(used, pair_ref[:nb, :]) > 0.0) & (chunk < first_tile + j), 1.0, 0.0).astype(BF16)
            place = _dot(flagged, earlier_ref[...])
            lands = (_iota((LANES, LANES), 0).astype(F32) == place[0:1, :]) & (flagged[0:1, :] > 0)
            listed = _dot_nt(chunk.astype(BF16), jnp.where(lands, 1.0, 0.0).astype(BF16))
            n_listed = _dot(flagged, jnp.ones((LANES, LANES), BF16))
            list_ref[0, g, j] = jnp.where(_iota((8, LANES), 0) < 4, listed, n_listed).astype(jnp.int32)

    step = min(CMP_ROWS_STEP, ncp)
    rows_needed = jnp.minimum(((first_tile + CMP_TILES) * qt - CMP_LEN) // CMP_STRIDE + 1, ncp)
    n_steps = (rows_needed + step - 1) // step
    for k in range(ncp // step):
        pl.when(n_steps == k + 1)(functools.partial(visible_prefix, (k + 1) * step))


def _nsa_cmp(q_t, kc, vct, poolt_m, pair_m, earlier_m, n_sel):
    b, ng, n_tiles, qrows, qcols = q_t.shape
    ncp = kc.shape[2]
    n_slc = poolt_m.shape[0]
    const = lambda shape: pl.BlockSpec(shape, lambda bi, i: (0,) * len(shape))
    tile5 = lambda rows, cols: pl.BlockSpec((1, ng, CMP_TILES, rows, cols), lambda bi, i: (bi, 0, i, 0, 0))
    return pl.pallas_call(
        functools.partial(_nsa_cmp_kernel, n_sel=n_sel),
        grid=(b, n_tiles // CMP_TILES),
        in_specs=[
            tile5(qrows, qcols),
            pl.BlockSpec((1, ng, ncp, kc.shape[3]), lambda bi, i: (bi, 0, 0, 0)),
            pl.BlockSpec((1, ng * HEAD_DIM, ncp), lambda bi, i: (bi, 0, 0)),
            const(poolt_m.shape), const(pair_m.shape), const(earlier_m.shape),
        ],
        out_specs=[tile5(HEAD_DIM, qcols), tile5(n_slc, Q_TILE), tile5(8, LANES)],
        out_shape=[
            jax.ShapeDtypeStruct((b, ng, n_tiles, HEAD_DIM, qcols), BF16),
            jax.ShapeDtypeStruct((b, ng, n_tiles, n_slc, Q_TILE), F32),
            jax.ShapeDtypeStruct((b, ng, n_tiles, 8, LANES), jnp.int32),
        ],
        compiler_params=_params("parallel", "parallel"),
        name="nsa_compressed",
    )(q_t, kc, vct, poolt_m, pair_m, earlier_m)


def _nsa_slc_kernel(lists_ref, counts_ref, qt_ref, ks_ref, vst_ref, kw_ref, vwt_ref, selt_ref, oct_ref, gt_ref,
                    out_ref, m_ref, l_ref, acc_ref, ow_ref, sa_ref, sb_ref, sw_ref):
    bi = pl.program_id(0)
    g = pl.program_id(1)
    i = pl.program_id(2)
    n_tiles = pl.num_programs(2)
    qt = out_ref.shape[1]
    q_rows = qt_ref[0, 0, 0]
    gq = q_rows.shape[1]
    start = i * qt
    lane_f = _iota((1, qt), 1).astype(F32)
    key_in_chunk = _iota((KEY_CHUNK, qt), 0)
    query_in_tile = _iota((KEY_CHUNK, qt), 1)

    aux_row = _iota((AUX_COLS, gq), 0)
    col_head = _iota((1, gq), 1) // qt
    slope_cols = jnp.zeros((1, gq), F32)
    for r in range(NSA_GROUP):
        slope_cols = jnp.where(col_head == r, _slope(g, r), slope_cols)
    aux_base = jnp.where(aux_row == AUX_SLOPE, slope_cols, 0.0)
    pad_rows = jnp.zeros((ks_ref.shape[3] - q_rows.shape[0] - AUX_COLS, gq), BF16)

    def scores_of(slots):
        return [_dot(k, jnp.concatenate([q_rows, aux.astype(BF16), pad_rows], axis=0))
                for k, aux, _, _, _ in slots]

    def values_of(slots):
        values = jnp.concatenate([v for _, _, v, _, _ in slots], axis=1)
        return jnp.concatenate([values, jnp.ones((ONES_ROWS, values.shape[1]), BF16)], axis=0)

    def softmax_step(slots, scores, v_cat, r, m_old):
        cols = slice(r * qt, (r + 1) * qt)
        srs, tops = [], []
        for j, (_, _, _, shift, mask) in enumerate(slots):
            sr = scores[j, :, cols] if hasattr(scores, "at") else scores[j][:, cols]
            sr = sr if mask is None else jnp.where(mask, sr, NEG_INF)
            srs.append(sr)
            tops.append(jnp.max(sr, axis=0, keepdims=True) + shift[r])
        m_new = functools.reduce(jnp.maximum, tops, m_old)
        ps = [jnp.exp((sr - (m_new - slot[3][r])).astype(BF16)) for slot, sr in zip(slots, srs)]
        weighted = _dot(v_cat, jnp.concatenate(ps, axis=0))
        return m_new, weighted[HEAD_DIM:HEAD_DIM + 1], weighted[:HEAD_DIM]

    def shifts(dist0, ok):
        rows = [-_slope(g, r) * (dist0 + lane_f) for r in range(NSA_GROUP)]
        return rows if ok is None else [jnp.where(ok, row, NEG_INF) for row in rows]

    tile_id = (bi * NSA_KV_HEADS + g) * n_tiles + i
    n_listed = counts_ref[tile_id]

    def selected_slot(c, ok, mask):
        at = pl.multiple_of(c * KEY_CHUNK, KEY_CHUNK)
        bias = selt_ref[0, 0, 0, pl.ds(c * BLOCKS_PER_CHUNK, BLOCKS_PER_CHUNK), :]
        aux = aux_base
        for blk in range(BLOCKS_PER_CHUNK):
            aux = jnp.where(aux_row == blk, jnp.concatenate([bias[blk:blk + 1]] * NSA_GROUP, axis=1), aux)
        return (ks_ref[0, 0, pl.ds(at, KEY_CHUNK), :], aux, vst_ref[0, 0, c],
                shifts((start - c * KEY_CHUNK).astype(F32), ok), mask)

    def listed_slot(idx):
        ok = idx < n_listed
        c = jnp.where(ok, lists_ref[tile_id * LANES + jnp.minimum(idx, jnp.maximum(n_listed - 1, 0))], 0)
        return selected_slot(c, ok, None)

    floor = jnp.full((1, qt), SOFTMAX_FLOOR, F32)

    n_back = WINDOW // KEY_CHUNK
    slots = []
    for j in range(n_back + 1):
        cs = start - WINDOW + j * KEY_CHUNK
        chunk = jnp.maximum(cs, 0) // KEY_CHUNK
        at = pl.multiple_of(chunk * KEY_CHUNK, KEY_CHUNK)
        mask = (query_in_tile < key_in_chunk) if j == 0 else (
            (key_in_chunk <= query_in_tile) if j == n_back else None)
        slots.append((kw_ref[0, 0, pl.ds(at, KEY_CHUNK), :], aux_base, vwt_ref[0, 0, chunk],
                      shifts(F32(WINDOW - j * KEY_CHUNK), cs >= 0), mask))
    window_slots = slots
    first_slots = [selected_slot(i, None, key_in_chunk <= query_in_tile)] + [
        listed_slot(j) for j in range(ATTN_BATCH - 1)]

    def listed_batch(n):
        return [listed_slot(ATTN_BATCH - 1 + n * ATTN_BATCH + j) for j in range(ATTN_BATCH)]

    def issue(slots, dst_ref):
        for j, s in enumerate(scores_of(slots)):
            dst_ref[j] = s

    issue(window_slots, sw_ref)
    issue(first_slots, sb_ref)
    v_cat = values_of(window_slots)
    for r in range(NSA_GROUP):
        _, total, weighted = softmax_step(window_slots, sw_ref, v_cat, r, floor)
        ow_ref[:, r * qt:(r + 1) * qt] = weighted * jnp.where(total > 0.0, 1.0 / total, 0.0)

    def consume(slots, src_ref):
        v_cat = values_of(slots)
        for r in range(NSA_GROUP):
            cols = slice(r * qt, (r + 1) * qt)
            m_old = m_ref[r]
            m_new, total, weighted = softmax_step(slots, src_ref, v_cat, r, m_old)
            alpha = jnp.exp(m_old - m_new)
            l_ref[r] = alpha * l_ref[r] + total
            acc_ref[:, cols] = alpha * acc_ref[:, cols] + weighted
            m_ref[r] = m_new

    issue(listed_batch(0), sa_ref)
    v_cat = values_of(first_slots)
    for r in range(NSA_GROUP):
        m_ref[r], l_ref[r], acc_ref[:, r * qt:(r + 1) * qt] = softmax_step(first_slots, sb_ref, v_cat, r, floor)

    def batch_pair(it, carry):
        first, second, third = (listed_batch(2 * it + n) for n in range(3))
        issue(second, sb_ref)
        consume(first, sa_ref)
        issue(third, sa_ref)
        consume(second, sb_ref)
        return carry

    n_rest = jnp.maximum(n_listed - (ATTN_BATCH - 1), 0)
    n_batches = (n_rest + ATTN_BATCH - 1) // ATTN_BATCH
    lax.fori_loop(0, n_batches // 2, batch_pair, 0)

    @pl.when(n_batches % 2 == 1)
    def _():
        consume(listed_batch(n_batches - 1), sa_ref)

    def finalize(r):
        l = l_ref[r]
        return acc_ref[:, r * qt:(r + 1) * qt] * jnp.where(l > 0.0, 1.0 / l, 0.0)

    outs = []
    for r in range(NSA_GROUP):
        cols = slice(r * qt, (r + 1) * qt)
        col = (g * NSA_GROUP + r) * NSA_BRANCHES
        gate = lambda br: jax.nn.sigmoid(gt_ref[0, pl.ds(col + br, 1), :])
        outs.append(gate(0) * oct_ref[0, 0, 0, :, cols].astype(F32) + gate(1) * finalize(r)
                    + gate(2) * ow_ref[:, cols])
    out_ref[0] = jnp.concatenate(outs, axis=0).T.astype(out_ref.dtype)


def _nsa_slc(lists, counts, q_t, ks, vst, kw, vwt, selt, oct, gates_t):
    b, _, t, kw_cols = ks.shape
    _, _, n_tiles, qrows, qcols = q_t.shape
    n_slc = selt.shape[3]
    n_chunks = vst.shape[2]
    gw = NSA_GROUP * HEAD_DIM
    once = dict(pipeline_mode=pl.Buffered(1))
    k_spec = pl.BlockSpec((1, 1, t, kw_cols), lambda bi, g, i, ls, ns: (bi, g, 0, 0), **once)
    vt_spec = pl.BlockSpec((1, 1, n_chunks, HEAD_DIM, KEY_CHUNK), lambda bi, g, i, ls, ns: (bi, g, 0, 0, 0), **once)
    tile5 = lambda rows, cols: pl.BlockSpec((1, 1, 1, rows, cols), lambda bi, g, i, ls, ns: (bi, g, i, 0, 0))
    grid_spec = pltpu.PrefetchScalarGridSpec(
        num_scalar_prefetch=2,
        grid=(b, NSA_KV_HEADS, n_tiles),
        in_specs=[
            tile5(qrows, qcols),
            k_spec, vt_spec, k_spec, vt_spec,
            tile5(n_slc, Q_TILE),
            tile5(HEAD_DIM, qcols),
            pl.BlockSpec((1, GATE_PAD, Q_TILE), lambda bi, g, i, ls, ns: (bi, 0, i)),
        ],
        out_specs=pl.BlockSpec((1, Q_TILE, gw), lambda bi, g, i, ls, ns: (bi, i, g)),
        scratch_shapes=[pltpu.VMEM((NSA_GROUP, 1, Q_TILE), F32), pltpu.VMEM((NSA_GROUP, 1, Q_TILE), F32),
                        pltpu.VMEM((HEAD_DIM, qcols), F32), pltpu.VMEM((HEAD_DIM, qcols), F32),
                        pltpu.VMEM((ATTN_BATCH, KEY_CHUNK, qcols), F32),
                        pltpu.VMEM((ATTN_BATCH, KEY_CHUNK, qcols), F32),
                        pltpu.VMEM((WINDOW // KEY_CHUNK + 1, KEY_CHUNK, qcols), F32)],
    )
    return pl.pallas_call(
        _nsa_slc_kernel,
        grid_spec=grid_spec,
        out_shape=jax.ShapeDtypeStruct((b, t, NSA_HEADS * HEAD_DIM), BF16),
        compiler_params=_params("parallel", "parallel", "parallel"),
        name="nsa_selected_window",
    )(lists, counts, q_t, ks, vst, kw, vwt, selt, oct, gates_t)


POOL_HALO = 16
CONV_HALO = 8


def _merge_kernel(h_ref, g_ref, nsa_ref, pool_ref, pool_halo_ref, conv_ref, conv_halo_ref,
                  wmg_ref, wnsa_ref, pool_bd_ref, pool_scale_ref, wpool_ref, convw_ref, wconv_ref, wo_ref,
                  g2_ref, wr_ref, br_ref, before_ref,
                  out_ref, xn_ref, comb_ref, count_ref, pool_ext, conv_ext, *, seq_len):
    i = pl.program_id(0)
    tm, d = h_ref.shape
    cw = pool_ref.shape[1]
    pos0 = (i * tm) % seq_len
    keep_halo = jnp.where(pos0 == 0, 0.0, 1.0)
    pos = pos0 + _iota((tm, 1), 0)

    u = pool_ref[...]
    pool_ext[0:POOL_HALO, :] = pool_halo_ref[...] * keep_halo
    pool_ext[POOL_HALO:, :] = u
    lane_group = _iota((1, cw), 1) // (cw // len(POOL_WINDOWS))
    total = u
    mean = jnp.zeros_like(u)
    done = 1
    for gi, win in enumerate(POOL_WINDOWS):
        for k in range(done, win):
            total = total + pool_ext[POOL_HALO - k:POOL_HALO - k + tm, :]
        done = win
        cnt = jnp.minimum(pos + 1, win).astype(F32)
        mean = jnp.where(lane_group == gi, total / cnt, mean)
    pooled = (mean - u).astype(BF16)
    mixed = _dot(pooled, pool_bd_ref[...]) * pool_scale_ref[...]
    y_pool = _dot(mixed.astype(BF16), wpool_ref[...])

    ch = conv_ref[:, 0:cw]
    cb = conv_ref[:, cw:2 * cw]
    cc = conv_ref[:, 2 * cw:3 * cw]
    conv_ext[0:CONV_HALO, :] = conv_halo_ref[:, 0:cw] * conv_halo_ref[:, 2 * cw:3 * cw] * keep_halo
    conv_ext[CONV_HALO:, :] = cc * ch
    y = jnp.zeros((tm, cw), F32)
    for k in range(CONV_K):
        off = CONV_HALO - (CONV_K - 1) + k
        y = y + convw_ref[k:k + 1, :] * conv_ext[off:off + tm, :]
    y_conv = _dot((cb * y).astype(BF16), wconv_ref[...])

    y_nsa = _dot(nsa_ref[...], wnsa_ref[...])

    h = h_ref[...]
    xn = _rms_norm(h, g_ref[...]).astype(BF16)
    merged = jnp.zeros((tm, d), F32)
    for br, y_br in enumerate((y_nsa, y_pool, y_conv)):
        mg = jax.nn.sigmoid(_dot(xn, wmg_ref[:, br * d:(br + 1) * d]))
        merged = merged + mg * y_br
    h_new = h + _dot(merged.astype(BF16), wo_ref[...])
    out_ref[...] = h_new
    _route_tile(h_new, g2_ref, wr_ref, br_ref, before_ref, xn_ref, comb_ref, count_ref)


def _merge(h, g, nsa, pool_u, conv, wmg, wnsa, pool_bd, pool_scale, wpool, convw, wconv, wo,
           g2, wr, br, before, seq_len):
    n, d = h.shape
    tm = TOKEN_TILE
    cw = pool_u.shape[1]
    row = lambda width: pl.BlockSpec((tm, width), lambda i: (i, 0))
    full = lambda a: pl.BlockSpec(a.shape, lambda i: (0,) * a.ndim)
    halo = lambda rows, width: pl.BlockSpec(
        (rows, width), lambda i: (jnp.maximum(i * (tm // rows) - 1, 0), 0))
    return pl.pallas_call(
        functools.partial(_merge_kernel, seq_len=seq_len),
        grid=(n // tm,),
        in_specs=[row(d), full(g), row(nsa.shape[1]), row(cw), halo(POOL_HALO, cw),
                  row(conv.shape[1]), halo(CONV_HALO, conv.shape[1]),
                  full(wmg), full(wnsa), full(pool_bd), full(pool_scale), full(wpool), full(convw),
                  full(wconv), full(wo), full(g2), full(wr), full(br), full(before)],
        out_specs=[row(d), row(d + ROUTER_PAD), row(ROUTER_PAD), pl.BlockSpec((8, ROUTER_PAD), lambda i: (0, 0))],
        out_shape=[jax.ShapeDtypeStruct((n, d), F32), jax.ShapeDtypeStruct((n, d + ROUTER_PAD), BF16),
                   jax.ShapeDtypeStruct((n, ROUTER_PAD), F32), jax.ShapeDtypeStruct((8, ROUTER_PAD), F32)],
        scratch_shapes=[pltpu.VMEM((tm + POOL_HALO, cw), F32), pltpu.VMEM((tm + CONV_HALO, cw), F32)],
        compiler_params=_params("arbitrary"),
        name="merge",
    )(h, g, nsa, pool_u, pool_u, conv, conv, wmg, wnsa, pool_bd, pool_scale, wpool, convw, wconv, wo,
      g2, wr, br, before)


def _route(logits):
    lane = _iota(logits.shape, 1)
    lane_f = lane.astype(F32)
    big = F32(1e9)
    is_group = lane < N_EXPERT_GROUPS
    gl = jnp.where(is_group, logits, NEG_INF)
    g_max = jnp.max(gl, axis=1, keepdims=True)
    g_sel = jnp.min(jnp.where(gl == g_max, lane_f, big), axis=1, keepdims=True)
    g_prob = 1.0 / jnp.sum(jnp.where(is_group, jnp.exp(gl - g_max), 0.0), axis=1, keepdims=True)
    lo = N_EXPERT_GROUPS + EXPERTS_PER_GROUP * g_sel
    in_group = (lane_f >= lo) & (lane_f < lo + EXPERTS_PER_GROUP)
    el = jnp.where(in_group, logits, NEG_INF)
    v1 = jnp.max(el, axis=1, keepdims=True)
    i1 = jnp.min(jnp.where((el == v1) & in_group, lane_f, big), axis=1, keepdims=True)
    el2 = jnp.where(lane_f == i1, NEG_INF, el)
    rest = in_group & (lane_f != i1)
    v2 = jnp.max(el2, axis=1, keepdims=True)
    i2 = jnp.min(jnp.where((el2 == v2) & rest, lane_f, big), axis=1, keepdims=True)
    e2 = jnp.exp(v2 - v1)
    w1 = g_prob / (1.0 + e2)
    w2 = g_prob * e2 / (1.0 + e2)
    return jnp.where(lane_f == i1, w1, 0.0) + jnp.where(lane_f == i2, w2, 0.0), g_sel


GROUP_LANE = N_EXPERT_GROUPS + N_EXPERTS


RANK_LANE = GROUP_LANE + 1


def _route_tile(h, g_ref, wr_ref, br_ref, before_ref, xn_ref, comb_ref, count_ref):
    @pl.when(pl.program_id(0) == 0)
    def _():
        count_ref[...] = jnp.zeros(count_ref.shape, F32)

    xn = _rms_norm(h, g_ref[...])
    xn_hi = xn.astype(BF16)
    xn_lo = (xn - xn_hi.astype(F32)).astype(BF16)
    logits = (_dot(xn_hi, wr_ref[0]) + (_dot(xn_hi, wr_ref[1]) + _dot(xn_lo, wr_ref[0]))) + br_ref[...]
    comb, g_sel = _route(logits)
    lane = _iota(comb.shape, 1)
    chose = jnp.where(lane.astype(F32) == g_sel, 1.0, 0.0)
    earlier = _dot(before_ref[...], chose.astype(BF16)) + count_ref[0:1, :]
    rank = jnp.sum(chose * earlier, axis=1, keepdims=True)
    count_ref[0:1, :] = count_ref[0:1, :] + jnp.sum(chose, axis=0, keepdims=True)
    comb_ref[...] = jnp.where(lane == GROUP_LANE, g_sel, jnp.where(lane == RANK_LANE, rank, comb))
    first = N_EXPERT_GROUPS + EXPERTS_PER_GROUP * g_sel
    local = jnp.zeros(comb.shape, F32)
    for e in range(EXPERTS_PER_GROUP):
        c_e = jnp.sum(jnp.where(lane.astype(F32) == first + e, comb, 0.0), axis=1, keepdims=True)
        local = jnp.where((lane == e) | (lane == EXPERTS_PER_GROUP + e), c_e, local)
    local_hi = local.astype(BF16)
    d = xn_hi.shape[1]
    xn_ref[:, :d] = xn_hi
    xn_ref[:, d:] = jnp.where(lane < EXPERTS_PER_GROUP, local_hi, (local - local_hi.astype(F32)).astype(BF16))


def _experts_kernel(tile_group_ref, n_active_ref, x_ref, w1_ref, w3_ref, w2_ref, out_ref, acc_ref):
    i = pl.program_id(0)

    @pl.when(i < n_active_ref[0])
    def _():
        d = out_ref.shape[1]
        x = x_ref[:, :d]
        comb = x_ref[:, d:].astype(F32)
        lane = _iota(comb.shape, 1)
        for e in range(EXPERTS_PER_GROUP):
            c_e = jnp.sum(jnp.where((lane == e) | (lane == EXPERTS_PER_GROUP + e), comb, 0.0),
                          axis=1, keepdims=True)
            a = (jax.nn.silu(_dot(x, w1_ref[0, 0, e].astype(BF16)))
                 * _dot(x, w3_ref[0, 0, e].astype(BF16))) * c_e
            y = _dot(a.astype(BF16), w2_ref[0, 0, e].astype(BF16))
            if e == 0:
                acc_ref[...] = y
            else:
                acc_ref[...] += y
        out_ref[...] = acc_ref[...].astype(out_ref.dtype)

    @pl.when(i >= n_active_ref[0])
    def _():
        out_ref[...] = jnp.zeros(out_ref.shape, out_ref.dtype)


def _experts(tile_group, n_active, x_sorted, w1, w3, w2, layer):
    ns = x_sorted.shape[0]
    d = w1.shape[3]
    tm = MOE_SORT_TILE
    group_w = lambda w: pl.BlockSpec((1, 1) + w.shape[2:], lambda i, tg, na: (layer, tg[i], 0, 0, 0),
                                     pipeline_mode=pl.Buffered(1))
    grid_spec = pltpu.PrefetchScalarGridSpec(
        num_scalar_prefetch=2,
        grid=(ns // tm,),
        in_specs=[
            pl.BlockSpec((tm, x_sorted.shape[1]), lambda i, tg, na: (i, 0)),
            group_w(w1), group_w(w3), group_w(w2),
        ],
        out_specs=pl.BlockSpec((tm, d), lambda i, tg, na: (i, 0)),
        scratch_shapes=[pltpu.VMEM((tm, d), F32)],
    )
    return pl.pallas_call(
        _experts_kernel,
        grid_spec=grid_spec,
        out_shape=jax.ShapeDtypeStruct((ns, d), BF16),
        compiler_params=_params("arbitrary"),
        name="experts",
    )(tile_group, n_active, x_sorted, w1, w3, w2)


def _residual_kernel(h_ref, y_ref, gf_ref, out_ref, *, final_norm):
    out = h_ref[...] + y_ref[...].astype(F32)
    out_ref[...] = _rms_norm(out, gf_ref[...]) if final_norm else out


def _residual(h, y, gf, final_norm):
    n, d = h.shape
    tm = MOE_TILE
    row = pl.BlockSpec((tm, d), lambda i: (i, 0))
    return pl.pallas_call(
        functools.partial(_residual_kernel, final_norm=final_norm),
        grid=(n // tm,),
        in_specs=[row, row, pl.BlockSpec(gf.shape, lambda i: (0, 0))],
        out_specs=row,
        out_shape=jax.ShapeDtypeStruct((n, d), F32),
        compiler_params=_params("parallel"),
        name="residual",
    )(h, y, gf)


def _group_sort_plan(group_id, rank, counts, tile):
    n = group_id.shape[0]
    n_slots = n + N_EXPERT_GROUPS * tile
    padded = (counts + tile - 1) // tile * tile
    ends = jnp.cumsum(padded)
    slot = (ends - padded)[group_id] + rank
    source = jnp.zeros((n_slots,), jnp.int32).at[slot].set(jnp.arange(n, dtype=jnp.int32))
    tile_start = jnp.arange(n_slots // tile, dtype=jnp.int32) * tile
    tile_group = jnp.minimum(jnp.searchsorted(ends, tile_start, side="right"), N_EXPERT_GROUPS - 1)
    return slot, source, tile_group.astype(jnp.int32), (ends[-1:] // tile).astype(jnp.int32)


def _moe(xn, comb, counts, w1, w3, w2, layer):
    as_int = lambda a: a.astype(jnp.int32)
    slot, source, tile_group, n_active = _group_sort_plan(
        as_int(comb[:, GROUP_LANE]), as_int(comb[:, RANK_LANE]), as_int(counts[0, :N_EXPERT_GROUPS]),
        MOE_SORT_TILE)
    grouped = lambda w: w.reshape((w.shape[0], N_EXPERT_GROUPS, EXPERTS_PER_GROUP) + w.shape[2:])
    rows = lambda a, idx: jnp.take(a, idx, axis=0, mode="clip")
    y_sorted = _experts(tile_group, n_active, rows(xn, source), grouped(w1), grouped(w3), grouped(w2), layer)
    return rows(y_sorted, slot)


def _selection_constants(seq_len):
    ncp = seq_len // CMP_STRIDE
    n_slc = seq_len // SLC_BLOCK
    ratio = SLC_BLOCK // CMP_STRIDE
    lead = CMP_LEN // CMP_STRIDE - 1
    c = np.arange(ncp)[:, None]
    j = np.arange(n_slc)[None, :]
    pool_m = ((c >= ratio * j - lead) & (c < ratio * j + ratio)).astype(np.float32)
    blocks_per_chunk = KEY_CHUNK // SLC_BLOCK
    n_chunks = seq_len // KEY_CHUNK
    pair_m = np.zeros((n_slc, LANES * ((n_chunks + LANES - 1) // LANES)), np.float32)
    pair_m[np.arange(n_slc), np.arange(n_slc) // blocks_per_chunk] = 1.0
    earlier_m = np.triu(np.ones((LANES, LANES), np.float32), 1)
    key_aux = np.zeros((TOKEN_TILE, LANES), np.float32)
    in_chunk = np.arange(TOKEN_TILE) % KEY_CHUNK
    key_aux[np.arange(TOKEN_TILE), HEAD_DIM + in_chunk // SLC_BLOCK] = 1.0
    key_aux[:, HEAD_DIM + AUX_SLOPE] = in_chunk
    cmp_aux = np.zeros((ncp, LANES), np.float32)
    cmp_aux[:, HEAD_DIM] = np.arange(ncp) // CMP_AUX_SPLIT
    cmp_aux[:, HEAD_DIM + 1] = np.arange(ncp) % CMP_AUX_SPLIT
    as_bf16 = lambda a: jnp.asarray(a, BF16)
    return as_bf16(pool_m.T), as_bf16(pair_m), as_bf16(earlier_m), jnp.asarray(key_aux), jnp.asarray(cmp_aux)


def kernel(x, norm1_g, w_in, cmp_pe, cmp_w1, cmp_w2, w_nsa_proj, pool_w, pool_scale, w_pool_proj, conv_w,
           w_conv_proj, w_o, norm2_g, router_group_w, router_group_b, router_expert_w, router_expert_b,
           expert_w1, expert_w3, expert_w2, final_norm_g):
    b, t, d = x.shape
    n = b * t
    depth = w_in.shape[0]
    dq = NSA_HEADS * HEAD_DIM
    dkv = 6 * NSA_KV_HEADS * HEAD_DIM
    dgate = NSA_HEADS * NSA_BRANCHES
    cw = d // 4
    assert t % TOKEN_TILE == 0 and n % MOE_TILE == 0 and t % (Q_TILE * CMP_TILES) == 0
    n_slc = t // SLC_BLOCK
    n_sel = min(SLC_TOPN, n_slc)
    n_chunks16 = t // CMP_STRIDE
    kvw = NSA_KV_HEADS * HEAD_DIM
    poolt_m, pair_m, earlier_m, key_aux, cmp_aux = _selection_constants(t)
    assert Q_TILE == KEY_CHUNK and AUX_SLOPE < AUX_COLS and KEY_CHUNK <= 256
    assert n_chunks16 <= 256 * CMP_AUX_SPLIT and n_chunks16 % min(CMP_ROWS_STEP, n_chunks16) == 0
    assert pair_m.shape[1] == LANES

    before = jnp.asarray(np.tril(np.ones((TOKEN_TILE, TOKEN_TILE), np.float32), -1), BF16)
    h, y = x.reshape(n, d), None
    for l in range(depth):
        wl = w_in[l]
        o_gate = dq + dkv
        o_pool = o_gate + dgate
        o_merge = o_pool + cw + 3 * cw
        kv_cols = lambda kind: wl[:, dq + kind * kvw:dq + (kind + 1) * kvw]
        wq_t = (wl[:, :dq] * (HEAD_DIM ** -0.5)).T.astype(BF16)
        wv_t = jnp.concatenate([kv_cols(3), kv_cols(5)], axis=1).T.astype(BF16)
        wg_t = jnp.pad(wl[:, o_gate:o_pool], ((0, 0), (0, GATE_PAD - dgate))).T.astype(BF16)
        no_aux = jnp.zeros((d, LANES - HEAD_DIM), F32)
        wk = jnp.concatenate([piece for kind in (2, 4) for gi in range(NSA_KV_HEADS)
                              for piece in (kv_cols(kind)[:, gi * HEAD_DIM:(gi + 1) * HEAD_DIM], no_aux)],
                             axis=1).astype(BF16)
        wn = jnp.concatenate([kv_cols(0), kv_cols(1), wl[:, o_pool:o_merge]], axis=1).astype(BF16)
        wmg = wl[:, o_merge:].astype(BF16)
        pool_bd = jax.scipy.linalg.block_diag(*[pool_w[l, gi] for gi in range(pool_w.shape[1])]).astype(BF16)
        convw = jnp.pad(conv_w[l], ((0, 8 - CONV_K), (0, 0)))
        wr = jnp.pad(jnp.concatenate([router_group_w[l], router_expert_w[l]], axis=1),
                     ((0, 0), (0, ROUTER_PAD - N_EXPERT_GROUPS - N_EXPERTS)))
        wr_hi = wr.astype(BF16)
        wr = jnp.stack([wr_hi, (wr - wr_hi.astype(F32)).astype(BF16)])
        br = jnp.pad(jnp.concatenate([router_group_b[l], router_expert_b[l]]),
                     (0, ROUTER_PAD - N_EXPERT_GROUPS - N_EXPERTS))[None, :]
        pe = jnp.broadcast_to(cmp_pe[l].reshape(2, 1, CMP_LEN * HEAD_DIM), (2, 8, CMP_LEN * HEAD_DIM)).astype(BF16)
        halves = CMP_LEN // CMP_STRIDE
        w1_bd = jnp.einsum("khldc,gq->khlgdqc",
                           cmp_w1[l].reshape(2, halves, CMP_STRIDE, HEAD_DIM, CMP_HIDDEN),
                           jnp.eye(NSA_KV_HEADS, dtype=F32))
        w1_bd = w1_bd.reshape(2, halves, CMP_STRIDE * kvw, NSA_KV_HEADS * CMP_HIDDEN).astype(BF16)
        w2_k = jnp.pad(cmp_w2[l, 0], ((0, 0), (0, LANES - HEAD_DIM))).astype(BF16)
        w2_vt = cmp_w2[l, 1].T.astype(BF16)

        h, q_t, vst, vwt, gates_t, ks, kw, cmp_src, pool_u, conv = _inproj(
            h, y, norm1_g[l][None, :], wq_t, wv_t, wg_t, wk, wn, key_aux, b)
        kc_aux, vc_t = _compress(cmp_src, pe, cmp_w1[l].astype(BF16), w1_bd, w2_k, w2_vt, cmp_aux)
        oc_t, sel_t, listed = _nsa_cmp(q_t, kc_aux, vc_t, poolt_m, pair_m, earlier_m, n_sel)
        nsa = _nsa_slc(listed[:, :, :, 0, :].reshape(-1), listed[:, :, :, 4, 0].reshape(-1),
                       q_t, ks, vst, kw, vwt, sel_t, oc_t, gates_t)
        h, xn, comb, counts = _merge(
            h, norm1_g[l][None, :], nsa.reshape(n, dq), pool_u, conv, wmg,
            w_nsa_proj[l].astype(BF16), pool_bd, pool_scale[l][None, :], w_pool_proj[l].astype(BF16),
            convw, w_conv_proj[l].astype(BF16), w_o[l].astype(BF16), norm2_g[l][None, :], wr, br, before, t)
        y = _moe(xn, comb, counts, expert_w1, expert_w3, expert_w2, l)
    return _residual(h, y, final_norm_g[None, :], final_norm=True).reshape(b, t, d)
```

```python
import functools

import jax
import jax.numpy as jnp
import numpy as np
from jax import lax
from jax.experimental import pallas as pl
from jax.experimental.pallas import tpu as pltpu

F32 = jnp.float32
BF16 = jnp.bfloat16

HEAD_DIM = 64
NSA_HEADS = 8
NSA_KV_HEADS = 2
NSA_GROUP = NSA_HEADS // NSA_KV_HEADS
CMP_LEN = 32
CMP_STRIDE = 16
CMP_HIDDEN = 4 * HEAD_DIM
SLC_BLOCK = 64
SLC_TOPN = 16
WINDOW = 512
NSA_BRANCHES = 3
POOL_WINDOWS = (2, 4, 8, 16)
CONV_K = 3
N_EXPERT_GROUPS = 4
EXPERTS_PER_GROUP = 8
N_EXPERTS = N_EXPERT_GROUPS * EXPERTS_PER_GROUP
RMS_EPS = 1e-6
NEG_INF = -1e30
ALIBI_SLOPES = tuple(float(2.0 ** (-8.0 * (h + 1) / NSA_HEADS)) for h in range(NSA_HEADS))

LANES = 128
VMEM_LIMIT = 56 * 1024 * 1024
TOKEN_TILE = 512
MOE_TILE = 1024
MOE_SORT_TILE = 512
Q_TILE = 128
KEY_CHUNK = 128
GATE_PAD = LANES
ROUTER_PAD = LANES


def _params(*semantics):
    return pltpu.CompilerParams(dimension_semantics=semantics, vmem_limit_bytes=VMEM_LIMIT)


def _dot(a, b):
    return jnp.dot(a, b, preferred_element_type=F32)


def _dot_nt(a, b):
    return lax.dot_general(a, b, (((1,), (1,)), ((), ())), preferred_element_type=F32)


def _rms_norm(x, g):
    y = x * lax.rsqrt(jnp.mean(x * x, axis=-1, keepdims=True) + RMS_EPS)
    return y * g


def _iota(shape, dim):
    return lax.broadcasted_iota(jnp.int32, shape, dim)


def _inproj_kernel(*refs, add_expert_output):
    if add_expert_output:
        x_ref, y_ref, *refs = refs
        *refs, h_ref = refs
        x = x_ref[...] + y_ref[...].astype(F32)
        h_ref[...] = x
    else:
        x_ref, *refs = refs
        x = x_ref[...]
    (g_ref, wq_ref, wv_ref, wg_ref, wk_ref, wn_ref, kaux_ref,
     q_ref, vs_ref, vw_ref, gate_ref, ks_ref, kw_ref, cmp_ref, pool_ref, conv_ref) = refs
    xn = _rms_norm(x, g_ref[...]).astype(BF16)
    sub_tiles = x.shape[0] // Q_TILE
    q_t = _dot_nt(wq_ref[...], xn)
    for g in range(NSA_KV_HEADS):
        for j in range(sub_tiles):
            for r in range(NSA_GROUP):
                head = g * NSA_GROUP + r
                q_ref[0, g, j, :, r * Q_TILE:(r + 1) * Q_TILE] = q_t[
                    head * HEAD_DIM:(head + 1) * HEAD_DIM, j * Q_TILE:(j + 1) * Q_TILE].astype(BF16)
    v_t = _dot_nt(wv_ref[...], xn)
    k = _dot(xn, wk_ref[...])
    for branch, (v_ref, k_ref) in enumerate(((vs_ref, ks_ref), (vw_ref, kw_ref))):
        for g in range(NSA_KV_HEADS):
            slab = branch * NSA_KV_HEADS + g
            for j in range(sub_tiles):
                v_ref[0, g, j] = v_t[slab * HEAD_DIM:(slab + 1) * HEAD_DIM,
                                     j * KEY_CHUNK:(j + 1) * KEY_CHUNK].astype(BF16)
            k_ref[0, g] = (k[:, slab * LANES:(slab + 1) * LANES] + kaux_ref[...]).astype(BF16)
    gate_ref[0] = _dot_nt(wg_ref[...], xn)
    col = 0
    for ref in (cmp_ref, pool_ref, conv_ref):
        width = ref.shape[-1]
        ref[...] = _dot(xn, wn_ref[:, col:col + width]).reshape(ref.shape)
        col += width


def _inproj(h, y, g, wq_t, wv_t, wg_t, wk, wn, kaux, batch):
    n, d = h.shape
    t = n // batch
    tm = TOKEN_TILE
    steps = t // tm
    sub = tm // Q_TILE
    cw = d // 4
    gq = NSA_GROUP * Q_TILE
    full = lambda a: pl.BlockSpec(a.shape, lambda i: (0,) * a.ndim)
    row = lambda width: pl.BlockSpec((tm, width), lambda i: (i, 0))
    tiles = lambda rows, cols: pl.BlockSpec((1, NSA_KV_HEADS, sub, rows, cols),
                                            lambda i: (i // steps, 0, i % steps, 0, 0))
    keys = pl.BlockSpec((1, NSA_KV_HEADS, tm, LANES), lambda i: (i // steps, 0, i % steps, 0))
    sds = jax.ShapeDtypeStruct
    v_shape = sds((batch, NSA_KV_HEADS, t // KEY_CHUNK, HEAD_DIM, KEY_CHUNK), BF16)
    k_shape = sds((batch, NSA_KV_HEADS, t, LANES), BF16)
    add = y is not None
    outs = pl.pallas_call(
        functools.partial(_inproj_kernel, add_expert_output=add),
        grid=(n // tm,),
        in_specs=[row(d)] * (2 if add else 1)
        + [full(g), full(wq_t), full(wv_t), full(wg_t), full(wk), full(wn), full(kaux)],
        out_specs=[tiles(HEAD_DIM, gq), tiles(HEAD_DIM, KEY_CHUNK), tiles(HEAD_DIM, KEY_CHUNK),
                   pl.BlockSpec((1, GATE_PAD, tm), lambda i: (i // steps, 0, i % steps)),
                   keys, keys,
                   pl.BlockSpec((1, tm, 2 * NSA_KV_HEADS * HEAD_DIM), lambda i: (i // steps, i % steps, 0)),
                   row(cw), row(3 * cw)] + ([row(d)] if add else []),
        out_shape=[sds((batch, NSA_KV_HEADS, t // Q_TILE, HEAD_DIM, gq), BF16), v_shape, v_shape,
                   sds((batch, GATE_PAD, t), F32), k_shape, k_shape,
                   sds((batch, t, 2 * NSA_KV_HEADS * HEAD_DIM), F32), sds((n, cw), F32), sds((n, 3 * cw), F32)]
        + ([sds((n, d), F32)] if add else []),
        compiler_params=_params("parallel"),
        name="inproj",
    )(*((h, y) if add else (h,)), g, wq_t, wv_t, wg_t, wk, wn, kaux)
    return (outs[-1], *outs[:-1]) if add else (h, *outs)


def _gelu_tanh(x):
    return 0.5 * x * (1.0 + jnp.tanh(0.7978845608028654 * (x + 0.044715 * x * x * x)))


def _compress_kernel(src_ref, pe_ref, w1_ref, w1bd_ref, w2k_ref, w2vt_ref, caux_ref, kc_ref, vct_ref):
    kind = pl.program_id(1)
    ncp = kc_ref.shape[2]
    hidden = w1_ref.shape[2]
    pieces = [src_ref[0, pl.ds(l, ncp, stride=CMP_STRIDE), :].astype(BF16) for l in range(CMP_STRIDE)]
    chunk = jnp.concatenate(pieces, axis=1)
    first = _dot(chunk, w1bd_ref[0, 0])
    second = _dot(chunk, w1bd_ref[0, 1])
    bias = _dot(pe_ref[0], w1_ref[0])[0:1, :]
    hid = first + pltpu.roll(second, ncp - 1, 0) + jnp.concatenate([bias] * NSA_KV_HEADS, axis=1)
    row = _iota((ncp, 1), 0)
    act = jnp.where(row < ncp - 1, _gelu_tanh(hid), 0.0).astype(BF16)
    for g in range(NSA_KV_HEADS):
        act_g = act[:, g * hidden:(g + 1) * hidden]

        @pl.when(kind == 0)
        def _():
            kc_ref[0, g] = (_dot(act_g, w2k_ref[...]) + caux_ref[...]).astype(BF16)

        @pl.when(kind == 1)
        def _():
            vct_ref[0, g * HEAD_DIM:(g + 1) * HEAD_DIM, :] = _dot_nt(w2vt_ref[...], act_g).astype(BF16)


def _compress(src, pe, w1, w1bd, w2k, w2vt, caux):
    b, t, _ = src.shape
    ncp = t // CMP_STRIDE
    gd = NSA_KV_HEADS * HEAD_DIM
    full = lambda a: pl.BlockSpec(a.shape, lambda bi, k: (0,) * a.ndim)
    per_kind = lambda a: pl.BlockSpec((1,) + a.shape[1:], lambda bi, k: (k,) + (0,) * (a.ndim - 1))
    return pl.pallas_call(
        _compress_kernel,
        grid=(b, 2),
        in_specs=[pl.BlockSpec((1, t, gd), lambda bi, k: (bi, 0, k)),
                  per_kind(pe), per_kind(w1), per_kind(w1bd), full(w2k), full(w2vt), full(caux)],
        out_specs=[pl.BlockSpec((1, NSA_KV_HEADS, ncp, LANES), lambda bi, k: (bi, 0, 0, 0)),
                   pl.BlockSpec((1, gd, ncp), lambda bi, k: (bi, 0, 0))],
        out_shape=[jax.ShapeDtypeStruct((b, NSA_KV_HEADS, ncp, LANES), BF16),
                   jax.ShapeDtypeStruct((b, gd, ncp), BF16)],
        compiler_params=_params("parallel", "arbitrary"),
        name="compress",
    )(src, pe, w1, w1bd, w2k, w2vt, caux)


SOFTMAX_FLOOR = -1e29
TAKEN = -3e38
ATTN_BATCH = 5
BLOCKS_PER_CHUNK = KEY_CHUNK // SLC_BLOCK
AUX_COLS = 16
AUX_SLOPE = BLOCKS_PER_CHUNK
ONES_ROWS = 16
CMP_ROWS_STEP = 128
CMP_TILES = 2
CMP_AUX_SPLIT = 128


def _slope(g, r):
    if isinstance(g, int):
        return jnp.float32(ALIBI_SLOPES[g * NSA_GROUP + r])
    s = jnp.float32(ALIBI_SLOPES[r])
    for gi in range(1, NSA_KV_HEADS):
        s = jnp.where(g == gi, jnp.float32(ALIBI_SLOPES[gi * NSA_GROUP + r]), s)
    return s


def _nsa_cmp_kernel(qt_ref, kc_ref, vct_ref, poolt_ref, pair_ref, earlier_ref, oct_ref, selt_ref, list_ref,
                    *, n_sel):
    first_tile = pl.program_id(1) * CMP_TILES
    qt = selt_ref.shape[4]
    ncp = kc_ref.shape[2]
    n_slc = poolt_ref.shape[0]
    gq = qt_ref.shape[4]
    units = [(g, j) for j in range(CMP_TILES) for g in range(NSA_KV_HEADS)]
    times = [(first_tile + j) * qt + _iota((1, qt), 1) for j in range(CMP_TILES)]
    aux_row = _iota((AUX_COLS, gq), 0)
    col_head = _iota((1, gq), 1) // qt
    pad_rows = jnp.zeros((kc_ref.shape[3] - qt_ref.shape[3] - AUX_COLS, gq), BF16)

    def weights(g, j):
        slope_cols = jnp.zeros((1, gq), F32)
        for r in range(NSA_GROUP):
            slope_cols = jnp.where(col_head == r, _slope(g, r), slope_cols)
        aux = jnp.where(aux_row == 0, slope_cols * (CMP_STRIDE * CMP_AUX_SPLIT),
                        jnp.where(aux_row == 1, slope_cols * CMP_STRIDE, 0.0))
        return jnp.concatenate([qt_ref[0, g, j], aux.astype(BF16), pad_rows], axis=0)

    def importance(g, j, nr, nb):
        t = times[j]
        w = weights(g, j)
        half = gq // 2
        scores = [_dot(kc_ref[0, g, :nr, :], w[:, c:c + half]) for c in range(0, gq, half)]
        edge = min(nr, 2 * CMP_ROWS_STEP)
        cmp_end = ((nr - edge) + _iota((edge, 1), 0)) * CMP_STRIDE + (CMP_LEN - 1)
        visible = cmp_end <= t
        vct = vct_ref[0, g * HEAD_DIM:(g + 1) * HEAD_DIM, :nr]
        psum = jnp.zeros((nr, qt), F32)
        for r in range(NSA_GROUP):
            cols = slice(r * qt, (r + 1) * qt)
            sr = scores[r * qt // half][:, (r * qt) % half:(r * qt) % half + qt]
            tail = jnp.where(visible, sr[nr - edge:], NEG_INF)
            sr = tail if edge == nr else jnp.concatenate([sr[:nr - edge], tail], axis=0)
            m = jnp.maximum(jnp.max(sr, axis=0, keepdims=True), SOFTMAX_FLOOR)
            e = jnp.exp(sr - m)
            l = jnp.sum(e, axis=0, keepdims=True)
            inv = jnp.where(l > 0.0, 1.0 / l, 0.0)
            oct_ref[0, g, j, :, cols] = (_dot(vct, e.astype(BF16)) * inv).astype(BF16)
            psum = psum + e * inv
        return _dot(poolt_ref[:nb, :nr], psum.astype(BF16))

    def visible_prefix(nr):
        nb = min(n_slc, nr * CMP_STRIDE // SLC_BLOCK)
        imp = jnp.concatenate([importance(g, j, nr, nb) for g, j in units], axis=1)
        blk = _iota((nb, 1), 0)
        cur = jnp.concatenate([times[j] // SLC_BLOCK for _, j in units], axis=1)
        forced = (blk == 0) | (blk == cur) | (blk == cur - 1)
        score = jnp.where(forced, TAKEN, jnp.where(blk <= cur, imp, NEG_INF))
        n_forced = 1 + jnp.where(cur >= 1, 1, 0) + jnp.where(cur >= 2, 1, 0)
        blk_f = blk.astype(F32)

        def take_one(score, active):
            m = jnp.max(score, axis=0, keepdims=True)
            first = jnp.min(jnp.where(score == m, blk_f, F32(1e9)), axis=0, keepdims=True)
            hit = (blk_f == first) if active is None else ((blk_f == first) & active)
            return jnp.where(hit, TAKEN, score)

        common_rounds = max(n_sel - 3, 0)
        for _ in range(common_rounds):
            score = take_one(score, None)

        def early_rounds(score):
            for k in range(common_rounds, n_sel - 1):
                score = take_one(score, n_sel - n_forced > k)
            return score

        score = lax.cond(first_tile * qt < 2 * SLC_BLOCK, early_rounds, lambda sc: sc, score)
        for u, (g, j) in enumerate(units):
            sel_g = score[:, u * qt:(u + 1) * qt] == TAKEN
            selt_ref[0, g, j, :nb, :] = jnp.where(sel_g, 0.0, NEG_INF)
            if nb < n_slc:
                selt_ref[0, g, j, nb:, :] = jnp.full((n_slc - nb, qt), NEG_INF, F32)
            count = _dot_nt(jnp.ones((8, qt), BF16), jnp.where(sel_g, 1.0, 0.0).astype(BF16))
            used = jnp.where(count > 0.0, 1.0, 0.0).astype(BF16)
            chunk = _iota((8, LANES), 1)
            flagged = jnp.where((_dot(used, pair_ref[:nb, :]) > 0.0) & (chunk < first_tile + j), 1.0, 0.0).astype(BF16)
            place = _dot(flagged, earlier_ref[...])
            lands = (_iota((LANES, LANES), 0).astype(F32) == place[0:1, :]) & (flagged[0:1, :] > 0)
            listed = _dot_nt(chunk.astype(BF16), jnp.where(lands, 1.0, 0.0).astype(BF16))
            n_listed = _dot(flagged, jnp.ones((LANES, LANES), BF16))
            list_ref[0, g, j] = jnp.where(_iota((8, LANES), 0) < 4, listed, n_listed).astype(jnp.int32)

    step = min(CMP_ROWS_STEP, ncp)
    rows_needed = jnp.minimum(((first_tile + CMP_TILES) * qt - CMP_LEN) // CMP_STRIDE + 1, ncp)
    n_steps = (rows_needed + step - 1) // step
    for k in range(ncp // step):
        pl.when(n_steps == k + 1)(functools.partial(visible_prefix, (k + 1) * step))


def _nsa_cmp(q_t, kc, vct, poolt_m, pair_m, earlier_m, n_sel):
    b, ng, n_tiles, qrows, qcols = q_t.shape
    ncp = kc.shape[2]
    n_slc = poolt_m.shape[0]
    const = lambda shape: pl.BlockSpec(shape, lambda bi, i: (0,) * len(shape))
    tile5 = lambda rows, cols: pl.BlockSpec((1, ng, CMP_TILES, rows, cols), lambda bi, i: (bi, 0, i, 0, 0))
    return pl.pallas_call(
        functools.partial(_nsa_cmp_kernel, n_sel=n_sel),
        grid=(b, n_tiles // CMP_TILES),
        in_specs=[
            tile5(qrows, qcols),
            pl.BlockSpec((1, ng, ncp, kc.shape[3]), lambda bi, i: (bi, 0, 0, 0)),
            pl.BlockSpec((1, ng * HEAD_DIM, ncp), lambda bi, i: (bi, 0, 0)),
            const(poolt_m.shape), const(pair_m.shape), const(earlier_m.shape),
        ],
        out_specs=[tile5(HEAD_DIM, qcols), tile5(n_slc, Q_TILE), tile5(8, LANES)],
        out_shape=[
            jax.ShapeDtypeStruct((b, ng, n_tiles, HEAD_DIM, qcols), BF16),
            jax.ShapeDtypeStruct((b, ng, n_tiles, n_slc, Q_TILE), F32),
            jax.ShapeDtypeStruct((b, ng, n_tiles, 8, LANES), jnp.int32),
        ],
        compiler_params=_params("parallel", "parallel"),
        name="nsa_compressed",
    )(q_t, kc, vct, poolt_m, pair_m, earlier_m)


def _nsa_slc_kernel(lists_ref, counts_ref, qt_ref, ks_ref, vst_ref, kw_ref, vwt_ref, selt_ref, oct_ref, gt_ref,
                    out_ref, m_ref, l_ref, acc_ref, ow_ref, sa_ref, sb_ref, sw_ref):
    bi = pl.program_id(0)
    g = pl.program_id(1)
    i = pl.program_id(2)
    n_tiles = pl.num_programs(2)
    qt = out_ref.shape[1]
    q_rows = qt_ref[0, 0, 0]
    gq = q_rows.shape[1]
    start = i * qt
    lane_f = _iota((1, qt), 1).astype(F32)
    key_in_chunk = _iota((KEY_CHUNK, qt), 0)
    query_in_tile = _iota((KEY_CHUNK, qt), 1)

    aux_row = _iota((AUX_COLS, gq), 0)
    col_head = _iota((1, gq), 1) // qt
    slope_cols = jnp.zeros((1, gq), F32)
    for r in range(NSA_GROUP):
        slope_cols = jnp.where(col_head == r, _slope(g, r), slope_cols)
    aux_base = jnp.where(aux_row == AUX_SLOPE, slope_cols, 0.0)
    pad_rows = jnp.zeros((ks_ref.shape[3] - q_rows.shape[0] - AUX_COLS, gq), BF16)

    def scores_of(slots):
        return [_dot(k, jnp.concatenate([q_rows, aux.astype(BF16), pad_rows], axis=0))
                for k, aux, _, _, _ in slots]

    def values_of(slots):
        values = jnp.concatenate([v for _, _, v, _, _ in slots], axis=1)
        return jnp.concatenate([values, jnp.ones((ONES_ROWS, values.shape[1]), BF16)], axis=0)

    def softmax_step(slots, scores, v_cat, r, m_old):
        cols = slice(r * qt, (r + 1) * qt)
        srs, tops = [], []
        for j, (_, _, _, shift, mask) in enumerate(slots):
            sr = scores[j, :, cols] if hasattr(scores, "at") else scores[j][:, cols]
            sr = sr if mask is None else jnp.where(mask, sr, NEG_INF)
            srs.append(sr)
            tops.append(jnp.max(sr, axis=0, keepdims=True) + shift[r])
        m_new = functools.reduce(jnp.maximum, tops, m_old)
        ps = [jnp.exp((sr - (m_new - slot[3][r])).astype(BF16)) for slot, sr in zip(slots, srs)]
        weighted = _dot(v_cat, jnp.concatenate(ps, axis=0))
        return m_new, weighted[HEAD_DIM:HEAD_DIM + 1], weighted[:HEAD_DIM]

    def shifts(dist0, ok):
        rows = [-_slope(g, r) * (dist0 + lane_f) for r in range(NSA_GROUP)]
        return rows if ok is None else [jnp.where(ok, row, NEG_INF) for row in rows]

    tile_id = (bi * NSA_KV_HEADS + g) * n_tiles + i
    n_listed = counts_ref[tile_id]

    def selected_slot(c, ok, mask):
        at = pl.multiple_of(c * KEY_CHUNK, KEY_CHUNK)
        bias = selt_ref[0, 0, 0, pl.ds(c * BLOCKS_PER_CHUNK, BLOCKS_PER_CHUNK), :]
        aux = aux_base
        for blk in range(BLOCKS_PER_CHUNK):
            aux = jnp.where(aux_row == blk, jnp.concatenate([bias[blk:blk + 1]] * NSA_GROUP, axis=1), aux)
        return (ks_ref[0, 0, pl.ds(at, KEY_CHUNK), :], aux, vst_ref[0, 0, c],
                shifts((start - c * KEY_CHUNK).astype(F32), ok), mask)

    def listed_slot(idx):
        ok = idx < n_listed
        c = jnp.where(ok, lists_ref[tile_id * LANES + jnp.minimum(idx, jnp.maximum(n_listed - 1, 0))], 0)
        return selected_slot(c, ok, None)

    floor = jnp.full((1, qt), SOFTMAX_FLOOR, F32)

    n_back = WINDOW // KEY_CHUNK
    slots = []
    for j in range(n_back + 1):
        cs = start - WINDOW + j * KEY_CHUNK
        chunk = jnp.maximum(cs, 0) // KEY_CHUNK
        at = pl.multiple_of(chunk * KEY_CHUNK, KEY_CHUNK)
        mask = (query_in_tile < key_in_chunk) if j == 0 else (
            (key_in_chunk <= query_in_tile) if j == n_back else None)
        slots.append((kw_ref[0, 0, pl.ds(at, KEY_CHUNK), :], aux_base, vwt_ref[0, 0, chunk],
                      shifts(F32(WINDOW - j * KEY_CHUNK), cs >= 0), mask))
    window_slots = slots
    first_slots = [selected_slot(i, None, key_in_chunk <= query_in_tile)] + [
        listed_slot(j) for j in range(ATTN_BATCH - 1)]

    def listed_batch(n):
        return [listed_slot(ATTN_BATCH - 1 + n * ATTN_BATCH + j) for j in range(ATTN_BATCH)]

    def issue(slots, dst_ref):
        for j, s in enumerate(scores_of(slots)):
            dst_ref[j] = s

    issue(window_slots, sw_ref)
    issue(first_slots, sb_ref)
    v_cat = values_of(window_slots)
    for r in range(NSA_GROUP):
        _, total, weighted = softmax_step(window_slots, sw_ref, v_cat, r, floor)
        ow_ref[:, r * qt:(r + 1) * qt] = weighted * jnp.where(total > 0.0, 1.0 / total, 0.0)

    def consume(slots, src_ref):
        v_cat = values_of(slots)
        for r in range(NSA_GROUP):
            cols = slice(r * qt, (r + 1) * qt)
            m_old = m_ref[r]
            m_new, total, weighted = softmax_step(slots, src_ref, v_cat, r, m_old)
            alpha = jnp.exp(m_old - m_new)
            l_ref[r] = alpha * l_ref[r] + total
            acc_ref[:, cols] = alpha * acc_ref[:, cols] + weighted
            m_ref[r] = m_new

    issue(listed_batch(0), sa_ref)
    v_cat = values_of(first_slots)
    for r in range(NSA_GROUP):
        m_ref[r], l_ref[r], acc_ref[:, r * qt:(r + 1) * qt] = softmax_step(first_slots, sb_ref, v_cat, r, floor)

    def batch_pair(it, carry):
        first, second, third = (listed_batch(2 * it + n) for n in range(3))
        issue(second, sb_ref)
        consume(first, sa_ref)
        issue(third, sa_ref)
        consume(second, sb_ref)
        return carry

    n_rest = jnp.maximum(n_listed - (ATTN_BATCH - 1), 0)
    n_batches = (n_rest + ATTN_BATCH - 1) // ATTN_BATCH
    lax.fori_loop(0, n_batches // 2, batch_pair, 0)

    @pl.when(n_batches % 2 == 1)
    def _():
        consume(listed_batch(n_batches - 1), sa_ref)

    def finalize(r):
        l = l_ref[r]
        return acc_ref[:, r * qt:(r + 1) * qt] * jnp.where(l > 0.0, 1.0 / l, 0.0)

    outs = []
    for r in range(NSA_GROUP):
        cols = slice(r * qt, (r + 1) * qt)
        col = (g * NSA_GROUP + r) * NSA_BRANCHES
        gate = lambda br: jax.nn.sigmoid(gt_ref[0, pl.ds(col + br, 1), :])
        outs.append(gate(0) * oct_ref[0, 0, 0, :, cols].astype(F32) + gate(1) * finalize(r)
                    + gate(2) * ow_ref[:, cols])
    out_ref[0] = jnp.concatenate(outs, axis=0).T.astype(out_ref.dtype)


def _nsa_slc(lists, counts, q_t, ks, vst, kw, vwt, selt, oct, gates_t):
    b, _, t, kw_cols = ks.shape
    _, _, n_tiles, qrows, qcols = q_t.shape
    n_slc = selt.shape[3]
    n_chunks = vst.shape[2]
    gw = NSA_GROUP * HEAD_DIM
    once = dict(pipeline_mode=pl.Buffered(1))
    k_spec = pl.BlockSpec((1, 1, t, kw_cols), lambda bi, g, i, ls, ns: (bi, g, 0, 0), **once)
    vt_spec = pl.BlockSpec((1, 1, n_chunks, HEAD_DIM, KEY_CHUNK), lambda bi, g, i, ls, ns: (bi, g, 0, 0, 0), **once)
    tile5 = lambda rows, cols: pl.BlockSpec((1, 1, 1, rows, cols), lambda bi, g, i, ls, ns: (bi, g, i, 0, 0))
    grid_spec = pltpu.PrefetchScalarGridSpec(
        num_scalar_prefetch=2,
        grid=(b, NSA_KV_HEADS, n_tiles),
        in_specs=[
            tile5(qrows, qcols),
            k_spec, vt_spec, k_spec, vt_spec,
            tile5(n_slc, Q_TILE),
            tile5(HEAD_DIM, qcols),
            pl.BlockSpec((1, GATE_PAD, Q_TILE), lambda bi, g, i, ls, ns: (bi, 0, i)),
        ],
        out_specs=pl.BlockSpec((1, Q_TILE, gw), lambda bi, g, i, ls, ns: (bi, i, g)),
        scratch_shapes=[pltpu.VMEM((NSA_GROUP, 1, Q_TILE), F32), pltpu.VMEM((NSA_GROUP, 1, Q_TILE), F32),
                        pltpu.VMEM((HEAD_DIM, qcols), F32), pltpu.VMEM((HEAD_DIM, qcols), F32),
                        pltpu.VMEM((ATTN_BATCH, KEY_CHUNK, qcols), F32),
                        pltpu.VMEM((ATTN_BATCH, KEY_CHUNK, qcols), F32),
                        pltpu.VMEM((WINDOW // KEY_CHUNK + 1, KEY_CHUNK, qcols), F32)],
    )
    return pl.pallas_call(
        _nsa_slc_kernel,
        grid_spec=grid_spec,
        out_shape=jax.ShapeDtypeStruct((b, t, NSA_HEADS * HEAD_DIM), BF16),
        compiler_params=_params("parallel", "parallel", "parallel"),
        name="nsa_selected_window",
    )(lists, counts, q_t, ks, vst, kw, vwt, selt, oct, gates_t)


POOL_HALO = 16
CONV_HALO = 8


def _merge_kernel(h_ref, g_ref, nsa_ref, pool_ref, pool_halo_ref, conv_ref, conv_halo_ref,
                  wmg_ref, wnsa_ref, pool_bd_ref, pool_scale_ref, wpool_ref, convw_ref, wconv_ref, wo_ref,
                  g2_ref, wr_ref, br_ref, before_ref,
                  out_ref, xn_ref, comb_ref, count_ref, pool_ext, conv_ext, *, seq_len):
    i = pl.program_id(0)
    tm, d = h_ref.shape
    cw = pool_ref.shape[1]
    pos0 = (i * tm) % seq_len
    keep_halo = jnp.where(pos0 == 0, 0.0, 1.0)
    pos = pos0 + _iota((tm, 1), 0)

    u = pool_ref[...]
    pool_ext[0:POOL_HALO, :] = pool_halo_ref[...] * keep_halo
    pool_ext[POOL_HALO:, :] = u
    lane_group = _iota((1, cw), 1) // (cw // len(POOL_WINDOWS))
    total = u
    mean = jnp.zeros_like(u)
    done = 1
    for gi, win in enumerate(POOL_WINDOWS):
        for k in range(done, win):
            total = total + pool_ext[POOL_HALO - k:POOL_HALO - k + tm, :]
        done = win
        cnt = jnp.minimum(pos + 1, win).astype(F32)
        mean = jnp.where(lane_group == gi, total / cnt, mean)
    pooled = (mean - u).astype(BF16)
    mixed = _dot(pooled, pool_bd_ref[...]) * pool_scale_ref[...]
    y_pool = _dot(mixed.astype(BF16), wpool_ref[...])

    ch = conv_ref[:, 0:cw]
    cb = conv_ref[:, cw:2 * cw]
    cc = conv_ref[:, 2 * cw:3 * cw]
    conv_ext[0:CONV_HALO, :] = conv_halo_ref[:, 0:cw] * conv_halo_ref[:, 2 * cw:3 * cw] * keep_halo
    conv_ext[CONV_HALO:, :] = cc * ch
    y = jnp.zeros((tm, cw), F32)
    for k in range(CONV_K):
        off = CONV_HALO - (CONV_K - 1) + k
        y = y + convw_ref[k:k + 1, :] * conv_ext[off:off + tm, :]
    y_conv = _dot((cb * y).astype(BF16), wconv_ref[...])

    y_nsa = _dot(nsa_ref[...], wnsa_ref[...])

    h = h_ref[...]
    xn = _rms_norm(h, g_ref[...]).astype(BF16)
    merged = jnp.zeros((tm, d), F32)
    for br, y_br in enumerate((y_nsa, y_pool, y_conv)):
        mg = jax.nn.sigmoid(_dot(xn, wmg_ref[:, br * d:(br + 1) * d]))
        merged = merged + mg * y_br
    h_new = h + _dot(merged.astype(BF16), wo_ref[...])
    out_ref[...] = h_new
    _route_tile(h_new, g2_ref, wr_ref, br_ref, before_ref, xn_ref, comb_ref, count_ref)


def _merge(h, g, nsa, pool_u, conv, wmg, wnsa, pool_bd, pool_scale, wpool, convw, wconv, wo,
           g2, wr, br, before, seq_len):
    n, d = h.shape
    tm = TOKEN_TILE
    cw = pool_u.shape[1]
    row = lambda width: pl.BlockSpec((tm, width), lambda i: (i, 0))
    full = lambda a: pl.BlockSpec(a.shape, lambda i: (0,) * a.ndim)
    halo = lambda rows, width: pl.BlockSpec(
        (rows, width), lambda i: (jnp.maximum(i * (tm // rows) - 1, 0), 0))
    return pl.pallas_call(
        functools.partial(_merge_kernel, seq_len=seq_len),
        grid=(n // tm,),
        in_specs=[row(d), full(g), row(nsa.shape[1]), row(cw), halo(POOL_HALO, cw),
                  row(conv.shape[1]), halo(CONV_HALO, conv.shape[1]),
                  full(wmg), full(wnsa), full(pool_bd), full(pool_scale), full(wpool), full(convw),
                  full(wconv), full(wo), full(g2), full(wr), full(br), full(before)],
        out_specs=[row(d), row(d + ROUTER_PAD), row(ROUTER_PAD), pl.BlockSpec((8, ROUTER_PAD), lambda i: (0, 0))],
        out_shape=[jax.ShapeDtypeStruct((n, d), F32), jax.ShapeDtypeStruct((n, d + ROUTER_PAD), BF16),
                   jax.ShapeDtypeStruct((n, ROUTER_PAD), F32), jax.ShapeDtypeStruct((8, ROUTER_PAD), F32)],
        scratch_shapes=[pltpu.VMEM((tm + POOL_HALO, cw), F32), pltpu.VMEM((tm + CONV_HALO, cw), F32)],
        compiler_params=_params("arbitrary"),
        name="merge",
    )(h, g, nsa, pool_u, pool_u, conv, conv, wmg, wnsa, pool_bd, pool_scale, wpool, convw, wconv, wo,
      g2, wr, br, before)


def _route(logits):
    lane = _iota(logits.shape, 1)
    lane_f = lane.astype(F32)
    big = F32(1e9)
    is_group = lane < N_EXPERT_GROUPS
    gl = jnp.where(is_group, logits, NEG_INF)
    g_max = jnp.max(gl, axis=1, keepdims=True)
    g_sel = jnp.min(jnp.where(gl == g_max, lane_f, big), axis=1, keepdims=True)
    g_prob = 1.0 / jnp.sum(jnp.where(is_group, jnp.exp(gl - g_max), 0.0), axis=1, keepdims=True)
    lo = N_EXPERT_GROUPS + EXPERTS_PER_GROUP * g_sel
    in_group = (lane_f >= lo) & (lane_f < lo + EXPERTS_PER_GROUP)
    el = jnp.where(in_group, logits, NEG_INF)
    v1 = jnp.max(el, axis=1, keepdims=True)
    i1 = jnp.min(jnp.where((el == v1) & in_group, lane_f, big), axis=1, keepdims=True)
    el2 = jnp.where(lane_f == i1, NEG_INF, el)
    rest = in_group & (lane_f != i1)
    v2 = jnp.max(el2, axis=1, keepdims=True)
    i2 = jnp.min(jnp.where((el2 == v2) & rest, lane_f, big), axis=1, keepdims=True)
    e2 = jnp.exp(v2 - v1)
    w1 = g_prob / (1.0 + e2)
    w2 = g_prob * e2 / (1.0 + e2)
    return jnp.where(lane_f == i1, w1, 0.0) + jnp.where(lane_f == i2, w2, 0.0), g_sel


GROUP_LANE = N_EXPERT_GROUPS + N_EXPERTS


RANK_LANE = GROUP_LANE + 1


def _route_tile(h, g_ref, wr_ref, br_ref, before_ref, xn_ref, comb_ref, count_ref):
    @pl.when(pl.program_id(0) == 0)
    def _():
        count_ref[...] = jnp.zeros(count_ref.shape, F32)

    xn = _rms_norm(h, g_ref[...])
    xn_hi = xn.astype(BF16)
    xn_lo = (xn - xn_hi.astype(F32)).astype(BF16)
    logits = (_dot(xn_hi, wr_ref[0]) + (_dot(xn_hi, wr_ref[1]) + _dot(xn_lo, wr_ref[0]))) + br_ref[...]
    comb, g_sel = _route(logits)
    lane = _iota(comb.shape, 1)
    chose = jnp.where(lane.astype(F32) == g_sel, 1.0, 0.0)
    earlier = _dot(before_ref[...], chose.astype(BF16)) + count_ref[0:1, :]
    rank = jnp.sum(chose * earlier, axis=1, keepdims=True)
    count_ref[0:1, :] = count_ref[0:1, :] + jnp.sum(chose, axis=0, keepdims=True)
    comb_ref[...] = jnp.where(lane == GROUP_LANE, g_sel, jnp.where(lane == RANK_LANE, rank, comb))
    first = N_EXPERT_GROUPS + EXPERTS_PER_GROUP * g_sel
    local = jnp.zeros(comb.shape, F32)
    for e in range(EXPERTS_PER_GROUP):
        c_e = jnp.sum(jnp.where(lane.astype(F32) == first + e, comb, 0.0), axis=1, keepdims=True)
        local = jnp.where((lane == e) | (lane == EXPERTS_PER_GROUP + e), c_e, local)
    local_hi = local.astype(BF16)
    d = xn_hi.shape[1]
    xn_ref[:, :d] = xn_hi
    xn_ref[:, d:] = jnp.where(lane < EXPERTS_PER_GROUP, local_hi, (local - local_hi.astype(F32)).astype(BF16))


def _experts_kernel(tile_group_ref, n_active_ref, x_ref, w1_ref, w3_ref, w2_ref, out_ref, acc_ref):
    i = pl.program_id(0)

    @pl.when(i < n_active_ref[0])
    def _():
        d = out_ref.shape[1]
        x = x_ref[:, :d]
        comb = x_ref[:, d:].astype(F32)
        lane = _iota(comb.shape, 1)
        for e in range(EXPERTS_PER_GROUP):
            c_e = jnp.sum(jnp.where((lane == e) | (lane == EXPERTS_PER_GROUP + e), comb, 0.0),
                          axis=1, keepdims=True)
            a = (jax.nn.silu(_dot(x, w1_ref[0, 0, e].astype(BF16)))
                 * _dot(x, w3_ref[0, 0, e].astype(BF16))) * c_e
            y = _dot(a.astype(BF16), w2_ref[0, 0, e].astype(BF16))
            if e == 0:
                acc_ref[...] = y
            else:
                acc_ref[...] += y
        out_ref[...] = acc_ref[...].astype(out_ref.dtype)

    @pl.when(i >= n_active_ref[0])
    def _():
        out_ref[...] = jnp.zeros(out_ref.shape, out_ref.dtype)


def _experts(tile_group, n_active, x_sorted, w1, w3, w2, layer):
    ns = x_sorted.shape[0]
    d = w1.shape[3]
    tm = MOE_SORT_TILE
    group_w = lambda w: pl.BlockSpec((1, 1) + w.shape[2:], lambda i, tg, na: (layer, tg[i], 0, 0, 0),
                                     pipeline_mode=pl.Buffered(1))
    grid_spec = pltpu.PrefetchScalarGridSpec(
        num_scalar_prefetch=2,
        grid=(ns // tm,),
        in_specs=[
            pl.BlockSpec((tm, x_sorted.shape[1]), lambda i, tg, na: (i, 0)),
            group_w(w1), group_w(w3), group_w(w2),
        ],
        out_specs=pl.BlockSpec((tm, d), lambda i, tg, na: (i, 0)),
        scratch_shapes=[pltpu.VMEM((tm, d), F32)],
    )
    return pl.pallas_call(
        _experts_kernel,
        grid_spec=grid_spec,
        out_shape=jax.ShapeDtypeStruct((ns, d), BF16),
        compiler_params=_params("arbitrary"),
        name="experts",
    )(tile_group, n_active, x_sorted, w1, w3, w2)


def _residual_kernel(h_ref, y_ref, gf_ref, out_ref, *, final_norm):
    out = h_ref[...] + y_ref[...].astype(F32)
    out_ref[...] = _rms_norm(out, gf_ref[...]) if final_norm else out


def _residual(h, y, gf, final_norm):
    n, d = h.shape
    tm = MOE_TILE
    row = pl.BlockSpec((tm, d), lambda i: (i, 0))
    return pl.pallas_call(
        functools.partial(_residual_kernel, final_norm=final_norm),
        grid=(n // tm,),
        in_specs=[row, row, pl.BlockSpec(gf.shape, lambda i: (0, 0))],
        out_specs=row,
        out_shape=jax.ShapeDtypeStruct((n, d), F32),
        compiler_params=_params("parallel"),
        name="residual",
    )(h, y, gf)


def _group_sort_plan(group_id, rank, counts, tile):
    n = group_id.shape[0]
    n_slots = n + N_EXPERT_GROUPS * tile
    padded = (counts + tile - 1) // tile * tile
    ends = jnp.cumsum(padded)
    slot = (ends - padded)[group_id] + rank
    source = jnp.zeros((n_slots,), jnp.int32).at[slot].set(jnp.arange(n, dtype=jnp.int32))
    tile_start = jnp.arange(n_slots // tile, dtype=jnp.int32) * tile
    tile_group = jnp.minimum(jnp.searchsorted(ends, tile_start, side="right"), N_EXPERT_GROUPS - 1)
    return slot, source, tile_group.astype(jnp.int32), (ends[-1:] // tile).astype(jnp.int32)


def _moe(xn, comb, counts, w1, w3, w2, layer):
    as_int = lambda a: a.astype(jnp.int32)
    slot, source, tile_group, n_active = _group_sort_plan(
        as_int(comb[:, GROUP_LANE]), as_int(comb[:, RANK_LANE]), as_int(counts[0, :N_EXPERT_GROUPS]),
        MOE_SORT_TILE)
    grouped = lambda w: w.reshape((w.shape[0], N_EXPERT_GROUPS, EXPERTS_PER_GROUP) + w.shape[2:])
    rows = lambda a, idx: jnp.take(a, idx, axis=0, mode="clip")
    y_sorted = _experts(tile_group, n_active, rows(xn, source), grouped(w1), grouped(w3), grouped(w2), layer)
    return rows(y_sorted, slot)


def _selection_constants(seq_len):
    ncp = seq_len // CMP_STRIDE
    n_slc = seq_len // SLC_BLOCK
    ratio = SLC_BLOCK // CMP_STRIDE
    lead = CMP_LEN // CMP_STRIDE - 1
    c = np.arange(ncp)[:, None]
    j = np.arange(n_slc)[None, :]
    pool_m = ((c >= ratio * j - lead) & (c < ratio * j + ratio)).astype(np.float32)
    blocks_per_chunk = KEY_CHUNK // SLC_BLOCK
    n_chunks = seq_len // KEY_CHUNK
    pair_m = np.zeros((n_slc, LANES * ((n_chunks + LANES - 1) // LANES)), np.float32)
    pair_m[np.arange(n_slc), np.arange(n_slc) // blocks_per_chunk] = 1.0
    earlier_m = np.triu(np.ones((LANES, LANES), np.float32), 1)
    key_aux = np.zeros((TOKEN_TILE, LANES), np.float32)
    in_chunk = np.arange(TOKEN_TILE) % KEY_CHUNK
    key_aux[np.arange(TOKEN_TILE), HEAD_DIM + in_chunk // SLC_BLOCK] = 1.0
    key_aux[:, HEAD_DIM + AUX_SLOPE] = in_chunk
    cmp_aux = np.zeros((ncp, LANES), np.float32)
    cmp_aux[:, HEAD_DIM] = np.arange(ncp) // CMP_AUX_SPLIT
    cmp_aux[:, HEAD_DIM + 1] = np.arange(ncp) % CMP_AUX_SPLIT
    as_bf16 = lambda a: jnp.asarray(a, BF16)
    return as_bf16(pool_m.T), as_bf16(pair_m), as_bf16(earlier_m), jnp.asarray(key_aux), jnp.asarray(cmp_aux)


def kernel(x, norm1_g, w_in, cmp_pe, cmp_w1, cmp_w2, w_nsa_proj, pool_w, pool_scale, w_pool_proj, conv_w,
           w_conv_proj, w_o, norm2_g, router_group_w, router_group_b, router_expert_w, router_expert_b,
           expert_w1, expert_w3, expert_w2, final_norm_g):
    b, t, d = x.shape
    n = b * t
    depth = w_in.shape[0]
    dq = NSA_HEADS * HEAD_DIM
    dkv = 6 * NSA_KV_HEADS * HEAD_DIM
    dgate = NSA_HEADS * NSA_BRANCHES
    cw = d // 4
    assert t % TOKEN_TILE == 0 and n % MOE_TILE == 0 and t % (Q_TILE * CMP_TILES) == 0
    n_slc = t // SLC_BLOCK
    n_sel = min(SLC_TOPN, n_slc)
    n_chunks16 = t // CMP_STRIDE
    kvw = NSA_KV_HEADS * HEAD_DIM
    poolt_m, pair_m, earlier_m, key_aux, cmp_aux = _selection_constants(t)
    assert Q_TILE == KEY_CHUNK and AUX_SLOPE < AUX_COLS and KEY_CHUNK <= 256
    assert n_chunks16 <= 256 * CMP_AUX_SPLIT and n_chunks16 % min(CMP_ROWS_STEP, n_chunks16) == 0
    assert pair_m.shape[1] == LANES

    before = jnp.asarray(np.tril(np.ones((TOKEN_TILE, TOKEN_TILE), np.float32), -1), BF16)
    h, y = x.reshape(n, d), None
    for l in range(depth):
        wl = w_in[l]
        o_gate = dq + dkv
        o_pool = o_gate + dgate
        o_merge = o_pool + cw + 3 * cw
        kv_cols = lambda kind: wl[:, dq + kind * kvw:dq + (kind + 1) * kvw]
        wq_t = (wl[:, :dq] * (HEAD_DIM ** -0.5)).T.astype(BF16)
        wv_t = jnp.concatenate([kv_cols(3), kv_cols(5)], axis=1).T.astype(BF16)
        wg_t = jnp.pad(wl[:, o_gate:o_pool], ((0, 0), (0, GATE_PAD - dgate))).T.astype(BF16)
        no_aux = jnp.zeros((d, LANES - HEAD_DIM), F32)
        wk = jnp.concatenate([piece for kind in (2, 4) for gi in range(NSA_KV_HEADS)
                              for piece in (kv_cols(kind)[:, gi * HEAD_DIM:(gi + 1) * HEAD_DIM], no_aux)],
                             axis=1).astype(BF16)
        wn = jnp.concatenate([kv_cols(0), kv_cols(1), wl[:, o_pool:o_merge]], axis=1).astype(BF16)
        wmg = wl[:, o_merge:].astype(BF16)
        pool_bd = jax.scipy.linalg.block_diag(*[pool_w[l, gi] for gi in range(pool_w.shape[1])]).astype(BF16)
        convw = jnp.pad(conv_w[l], ((0, 8 - CONV_K), (0, 0)))
        wr = jnp.pad(jnp.concatenate([router_group_w[l], router_expert_w[l]], axis=1),
                     ((0, 0), (0, ROUTER_PAD - N_EXPERT_GROUPS - N_EXPERTS)))
        wr_hi = wr.astype(BF16)
        wr = jnp.stack([wr_hi, (wr - wr_hi.astype(F32)).astype(BF16)])
        br = jnp.pad(jnp.concatenate([router_group_b[l], router_expert_b[l]]),
                     (0, ROUTER_PAD - N_EXPERT_GROUPS - N_EXPERTS))[None, :]
        pe = jnp.broadcast_to(cmp_pe[l].reshape(2, 1, CMP_LEN * HEAD_DIM), (2, 8, CMP_LEN * HEAD_DIM)).astype(BF16)
        halves = CMP_LEN // CMP_STRIDE
        w1_bd = jnp.einsum("khldc,gq->khlgdqc",
                           cmp_w1[l].reshape(2, halves, CMP_STRIDE, HEAD_DIM, CMP_HIDDEN),
                           jnp.eye(NSA_KV_HEADS, dtype=F32))
        w1_bd = w1_bd.reshape(2, halves, CMP_STRIDE * kvw, NSA_KV_HEADS * CMP_HIDDEN).astype(BF16)
        w2_k = jnp.pad(cmp_w2[l, 0], ((0, 0), (0, LANES - HEAD_DIM))).astype(BF16)
        w2_vt = cmp_w2[l, 1].T.astype(BF16)

        h, q_t, vst, vwt, gates_t, ks, kw, cmp_src, pool_u, conv = _inproj(
            h, y, norm1_g[l][None, :], wq_t, wv_t, wg_t, wk, wn, key_aux, b)
        kc_aux, vc_t = _compress(cmp_src, pe, cmp_w1[l].astype(BF16), w1_bd, w2_k, w2_vt, cmp_aux)
        oc_t, sel_t, listed = _nsa_cmp(q_t, kc_aux, vc_t, poolt_m, pair_m, earlier_m, n_sel)
        nsa = _nsa_slc(listed[:, :, :, 0, :].reshape(-1), listed[:, :, :, 4, 0].reshape(-1),
                       q_t, ks, vst, kw, vwt, sel_t, oc_t, gates_t)
        h, xn, comb, counts = _merge(
            h, norm1_g[l][None, :], nsa.reshape(n, dq), pool_u, conv, wmg,
            w_nsa_proj[l].astype(BF16), pool_bd, pool_scale[l][None, :], w_pool_proj[l].astype(BF16),
            convw, w_conv_proj[l].astype(BF16), w_o[l].astype(BF16), norm2_g[l][None, :], wr, br, before, t)
        y = _moe(xn, comb, counts, expert_w1, expert_w3, expert_w2, l)
    return _residual(h, y, final_norm_g[None, :], final_norm=True).reshape(b, t, d)
```
